```python
import math
import jax, jax.numpy as jnp
from jax import lax
import numpy as np

D_MODEL = 1024
BATCH = 16
SEQ = 256
DEPTH = 2
DEC_BATCH = 2
DEC_SEQ = 4096
PAST_LEN = 512

GRID_W = 64
N_LAYERS_A = (DEPTH + 1) // 2
N_LAYERS_B = DEPTH // 2
M_HEADS = 4
M_DK = D_MODEL // 2 // M_HEADS
M_DV = D_MODEL // M_HEADS
M_HK = M_HEADS * M_DK
M_PROJ = 2 * M_HK + 2 * D_MODEL + 4 * M_HEADS
M_CHUNK = 64
NA_HEADS = 16
NA_DH = D_MODEL // NA_HEADS
NA_ROWS = 8
NA_COLS = 16
ATTN_BLOCK = 128
D_FF = 2816
N_EXPERTS = 8
TOP_K = 2
DEEPNORM_ALPHA = (2.0 * DEPTH) ** 0.25
DEEPNORM_BETA = (8.0 * DEPTH) ** -0.25
LN_EPS = 1e-5

kernel_name = "hybrid_mlstm_natten_diffusion_step"

F32 = jnp.float32


def layer_norm(x, g, b):
    xf = x.astype(F32)
    mu = jnp.mean(xf, axis=-1, keepdims=True)
    var = jnp.mean(jnp.square(xf - mu), axis=-1, keepdims=True)
    return ((xf - mu) * lax.rsqrt(var + LN_EPS) * g + b).astype(x.dtype)


def modulate(x, shift, scale):
    return x * (1 + scale) + shift


def mlstm_chunked(q, k, v, logi, logf, C0, n0, m0):
    B, S, H, _ = q.shape
    nc = S // M_CHUNK

    def chunks(a):
        a = a.reshape((B, nc, M_CHUNK, H) + a.shape[3:])
        return jnp.moveaxis(a, (1, 3), (0, 2))

    causal = jnp.tril(jnp.ones((M_CHUNK, M_CHUNK), dtype=bool))

    def step(carry, inp):
        C, n, m = carry
        qc, kc, vc, lic, lfc = inp
        b = jnp.cumsum(lfc, axis=-1)
        dmat = b[..., :, None] - b[..., None, :] + lic[..., None, :]
        dmat = jnp.where(causal, dmat, -jnp.inf)
        inter = b + m[..., None]
        m_t = jnp.maximum(inter, jnp.max(dmat, axis=-1))
        w = jnp.exp(dmat - m_t[..., None])
        a = jnp.exp(inter - m_t)
        s = jnp.einsum('bhtk,bhsk->bhts', qc, kc) * w
        num = a[..., None] * jnp.einsum('bhtk,bhkv->bhtv', qc, C) + jnp.einsum('bhts,bhsv->bhtv', s, vc)
        den = a * jnp.einsum('bhtk,bhk->bht', qc, n) + jnp.sum(s, axis=-1)
        h = num / jnp.maximum(jnp.abs(den), jnp.exp(-m_t))[..., None]
        b_last = b[..., -1]
        g = b_last[..., None] - b + lic
        m_new = jnp.maximum(b_last + m, jnp.max(g, axis=-1))
        ws = jnp.exp(g - m_new[..., None])
        a0 = jnp.exp(b_last + m - m_new)
        C_new = a0[..., None, None] * C + jnp.einsum('bhsk,bhsv->bhkv', kc * ws[..., None], vc)
        n_new = a0[..., None] * n + jnp.einsum('bhs,bhsk->bhk', ws, kc)
        return (C_new, n_new, m_new), h

    (C, n, m), h = lax.scan(step, (C0, n0, m0),
                            (chunks(q), chunks(k), chunks(v), chunks(logi), chunks(logf)))
    h = jnp.moveaxis(h, (0, 2), (1, 3)).reshape(B, S, H, v.shape[-1])
    return h, (C, n, m)


def mlstm_mixer(u, C0, n0, m0, w_in, b_gates, norm_g, w_out):
    B, S, _ = u.shape
    proj = u @ w_in
    q, k, v, o, g = jnp.split(proj, [M_HK, 2 * M_HK, 2 * M_HK + D_MODEL, 2 * M_HK + 2 * D_MODEL], axis=-1)
    q = q.reshape(B, S, M_HEADS, M_DK).astype(F32) * (M_DK ** -0.5)
    k = k.reshape(B, S, M_HEADS, M_DK).astype(F32)
    v = v.reshape(B, S, M_HEADS, M_DV).astype(F32)
    g = g.astype(F32).reshape(B, S, 2, 2, M_HEADS) + b_gates.astype(F32)
    logi = g[:, :, :, 0]
    logf = jax.nn.log_sigmoid(g[:, :, :, 1])
    C0, n0, m0 = C0.astype(F32), n0.astype(F32), m0.astype(F32)
    h_f, (Cf, nf, mf) = mlstm_chunked(q, k, v, logi[:, :, 0], logf[:, :, 0], C0[:, 0], n0[:, 0], m0[:, 0])
    flip = lambda a: jnp.flip(a, axis=1)
    h_b, (Cb, nb, mb) = mlstm_chunked(flip(q), flip(k), flip(v), flip(logi[:, :, 1]), flip(logf[:, :, 1]),
                                      C0[:, 1], n0[:, 1], m0[:, 1])
    h = h_f + flip(h_b)
    mu = jnp.mean(h, axis=-1, keepdims=True)
    var = jnp.mean(jnp.square(h - mu), axis=-1, keepdims=True)
    hn = ((h - mu) * lax.rsqrt(var + LN_EPS)).reshape(B, S, D_MODEL) * norm_g
    y = (hn * jax.nn.sigmoid(o.astype(F32))).astype(u.dtype) @ w_out
    return y, (jnp.stack([Cf, Cb], axis=1), jnp.stack([nf, nb], axis=1), jnp.stack([mf, mb], axis=1))


def na_project(u, w_qkv):
    B, S, _ = u.shape
    q, k, v = jnp.split(u @ w_qkv, 3, axis=-1)
    shp = (B, S, NA_HEADS, NA_DH)
    return q.reshape(shp), k.reshape(shp), v.reshape(shp)


def context_attention(q, k, v):
    B, P, H, dh = q.shape
    nb = P // ATTN_BLOCK
    qb = jnp.moveaxis(q.reshape(B, nb, ATTN_BLOCK, H, dh), 1, 0)

    def block(qq):
        s = jnp.einsum('bqhd,bkhd->bhqk', qq, k).astype(F32) * (dh ** -0.5)
        p = jax.nn.softmax(s, axis=-1).astype(v.dtype)
        return jnp.einsum('bhqk,bkhd->bqhd', p, v)

    o = lax.map(block, qb)
    return jnp.moveaxis(o, 0, 1).reshape(B, P, H, dh)


def neighbourhood_attention(q, k, v, k_ctx, v_ctx, rpb):
    B, S, H, dh = q.shape
    rows = S // GRID_W
    kr = min(NA_ROWS, rows)
    r = jnp.arange(rows)
    r_start = jnp.clip(r - kr // 2, 0, rows - kr)
    row_idx = r_start[:, None] + jnp.arange(kr)[None, :]
    col = jnp.arange(GRID_W)
    c_start = jnp.clip(col - NA_COLS // 2, 0, GRID_W - NA_COLS)
    col_in = (col[None, :] >= c_start[:, None]) & (col[None, :] < c_start[:, None] + NA_COLS)
    dr = row_idx - r[:, None] + NA_ROWS - 1
    dc = jnp.clip(col[None, :] - col[:, None] + NA_COLS - 1, 0, 2 * NA_COLS - 2)
    bias = rpb[:, dr[:, :, None, None], dc[None, None, :, :]]
    bias = jnp.transpose(bias, (1, 0, 3, 2, 4)).astype(F32)
    qg = q.reshape(B, rows, GRID_W, H, dh)
    kg = jnp.take(k.reshape(B, rows, GRID_W, H, dh), row_idx, axis=1)
    vg = jnp.take(v.reshape(B, rows, GRID_W, H, dh), row_idx, axis=1)
    scale = dh ** -0.5
    s_loc = jnp.einsum('brqhd,brkwhd->brhqkw', qg, kg).astype(F32) * scale + bias[None]
    s_loc = jnp.where(col_in[:, None, :], s_loc, -jnp.inf)
    s_ctx = jnp.einsum('brqhd,bphd->brhqp', qg, k_ctx).astype(F32) * scale
    n_loc = kr * GRID_W
    s = jnp.concatenate([s_loc.reshape(B, rows, H, GRID_W, n_loc), s_ctx], axis=-1)
    p = jax.nn.softmax(s, axis=-1).astype(v.dtype)
    p_loc = p[..., :n_loc].reshape(B, rows, H, GRID_W, kr, GRID_W)
    p_ctx = p[..., n_loc:]
    o = (jnp.einsum('brhqkw,brkwhd->brqhd', p_loc, vg)
         + jnp.einsum('brhqp,bphd->brqhd', p_ctx, v_ctx))
    return o.reshape(B, S, H, dh)


def swiglu(u, w_in, w_out):
    g, up = jnp.split(u @ w_in, 2, axis=-1)
    return (jax.nn.silu(g) * up) @ w_out


def moe_swiglu(u, w_router, w_in, w_out):
    shp = u.shape
    t = u.reshape(-1, shp[-1])
    logits = (t @ w_router).astype(F32)
    top_val, top_idx = lax.top_k(logits, TOP_K)
    weights = jax.nn.softmax(top_val, axis=-1)
    combine = jnp.sum(jax.nn.one_hot(top_idx, N_EXPERTS, dtype=F32) * weights[..., None], axis=1)
    out = jnp.zeros(t.shape, F32)
    for e in range(N_EXPERTS):
        out = out + combine[:, e:e + 1] * swiglu(t, w_in[e], w_out[e]).astype(F32)
    return out.astype(u.dtype).reshape(shp)


def setup_inputs(seed: int = 0) -> dict:
    key = jax.random.key(seed)
    ks = jax.random.split(key, 26)

    def nrm(k, shape, scale=1.0):
        return jax.random.normal(k, shape, F32) * scale

    inv = D_MODEL ** -0.5
    gate_noise = nrm(ks[12], (N_LAYERS_A, 2, 2, M_HEADS))
    gate_offset = jnp.array([0.0, 3.0], F32)[None, None, :, None]
    gate_spread = jnp.array([0.1, 0.5], F32)[None, None, :, None]
    return {
        "x_prompt": nrm(ks[0], (BATCH, SEQ, D_MODEL)),
        "x_sample": nrm(ks[1], (DEC_BATCH, DEC_SEQ, D_MODEL)),
        "state_mlstm_C": nrm(ks[2], (DEC_BATCH, N_LAYERS_A, 2, M_HEADS, M_DK, M_DV)),
        "state_mlstm_n": nrm(ks[3], (DEC_BATCH, N_LAYERS_A, 2, M_HEADS, M_DK)),
        "state_mlstm_m": nrm(ks[4], (DEC_BATCH, N_LAYERS_A, 2, M_HEADS)),
        "cache_na_k": nrm(ks[5], (DEC_BATCH, N_LAYERS_B, PAST_LEN, NA_HEADS, NA_DH)),
        "cache_na_v": nrm(ks[6], (DEC_BATCH, N_LAYERS_B, PAST_LEN, NA_HEADS, NA_DH)),
        "c": nrm(ks[7], (DEC_BATCH, D_MODEL)),
        "c_ctx": nrm(ks[8], (D_MODEL,)),
        "ada_w": nrm(ks[9], (DEPTH, D_MODEL, 6 * D_MODEL), 0.5 * inv),
        "ada_b": nrm(ks[10], (DEPTH, 6 * D_MODEL), 0.02),
        "ln_g": 1.0 + nrm(ks[11], (DEPTH, 2, D_MODEL), 0.05),
        "ln_b": nrm(ks[13], (DEPTH, 2, D_MODEL), 0.02),
        "mlstm_w_in": nrm(ks[14], (N_LAYERS_A, D_MODEL, M_PROJ), inv),
        "mlstm_b_gates": gate_offset + gate_spread * gate_noise,
        "mlstm_norm_g": 1.0 + nrm(ks[15], (N_LAYERS_A, D_MODEL), 0.05),
        "mlstm_w_out": nrm(ks[16], (N_LAYERS_A, D_MODEL, D_MODEL), inv * DEEPNORM_BETA),
        "na_w_qkv": nrm(ks[17], (N_LAYERS_B, D_MODEL, 3 * D_MODEL), inv),
        "na_rpb": nrm(ks[18], (N_LAYERS_B, NA_HEADS, 2 * NA_ROWS - 1, 2 * NA_COLS - 1), 0.1),
        "na_w_out": nrm(ks[19], (N_LAYERS_B, D_MODEL, D_MODEL), inv * DEEPNORM_BETA),
        "ffn_w_in": nrm(ks[20], (N_LAYERS_A, D_MODEL, 2 * D_FF), inv),
        "ffn_w_out": nrm(ks[21], (N_LAYERS_A, D_FF, D_MODEL), (D_FF ** -0.5) * DEEPNORM_BETA),
        "moe_w_router": nrm(ks[22], (N_LAYERS_B, D_MODEL, N_EXPERTS), inv),
        "moe_w_in": nrm(ks[23], (N_LAYERS_B, N_EXPERTS, D_MODEL, 2 * D_FF), inv),
        "moe_w_out": nrm(ks[24], (N_LAYERS_B, N_EXPERTS, D_FF, D_MODEL), (D_FF ** -0.5) * DEEPNORM_BETA),
    }


def reference(x_prompt, x_sample, state_mlstm_C, state_mlstm_n, state_mlstm_m, cache_na_k, cache_na_v,
              c, c_ctx, ada_w, ada_b, ln_g, ln_b, mlstm_w_in, mlstm_b_gates, mlstm_norm_g, mlstm_w_out,
              na_w_qkv, na_rpb, na_w_out, ffn_w_in, ffn_w_out, moe_w_router, moe_w_in, moe_w_out):
    xp, xs = x_prompt, x_sample
    bp = xp.shape[0]
    new_C, new_n, new_m, new_k, new_v = [], [], [], [], []
    for i in range(DEPTH):
        j = i // 2
        mod_p = jnp.split(jax.nn.silu(c_ctx) @ ada_w[i] + ada_b[i], 6, axis=-1)
        mod_s = jnp.split((jax.nn.silu(c) @ ada_w[i] + ada_b[i])[:, None, :], 6, axis=-1)
        up = modulate(xp, mod_p[0], mod_p[1])
        us = modulate(xs, mod_s[0], mod_s[1])
        if i % 2 == 0:
            zC = jnp.zeros((bp, 2, M_HEADS, M_DK, M_DV), F32)
            zn = jnp.zeros((bp, 2, M_HEADS, M_DK), F32)
            zm = jnp.zeros((bp, 2, M_HEADS), F32)
            yp, (Cp, n_p, m_p) = mlstm_mixer(up, zC, zn, zm, mlstm_w_in[j], mlstm_b_gates[j],
                                             mlstm_norm_g[j], mlstm_w_out[j])
            ys, _ = mlstm_mixer(us, state_mlstm_C[:, j], state_mlstm_n[:, j], state_mlstm_m[:, j],
                                mlstm_w_in[j], mlstm_b_gates[j], mlstm_norm_g[j], mlstm_w_out[j])
            new_C.append(Cp)
            new_n.append(n_p)
            new_m.append(m_p)
        else:
            qp, kp, vp = na_project(up, na_w_qkv[j])
            yp = context_attention(qp, kp, vp).reshape(up.shape) @ na_w_out[j]
            qs, ks_, vs = na_project(us, na_w_qkv[j])
            ys = neighbourhood_attention(qs, ks_, vs, cache_na_k[:, j], cache_na_v[:, j],
                                         na_rpb[j]).reshape(us.shape) @ na_w_out[j]
            new_k.append(kp)
            new_v.append(vp)
        xp = layer_norm(DEEPNORM_ALPHA * xp + mod_p[2] * yp, ln_g[i, 0], ln_b[i, 0])
        xs = layer_norm(DEEPNORM_ALPHA * xs + mod_s[2] * ys, ln_g[i, 0], ln_b[i, 0])
        up = modulate(xp, mod_p[3], mod_p[4])
        us = modulate(xs, mod_s[3], mod_s[4])
        if i % 2 == 0:
            fp = swiglu(up, ffn_w_in[j], ffn_w_out[j])
            fs = swiglu(us, ffn_w_in[j], ffn_w_out[j])
        else:
            fp = moe_swiglu(up, moe_w_router[j], moe_w_in[j], moe_w_out[j])
            fs = moe_swiglu(us, moe_w_router[j], moe_w_in[j], moe_w_out[j])
        xp = layer_norm(DEEPNORM_ALPHA * xp + mod_p[5] * fp, ln_g[i, 1], ln_b[i, 1])
        xs = layer_norm(DEEPNORM_ALPHA * xs + mod_s[5] * fs, ln_g[i, 1], ln_b[i, 1])
    return (xp, xs, jnp.stack(new_C, axis=1), jnp.stack(new_n, axis=1), jnp.stack(new_m, axis=1),
            jnp.stack(new_k, axis=1), jnp.stack(new_v, axis=1))
```

```python
import functools
import math

import jax
import jax.numpy as jnp
from jax import lax
from jax.experimental import pallas as pl
from jax.experimental.pallas import tpu as pltpu

F32 = jnp.float32
BF16 = jnp.bfloat16

D_MODEL = 1024
DEPTH = 2
GRID_W = 64
M_HEADS = 4
M_DK = 128
M_DV = 256
M_HK = M_HEADS * M_DK
NA_HEADS = 16
NA_DH = 64
NA_ROWS = 8
NA_COLS = 16
D_FF = 2816
N_EXPERTS = 8
DEEPNORM_ALPHA = (2.0 * DEPTH) ** 0.25
LN_EPS = 1e-5
NEG = -1e30

VMEM_LIMIT_BYTES = 56 * 1024 * 1024

MLSTM_CHUNK = 256
TOKEN_TILE = 512
FFN_TOKEN_TILE = 1024
FFN_F_TILE = 256
NA_QROWS = 4
NA_HALO = 4


def _cparams(*sem):
    return pltpu.CompilerParams(dimension_semantics=sem, vmem_limit_bytes=VMEM_LIMIT_BYTES)


def _dot(a, b):
    return jnp.dot(a, b, preferred_element_type=F32)


def _dot_nt(a, b):
    return lax.dot_general(a, b, (((1,), (1,)), ((), ())), preferred_element_type=F32)


def _onehot(mask):
    return jnp.where(mask, 1.0, 0.0).astype(BF16)


def _split3(x):
    hi = x.astype(BF16)
    r = x - hi.astype(F32)
    mid = r.astype(BF16)
    lo = (r - mid.astype(F32)).astype(BF16)
    return hi, mid, lo


def _layer_norm(z, g, b):
    mu = jnp.mean(z, axis=-1, keepdims=True)
    zc = z - mu
    var = jnp.mean(zc * zc, axis=-1, keepdims=True)
    return zc * lax.rsqrt(var + LN_EPS) * g + b


def _log_sigmoid(x):
    return jnp.minimum(x, 0.0) - jnp.log(1.0 + jnp.exp(-jnp.abs(x)))


def _seg_of_tile(i, tile, n_prompt_tok, sample_len):
    tok = i * tile
    return jnp.where(tok < n_prompt_tok, 0, 1 + (tok - n_prompt_tok) // sample_len)


def _adaln_kernel(c_ref, w_ref, b_ref, o_ref):
    c = c_ref[...]
    a = (c * jax.nn.sigmoid(c)).astype(BF16)
    o_ref[0] = _dot(a, w_ref[0].astype(BF16)) + b_ref[0]


def adaln(cvec, ada_w, ada_b):
    depth, d, n = ada_w.shape
    tn = 1536
    return pl.pallas_call(
        _adaln_kernel,
        grid=(depth, n // tn),
        in_specs=[
            pl.BlockSpec((8, d), lambda l, j: (0, 0)),
            pl.BlockSpec((1, d, tn), lambda l, j: (l, 0, j)),
            pl.BlockSpec((1, 1, tn), lambda l, j: (l, 0, j)),
        ],
        out_specs=pl.BlockSpec((1, 8, tn), lambda l, j: (l, 0, j)),
        out_shape=jax.ShapeDtypeStruct((depth, 8, n), F32),
        compiler_params=_cparams("arbitrary", "arbitrary"),
        name="adaln",
    )(cvec, ada_w, ada_b.reshape(depth, 1, n))


def _inproj_kernel(x_ref, mod_ref, w_ref, wg_ref, wgt_ref, bg_ref, bgt_ref,
                   qkv_ref, o_ref, g_ref, gt_ref):
    x = x_ref[...]
    u = (x * (1.0 + mod_ref[0, 1:2, :]) + mod_ref[0, 0:1, :]).astype(BF16)
    p = _dot(u, w_ref[...])
    nqkv = qkv_ref.shape[1]
    qkv_ref[...] = p[:, :nqkv].astype(BF16)
    o_ref[...] = p[:, nqkv:]
    g = _dot(u, wg_ref[...]) + bg_ref[...]
    col = lax.broadcasted_iota(jnp.int32, g.shape, 1)
    g_ref[...] = jnp.where(((col >> 2) & 1) == 1, _log_sigmoid(g), g)
    gt = _dot_nt(wgt_ref[...], u) + bgt_ref[...]
    row = lax.broadcasted_iota(jnp.int32, gt.shape, 0)
    gt_ref[...] = jnp.where(((row >> 2) & 1) == 1, _log_sigmoid(gt), gt)


def mlstm_inproj(x, mod, w_main, w_gate, b_gate, n_prompt_tok, sample_len):
    t, d = x.shape
    n_main = w_main.shape[1]
    n_qkv = 2 * M_HK + D_MODEL
    ng = w_gate.shape[1]
    tt = TOKEN_TILE
    seg = functools.partial(_seg_of_tile, tile=tt, n_prompt_tok=n_prompt_tok, sample_len=sample_len)
    return pl.pallas_call(
        _inproj_kernel,
        grid=(t // tt,),
        in_specs=[
            pl.BlockSpec((tt, d), lambda i: (i, 0)),
            pl.BlockSpec((1, 6, d), lambda i: (seg(i), 0, 0)),
            pl.BlockSpec((d, n_main), lambda i: (0, 0)),
            pl.BlockSpec((d, ng), lambda i: (0, 0)),
            pl.BlockSpec((ng, d), lambda i: (0, 0)),
            pl.BlockSpec((1, ng), lambda i: (0, 0)),
            pl.BlockSpec((ng, 1), lambda i: (0, 0)),
        ],
        out_specs=[
            pl.BlockSpec((tt, n_qkv), lambda i: (i, 0)),
            pl.BlockSpec((tt, n_main - n_qkv), lambda i: (i, 0)),
            pl.BlockSpec((tt, ng), lambda i: (i, 0)),
            pl.BlockSpec((ng, tt), lambda i: (0, i)),
        ],
        out_shape=[
            jax.ShapeDtypeStruct((t, n_qkv), BF16),
            jax.ShapeDtypeStruct((t, n_main - n_qkv), F32),
            jax.ShapeDtypeStruct((t, ng), F32),
            jax.ShapeDtypeStruct((ng, t), F32),
        ],
        compiler_params=_cparams("arbitrary"),
        name="mlstm_inproj",
    )(x, mod, w_main, w_gate, w_gate.T, b_gate.reshape(1, ng), b_gate.reshape(ng, 1))


def _mlstm_kernel(*refs, direction, n_prompt_chunks, chunks_per_sample):
    backward = direction == 1
    if backward:
        (q_ref, k_ref, v_ref, g_ref, gt_ref, c0_ref, n0_ref, m0_ref, hprev_ref, o_ref, ng_ref,
         out_ref, cf_ref, nf_ref, mf_ref, c_scr, n_scr, m_scr) = refs
    else:
        (q_ref, k_ref, v_ref, g_ref, gt_ref, c0_ref, n0_ref, m0_ref,
         out_ref, cf_ref, nf_ref, mf_ref, c_scr, n_scr, m_scr) = refs
    step = pl.program_id(0)
    is_prompt = step < n_prompt_chunks
    pos = (step - n_prompt_chunks) % chunks_per_sample
    L = q_ref.shape[0]

    @pl.when(is_prompt)
    def _():
        c_scr[...] = jnp.zeros_like(c_scr)
        n_scr[...] = jnp.zeros_like(n_scr)
        m_scr[...] = jnp.zeros_like(m_scr)

    @pl.when(jnp.logical_and(jnp.logical_not(is_prompt), pos == 0))
    def _():
        c_scr[...] = c0_ref[0]
        n_scr[...] = n0_ref[0]
        m_scr[...] = m0_ref[0]

    row = lax.broadcasted_iota(jnp.int32, (L, L), 0)
    col = lax.broadcasted_iota(jnp.int32, (L, L), 1)
    lower = col <= row
    upper = col >= row
    valid = upper if backward else lower
    tri_col = _onehot(valid)
    tri_row = _onehot(lower if backward else upper)

    g = g_ref[...]
    gt = gt_ref[...]
    ghi, gmid, glo = _split3(g)
    b_cols = _dot(tri_col, ghi) + _dot(tri_col, gmid) + _dot(tri_col, glo)
    thi, tmid, tlo = _split3(gt)
    b_rows = _dot(thi, tri_row) + _dot(tmid, tri_row) + _dot(tlo, tri_row)

    scale = M_DK ** -0.5
    for h in range(M_HEADS):
        ci = direction * 8 + h
        cf = direction * 8 + 4 + h
        li_row = gt[ci:ci + 1, :]
        lf_row = gt[cf:cf + 1, :]
        b_row = b_rows[cf:cf + 1, :]
        b_col = b_cols[:, cf:cf + 1]
        total = jnp.sum(lf_row, axis=1, keepdims=True)
        m_prev = m_scr[h][0:1, 0:1]
        n_prev = n_scr[h]
        c_prev = c_scr[h]

        qh = q_ref[:, h * M_DK:(h + 1) * M_DK]
        kh = k_ref[:, h * M_DK:(h + 1) * M_DK]
        vh = v_ref[:, h * M_DV:(h + 1) * M_DV]

        dmat = jnp.where(valid, b_col + (li_row - b_row), NEG)
        inter = b_col + m_prev
        m_t = jnp.maximum(inter, jnp.max(dmat, axis=1, keepdims=True))
        w = jnp.exp(dmat - m_t)
        a = jnp.exp(inter - m_t)
        s = _dot_nt(qh, kh) * scale * w
        qc = _dot(qh, c_prev.astype(BF16)) * scale
        qn = _dot_nt(qh, n_prev.astype(BF16))[:, 0:1] * scale
        num = a * qc + _dot(s.astype(BF16), vh)
        den = a * qn + jnp.sum(s, axis=1, keepdims=True)
        hh = num / jnp.maximum(jnp.abs(den), jnp.exp(-m_t))

        sl = slice(h * M_DV, (h + 1) * M_DV)
        if backward:
            ht = hprev_ref[:, sl] + hh
            mu = jnp.mean(ht, axis=-1, keepdims=True)
            hc = ht - mu
            var = jnp.mean(hc * hc, axis=-1, keepdims=True)
            hn = hc * lax.rsqrt(var + LN_EPS) * ng_ref[:, sl]
            out_ref[:, sl] = (hn * jax.nn.sigmoid(o_ref[:, sl])).astype(out_ref.dtype)
        else:
            out_ref[:, sl] = hh

        g_row = total - b_row + li_row
        m_new = jnp.maximum(total + m_prev, jnp.max(g_row, axis=1, keepdims=True))
        ws_row = jnp.exp(g_row - m_new)
        a0 = jnp.exp(total + m_prev - m_new)
        kt = kh.astype(F32).T
        c_new = a0 * c_prev + _dot((kt * ws_row).astype(BF16), vh)
        ws8 = jnp.broadcast_to(ws_row, (8, L)).astype(BF16)
        c_scr[h] = c_new
        n_scr[h] = a0 * n_prev + _dot(ws8, kh)
        m_scr[h] = jnp.broadcast_to(m_new, m_scr.shape[1:])

    @pl.when(is_prompt)
    def _():
        cf_ref[0] = c_scr[...]
        nf_ref[0] = n_scr[...]
        mf_ref[0] = m_scr[...]


def mlstm_direction(direction, qkv, g, gt, c0, n0, m0, n_prompt_chunks, chunks_per_sample, n_sample,
                    hprev=None, ogate=None, norm_g=None):
    L = MLSTM_CHUNK
    t = qkv.shape[0]
    n_chunks = t // L
    backward = direction == 1
    npc, cps = n_prompt_chunks, chunks_per_sample

    def chunk_of(s):
        rel = s - npc
        seq = rel // cps
        pos = rel % cps
        blk = npc + seq * cps + ((cps - 1 - pos) if backward else pos)
        return jnp.where(s < npc, s, blk)

    def seq_of(s):
        return jnp.clip((s - npc) // cps, 0, n_sample - 1)

    def prompt_of(s):
        return jnp.minimum(s, npc - 1)

    kq = M_HK // M_HK
    in_specs = [
        pl.BlockSpec((L, M_HK), lambda s: (chunk_of(s), 0)),
        pl.BlockSpec((L, M_HK), lambda s: (chunk_of(s), kq)),
        pl.BlockSpec((L, D_MODEL), lambda s: (chunk_of(s), 1)),
        pl.BlockSpec((L, 16), lambda s: (chunk_of(s), 0)),
        pl.BlockSpec((16, L), lambda s: (0, chunk_of(s))),
        pl.BlockSpec((1, M_HEADS, M_DK, M_DV), lambda s: (seq_of(s), 0, 0, 0)),
        pl.BlockSpec((1, M_HEADS, 8, 128), lambda s: (seq_of(s), 0, 0, 0)),
        pl.BlockSpec((1, M_HEADS, 8, 128), lambda s: (seq_of(s), 0, 0, 0)),
    ]
    args = [qkv, qkv, qkv, g, gt, c0, n0, m0]
    if backward:
        in_specs += [
            pl.BlockSpec((L, D_MODEL), lambda s: (chunk_of(s), 0)),
            pl.BlockSpec((L, D_MODEL), lambda s: (chunk_of(s), 0)),
            pl.BlockSpec((1, D_MODEL), lambda s: (0, 0)),
        ]
        args += [hprev, ogate, norm_g.reshape(1, D_MODEL)]
    out_dtype = BF16 if backward else F32
    return pl.pallas_call(
        functools.partial(_mlstm_kernel, direction=direction, n_prompt_chunks=npc, chunks_per_sample=cps),
        grid=(n_chunks,),
        in_specs=in_specs,
        out_specs=[
            pl.BlockSpec((L, D_MODEL), lambda s: (chunk_of(s), 0)),
            pl.BlockSpec((1, M_HEADS, M_DK, M_DV), lambda s: (prompt_of(s), 0, 0, 0)),
            pl.BlockSpec((1, M_HEADS, 8, 128), lambda s: (prompt_of(s), 0, 0, 0)),
            pl.BlockSpec((1, M_HEADS, 8, 128), lambda s: (prompt_of(s), 0, 0, 0)),
        ],
        out_shape=[
            jax.ShapeDtypeStruct((t, D_MODEL), out_dtype),
            jax.ShapeDtypeStruct((npc, M_HEADS, M_DK, M_DV), F32),
            jax.ShapeDtypeStruct((npc, M_HEADS, 8, 128), F32),
            jax.ShapeDtypeStruct((npc, M_HEADS, 8, 128), F32),
        ],
        scratch_shapes=[
            pltpu.VMEM((M_HEADS, M_DK, M_DV), F32),
            pltpu.VMEM((M_HEADS, 8, 128), F32),
            pltpu.VMEM((M_HEADS, 8, 128), F32),
        ],
        compiler_params=_cparams("arbitrary"),
        name="mlstm_bwd" if backward else "mlstm_fwd",
    )(*args)


def _outproj_kernel(a_ref, x_ref, mod_ref, w_ref, lng_ref, lnb_ref, xo_ref, uo_ref, *, gate_row, next_row):
    y = _dot(a_ref[...], w_ref[...])
    z = DEEPNORM_ALPHA * x_ref[...] + mod_ref[0, gate_row:gate_row + 1, :] * y
    xn = _layer_norm(z, lng_ref[...], lnb_ref[...])
    xo_ref[...] = xn
    uo_ref[...] = (xn * (1.0 + mod_ref[0, next_row + 1:next_row + 2, :])
                   + mod_ref[0, next_row:next_row + 1, :]).astype(BF16)


def mixer_outproj(a, x, mod, w, ln_g, ln_b, n_prompt_tok, sample_len):
    t, d = x.shape
    tt = TOKEN_TILE
    seg = functools.partial(_seg_of_tile, tile=tt, n_prompt_tok=n_prompt_tok, sample_len=sample_len)
    return pl.pallas_call(
        functools.partial(_outproj_kernel, gate_row=2, next_row=3),
        grid=(t // tt,),
        in_specs=[
            pl.BlockSpec((tt, d), lambda i: (i, 0)),
            pl.BlockSpec((tt, d), lambda i: (i, 0)),
            pl.BlockSpec((1, 6, d), lambda i: (seg(i), 0, 0)),
            pl.BlockSpec((d, d), lambda i: (0, 0)),
            pl.BlockSpec((1, d), lambda i: (0, 0)),
            pl.BlockSpec((1, d), lambda i: (0, 0)),
        ],
        out_specs=[pl.BlockSpec((tt, d), lambda i: (i, 0)), pl.BlockSpec((tt, d), lambda i: (i, 0))],
        out_shape=[jax.ShapeDtypeStruct((t, d), F32), jax.ShapeDtypeStruct((t, d), BF16)],
        compiler_params=_cparams("arbitrary"),
        name="mixer_outproj",
    )(a, x, mod, w, ln_g.reshape(1, d), ln_b.reshape(1, d))


def _ffn_kernel(u_ref, x_ref, mod_ref, modn_ref, wg_ref, wu_ref, wo_ref, lng_ref, lnb_ref,
                xo_ref, uo_ref, acc_ref):
    f = pl.program_id(1)

    @pl.when(f == 0)
    def _():
        acc_ref[...] = jnp.zeros_like(acc_ref)

    u = u_ref[...]
    gp = _dot(u, wg_ref[...])
    up = _dot(u, wu_ref[...])
    act = (gp * jax.nn.sigmoid(gp) * up).astype(BF16)
    acc_ref[...] += _dot(act, wo_ref[...])

    @pl.when(f == pl.num_programs(1) - 1)
    def _():
        z = DEEPNORM_ALPHA * x_ref[...] + mod_ref[0, 5:6, :] * acc_ref[...]
        xn = _layer_norm(z, lng_ref[...], lnb_ref[...])
        xo_ref[...] = xn
        uo_ref[...] = (xn * (1.0 + modn_ref[0, 1:2, :]) + modn_ref[0, 0:1, :]).astype(BF16)


def dense_ffn(u, x, mod, mod_next, w_in, w_out, ln_g, ln_b, n_prompt_tok, sample_len):
    t, d = x.shape
    dff = w_out.shape[0]
    tm, tf = FFN_TOKEN_TILE, FFN_F_TILE
    nf = dff // tf
    seg = functools.partial(_seg_of_tile, tile=tm, n_prompt_tok=n_prompt_tok, sample_len=sample_len)
    return pl.pallas_call(
        _ffn_kernel,
        grid=(t // tm, nf),
        in_specs=[
            pl.BlockSpec((tm, d), lambda i, f: (i, 0)),
            pl.BlockSpec((tm, d), lambda i, f: (i, 0)),
            pl.BlockSpec((1, 6, d), lambda i, f: (seg(i), 0, 0)),
            pl.BlockSpec((1, 6, d), lambda i, f: (seg(i), 0, 0)),
            pl.BlockSpec((d, tf), lambda i, f: (0, f)),
            pl.BlockSpec((d, tf), lambda i, f: (0, nf + f)),
            pl.BlockSpec((tf, d), lambda i, f: (f, 0)),
            pl.BlockSpec((1, d), lambda i, f: (0, 0)),
            pl.BlockSpec((1, d), lambda i, f: (0, 0)),
        ],
        out_specs=[pl.BlockSpec((tm, d), lambda i, f: (i, 0)), pl.BlockSpec((tm, d), lambda i, f: (i, 0))],
        out_shape=[jax.ShapeDtypeStruct((t, d), F32), jax.ShapeDtypeStruct((t, d), BF16)],
        scratch_shapes=[pltpu.VMEM((tm, d), F32)],
        compiler_params=_cparams("arbitrary", "arbitrary"),
        name="dense_ffn",
    )(u, x, mod, mod_next, w_in, w_in, w_out, ln_g.reshape(1, d), ln_b.reshape(1, d))


def _modulation_tables(c, c_ctx, ada_w, ada_b):
    n_s = c.shape[0]
    cvec = jnp.concatenate([c_ctx[None, :], c, jnp.zeros((8 - 1 - n_s, c.shape[1]), F32)], axis=0)
    mod = adaln(cvec, ada_w, ada_b)
    return mod[:, :1 + n_s, :].reshape(ada_w.shape[0], 1 + n_s, 6, c.shape[1])


def _layer0(x, mod0, mod1, state_c, state_n, state_m, ln_g, ln_b, w_in, b_gates, norm_g, w_out,
            ffn_w_in, ffn_w_out, n_prompt, prompt_len, n_sample, sample_len):
    n_prompt_tok = n_prompt * prompt_len
    n_main = 2 * M_HK + 2 * D_MODEL
    qkv, ogate, g, gt = mlstm_inproj(x, mod0, w_in[:, :n_main].astype(BF16), w_in[:, n_main:].astype(BF16),
                                     b_gates.reshape(-1), n_prompt_tok, sample_len)
    npc = n_prompt_tok // MLSTM_CHUNK
    cps = sample_len // MLSTM_CHUNK
    rep = lambda a: jnp.broadcast_to(a[..., None, :], a.shape[:-1] + (8, a.shape[-1]))
    n0 = rep(state_n)
    m0 = jnp.broadcast_to(state_m[..., None, None], state_m.shape + (8, 128))
    hf, cf, nf, mf = mlstm_direction(0, qkv, g, gt, state_c[:, 0], n0[:, 0], m0[:, 0], npc, cps, n_sample)
    a, cb, nb, mb = mlstm_direction(1, qkv, g, gt, state_c[:, 1], n0[:, 1], m0[:, 1], npc, cps, n_sample,
                                    hprev=hf, ogate=ogate, norm_g=norm_g)
    x1, u1 = mixer_outproj(a, x, mod0, w_out.astype(BF16), ln_g[0], ln_b[0], n_prompt_tok, sample_len)
    x2, u2 = dense_ffn(u1, x1, mod0, mod1, ffn_w_in.astype(BF16), ffn_w_out.astype(BF16), ln_g[1], ln_b[1],
                       n_prompt_tok, sample_len)
    new_c = jnp.stack([cf, cb], axis=1)
    new_n = jnp.stack([nf[:, :, 0, :], nb[:, :, 0, :]], axis=1)
    new_m = jnp.stack([mf[:, :, 0, 0], mb[:, :, 0, 0]], axis=1)
    return x1, x2, u2, new_c, new_n, new_m


def _qkv_kernel(u_ref, w_ref, qkv_ref, k_ref, v_ref, *, n_prompt_tiles):
    p = _dot(u_ref[...], w_ref[...])
    qkv_ref[...] = p.astype(BF16)
    d = k_ref.shape[1]

    @pl.when(pl.program_id(0) < n_prompt_tiles)
    def _():
        k_ref[...] = p[:, d:2 * d]
        v_ref[...] = p[:, 2 * d:]


def na_qkv(u, w, n_prompt_tok):
    t, d = u.shape
    tt = TOKEN_TILE
    npt = n_prompt_tok // tt
    return pl.pallas_call(
        functools.partial(_qkv_kernel, n_prompt_tiles=npt),
        grid=(t // tt,),
        in_specs=[pl.BlockSpec((tt, d), lambda i: (i, 0)), pl.BlockSpec((d, 3 * d), lambda i: (0, 0))],
        out_specs=[
            pl.BlockSpec((tt, 3 * d), lambda i: (i, 0)),
            pl.BlockSpec((tt, d), lambda i: (jnp.minimum(i, npt - 1), 0)),
            pl.BlockSpec((tt, d), lambda i: (jnp.minimum(i, npt - 1), 0)),
        ],
        out_shape=[
            jax.ShapeDtypeStruct((t, 3 * d), BF16),
            jax.ShapeDtypeStruct((n_prompt_tok, d), F32),
            jax.ShapeDtypeStruct((n_prompt_tok, d), F32),
        ],
        compiler_params=_cparams("arbitrary"),
        name="na_qkv",
    )(u, w)


def _head_pair_masks():
    lane = lax.broadcasted_iota(jnp.int32, (1, 2 * NA_DH), 1)
    return (_onehot(lane < NA_DH), _onehot(lane >= NA_DH))


def _softmax_pv(score_blocks, value_blocks):
    mx = None
    for s in score_blocks:
        bm = jnp.max(s, axis=1, keepdims=True)
        mx = bm if mx is None else jnp.maximum(mx, bm)
    acc, denom = None, None
    for s, v in zip(score_blocks, value_blocks):
        p = jnp.exp(s - mx)
        ps = jnp.sum(p, axis=1, keepdims=True)
        pv = _dot(p.astype(BF16), v)
        acc = pv if acc is None else acc + pv
        denom = ps if denom is None else denom + ps
    return acc / denom


def _ctx_attn_kernel(q_ref, k_ref, v_ref, o_ref):
    scale = NA_DH ** -0.5
    masks = _head_pair_masks()
    for hp in range(NA_HEADS // 2):
        sl = slice(hp * 2 * NA_DH, (hp + 1) * 2 * NA_DH)
        q2, k2, v2 = q_ref[:, sl], k_ref[:, sl], v_ref[:, sl]
        o2 = None
        for msk in masks:
            s = _dot_nt(q2 * msk, k2) * scale
            o = _softmax_pv([s], [v2 * msk])
            o2 = o if o2 is None else o2 + o
        o_ref[:, sl] = o2.astype(o_ref.dtype)


def context_attention(qkv, n_prompt, prompt_len):
    t = qkv.shape[0]
    d = D_MODEL
    return pl.pallas_call(
        _ctx_attn_kernel,
        grid=(n_prompt,),
        in_specs=[
            pl.BlockSpec((prompt_len, d), lambda b: (b, 0)),
            pl.BlockSpec((prompt_len, d), lambda b: (b, 1)),
            pl.BlockSpec((prompt_len, d), lambda b: (b, 2)),
        ],
        out_specs=pl.BlockSpec((prompt_len, d), lambda b: (b, 0)),
        out_shape=jax.ShapeDtypeStruct((t, d), BF16),
        compiler_params=_cparams("arbitrary"),
        name="context_attention",
    )(qkv, qkv, qkv)


def _na_kernel(q_ref, kp_ref, kc_ref, kn_ref, vp_ref, vc_ref, vn_ref, kctx_ref, vctx_ref, bias_ref, buf_ref,
               o_ref, *, grid_rows):
    del buf_ref
    j = pl.program_id(1)
    m = q_ref.shape[0]
    halo = NA_HALO * GRID_W
    nloc = m + 2 * halo
    shift = GRID_W.bit_length() - 1
    qi = lax.broadcasted_iota(jnp.int32, (m, nloc), 0)
    ki = lax.broadcasted_iota(jnp.int32, (m, nloc), 1)
    r = NA_QROWS * j + (qi >> shift)
    rk = NA_QROWS * j - NA_HALO + (ki >> shift)
    r_start = jnp.clip(r - NA_ROWS // 2, 0, grid_rows - NA_ROWS)
    row_ok = jnp.logical_and(rk >= r_start, rk < r_start + NA_ROWS)
    row_mask = jnp.where(row_ok, 0.0, NEG)

    scale = NA_DH ** -0.5
    masks = _head_pair_masks()
    for hp in range(NA_HEADS // 2):
        sl = slice(hp * 2 * NA_DH, (hp + 1) * 2 * NA_DH)
        q2 = q_ref[:, sl]
        kloc = jnp.concatenate([kp_ref[m - halo:, sl], kc_ref[:, sl], kn_ref[:halo, sl]], axis=0)
        vloc = jnp.concatenate([vp_ref[m - halo:, sl], vc_ref[:, sl], vn_ref[:halo, sl]], axis=0)
        kctx = kctx_ref[0][:, sl]
        vctx = vctx_ref[0][:, sl]
        o2 = None
        for half, msk in enumerate(masks):
            qm = q2 * msk
            s_loc = _dot_nt(qm, kloc) * scale + (bias_ref[2 * hp + half] + row_mask)
            s_ctx = _dot_nt(qm, kctx) * scale
            o = _softmax_pv([s_loc, s_ctx], [vloc * msk, vctx * msk])
            o2 = o if o2 is None else o2 + o
        o_ref[:, sl] = o2.astype(o_ref.dtype)


def _na_bias_table(rpb):
    m = NA_QROWS * GRID_W
    nloc = (NA_QROWS + 2 * NA_HALO) * GRID_W
    qi = jnp.arange(m)
    ki = jnp.arange(nloc)
    dr = (ki[None, :] // GRID_W - NA_HALO) - (qi[:, None] // GRID_W) + NA_ROWS - 1
    qc = qi[:, None] % GRID_W
    kc = ki[None, :] % GRID_W
    c_start = jnp.clip(qc - NA_COLS // 2, 0, GRID_W - NA_COLS)
    col_in = (kc >= c_start) & (kc < c_start + NA_COLS)
    dc = jnp.clip(kc - qc + NA_COLS - 1, 0, 2 * NA_COLS - 2)
    bias = rpb[:, jnp.clip(dr, 0, 2 * NA_ROWS - 2), dc]
    return jnp.where(col_in[None], bias, NEG).astype(F32)


def neighbourhood_attention(qkv, attn_buf, k_ctx, v_ctx, rpb, n_prompt_tok, n_sample, sample_len):
    d = D_MODEL
    m = NA_QROWS * GRID_W
    grid_rows = sample_len // GRID_W
    nblk = sample_len // m
    base = n_prompt_tok // m
    nloc = m + 2 * NA_HALO * GRID_W
    bias = _na_bias_table(rpb)

    def blk(col, off):
        return pl.BlockSpec((m, d), lambda b, j: (base + b * nblk + jnp.clip(j + off, 0, nblk - 1), col))

    return pl.pallas_call(
        functools.partial(_na_kernel, grid_rows=grid_rows),
        grid=(n_sample, nblk),
        in_specs=[
            blk(0, 0), blk(1, -1), blk(1, 0), blk(1, 1), blk(2, -1), blk(2, 0), blk(2, 1),
            pl.BlockSpec((1,) + k_ctx.shape[1:], lambda b, j: (b, 0, 0)),
            pl.BlockSpec((1,) + v_ctx.shape[1:], lambda b, j: (b, 0, 0)),
            pl.BlockSpec((NA_HEADS, m, nloc), lambda b, j: (0, 0, 0)),
            pl.BlockSpec(memory_space=pl.ANY),
        ],
        out_specs=pl.BlockSpec((m, d), lambda b, j: (base + b * nblk + j, 0)),
        out_shape=jax.ShapeDtypeStruct(attn_buf.shape, attn_buf.dtype),
        input_output_aliases={10: 0},
        compiler_params=_cparams("arbitrary", "arbitrary"),
        name="neighbourhood_attention",
    )(qkv, qkv, qkv, qkv, qkv, qkv, qkv, k_ctx, v_ctx, bias, attn_buf)


MOE_BLOCK = 256
MOE_ROW_TILE = 512
MOE_F_TILE = 1408


def _router_kernel(u_ref, wt_ref, comb_ref, pos_ref, combc_ref, posc_ref, after_ref, carry_scr):
    i = pl.program_id(0)

    @pl.when(i == 0)
    def _():
        carry_scr[...] = jnp.zeros_like(carry_scr)

    u = u_ref[...]
    w1, w2, w3 = _split3(wt_ref[...])
    lg = _dot_nt(w1, u) + _dot_nt(w2, u) + _dot_nt(w3, u)
    ne, nt = lg.shape
    e = lax.broadcasted_iota(jnp.int32, lg.shape, 0)
    m1 = jnp.max(lg, axis=0, keepdims=True)
    i1 = jnp.min(jnp.where(lg == m1, e, ne), axis=0, keepdims=True)
    sel1 = e == i1
    lg2 = jnp.where(sel1, -jnp.inf, lg)
    m2 = jnp.max(lg2, axis=0, keepdims=True)
    i2 = jnp.min(jnp.where(lg2 == m2, e, ne), axis=0, keepdims=True)
    sel2 = e == i2
    ex = jnp.exp(m2 - m1)
    wa = 1.0 / (1.0 + ex)
    wb = ex / (1.0 + ex)
    comb = jnp.where(sel1, wa, jnp.where(sel2, wb, 0.0))
    assign = jnp.logical_or(sel1, sel2)
    a = jnp.where(assign, 1.0, 0.0)
    row = lax.broadcasted_iota(jnp.int32, (nt, nt), 0)
    col = lax.broadcasted_iota(jnp.int32, (nt, nt), 1)
    tri = _onehot(row < col)
    carry = carry_scr[:, 0:1]
    rank = _dot(a.astype(BF16), tri) + carry
    pos = jnp.where(assign, rank, -1.0)
    comb_ref[...] = comb
    pos_ref[...] = pos
    combc_ref[...] = comb.T
    posc_ref[...] = pos.T
    carry_new = carry + jnp.sum(a, axis=1, keepdims=True)
    carry_scr[...] = jnp.broadcast_to(carry_new, carry_scr.shape)
    after_ref[0] = jnp.broadcast_to(carry_new, carry_scr.shape)


def moe_router(u, w_router):
    t, d = u.shape
    ne = w_router.shape[1]
    tb = MOE_BLOCK
    nb = t // tb
    return pl.pallas_call(
        _router_kernel,
        grid=(nb,),
        in_specs=[pl.BlockSpec((tb, d), lambda i: (i, 0)), pl.BlockSpec((ne, d), lambda i: (0, 0))],
        out_specs=[
            pl.BlockSpec((ne, tb), lambda i: (0, i)),
            pl.BlockSpec((ne, tb), lambda i: (0, i)),
            pl.BlockSpec((tb, ne), lambda i: (i, 0)),
            pl.BlockSpec((tb, ne), lambda i: (i, 0)),
            pl.BlockSpec((1, ne, 128), lambda i: (i, 0, 0)),
        ],
        out_shape=[
            jax.ShapeDtypeStruct((ne, t), F32),
            jax.ShapeDtypeStruct((ne, t), F32),
            jax.ShapeDtypeStruct((t, ne), F32),
            jax.ShapeDtypeStruct((t, ne), F32),
            jax.ShapeDtypeStruct((nb, ne, 128), F32),
        ],
        scratch_shapes=[pltpu.VMEM((ne, 128), F32)],
        compiler_params=_cparams("arbitrary"),
        name="moe_router",
    )(u, w_router.T)


def _moe_plan(after, n_tok):
    nb, ne = after.shape
    tm, tb = MOE_ROW_TILE, MOE_BLOCK
    n_tiles = (2 * n_tok) // tm + ne
    before = jnp.concatenate([jnp.zeros((1, ne), jnp.int32), after[:-1]], axis=0)
    totals = after[-1]
    tiles_per = (totals + tm - 1) // tm
    tile_end = jnp.cumsum(tiles_per)
    tile_base = tile_end - tiles_per
    n_used = tile_end[-1]
    base_rows = tile_base * tm
    ti = jnp.arange(n_tiles, dtype=jnp.int32)
    tile_expert = jnp.minimum(jnp.sum((ti[:, None] >= tile_end[None, :]).astype(jnp.int32), axis=1), ne - 1)
    last_expert = tile_expert[jnp.maximum(n_used - 1, 0)]
    tile_expert = jnp.where(ti < n_used, tile_expert, last_expert)
    per = tm // tb
    si = jnp.arange(n_tiles * per, dtype=jnp.int32)
    sub_expert = tile_expert[si // per]
    sub_r0 = (si - tile_base[sub_expert] * per) * tb
    sub_valid = jnp.logical_and(si // per < n_used, sub_r0 < totals[sub_expert])
    aft_e = after.T[sub_expert]
    bef_e = before.T[sub_expert]
    lo = jnp.sum((aft_e <= sub_r0[:, None]).astype(jnp.int32), axis=1)
    hi = jnp.sum((bef_e < (sub_r0 + tb)[:, None]).astype(jnp.int32), axis=1) - 1
    hi = jnp.minimum(hi, nb - 1)
    lo = jnp.where(sub_valid, lo, 1)
    hi = jnp.where(sub_valid, hi, 0)
    first_row = base_rows[None, :] + before
    win_blk = first_row // tb
    rel_off = base_rows[None, :] - win_blk * tb
    return dict(n_tiles=n_tiles, tile_expert=tile_expert.astype(jnp.int32), n_used=n_used.reshape(1).astype(jnp.int32),
                sub_expert=sub_expert.astype(jnp.int32), sub_r0=sub_r0.astype(jnp.int32),
                sub_lo=lo.astype(jnp.int32), sub_hi=hi.astype(jnp.int32),
                win_blk=win_blk.reshape(-1).astype(jnp.int32), rel_off=rel_off.reshape(-1).astype(jnp.int32))


def _dispatch_kernel(se_ref, r0_ref, lo_ref, hi_ref, pos_ref, u_ref, xs_ref, acc_ref):
    i = pl.program_id(0)
    e = se_ref[i]
    tb = xs_ref.shape[0]
    target = (r0_ref[i] + lax.broadcasted_iota(jnp.int32, (tb, tb), 0)).astype(F32)
    acc_ref[...] = jnp.zeros_like(acc_ref)

    def body(sb, carry):
        off = pl.multiple_of(sb * tb, tb)
        p = pos_ref[pl.ds(e, 1), pl.ds(off, tb)]
        acc_ref[...] += _dot(_onehot(p == target), u_ref[pl.ds(off, tb), :])
        return carry

    lax.fori_loop(lo_ref[i], hi_ref[i] + 1, body, 0)
    xs_ref[...] = acc_ref[...].astype(xs_ref.dtype)


def moe_dispatch(u, pos, plan):
    t, d = u.shape
    tb = MOE_BLOCK
    n_sub = plan["sub_expert"].shape[0]
    return pl.pallas_call(
        _dispatch_kernel,
        grid_spec=pltpu.PrefetchScalarGridSpec(
            num_scalar_prefetch=4,
            grid=(n_sub,),
            in_specs=[pl.BlockSpec(memory_space=pltpu.VMEM), pl.BlockSpec(memory_space=pltpu.VMEM)],
            out_specs=pl.BlockSpec((tb, d), lambda i, *_: (i, 0)),
            scratch_shapes=[pltpu.VMEM((tb, d), F32)],
        ),
        out_shape=jax.ShapeDtypeStruct((n_sub * tb, d), BF16),
        compiler_params=_cparams("arbitrary"),
        name="moe_dispatch",
    )(plan["sub_expert"], plan["sub_r0"], plan["sub_lo"], plan["sub_hi"], pos, u)


def _gffn_kernel(te_ref, nu_ref, x_ref, wg_ref, wu_ref, wo_ref, y_ref, acc_ref):
    i = pl.program_id(0)
    f = pl.program_id(1)
    used = i < nu_ref[0]

    @pl.when(jnp.logical_and(used, f == 0))
    def _():
        acc_ref[...] = jnp.zeros_like(acc_ref)

    @pl.when(used)
    def _():
        x = x_ref[...]
        gp = _dot(x, wg_ref[0])
        up = _dot(x, wu_ref[0])
        act = (gp * jax.nn.sigmoid(gp) * up).astype(BF16)
        acc_ref[...] += _dot(act, wo_ref[0])

    @pl.when(f == pl.num_programs(1) - 1)
    def _():
        y_ref[...] = jnp.where(used, acc_ref[...], 0.0).astype(y_ref.dtype)


def moe_grouped_ffn(xs, w_in, w_out, plan):
    nr, d = xs.shape
    ne, dff, _ = w_out.shape
    tm, tf = MOE_ROW_TILE, MOE_F_TILE
    nf = dff // tf
    n_tiles = nr // tm

    def fsel(i, f, nu):
        return jnp.where(i < nu[0], f, nf - 1)

    return pl.pallas_call(
        _gffn_kernel,
        grid_spec=pltpu.PrefetchScalarGridSpec(
            num_scalar_prefetch=2,
            grid=(n_tiles, nf),
            in_specs=[
                pl.BlockSpec((tm, d), lambda i, f, te, nu: (jnp.minimum(i, nu[0] - 1), 0)),
                pl.BlockSpec((1, d, tf), lambda i, f, te, nu: (te[i], 0, fsel(i, f, nu))),
                pl.BlockSpec((1, d, tf), lambda i, f, te, nu: (te[i], 0, nf + fsel(i, f, nu))),
                pl.BlockSpec((1, tf, d), lambda i, f, te, nu: (te[i], fsel(i, f, nu), 0)),
            ],
            out_specs=pl.BlockSpec((tm, d), lambda i, f, te, nu: (i, 0)),
            scratch_shapes=[pltpu.VMEM((tm, d), F32)],
        ),
        out_shape=jax.ShapeDtypeStruct((nr, d), BF16),
        compiler_params=_cparams("arbitrary", "arbitrary"),
        name="moe_grouped_ffn",
    )(plan["tile_expert"], plan["n_used"], xs, w_in, w_in, w_out)


def _combine_kernel(wb_ref, off_ref, *refs, n_experts, n_prompt_blocks):
    y_refs = refs[:2 * n_experts]
    posc_ref, combc_ref, x_ref, mod_ref, lng_ref, lnb_ref, op_ref, os_ref = refs[2 * n_experts:]
    b = pl.program_id(0)
    tb = x_ref.shape[0]
    lane = lax.broadcasted_iota(jnp.int32, (tb, tb), 1).astype(F32)
    posc = posc_ref[...]
    combc = combc_ref[...]
    moe = jnp.zeros(x_ref.shape, F32)
    for e in range(n_experts):
        pe = posc[:, e:e + 1]
        rel = pe + off_ref[b * n_experts + e].astype(F32)
        hit0 = jnp.logical_and(pe >= 0.0, rel == lane)
        hit1 = jnp.logical_and(pe >= 0.0, rel == lane + float(tb))
        ge = _dot(_onehot(hit0), y_refs[2 * e][...]) + _dot(_onehot(hit1), y_refs[2 * e + 1][...])
        moe = moe + combc[:, e:e + 1] * ge
    z = DEEPNORM_ALPHA * x_ref[...] + mod_ref[0, 5:6, :] * moe
    xn = _layer_norm(z, lng_ref[...], lnb_ref[...])

    @pl.when(b < n_prompt_blocks)
    def _():
        op_ref[...] = xn

    @pl.when(b >= n_prompt_blocks)
    def _():
        os_ref[...] = xn


def moe_combine(y, posc, combc, x, mod, ln_g, ln_b, plan, n_prompt_tok, sample_len):
    t, d = x.shape
    ne = posc.shape[1]
    tb = MOE_BLOCK
    nb = t // tb
    npb = n_prompt_tok // tb
    nyb = y.shape[0] // tb
    seg = functools.partial(_seg_of_tile, tile=tb, n_prompt_tok=n_prompt_tok, sample_len=sample_len)

    def yspec(e, k):
        return pl.BlockSpec((tb, d), lambda b, wb, off: (jnp.minimum(wb[b * ne + e] + k, nyb - 1), 0))

    in_specs = [yspec(e, k) for e in range(ne) for k in range(2)] + [
        pl.BlockSpec((tb, ne), lambda b, wb, off: (b, 0)),
        pl.BlockSpec((tb, ne), lambda b, wb, off: (b, 0)),
        pl.BlockSpec((tb, d), lambda b, wb, off: (b, 0)),
        pl.BlockSpec((1, 6, d), lambda b, wb, off: (seg(b), 0, 0)),
        pl.BlockSpec((1, d), lambda b, wb, off: (0, 0)),
        pl.BlockSpec((1, d), lambda b, wb, off: (0, 0)),
    ]
    return pl.pallas_call(
        functools.partial(_combine_kernel, n_experts=ne, n_prompt_blocks=npb),
        grid_spec=pltpu.PrefetchScalarGridSpec(
            num_scalar_prefetch=2,
            grid=(nb,),
            in_specs=in_specs,
            out_specs=[
                pl.BlockSpec((tb, d), lambda b, wb, off: (jnp.minimum(b, npb - 1), 0)),
                pl.BlockSpec((tb, d), lambda b, wb, off: (jnp.maximum(b - npb, 0), 0)),
            ],
        ),
        out_shape=[jax.ShapeDtypeStruct((n_prompt_tok, d), F32), jax.ShapeDtypeStruct((t - n_prompt_tok, d), F32)],
        compiler_params=_cparams("arbitrary"),
        name="moe_combine",
    )(plan["win_blk"], plan["rel_off"], *([y] * (2 * ne)), posc, combc, x, mod, ln_g.reshape(1, d), ln_b.reshape(1, d))


def _layer1(x, u, mod1, cache_k, cache_v, ln_g, ln_b, w_qkv, rpb, w_out, w_router, moe_w_in, moe_w_out,
            n_prompt, prompt_len, n_sample, sample_len):
    n_prompt_tok = n_prompt * prompt_len
    d = D_MODEL
    qkv, k_p, v_p = na_qkv(u, w_qkv.astype(BF16), n_prompt_tok)
    attn = context_attention(qkv, n_prompt, prompt_len)
    k_ctx = cache_k.reshape(n_sample, -1, d).astype(BF16)
    v_ctx = cache_v.reshape(n_sample, -1, d).astype(BF16)
    attn = neighbourhood_attention(qkv, attn, k_ctx, v_ctx, rpb, n_prompt_tok, n_sample, sample_len)
    x1, u1 = mixer_outproj(attn, x, mod1, w_out.astype(BF16), ln_g[0], ln_b[0], n_prompt_tok, sample_len)
    comb, pos, combc, posc, after = moe_router(u1, w_router)
    plan = _moe_plan(after[:, :, 0].astype(jnp.int32), x.shape[0])
    xs = moe_dispatch(u1, pos, plan)
    y = moe_grouped_ffn(xs, moe_w_in.astype(BF16), moe_w_out.astype(BF16), plan)
    y_p, y_s = moe_combine(y, posc, combc, x1, mod1, ln_g[1], ln_b[1], plan, n_prompt_tok, sample_len)
    return y_p, y_s, k_p, v_p


def kernel(x_prompt, x_sample, state_mlstm_C, state_mlstm_n, state_mlstm_m, cache_na_k, cache_na_v, c, c_ctx,
           ada_w, ada_b, ln_g, ln_b, mlstm_w_in, mlstm_b_gates, mlstm_norm_g, mlstm_w_out, na_w_qkv, na_rpb,
           na_w_out, ffn_w_in, ffn_w_out, moe_w_router, moe_w_in, moe_w_out):
    n_prompt, prompt_len, d = x_prompt.shape
    n_sample, sample_len, _ = x_sample.shape
    assert d == D_MODEL and prompt_len == MLSTM_CHUNK and sample_len % MLSTM_CHUNK == 0
    assert ada_w.shape[0] == DEPTH == 2
    x = jnp.concatenate([x_prompt.reshape(-1, d), x_sample.reshape(-1, d)], axis=0)
    mod = _modulation_tables(c, c_ctx, ada_w, ada_b)
    _, x2, u2, new_c, new_n, new_m = _layer0(
        x, mod[0], mod[1], state_mlstm_C[:, 0], state_mlstm_n[:, 0], state_mlstm_m[:, 0], ln_g[0], ln_b[0],
        mlstm_w_in[0], mlstm_b_gates[0], mlstm_norm_g[0], mlstm_w_out[0], ffn_w_in[0], ffn_w_out[0],
        n_prompt, prompt_len, n_sample, sample_len)
    y_p, y_s, k_p, v_p = _layer1(
        x2, u2, mod[1], cache_na_k[:, 0], cache_na_v[:, 0], ln_g[1], ln_b[1], na_w_qkv[0], na_rpb[0], na_w_out[0],
        moe_w_router[0], moe_w_in[0], moe_w_out[0], n_prompt, prompt_len, n_sample, sample_len)
    kv_shape = (n_prompt, 1, prompt_len, NA_HEADS, NA_DH)
    return (y_p.reshape(x_prompt.shape), y_s.reshape(x_sample.shape), new_c[:, None], new_n[:, None], new_m[:, None],
            k_p.reshape(kv_shape), v_p.reshape(kv_shape))
```

```python
import functools
import math

import jax
import jax.numpy as jnp
from jax import lax
from jax.experimental import pallas as pl
from jax.experimental.pallas import tpu as pltpu

F32 = jnp.float32
BF16 = jnp.bfloat16

D_MODEL = 1024
DEPTH = 2
GRID_W = 64
M_HEADS = 4
M_DK = 128
M_DV = 256
M_HK = M_HEADS * M_DK
NA_HEADS = 16
NA_DH = 64
NA_ROWS = 8
NA_COLS = 16
D_FF = 2816
N_EXPERTS = 8
DEEPNORM_ALPHA = (2.0 * DEPTH) ** 0.25
LN_EPS = 1e-5
NEG = -1e30

VMEM_LIMIT_BYTES = 56 * 1024 * 1024

MLSTM_CHUNK = 256
TOKEN_TILE = 512
FFN_TOKEN_TILE = 1024
FFN_F_TILE = 256
NA_QROWS = 4
NA_HALO = 4


def _cparams(*sem):
    return pltpu.CompilerParams(dimension_semantics=sem, vmem_limit_bytes=VMEM_LIMIT_BYTES)


def _dot(a, b):
    return jnp.dot(a, b, preferred_element_type=F32)


def _dot_nt(a, b):
    return lax.dot_general(a, b, (((1,), (1,)), ((), ())), preferred_element_type=F32)


def _onehot(mask):
    return jnp.where(mask, 1.0, 0.0).astype(BF16)


def _split3(x):
    hi = x.astype(BF16)
    r = x - hi.astype(F32)
    mid = r.astype(BF16)
    lo = (r - mid.astype(F32)).astype(BF16)
    return hi, mid, lo


def _layer_norm(z, g, b):
    mu = jnp.mean(z, axis=-1, keepdims=True)
    zc = z - mu
    var = jnp.mean(zc * zc, axis=-1, keepdims=True)
    return zc * lax.rsqrt(var + LN_EPS) * g + b


def _log_sigmoid(x):
    return jnp.minimum(x, 0.0) - jnp.log(1.0 + jnp.exp(-jnp.abs(x)))


def _seg_of_tile(i, tile, n_prompt_tok, sample_len):
    tok = i * tile
    return jnp.where(tok < n_prompt_tok, 0, 1 + (tok - n_prompt_tok) // sample_len)


def _adaln_kernel(c_ref, w_ref, b_ref, o_ref):
    c = c_ref[...]
    a = (c * jax.nn.sigmoid(c)).astype(BF16)
    o_ref[0] = _dot(a, w_ref[0].astype(BF16)) + b_ref[0]


def adaln(cvec, ada_w, ada_b):
    depth, d, n = ada_w.shape
    tn = 1536
    return pl.pallas_call(
        _adaln_kernel,
        grid=(depth, n // tn),
        in_specs=[
            pl.BlockSpec((8, d), lambda l, j: (0, 0)),
            pl.BlockSpec((1, d, tn), lambda l, j: (l, 0, j)),
            pl.BlockSpec((1, 1, tn), lambda l, j: (l, 0, j)),
        ],
        out_specs=pl.BlockSpec((1, 8, tn), lambda l, j: (l, 0, j)),
        out_shape=jax.ShapeDtypeStruct((depth, 8, n), F32),
        compiler_params=_cparams("arbitrary", "arbitrary"),
        name="adaln",
    )(cvec, ada_w, ada_b.reshape(depth, 1, n))


def _inproj_kernel(xp_ref, xs_ref, mod_ref, w_ref, wg_ref, wgt_ref, bg_ref, bgt_ref,
                   qkv_ref, o_ref, g_ref, gt_ref, *, n_prompt_tiles):
    x = jnp.where(pl.program_id(0) < n_prompt_tiles, xp_ref[...], xs_ref[...])
    u = (x * (1.0 + mod_ref[0, 1:2, :]) + mod_ref[0, 0:1, :]).astype(BF16)
    p = _dot(u, w_ref[...])
    nqkv = qkv_ref.shape[1]
    qkv_ref[...] = p[:, :nqkv].astype(BF16)
    o_ref[...] = p[:, nqkv:]
    g = _dot(u, wg_ref[...]) + bg_ref[...]
    col = lax.broadcasted_iota(jnp.int32, g.shape, 1)
    g_ref[...] = jnp.where(((col >> 2) & 1) == 1, _log_sigmoid(g), g)
    gt = _dot_nt(wgt_ref[...], u) + bgt_ref[...]
    row = lax.broadcasted_iota(jnp.int32, gt.shape, 0)
    gt_ref[...] = jnp.where(((row >> 2) & 1) == 1, _log_sigmoid(gt), gt)


def _split_token_specs(tile, d, n_prompt_tiles):
    return [pl.BlockSpec((tile, d), lambda i: (jnp.minimum(i, n_prompt_tiles - 1), 0)),
            pl.BlockSpec((tile, d), lambda i: (jnp.maximum(i - n_prompt_tiles, 0), 0))]


def mlstm_inproj(xp, xs, mod, w_main, w_gate, b_gate, sample_len):
    n_prompt_tok, d = xp.shape
    t = n_prompt_tok + xs.shape[0]
    n_main = w_main.shape[1]
    n_qkv = 2 * M_HK + D_MODEL
    ng = w_gate.shape[1]
    tt = TOKEN_TILE
    npt = n_prompt_tok // tt
    seg = functools.partial(_seg_of_tile, tile=tt, n_prompt_tok=n_prompt_tok, sample_len=sample_len)
    return pl.pallas_call(
        functools.partial(_inproj_kernel, n_prompt_tiles=npt),
        grid=(t // tt,),
        in_specs=_split_token_specs(tt, d, npt) + [
            pl.BlockSpec((1, 6, d), lambda i: (seg(i), 0, 0)),
            pl.BlockSpec((d, n_main), lambda i: (0, 0)),
            pl.BlockSpec((d, ng), lambda i: (0, 0)),
            pl.BlockSpec((ng, d), lambda i: (0, 0)),
            pl.BlockSpec((1, ng), lambda i: (0, 0)),
            pl.BlockSpec((ng, 1), lambda i: (0, 0)),
        ],
        out_specs=[
            pl.BlockSpec((tt, n_qkv), lambda i: (i, 0)),
            pl.BlockSpec((tt, n_main - n_qkv), lambda i: (i, 0)),
            pl.BlockSpec((tt, ng), lambda i: (i, 0)),
            pl.BlockSpec((ng, tt), lambda i: (0, i)),
        ],
        out_shape=[
            jax.ShapeDtypeStruct((t, n_qkv), BF16),
            jax.ShapeDtypeStruct((t, n_main - n_qkv), F32),
            jax.ShapeDtypeStruct((t, ng), F32),
            jax.ShapeDtypeStruct((ng, t), F32),
        ],
        compiler_params=_cparams("arbitrary"),
        name="mlstm_inproj",
    )(xp, xs, mod, w_main, w_gate, w_gate.T, b_gate.reshape(1, ng), b_gate.reshape(ng, 1))


def _mlstm_kernel(*refs, direction, n_prompt_chunks, chunks_per_sample):
    backward = direction == 1
    if backward:
        (q_ref, k_ref, v_ref, g_ref, gt_ref, c0_ref, n0_ref, m0_ref, hprev_ref, o_ref, ng_ref,
         out_ref, cf_ref, nf_ref, mf_ref, c_scr, n_scr, m_scr) = refs
    else:
        (q_ref, k_ref, v_ref, g_ref, gt_ref, c0_ref, n0_ref, m0_ref,
         out_ref, cf_ref, nf_ref, mf_ref, c_scr, n_scr, m_scr) = refs
    step = pl.program_id(0)
    is_prompt = step < n_prompt_chunks
    pos = (step - n_prompt_chunks) % chunks_per_sample
    L = q_ref.shape[0]

    @pl.when(is_prompt)
    def _():
        c_scr[...] = jnp.zeros_like(c_scr)
        n_scr[...] = jnp.zeros_like(n_scr)
        m_scr[...] = jnp.zeros_like(m_scr)

    @pl.when(jnp.logical_and(jnp.logical_not(is_prompt), pos == 0))
    def _():
        c_scr[...] = c0_ref[0]
        n_scr[...] = n0_ref[0]
        m_scr[...] = m0_ref[0]

    row = lax.broadcasted_iota(jnp.int32, (L, L), 0)
    col = lax.broadcasted_iota(jnp.int32, (L, L), 1)
    lower = col <= row
    upper = col >= row
    valid = upper if backward else lower
    tri_col = _onehot(valid)
    tri_row = _onehot(lower if backward else upper)

    g = g_ref[...]
    gt = gt_ref[...]
    ghi, gmid, glo = _split3(g)
    b_cols = _dot(tri_col, ghi) + _dot(tri_col, gmid) + _dot(tri_col, glo)
    thi, tmid, tlo = _split3(gt)
    b_rows = _dot(thi, tri_row) + _dot(tmid, tri_row) + _dot(tlo, tri_row)

    scale = M_DK ** -0.5
    for h in range(M_HEADS):
        ci = direction * 8 + h
        cf = direction * 8 + 4 + h
        li_row = gt[ci:ci + 1, :]
        lf_row = gt[cf:cf + 1, :]
        b_row = b_rows[cf:cf + 1, :]
        b_col = b_cols[:, cf:cf + 1]
        total = jnp.sum(lf_row, axis=1, keepdims=True)
        m_prev = m_scr[h][0:1, 0:1]
        n_prev = n_scr[h]
        c_prev = c_scr[h]

        qh = q_ref[:, h * M_DK:(h + 1) * M_DK]
        kh = k_ref[:, h * M_DK:(h + 1) * M_DK]
        vh = v_ref[:, h * M_DV:(h + 1) * M_DV]

        dmat = jnp.where(valid, b_col + (li_row - b_row), NEG)
        inter = b_col + m_prev
        m_t = jnp.maximum(inter, jnp.max(dmat, axis=1, keepdims=True))
        w = jnp.exp(dmat - m_t)
        a = jnp.exp(inter - m_t)
        s = _dot_nt(qh, kh) * scale * w
        qc = _dot(qh, c_prev.astype(BF16)) * scale
        qn = _dot_nt(qh, n_prev.astype(BF16))[:, 0:1] * scale
        num = a * qc + _dot(s.astype(BF16), vh)
        den = a * qn + jnp.sum(s, axis=1, keepdims=True)
        hh = num / jnp.maximum(jnp.abs(den), jnp.exp(-m_t))

        sl = slice(h * M_DV, (h + 1) * M_DV)
        if backward:
            ht = hprev_ref[:, sl] + hh
            mu = jnp.mean(ht, axis=-1, keepdims=True)
            hc = ht - mu
            var = jnp.mean(hc * hc, axis=-1, keepdims=True)
            hn = hc * lax.rsqrt(var + LN_EPS) * ng_ref[:, sl]
            out_ref[:, sl] = (hn * jax.nn.sigmoid(o_ref[:, sl])).astype(out_ref.dtype)
        else:
            out_ref[:, sl] = hh

        g_row = total - b_row + li_row
        m_new = jnp.maximum(total + m_prev, jnp.max(g_row, axis=1, keepdims=True))
        ws_row = jnp.exp(g_row - m_new)
        a0 = jnp.exp(total + m_prev - m_new)
        kt = kh.astype(F32).T
        c_new = a0 * c_prev + _dot((kt * ws_row).astype(BF16), vh)
        ws8 = jnp.broadcast_to(ws_row, (8, L)).astype(BF16)
        c_scr[h] = c_new
        n_scr[h] = a0 * n_prev + _dot(ws8, kh)
        m_scr[h] = jnp.broadcast_to(m_new, m_scr.shape[1:])

    @pl.when(is_prompt)
    def _():
        cf_ref[0] = c_scr[...]
        nf_ref[0] = n_scr[...]
        mf_ref[0] = m_scr[...]


def mlstm_direction(direction, qkv, g, gt, c0, n0, m0, n_prompt_chunks, chunks_per_sample, n_sample,
                    hprev=None, ogate=None, norm_g=None):
    L = MLSTM_CHUNK
    t = qkv.shape[0]
    n_chunks = t // L
    backward = direction == 1
    npc, cps = n_prompt_chunks, chunks_per_sample

    def chunk_of(s):
        rel = s - npc
        seq = rel // cps
        pos = rel % cps
        blk = npc + seq * cps + ((cps - 1 - pos) if backward else pos)
        return jnp.where(s < npc, s, blk)

    def seq_of(s):
        return jnp.clip((s - npc) // cps, 0, n_sample - 1)

    def prompt_of(s):
        return jnp.minimum(s, npc - 1)

    kq = M_HK // M_HK
    in_specs = [
        pl.BlockSpec((L, M_HK), lambda s: (chunk_of(s), 0)),
        pl.BlockSpec((L, M_HK), lambda s: (chunk_of(s), kq)),
        pl.BlockSpec((L, D_MODEL), lambda s: (chunk_of(s), 1)),
        pl.BlockSpec((L, 16), lambda s: (chunk_of(s), 0)),
        pl.BlockSpec((16, L), lambda s: (0, chunk_of(s))),
        pl.BlockSpec((1, M_HEADS, M_DK, M_DV), lambda s: (seq_of(s), 0, 0, 0)),
        pl.BlockSpec((1, M_HEADS, 8, 128), lambda s: (seq_of(s), 0, 0, 0)),
        pl.BlockSpec((1, M_HEADS, 8, 128), lambda s: (seq_of(s), 0, 0, 0)),
    ]
    args = [qkv, qkv, qkv, g, gt, c0, n0, m0]
    if backward:
        in_specs += [
            pl.BlockSpec((L, D_MODEL), lambda s: (chunk_of(s), 0)),
            pl.BlockSpec((L, D_MODEL), lambda s: (chunk_of(s), 0)),
            pl.BlockSpec((1, D_MODEL), lambda s: (0, 0)),
        ]
        args += [hprev, ogate, norm_g.reshape(1, D_MODEL)]
    out_dtype = BF16 if backward else F32
    return pl.pallas_call(
        functools.partial(_mlstm_kernel, direction=direction, n_prompt_chunks=npc, chunks_per_sample=cps),
        grid=(n_chunks,),
        in_specs=in_specs,
        out_specs=[
            pl.BlockSpec((L, D_MODEL), lambda s: (chunk_of(s), 0)),
            pl.BlockSpec((1, M_HEADS, M_DK, M_DV), lambda s: (prompt_of(s), 0, 0, 0)),
            pl.BlockSpec((1, M_HEADS, 8, 128), lambda s: (prompt_of(s), 0, 0, 0)),
            pl.BlockSpec((1, M_HEADS, 8, 128), lambda s: (prompt_of(s), 0, 0, 0)),
        ],
        out_shape=[
            jax.ShapeDtypeStruct((t, D_MODEL), out_dtype),
            jax.ShapeDtypeStruct((npc, M_HEADS, M_DK, M_DV), F32),
            jax.ShapeDtypeStruct((npc, M_HEADS, 8, 128), F32),
            jax.ShapeDtypeStruct((npc, M_HEADS, 8, 128), F32),
        ],
        scratch_shapes=[
            pltpu.VMEM((M_HEADS, M_DK, M_DV), F32),
            pltpu.VMEM((M_HEADS, 8, 128), F32),
            pltpu.VMEM((M_HEADS, 8, 128), F32),
        ],
        compiler_params=_cparams("arbitrary"),
        name="mlstm_bwd" if backward else "mlstm_fwd",
    )(*args)


def _outproj_kernel(a_ref, *refs, n_prompt_tiles):
    if n_prompt_tiles is None:
        x_ref, mod_ref, w_ref, lng_ref, lnb_ref, xo_ref, uo_ref = refs
        x = x_ref[...]
    else:
        xp_ref, xs_ref, mod_ref, w_ref, lng_ref, lnb_ref, xo_ref, uo_ref = refs
        x = jnp.where(pl.program_id(0) < n_prompt_tiles, xp_ref[...], xs_ref[...])
    y = _dot(a_ref[...], w_ref[...])
    z = DEEPNORM_ALPHA * x + mod_ref[0, 2:3, :] * y
    xn = _layer_norm(z, lng_ref[...], lnb_ref[...])
    xo_ref[...] = xn
    uo_ref[...] = (xn * (1.0 + mod_ref[0, 4:5, :]) + mod_ref[0, 3:4, :]).astype(BF16)


def mixer_outproj(a, x, mod, w, ln_g, ln_b, n_prompt_tok, sample_len):
    t, d = a.shape
    tt = TOKEN_TILE
    seg = functools.partial(_seg_of_tile, tile=tt, n_prompt_tok=n_prompt_tok, sample_len=sample_len)
    if isinstance(x, tuple):
        npt = n_prompt_tok // tt
        x_specs, x_args = _split_token_specs(tt, d, npt), list(x)
    else:
        npt = None
        x_specs, x_args = [pl.BlockSpec((tt, d), lambda i: (i, 0))], [x]
    return pl.pallas_call(
        functools.partial(_outproj_kernel, n_prompt_tiles=npt),
        grid=(t // tt,),
        in_specs=[pl.BlockSpec((tt, d), lambda i: (i, 0))] + x_specs + [
            pl.BlockSpec((1, 6, d), lambda i: (seg(i), 0, 0)),
            pl.BlockSpec((d, d), lambda i: (0, 0)),
            pl.BlockSpec((1, d), lambda i: (0, 0)),
            pl.BlockSpec((1, d), lambda i: (0, 0)),
        ],
        out_specs=[pl.BlockSpec((tt, d), lambda i: (i, 0)), pl.BlockSpec((tt, d), lambda i: (i, 0))],
        out_shape=[jax.ShapeDtypeStruct((t, d), F32), jax.ShapeDtypeStruct((t, d), BF16)],
        compiler_params=_cparams("arbitrary"),
        name="mixer_outproj",
    )(a, *x_args, mod, w, ln_g.reshape(1, d), ln_b.reshape(1, d))


def _ffn_kernel(u_ref, x_ref, mod_ref, modn_ref, wg_ref, wu_ref, wo_ref, lng_ref, lnb_ref,
                xo_ref, uo_ref, acc_ref):
    f = pl.program_id(1)

    @pl.when(f == 0)
    def _():
        acc_ref[...] = jnp.zeros_like(acc_ref)

    u = u_ref[...]
    gp = _dot(u, wg_ref[...])
    up = _dot(u, wu_ref[...])
    act = (gp * jax.nn.sigmoid(gp) * up).astype(BF16)
    acc_ref[...] += _dot(act, wo_ref[...])

    @pl.when(f == pl.num_programs(1) - 1)
    def _():
        z = DEEPNORM_ALPHA * x_ref[...] + mod_ref[0, 5:6, :] * acc_ref[...]
        xn = _layer_norm(z, lng_ref[...], lnb_ref[...])
        xo_ref[...] = xn
        uo_ref[...] = (xn * (1.0 + modn_ref[0, 1:2, :]) + modn_ref[0, 0:1, :]).astype(BF16)


def dense_ffn(u, x, mod, mod_next, w_in, w_out, ln_g, ln_b, n_prompt_tok, sample_len):
    t, d = x.shape
    dff = w_out.shape[0]
    tm, tf = FFN_TOKEN_TILE, FFN_F_TILE
    nf = dff // tf
    seg = functools.partial(_seg_of_tile, tile=tm, n_prompt_tok=n_prompt_tok, sample_len=sample_len)
    return pl.pallas_call(
        _ffn_kernel,
        grid=(t // tm, nf),
        in_specs=[
            pl.BlockSpec((tm, d), lambda i, f: (i, 0)),
            pl.BlockSpec((tm, d), lambda i, f: (i, 0)),
            pl.BlockSpec((1, 6, d), lambda i, f: (seg(i), 0, 0)),
            pl.BlockSpec((1, 6, d), lambda i, f: (seg(i), 0, 0)),
            pl.BlockSpec((d, tf), lambda i, f: (0, f)),
            pl.BlockSpec((d, tf), lambda i, f: (0, nf + f)),
            pl.BlockSpec((tf, d), lambda i, f: (f, 0)),
            pl.BlockSpec((1, d), lambda i, f: (0, 0)),
            pl.BlockSpec((1, d), lambda i, f: (0, 0)),
        ],
        out_specs=[pl.BlockSpec((tm, d), lambda i, f: (i, 0)), pl.BlockSpec((tm, d), lambda i, f: (i, 0))],
        out_shape=[jax.ShapeDtypeStruct((t, d), F32), jax.ShapeDtypeStruct((t, d), BF16)],
        scratch_shapes=[pltpu.VMEM((tm, d), F32)],
        compiler_params=_cparams("arbitrary", "arbitrary"),
        name="dense_ffn",
    )(u, x, mod, mod_next, w_in, w_in, w_out, ln_g.reshape(1, d), ln_b.reshape(1, d))


def _modulation_tables(c, c_ctx, ada_w, ada_b):
    n_s = c.shape[0]
    cvec = jnp.concatenate([c_ctx[None, :], c, jnp.zeros((8 - 1 - n_s, c.shape[1]), F32)], axis=0)
    mod = adaln(cvec, ada_w, ada_b)
    return mod[:, :1 + n_s, :].reshape(ada_w.shape[0], 1 + n_s, 6, c.shape[1])


def _layer0(xp, xs, mod0, mod1, state_c, state_n, state_m, ln_g, ln_b, w_in, b_gates, norm_g, w_out,
            ffn_w_in, ffn_w_out, n_prompt, prompt_len, n_sample, sample_len):
    n_prompt_tok = n_prompt * prompt_len
    n_main = 2 * M_HK + 2 * D_MODEL
    qkv, ogate, g, gt = mlstm_inproj(xp, xs, mod0, w_in[:, :n_main].astype(BF16), w_in[:, n_main:].astype(BF16),
                                     b_gates.reshape(-1), sample_len)
    npc = n_prompt_tok // MLSTM_CHUNK
    cps = sample_len // MLSTM_CHUNK
    rep = lambda a: jnp.broadcast_to(a[..., None, :], a.shape[:-1] + (8, a.shape[-1]))
    n0 = rep(state_n)
    m0 = jnp.broadcast_to(state_m[..., None, None], state_m.shape + (8, 128))
    hf, cf, nf, mf = mlstm_direction(0, qkv, g, gt, state_c[:, 0], n0[:, 0], m0[:, 0], npc, cps, n_sample)
    a, cb, nb, mb = mlstm_direction(1, qkv, g, gt, state_c[:, 1], n0[:, 1], m0[:, 1], npc, cps, n_sample,
                                    hprev=hf, ogate=ogate, norm_g=norm_g)
    x1, u1 = mixer_outproj(a, (xp, xs), mod0, w_out.astype(BF16), ln_g[0], ln_b[0], n_prompt_tok, sample_len)
    x2, u2 = dense_ffn(u1, x1, mod0, mod1, ffn_w_in.astype(BF16), ffn_w_out.astype(BF16), ln_g[1], ln_b[1],
                       n_prompt_tok, sample_len)
    new_c = jnp.stack([cf, cb], axis=1)
    new_n = jnp.stack([nf[:, :, 0, :], nb[:, :, 0, :]], axis=1)
    new_m = jnp.stack([mf[:, :, 0, 0], mb[:, :, 0, 0]], axis=1)
    return x1, x2, u2, new_c, new_n, new_m


def _qkv_kernel(u_ref, w_ref, qkv_ref, k_ref, v_ref, *, n_prompt_tiles):
    p = _dot(u_ref[...], w_ref[...])
    qkv_ref[...] = p.astype(BF16)
    d = k_ref.shape[1]

    @pl.when(pl.program_id(0) < n_prompt_tiles)
    def _():
        k_ref[...] = p[:, d:2 * d]
        v_ref[...] = p[:, 2 * d:]


def na_qkv(u, w, n_prompt_tok):
    t, d = u.shape
    tt = TOKEN_TILE
    npt = n_prompt_tok // tt
    return pl.pallas_call(
        functools.partial(_qkv_kernel, n_prompt_tiles=npt),
        grid=(t // tt,),
        in_specs=[pl.BlockSpec((tt, d), lambda i: (i, 0)), pl.BlockSpec((d, 3 * d), lambda i: (0, 0))],
        out_specs=[
            pl.BlockSpec((tt, 3 * d), lambda i: (i, 0)),
            pl.BlockSpec((tt, d), lambda i: (jnp.minimum(i, npt - 1), 0)),
            pl.BlockSpec((tt, d), lambda i: (jnp.minimum(i, npt - 1), 0)),
        ],
        out_shape=[
            jax.ShapeDtypeStruct((t, 3 * d), BF16),
            jax.ShapeDtypeStruct((n_prompt_tok, d), F32),
            jax.ShapeDtypeStruct((n_prompt_tok, d), F32),
        ],
        compiler_params=_cparams("arbitrary"),
        name="na_qkv",
    )(u, w)


def _head_pair_masks(scale):
    assert math.frexp(scale)[0] == 0.5
    lane = lax.broadcasted_iota(jnp.int32, (1, 2 * NA_DH), 1)
    first, second = lane < NA_DH, lane >= NA_DH
    return tuple((jnp.where(sel, scale, 0.0).astype(BF16), _onehot(sel)) for sel in (first, second))


def _softmax_pv(score_blocks, value_blocks):
    mx = None
    for s in score_blocks:
        bm = jnp.max(s, axis=1, keepdims=True)
        mx = bm if mx is None else jnp.maximum(mx, bm)
    acc, denom = None, None
    for s, v in zip(score_blocks, value_blocks):
        p = jnp.exp(s - mx)
        ps = jnp.sum(p, axis=1, keepdims=True)
        pv = _dot(p.astype(BF16), v)
        acc = pv if acc is None else acc + pv
        denom = ps if denom is None else denom + ps
    return acc / denom


def _ctx_attn_kernel(q_ref, k_ref, v_ref, o_ref):
    masks = _head_pair_masks(NA_DH ** -0.5)
    for hp in range(NA_HEADS // 2):
        sl = slice(hp * 2 * NA_DH, (hp + 1) * 2 * NA_DH)
        q2, k2, v2 = q_ref[:, sl], k_ref[:, sl], v_ref[:, sl]
        o2 = None
        for qmsk, vmsk in masks:
            s = _dot_nt(q2 * qmsk, k2)
            o = _softmax_pv([s], [v2 * vmsk])
            o2 = o if o2 is None else o2 + o
        o_ref[:, sl] = o2.astype(o_ref.dtype)


def context_attention(qkv, n_prompt, prompt_len):
    t = qkv.shape[0]
    d = D_MODEL
    return pl.pallas_call(
        _ctx_attn_kernel,
        grid=(n_prompt,),
        in_specs=[
            pl.BlockSpec((prompt_len, d), lambda b: (b, 0)),
            pl.BlockSpec((prompt_len, d), lambda b: (b, 1)),
            pl.BlockSpec((prompt_len, d), lambda b: (b, 2)),
        ],
        out_specs=pl.BlockSpec((prompt_len, d), lambda b: (b, 0)),
        out_shape=jax.ShapeDtypeStruct((t, d), BF16),
        compiler_params=_cparams("arbitrary"),
        name="context_attention",
    )(qkv, qkv, qkv)


def _na_kernel(q_ref, kp_ref, kc_ref, kn_ref, vp_ref, vc_ref, vn_ref, kctx_ref, vctx_ref, bias_ref, buf_ref,
               o_ref):
    del buf_ref
    m = q_ref.shape[0]
    halo = NA_HALO * GRID_W
    masks = _head_pair_masks(NA_DH ** -0.5)
    for hp in range(NA_HEADS // 2):
        sl = slice(hp * 2 * NA_DH, (hp + 1) * 2 * NA_DH)
        q2 = q_ref[:, sl]
        kloc = jnp.concatenate([kp_ref[m - halo:, sl], kc_ref[:, sl], kn_ref[:halo, sl]], axis=0)
        vloc = jnp.concatenate([vp_ref[m - halo:, sl], vc_ref[:, sl], vn_ref[:halo, sl]], axis=0)
        kctx = kctx_ref[0][:, sl]
        vctx = vctx_ref[0][:, sl]
        o2 = None
        for half, (qmsk, vmsk) in enumerate(masks):
            qm = q2 * qmsk
            s_loc = _dot_nt(qm, kloc) + bias_ref[0, 2 * hp + half]
            s_ctx = _dot_nt(qm, kctx)
            o = _softmax_pv([s_loc, s_ctx], [vloc * vmsk, vctx * vmsk])
            o2 = o if o2 is None else o2 + o
        o_ref[:, sl] = o2.astype(o_ref.dtype)


def _na_bias_kernel(rpb_ref, o_ref):
    nc, npos = rpb_ref.shape[1], o_ref.shape[1]
    shift = GRID_W.bit_length() - 1
    c = lax.broadcasted_iota(jnp.int32, (nc, npos), 0)
    pos = lax.broadcasted_iota(jnp.int32, (nc, npos), 1)
    qc = pos >> shift
    kc = pos & (GRID_W - 1)
    c_start = jnp.clip(qc - NA_COLS // 2, 0, GRID_W - NA_COLS)
    col_in = jnp.logical_and(kc >= c_start, kc < c_start + NA_COLS)
    dc = jnp.clip(kc - qc + NA_COLS - 1, 0, 2 * NA_COLS - 2)
    sel = _onehot(jnp.logical_and(c == dc, col_in))
    r1, r2, r3 = _split3(rpb_ref[...])
    o_ref[...] = _dot(r1, sel) + _dot(r2, sel) + _dot(r3, sel) + jnp.where(col_in[0:1, :], 0.0, NEG)


def _na_bias_table(rpb):
    nh, ndr, ndc = rpb.shape
    nc = 32
    rpb2 = jnp.pad(rpb.reshape(nh * ndr, ndc), ((0, 0), (0, nc - ndc)))
    toep = pl.pallas_call(
        _na_bias_kernel,
        out_shape=jax.ShapeDtypeStruct((nh * ndr, GRID_W * GRID_W), F32),
        name="na_bias",
    )(rpb2).reshape(nh, ndr, GRID_W, GRID_W)
    n_key_rows = NA_QROWS + 2 * NA_HALO
    masked = jnp.full((nh, GRID_W, GRID_W), NEG, F32)
    assert NA_QROWS >= NA_ROWS // 2 and NA_HALO == NA_ROWS // 2
    first_key_row = (lambda rq: NA_HALO, lambda rq: rq, lambda rq: NA_QROWS + NA_HALO - NA_ROWS)
    variants = []
    for first in first_key_row:
        rows = [jnp.concatenate([toep[:, kk - rq + NA_ROWS - 1 - NA_HALO]
                                 if first(rq) <= kk < first(rq) + NA_ROWS else masked
                                 for kk in range(n_key_rows)], axis=-1)
                for rq in range(NA_QROWS)]
        variants.append(jnp.concatenate(rows, axis=1))
    return jnp.stack(variants)


def neighbourhood_attention(qkv, attn_buf, k_ctx, v_ctx, rpb, n_prompt_tok, n_sample, sample_len):
    d = D_MODEL
    m = NA_QROWS * GRID_W
    nblk = sample_len // m
    assert nblk >= 3
    base = n_prompt_tok // m
    nloc = m + 2 * NA_HALO * GRID_W
    bias = _na_bias_table(rpb)

    def blk(col, off):
        return pl.BlockSpec((m, d), lambda b, j: (base + b * nblk + jnp.clip(j + off, 0, nblk - 1), col))

    def variant(j):
        return jnp.where(j == 0, 0, jnp.where(j == nblk - 1, 2, 1))

    return pl.pallas_call(
        _na_kernel,
        grid=(n_sample, nblk),
        in_specs=[
            blk(0, 0), blk(1, -1), blk(1, 0), blk(1, 1), blk(2, -1), blk(2, 0), blk(2, 1),
            pl.BlockSpec((1,) + k_ctx.shape[1:], lambda b, j: (b, 0, 0)),
            pl.BlockSpec((1,) + v_ctx.shape[1:], lambda b, j: (b, 0, 0)),
            pl.BlockSpec((1, NA_HEADS, m, nloc), lambda b, j: (variant(j), 0, 0, 0)),
            pl.BlockSpec(memory_space=pl.ANY),
        ],
        out_specs=pl.BlockSpec((m, d), lambda b, j: (base + b * nblk + j, 0)),
        out_shape=jax.ShapeDtypeStruct(attn_buf.shape, attn_buf.dtype),
        input_output_aliases={10: 0},
        compiler_params=_cparams("arbitrary", "arbitrary"),
        name="neighbourhood_attention",
    )(qkv, qkv, qkv, qkv, qkv, qkv, qkv, k_ctx, v_ctx, bias, attn_buf)


MOE_BLOCK = 256
MOE_ROW_TILE = 512
MOE_F_TILE = 1408


def _router_kernel(u_ref, wt_ref, comb_ref, pos_ref, combc_ref, posc_ref, after_ref, carry_scr):
    i = pl.program_id(0)

    @pl.when(i == 0)
    def _():
        carry_scr[...] = jnp.zeros_like(carry_scr)

    u = u_ref[...]
    w1, w2, w3 = _split3(wt_ref[...])
    lg = _dot_nt(w1, u) + _dot_nt(w2, u) + _dot_nt(w3, u)
    ne, nt = lg.shape
    e = lax.broadcasted_iota(jnp.int32, lg.shape, 0)
    m1 = jnp.max(lg, axis=0, keepdims=True)
    i1 = jnp.min(jnp.where(lg == m1, e, ne), axis=0, keepdims=True)
    sel1 = e == i1
    lg2 = jnp.where(sel1, -jnp.inf, lg)
    m2 = jnp.max(lg2, axis=0, keepdims=True)
    i2 = jnp.min(jnp.where(lg2 == m2, e, ne), axis=0, keepdims=True)
    sel2 = e == i2
    ex = jnp.exp(m2 - m1)
    wa = 1.0 / (1.0 + ex)
    wb = ex / (1.0 + ex)
    comb = jnp.where(sel1, wa, jnp.where(sel2, wb, 0.0))
    assign = jnp.logical_or(sel1, sel2)
    a = jnp.where(assign, 1.0, 0.0)
    row = lax.broadcasted_iota(jnp.int32, (nt, nt), 0)
    col = lax.broadcasted_iota(jnp.int32, (nt, nt), 1)
    tri = _onehot(row < col)
    carry = carry_scr[:, 0:1]
    rank = _dot(a.astype(BF16), tri) + carry
    pos = jnp.where(assign, rank, -1.0)
    comb_ref[...] = comb
    pos_ref[...] = pos
    combc_ref[...] = comb.T
    posc_ref[...] = pos.T
    carry_new = carry + jnp.sum(a, axis=1, keepdims=True)
    carry_scr[...] = jnp.broadcast_to(carry_new, carry_scr.shape)
    after_ref[0] = jnp.broadcast_to(carry_new, carry_scr.shape)


def moe_router(u, w_router):
    t, d = u.shape
    ne = w_router.shape[1]
    tb = MOE_BLOCK
    nb = t // tb
    return pl.pallas_call(
        _router_kernel,
        grid=(nb,),
        in_specs=[pl.BlockSpec((tb, d), lambda i: (i, 0)), pl.BlockSpec((ne, d), lambda i: (0, 0))],
        out_specs=[
            pl.BlockSpec((ne, tb), lambda i: (0, i)),
            pl.BlockSpec((ne, tb), lambda i: (0, i)),
            pl.BlockSpec((tb, ne), lambda i: (i, 0)),
            pl.BlockSpec((tb, ne), lambda i: (i, 0)),
            pl.BlockSpec((1, ne, 128), lambda i: (i, 0, 0)),
        ],
        out_shape=[
            jax.ShapeDtypeStruct((ne, t), F32),
            jax.ShapeDtypeStruct((ne, t), F32),
            jax.ShapeDtypeStruct((t, ne), F32),
            jax.ShapeDtypeStruct((t, ne), F32),
            jax.ShapeDtypeStruct((nb, ne, 128), F32),
        ],
        scratch_shapes=[pltpu.VMEM((ne, 128), F32)],
        compiler_params=_cparams("arbitrary"),
        name="moe_router",
    )(u, w_router.T)


def _moe_plan(after, n_tok):
    nb, ne = after.shape
    tm, tb = MOE_ROW_TILE, MOE_BLOCK
    n_tiles = (2 * n_tok) // tm + ne
    before = jnp.concatenate([jnp.zeros((1, ne), jnp.int32), after[:-1]], axis=0)
    totals = after[-1]
    tiles_per = (totals + tm - 1) // tm
    tile_end = jnp.cumsum(tiles_per)
    tile_base = tile_end - tiles_per
    n_used = tile_end[-1]
    base_rows = tile_base * tm
    ti = jnp.arange(n_tiles, dtype=jnp.int32)
    tile_expert = jnp.minimum(jnp.sum((ti[:, None] >= tile_end[None, :]).astype(jnp.int32), axis=1), ne - 1)
    last_expert = tile_expert[jnp.maximum(n_used - 1, 0)]
    tile_expert = jnp.where(ti < n_used, tile_expert, last_expert)
    per = tm // tb
    si = jnp.arange(n_tiles * per, dtype=jnp.int32)
    sub_expert = tile_expert[si // per]
    sub_r0 = (si - tile_base[sub_expert] * per) * tb
    sub_valid = jnp.logical_and(si // per < n_used, sub_r0 < totals[sub_expert])
    aft_e = after.T[sub_expert]
    bef_e = before.T[sub_expert]
    lo = jnp.sum((aft_e <= sub_r0[:, None]).astype(jnp.int32), axis=1)
    hi = jnp.sum((bef_e < (sub_r0 + tb)[:, None]).astype(jnp.int32), axis=1) - 1
    hi = jnp.minimum(hi, nb - 1)
    lo = jnp.where(sub_valid, lo, 1)
    hi = jnp.where(sub_valid, hi, 0)
    first_row = base_rows[None, :] + before
    win_blk = first_row // tb
    rel_off = base_rows[None, :] - win_blk * tb
    return dict(n_tiles=n_tiles, tile_expert=tile_expert.astype(jnp.int32), n_used=n_used.reshape(1).astype(jnp.int32),
                sub_expert=sub_expert.astype(jnp.int32), sub_r0=sub_r0.astype(jnp.int32),
                sub_lo=lo.astype(jnp.int32), sub_hi=hi.astype(jnp.int32),
                win_blk=win_blk.reshape(-1).astype(jnp.int32), rel_off=rel_off.reshape(-1).astype(jnp.int32))


def _dispatch_kernel(se_ref, r0_ref, lo_ref, hi_ref, pos_ref, u_ref, xs_ref, acc_ref):
    i = pl.program_id(0)
    e = se_ref[i]
    tb = xs_ref.shape[0]
    target = (r0_ref[i] + lax.broadcasted_iota(jnp.int32, (tb, tb), 0)).astype(F32)
    acc_ref[...] = jnp.zeros_like(acc_ref)

    def body(sb, carry):
        off = pl.multiple_of(sb * tb, tb)
        p = pos_ref[pl.ds(e, 1), pl.ds(off, tb)]
        acc_ref[...] += _dot(_onehot(p == target), u_ref[pl.ds(off, tb), :])
        return carry

    lax.fori_loop(lo_ref[i], hi_ref[i] + 1, body, 0)
    xs_ref[...] = acc_ref[...].astype(xs_ref.dtype)


def moe_dispatch(u, pos, plan):
    t, d = u.shape
    tb = MOE_BLOCK
    n_sub = plan["sub_expert"].shape[0]
    return pl.pallas_call(
        _dispatch_kernel,
        grid_spec=pltpu.PrefetchScalarGridSpec(
            num_scalar_prefetch=4,
            grid=(n_sub,),
            in_specs=[pl.BlockSpec(memory_space=pltpu.VMEM), pl.BlockSpec(memory_space=pltpu.VMEM)],
            out_specs=pl.BlockSpec((tb, d), lambda i, *_: (i, 0)),
            scratch_shapes=[pltpu.VMEM((tb, d), F32)],
        ),
        out_shape=jax.ShapeDtypeStruct((n_sub * tb, d), BF16),
        compiler_params=_cparams("arbitrary"),
        name="moe_dispatch",
    )(plan["sub_expert"], plan["sub_r0"], plan["sub_lo"], plan["sub_hi"], pos, u)


def _gffn_kernel(te_ref, nu_ref, x_ref, wg_ref, wu_ref, wo_ref, y_ref, acc_ref):
    i = pl.program_id(0)
    f = pl.program_id(1)
    used = i < nu_ref[0]

    @pl.when(jnp.logical_and(used, f == 0))
    def _():
        acc_ref[...] = jnp.zeros_like(acc_ref)

    @pl.when(used)
    def _():
        x = x_ref[...]
        gp = _dot(x, wg_ref[0])
        up = _dot(x, wu_ref[0])
        act = (gp * jax.nn.sigmoid(gp) * up).astype(BF16)
        acc_ref[...] += _dot(act, wo_ref[0])

    @pl.when(f == pl.num_programs(1) - 1)
    def _():
        y_ref[...] = jnp.where(used, acc_ref[...], 0.0).astype(y_ref.dtype)


def moe_grouped_ffn(xs, w_in, w_out, plan):
    nr, d = xs.shape
    ne, dff, _ = w_out.shape
    tm, tf = MOE_ROW_TILE, MOE_F_TILE
    nf = dff // tf
    n_tiles = nr // tm

    def fsel(i, f, nu):
        return jnp.where(i < nu[0], f, nf - 1)

    return pl.pallas_call(
        _gffn_kernel,
        grid_spec=pltpu.PrefetchScalarGridSpec(
            num_scalar_prefetch=2,
            grid=(n_tiles, nf),
            in_specs=[
                pl.BlockSpec((tm, d), lambda i, f, te, nu: (jnp.minimum(i, nu[0] - 1), 0)),
                pl.BlockSpec((1, d, tf), lambda i, f, te, nu: (te[i], 0, fsel(i, f, nu))),
                pl.BlockSpec((1, d, tf), lambda i, f, te, nu: (te[i], 0, nf + fsel(i, f, nu))),
                pl.BlockSpec((1, tf, d), lambda i, f, te, nu: (te[i], fsel(i, f, nu), 0)),
            ],
            out_specs=pl.BlockSpec((tm, d), lambda i, f, te, nu: (i, 0)),
            scratch_shapes=[pltpu.VMEM((tm, d), F32)],
        ),
        out_shape=jax.ShapeDtypeStruct((nr, d), BF16),
        compiler_params=_cparams("arbitrary", "arbitrary"),
        name="moe_grouped_ffn",
    )(plan["tile_expert"], plan["n_used"], xs, w_in, w_in, w_out)


def _combine_kernel(wb_ref, off_ref, *refs, n_experts, n_prompt_blocks):
    y_refs = refs[:2 * n_experts]
    posc_ref, combc_ref, x_ref, mod_ref, lng_ref, lnb_ref, op_ref, os_ref = refs[2 * n_experts:]
    b = pl.program_id(0)
    tb = x_ref.shape[0]
    lane = lax.broadcasted_iota(jnp.int32, (tb, tb), 1).astype(F32)
    posc = posc_ref[...]
    combc = combc_ref[...]
    moe = jnp.zeros(x_ref.shape, F32)
    for e in range(n_experts):
        pe = posc[:, e:e + 1]
        rel = pe + off_ref[b * n_experts + e].astype(F32)
        hit0 = jnp.logical_and(pe >= 0.0, rel == lane)
        hit1 = jnp.logical_and(pe >= 0.0, rel == lane + float(tb))
        ge = _dot(_onehot(hit0), y_refs[2 * e][...]) + _dot(_onehot(hit1), y_refs[2 * e + 1][...])
        moe = moe + combc[:, e:e + 1] * ge
    z = DEEPNORM_ALPHA * x_ref[...] + mod_ref[0, 5:6, :] * moe
    xn = _layer_norm(z, lng_ref[...], lnb_ref[...])

    @pl.when(b < n_prompt_blocks)
    def _():
        op_ref[...] = xn

    @pl.when(b >= n_prompt_blocks)
    def _():
        os_ref[...] = xn


def moe_combine(y, posc, combc, x, mod, ln_g, ln_b, plan, n_prompt_tok, sample_len):
    t, d = x.shape
    ne = posc.shape[1]
    tb = MOE_BLOCK
    nb = t // tb
    npb = n_prompt_tok // tb
    nyb = y.shape[0] // tb
    seg = functools.partial(_seg_of_tile, tile=tb, n_prompt_tok=n_prompt_tok, sample_len=sample_len)

    def yspec(e, k):
        return pl.BlockSpec((tb, d), lambda b, wb, off: (jnp.minimum(wb[b * ne + e] + k, nyb - 1), 0))

    in_specs = [yspec(e, k) for e in range(ne) for k in range(2)] + [
        pl.BlockSpec((tb, ne), lambda b, wb, off: (b, 0)),
        pl.BlockSpec((tb, ne), lambda b, wb, off: (b, 0)),
        pl.BlockSpec((tb, d), lambda b, wb, off: (b, 0)),
        pl.BlockSpec((1, 6, d), lambda b, wb, off: (seg(b), 0, 0)),
        pl.BlockSpec((1, d), lambda b, wb, off: (0, 0)),
        pl.BlockSpec((1, d), lambda b, wb, off: (0, 0)),
    ]
    return pl.pallas_call(
        functools.partial(_combine_kernel, n_experts=ne, n_prompt_blocks=npb),
        grid_spec=pltpu.PrefetchScalarGridSpec(
            num_scalar_prefetch=2,
            grid=(nb,),
            in_specs=in_specs,
            out_specs=[
                pl.BlockSpec((tb, d), lambda b, wb, off: (jnp.minimum(b, npb - 1), 0)),
                pl.BlockSpec((tb, d), lambda b, wb, off: (jnp.maximum(b - npb, 0), 0)),
            ],
        ),
        out_shape=[jax.ShapeDtypeStruct((n_prompt_tok, d), F32), jax.ShapeDtypeStruct((t - n_prompt_tok, d), F32)],
        compiler_params=_cparams("arbitrary"),
        name="moe_combine",
    )(plan["win_blk"], plan["rel_off"], *([y] * (2 * ne)), posc, combc, x, mod, ln_g.reshape(1, d), ln_b.reshape(1, d))


def _layer1(x, u, mod1, cache_k, cache_v, ln_g, ln_b, w_qkv, rpb, w_out, w_router, moe_w_in, moe_w_out,
            n_prompt, prompt_len, n_sample, sample_len):
    n_prompt_tok = n_prompt * prompt_len
    d = D_MODEL
    qkv, k_p, v_p = na_qkv(u, w_qkv.astype(BF16), n_prompt_tok)
    attn = context_attention(qkv, n_prompt, prompt_len)
    k_ctx = cache_k.reshape(n_sample, -1, d).astype(BF16)
    v_ctx = cache_v.reshape(n_sample, -1, d).astype(BF16)
    attn = neighbourhood_attention(qkv, attn, k_ctx, v_ctx, rpb, n_prompt_tok, n_sample, sample_len)
    x1, u1 = mixer_outproj(attn, x, mod1, w_out.astype(BF16), ln_g[0], ln_b[0], n_prompt_tok, sample_len)
    comb, pos, combc, posc, after = moe_router(u1, w_router)
    plan = _moe_plan(after[:, :, 0].astype(jnp.int32), x.shape[0])
    xs = moe_dispatch(u1, pos, plan)
    y = moe_grouped_ffn(xs, moe_w_in.astype(BF16), moe_w_out.astype(BF16), plan)
    y_p, y_s = moe_combine(y, posc, combc, x1, mod1, ln_g[1], ln_b[1], plan, n_prompt_tok, sample_len)
    return y_p, y_s, k_p, v_p


def kernel(x_prompt, x_sample, state_mlstm_C, state_mlstm_n, state_mlstm_m, cache_na_k, cache_na_v, c, c_ctx,
           ada_w, ada_b, ln_g, ln_b, mlstm_w_in, mlstm_b_gates, mlstm_norm_g, mlstm_w_out, na_w_qkv, na_rpb,
           na_w_out, ffn_w_in, ffn_w_out, moe_w_router, moe_w_in, moe_w_out):
    n_prompt, prompt_len, d = x_prompt.shape
    n_sample, sample_len, _ = x_sample.shape
    assert d == D_MODEL and prompt_len == MLSTM_CHUNK and sample_len % MLSTM_CHUNK == 0
    assert ada_w.shape[0] == DEPTH == 2
    mod = _modulation_tables(c, c_ctx, ada_w, ada_b)
    _, x2, u2, new_c, new_n, new_m = _layer0(
        x_prompt.reshape(-1, d), x_sample.reshape(-1, d), mod[0], mod[1], state_mlstm_C[:, 0], state_mlstm_n[:, 0], state_mlstm_m[:, 0], ln_g[0], ln_b[0],
        mlstm_w_in[0], mlstm_b_gates[0], mlstm_norm_g[0], mlstm_w_out[0], ffn_w_in[0], ffn_w_out[0],
        n_prompt, prompt_len, n_sample, sample_len)
    y_p, y_s, k_p, v_p = _layer1(
        x2, u2, mod[1], cache_na_k[:, 0], cache_na_v[:, 0], ln_g[1], ln_b[1], na_w_qkv[0], na_rpb[0], na_w_out[0],
        moe_w_router[0], moe_w_in[0], moe_w_out[0], n_prompt, prompt_len, n_sample, sample_len)
    kv_shape = (n_prompt, 1, prompt_len, NA_HEADS, NA_DH)
    return (y_p.reshape(x_prompt.shape), y_s.reshape(x_sample.shape), new_c[:, None], new_n[:, None], new_m[:, None],
            k_p.reshape(kv_shape), v_p.reshape(kv_shape))
```

```python
import functools
import math

import jax
import jax.numpy as jnp
from jax import lax
from jax.experimental import pallas as pl
from jax.experimental.pallas import tpu as pltpu

F32 = jnp.float32
BF16 = jnp.bfloat16

D_MODEL = 1024
DEPTH = 2
GRID_W = 64
M_HEADS = 4
M_DK = 128
M_DV = 256
M_HK = M_HEADS * M_DK
NA_HEADS = 16
NA_DH = 64
NA_ROWS = 8
NA_COLS = 16
D_FF = 2816
N_EXPERTS = 8
DEEPNORM_ALPHA = (2.0 * DEPTH) ** 0.25
LN_EPS = 1e-5
NEG = -1e30

VMEM_LIMIT_BYTES = 56 * 1024 * 1024

MLSTM_CHUNK = 256
TOKEN_TILE = 512
FFN_TOKEN_TILE = 512
FFN_F_TILE = 1408
NA_QROWS = 4
NA_HALO = 4


def _cparams(*sem):
    return pltpu.CompilerParams(dimension_semantics=sem, vmem_limit_bytes=VMEM_LIMIT_BYTES)


def _dot(a, b):
    return jnp.dot(a, b, preferred_element_type=F32)


def _dot_nt(a, b):
    return lax.dot_general(a, b, (((1,), (1,)), ((), ())), preferred_element_type=F32)


def _onehot(mask):
    return jnp.where(mask, 1.0, 0.0).astype(BF16)


def _split3(x):
    hi = x.astype(BF16)
    r = x - hi.astype(F32)
    mid = r.astype(BF16)
    lo = (r - mid.astype(F32)).astype(BF16)
    return hi, mid, lo


def _layer_norm(z, g, b):
    mu = jnp.mean(z, axis=-1, keepdims=True)
    zc = z - mu
    var = jnp.mean(zc * zc, axis=-1, keepdims=True)
    return zc * lax.rsqrt(var + LN_EPS) * g + b


def _log_sigmoid(x):
    return jnp.minimum(x, 0.0) - jnp.log(1.0 + jnp.exp(-jnp.abs(x)))


def _seg_of_tile(i, tile, n_prompt_tok, sample_len):
    tok = i * tile
    return jnp.where(tok < n_prompt_tok, 0, 1 + (tok - n_prompt_tok) // sample_len)


def _adaln_kernel(c_ref, w_ref, b_ref, o_ref):
    c = c_ref[...]
    a = (c * jax.nn.sigmoid(c)).astype(BF16)
    o_ref[0] = _dot(a, w_ref[0].astype(BF16)) + b_ref[0]


def adaln(cvec, ada_w, ada_b):
    depth, d, n = ada_w.shape
    tn = 1536
    return pl.pallas_call(
        _adaln_kernel,
        grid=(depth, n // tn),
        in_specs=[
            pl.BlockSpec((8, d), lambda l, j: (0, 0)),
            pl.BlockSpec((1, d, tn), lambda l, j: (l, 0, j)),
            pl.BlockSpec((1, 1, tn), lambda l, j: (l, 0, j)),
        ],
        out_specs=pl.BlockSpec((1, 8, tn), lambda l, j: (l, 0, j)),
        out_shape=jax.ShapeDtypeStruct((depth, 8, n), F32),
        compiler_params=_cparams("arbitrary", "arbitrary"),
        name="adaln",
    )(cvec, ada_w, ada_b.reshape(depth, 1, n))


def _inproj_kernel(xp_ref, xs_ref, mod_ref, w_ref, wg_ref, wgt_ref, bg_ref, bgt_ref,
                   qkv_ref, o_ref, g_ref, gt_ref, *, n_prompt_tiles):
    x = jnp.where(pl.program_id(0) < n_prompt_tiles, xp_ref[...], xs_ref[...])
    u = (x * (1.0 + mod_ref[0, 1:2, :]) + mod_ref[0, 0:1, :]).astype(BF16)
    p = _dot(u, w_ref[...])
    nqkv = qkv_ref.shape[1]
    qkv_ref[...] = p[:, :nqkv].astype(BF16)
    o_ref[...] = p[:, nqkv:]
    g = _dot(u, wg_ref[...]) + bg_ref[...]
    col = lax.broadcasted_iota(jnp.int32, g.shape, 1)
    g_ref[...] = jnp.where(((col >> 2) & 1) == 1, _log_sigmoid(g), g)
    gt = _dot_nt(wgt_ref[...], u) + bgt_ref[...]
    row = lax.broadcasted_iota(jnp.int32, gt.shape, 0)
    gt_ref[...] = jnp.where(((row >> 2) & 1) == 1, _log_sigmoid(gt), gt)


def _split_token_specs(tile, d, n_prompt_tiles):
    return [pl.BlockSpec((tile, d), lambda i: (jnp.minimum(i, n_prompt_tiles - 1), 0)),
            pl.BlockSpec((tile, d), lambda i: (jnp.maximum(i - n_prompt_tiles, 0), 0))]


def mlstm_inproj(xp, xs, mod, w_main, w_gate, b_gate, sample_len):
    n_prompt_tok, d = xp.shape
    t = n_prompt_tok + xs.shape[0]
    n_main = w_main.shape[1]
    n_qkv = 2 * M_HK + D_MODEL
    ng = w_gate.shape[1]
    tt = TOKEN_TILE
    npt = n_prompt_tok // tt
    seg = functools.partial(_seg_of_tile, tile=tt, n_prompt_tok=n_prompt_tok, sample_len=sample_len)
    return pl.pallas_call(
        functools.partial(_inproj_kernel, n_prompt_tiles=npt),
        grid=(t // tt,),
        in_specs=_split_token_specs(tt, d, npt) + [
            pl.BlockSpec((1, 6, d), lambda i: (seg(i), 0, 0)),
            pl.BlockSpec((d, n_main), lambda i: (0, 0)),
            pl.BlockSpec((d, ng), lambda i: (0, 0)),
            pl.BlockSpec((ng, d), lambda i: (0, 0)),
            pl.BlockSpec((1, ng), lambda i: (0, 0)),
            pl.BlockSpec((ng, 1), lambda i: (0, 0)),
        ],
        out_specs=[
            pl.BlockSpec((tt, n_qkv), lambda i: (i, 0)),
            pl.BlockSpec((tt, n_main - n_qkv), lambda i: (i, 0)),
            pl.BlockSpec((tt, ng), lambda i: (i, 0)),
            pl.BlockSpec((ng, tt), lambda i: (0, i)),
        ],
        out_shape=[
            jax.ShapeDtypeStruct((t, n_qkv), BF16),
            jax.ShapeDtypeStruct((t, n_main - n_qkv), F32),
            jax.ShapeDtypeStruct((t, ng), F32),
            jax.ShapeDtypeStruct((ng, t), F32),
        ],
        compiler_params=_cparams("arbitrary"),
        name="mlstm_inproj",
    )(xp, xs, mod, w_main, w_gate, w_gate.T, b_gate.reshape(1, ng), b_gate.reshape(ng, 1))


def _mlstm_kernel(*refs, direction, n_prompt_chunks, chunks_per_sample):
    backward = direction == 1
    if backward:
        (q_ref, k_ref, v_ref, g_ref, gt_ref, c0_ref, n0_ref, m0_ref, hprev_ref, o_ref, ng_ref,
         out_ref, cf_ref, nf_ref, mf_ref, c_scr, n_scr, m_scr) = refs
    else:
        (q_ref, k_ref, v_ref, g_ref, gt_ref, c0_ref, n0_ref, m0_ref,
         out_ref, cf_ref, nf_ref, mf_ref, c_scr, n_scr, m_scr) = refs
    step = pl.program_id(0)
    is_prompt = step < n_prompt_chunks
    pos = (step - n_prompt_chunks) % chunks_per_sample
    L = q_ref.shape[0]

    @pl.when(is_prompt)
    def _():
        c_scr[...] = jnp.zeros_like(c_scr)
        n_scr[...] = jnp.zeros_like(n_scr)
        m_scr[...] = jnp.zeros_like(m_scr)

    @pl.when(jnp.logical_and(jnp.logical_not(is_prompt), pos == 0))
    def _():
        c_scr[...] = c0_ref[0]
        n_scr[...] = n0_ref[0]
        m_scr[...] = m0_ref[0]

    row = lax.broadcasted_iota(jnp.int32, (L, L), 0)
    col = lax.broadcasted_iota(jnp.int32, (L, L), 1)
    lower = col <= row
    upper = col >= row
    valid = upper if backward else lower
    tri_col = _onehot(valid)
    tri_row = _onehot(lower if backward else upper)

    g = g_ref[...]
    gt = gt_ref[...]
    ghi, gmid, glo = _split3(g)
    b_cols = _dot(tri_col, ghi) + _dot(tri_col, gmid) + _dot(tri_col, glo)
    thi, tmid, tlo = _split3(gt)
    b_rows = _dot(thi, tri_row) + _dot(tmid, tri_row) + _dot(tlo, tri_row)

    scale = M_DK ** -0.5
    for h in range(M_HEADS):
        ci = direction * 8 + h
        cf = direction * 8 + 4 + h
        li_row = gt[ci:ci + 1, :]
        lf_row = gt[cf:cf + 1, :]
        b_row = b_rows[cf:cf + 1, :]
        b_col = b_cols[:, cf:cf + 1]
        total = jnp.sum(lf_row, axis=1, keepdims=True)
        m_prev = m_scr[h][0:1, 0:1]
        n_prev = n_scr[h]
        c_prev = c_scr[h]

        qh = q_ref[:, h * M_DK:(h + 1) * M_DK]
        kh = k_ref[:, h * M_DK:(h + 1) * M_DK]
        vh = v_ref[:, h * M_DV:(h + 1) * M_DV]

        dmat = jnp.where(valid, b_col + (li_row - b_row), NEG)
        inter = b_col + m_prev
        m_t = jnp.maximum(inter, jnp.max(dmat, axis=1, keepdims=True))
        w = jnp.exp(dmat - m_t)
        a = jnp.exp(inter - m_t)
        s = _dot_nt(qh, kh) * scale * w
        qc = _dot(qh, c_prev.astype(BF16)) * scale
        qn = _dot_nt(qh, n_prev.astype(BF16))[:, 0:1] * scale
        num = a * qc + _dot(s.astype(BF16), vh)
        den = a * qn + jnp.sum(s, axis=1, keepdims=True)
        hh = num / jnp.maximum(jnp.abs(den), jnp.exp(-m_t))

        sl = slice(h * M_DV, (h + 1) * M_DV)
        if backward:
            ht = hprev_ref[:, sl] + hh
            mu = jnp.mean(ht, axis=-1, keepdims=True)
            hc = ht - mu
            var = jnp.mean(hc * hc, axis=-1, keepdims=True)
            hn = hc * lax.rsqrt(var + LN_EPS) * ng_ref[:, sl]
            out_ref[:, sl] = (hn * jax.nn.sigmoid(o_ref[:, sl])).astype(out_ref.dtype)
        else:
            out_ref[:, sl] = hh

        g_row = total - b_row + li_row
        m_new = jnp.maximum(total + m_prev, jnp.max(g_row, axis=1, keepdims=True))
        ws_row = jnp.exp(g_row - m_new)
        a0 = jnp.exp(total + m_prev - m_new)
        kt = kh.astype(F32).T
        c_new = a0 * c_prev + _dot((kt * ws_row).astype(BF16), vh)
        ws8 = jnp.broadcast_to(ws_row, (8, L)).astype(BF16)
        c_scr[h] = c_new
        n_scr[h] = a0 * n_prev + _dot(ws8, kh)
        m_scr[h] = jnp.broadcast_to(m_new, m_scr.shape[1:])

    @pl.when(is_prompt)
    def _():
        cf_ref[0] = c_scr[...]
        nf_ref[0] = n_scr[...]
        mf_ref[0] = m_scr[...]


def mlstm_direction(direction, qkv, g, gt, c0, n0, m0, n_prompt_chunks, chunks_per_sample, n_sample,
                    hprev=None, ogate=None, norm_g=None):
    L = MLSTM_CHUNK
    t = qkv.shape[0]
    n_chunks = t // L
    backward = direction == 1
    npc, cps = n_prompt_chunks, chunks_per_sample

    def chunk_of(s):
        rel = s - npc
        seq = rel // cps
        pos = rel % cps
        blk = npc + seq * cps + ((cps - 1 - pos) if backward else pos)
        return jnp.where(s < npc, s, blk)

    def seq_of(s):
        return jnp.clip((s - npc) // cps, 0, n_sample - 1)

    def prompt_of(s):
        return jnp.minimum(s, npc - 1)

    kq = M_HK // M_HK
    in_specs = [
        pl.BlockSpec((L, M_HK), lambda s: (chunk_of(s), 0)),
        pl.BlockSpec((L, M_HK), lambda s: (chunk_of(s), kq)),
        pl.BlockSpec((L, D_MODEL), lambda s: (chunk_of(s), 1)),
        pl.BlockSpec((L, 16), lambda s: (chunk_of(s), 0)),
        pl.BlockSpec((16, L), lambda s: (0, chunk_of(s))),
        pl.BlockSpec((1, M_HEADS, M_DK, M_DV), lambda s: (seq_of(s), 0, 0, 0)),
        pl.BlockSpec((1, M_HEADS, 8, 128), lambda s: (seq_of(s), 0, 0, 0)),
        pl.BlockSpec((1, M_HEADS, 8, 128), lambda s: (seq_of(s), 0, 0, 0)),
    ]
    args = [qkv, qkv, qkv, g, gt, c0, n0, m0]
    if backward:
        in_specs += [
            pl.BlockSpec((L, D_MODEL), lambda s: (chunk_of(s), 0)),
            pl.BlockSpec((L, D_MODEL), lambda s: (chunk_of(s), 0)),
            pl.BlockSpec((1, D_MODEL), lambda s: (0, 0)),
        ]
        args += [hprev, ogate, norm_g.reshape(1, D_MODEL)]
    out_dtype = BF16 if backward else F32
    return pl.pallas_call(
        functools.partial(_mlstm_kernel, direction=direction, n_prompt_chunks=npc, chunks_per_sample=cps),
        grid=(n_chunks,),
        in_specs=in_specs,
        out_specs=[
            pl.BlockSpec((L, D_MODEL), lambda s: (chunk_of(s), 0)),
            pl.BlockSpec((1, M_HEADS, M_DK, M_DV), lambda s: (prompt_of(s), 0, 0, 0)),
            pl.BlockSpec((1, M_HEADS, 8, 128), lambda s: (prompt_of(s), 0, 0, 0)),
            pl.BlockSpec((1, M_HEADS, 8, 128), lambda s: (prompt_of(s), 0, 0, 0)),
        ],
        out_shape=[
            jax.ShapeDtypeStruct((t, D_MODEL), out_dtype),
            jax.ShapeDtypeStruct((npc, M_HEADS, M_DK, M_DV), F32),
            jax.ShapeDtypeStruct((npc, M_HEADS, 8, 128), F32),
            jax.ShapeDtypeStruct((npc, M_HEADS, 8, 128), F32),
        ],
        scratch_shapes=[
            pltpu.VMEM((M_HEADS, M_DK, M_DV), F32),
            pltpu.VMEM((M_HEADS, 8, 128), F32),
            pltpu.VMEM((M_HEADS, 8, 128), F32),
        ],
        compiler_params=_cparams("arbitrary"),
        name="mlstm_bwd" if backward else "mlstm_fwd",
    )(*args)


def _outproj_kernel(*refs, n_prompt_tiles, a_split, x_split):
    refs = list(refs)
    is_prompt = pl.program_id(0) < n_prompt_tiles

    def take(split):
        if split:
            p_ref, s_ref = refs.pop(0), refs.pop(0)
            return jnp.where(is_prompt, p_ref[...], s_ref[...])
        return refs.pop(0)[...]

    a = take(a_split)
    x = take(x_split)
    mod_ref, w_ref, lng_ref, lnb_ref, xo_ref, uo_ref = refs
    y = _dot(a, w_ref[...])
    z = DEEPNORM_ALPHA * x + mod_ref[0, 2:3, :] * y
    xn = _layer_norm(z, lng_ref[...], lnb_ref[...])
    xo_ref[...] = xn
    uo_ref[...] = (xn * (1.0 + mod_ref[0, 4:5, :]) + mod_ref[0, 3:4, :]).astype(BF16)


def mixer_outproj(a, x, mod, w, ln_g, ln_b, n_prompt_tok, sample_len):
    d = w.shape[0]
    tt = TOKEN_TILE
    npt = n_prompt_tok // tt
    seg = functools.partial(_seg_of_tile, tile=tt, n_prompt_tok=n_prompt_tok, sample_len=sample_len)

    def specs(v):
        if isinstance(v, tuple):
            return _split_token_specs(tt, d, npt), list(v), v[0].shape[0] + v[1].shape[0]
        return [pl.BlockSpec((tt, d), lambda i: (i, 0))], [v], v.shape[0]

    a_specs, a_args, t = specs(a)
    x_specs, x_args, _ = specs(x)
    return pl.pallas_call(
        functools.partial(_outproj_kernel, n_prompt_tiles=npt, a_split=isinstance(a, tuple),
                          x_split=isinstance(x, tuple)),
        grid=(t // tt,),
        in_specs=a_specs + x_specs + [
            pl.BlockSpec((1, 6, d), lambda i: (seg(i), 0, 0)),
            pl.BlockSpec((d, d), lambda i: (0, 0)),
            pl.BlockSpec((1, d), lambda i: (0, 0)),
            pl.BlockSpec((1, d), lambda i: (0, 0)),
        ],
        out_specs=[pl.BlockSpec((tt, d), lambda i: (i, 0)), pl.BlockSpec((tt, d), lambda i: (i, 0))],
        out_shape=[jax.ShapeDtypeStruct((t, d), F32), jax.ShapeDtypeStruct((t, d), BF16)],
        compiler_params=_cparams("arbitrary"),
        name="mixer_outproj",
    )(*a_args, *x_args, mod, w, ln_g.reshape(1, d), ln_b.reshape(1, d))


def _ffn_kernel(u_ref, x_ref, mod_ref, modn_ref, wg_ref, wu_ref, wo_ref, lng_ref, lnb_ref,
                xo_ref, uo_ref, acc_ref):
    f = pl.program_id(1)

    @pl.when(f == 0)
    def _():
        acc_ref[...] = jnp.zeros_like(acc_ref)

    u = u_ref[...]
    gp = _dot(u, wg_ref[...])
    up = _dot(u, wu_ref[...])
    act = (gp * jax.nn.sigmoid(gp) * up).astype(BF16)
    acc_ref[...] += _dot(act, wo_ref[...])

    @pl.when(f == pl.num_programs(1) - 1)
    def _():
        z = DEEPNORM_ALPHA * x_ref[...] + mod_ref[0, 5:6, :] * acc_ref[...]
        xn = _layer_norm(z, lng_ref[...], lnb_ref[...])
        xo_ref[...] = xn
        uo_ref[...] = (xn * (1.0 + modn_ref[0, 1:2, :]) + modn_ref[0, 0:1, :]).astype(BF16)


def dense_ffn(u, x, mod, mod_next, w_in, w_out, ln_g, ln_b, n_prompt_tok, sample_len):
    t, d = x.shape
    dff = w_out.shape[0]
    tm, tf = FFN_TOKEN_TILE, FFN_F_TILE
    nf = dff // tf
    seg = functools.partial(_seg_of_tile, tile=tm, n_prompt_tok=n_prompt_tok, sample_len=sample_len)
    return pl.pallas_call(
        _ffn_kernel,
        grid=(t // tm, nf),
        in_specs=[
            pl.BlockSpec((tm, d), lambda i, f: (i, 0)),
            pl.BlockSpec((tm, d), lambda i, f: (i, 0)),
            pl.BlockSpec((1, 6, d), lambda i, f: (seg(i), 0, 0)),
            pl.BlockSpec((1, 6, d), lambda i, f: (seg(i), 0, 0)),
            pl.BlockSpec((d, tf), lambda i, f: (0, f)),
            pl.BlockSpec((d, tf), lambda i, f: (0, nf + f)),
            pl.BlockSpec((tf, d), lambda i, f: (f, 0)),
            pl.BlockSpec((1, d), lambda i, f: (0, 0)),
            pl.BlockSpec((1, d), lambda i, f: (0, 0)),
        ],
        out_specs=[pl.BlockSpec((tm, d), lambda i, f: (i, 0)), pl.BlockSpec((tm, d), lambda i, f: (i, 0))],
        out_shape=[jax.ShapeDtypeStruct((t, d), F32), jax.ShapeDtypeStruct((t, d), BF16)],
        scratch_shapes=[pltpu.VMEM((tm, d), F32)],
        compiler_params=_cparams("arbitrary", "arbitrary"),
        name="dense_ffn",
    )(u, x, mod, mod_next, w_in, w_in, w_out, ln_g.reshape(1, d), ln_b.reshape(1, d))


def _modulation_tables(c, c_ctx, ada_w, ada_b):
    n_s = c.shape[0]
    cvec = jnp.concatenate([c_ctx[None, :], c, jnp.zeros((8 - 1 - n_s, c.shape[1]), F32)], axis=0)
    mod = adaln(cvec, ada_w, ada_b)
    return mod[:, :1 + n_s, :].reshape(ada_w.shape[0], 1 + n_s, 6, c.shape[1])


def _layer0(xp, xs, mod0, mod1, state_c, state_n, state_m, ln_g, ln_b, w_in, b_gates, norm_g, w_out,
            ffn_w_in, ffn_w_out, n_prompt, prompt_len, n_sample, sample_len):
    n_prompt_tok = n_prompt * prompt_len
    n_main = 2 * M_HK + 2 * D_MODEL
    qkv, ogate, g, gt = mlstm_inproj(xp, xs, mod0, w_in[:, :n_main].astype(BF16), w_in[:, n_main:].astype(BF16),
                                     b_gates.reshape(-1), sample_len)
    npc = n_prompt_tok // MLSTM_CHUNK
    cps = sample_len // MLSTM_CHUNK
    rep = lambda a: jnp.broadcast_to(a[..., None, :], a.shape[:-1] + (8, a.shape[-1]))
    n0 = rep(state_n)
    m0 = jnp.broadcast_to(state_m[..., None, None], state_m.shape + (8, 128))
    hf, cf, nf, mf = mlstm_direction(0, qkv, g, gt, state_c[:, 0], n0[:, 0], m0[:, 0], npc, cps, n_sample)
    a, cb, nb, mb = mlstm_direction(1, qkv, g, gt, state_c[:, 1], n0[:, 1], m0[:, 1], npc, cps, n_sample,
                                    hprev=hf, ogate=ogate, norm_g=norm_g)
    x1, u1 = mixer_outproj(a, (xp, xs), mod0, w_out.astype(BF16), ln_g[0], ln_b[0], n_prompt_tok, sample_len)
    x2, u2 = dense_ffn(u1, x1, mod0, mod1, ffn_w_in.astype(BF16), ffn_w_out.astype(BF16), ln_g[1], ln_b[1],
                       n_prompt_tok, sample_len)
    new_c = jnp.stack([cf, cb], axis=1)
    new_n = jnp.stack([nf[:, :, 0, :], nb[:, :, 0, :]], axis=1)
    new_m = jnp.stack([mf[:, :, 0, 0], mb[:, :, 0, 0]], axis=1)
    return x1, x2, u2, new_c, new_n, new_m


def _qkv_kernel(u_ref, w_ref, qkv_ref, k_ref, v_ref, *, n_prompt_tiles):
    p = _dot(u_ref[...], w_ref[...])
    qkv_ref[...] = p.astype(BF16)
    d = k_ref.shape[1]

    @pl.when(pl.program_id(0) < n_prompt_tiles)
    def _():
        k_ref[...] = p[:, d:2 * d]
        v_ref[...] = p[:, 2 * d:]


def na_qkv(u, w, n_prompt_tok):
    t, d = u.shape
    tt = TOKEN_TILE
    npt = n_prompt_tok // tt
    return pl.pallas_call(
        functools.partial(_qkv_kernel, n_prompt_tiles=npt),
        grid=(t // tt,),
        in_specs=[pl.BlockSpec((tt, d), lambda i: (i, 0)), pl.BlockSpec((d, 3 * d), lambda i: (0, 0))],
        out_specs=[
            pl.BlockSpec((tt, 3 * d), lambda i: (i, 0)),
            pl.BlockSpec((tt, d), lambda i: (jnp.minimum(i, npt - 1), 0)),
            pl.BlockSpec((tt, d), lambda i: (jnp.minimum(i, npt - 1), 0)),
        ],
        out_shape=[
            jax.ShapeDtypeStruct((t, 3 * d), BF16),
            jax.ShapeDtypeStruct((n_prompt_tok, d), F32),
            jax.ShapeDtypeStruct((n_prompt_tok, d), F32),
        ],
        compiler_params=_cparams("arbitrary"),
        name="na_qkv",
    )(u, w)


def _head_pair_masks(scale):
    assert math.frexp(scale)[0] == 0.5
    lane = lax.broadcasted_iota(jnp.int32, (1, 2 * NA_DH), 1)
    first, second = lane < NA_DH, lane >= NA_DH
    return tuple((jnp.where(sel, scale, 0.0).astype(BF16), _onehot(sel)) for sel in (first, second))


def _softmax_pv(score_blocks, value_blocks):
    mx = None
    for s in score_blocks:
        bm = jnp.max(s, axis=1, keepdims=True)
        mx = bm if mx is None else jnp.maximum(mx, bm)
    acc, denom = None, None
    for s, v in zip(score_blocks, value_blocks):
        p = jnp.exp(s - mx)
        ps = jnp.sum(p, axis=1, keepdims=True)
        pv = _dot(p.astype(BF16), v)
        acc = pv if acc is None else acc + pv
        denom = ps if denom is None else denom + ps
    return acc / denom


def _ctx_attn_kernel(q_ref, k_ref, v_ref, o_ref):
    masks = _head_pair_masks(NA_DH ** -0.5)
    for hp in range(NA_HEADS // 2):
        sl = slice(hp * 2 * NA_DH, (hp + 1) * 2 * NA_DH)
        q2, k2, v2 = q_ref[:, sl], k_ref[:, sl], v_ref[:, sl]
        o2 = None
        for qmsk, vmsk in masks:
            s = _dot_nt(q2 * qmsk, k2)
            o = _softmax_pv([s], [v2 * vmsk])
            o2 = o if o2 is None else o2 + o
        o_ref[:, sl] = o2.astype(o_ref.dtype)


def context_attention(qkv, n_prompt, prompt_len):
    t = n_prompt * prompt_len
    d = D_MODEL
    return pl.pallas_call(
        _ctx_attn_kernel,
        grid=(n_prompt,),
        in_specs=[
            pl.BlockSpec((prompt_len, d), lambda b: (b, 0)),
            pl.BlockSpec((prompt_len, d), lambda b: (b, 1)),
            pl.BlockSpec((prompt_len, d), lambda b: (b, 2)),
        ],
        out_specs=pl.BlockSpec((prompt_len, d), lambda b: (b, 0)),
        out_shape=jax.ShapeDtypeStruct((t, d), BF16),
        compiler_params=_cparams("arbitrary"),
        name="context_attention",
    )(qkv, qkv, qkv)


def _na_kernel(tile_ref, q_ref, kp_ref, kc_ref, kn_ref, vp_ref, vc_ref, vn_ref, kctx_ref, vctx_ref, bias_ref,
               o_ref):
    j = pl.program_id(1)
    nblk = pl.num_programs(1)
    m = q_ref.shape[0]
    halo = NA_HALO * GRID_W
    n_pairs = (NA_QROWS + 2 * NA_HALO) // 2
    variant = jnp.where(j == 0, 0, jnp.where(j == nblk - 1, 2, 1))
    kinds = [[tile_ref[(variant * NA_QROWS + rq) * n_pairs + kp] for kp in range(n_pairs)]
             for rq in range(NA_QROWS)]

    def bias_of(head):
        return jnp.concatenate(
            [jnp.concatenate([bias_ref[kinds[rq][kp], head] for kp in range(n_pairs)], axis=1)
             for rq in range(NA_QROWS)], axis=0)

    masks = _head_pair_masks(NA_DH ** -0.5)
    for hp in range(NA_HEADS // 2):
        sl = slice(hp * 2 * NA_DH, (hp + 1) * 2 * NA_DH)
        q2 = q_ref[:, sl]
        kloc = jnp.concatenate([kp_ref[m - halo:, sl], kc_ref[:, sl], kn_ref[:halo, sl]], axis=0)
        vloc = jnp.concatenate([vp_ref[m - halo:, sl], vc_ref[:, sl], vn_ref[:halo, sl]], axis=0)
        kctx = kctx_ref[0][:, sl]
        vctx = vctx_ref[0][:, sl]
        o2 = None
        for half, (qmsk, vmsk) in enumerate(masks):
            qm = q2 * qmsk
            s_loc = _dot_nt(qm, kloc) + bias_of(2 * hp + half)
            s_ctx = _dot_nt(qm, kctx)
            o = _softmax_pv([s_loc, s_ctx], [vloc * vmsk, vctx * vmsk])
            o2 = o if o2 is None else o2 + o
        o_ref[:, sl] = o2.astype(o_ref.dtype)


def _na_bias_kernel(lc_ref, rc_ref, lv_ref, rv_ref, o_ref):
    nc, npos = lc_ref.shape[2], o_ref.shape[2]
    pair_shift = (2 * GRID_W).bit_length() - 1
    c = lax.broadcasted_iota(jnp.int32, (nc, npos), 0)
    pos = lax.broadcasted_iota(jnp.int32, (nc, npos), 1)
    qc = pos >> pair_shift
    kc = pos & (GRID_W - 1)
    c_start = jnp.clip(qc - NA_COLS // 2, 0, GRID_W - NA_COLS)
    col_in = jnp.logical_and(kc >= c_start, kc < c_start + NA_COLS)
    dc = jnp.clip(kc - qc + NA_COLS - 1, 0, 2 * NA_COLS - 2)
    sel = _onehot(jnp.logical_and(c == dc, col_in))

    def pick(row_ref):
        r1, r2, r3 = _split3(row_ref[0])
        return _dot(r1, sel) + _dot(r2, sel) + _dot(r3, sel)

    right = (pos[0:1, :] & GRID_W) != 0
    val = jnp.where(right, pick(rc_ref), pick(lc_ref))
    visible = jnp.where(right, rv_ref[0], lv_ref[0]) > 0.0
    o_ref[0] = jnp.where(jnp.logical_and(visible, col_in[0:1, :]), val, NEG)


def _na_bias_plan():
    ndr = 2 * NA_ROWS - 1
    assert NA_QROWS >= NA_ROWS // 2 and NA_HALO == NA_ROWS // 2 and (NA_QROWS + 2 * NA_HALO) % 2 == 0
    ok = lambda dr: dr if dr is not None and 0 <= dr < ndr else None
    kinds = [(ok(dr), ok(dr + 1)) for dr in range(-1, ndr)]
    first_key_row = (lambda rq: NA_HALO, lambda rq: rq, lambda rq: NA_QROWS + NA_HALO - NA_ROWS)
    table = []
    for first in first_key_row:
        for rq in range(NA_QROWS):
            for kp in range((NA_QROWS + 2 * NA_HALO) // 2):
                drs = []
                for kk in (2 * kp, 2 * kp + 1):
                    visible = first(rq) <= kk < first(rq) + NA_ROWS
                    drs.append(kk - rq + NA_ROWS - 1 - NA_HALO if visible else None)
                kind = (drs[0], drs[1])
                if kind not in kinds:
                    kinds.append(kind)
                table.append(kinds.index(kind))
    return kinds, table


def _na_bias_tiles(rpb):
    nh, ndr, ndc = rpb.shape
    nc = 32
    kinds, table = _na_bias_plan()
    nk = len(kinds)
    rpb_p = jnp.pad(rpb, ((0, 0), (0, 0), (0, nc - ndc)))
    zero = jnp.zeros((nh, nc), F32)
    lc = jnp.stack([zero if l is None else rpb_p[:, l] for l, _ in kinds])
    rc = jnp.stack([zero if r is None else rpb_p[:, r] for _, r in kinds])
    vis = lambda flags: jnp.broadcast_to(jnp.asarray(flags, F32)[:, None, None], (nk, nh, 1))
    lv = vis([0.0 if l is None else 1.0 for l, _ in kinds])
    rv = vis([0.0 if r is None else 1.0 for _, r in kinds])
    npos = GRID_W * 2 * GRID_W
    row_spec = pl.BlockSpec((1, nh, nc), lambda t: (t, 0, 0))
    flag_spec = pl.BlockSpec((1, nh, 1), lambda t: (t, 0, 0))
    tiles = pl.pallas_call(
        _na_bias_kernel,
        grid=(nk,),
        in_specs=[row_spec, row_spec, flag_spec, flag_spec],
        out_specs=pl.BlockSpec((1, nh, npos), lambda t: (t, 0, 0)),
        out_shape=jax.ShapeDtypeStruct((nk, nh, npos), F32),
        compiler_params=_cparams("arbitrary"),
        name="na_bias",
    )(lc, rc, lv, rv)
    return tiles.reshape(nk, nh, GRID_W, 2 * GRID_W), jnp.asarray(table, jnp.int32)


def neighbourhood_attention(qkv, k_ctx, v_ctx, rpb, n_prompt_tok, n_sample, sample_len):
    d = D_MODEL
    m = NA_QROWS * GRID_W
    nblk = sample_len // m
    assert nblk >= 3
    base = n_prompt_tok // m
    bias, table = _na_bias_tiles(rpb)

    def blk(col, off):
        return pl.BlockSpec((m, d), lambda b, j, tbl: (base + b * nblk + jnp.clip(j + off, 0, nblk - 1), col))

    return pl.pallas_call(
        _na_kernel,
        grid_spec=pltpu.PrefetchScalarGridSpec(
            num_scalar_prefetch=1,
            grid=(n_sample, nblk),
            in_specs=[
                blk(0, 0), blk(1, -1), blk(1, 0), blk(1, 1), blk(2, -1), blk(2, 0), blk(2, 1),
                pl.BlockSpec((1,) + k_ctx.shape[1:], lambda b, j, tbl: (b, 0, 0)),
                pl.BlockSpec((1,) + v_ctx.shape[1:], lambda b, j, tbl: (b, 0, 0)),
                pl.BlockSpec(bias.shape, lambda b, j, tbl: (0, 0, 0, 0)),
            ],
            out_specs=pl.BlockSpec((m, d), lambda b, j, tbl: (b * nblk + j, 0)),
        ),
        out_shape=jax.ShapeDtypeStruct((n_sample * sample_len, d), BF16),
        compiler_params=_cparams("arbitrary", "arbitrary"),
        name="neighbourhood_attention",
    )(table, qkv, qkv, qkv, qkv, qkv, qkv, qkv, k_ctx, v_ctx, bias)


MOE_BLOCK = 256
MOE_ROW_TILE = 512
MOE_F_TILE = 1408


def _router_kernel(u_ref, wt_ref, comb_ref, pos_ref, combc_ref, posc_ref, after_ref, carry_scr):
    i = pl.program_id(0)

    @pl.when(i == 0)
    def _():
        carry_scr[...] = jnp.zeros_like(carry_scr)

    u = u_ref[...]
    w1, w2, w3 = _split3(wt_ref[...])
    lg = _dot_nt(w1, u) + _dot_nt(w2, u) + _dot_nt(w3, u)
    ne, nt = lg.shape
    e = lax.broadcasted_iota(jnp.int32, lg.shape, 0)
    m1 = jnp.max(lg, axis=0, keepdims=True)
    i1 = jnp.min(jnp.where(lg == m1, e, ne), axis=0, keepdims=True)
    sel1 = e == i1
    lg2 = jnp.where(sel1, -jnp.inf, lg)
    m2 = jnp.max(lg2, axis=0, keepdims=True)
    i2 = jnp.min(jnp.where(lg2 == m2, e, ne), axis=0, keepdims=True)
    sel2 = e == i2
    ex = jnp.exp(m2 - m1)
    wa = 1.0 / (1.0 + ex)
    wb = ex / (1.0 + ex)
    comb = jnp.where(sel1, wa, jnp.where(sel2, wb, 0.0))
    assign = jnp.logical_or(sel1, sel2)
    a = jnp.where(assign, 1.0, 0.0)
    row = lax.broadcasted_iota(jnp.int32, (nt, nt), 0)
    col = lax.broadcasted_iota(jnp.int32, (nt, nt), 1)
    tri = _onehot(row < col)
    carry = carry_scr[:, 0:1]
    rank = _dot(a.astype(BF16), tri) + carry
    pos = jnp.where(assign, rank, -1.0)
    comb_ref[...] = comb
    pos_ref[...] = pos
    combc_ref[...] = comb.T
    posc_ref[...] = pos.T
    carry_new = carry + jnp.sum(a, axis=1, keepdims=True)
    carry_scr[...] = jnp.broadcast_to(carry_new, carry_scr.shape)
    after_ref[0] = jnp.broadcast_to(carry_new, carry_scr.shape)


def moe_router(u, w_router):
    t, d = u.shape
    ne = w_router.shape[1]
    tb = MOE_BLOCK
    nb = t // tb
    return pl.pallas_call(
        _router_kernel,
        grid=(nb,),
        in_specs=[pl.BlockSpec((tb, d), lambda i: (i, 0)), pl.BlockSpec((ne, d), lambda i: (0, 0))],
        out_specs=[
            pl.BlockSpec((ne, tb), lambda i: (0, i)),
            pl.BlockSpec((ne, tb), lambda i: (0, i)),
            pl.BlockSpec((tb, ne), lambda i: (i, 0)),
            pl.BlockSpec((tb, ne), lambda i: (i, 0)),
            pl.BlockSpec((1, ne, 128), lambda i: (i, 0, 0)),
        ],
        out_shape=[
            jax.ShapeDtypeStruct((ne, t), F32),
            jax.ShapeDtypeStruct((ne, t), F32),
            jax.ShapeDtypeStruct((t, ne), F32),
            jax.ShapeDtypeStruct((t, ne), F32),
            jax.ShapeDtypeStruct((nb, ne, 128), F32),
        ],
        scratch_shapes=[pltpu.VMEM((ne, 128), F32)],
        compiler_params=_cparams("arbitrary"),
        name="moe_router",
    )(u, w_router.T)


def _moe_plan(after, n_tok):
    nb, ne = after.shape
    tm, tb = MOE_ROW_TILE, MOE_BLOCK
    n_tiles = (2 * n_tok) // tm + ne
    before = jnp.concatenate([jnp.zeros((1, ne), jnp.int32), after[:-1]], axis=0)
    totals = after[-1]
    tiles_per = (totals + tm - 1) // tm
    tile_end = jnp.cumsum(tiles_per)
    tile_base = tile_end - tiles_per
    n_used = tile_end[-1]
    base_rows = tile_base * tm
    ti = jnp.arange(n_tiles, dtype=jnp.int32)
    tile_expert = jnp.minimum(jnp.sum((ti[:, None] >= tile_end[None, :]).astype(jnp.int32), axis=1), ne - 1)
    last_expert = tile_expert[jnp.maximum(n_used - 1, 0)]
    tile_expert = jnp.where(ti < n_used, tile_expert, last_expert)
    per = tm // tb
    si = jnp.arange(n_tiles * per, dtype=jnp.int32)
    sub_expert = tile_expert[si // per]
    sub_r0 = (si - tile_base[sub_expert] * per) * tb
    sub_valid = jnp.logical_and(si // per < n_used, sub_r0 < totals[sub_expert])
    aft_e = after.T[sub_expert]
    bef_e = before.T[sub_expert]
    lo = jnp.sum((aft_e <= sub_r0[:, None]).astype(jnp.int32), axis=1)
    hi = jnp.sum((bef_e < (sub_r0 + tb)[:, None]).astype(jnp.int32), axis=1) - 1
    hi = jnp.minimum(hi, nb - 1)
    lo = jnp.where(sub_valid, lo, 1)
    hi = jnp.where(sub_valid, hi, 0)
    first_row = base_rows[None, :] + before
    win_blk = first_row // tb
    rel_off = base_rows[None, :] - win_blk * tb
    return dict(n_tiles=n_tiles, tile_expert=tile_expert.astype(jnp.int32), n_used=n_used.reshape(1).astype(jnp.int32),
                sub_expert=sub_expert.astype(jnp.int32), sub_r0=sub_r0.astype(jnp.int32),
                sub_lo=lo.astype(jnp.int32), sub_hi=hi.astype(jnp.int32),
                win_blk=win_blk.reshape(-1).astype(jnp.int32), rel_off=rel_off.reshape(-1).astype(jnp.int32))


def _dispatch_kernel(se_ref, r0_ref, lo_ref, hi_ref, pos_ref, u_ref, xs_ref, acc_ref):
    i = pl.program_id(0)
    e = se_ref[i]
    tb = xs_ref.shape[0]
    target = (r0_ref[i] + lax.broadcasted_iota(jnp.int32, (tb, tb), 0)).astype(F32)
    acc_ref[...] = jnp.zeros_like(acc_ref)

    def body(sb, carry):
        off = pl.multiple_of(sb * tb, tb)
        p = pos_ref[pl.ds(e, 1), pl.ds(off, tb)]
        acc_ref[...] += _dot(_onehot(p == target), u_ref[pl.ds(off, tb), :])
        return carry

    lax.fori_loop(lo_ref[i], hi_ref[i] + 1, body, 0)
    xs_ref[...] = acc_ref[...].astype(xs_ref.dtype)


def moe_dispatch(u, pos, plan):
    t, d = u.shape
    tb = MOE_BLOCK
    n_sub = plan["sub_expert"].shape[0]
    return pl.pallas_call(
        _dispatch_kernel,
        grid_spec=pltpu.PrefetchScalarGridSpec(
            num_scalar_prefetch=4,
            grid=(n_sub,),
            in_specs=[pl.BlockSpec(memory_space=pltpu.VMEM), pl.BlockSpec(memory_space=pltpu.VMEM)],
            out_specs=pl.BlockSpec((tb, d), lambda i, *_: (i, 0)),
            scratch_shapes=[pltpu.VMEM((tb, d), F32)],
        ),
        out_shape=jax.ShapeDtypeStruct((n_sub * tb, d), BF16),
        compiler_params=_cparams("arbitrary"),
        name="moe_dispatch",
    )(plan["sub_expert"], plan["sub_r0"], plan["sub_lo"], plan["sub_hi"], pos, u)


def _gffn_kernel(te_ref, nu_ref, x_ref, wg_ref, wu_ref, wo_ref, y_ref, acc_ref):
    i = pl.program_id(0)
    f = pl.program_id(1)
    used = i < nu_ref[0]

    @pl.when(jnp.logical_and(used, f == 0))
    def _():
        acc_ref[...] = jnp.zeros_like(acc_ref)

    @pl.when(used)
    def _():
        x = x_ref[...]
        gp = _dot(x, wg_ref[0])
        up = _dot(x, wu_ref[0])
        act = (gp * jax.nn.sigmoid(gp) * up).astype(BF16)
        acc_ref[...] += _dot(act, wo_ref[0])

    @pl.when(f == pl.num_programs(1) - 1)
    def _():
        y_ref[...] = jnp.where(used, acc_ref[...], 0.0).astype(y_ref.dtype)


def moe_grouped_ffn(xs, w_in, w_out, plan):
    nr, d = xs.shape
    ne, dff, _ = w_out.shape
    tm, tf = MOE_ROW_TILE, MOE_F_TILE
    nf = dff // tf
    n_tiles = nr // tm

    def fsel(i, f, nu):
        return jnp.where(i < nu[0], f, nf - 1)

    return pl.pallas_call(
        _gffn_kernel,
        grid_spec=pltpu.PrefetchScalarGridSpec(
            num_scalar_prefetch=2,
            grid=(n_tiles, nf),
            in_specs=[
                pl.BlockSpec((tm, d), lambda i, f, te, nu: (jnp.minimum(i, nu[0] - 1), 0)),
                pl.BlockSpec((1, d, tf), lambda i, f, te, nu: (te[i], 0, fsel(i, f, nu))),
                pl.BlockSpec((1, d, tf), lambda i, f, te, nu: (te[i], 0, nf + fsel(i, f, nu))),
                pl.BlockSpec((1, tf, d), lambda i, f, te, nu: (te[i], fsel(i, f, nu), 0)),
            ],
            out_specs=pl.BlockSpec((tm, d), lambda i, f, te, nu: (i, 0)),
            scratch_shapes=[pltpu.VMEM((tm, d), F32)],
        ),
        out_shape=jax.ShapeDtypeStruct((nr, d), BF16),
        compiler_params=_cparams("arbitrary", "arbitrary"),
        name="moe_grouped_ffn",
    )(plan["tile_expert"], plan["n_used"], xs, w_in, w_in, w_out)


def _combine_kernel(wb_ref, off_ref, *refs, n_experts, n_prompt_blocks):
    y_refs = refs[:2 * n_experts]
    posc_ref, combc_ref, x_ref, mod_ref, lng_ref, lnb_ref, op_ref, os_ref = refs[2 * n_experts:]
    b = pl.program_id(0)
    tb = x_ref.shape[0]
    lane = lax.broadcasted_iota(jnp.int32, (tb, tb), 1).astype(F32)
    posc = posc_ref[...]
    combc = combc_ref[...]
    moe = jnp.zeros(x_ref.shape, F32)
    for e in range(n_experts):
        pe = posc[:, e:e + 1]
        rel = pe + off_ref[b * n_experts + e].astype(F32)
        hit0 = jnp.logical_and(pe >= 0.0, rel == lane)
        hit1 = jnp.logical_and(pe >= 0.0, rel == lane + float(tb))
        ge = _dot(_onehot(hit0), y_refs[2 * e][...]) + _dot(_onehot(hit1), y_refs[2 * e + 1][...])
        moe = moe + combc[:, e:e + 1] * ge
    z = DEEPNORM_ALPHA * x_ref[...] + mod_ref[0, 5:6, :] * moe
    xn = _layer_norm(z, lng_ref[...], lnb_ref[...])

    @pl.when(b < n_prompt_blocks)
    def _():
        op_ref[...] = xn

    @pl.when(b >= n_prompt_blocks)
    def _():
        os_ref[...] = xn


def moe_combine(y, posc, combc, x, mod, ln_g, ln_b, plan, n_prompt_tok, sample_len):
    t, d = x.shape
    ne = posc.shape[1]
    tb = MOE_BLOCK
    nb = t // tb
    npb = n_prompt_tok // tb
    nyb = y.shape[0] // tb
    seg = functools.partial(_seg_of_tile, tile=tb, n_prompt_tok=n_prompt_tok, sample_len=sample_len)

    def yspec(e, k):
        return pl.BlockSpec((tb, d), lambda b, wb, off: (jnp.minimum(wb[b * ne + e] + k, nyb - 1), 0))

    in_specs = [yspec(e, k) for e in range(ne) for k in range(2)] + [
        pl.BlockSpec((tb, ne), lambda b, wb, off: (b, 0)),
        pl.BlockSpec((tb, ne), lambda b, wb, off: (b, 0)),
        pl.BlockSpec((tb, d), lambda b, wb, off: (b, 0)),
        pl.BlockSpec((1, 6, d), lambda b, wb, off: (seg(b), 0, 0)),
        pl.BlockSpec((1, d), lambda b, wb, off: (0, 0)),
        pl.BlockSpec((1, d), lambda b, wb, off: (0, 0)),
    ]
    return pl.pallas_call(
        functools.partial(_combine_kernel, n_experts=ne, n_prompt_blocks=npb),
        grid_spec=pltpu.PrefetchScalarGridSpec(
            num_scalar_prefetch=2,
            grid=(nb,),
            in_specs=in_specs,
            out_specs=[
                pl.BlockSpec((tb, d), lambda b, wb, off: (jnp.minimum(b, npb - 1), 0)),
                pl.BlockSpec((tb, d), lambda b, wb, off: (jnp.maximum(b - npb, 0), 0)),
            ],
        ),
        out_shape=[jax.ShapeDtypeStruct((n_prompt_tok, d), F32), jax.ShapeDtypeStruct((t - n_prompt_tok, d), F32)],
        compiler_params=_cparams("arbitrary"),
        name="moe_combine",
    )(plan["win_blk"], plan["rel_off"], *([y] * (2 * ne)), posc, combc, x, mod, ln_g.reshape(1, d), ln_b.reshape(1, d))


def _layer1(x, u, mod1, cache_k, cache_v, ln_g, ln_b, w_qkv, rpb, w_out, w_router, moe_w_in, moe_w_out,
            n_prompt, prompt_len, n_sample, sample_len):
    n_prompt_tok = n_prompt * prompt_len
    d = D_MODEL
    qkv, k_p, v_p = na_qkv(u, w_qkv.astype(BF16), n_prompt_tok)
    attn_p = context_attention(qkv, n_prompt, prompt_len)
    k_ctx = cache_k.reshape(n_sample, -1, d).astype(BF16)
    v_ctx = cache_v.reshape(n_sample, -1, d).astype(BF16)
    attn_s = neighbourhood_attention(qkv, k_ctx, v_ctx, rpb, n_prompt_tok, n_sample, sample_len)
    x1, u1 = mixer_outproj((attn_p, attn_s), x, mod1, w_out.astype(BF16), ln_g[0], ln_b[0], n_prompt_tok, sample_len)
    comb, pos, combc, posc, after = moe_router(u1, w_router)
    plan = _moe_plan(after[:, :, 0].astype(jnp.int32), x.shape[0])
    xs = moe_dispatch(u1, pos, plan)
    y = moe_grouped_ffn(xs, moe_w_in.astype(BF16), moe_w_out.astype(BF16), plan)
    y_p, y_s = moe_combine(y, posc, combc, x1, mod1, ln_g[1], ln_b[1], plan, n_prompt_tok, sample_len)
    return y_p, y_s, k_p, v_p


def kernel(x_prompt, x_sample, state_mlstm_C, state_mlstm_n, state_mlstm_m, cache_na_k, cache_na_v, c, c_ctx,
           ada_w, ada_b, ln_g, ln_b, mlstm_w_in, mlstm_b_gates, mlstm_norm_g, mlstm_w_out, na_w_qkv, na_rpb,
           na_w_out, ffn_w_in, ffn_w_out, moe_w_router, moe_w_in, moe_w_out):
    n_prompt, prompt_len, d = x_prompt.shape
    n_sample, sample_len, _ = x_sample.shape
    assert d == D_MODEL and prompt_len == MLSTM_CHUNK and sample_len % MLSTM_CHUNK == 0
    assert ada_w.shape[0] == DEPTH == 2
    mod = _modulation_tables(c, c_ctx, ada_w, ada_b)
    _, x2, u2, new_c, new_n, new_m = _layer0(
        x_prompt.reshape(-1, d), x_sample.reshape(-1, d), mod[0], mod[1], state_mlstm_C[:, 0], state_mlstm_n[:, 0], state_mlstm_m[:, 0], ln_g[0], ln_b[0],
        mlstm_w_in[0], mlstm_b_gates[0], mlstm_norm_g[0], mlstm_w_out[0], ffn_w_in[0], ffn_w_out[0],
        n_prompt, prompt_len, n_sample, sample_len)
    y_p, y_s, k_p, v_p = _layer1(
        x2, u2, mod[1], cache_na_k[:, 0], cache_na_v[:, 0], ln_g[1], ln_b[1], na_w_qkv[0], na_rpb[0], na_w_out[0],
        moe_w_router[0], moe_w_in[0], moe_w_out[0], n_prompt, prompt_len, n_sample, sample_len)
    kv_shape = (n_prompt, 1, prompt_len, NA_HEADS, NA_DH)
    return (y_p.reshape(x_prompt.shape), y_s.reshape(x_sample.shape), new_c[:, None], new_n[:, None], new_m[:, None],
            k_p.reshape(kv_shape), v_p.reshape(kv_shape))
```

```python
import functools
import math

import jax
import jax.numpy as jnp
from jax import lax
from jax.experimental import pallas as pl
from jax.experimental.pallas import tpu as pltpu

F32 = jnp.float32
BF16 = jnp.bfloat16

D_MODEL = 1024
DEPTH = 2
GRID_W = 64
M_HEADS = 4
M_DK = 128
M_DV = 256
M_HK = M_HEADS * M_DK
NA_HEADS = 16
NA_DH = 64
NA_ROWS = 8
NA_COLS = 16
D_FF = 2816
N_EXPERTS = 8
DEEPNORM_ALPHA = (2.0 * DEPTH) ** 0.25
LN_EPS = 1e-5
NEG = -1e30

VMEM_LIMIT_BYTES = 56 * 1024 * 1024

MLSTM_CHUNK = 256
TOKEN_TILE = 512
FFN_TOKEN_TILE = 512
FFN_F_TILE = 1408
NA_QROWS = 4
NA_HALO = 4


def _cparams(*sem):
    return pltpu.CompilerParams(dimension_semantics=sem, vmem_limit_bytes=VMEM_LIMIT_BYTES)


def _dot(a, b):
    return jnp.dot(a, b, preferred_element_type=F32)


def _dot_nt(a, b):
    return lax.dot_general(a, b, (((1,), (1,)), ((), ())), preferred_element_type=F32)


def _onehot(mask):
    return jnp.where(mask, 1.0, 0.0).astype(BF16)


def _split3(x):
    hi = x.astype(BF16)
    r = x - hi.astype(F32)
    mid = r.astype(BF16)
    lo = (r - mid.astype(F32)).astype(BF16)
    return hi, mid, lo


def _layer_norm(z, g, b):
    mu = jnp.mean(z, axis=-1, keepdims=True)
    zc = z - mu
    var = jnp.mean(zc * zc, axis=-1, keepdims=True)
    return zc * lax.rsqrt(var + LN_EPS) * g + b


def _log_sigmoid(x):
    return jnp.minimum(x, 0.0) - jnp.log(1.0 + jnp.exp(-jnp.abs(x)))


def _seg_of_tile(i, tile, n_prompt_tok, sample_len):
    tok = i * tile
    return jnp.where(tok < n_prompt_tok, 0, 1 + (tok - n_prompt_tok) // sample_len)


def _adaln_kernel(c_ref, w_ref, b_ref, o_ref):
    c = c_ref[...]
    a = (c * jax.nn.sigmoid(c)).astype(BF16)
    o_ref[0] = _dot(a, w_ref[0].astype(BF16)) + b_ref[0]


def adaln(cvec, ada_w, ada_b):
    depth, d, n = ada_w.shape
    tn = 1536
    return pl.pallas_call(
        _adaln_kernel,
        grid=(depth, n // tn),
        in_specs=[
            pl.BlockSpec((8, d), lambda l, j: (0, 0)),
            pl.BlockSpec((1, d, tn), lambda l, j: (l, 0, j)),
            pl.BlockSpec((1, 1, tn), lambda l, j: (l, 0, j)),
        ],
        out_specs=pl.BlockSpec((1, 8, tn), lambda l, j: (l, 0, j)),
        out_shape=jax.ShapeDtypeStruct((depth, 8, n), F32),
        compiler_params=_cparams("arbitrary", "arbitrary"),
        name="adaln",
    )(cvec, ada_w, ada_b.reshape(depth, 1, n))


def _inproj_kernel(xp_ref, xs_ref, mod_ref, w_ref, wg_ref, wgt_ref, bg_ref, bgt_ref,
                   qkv_ref, o_ref, g_ref, gt_ref, *, n_prompt_tiles):
    x = jnp.where(pl.program_id(0) < n_prompt_tiles, xp_ref[...], xs_ref[...])
    u = (x * (1.0 + mod_ref[0, 1:2, :]) + mod_ref[0, 0:1, :]).astype(BF16)
    p = _dot(u, w_ref[...])
    nqkv = qkv_ref.shape[1]
    qkv_ref[...] = p[:, :nqkv].astype(BF16)
    o_ref[...] = p[:, nqkv:]
    g = _dot(u, wg_ref[...]) + bg_ref[...]
    col = lax.broadcasted_iota(jnp.int32, g.shape, 1)
    g_ref[...] = jnp.where(((col >> 2) & 1) == 1, _log_sigmoid(g), g)
    gt = _dot_nt(wgt_ref[...], u) + bgt_ref[...]
    row = lax.broadcasted_iota(jnp.int32, gt.shape, 0)
    gt_ref[...] = jnp.where(((row >> 2) & 1) == 1, _log_sigmoid(gt), gt)


def _split_token_specs(tile, d, n_prompt_tiles):
    return [pl.BlockSpec((tile, d), lambda i: (jnp.minimum(i, n_prompt_tiles - 1), 0)),
            pl.BlockSpec((tile, d), lambda i: (jnp.maximum(i - n_prompt_tiles, 0), 0))]


def mlstm_inproj(xp, xs, mod, w_main, w_gate, b_gate, sample_len):
    n_prompt_tok, d = xp.shape
    t = n_prompt_tok + xs.shape[0]
    n_main = w_main.shape[1]
    n_qkv = 2 * M_HK + D_MODEL
    ng = w_gate.shape[1]
    tt = TOKEN_TILE
    npt = n_prompt_tok // tt
    seg = functools.partial(_seg_of_tile, tile=tt, n_prompt_tok=n_prompt_tok, sample_len=sample_len)
    return pl.pallas_call(
        functools.partial(_inproj_kernel, n_prompt_tiles=npt),
        grid=(t // tt,),
        in_specs=_split_token_specs(tt, d, npt) + [
            pl.BlockSpec((1, 6, d), lambda i: (seg(i), 0, 0)),
            pl.BlockSpec((d, n_main), lambda i: (0, 0)),
            pl.BlockSpec((d, ng), lambda i: (0, 0)),
            pl.BlockSpec((ng, d), lambda i: (0, 0)),
            pl.BlockSpec((1, ng), lambda i: (0, 0)),
            pl.BlockSpec((ng, 1), lambda i: (0, 0)),
        ],
        out_specs=[
            pl.BlockSpec((tt, n_qkv), lambda i: (i, 0)),
            pl.BlockSpec((tt, n_main - n_qkv), lambda i: (i, 0)),
            pl.BlockSpec((tt, ng), lambda i: (i, 0)),
            pl.BlockSpec((ng, tt), lambda i: (0, i)),
        ],
        out_shape=[
            jax.ShapeDtypeStruct((t, n_qkv), BF16),
            jax.ShapeDtypeStruct((t, n_main - n_qkv), F32),
            jax.ShapeDtypeStruct((t, ng), F32),
            jax.ShapeDtypeStruct((ng, t), F32),
        ],
        compiler_params=_cparams("arbitrary"),
        name="mlstm_inproj",
    )(xp, xs, mod, w_main, w_gate, w_gate.T, b_gate.reshape(1, ng), b_gate.reshape(ng, 1))


def _mlstm_gate_sums(g, gt, backward):
    L = g.shape[0]
    row = lax.broadcasted_iota(jnp.int32, (L, L), 0)
    col = lax.broadcasted_iota(jnp.int32, (L, L), 1)
    lower = col <= row
    upper = col >= row
    tri_col = _onehot(upper if backward else lower)
    tri_row = _onehot(lower if backward else upper)
    ghi, gmid, glo = _split3(g)
    b_cols = _dot(tri_col, ghi) + _dot(tri_col, gmid) + _dot(tri_col, glo)
    thi, tmid, tlo = _split3(gt)
    b_rows = _dot(thi, tri_row) + _dot(tmid, tri_row) + _dot(tlo, tri_row)
    return b_cols, b_rows


def _mlstm_heads(items):
    scale = M_DK ** -0.5
    log_scale = math.log(scale)
    L = items[0][0].shape[0]
    row = lax.broadcasted_iota(jnp.int32, (L, L), 0)
    col = lax.broadcasted_iota(jnp.int32, (L, L), 1)
    masks = {False: col <= row, True: col >= row}
    n = range(len(items))
    qh, kh, vh, li_row, lf_row, b_row, b_col, backward, state = zip(*items)
    has_state = [st is not None for st in state]
    m_prev = [st[2] if st is not None else 0.0 for st in state]

    qk = [_dot_nt(qh[i], kh[i]) for i in n]
    qc = [_dot(qh[i], state[i][0].astype(BF16)) if has_state[i] else None for i in n]
    qn = [_dot_nt(qh[i], state[i][1].astype(BF16))[:, 0:1] if has_state[i] else None for i in n]
    total = [jnp.sum(lf_row[i], axis=1, keepdims=True) for i in n]
    d = [jnp.where(masks[backward[i]], b_col[i] + (li_row[i] - b_row[i] + log_scale), NEG) for i in n]
    inter = [b_col[i] + m_prev[i] if has_state[i] else b_col[i] for i in n]
    m_t = [jnp.maximum(inter[i], jnp.max(d[i], axis=1, keepdims=True) - log_scale) for i in n]
    p = [jnp.exp(d[i] - m_t[i]) for i in n]
    s = [qk[i] * p[i] for i in n]
    den = [jnp.sum(s[i], axis=1, keepdims=True) for i in n]
    num = [_dot(s[i].astype(BF16), vh[i]) for i in n]
    a = [jnp.exp(inter[i] - m_t[i]) * scale if has_state[i] else None for i in n]
    num = [a[i] * qc[i] + num[i] if has_state[i] else num[i] for i in n]
    den = [a[i] * qn[i] + den[i] if has_state[i] else den[i] for i in n]
    hh = [num[i] / jnp.maximum(jnp.abs(den[i]), jnp.exp(-m_t[i])) for i in n]

    g_row = [total[i] - b_row[i] + li_row[i] for i in n]
    m_new = [jnp.maximum(total[i] + m_prev[i], jnp.max(g_row[i], axis=1, keepdims=True)) for i in n]
    ws_row = [jnp.exp(g_row[i] - m_new[i]) for i in n]
    kt = [kh[i].astype(F32).T for i in n]
    c_new = [_dot((kt[i] * ws_row[i]).astype(BF16), vh[i]) for i in n]
    n_new = [_dot(jnp.broadcast_to(ws_row[i], (8, L)).astype(BF16), kh[i]) for i in n]
    a0 = [jnp.exp(total[i] + m_prev[i] - m_new[i]) if has_state[i] else None for i in n]
    c_new = [a0[i] * state[i][0] + c_new[i] if has_state[i] else c_new[i] for i in n]
    n_new = [a0[i] * state[i][1] + n_new[i] if has_state[i] else n_new[i] for i in n]
    return hh, c_new, n_new, m_new


def _head_norm_gate(ht, norm_g, ogate):
    mu = jnp.mean(ht, axis=-1, keepdims=True)
    hc = ht - mu
    var = jnp.mean(hc * hc, axis=-1, keepdims=True)
    return (hc * lax.rsqrt(var + LN_EPS) * norm_g * jax.nn.sigmoid(ogate)).astype(BF16)


def _head_operands(q_ref, k_ref, v_ref, h):
    return (q_ref[:, h * M_DK:(h + 1) * M_DK], k_ref[:, h * M_DK:(h + 1) * M_DK], v_ref[:, h * M_DV:(h + 1) * M_DV])


def _gate_operands(gt, b_rows, b_cols, direction, h):
    ci = direction * 8 + h
    cf = direction * 8 + 4 + h
    return gt[ci:ci + 1, :], gt[cf:cf + 1, :], b_rows[cf:cf + 1, :], b_cols[:, cf:cf + 1]


def _mlstm_prompt_kernel(q_ref, k_ref, v_ref, g_ref, gt_ref, o_ref, ng_ref, a_ref, cf_ref, nf_ref, mf_ref):
    g, gt = g_ref[...], gt_ref[...]
    sums = [_mlstm_gate_sums(g, gt, direction == 1) for direction in (0, 1)]
    keys = [(direction, h) for direction in (0, 1) for h in range(M_HEADS)]
    items = [_head_operands(q_ref, k_ref, v_ref, h)
             + _gate_operands(gt, sums[direction][1], sums[direction][0], direction, h) + (direction == 1, None)
             for direction, h in keys]
    hh, c_new, n_new, m_new = _mlstm_heads(items)
    for i, (direction, h) in enumerate(keys):
        cf_ref[0, 0, direction, h] = c_new[i]
        nf_ref[0, 0, direction, h] = n_new[i]
        mf_ref[0, 0, direction, h] = jnp.broadcast_to(m_new[i], mf_ref.shape[-2:])
    for h in range(M_HEADS):
        sl = slice(h * M_DV, (h + 1) * M_DV)
        a_ref[:, sl] = _head_norm_gate(hh[h] + hh[M_HEADS + h], ng_ref[:, sl], o_ref[:, sl])


def _mlstm_sample_kernel(*refs, direction, n_units):
    backward = direction == 1
    refs = list(refs)
    unit_refs = [tuple(refs.pop(0) for _ in range(5)) for _ in range(n_units)]
    c0_ref, n0_ref, m0_ref = refs.pop(0), refs.pop(0), refs.pop(0)
    if backward:
        hprev_ref = refs.pop(0)
        o_refs = [refs.pop(0) for _ in range(n_units)]
        ng_ref = refs.pop(0)
    out_ref, c_scr, n_scr, m_scr = refs

    @pl.when(pl.program_id(0) == 0)
    def _():
        c_scr[...] = c0_ref[...]
        n_scr[...] = n0_ref[...]
        m_scr[...] = m0_ref[...]

    keys, items = [], []
    for u, (q_ref, k_ref, v_ref, g_ref, gt_ref) in enumerate(unit_refs):
        gt = gt_ref[...]
        b_cols, b_rows = _mlstm_gate_sums(g_ref[...], gt, backward)
        for h in range(M_HEADS):
            state = (c_scr[u, h], n_scr[u, h], m_scr[u, h][0:1, 0:1])
            keys.append((u, h))
            items.append(_head_operands(q_ref, k_ref, v_ref, h) + _gate_operands(gt, b_rows, b_cols, direction, h)
                         + (backward, state))
    hh, c_new, n_new, m_new = _mlstm_heads(items)
    for i, (u, h) in enumerate(keys):
        c_scr[u, h] = c_new[i]
        n_scr[u, h] = n_new[i]
        m_scr[u, h] = jnp.broadcast_to(m_new[i], m_scr.shape[-2:])
        sl = slice(h * M_DV, (h + 1) * M_DV)
        if backward:
            out_ref[u, :, sl] = _head_norm_gate(hprev_ref[u, :, sl] + hh[i], ng_ref[:, sl], o_refs[u][:, sl])
        else:
            out_ref[u, :, sl] = hh[i]


def _mlstm_chunk_specs(block_of):
    L = MLSTM_CHUNK
    return [
        pl.BlockSpec((L, M_HK), lambda s: (block_of(s), 0)),
        pl.BlockSpec((L, M_HK), lambda s: (block_of(s), 1)),
        pl.BlockSpec((L, D_MODEL), lambda s: (block_of(s), 1)),
        pl.BlockSpec((L, 16), lambda s: (block_of(s), 0)),
        pl.BlockSpec((16, L), lambda s: (0, block_of(s))),
    ]


def mlstm_prompt(qkv, g, gt, ogate, norm_g, n_prompt):
    L = MLSTM_CHUNK
    state_blk = lambda *tail: pl.BlockSpec((1, 1, 2, M_HEADS) + tail, lambda s: (s, 0, 0, 0, 0, 0))
    return pl.pallas_call(
        _mlstm_prompt_kernel,
        grid=(n_prompt,),
        in_specs=_mlstm_chunk_specs(lambda s: s) + [
            pl.BlockSpec((L, D_MODEL), lambda s: (s, 0)),
            pl.BlockSpec((1, D_MODEL), lambda s: (0, 0)),
        ],
        out_specs=[pl.BlockSpec((L, D_MODEL), lambda s: (s, 0)),
                   state_blk(M_DK, M_DV), state_blk(8, M_DK), state_blk(8, 128)],
        out_shape=[
            jax.ShapeDtypeStruct((n_prompt * L, D_MODEL), BF16),
            jax.ShapeDtypeStruct((n_prompt, 1, 2, M_HEADS, M_DK, M_DV), F32),
            jax.ShapeDtypeStruct((n_prompt, 1, 2, M_HEADS, 8, M_DK), F32),
            jax.ShapeDtypeStruct((n_prompt, 1, 2, M_HEADS, 8, 128), F32),
        ],
        compiler_params=_cparams("arbitrary"),
        name="mlstm_prompt",
    )(qkv, qkv, qkv, g, gt, ogate, norm_g.reshape(1, D_MODEL))


def mlstm_sample(direction, qkv, g, gt, c0, n0, m0, first_block, chunks_per_sample,
                 hprev=None, ogate=None, norm_g=None):
    L = MLSTM_CHUNK
    backward = direction == 1
    cps = chunks_per_sample
    n_units = c0.shape[0]
    pos = (lambda s: cps - 1 - s) if backward else (lambda s: s)
    unit_block = [functools.partial(lambda s, u: first_block + u * cps + pos(s), u=u) for u in range(n_units)]
    whole = lambda a: pl.BlockSpec(a.shape, lambda s: (0,) * a.ndim)
    in_specs, args = [], []
    for u in range(n_units):
        in_specs += _mlstm_chunk_specs(unit_block[u])
        args += [qkv, qkv, qkv, g, gt]
    in_specs += [whole(c0), whole(n0), whole(m0)]
    args += [c0, n0, m0]
    if backward:
        in_specs += [pl.BlockSpec((n_units, L, D_MODEL), lambda s: (0, pos(s), 0))]
        in_specs += [pl.BlockSpec((L, D_MODEL), functools.partial(lambda s, u: (unit_block[u](s), 0), u=u))
                     for u in range(n_units)]
        in_specs += [pl.BlockSpec((1, D_MODEL), lambda s: (0, 0))]
        args += [hprev] + [ogate] * n_units + [norm_g.reshape(1, D_MODEL)]
    return pl.pallas_call(
        functools.partial(_mlstm_sample_kernel, direction=direction, n_units=n_units),
        grid=(cps,),
        in_specs=in_specs,
        out_specs=pl.BlockSpec((n_units, L, D_MODEL), lambda s: (0, pos(s), 0)),
        out_shape=jax.ShapeDtypeStruct((n_units, cps * L, D_MODEL), BF16 if backward else F32),
        scratch_shapes=[pltpu.VMEM(c0.shape, F32), pltpu.VMEM(n0.shape, F32), pltpu.VMEM(m0.shape, F32)],
        compiler_params=_cparams("arbitrary"),
        name="mlstm_sample_bwd" if backward else "mlstm_sample_fwd",
    )(*args)


def _outproj_kernel(*refs, n_prompt_tiles, a_split, x_split):
    refs = list(refs)
    is_prompt = pl.program_id(0) < n_prompt_tiles

    def take(split):
        if split:
            p_ref, s_ref = refs.pop(0), refs.pop(0)
            return jnp.where(is_prompt, p_ref[...], s_ref[...])
        return refs.pop(0)[...]

    a = take(a_split)
    x = take(x_split)
    mod_ref, w_ref, lng_ref, lnb_ref, xo_ref, uo_ref = refs
    y = _dot(a, w_ref[...])
    z = DEEPNORM_ALPHA * x + mod_ref[0, 2:3, :] * y
    xn = _layer_norm(z, lng_ref[...], lnb_ref[...])
    xo_ref[...] = xn
    uo_ref[...] = (xn * (1.0 + mod_ref[0, 4:5, :]) + mod_ref[0, 3:4, :]).astype(BF16)


def mixer_outproj(a, x, mod, w, ln_g, ln_b, n_prompt_tok, sample_len):
    d = w.shape[0]
    tt = TOKEN_TILE
    npt = n_prompt_tok // tt
    seg = functools.partial(_seg_of_tile, tile=tt, n_prompt_tok=n_prompt_tok, sample_len=sample_len)

    def specs(v):
        if isinstance(v, tuple):
            return _split_token_specs(tt, d, npt), list(v), v[0].shape[0] + v[1].shape[0]
        return [pl.BlockSpec((tt, d), lambda i: (i, 0))], [v], v.shape[0]

    a_specs, a_args, t = specs(a)
    x_specs, x_args, _ = specs(x)
    return pl.pallas_call(
        functools.partial(_outproj_kernel, n_prompt_tiles=npt, a_split=isinstance(a, tuple),
                          x_split=isinstance(x, tuple)),
        grid=(t // tt,),
        in_specs=a_specs + x_specs + [
            pl.BlockSpec((1, 6, d), lambda i: (seg(i), 0, 0)),
            pl.BlockSpec((d, d), lambda i: (0, 0)),
            pl.BlockSpec((1, d), lambda i: (0, 0)),
            pl.BlockSpec((1, d), lambda i: (0, 0)),
        ],
        out_specs=[pl.BlockSpec((tt, d), lambda i: (i, 0)), pl.BlockSpec((tt, d), lambda i: (i, 0))],
        out_shape=[jax.ShapeDtypeStruct((t, d), F32), jax.ShapeDtypeStruct((t, d), BF16)],
        compiler_params=_cparams("arbitrary"),
        name="mixer_outproj",
    )(*a_args, *x_args, mod, w, ln_g.reshape(1, d), ln_b.reshape(1, d))


def _ffn_kernel(u_ref, x_ref, mod_ref, modn_ref, wg_ref, wu_ref, wo_ref, lng_ref, lnb_ref,
                xo_ref, uo_ref, acc_ref):
    f = pl.program_id(1)

    @pl.when(f == 0)
    def _():
        acc_ref[...] = jnp.zeros_like(acc_ref)

    u = u_ref[...]
    gp = _dot(u, wg_ref[...])
    up = _dot(u, wu_ref[...])
    act = (gp * jax.nn.sigmoid(gp) * up).astype(BF16)
    acc_ref[...] += _dot(act, wo_ref[...])

    @pl.when(f == pl.num_programs(1) - 1)
    def _():
        z = DEEPNORM_ALPHA * x_ref[...] + mod_ref[0, 5:6, :] * acc_ref[...]
        xn = _layer_norm(z, lng_ref[...], lnb_ref[...])
        xo_ref[...] = xn
        uo_ref[...] = (xn * (1.0 + modn_ref[0, 1:2, :]) + modn_ref[0, 0:1, :]).astype(BF16)


def dense_ffn(u, x, mod, mod_next, w_in, w_out, ln_g, ln_b, n_prompt_tok, sample_len):
    t, d = x.shape
    dff = w_out.shape[0]
    tm, tf = FFN_TOKEN_TILE, FFN_F_TILE
    nf = dff // tf
    seg = functools.partial(_seg_of_tile, tile=tm, n_prompt_tok=n_prompt_tok, sample_len=sample_len)
    return pl.pallas_call(
        _ffn_kernel,
        grid=(t // tm, nf),
        in_specs=[
            pl.BlockSpec((tm, d), lambda i, f: (i, 0)),
            pl.BlockSpec((tm, d), lambda i, f: (i, 0)),
            pl.BlockSpec((1, 6, d), lambda i, f: (seg(i), 0, 0)),
            pl.BlockSpec((1, 6, d), lambda i, f: (seg(i), 0, 0)),
            pl.BlockSpec((d, tf), lambda i, f: (0, f)),
            pl.BlockSpec((d, tf), lambda i, f: (0, nf + f)),
            pl.BlockSpec((tf, d), lambda i, f: (f, 0)),
            pl.BlockSpec((1, d), lambda i, f: (0, 0)),
            pl.BlockSpec((1, d), lambda i, f: (0, 0)),
        ],
        out_specs=[pl.BlockSpec((tm, d), lambda i, f: (i, 0)), pl.BlockSpec((tm, d), lambda i, f: (i, 0))],
        out_shape=[jax.ShapeDtypeStruct((t, d), F32), jax.ShapeDtypeStruct((t, d), BF16)],
        scratch_shapes=[pltpu.VMEM((tm, d), F32)],
        compiler_params=_cparams("arbitrary", "arbitrary"),
        name="dense_ffn",
    )(u, x, mod, mod_next, w_in, w_in, w_out, ln_g.reshape(1, d), ln_b.reshape(1, d))


def _modulation_tables(c, c_ctx, ada_w, ada_b):
    n_s = c.shape[0]
    cvec = jnp.concatenate([c_ctx[None, :], c, jnp.zeros((8 - 1 - n_s, c.shape[1]), F32)], axis=0)
    mod = adaln(cvec, ada_w, ada_b)
    return mod[:, :1 + n_s, :].reshape(ada_w.shape[0], 1 + n_s, 6, c.shape[1])


def _layer0(xp, xs, mod0, mod1, state_c, state_n, state_m, ln_g, ln_b, w_in, b_gates, norm_g, w_out,
            ffn_w_in, ffn_w_out, n_prompt, prompt_len, n_sample, sample_len):
    n_prompt_tok = n_prompt * prompt_len
    n_main = 2 * M_HK + 2 * D_MODEL
    qkv, ogate, g, gt = mlstm_inproj(xp, xs, mod0, w_in[:, :n_main].astype(BF16), w_in[:, n_main:].astype(BF16),
                                     b_gates.reshape(-1), sample_len)
    npc = n_prompt_tok // MLSTM_CHUNK
    cps = sample_len // MLSTM_CHUNK
    rep = lambda a: jnp.broadcast_to(a[..., None, :], a.shape[:-1] + (8, a.shape[-1]))
    n0 = rep(state_n)
    m0 = jnp.broadcast_to(state_m[..., None, None], state_m.shape + (8, 128))
    a_p, new_c, nf, mf = mlstm_prompt(qkv, g, gt, ogate, norm_g, n_prompt)
    hf = mlstm_sample(0, qkv, g, gt, state_c[:, 0], n0[:, 0], m0[:, 0], npc, cps)
    a_s = mlstm_sample(1, qkv, g, gt, state_c[:, 1], n0[:, 1], m0[:, 1], npc, cps,
                       hprev=hf, ogate=ogate, norm_g=norm_g)
    x1, u1 = mixer_outproj((a_p, a_s.reshape(-1, D_MODEL)), (xp, xs), mod0, w_out.astype(BF16), ln_g[0], ln_b[0],
                           n_prompt_tok, sample_len)
    x2, u2 = dense_ffn(u1, x1, mod0, mod1, ffn_w_in.astype(BF16), ffn_w_out.astype(BF16), ln_g[1], ln_b[1],
                       n_prompt_tok, sample_len)
    return x1, x2, u2, new_c, nf[..., 0, :], mf[..., 0, 0]


def _qkv_kernel(u_ref, w_ref, qkv_ref, k_ref, v_ref, *, n_prompt_tiles):
    p = _dot(u_ref[...], w_ref[...])
    qkv_ref[...] = p.astype(BF16)
    d = k_ref.shape[1]

    @pl.when(pl.program_id(0) < n_prompt_tiles)
    def _():
        k_ref[...] = p[:, d:2 * d]
        v_ref[...] = p[:, 2 * d:]


def na_qkv(u, w, n_prompt_tok):
    t, d = u.shape
    tt = TOKEN_TILE
    npt = n_prompt_tok // tt
    return pl.pallas_call(
        functools.partial(_qkv_kernel, n_prompt_tiles=npt),
        grid=(t // tt,),
        in_specs=[pl.BlockSpec((tt, d), lambda i: (i, 0)), pl.BlockSpec((d, 3 * d), lambda i: (0, 0))],
        out_specs=[
            pl.BlockSpec((tt, 3 * d), lambda i: (i, 0)),
            pl.BlockSpec((tt, d), lambda i: (jnp.minimum(i, npt - 1), 0)),
            pl.BlockSpec((tt, d), lambda i: (jnp.minimum(i, npt - 1), 0)),
        ],
        out_shape=[
            jax.ShapeDtypeStruct((t, 3 * d), BF16),
            jax.ShapeDtypeStruct((n_prompt_tok, d), F32),
            jax.ShapeDtypeStruct((n_prompt_tok, d), F32),
        ],
        compiler_params=_cparams("arbitrary"),
        name="na_qkv",
    )(u, w)


def _head_pair_masks(scale):
    assert math.frexp(scale)[0] == 0.5
    lane = lax.broadcasted_iota(jnp.int32, (1, 2 * NA_DH), 1)
    first, second = lane < NA_DH, lane >= NA_DH
    return tuple((jnp.where(sel, scale, 0.0).astype(BF16), _onehot(sel)) for sel in (first, second))


def _softmax_pv(score_blocks, value_blocks):
    mx = None
    for s in score_blocks:
        bm = jnp.max(s, axis=1, keepdims=True)
        mx = bm if mx is None else jnp.maximum(mx, bm)
    acc, denom = None, None
    for s, v in zip(score_blocks, value_blocks):
        p = jnp.exp(s - mx)
        ps = jnp.sum(p, axis=1, keepdims=True)
        pv = _dot(p.astype(BF16), v)
        acc = pv if acc is None else acc + pv
        denom = ps if denom is None else denom + ps
    return acc / denom


def _ctx_attn_kernel(q_ref, k_ref, v_ref, o_ref):
    masks = _head_pair_masks(NA_DH ** -0.5)
    for hp in range(NA_HEADS // 2):
        sl = slice(hp * 2 * NA_DH, (hp + 1) * 2 * NA_DH)
        q2, k2, v2 = q_ref[:, sl], k_ref[:, sl], v_ref[:, sl]
        o2 = None
        for qmsk, vmsk in masks:
            s = _dot_nt(q2 * qmsk, k2)
            o = _softmax_pv([s], [v2 * vmsk])
            o2 = o if o2 is None else o2 + o
        o_ref[:, sl] = o2.astype(o_ref.dtype)


def context_attention(qkv, n_prompt, prompt_len):
    t = n_prompt * prompt_len
    d = D_MODEL
    return pl.pallas_call(
        _ctx_attn_kernel,
        grid=(n_prompt,),
        in_specs=[
            pl.BlockSpec((prompt_len, d), lambda b: (b, 0)),
            pl.BlockSpec((prompt_len, d), lambda b: (b, 1)),
            pl.BlockSpec((prompt_len, d), lambda b: (b, 2)),
        ],
        out_specs=pl.BlockSpec((prompt_len, d), lambda b: (b, 0)),
        out_shape=jax.ShapeDtypeStruct((t, d), BF16),
        compiler_params=_cparams("arbitrary"),
        name="context_attention",
    )(qkv, qkv, qkv)


def _na_kernel(tile_ref, q_ref, kp_ref, kc_ref, kn_ref, vp_ref, vc_ref, vn_ref, kctx_ref, vctx_ref, bias_ref,
               o_ref):
    j = pl.program_id(1)
    nblk = pl.num_programs(1)
    m = q_ref.shape[0]
    halo = NA_HALO * GRID_W
    n_pairs = (NA_QROWS + 2 * NA_HALO) // 2
    variant = jnp.where(j == 0, 0, jnp.where(j == nblk - 1, 2, 1))
    kinds = [[tile_ref[(variant * NA_QROWS + rq) * n_pairs + kp] for kp in range(n_pairs)]
             for rq in range(NA_QROWS)]

    def bias_of(head):
        return jnp.concatenate(
            [jnp.concatenate([bias_ref[kinds[rq][kp], head] for kp in range(n_pairs)], axis=1)
             for rq in range(NA_QROWS)], axis=0)

    masks = _head_pair_masks(NA_DH ** -0.5)
    for hp in range(NA_HEADS // 2):
        sl = slice(hp * 2 * NA_DH, (hp + 1) * 2 * NA_DH)
        q2 = q_ref[:, sl]
        kloc = jnp.concatenate([kp_ref[m - halo:, sl], kc_ref[:, sl], kn_ref[:halo, sl]], axis=0)
        vloc = jnp.concatenate([vp_ref[m - halo:, sl], vc_ref[:, sl], vn_ref[:halo, sl]], axis=0)
        kctx = kctx_ref[0][:, sl]
        vctx = vctx_ref[0][:, sl]
        o2 = None
        for half, (qmsk, vmsk) in enumerate(masks):
            qm = q2 * qmsk
            s_loc = _dot_nt(qm, kloc) + bias_of(2 * hp + half)
            s_ctx = _dot_nt(qm, kctx)
            o = _softmax_pv([s_loc, s_ctx], [vloc * vmsk, vctx * vmsk])
            o2 = o if o2 is None else o2 + o
        o_ref[:, sl] = o2.astype(o_ref.dtype)


def _na_bias_kernel(lc_ref, rc_ref, lv_ref, rv_ref, o_ref):
    nc, npos = lc_ref.shape[2], o_ref.shape[2]
    pair_shift = (2 * GRID_W).bit_length() - 1
    c = lax.broadcasted_iota(jnp.int32, (nc, npos), 0)
    pos = lax.broadcasted_iota(jnp.int32, (nc, npos), 1)
    qc = pos >> pair_shift
    kc = pos & (GRID_W - 1)
    c_start = jnp.clip(qc - NA_COLS // 2, 0, GRID_W - NA_COLS)
    col_in = jnp.logical_and(kc >= c_start, kc < c_start + NA_COLS)
    dc = jnp.clip(kc - qc + NA_COLS - 1, 0, 2 * NA_COLS - 2)
    sel = _onehot(jnp.logical_and(c == dc, col_in))

    def pick(row_ref):
        r1, r2, r3 = _split3(row_ref[0])
        return _dot(r1, sel) + _dot(r2, sel) + _dot(r3, sel)

    right = (pos[0:1, :] & GRID_W) != 0
    val = jnp.where(right, pick(rc_ref), pick(lc_ref))
    visible = jnp.where(right, rv_ref[0], lv_ref[0]) > 0.0
    o_ref[0] = jnp.where(jnp.logical_and(visible, col_in[0:1, :]), val, NEG)


def _na_bias_plan():
    ndr = 2 * NA_ROWS - 1
    assert NA_QROWS >= NA_ROWS // 2 and NA_HALO == NA_ROWS // 2 and (NA_QROWS + 2 * NA_HALO) % 2 == 0
    ok = lambda dr: dr if dr is not None and 0 <= dr < ndr else None
    kinds = [(ok(dr), ok(dr + 1)) for dr in range(-1, ndr)]
    first_key_row = (lambda rq: NA_HALO, lambda rq: rq, lambda rq: NA_QROWS + NA_HALO - NA_ROWS)
    table = []
    for first in first_key_row:
        for rq in range(NA_QROWS):
            for kp in range((NA_QROWS + 2 * NA_HALO) // 2):
                drs = []
                for kk in (2 * kp, 2 * kp + 1):
                    visible = first(rq) <= kk < first(rq) + NA_ROWS
                    drs.append(kk - rq + NA_ROWS - 1 - NA_HALO if visible else None)
                kind = (drs[0], drs[1])
                if kind not in kinds:
                    kinds.append(kind)
                table.append(kinds.index(kind))
    return kinds, table


def _na_bias_tiles(rpb):
    nh, ndr, ndc = rpb.shape
    nc = 32
    kinds, table = _na_bias_plan()
    nk = len(kinds)
    rpb_p = jnp.pad(rpb, ((0, 0), (0, 0), (0, nc - ndc)))
    zero = jnp.zeros((nh, nc), F32)
    lc = jnp.stack([zero if l is None else rpb_p[:, l] for l, _ in kinds])
    rc = jnp.stack([zero if r is None else rpb_p[:, r] for _, r in kinds])
    vis = lambda flags: jnp.broadcast_to(jnp.asarray(flags, F32)[:, None, None], (nk, nh, 1))
    lv = vis([0.0 if l is None else 1.0 for l, _ in kinds])
    rv = vis([0.0 if r is None else 1.0 for _, r in kinds])
    npos = GRID_W * 2 * GRID_W
    row_spec = pl.BlockSpec((1, nh, nc), lambda t: (t, 0, 0))
    flag_spec = pl.BlockSpec((1, nh, 1), lambda t: (t, 0, 0))
    tiles = pl.pallas_call(
        _na_bias_kernel,
        grid=(nk,),
        in_specs=[row_spec, row_spec, flag_spec, flag_spec],
        out_specs=pl.BlockSpec((1, nh, npos), lambda t: (t, 0, 0)),
        out_shape=jax.ShapeDtypeStruct((nk, nh, npos), F32),
        compiler_params=_cparams("arbitrary"),
        name="na_bias",
    )(lc, rc, lv, rv)
    return tiles.reshape(nk, nh, GRID_W, 2 * GRID_W), jnp.asarray(table, jnp.int32)


def neighbourhood_attention(qkv, k_ctx, v_ctx, rpb, n_prompt_tok, n_sample, sample_len):
    d = D_MODEL
    m = NA_QROWS * GRID_W
    nblk = sample_len // m
    assert nblk >= 3
    base = n_prompt_tok // m
    bias, table = _na_bias_tiles(rpb)

    def blk(col, off):
        return pl.BlockSpec((m, d), lambda b, j, tbl: (base + b * nblk + jnp.clip(j + off, 0, nblk - 1), col))

    return pl.pallas_call(
        _na_kernel,
        grid_spec=pltpu.PrefetchScalarGridSpec(
            num_scalar_prefetch=1,
            grid=(n_sample, nblk),
            in_specs=[
                blk(0, 0), blk(1, -1), blk(1, 0), blk(1, 1), blk(2, -1), blk(2, 0), blk(2, 1),
                pl.BlockSpec((1,) + k_ctx.shape[1:], lambda b, j, tbl: (b, 0, 0)),
                pl.BlockSpec((1,) + v_ctx.shape[1:], lambda b, j, tbl: (b, 0, 0)),
                pl.BlockSpec(bias.shape, lambda b, j, tbl: (0, 0, 0, 0)),
            ],
            out_specs=pl.BlockSpec((m, d), lambda b, j, tbl: (b * nblk + j, 0)),
        ),
        out_shape=jax.ShapeDtypeStruct((n_sample * sample_len, d), BF16),
        compiler_params=_cparams("arbitrary", "arbitrary"),
        name="neighbourhood_attention",
    )(table, qkv, qkv, qkv, qkv, qkv, qkv, qkv, k_ctx, v_ctx, bias)


MOE_BLOCK = 256
MOE_ROW_TILE = 512
MOE_F_TILE = 1408


def _router_kernel(u_ref, wt_ref, comb_ref, pos_ref, combc_ref, posc_ref, after_ref, carry_scr):
    i = pl.program_id(0)

    @pl.when(i == 0)
    def _():
        carry_scr[...] = jnp.zeros_like(carry_scr)

    u = u_ref[...]
    w1, w2, w3 = _split3(wt_ref[...])
    lg = _dot_nt(w1, u) + _dot_nt(w2, u) + _dot_nt(w3, u)
    ne, nt = lg.shape
    e = lax.broadcasted_iota(jnp.int32, lg.shape, 0)
    m1 = jnp.max(lg, axis=0, keepdims=True)
    i1 = jnp.min(jnp.where(lg == m1, e, ne), axis=0, keepdims=True)
    sel1 = e == i1
    lg2 = jnp.where(sel1, -jnp.inf, lg)
    m2 = jnp.max(lg2, axis=0, keepdims=True)
    i2 = jnp.min(jnp.where(lg2 == m2, e, ne), axis=0, keepdims=True)
    sel2 = e == i2
    ex = jnp.exp(m2 - m1)
    wa = 1.0 / (1.0 + ex)
    wb = ex / (1.0 + ex)
    comb = jnp.where(sel1, wa, jnp.where(sel2, wb, 0.0))
    assign = jnp.logical_or(sel1, sel2)
    a = jnp.where(assign, 1.0, 0.0)
    row = lax.broadcasted_iota(jnp.int32, (nt, nt), 0)
    col = lax.broadcasted_iota(jnp.int32, (nt, nt), 1)
    tri = _onehot(row < col)
    carry = carry_scr[:, 0:1]
    rank = _dot(a.astype(BF16), tri) + carry
    pos = jnp.where(assign, rank, -1.0)
    comb_ref[...] = comb
    pos_ref[...] = pos
    combc_ref[...] = comb.T
    posc_ref[...] = pos.T
    carry_new = carry + jnp.sum(a, axis=1, keepdims=True)
    carry_scr[...] = jnp.broadcast_to(carry_new, carry_scr.shape)
    after_ref[0] = jnp.broadcast_to(carry_new, carry_scr.shape)


def moe_router(u, w_router):
    t, d = u.shape
    ne = w_router.shape[1]
    tb = MOE_BLOCK
    nb = t // tb
    return pl.pallas_call(
        _router_kernel,
        grid=(nb,),
        in_specs=[pl.BlockSpec((tb, d), lambda i: (i, 0)), pl.BlockSpec((ne, d), lambda i: (0, 0))],
        out_specs=[
            pl.BlockSpec((ne, tb), lambda i: (0, i)),
            pl.BlockSpec((ne, tb), lambda i: (0, i)),
            pl.BlockSpec((tb, ne), lambda i: (i, 0)),
            pl.BlockSpec((tb, ne), lambda i: (i, 0)),
            pl.BlockSpec((1, ne, 128), lambda i: (i, 0, 0)),
        ],
        out_shape=[
            jax.ShapeDtypeStruct((ne, t), F32),
            jax.ShapeDtypeStruct((ne, t), F32),
            jax.ShapeDtypeStruct((t, ne), F32),
            jax.ShapeDtypeStruct((t, ne), F32),
            jax.ShapeDtypeStruct((nb, ne, 128), F32),
        ],
        scratch_shapes=[pltpu.VMEM((ne, 128), F32)],
        compiler_params=_cparams("arbitrary"),
        name="moe_router",
    )(u, w_router.T)


def _moe_plan(after, n_tok):
    nb, ne = after.shape
    tm, tb = MOE_ROW_TILE, MOE_BLOCK
    n_tiles = (2 * n_tok) // tm + ne
    before = jnp.concatenate([jnp.zeros((1, ne), jnp.int32), after[:-1]], axis=0)
    totals = after[-1]
    tiles_per = (totals + tm - 1) // tm
    tile_end = jnp.cumsum(tiles_per)
    tile_base = tile_end - tiles_per
    n_used = tile_end[-1]
    base_rows = tile_base * tm
    ti = jnp.arange(n_tiles, dtype=jnp.int32)
    tile_expert = jnp.minimum(jnp.sum((ti[:, None] >= tile_end[None, :]).astype(jnp.int32), axis=1), ne - 1)
    last_expert = tile_expert[jnp.maximum(n_used - 1, 0)]
    tile_expert = jnp.where(ti < n_used, tile_expert, last_expert)
    per = tm // tb
    si = jnp.arange(n_tiles * per, dtype=jnp.int32)
    sub_expert = tile_expert[si // per]
    sub_r0 = (si - tile_base[sub_expert] * per) * tb
    sub_valid = jnp.logical_and(si // per < n_used, sub_r0 < totals[sub_expert])
    aft_e = after.T[sub_expert]
    bef_e = before.T[sub_expert]
    lo = jnp.sum((aft_e <= sub_r0[:, None]).astype(jnp.int32), axis=1)
    hi = jnp.sum((bef_e < (sub_r0 + tb)[:, None]).astype(jnp.int32), axis=1) - 1
    hi = jnp.minimum(hi, nb - 1)
    lo = jnp.where(sub_valid, lo, 1)
    hi = jnp.where(sub_valid, hi, 0)
    first_row = base_rows[None, :] + before
    win_blk = first_row // tb
    rel_off = base_rows[None, :] - win_blk * tb
    return dict(n_tiles=n_tiles, tile_expert=tile_expert.astype(jnp.int32), n_used=n_used.reshape(1).astype(jnp.int32),
                sub_expert=sub_expert.astype(jnp.int32), sub_r0=sub_r0.astype(jnp.int32),
                sub_lo=lo.astype(jnp.int32), sub_hi=hi.astype(jnp.int32),
                win_blk=win_blk.reshape(-1).astype(jnp.int32), rel_off=rel_off.reshape(-1).astype(jnp.int32))


def _dispatch_kernel(se_ref, r0_ref, lo_ref, hi_ref, pos_ref, u_ref, xs_ref, acc_ref):
    i = pl.program_id(0)
    e = se_ref[i]
    tb = xs_ref.shape[0]
    target = (r0_ref[i] + lax.broadcasted_iota(jnp.int32, (tb, tb), 0)).astype(F32)
    acc_ref[...] = jnp.zeros_like(acc_ref)

    def body(sb, carry):
        off = pl.multiple_of(sb * tb, tb)
        p = pos_ref[pl.ds(e, 1), pl.ds(off, tb)]
        acc_ref[...] += _dot(_onehot(p == target), u_ref[pl.ds(off, tb), :])
        return carry

    lax.fori_loop(lo_ref[i], hi_ref[i] + 1, body, 0)
    xs_ref[...] = acc_ref[...].astype(xs_ref.dtype)


def moe_dispatch(u, pos, plan):
    t, d = u.shape
    tb = MOE_BLOCK
    n_sub = plan["sub_expert"].shape[0]
    return pl.pallas_call(
        _dispatch_kernel,
        grid_spec=pltpu.PrefetchScalarGridSpec(
            num_scalar_prefetch=4,
            grid=(n_sub,),
            in_specs=[pl.BlockSpec(memory_space=pltpu.VMEM), pl.BlockSpec(memory_space=pltpu.VMEM)],
            out_specs=pl.BlockSpec((tb, d), lambda i, *_: (i, 0)),
            scratch_shapes=[pltpu.VMEM((tb, d), F32)],
        ),
        out_shape=jax.ShapeDtypeStruct((n_sub * tb, d), BF16),
        compiler_params=_cparams("arbitrary"),
        name="moe_dispatch",
    )(plan["sub_expert"], plan["sub_r0"], plan["sub_lo"], plan["sub_hi"], pos, u)


def _gffn_kernel(te_ref, nu_ref, x_ref, wg_ref, wu_ref, wo_ref, y_ref, acc_ref):
    i = pl.program_id(0)
    f = pl.program_id(1)
    used = i < nu_ref[0]

    @pl.when(jnp.logical_and(used, f == 0))
    def _():
        acc_ref[...] = jnp.zeros_like(acc_ref)

    @pl.when(used)
    def _():
        x = x_ref[...]
        gp = _dot(x, wg_ref[0])
        up = _dot(x, wu_ref[0])
        act = (gp * jax.nn.sigmoid(gp) * up).astype(BF16)
        acc_ref[...] += _dot(act, wo_ref[0])

    @pl.when(f == pl.num_programs(1) - 1)
    def _():
        y_ref[...] = jnp.where(used, acc_ref[...], 0.0).astype(y_ref.dtype)


def moe_grouped_ffn(xs, w_in, w_out, plan):
    nr, d = xs.shape
    ne, dff, _ = w_out.shape
    tm, tf = MOE_ROW_TILE, MOE_F_TILE
    nf = dff // tf
    n_tiles = nr // tm

    def fsel(i, f, nu):
        return jnp.where(i < nu[0], f, nf - 1)

    return pl.pallas_call(
        _gffn_kernel,
        grid_spec=pltpu.PrefetchScalarGridSpec(
            num_scalar_prefetch=2,
            grid=(n_tiles, nf),
            in_specs=[
                pl.BlockSpec((tm, d), lambda i, f, te, nu: (jnp.minimum(i, nu[0] - 1), 0)),
                pl.BlockSpec((1, d, tf), lambda i, f, te, nu: (te[i], 0, fsel(i, f, nu))),
                pl.BlockSpec((1, d, tf), lambda i, f, te, nu: (te[i], 0, nf + fsel(i, f, nu))),
                pl.BlockSpec((1, tf, d), lambda i, f, te, nu: (te[i], fsel(i, f, nu), 0)),
            ],
            out_specs=pl.BlockSpec((tm, d), lambda i, f, te, nu: (i, 0)),
            scratch_shapes=[pltpu.VMEM((tm, d), F32)],
        ),
        out_shape=jax.ShapeDtypeStruct((nr, d), BF16),
        compiler_params=_cparams("arbitrary", "arbitrary"),
        name="moe_grouped_ffn",
    )(plan["tile_expert"], plan["n_used"], xs, w_in, w_in, w_out)


def _combine_kernel(wb_ref, off_ref, *refs, n_experts, n_prompt_blocks):
    y_refs = refs[:2 * n_experts]
    posc_ref, combc_ref, x_ref, mod_ref, lng_ref, lnb_ref, op_ref, os_ref = refs[2 * n_experts:]
    b = pl.program_id(0)
    tb = x_ref.shape[0]
    lane = lax.broadcasted_iota(jnp.int32, (tb, tb), 1).astype(F32)
    posc = posc_ref[...]
    combc = combc_ref[...]
    moe = jnp.zeros(x_ref.shape, F32)
    for e in range(n_experts):
        pe = posc[:, e:e + 1]
        rel = pe + off_ref[b * n_experts + e].astype(F32)
        hit0 = jnp.logical_and(pe >= 0.0, rel == lane)
        hit1 = jnp.logical_and(pe >= 0.0, rel == lane + float(tb))
        ge = _dot(_onehot(hit0), y_refs[2 * e][...]) + _dot(_onehot(hit1), y_refs[2 * e + 1][...])
        moe = moe + combc[:, e:e + 1] * ge
    z = DEEPNORM_ALPHA * x_ref[...] + mod_ref[0, 5:6, :] * moe
    xn = _layer_norm(z, lng_ref[...], lnb_ref[...])

    @pl.when(b < n_prompt_blocks)
    def _():
        op_ref[...] = xn

    @pl.when(b >= n_prompt_blocks)
    def _():
        os_ref[...] = xn


def moe_combine(y, posc, combc, x, mod, ln_g, ln_b, plan, n_prompt_tok, sample_len):
    t, d = x.shape
    ne = posc.shape[1]
    tb = MOE_BLOCK
    nb = t // tb
    npb = n_prompt_tok // tb
    nyb = y.shape[0] // tb
    seg = functools.partial(_seg_of_tile, tile=tb, n_prompt_tok=n_prompt_tok, sample_len=sample_len)

    def yspec(e, k):
        return pl.BlockSpec((tb, d), lambda b, wb, off: (jnp.minimum(wb[b * ne + e] + k, nyb - 1), 0))

    in_specs = [yspec(e, k) for e in range(ne) for k in range(2)] + [
        pl.BlockSpec((tb, ne), lambda b, wb, off: (b, 0)),
        pl.BlockSpec((tb, ne), lambda b, wb, off: (b, 0)),
        pl.BlockSpec((tb, d), lambda b, wb, off: (b, 0)),
        pl.BlockSpec((1, 6, d), lambda b, wb, off: (seg(b), 0, 0)),
        pl.BlockSpec((1, d), lambda b, wb, off: (0, 0)),
        pl.BlockSpec((1, d), lambda b, wb, off: (0, 0)),
    ]
    return pl.pallas_call(
        functools.partial(_combine_kernel, n_experts=ne, n_prompt_blocks=npb),
        grid_spec=pltpu.PrefetchScalarGridSpec(
            num_scalar_prefetch=2,
            grid=(nb,),
            in_specs=in_specs,
            out_specs=[
                pl.BlockSpec((tb, d), lambda b, wb, off: (jnp.minimum(b, npb - 1), 0)),
                pl.BlockSpec((tb, d), lambda b, wb, off: (jnp.maximum(b - npb, 0), 0)),
            ],
        ),
        out_shape=[jax.ShapeDtypeStruct((n_prompt_tok, d), F32), jax.ShapeDtypeStruct((t - n_prompt_tok, d), F32)],
        compiler_params=_cparams("arbitrary"),
        name="moe_combine",
    )(plan["win_blk"], plan["rel_off"], *([y] * (2 * ne)), posc, combc, x, mod, ln_g.reshape(1, d), ln_b.reshape(1, d))


def _layer1(x, u, mod1, cache_k, cache_v, ln_g, ln_b, w_qkv, rpb, w_out, w_router, moe_w_in, moe_w_out,
            n_prompt, prompt_len, n_sample, sample_len):
    n_prompt_tok = n_prompt * prompt_len
    d = D_MODEL
    qkv, k_p, v_p = na_qkv(u, w_qkv.astype(BF16), n_prompt_tok)
    attn_p = context_attention(qkv, n_prompt, prompt_len)
    k_ctx = cache_k.reshape(n_sample, -1, d).astype(BF16)
    v_ctx = cache_v.reshape(n_sample, -1, d).astype(BF16)
    attn_s = neighbourhood_attention(qkv, k_ctx, v_ctx, rpb, n_prompt_tok, n_sample, sample_len)
    x1, u1 = mixer_outproj((attn_p, attn_s), x, mod1, w_out.astype(BF16), ln_g[0], ln_b[0], n_prompt_tok, sample_len)
    comb, pos, combc, posc, after = moe_router(u1, w_router)
    plan = _moe_plan(after[:, :, 0].astype(jnp.int32), x.shape[0])
    xs = moe_dispatch(u1, pos, plan)
    y = moe_grouped_ffn(xs, moe_w_in.astype(BF16), moe_w_out.astype(BF16), plan)
    y_p, y_s = moe_combine(y, posc, combc, x1, mod1, ln_g[1], ln_b[1], plan, n_prompt_tok, sample_len)
    return y_p, y_s, k_p, v_p


def kernel(x_prompt, x_sample, state_mlstm_C, state_mlstm_n, state_mlstm_m, cache_na_k, cache_na_v, c, c_ctx,
           ada_w, ada_b, ln_g, ln_b, mlstm_w_in, mlstm_b_gates, mlstm_norm_g, mlstm_w_out, na_w_qkv, na_rpb,
           na_w_out, ffn_w_in, ffn_w_out, moe_w_router, moe_w_in, moe_w_out):
    n_prompt, prompt_len, d = x_prompt.shape
    n_sample, sample_len, _ = x_sample.shape
    assert d == D_MODEL and prompt_len == MLSTM_CHUNK and sample_len % MLSTM_CHUNK == 0
    assert ada_w.shape[0] == DEPTH == 2
    mod = _modulation_tables(c, c_ctx, ada_w, ada_b)
    _, x2, u2, new_c, new_n, new_m = _layer0(
        x_prompt.reshape(-1, d), x_sample.reshape(-1, d), mod[0], mod[1], state_mlstm_C[:, 0], state_mlstm_n[:, 0], state_mlstm_m[:, 0], ln_g[0], ln_b[0],
        mlstm_w_in[0], mlstm_b_gates[0], mlstm_norm_g[0], mlstm_w_out[0], ffn_w_in[0], ffn_w_out[0],
        n_prompt, prompt_len, n_sample, sample_len)
    y_p, y_s, k_p, v_p = _layer1(
        x2, u2, mod[1], cache_na_k[:, 0], cache_na_v[:, 0], ln_g[1], ln_b[1], na_w_qkv[0], na_rpb[0], na_w_out[0],
        moe_w_router[0], moe_w_in[0], moe_w_out[0], n_prompt, prompt_len, n_sample, sample_len)
    kv_shape = (n_prompt, 1, prompt_len, NA_HEADS, NA_DH)
    return (y_p.reshape(x_prompt.shape), y_s.reshape(x_sample.shape), new_c, new_n, new_m,
            k_p.reshape(kv_shape), v_p.reshape(kv_shape))
```

```python
import functools
import math

import jax
import jax.numpy as jnp
from jax import lax
from jax.experimental import pallas as pl
from jax.experimental.pallas import tpu as pltpu

F32 = jnp.float32
BF16 = jnp.bfloat16

D_MODEL = 1024
DEPTH = 2
GRID_W = 64
M_HEADS = 4
M_DK = 128
M_DV = 256
M_HK = M_HEADS * M_DK
NA_HEADS = 16
NA_DH = 64
NA_ROWS = 8
NA_COLS = 16
D_FF = 2816
N_EXPERTS = 8
DEEPNORM_ALPHA = (2.0 * DEPTH) ** 0.25
LN_EPS = 1e-5
NEG = -1e30

VMEM_LIMIT_BYTES = 56 * 1024 * 1024

MLSTM_CHUNK = 256
TOKEN_TILE = 512
FFN_TOKEN_TILE = 512
FFN_SUB = 256
NA_QROWS = 4
NA_HALO = 4


def _cparams(*sem):
    return pltpu.CompilerParams(dimension_semantics=sem, vmem_limit_bytes=VMEM_LIMIT_BYTES)


def _dot(a, b):
    return jnp.dot(a, b, preferred_element_type=F32)


def _dot_nt(a, b):
    return lax.dot_general(a, b, (((1,), (1,)), ((), ())), preferred_element_type=F32)


def _onehot(mask):
    return jnp.where(mask, 1.0, 0.0).astype(BF16)


def _split3(x):
    hi = x.astype(BF16)
    r = x - hi.astype(F32)
    mid = r.astype(BF16)
    lo = (r - mid.astype(F32)).astype(BF16)
    return hi, mid, lo


def _layer_norm(z, g, b):
    mu = jnp.mean(z, axis=-1, keepdims=True)
    zc = z - mu
    var = jnp.mean(zc * zc, axis=-1, keepdims=True)
    return zc * lax.rsqrt(var + LN_EPS) * g + b


def _log_sigmoid(x):
    return jnp.minimum(x, 0.0) - jnp.log(1.0 + jnp.exp(-jnp.abs(x)))


def _seg_of_tile(i, tile, n_prompt_tok, sample_len):
    tok = i * tile
    return jnp.where(tok < n_prompt_tok, 0, 1 + (tok - n_prompt_tok) // sample_len)


def _adaln_kernel(c_ref, w_ref, b_ref, o_ref):
    c = c_ref[...]
    a = (c * jax.nn.sigmoid(c)).astype(BF16)
    o_ref[0] = _dot(a, w_ref[0].astype(BF16)) + b_ref[0]


def adaln(cvec, ada_w, ada_b):
    depth, d, n = ada_w.shape
    tn = 1536
    return pl.pallas_call(
        _adaln_kernel,
        grid=(depth, n // tn),
        in_specs=[
            pl.BlockSpec((8, d), lambda l, j: (0, 0)),
            pl.BlockSpec((1, d, tn), lambda l, j: (l, 0, j)),
            pl.BlockSpec((1, 1, tn), lambda l, j: (l, 0, j)),
        ],
        out_specs=pl.BlockSpec((1, 8, tn), lambda l, j: (l, 0, j)),
        out_shape=jax.ShapeDtypeStruct((depth, 8, n), F32),
        compiler_params=_cparams("arbitrary", "arbitrary"),
        name="adaln",
    )(cvec, ada_w, ada_b.reshape(depth, 1, n))


def _inproj_kernel(xp_ref, xs_ref, mod_ref, w_ref, wg_ref, wgt_ref, bg_ref, bgt_ref,
                   qkv_ref, o_ref, g_ref, gt_ref, *, n_prompt_tiles):
    x = jnp.where(pl.program_id(0) < n_prompt_tiles, xp_ref[...], xs_ref[...])
    u = (x * (1.0 + mod_ref[0, 1:2, :]) + mod_ref[0, 0:1, :]).astype(BF16)
    p = _dot(u, w_ref[...])
    nqkv = qkv_ref.shape[1]
    qkv_ref[...] = p[:, :nqkv].astype(BF16)
    o_ref[...] = p[:, nqkv:]
    g = _dot(u, wg_ref[...]) + bg_ref[...]
    col = lax.broadcasted_iota(jnp.int32, g.shape, 1)
    g_ref[...] = jnp.where(((col >> 2) & 1) == 1, _log_sigmoid(g), g)
    gt = _dot_nt(wgt_ref[...], u) + bgt_ref[...]
    row = lax.broadcasted_iota(jnp.int32, gt.shape, 0)
    gt_ref[...] = jnp.where(((row >> 2) & 1) == 1, _log_sigmoid(gt), gt)


def _split_token_specs(tile, d, n_prompt_tiles):
    return [pl.BlockSpec((tile, d), lambda i: (jnp.minimum(i, n_prompt_tiles - 1), 0)),
            pl.BlockSpec((tile, d), lambda i: (jnp.maximum(i - n_prompt_tiles, 0), 0))]


def mlstm_inproj(xp, xs, mod, w_main, w_gate, b_gate, sample_len):
    n_prompt_tok, d = xp.shape
    t = n_prompt_tok + xs.shape[0]
    n_main = w_main.shape[1]
    n_qkv = 2 * M_HK + D_MODEL
    ng = w_gate.shape[1]
    tt = TOKEN_TILE
    npt = n_prompt_tok // tt
    seg = functools.partial(_seg_of_tile, tile=tt, n_prompt_tok=n_prompt_tok, sample_len=sample_len)
    return pl.pallas_call(
        functools.partial(_inproj_kernel, n_prompt_tiles=npt),
        grid=(t // tt,),
        in_specs=_split_token_specs(tt, d, npt) + [
            pl.BlockSpec((1, 6, d), lambda i: (seg(i), 0, 0)),
            pl.BlockSpec((d, n_main), lambda i: (0, 0)),
            pl.BlockSpec((d, ng), lambda i: (0, 0)),
            pl.BlockSpec((ng, d), lambda i: (0, 0)),
            pl.BlockSpec((1, ng), lambda i: (0, 0)),
            pl.BlockSpec((ng, 1), lambda i: (0, 0)),
        ],
        out_specs=[
            pl.BlockSpec((tt, n_qkv), lambda i: (i, 0)),
            pl.BlockSpec((tt, n_main - n_qkv), lambda i: (i, 0)),
            pl.BlockSpec((tt, ng), lambda i: (i, 0)),
            pl.BlockSpec((ng, tt), lambda i: (0, i)),
        ],
        out_shape=[
            jax.ShapeDtypeStruct((t, n_qkv), BF16),
            jax.ShapeDtypeStruct((t, n_main - n_qkv), F32),
            jax.ShapeDtypeStruct((t, ng), F32),
            jax.ShapeDtypeStruct((ng, t), F32),
        ],
        compiler_params=_cparams("arbitrary"),
        name="mlstm_inproj",
    )(xp, xs, mod, w_main, w_gate, w_gate.T, b_gate.reshape(1, ng), b_gate.reshape(ng, 1))


def _mlstm_gate_sums(g, gt, backward):
    L = g.shape[0]
    row = lax.broadcasted_iota(jnp.int32, (L, L), 0)
    col = lax.broadcasted_iota(jnp.int32, (L, L), 1)
    lower = col <= row
    upper = col >= row
    tri_col = _onehot(upper if backward else lower)
    tri_row = _onehot(lower if backward else upper)
    ghi, gmid, glo = _split3(g)
    b_cols = _dot(tri_col, ghi) + _dot(tri_col, gmid) + _dot(tri_col, glo)
    thi, tmid, tlo = _split3(gt)
    b_rows = _dot(thi, tri_row) + _dot(tmid, tri_row) + _dot(tlo, tri_row)
    return b_cols, b_rows


def _mlstm_heads(items):
    scale = M_DK ** -0.5
    log_scale = math.log(scale)
    L = items[0][0].shape[0]
    row = lax.broadcasted_iota(jnp.int32, (L, L), 0)
    col = lax.broadcasted_iota(jnp.int32, (L, L), 1)
    masks = {False: col <= row, True: col >= row}
    n = range(len(items))
    qh, kh, vh, li_row, lf_row, b_row, b_col, backward, state = zip(*items)
    has_state = [st is not None for st in state]
    m_prev = [st[2] if st is not None else 0.0 for st in state]

    qk = [_dot_nt(qh[i], kh[i]) for i in n]
    qc = [_dot(qh[i], state[i][0].astype(BF16)) if has_state[i] else None for i in n]
    qn = [_dot_nt(qh[i], state[i][1].astype(BF16))[:, 0:1] if has_state[i] else None for i in n]
    total = [jnp.sum(lf_row[i], axis=1, keepdims=True) for i in n]
    d = [jnp.where(masks[backward[i]], b_col[i] + (li_row[i] - b_row[i] + log_scale), NEG) for i in n]
    inter = [b_col[i] + m_prev[i] if has_state[i] else b_col[i] for i in n]
    m_t = [jnp.maximum(inter[i], jnp.max(d[i], axis=1, keepdims=True) - log_scale) for i in n]
    p = [jnp.exp(d[i] - m_t[i]) for i in n]
    s = [qk[i] * p[i] for i in n]
    den = [jnp.sum(s[i], axis=1, keepdims=True) for i in n]
    num = [_dot(s[i].astype(BF16), vh[i]) for i in n]
    a = [jnp.exp(inter[i] - m_t[i]) * scale if has_state[i] else None for i in n]
    num = [a[i] * qc[i] + num[i] if has_state[i] else num[i] for i in n]
    den = [a[i] * qn[i] + den[i] if has_state[i] else den[i] for i in n]
    hh = [num[i] / jnp.maximum(jnp.abs(den[i]), jnp.exp(-m_t[i])) for i in n]

    g_row = [total[i] - b_row[i] + li_row[i] for i in n]
    m_new = [jnp.maximum(total[i] + m_prev[i], jnp.max(g_row[i], axis=1, keepdims=True)) for i in n]
    ws_row = [jnp.exp(g_row[i] - m_new[i]) for i in n]
    kt = [kh[i].astype(F32).T for i in n]
    c_new = [_dot((kt[i] * ws_row[i]).astype(BF16), vh[i]) for i in n]
    n_new = [_dot(jnp.broadcast_to(ws_row[i], (8, L)).astype(BF16), kh[i]) for i in n]
    a0 = [jnp.exp(total[i] + m_prev[i] - m_new[i]) if has_state[i] else None for i in n]
    c_new = [a0[i] * state[i][0] + c_new[i] if has_state[i] else c_new[i] for i in n]
    n_new = [a0[i] * state[i][1] + n_new[i] if has_state[i] else n_new[i] for i in n]
    return hh, c_new, n_new, m_new


def _head_norm_gate(ht, norm_g, ogate):
    mu = jnp.mean(ht, axis=-1, keepdims=True)
    hc = ht - mu
    var = jnp.mean(hc * hc, axis=-1, keepdims=True)
    return (hc * lax.rsqrt(var + LN_EPS) * norm_g * jax.nn.sigmoid(ogate)).astype(BF16)


def _head_operands(q_ref, k_ref, v_ref, h):
    return (q_ref[:, h * M_DK:(h + 1) * M_DK], k_ref[:, h * M_DK:(h + 1) * M_DK], v_ref[:, h * M_DV:(h + 1) * M_DV])


def _gate_operands(gt, b_rows, b_cols, direction, h):
    ci = direction * 8 + h
    cf = direction * 8 + 4 + h
    return gt[ci:ci + 1, :], gt[cf:cf + 1, :], b_rows[cf:cf + 1, :], b_cols[:, cf:cf + 1]


def _mlstm_prompt_kernel(q_ref, k_ref, v_ref, g_ref, gt_ref, o_ref, ng_ref, a_ref, cf_ref, nf_ref, mf_ref):
    g, gt = g_ref[...], gt_ref[...]
    sums = [_mlstm_gate_sums(g, gt, direction == 1) for direction in (0, 1)]
    keys = [(direction, h) for direction in (0, 1) for h in range(M_HEADS)]
    items = [_head_operands(q_ref, k_ref, v_ref, h)
             + _gate_operands(gt, sums[direction][1], sums[direction][0], direction, h) + (direction == 1, None)
             for direction, h in keys]
    hh, c_new, n_new, m_new = _mlstm_heads(items)
    for i, (direction, h) in enumerate(keys):
        cf_ref[0, 0, direction, h] = c_new[i]
        nf_ref[0, 0, direction, h] = n_new[i]
        mf_ref[0, 0, direction, h] = jnp.broadcast_to(m_new[i], mf_ref.shape[-2:])
    for h in range(M_HEADS):
        sl = slice(h * M_DV, (h + 1) * M_DV)
        a_ref[:, sl] = _head_norm_gate(hh[h] + hh[M_HEADS + h], ng_ref[:, sl], o_ref[:, sl])


def _mlstm_sample_kernel(*refs, direction, n_units):
    backward = direction == 1
    refs = list(refs)
    unit_refs = [tuple(refs.pop(0) for _ in range(5)) for _ in range(n_units)]
    c0_ref, n0_ref, m0_ref = refs.pop(0), refs.pop(0), refs.pop(0)
    if backward:
        hprev_ref = refs.pop(0)
        o_refs = [refs.pop(0) for _ in range(n_units)]
        ng_ref = refs.pop(0)
    out_ref, c_scr, n_scr, m_scr = refs

    @pl.when(pl.program_id(0) == 0)
    def _():
        c_scr[...] = c0_ref[...]
        n_scr[...] = n0_ref[...]
        m_scr[...] = m0_ref[...]

    keys, items = [], []
    for u, (q_ref, k_ref, v_ref, g_ref, gt_ref) in enumerate(unit_refs):
        gt = gt_ref[...]
        b_cols, b_rows = _mlstm_gate_sums(g_ref[...], gt, backward)
        for h in range(M_HEADS):
            state = (c_scr[u, h], n_scr[u, h], m_scr[u, h][0:1, 0:1])
            keys.append((u, h))
            items.append(_head_operands(q_ref, k_ref, v_ref, h) + _gate_operands(gt, b_rows, b_cols, direction, h)
                         + (backward, state))
    hh, c_new, n_new, m_new = _mlstm_heads(items)
    for i, (u, h) in enumerate(keys):
        c_scr[u, h] = c_new[i]
        n_scr[u, h] = n_new[i]
        m_scr[u, h] = jnp.broadcast_to(m_new[i], m_scr.shape[-2:])
        sl = slice(h * M_DV, (h + 1) * M_DV)
        if backward:
            out_ref[u, :, sl] = _head_norm_gate(hprev_ref[u, :, sl] + hh[i], ng_ref[:, sl], o_refs[u][:, sl])
        else:
            out_ref[u, :, sl] = hh[i]


def _mlstm_chunk_specs(block_of):
    L = MLSTM_CHUNK
    return [
        pl.BlockSpec((L, M_HK), lambda s: (block_of(s), 0)),
        pl.BlockSpec((L, M_HK), lambda s: (block_of(s), 1)),
        pl.BlockSpec((L, D_MODEL), lambda s: (block_of(s), 1)),
        pl.BlockSpec((L, 16), lambda s: (block_of(s), 0)),
        pl.BlockSpec((16, L), lambda s: (0, block_of(s))),
    ]


def mlstm_prompt(qkv, g, gt, ogate, norm_g, n_prompt):
    L = MLSTM_CHUNK
    state_blk = lambda *tail: pl.BlockSpec((1, 1, 2, M_HEADS) + tail, lambda s: (s, 0, 0, 0, 0, 0))
    return pl.pallas_call(
        _mlstm_prompt_kernel,
        grid=(n_prompt,),
        in_specs=_mlstm_chunk_specs(lambda s: s) + [
            pl.BlockSpec((L, D_MODEL), lambda s: (s, 0)),
            pl.BlockSpec((1, D_MODEL), lambda s: (0, 0)),
        ],
        out_specs=[pl.BlockSpec((L, D_MODEL), lambda s: (s, 0)),
                   state_blk(M_DK, M_DV), state_blk(8, M_DK), state_blk(8, 128)],
        out_shape=[
            jax.ShapeDtypeStruct((n_prompt * L, D_MODEL), BF16),
            jax.ShapeDtypeStruct((n_prompt, 1, 2, M_HEADS, M_DK, M_DV), F32),
            jax.ShapeDtypeStruct((n_prompt, 1, 2, M_HEADS, 8, M_DK), F32),
            jax.ShapeDtypeStruct((n_prompt, 1, 2, M_HEADS, 8, 128), F32),
        ],
        compiler_params=_cparams("arbitrary"),
        name="mlstm_prompt",
    )(qkv, qkv, qkv, g, gt, ogate, norm_g.reshape(1, D_MODEL))


def mlstm_sample(direction, qkv, g, gt, c0, n0, m0, first_block, chunks_per_sample,
                 hprev=None, ogate=None, norm_g=None):
    L = MLSTM_CHUNK
    backward = direction == 1
    cps = chunks_per_sample
    n_units = c0.shape[0]
    pos = (lambda s: cps - 1 - s) if backward else (lambda s: s)
    unit_block = [functools.partial(lambda s, u: first_block + u * cps + pos(s), u=u) for u in range(n_units)]
    whole = lambda a: pl.BlockSpec(a.shape, lambda s: (0,) * a.ndim)
    in_specs, args = [], []
    for u in range(n_units):
        in_specs += _mlstm_chunk_specs(unit_block[u])
        args += [qkv, qkv, qkv, g, gt]
    in_specs += [whole(c0), whole(n0), whole(m0)]
    args += [c0, n0, m0]
    if backward:
        in_specs += [pl.BlockSpec((n_units, L, D_MODEL), lambda s: (0, pos(s), 0))]
        in_specs += [pl.BlockSpec((L, D_MODEL), functools.partial(lambda s, u: (unit_block[u](s), 0), u=u))
                     for u in range(n_units)]
        in_specs += [pl.BlockSpec((1, D_MODEL), lambda s: (0, 0))]
        args += [hprev] + [ogate] * n_units + [norm_g.reshape(1, D_MODEL)]
    return pl.pallas_call(
        functools.partial(_mlstm_sample_kernel, direction=direction, n_units=n_units),
        grid=(cps,),
        in_specs=in_specs,
        out_specs=pl.BlockSpec((n_units, L, D_MODEL), lambda s: (0, pos(s), 0)),
        out_shape=jax.ShapeDtypeStruct((n_units, cps * L, D_MODEL), BF16 if backward else F32),
        scratch_shapes=[pltpu.VMEM(c0.shape, F32), pltpu.VMEM(n0.shape, F32), pltpu.VMEM(m0.shape, F32)],
        compiler_params=_cparams("arbitrary"),
        name="mlstm_sample_bwd" if backward else "mlstm_sample_fwd",
    )(*args)


def _outproj_kernel(*refs, n_prompt_tiles, a_split, x_split):
    refs = list(refs)
    is_prompt = pl.program_id(0) < n_prompt_tiles

    def take(split):
        if split:
            p_ref, s_ref = refs.pop(0), refs.pop(0)
            return jnp.where(is_prompt, p_ref[...], s_ref[...])
        return refs.pop(0)[...]

    a = take(a_split)
    x = take(x_split)
    mod_ref, w_ref, lng_ref, lnb_ref, xo_ref, uo_ref = refs
    y = _dot(a, w_ref[...])
    z = DEEPNORM_ALPHA * x + mod_ref[0, 2:3, :] * y
    xn = _layer_norm(z, lng_ref[...], lnb_ref[...])
    xo_ref[...] = xn
    uo_ref[...] = (xn * (1.0 + mod_ref[0, 4:5, :]) + mod_ref[0, 3:4, :]).astype(BF16)


def mixer_outproj(a, x, mod, w, ln_g, ln_b, n_prompt_tok, sample_len):
    d = w.shape[0]
    tt = TOKEN_TILE
    npt = n_prompt_tok // tt
    seg = functools.partial(_seg_of_tile, tile=tt, n_prompt_tok=n_prompt_tok, sample_len=sample_len)

    def specs(v):
        if isinstance(v, tuple):
            return _split_token_specs(tt, d, npt), list(v), v[0].shape[0] + v[1].shape[0]
        return [pl.BlockSpec((tt, d), lambda i: (i, 0))], [v], v.shape[0]

    a_specs, a_args, t = specs(a)
    x_specs, x_args, _ = specs(x)
    return pl.pallas_call(
        functools.partial(_outproj_kernel, n_prompt_tiles=npt, a_split=isinstance(a, tuple),
                          x_split=isinstance(x, tuple)),
        grid=(t // tt,),
        in_specs=a_specs + x_specs + [
            pl.BlockSpec((1, 6, d), lambda i: (seg(i), 0, 0)),
            pl.BlockSpec((d, d), lambda i: (0, 0)),
            pl.BlockSpec((1, d), lambda i: (0, 0)),
            pl.BlockSpec((1, d), lambda i: (0, 0)),
        ],
        out_specs=[pl.BlockSpec((tt, d), lambda i: (i, 0)), pl.BlockSpec((tt, d), lambda i: (i, 0))],
        out_shape=[jax.ShapeDtypeStruct((t, d), F32), jax.ShapeDtypeStruct((t, d), BF16)],
        compiler_params=_cparams("arbitrary"),
        name="mixer_outproj",
    )(*a_args, *x_args, mod, w, ln_g.reshape(1, d), ln_b.reshape(1, d))


def _swiglu_tile(x, wi, wo, dff):
    sub = FFN_SUB
    proj = lambda j: (_dot(x, wi(slice(j * sub, (j + 1) * sub))), _dot(x, wi(slice(dff + j * sub, dff + (j + 1) * sub))))
    acts, cur = [], proj(0)
    for j in range(dff // sub):
        nxt = proj(j + 1) if (j + 1) * sub < dff else None
        gp, up = cur
        acts.append((gp * jax.nn.sigmoid(gp) * up).astype(BF16))
        cur = nxt
    return _dot(jnp.concatenate(acts, axis=1), wo(slice(0, dff)))


def _mixer_ffn_kernel(ap_ref, as_ref, xp_ref, xs_ref, mod_ref, modn_ref, wmix_ref, wi_ref, wo_ref, ln_ref,
                      xo_ref, uo_ref, *, n_prompt_tiles):
    is_prompt = pl.program_id(0) < n_prompt_tiles
    a = jnp.where(is_prompt, ap_ref[...], as_ref[...])
    x = jnp.where(is_prompt, xp_ref[...], xs_ref[...])
    z = DEEPNORM_ALPHA * x + mod_ref[0, 2:3, :] * _dot(a, wmix_ref[...])
    x1 = _layer_norm(z, ln_ref[0:1, :], ln_ref[1:2, :])
    u1 = (x1 * (1.0 + mod_ref[0, 4:5, :]) + mod_ref[0, 3:4, :]).astype(BF16)
    f = _swiglu_tile(u1, lambda c: wi_ref[:, c], lambda r: wo_ref[r, :], wo_ref.shape[0])
    x2 = _layer_norm(DEEPNORM_ALPHA * x1 + mod_ref[0, 5:6, :] * f, ln_ref[2:3, :], ln_ref[3:4, :])
    xo_ref[...] = x2
    uo_ref[...] = (x2 * (1.0 + modn_ref[0, 1:2, :]) + modn_ref[0, 0:1, :]).astype(BF16)


def _resident(shape):
    return pl.BlockSpec(shape, lambda i: (0,) * len(shape), pipeline_mode=pl.Buffered(1))


def mixer_ffn(a_p, a_s, xp, xs, mod, mod_next, w_mix, w_in, w_out, ln_g, ln_b, sample_len):
    n_prompt_tok, d = xp.shape
    t = n_prompt_tok + xs.shape[0]
    tm = FFN_TOKEN_TILE
    npt = n_prompt_tok // tm
    seg = functools.partial(_seg_of_tile, tile=tm, n_prompt_tok=n_prompt_tok, sample_len=sample_len)
    ln = jnp.stack([ln_g[0], ln_b[0], ln_g[1], ln_b[1]])
    return pl.pallas_call(
        functools.partial(_mixer_ffn_kernel, n_prompt_tiles=npt),
        grid=(t // tm,),
        in_specs=_split_token_specs(tm, d, npt) + _split_token_specs(tm, d, npt) + [
            pl.BlockSpec((1, 6, d), lambda i: (seg(i), 0, 0)),
            pl.BlockSpec((1, 6, d), lambda i: (seg(i), 0, 0)),
            _resident(w_mix.shape), _resident(w_in.shape), _resident(w_out.shape), _resident(ln.shape),
        ],
        out_specs=[pl.BlockSpec((tm, d), lambda i: (i, 0)), pl.BlockSpec((tm, d), lambda i: (i, 0))],
        out_shape=[jax.ShapeDtypeStruct((t, d), F32), jax.ShapeDtypeStruct((t, d), BF16)],
        compiler_params=_cparams("arbitrary"),
        name="mixer_ffn",
    )(a_p, a_s, xp, xs, mod, mod_next, w_mix, w_in, w_out, ln)


def _modulation_tables(c, c_ctx, ada_w, ada_b):
    n_s = c.shape[0]
    cvec = jnp.concatenate([c_ctx[None, :], c, jnp.zeros((8 - 1 - n_s, c.shape[1]), F32)], axis=0)
    mod = adaln(cvec, ada_w, ada_b)
    return mod[:, :1 + n_s, :].reshape(ada_w.shape[0], 1 + n_s, 6, c.shape[1])


def _layer0(xp, xs, mod0, mod1, state_c, state_n, state_m, ln_g, ln_b, w_in, b_gates, norm_g, w_out,
            ffn_w_in, ffn_w_out, n_prompt, prompt_len, n_sample, sample_len):
    n_prompt_tok = n_prompt * prompt_len
    n_main = 2 * M_HK + 2 * D_MODEL
    qkv, ogate, g, gt = mlstm_inproj(xp, xs, mod0, w_in[:, :n_main].astype(BF16), w_in[:, n_main:].astype(BF16),
                                     b_gates.reshape(-1), sample_len)
    npc = n_prompt_tok // MLSTM_CHUNK
    cps = sample_len // MLSTM_CHUNK
    rep = lambda a: jnp.broadcast_to(a[..., None, :], a.shape[:-1] + (8, a.shape[-1]))
    n0 = rep(state_n)
    m0 = jnp.broadcast_to(state_m[..., None, None], state_m.shape + (8, 128))
    a_p, new_c, nf, mf = mlstm_prompt(qkv, g, gt, ogate, norm_g, n_prompt)
    hf = mlstm_sample(0, qkv, g, gt, state_c[:, 0], n0[:, 0], m0[:, 0], npc, cps)
    a_s = mlstm_sample(1, qkv, g, gt, state_c[:, 1], n0[:, 1], m0[:, 1], npc, cps,
                       hprev=hf, ogate=ogate, norm_g=norm_g)
    x2, u2 = mixer_ffn(a_p, a_s.reshape(-1, D_MODEL), xp, xs, mod0, mod1, w_out.astype(BF16),
                       ffn_w_in.astype(BF16), ffn_w_out.astype(BF16), ln_g, ln_b, sample_len)
    return x2, u2, new_c, nf[..., 0, :], mf[..., 0, 0]


def _qkv_kernel(u_ref, w_ref, qkv_ref, k_ref, v_ref, *, n_prompt_tiles):
    p = _dot(u_ref[...], w_ref[...])
    qkv_ref[...] = p.astype(BF16)
    tt, d = u_ref.shape

    @pl.when(pl.program_id(0) < n_prompt_tiles)
    def _():
        for h in range(NA_HEADS):
            rows = pl.ds(h, tt, stride=NA_HEADS)
            k_ref[rows, :] = p[:, d + h * NA_DH:d + (h + 1) * NA_DH]
            v_ref[rows, :] = p[:, 2 * d + h * NA_DH:2 * d + (h + 1) * NA_DH]


def na_qkv(u, w, n_prompt_tok):
    t, d = u.shape
    tt = TOKEN_TILE
    npt = n_prompt_tok // tt
    kv_spec = pl.BlockSpec((tt * NA_HEADS, NA_DH), lambda i: (jnp.minimum(i, npt - 1), 0))
    kv_shape = jax.ShapeDtypeStruct((n_prompt_tok * NA_HEADS, NA_DH), F32)
    return pl.pallas_call(
        functools.partial(_qkv_kernel, n_prompt_tiles=npt),
        grid=(t // tt,),
        in_specs=[pl.BlockSpec((tt, d), lambda i: (i, 0)), pl.BlockSpec((d, 3 * d), lambda i: (0, 0))],
        out_specs=[pl.BlockSpec((tt, 3 * d), lambda i: (i, 0)), kv_spec, kv_spec],
        out_shape=[jax.ShapeDtypeStruct((t, 3 * d), BF16), kv_shape, kv_shape],
        compiler_params=_cparams("arbitrary"),
        name="na_qkv",
    )(u, w)


def _head_pair_masks(scale):
    assert math.frexp(scale)[0] == 0.5
    lane = lax.broadcasted_iota(jnp.int32, (1, 2 * NA_DH), 1)
    first, second = lane < NA_DH, lane >= NA_DH
    return tuple((jnp.where(sel, scale, 0.0).astype(BF16), _onehot(sel)) for sel in (first, second))


def _softmax_pv(score_blocks, value_blocks):
    mx = None
    for s in score_blocks:
        bm = jnp.max(s, axis=1, keepdims=True)
        mx = bm if mx is None else jnp.maximum(mx, bm)
    acc, denom = None, None
    for s, v in zip(score_blocks, value_blocks):
        p = jnp.exp(s - mx)
        ps = jnp.sum(p, axis=1, keepdims=True)
        pv = _dot(p.astype(BF16), v)
        acc = pv if acc is None else acc + pv
        denom = ps if denom is None else denom + ps
    return acc / denom


def _ctx_attn_kernel(q_ref, k_ref, v_ref, o_ref):
    masks = _head_pair_masks(NA_DH ** -0.5)
    for hp in range(NA_HEADS // 2):
        sl = slice(hp * 2 * NA_DH, (hp + 1) * 2 * NA_DH)
        q2, k2, v2 = q_ref[:, sl], k_ref[:, sl], v_ref[:, sl]
        o2 = None
        for qmsk, vmsk in masks:
            s = _dot_nt(q2 * qmsk, k2)
            o = _softmax_pv([s], [v2 * vmsk])
            o2 = o if o2 is None else o2 + o
        o_ref[:, sl] = o2.astype(o_ref.dtype)


def context_attention(qkv, n_prompt, prompt_len):
    t = n_prompt * prompt_len
    d = D_MODEL
    return pl.pallas_call(
        _ctx_attn_kernel,
        grid=(n_prompt,),
        in_specs=[
            pl.BlockSpec((prompt_len, d), lambda b: (b, 0)),
            pl.BlockSpec((prompt_len, d), lambda b: (b, 1)),
            pl.BlockSpec((prompt_len, d), lambda b: (b, 2)),
        ],
        out_specs=pl.BlockSpec((prompt_len, d), lambda b: (b, 0)),
        out_shape=jax.ShapeDtypeStruct((t, d), BF16),
        compiler_params=_cparams("arbitrary"),
        name="context_attention",
    )(qkv, qkv, qkv)


def _na_kernel(tile_ref, q_ref, kp_ref, kc_ref, kn_ref, vp_ref, vc_ref, vn_ref, kctx_ref, vctx_ref, bias_ref,
               o_ref):
    j = pl.program_id(1)
    nblk = pl.num_programs(1)
    m = q_ref.shape[0]
    halo = NA_HALO * GRID_W
    n_pairs = (NA_QROWS + 2 * NA_HALO) // 2
    variant = jnp.where(j == 0, 0, jnp.where(j == nblk - 1, 2, 1))
    kinds = [[tile_ref[(variant * NA_QROWS + rq) * n_pairs + kp] for kp in range(n_pairs)]
             for rq in range(NA_QROWS)]

    def bias_of(head):
        return jnp.concatenate(
            [jnp.concatenate([bias_ref[kinds[rq][kp], head] for kp in range(n_pairs)], axis=1)
             for rq in range(NA_QROWS)], axis=0)

    masks = _head_pair_masks(NA_DH ** -0.5)
    for hp in range(NA_HEADS // 2):
        sl = slice(hp * 2 * NA_DH, (hp + 1) * 2 * NA_DH)
        q2 = q_ref[:, sl]
        kloc = jnp.concatenate([kp_ref[m - halo:, sl], kc_ref[:, sl], kn_ref[:halo, sl]], axis=0)
        vloc = jnp.concatenate([vp_ref[m - halo:, sl], vc_ref[:, sl], vn_ref[:halo, sl]], axis=0)
        kctx = kctx_ref[0][:, sl]
        vctx = vctx_ref[0][:, sl]
        o2 = None
        for half, (qmsk, vmsk) in enumerate(masks):
            qm = q2 * qmsk
            s_loc = _dot_nt(qm, kloc) + bias_of(2 * hp + half)
            s_ctx = _dot_nt(qm, kctx)
            o = _softmax_pv([s_loc, s_ctx], [vloc * vmsk, vctx * vmsk])
            o2 = o if o2 is None else o2 + o
        o_ref[:, sl] = o2.astype(o_ref.dtype)


def _na_bias_kernel(lc_ref, rc_ref, lv_ref, rv_ref, o_ref):
    nc, npos = lc_ref.shape[2], o_ref.shape[2]
    pair_shift = (2 * GRID_W).bit_length() - 1
    c = lax.broadcasted_iota(jnp.int32, (nc, npos), 0)
    pos = lax.broadcasted_iota(jnp.int32, (nc, npos), 1)
    qc = pos >> pair_shift
    kc = pos & (GRID_W - 1)
    c_start = jnp.clip(qc - NA_COLS // 2, 0, GRID_W - NA_COLS)
    col_in = jnp.logical_and(kc >= c_start, kc < c_start + NA_COLS)
    dc = jnp.clip(kc - qc + NA_COLS - 1, 0, 2 * NA_COLS - 2)
    sel = _onehot(jnp.logical_and(c == dc, col_in))

    def pick(row_ref):
        r1, r2, r3 = _split3(row_ref[0])
        return _dot(r1, sel) + _dot(r2, sel) + _dot(r3, sel)

    right = (pos[0:1, :] & GRID_W) != 0
    val = jnp.where(right, pick(rc_ref), pick(lc_ref))
    visible = jnp.where(right, rv_ref[0], lv_ref[0]) > 0.0
    o_ref[0] = jnp.where(jnp.logical_and(visible, col_in[0:1, :]), val, NEG)


def _na_bias_plan():
    ndr = 2 * NA_ROWS - 1
    assert NA_QROWS >= NA_ROWS // 2 and NA_HALO == NA_ROWS // 2 and (NA_QROWS + 2 * NA_HALO) % 2 == 0
    ok = lambda dr: dr if dr is not None and 0 <= dr < ndr else None
    kinds = [(ok(dr), ok(dr + 1)) for dr in range(-1, ndr)]
    first_key_row = (lambda rq: NA_HALO, lambda rq: rq, lambda rq: NA_QROWS + NA_HALO - NA_ROWS)
    table = []
    for first in first_key_row:
        for rq in range(NA_QROWS):
            for kp in range((NA_QROWS + 2 * NA_HALO) // 2):
                drs = []
                for kk in (2 * kp, 2 * kp + 1):
                    visible = first(rq) <= kk < first(rq) + NA_ROWS
                    drs.append(kk - rq + NA_ROWS - 1 - NA_HALO if visible else None)
                kind = (drs[0], drs[1])
                if kind not in kinds:
                    kinds.append(kind)
                table.append(kinds.index(kind))
    return kinds, table


def _na_bias_tiles(rpb):
    nh, ndr, ndc = rpb.shape
    nc = 32
    kinds, table = _na_bias_plan()
    nk = len(kinds)
    rpb_p = jnp.pad(rpb, ((0, 0), (0, 0), (0, nc - ndc)))
    zero = jnp.zeros((nh, nc), F32)
    lc = jnp.stack([zero if l is None else rpb_p[:, l] for l, _ in kinds])
    rc = jnp.stack([zero if r is None else rpb_p[:, r] for _, r in kinds])
    vis = lambda flags: jnp.broadcast_to(jnp.asarray(flags, F32)[:, None, None], (nk, nh, 1))
    lv = vis([0.0 if l is None else 1.0 for l, _ in kinds])
    rv = vis([0.0 if r is None else 1.0 for _, r in kinds])
    npos = GRID_W * 2 * GRID_W
    row_spec = pl.BlockSpec((1, nh, nc), lambda t: (t, 0, 0))
    flag_spec = pl.BlockSpec((1, nh, 1), lambda t: (t, 0, 0))
    tiles = pl.pallas_call(
        _na_bias_kernel,
        grid=(nk,),
        in_specs=[row_spec, row_spec, flag_spec, flag_spec],
        out_specs=pl.BlockSpec((1, nh, npos), lambda t: (t, 0, 0)),
        out_shape=jax.ShapeDtypeStruct((nk, nh, npos), F32),
        compiler_params=_cparams("arbitrary"),
        name="na_bias",
    )(lc, rc, lv, rv)
    return tiles.reshape(nk, nh, GRID_W, 2 * GRID_W), jnp.asarray(table, jnp.int32)


def neighbourhood_attention(qkv, k_ctx, v_ctx, rpb, n_prompt_tok, n_sample, sample_len):
    d = D_MODEL
    m = NA_QROWS * GRID_W
    nblk = sample_len // m
    assert nblk >= 3
    base = n_prompt_tok // m
    bias, table = _na_bias_tiles(rpb)

    def blk(col, off):
        return pl.BlockSpec((m, d), lambda b, j, tbl: (base + b * nblk + jnp.clip(j + off, 0, nblk - 1), col))

    return pl.pallas_call(
        _na_kernel,
        grid_spec=pltpu.PrefetchScalarGridSpec(
            num_scalar_prefetch=1,
            grid=(n_sample, nblk),
            in_specs=[
                blk(0, 0), blk(1, -1), blk(1, 0), blk(1, 1), blk(2, -1), blk(2, 0), blk(2, 1),
                pl.BlockSpec((1,) + k_ctx.shape[1:], lambda b, j, tbl: (b, 0, 0)),
                pl.BlockSpec((1,) + v_ctx.shape[1:], lambda b, j, tbl: (b, 0, 0)),
                pl.BlockSpec(bias.shape, lambda b, j, tbl: (0, 0, 0, 0)),
            ],
            out_specs=pl.BlockSpec((m, d), lambda b, j, tbl: (b * nblk + j, 0)),
        ),
        out_shape=jax.ShapeDtypeStruct((n_sample * sample_len, d), BF16),
        compiler_params=_cparams("arbitrary", "arbitrary"),
        name="neighbourhood_attention",
    )(table, qkv, qkv, qkv, qkv, qkv, qkv, qkv, k_ctx, v_ctx, bias)


MOE_BLOCK = 256
MOE_ROW_TILE = 512
ROW_ALIGN = 16
MOE_WINDOW = MOE_BLOCK + ROW_ALIGN

def _router_kernel(u_ref, wt_ref, comb_ref, pos_ref, combc_ref, posc_ref, after_ref, carry_scr):
    i = pl.program_id(0)

    @pl.when(i == 0)
    def _():
        carry_scr[...] = jnp.zeros_like(carry_scr)

    u = u_ref[...]
    w1, w2, w3 = _split3(wt_ref[...])
    lg = _dot_nt(w1, u) + _dot_nt(w2, u) + _dot_nt(w3, u)
    ne, nt = lg.shape
    e = lax.broadcasted_iota(jnp.int32, lg.shape, 0)
    m1 = jnp.max(lg, axis=0, keepdims=True)
    i1 = jnp.min(jnp.where(lg == m1, e, ne), axis=0, keepdims=True)
    sel1 = e == i1
    lg2 = jnp.where(sel1, -jnp.inf, lg)
    m2 = jnp.max(lg2, axis=0, keepdims=True)
    i2 = jnp.min(jnp.where(lg2 == m2, e, ne), axis=0, keepdims=True)
    sel2 = e == i2
    ex = jnp.exp(m2 - m1)
    wa = 1.0 / (1.0 + ex)
    wb = ex / (1.0 + ex)
    comb = jnp.where(sel1, wa, jnp.where(sel2, wb, 0.0))
    assign = jnp.logical_or(sel1, sel2)
    a = jnp.where(assign, 1.0, 0.0)
    row = lax.broadcasted_iota(jnp.int32, (nt, nt), 0)
    col = lax.broadcasted_iota(jnp.int32, (nt, nt), 1)
    tri = _onehot(row < col)
    carry = carry_scr[:, 0:1]
    rank = _dot(a.astype(BF16), tri) + carry
    pos = jnp.where(assign, rank, -1.0)
    comb_ref[...] = comb
    pos_ref[...] = pos
    combc_ref[...] = comb.T
    posc_ref[...] = pos.T
    carry_new = carry + jnp.sum(a, axis=1, keepdims=True)
    carry_scr[...] = jnp.broadcast_to(carry_new, carry_scr.shape)
    after_ref[0] = jnp.broadcast_to(carry_new, carry_scr.shape)


def moe_router(u, w_router):
    t, d = u.shape
    ne = w_router.shape[1]
    tb = MOE_BLOCK
    nb = t // tb
    return pl.pallas_call(
        _router_kernel,
        grid=(nb,),
        in_specs=[pl.BlockSpec((tb, d), lambda i: (i, 0)), pl.BlockSpec((ne, d), lambda i: (0, 0))],
        out_specs=[
            pl.BlockSpec((ne, tb), lambda i: (0, i)),
            pl.BlockSpec((ne, tb), lambda i: (0, i)),
            pl.BlockSpec((tb, ne), lambda i: (i, 0)),
            pl.BlockSpec((tb, ne), lambda i: (i, 0)),
            pl.BlockSpec((1, ne, 128), lambda i: (i, 0, 0)),
        ],
        out_shape=[
            jax.ShapeDtypeStruct((ne, t), F32),
            jax.ShapeDtypeStruct((ne, t), F32),
            jax.ShapeDtypeStruct((t, ne), F32),
            jax.ShapeDtypeStruct((t, ne), F32),
            jax.ShapeDtypeStruct((nb, ne, 128), F32),
        ],
        scratch_shapes=[pltpu.VMEM((ne, 128), F32)],
        compiler_params=_cparams("arbitrary"),
        name="moe_router",
    )(u, w_router.T)


def _moe_plan(after, n_tok):
    nb, ne = after.shape
    tm, tb = MOE_ROW_TILE, MOE_BLOCK
    n_tiles = (2 * n_tok) // tm + ne
    before = jnp.concatenate([jnp.zeros((1, ne), jnp.int32), after[:-1]], axis=0)
    totals = after[-1]
    tiles_per = (totals + tm - 1) // tm
    tile_end = jnp.cumsum(tiles_per)
    tile_base = tile_end - tiles_per
    n_used = tile_end[-1]
    base_rows = tile_base * tm
    ti = jnp.arange(n_tiles, dtype=jnp.int32)
    tile_expert = jnp.minimum(jnp.sum((ti[:, None] >= tile_end[None, :]).astype(jnp.int32), axis=1), ne - 1)
    last_expert = tile_expert[jnp.maximum(n_used - 1, 0)]
    tile_expert = jnp.where(ti < n_used, tile_expert, last_expert)
    per = tm // tb
    si = jnp.arange(n_tiles * per, dtype=jnp.int32)
    sub_expert = tile_expert[si // per]
    sub_r0 = (si - tile_base[sub_expert] * per) * tb
    sub_valid = jnp.logical_and(si // per < n_used, sub_r0 < totals[sub_expert])
    aft_e = after.T[sub_expert]
    bef_e = before.T[sub_expert]
    lo = jnp.sum((aft_e <= sub_r0[:, None]).astype(jnp.int32), axis=1)
    hi = jnp.sum((bef_e < (sub_r0 + tb)[:, None]).astype(jnp.int32), axis=1) - 1
    hi = jnp.minimum(hi, nb - 1)
    lo = jnp.where(sub_valid, lo, 1)
    hi = jnp.where(sub_valid, hi, 0)
    first_row = base_rows[None, :] + before
    win_start = jnp.minimum(first_row // ROW_ALIGN * ROW_ALIGN, n_tiles * tm - MOE_WINDOW)
    rel_off = base_rows[None, :] - win_start
    return dict(n_tiles=n_tiles, tile_expert=tile_expert.astype(jnp.int32), n_used=n_used.reshape(1).astype(jnp.int32),
                sub_expert=sub_expert.astype(jnp.int32), sub_r0=sub_r0.astype(jnp.int32),
                sub_lo=lo.astype(jnp.int32), sub_hi=hi.astype(jnp.int32),
                win_start=(win_start // ROW_ALIGN).reshape(-1).astype(jnp.int32),
                rel_off=rel_off.reshape(-1).astype(jnp.int32))


def _dispatch_kernel(se_ref, r0_ref, lo_ref, hi_ref, pos_ref, u_ref, xs_ref, acc_ref):
    i = pl.program_id(0)
    e = se_ref[i]
    tb = xs_ref.shape[0]
    target = (r0_ref[i] + lax.broadcasted_iota(jnp.int32, (tb, tb), 0)).astype(F32)
    acc_ref[...] = jnp.zeros_like(acc_ref)

    def body(sb, carry):
        off = pl.multiple_of(sb * tb, tb)
        p = pos_ref[pl.ds(e, 1), pl.ds(off, tb)]
        acc_ref[...] += _dot(_onehot(p == target), u_ref[pl.ds(off, tb), :])
        return carry

    lax.fori_loop(lo_ref[i], hi_ref[i] + 1, body, 0)
    xs_ref[...] = acc_ref[...].astype(xs_ref.dtype)


def moe_dispatch(u, pos, plan):
    t, d = u.shape
    tb = MOE_BLOCK
    n_sub = plan["sub_expert"].shape[0]
    return pl.pallas_call(
        _dispatch_kernel,
        grid_spec=pltpu.PrefetchScalarGridSpec(
            num_scalar_prefetch=4,
            grid=(n_sub,),
            in_specs=[pl.BlockSpec(memory_space=pltpu.VMEM), pl.BlockSpec(memory_space=pltpu.VMEM)],
            out_specs=pl.BlockSpec((tb, d), lambda i, *_: (i, 0)),
            scratch_shapes=[pltpu.VMEM((tb, d), F32)],
        ),
        out_shape=jax.ShapeDtypeStruct((n_sub * tb, d), BF16),
        compiler_params=_cparams("arbitrary"),
        name="moe_dispatch",
    )(plan["sub_expert"], plan["sub_r0"], plan["sub_lo"], plan["sub_hi"], pos, u)


def _gffn_kernel(te_ref, nu_ref, x_ref, wi_ref, wo_ref, y_ref):
    used = pl.program_id(0) < nu_ref[0]

    @pl.when(used)
    def _():
        y = _swiglu_tile(x_ref[...], lambda c: wi_ref[0, :, c], lambda r: wo_ref[0, r, :], wo_ref.shape[1])
        y_ref[...] = y.astype(y_ref.dtype)

    @pl.when(jnp.logical_not(used))
    def _():
        y_ref[...] = jnp.zeros_like(y_ref)


def moe_grouped_ffn(xs, w_in, w_out, plan):
    nr, d = xs.shape
    ne, dff, _ = w_out.shape
    tm = MOE_ROW_TILE
    return pl.pallas_call(
        _gffn_kernel,
        grid_spec=pltpu.PrefetchScalarGridSpec(
            num_scalar_prefetch=2,
            grid=(nr // tm,),
            in_specs=[
                pl.BlockSpec((tm, d), lambda i, te, nu: (jnp.minimum(i, nu[0] - 1), 0)),
                pl.BlockSpec((1, d, 2 * dff), lambda i, te, nu: (te[i], 0, 0)),
                pl.BlockSpec((1, dff, d), lambda i, te, nu: (te[i], 0, 0)),
            ],
            out_specs=pl.BlockSpec((tm, d), lambda i, te, nu: (i, 0)),
        ),
        out_shape=jax.ShapeDtypeStruct((nr, d), BF16),
        compiler_params=_cparams("arbitrary"),
        name="moe_grouped_ffn",
    )(plan["tile_expert"], plan["n_used"], xs, w_in, w_out)


def _combine_kernel(wb_ref, off_ref, *refs, n_experts, n_prompt_blocks):
    y_refs = refs[:n_experts]
    posc_ref, combc_ref, x_ref, mod_ref, lng_ref, lnb_ref, op_ref, os_ref = refs[n_experts:]
    b = pl.program_id(0)
    tb = x_ref.shape[0]
    win = y_refs[0].shape[0]
    lane = lax.broadcasted_iota(jnp.int32, (tb, win), 1).astype(F32)
    posc = posc_ref[...]
    combc = combc_ref[...]
    moe = jnp.zeros(x_ref.shape, F32)
    for e in range(n_experts):
        pe = posc[:, e:e + 1]
        rel = pe + off_ref[b * n_experts + e].astype(F32)
        hit = jnp.logical_and(pe >= 0.0, rel == lane)
        moe = moe + combc[:, e:e + 1] * _dot(_onehot(hit), y_refs[e][...])
    z = DEEPNORM_ALPHA * x_ref[...] + mod_ref[0, 5:6, :] * moe
    xn = _layer_norm(z, lng_ref[...], lnb_ref[...])

    @pl.when(b < n_prompt_blocks)
    def _():
        op_ref[...] = xn

    @pl.when(b >= n_prompt_blocks)
    def _():
        os_ref[...] = xn


def moe_combine(y, posc, combc, x, mod, ln_g, ln_b, plan, n_prompt_tok, sample_len):
    t, d = x.shape
    ne = posc.shape[1]
    tb = MOE_BLOCK
    nb = t // tb
    npb = n_prompt_tok // tb
    seg = functools.partial(_seg_of_tile, tile=tb, n_prompt_tok=n_prompt_tok, sample_len=sample_len)

    def yspec(e):
        return pl.BlockSpec((pl.Element(MOE_WINDOW), pl.Element(d)),
                            lambda b, ws, off: (ws[b * ne + e] * ROW_ALIGN, 0))

    in_specs = [yspec(e) for e in range(ne)] + [
        pl.BlockSpec((tb, ne), lambda b, wb, off: (b, 0)),
        pl.BlockSpec((tb, ne), lambda b, wb, off: (b, 0)),
        pl.BlockSpec((tb, d), lambda b, wb, off: (b, 0)),
        pl.BlockSpec((1, 6, d), lambda b, wb, off: (seg(b), 0, 0)),
        pl.BlockSpec((1, d), lambda b, wb, off: (0, 0)),
        pl.BlockSpec((1, d), lambda b, wb, off: (0, 0)),
    ]
    return pl.pallas_call(
        functools.partial(_combine_kernel, n_experts=ne, n_prompt_blocks=npb),
        grid_spec=pltpu.PrefetchScalarGridSpec(
            num_scalar_prefetch=2,
            grid=(nb,),
            in_specs=in_specs,
            out_specs=[
                pl.BlockSpec((tb, d), lambda b, wb, off: (jnp.minimum(b, npb - 1), 0)),
                pl.BlockSpec((tb, d), lambda b, wb, off: (jnp.maximum(b - npb, 0), 0)),
            ],
        ),
        out_shape=[jax.ShapeDtypeStruct((n_prompt_tok, d), F32), jax.ShapeDtypeStruct((t - n_prompt_tok, d), F32)],
        compiler_params=_cparams("arbitrary"),
        name="moe_combine",
    )(plan["win_start"], plan["rel_off"], *([y] * ne), posc, combc, x, mod, ln_g.reshape(1, d), ln_b.reshape(1, d))


def _layer1(x, u, mod1, cache_k, cache_v, ln_g, ln_b, w_qkv, rpb, w_out, w_router, moe_w_in, moe_w_out,
            n_prompt, prompt_len, n_sample, sample_len):
    n_prompt_tok = n_prompt * prompt_len
    d = D_MODEL
    qkv, k_p, v_p = na_qkv(u, w_qkv.astype(BF16), n_prompt_tok)
    attn_p = context_attention(qkv, n_prompt, prompt_len)
    k_ctx = cache_k.reshape(n_sample, -1, d).astype(BF16)
    v_ctx = cache_v.reshape(n_sample, -1, d).astype(BF16)
    attn_s = neighbourhood_attention(qkv, k_ctx, v_ctx, rpb, n_prompt_tok, n_sample, sample_len)
    x1, u1 = mixer_outproj((attn_p, attn_s), x, mod1, w_out.astype(BF16), ln_g[0], ln_b[0], n_prompt_tok, sample_len)
    comb, pos, combc, posc, after = moe_router(u1, w_router)
    plan = _moe_plan(after[:, :, 0].astype(jnp.int32), x.shape[0])
    xs = moe_dispatch(u1, pos, plan)
    y = moe_grouped_ffn(xs, moe_w_in.astype(BF16), moe_w_out.astype(BF16), plan)
    y_p, y_s = moe_combine(y, posc, combc, x1, mod1, ln_g[1], ln_b[1], plan, n_prompt_tok, sample_len)
    return y_p, y_s, k_p, v_p


def kernel(x_prompt, x_sample, state_mlstm_C, state_mlstm_n, state_mlstm_m, cache_na_k, cache_na_v, c, c_ctx,
           ada_w, ada_b, ln_g, ln_b, mlstm_w_in, mlstm_b_gates, mlstm_norm_g, mlstm_w_out, na_w_qkv, na_rpb,
           na_w_out, ffn_w_in, ffn_w_out, moe_w_router, moe_w_in, moe_w_out):
    n_prompt, prompt_len, d = x_prompt.shape
    n_sample, sample_len, _ = x_sample.shape
    assert d == D_MODEL and prompt_len == MLSTM_CHUNK and sample_len % MLSTM_CHUNK == 0
    assert ada_w.shape[0] == DEPTH == 2
    mod = _modulation_tables(c, c_ctx, ada_w, ada_b)
    x2, u2, new_c, new_n, new_m = _layer0(
        x_prompt.reshape(-1, d), x_sample.reshape(-1, d), mod[0], mod[1], state_mlstm_C[:, 0], state_mlstm_n[:, 0], state_mlstm_m[:, 0], ln_g[0], ln_b[0],
        mlstm_w_in[0], mlstm_b_gates[0], mlstm_norm_g[0], mlstm_w_out[0], ffn_w_in[0], ffn_w_out[0],
        n_prompt, prompt_len, n_sample, sample_len)
    y_p, y_s, k_p, v_p = _layer1(
        x2, u2, mod[1], cache_na_k[:, 0], cache_na_v[:, 0], ln_g[1], ln_b[1], na_w_qkv[0], na_rpb[0], na_w_out[0],
        moe_w_router[0], moe_w_in[0], moe_w_out[0], n_prompt, prompt_len, n_sample, sample_len)
    kv_shape = (n_prompt, 1, prompt_len, NA_HEADS, NA_DH)
    return (y_p.reshape(x_prompt.shape), y_s.reshape(x_sample.shape), new_c, new_n, new_m,
            k_p.reshape(kv_shape), v_p.reshape(kv_shape))
```

```python
import functools
import math

import jax
import jax.numpy as jnp
from jax import lax
from jax.experimental import pallas as pl
from jax.experimental.pallas import tpu as pltpu
from jax.experimental.pallas import tpu_sc as plsc

F32 = jnp.float32
BF16 = jnp.bfloat16

D_MODEL = 1024
DEPTH = 2
GRID_W = 64
M_HEADS = 4
M_DK = 128
M_DV = 256
M_HK = M_HEADS * M_DK
NA_HEADS = 16
NA_DH = 64
NA_ROWS = 8
NA_COLS = 16
D_FF = 2816
N_EXPERTS = 8
DEEPNORM_ALPHA = (2.0 * DEPTH) ** 0.25
LN_EPS = 1e-5
NEG = -1e30

VMEM_LIMIT_BYTES = 56 * 1024 * 1024

MLSTM_CHUNK = 256
TOKEN_TILE = 512
FFN_TOKEN_TILE = 512
FFN_SUB = 256
NA_QROWS = 4
NA_HALO = 4


def _cparams(*sem):
    return pltpu.CompilerParams(dimension_semantics=sem, vmem_limit_bytes=VMEM_LIMIT_BYTES)


def _dot(a, b):
    return jnp.dot(a, b, preferred_element_type=F32)


def _dot_nt(a, b):
    return lax.dot_general(a, b, (((1,), (1,)), ((), ())), preferred_element_type=F32)


def _onehot(mask):
    return jnp.where(mask, 1.0, 0.0).astype(BF16)


def _split3(x):
    hi = x.astype(BF16)
    r = x - hi.astype(F32)
    mid = r.astype(BF16)
    lo = (r - mid.astype(F32)).astype(BF16)
    return hi, mid, lo


def _layer_norm(z, g, b):
    mu = jnp.mean(z, axis=-1, keepdims=True)
    zc = z - mu
    var = jnp.mean(zc * zc, axis=-1, keepdims=True)
    return zc * lax.rsqrt(var + LN_EPS) * g + b


def _log_sigmoid(x):
    return jnp.minimum(x, 0.0) - jnp.log(1.0 + jnp.exp(-jnp.abs(x)))


def _seg_of_tile(i, tile, n_prompt_tok, sample_len):
    tok = i * tile
    return jnp.where(tok < n_prompt_tok, 0, 1 + (tok - n_prompt_tok) // sample_len)


def _adaln_kernel(c_ref, w_ref, b_ref, o_ref):
    c = c_ref[...]
    a = (c * jax.nn.sigmoid(c)).astype(BF16)
    o_ref[0] = _dot(a, w_ref[0].astype(BF16)) + b_ref[0]


def adaln(cvec, ada_w, ada_b):
    depth, d, n = ada_w.shape
    tn = 1536
    return pl.pallas_call(
        _adaln_kernel,
        grid=(depth, n // tn),
        in_specs=[
            pl.BlockSpec((8, d), lambda l, j: (0, 0)),
            pl.BlockSpec((1, d, tn), lambda l, j: (l, 0, j)),
            pl.BlockSpec((1, 1, tn), lambda l, j: (l, 0, j)),
        ],
        out_specs=pl.BlockSpec((1, 8, tn), lambda l, j: (l, 0, j)),
        out_shape=jax.ShapeDtypeStruct((depth, 8, n), F32),
        compiler_params=_cparams("arbitrary", "arbitrary"),
        name="adaln",
    )(cvec, ada_w, ada_b.reshape(depth, 1, n))


def _inproj_kernel(xp_ref, xs_ref, mod_ref, w_ref, wg_ref, wgt_ref, bg_ref, bgt_ref,
                   qkv_ref, o_ref, g_ref, gt_ref, *, n_prompt_tiles):
    x = jnp.where(pl.program_id(0) < n_prompt_tiles, xp_ref[...], xs_ref[...])
    u = (x * (1.0 + mod_ref[0, 1:2, :]) + mod_ref[0, 0:1, :]).astype(BF16)
    p = _dot(u, w_ref[...])
    nqkv = qkv_ref.shape[1]
    qkv_ref[...] = p[:, :nqkv].astype(BF16)
    o_ref[...] = p[:, nqkv:]
    g = _dot(u, wg_ref[...]) + bg_ref[...]
    col = lax.broadcasted_iota(jnp.int32, g.shape, 1)
    g_ref[...] = jnp.where(((col >> 2) & 1) == 1, _log_sigmoid(g), g)
    gt = _dot_nt(wgt_ref[...], u) + bgt_ref[...]
    row = lax.broadcasted_iota(jnp.int32, gt.shape, 0)
    gt_ref[...] = jnp.where(((row >> 2) & 1) == 1, _log_sigmoid(gt), gt)


def _split_token_specs(tile, d, n_prompt_tiles):
    return [pl.BlockSpec((tile, d), lambda i: (jnp.minimum(i, n_prompt_tiles - 1), 0)),
            pl.BlockSpec((tile, d), lambda i: (jnp.maximum(i - n_prompt_tiles, 0), 0))]


def mlstm_inproj(xp, xs, mod, w_main, w_gate, b_gate, sample_len):
    n_prompt_tok, d = xp.shape
    t = n_prompt_tok + xs.shape[0]
    n_main = w_main.shape[1]
    n_qkv = 2 * M_HK + D_MODEL
    ng = w_gate.shape[1]
    tt = TOKEN_TILE
    npt = n_prompt_tok // tt
    seg = functools.partial(_seg_of_tile, tile=tt, n_prompt_tok=n_prompt_tok, sample_len=sample_len)
    return pl.pallas_call(
        functools.partial(_inproj_kernel, n_prompt_tiles=npt),
        grid=(t // tt,),
        in_specs=_split_token_specs(tt, d, npt) + [
            pl.BlockSpec((1, 6, d), lambda i: (seg(i), 0, 0)),
            pl.BlockSpec((d, n_main), lambda i: (0, 0)),
            pl.BlockSpec((d, ng), lambda i: (0, 0)),
            pl.BlockSpec((ng, d), lambda i: (0, 0)),
            pl.BlockSpec((1, ng), lambda i: (0, 0)),
            pl.BlockSpec((ng, 1), lambda i: (0, 0)),
        ],
        out_specs=[
            pl.BlockSpec((tt, n_qkv), lambda i: (i, 0)),
            pl.BlockSpec((tt, n_main - n_qkv), lambda i: (i, 0)),
            pl.BlockSpec((tt, ng), lambda i: (i, 0)),
            pl.BlockSpec((ng, tt), lambda i: (0, i)),
        ],
        out_shape=[
            jax.ShapeDtypeStruct((t, n_qkv), BF16),
            jax.ShapeDtypeStruct((t, n_main - n_qkv), F32),
            jax.ShapeDtypeStruct((t, ng), F32),
            jax.ShapeDtypeStruct((ng, t), F32),
        ],
        compiler_params=_cparams("arbitrary"),
        name="mlstm_inproj",
    )(xp, xs, mod, w_main, w_gate, w_gate.T, b_gate.reshape(1, ng), b_gate.reshape(ng, 1))


def _mlstm_gate_sums(g, gt, backward):
    L = g.shape[0]
    row = lax.broadcasted_iota(jnp.int32, (L, L), 0)
    col = lax.broadcasted_iota(jnp.int32, (L, L), 1)
    lower = col <= row
    upper = col >= row
    tri_col = _onehot(upper if backward else lower)
    tri_row = _onehot(lower if backward else upper)
    ghi, gmid, glo = _split3(g)
    b_cols = _dot(tri_col, ghi) + _dot(tri_col, gmid) + _dot(tri_col, glo)
    thi, tmid, tlo = _split3(gt)
    b_rows = _dot(thi, tri_row) + _dot(tmid, tri_row) + _dot(tlo, tri_row)
    return b_cols, b_rows


def _mlstm_heads(items):
    scale = M_DK ** -0.5
    log_scale = math.log(scale)
    L = items[0][0].shape[0]
    row = lax.broadcasted_iota(jnp.int32, (L, L), 0)
    col = lax.broadcasted_iota(jnp.int32, (L, L), 1)
    masks = {False: col <= row, True: col >= row}
    n = range(len(items))
    qh, kh, vh, li_row, lf_row, b_row, b_col, backward, state = zip(*items)
    has_state = [st is not None for st in state]
    m_prev = [st[2] if st is not None else 0.0 for st in state]

    qk = [_dot_nt(qh[i], kh[i]) for i in n]
    qc = [_dot(qh[i], state[i][0].astype(BF16)) if has_state[i] else None for i in n]
    qn = [_dot_nt(qh[i], state[i][1].astype(BF16))[:, 0:1] if has_state[i] else None for i in n]
    total = [jnp.sum(lf_row[i], axis=1, keepdims=True) for i in n]
    d = [jnp.where(masks[backward[i]], b_col[i] + (li_row[i] - b_row[i] + log_scale), NEG) for i in n]
    inter = [b_col[i] + m_prev[i] if has_state[i] else b_col[i] for i in n]
    m_t = [jnp.maximum(inter[i], jnp.max(d[i], axis=1, keepdims=True) - log_scale) for i in n]
    p = [jnp.exp(d[i] - m_t[i]) for i in n]
    s = [qk[i] * p[i] for i in n]
    den = [jnp.sum(s[i], axis=1, keepdims=True) for i in n]
    num = [_dot(s[i].astype(BF16), vh[i]) for i in n]
    a = [jnp.exp(inter[i] - m_t[i]) * scale if has_state[i] else None for i in n]
    num = [a[i] * qc[i] + num[i] if has_state[i] else num[i] for i in n]
    den = [a[i] * qn[i] + den[i] if has_state[i] else den[i] for i in n]
    hh = [num[i] / jnp.maximum(jnp.abs(den[i]), jnp.exp(-m_t[i])) for i in n]

    g_row = [total[i] - b_row[i] + li_row[i] for i in n]
    m_new = [jnp.maximum(total[i] + m_prev[i], jnp.max(g_row[i], axis=1, keepdims=True)) for i in n]
    ws_row = [jnp.exp(g_row[i] - m_new[i]) for i in n]
    kt = [kh[i].astype(F32).T for i in n]
    c_new = [_dot((kt[i] * ws_row[i]).astype(BF16), vh[i]) for i in n]
    n_new = [_dot(jnp.broadcast_to(ws_row[i], (8, L)).astype(BF16), kh[i]) for i in n]
    a0 = [jnp.exp(total[i] + m_prev[i] - m_new[i]) if has_state[i] else None for i in n]
    c_new = [a0[i] * state[i][0] + c_new[i] if has_state[i] else c_new[i] for i in n]
    n_new = [a0[i] * state[i][1] + n_new[i] if has_state[i] else n_new[i] for i in n]
    return hh, c_new, n_new, m_new


def _head_norm_gate(ht, norm_g, ogate):
    mu = jnp.mean(ht, axis=-1, keepdims=True)
    hc = ht - mu
    var = jnp.mean(hc * hc, axis=-1, keepdims=True)
    return (hc * lax.rsqrt(var + LN_EPS) * norm_g * jax.nn.sigmoid(ogate)).astype(BF16)


def _head_operands(q_ref, k_ref, v_ref, h):
    return (q_ref[:, h * M_DK:(h + 1) * M_DK], k_ref[:, h * M_DK:(h + 1) * M_DK], v_ref[:, h * M_DV:(h + 1) * M_DV])


def _gate_operands(gt, b_rows, b_cols, direction, h):
    ci = direction * 8 + h
    cf = direction * 8 + 4 + h
    return gt[ci:ci + 1, :], gt[cf:cf + 1, :], b_rows[cf:cf + 1, :], b_cols[:, cf:cf + 1]


def _mlstm_prompt_kernel(q_ref, k_ref, v_ref, g_ref, gt_ref, o_ref, ng_ref, a_ref, cf_ref, nf_ref, mf_ref):
    g, gt = g_ref[...], gt_ref[...]
    sums = [_mlstm_gate_sums(g, gt, direction == 1) for direction in (0, 1)]
    keys = [(direction, h) for direction in (0, 1) for h in range(M_HEADS)]
    items = [_head_operands(q_ref, k_ref, v_ref, h)
             + _gate_operands(gt, sums[direction][1], sums[direction][0], direction, h) + (direction == 1, None)
             for direction, h in keys]
    hh, c_new, n_new, m_new = _mlstm_heads(items)
    for i, (direction, h) in enumerate(keys):
        cf_ref[0, 0, direction, h] = c_new[i]
        nf_ref[0, 0, direction, h] = n_new[i]
        mf_ref[0, 0, direction, h] = jnp.broadcast_to(m_new[i], mf_ref.shape[-2:])
    for h in range(M_HEADS):
        sl = slice(h * M_DV, (h + 1) * M_DV)
        a_ref[:, sl] = _head_norm_gate(hh[h] + hh[M_HEADS + h], ng_ref[:, sl], o_ref[:, sl])


def _mlstm_sample_kernel(*refs, direction, n_units):
    backward = direction == 1
    refs = list(refs)
    unit_refs = [tuple(refs.pop(0) for _ in range(5)) for _ in range(n_units)]
    c0_ref, n0_ref, m0_ref = refs.pop(0), refs.pop(0), refs.pop(0)
    if backward:
        hprev_ref = refs.pop(0)
        o_refs = [refs.pop(0) for _ in range(n_units)]
        ng_ref = refs.pop(0)
    out_ref, c_scr, n_scr, m_scr = refs

    @pl.when(pl.program_id(0) == 0)
    def _():
        c_scr[...] = c0_ref[...]
        n_scr[...] = n0_ref[...]
        m_scr[...] = m0_ref[...]

    keys, items = [], []
    for u, (q_ref, k_ref, v_ref, g_ref, gt_ref) in enumerate(unit_refs):
        gt = gt_ref[...]
        b_cols, b_rows = _mlstm_gate_sums(g_ref[...], gt, backward)
        for h in range(M_HEADS):
            state = (c_scr[u, h], n_scr[u, h], m_scr[u, h][0:1, 0:1])
            keys.append((u, h))
            items.append(_head_operands(q_ref, k_ref, v_ref, h) + _gate_operands(gt, b_rows, b_cols, direction, h)
                         + (backward, state))
    hh, c_new, n_new, m_new = _mlstm_heads(items)
    for i, (u, h) in enumerate(keys):
        c_scr[u, h] = c_new[i]
        n_scr[u, h] = n_new[i]
        m_scr[u, h] = jnp.broadcast_to(m_new[i], m_scr.shape[-2:])
        sl = slice(h * M_DV, (h + 1) * M_DV)
        if backward:
            out_ref[u, :, sl] = _head_norm_gate(hprev_ref[u, :, sl] + hh[i], ng_ref[:, sl], o_refs[u][:, sl])
        else:
            out_ref[u, :, sl] = hh[i]


def _mlstm_chunk_specs(block_of):
    L = MLSTM_CHUNK
    return [
        pl.BlockSpec((L, M_HK), lambda s: (block_of(s), 0)),
        pl.BlockSpec((L, M_HK), lambda s: (block_of(s), 1)),
        pl.BlockSpec((L, D_MODEL), lambda s: (block_of(s), 1)),
        pl.BlockSpec((L, 16), lambda s: (block_of(s), 0)),
        pl.BlockSpec((16, L), lambda s: (0, block_of(s))),
    ]


def mlstm_prompt(qkv, g, gt, ogate, norm_g, n_prompt):
    L = MLSTM_CHUNK
    state_blk = lambda *tail: pl.BlockSpec((1, 1, 2, M_HEADS) + tail, lambda s: (s, 0, 0, 0, 0, 0))
    return pl.pallas_call(
        _mlstm_prompt_kernel,
        grid=(n_prompt,),
        in_specs=_mlstm_chunk_specs(lambda s: s) + [
            pl.BlockSpec((L, D_MODEL), lambda s: (s, 0)),
            pl.BlockSpec((1, D_MODEL), lambda s: (0, 0)),
        ],
        out_specs=[pl.BlockSpec((L, D_MODEL), lambda s: (s, 0)),
                   state_blk(M_DK, M_DV), state_blk(8, M_DK), state_blk(8, 128)],
        out_shape=[
            jax.ShapeDtypeStruct((n_prompt * L, D_MODEL), BF16),
            jax.ShapeDtypeStruct((n_prompt, 1, 2, M_HEADS, M_DK, M_DV), F32),
            jax.ShapeDtypeStruct((n_prompt, 1, 2, M_HEADS, 8, M_DK), F32),
            jax.ShapeDtypeStruct((n_prompt, 1, 2, M_HEADS, 8, 128), F32),
        ],
        compiler_params=_cparams("arbitrary"),
        name="mlstm_prompt",
    )(qkv, qkv, qkv, g, gt, ogate, norm_g.reshape(1, D_MODEL))


def mlstm_sample(direction, qkv, g, gt, c0, n0, m0, first_block, chunks_per_sample,
                 hprev=None, ogate=None, norm_g=None):
    L = MLSTM_CHUNK
    backward = direction == 1
    cps = chunks_per_sample
    n_units = c0.shape[0]
    pos = (lambda s: cps - 1 - s) if backward else (lambda s: s)
    unit_block = [functools.partial(lambda s, u: first_block + u * cps + pos(s), u=u) for u in range(n_units)]
    whole = lambda a: pl.BlockSpec(a.shape, lambda s: (0,) * a.ndim)
    in_specs, args = [], []
    for u in range(n_units):
        in_specs += _mlstm_chunk_specs(unit_block[u])
        args += [qkv, qkv, qkv, g, gt]
    in_specs += [whole(c0), whole(n0), whole(m0)]
    args += [c0, n0, m0]
    if backward:
        in_specs += [pl.BlockSpec((n_units, L, D_MODEL), lambda s: (0, pos(s), 0))]
        in_specs += [pl.BlockSpec((L, D_MODEL), functools.partial(lambda s, u: (unit_block[u](s), 0), u=u))
                     for u in range(n_units)]
        in_specs += [pl.BlockSpec((1, D_MODEL), lambda s: (0, 0))]
        args += [hprev] + [ogate] * n_units + [norm_g.reshape(1, D_MODEL)]
    return pl.pallas_call(
        functools.partial(_mlstm_sample_kernel, direction=direction, n_units=n_units),
        grid=(cps,),
        in_specs=in_specs,
        out_specs=pl.BlockSpec((n_units, L, D_MODEL), lambda s: (0, pos(s), 0)),
        out_shape=jax.ShapeDtypeStruct((n_units, cps * L, D_MODEL), BF16 if backward else F32),
        scratch_shapes=[pltpu.VMEM(c0.shape, F32), pltpu.VMEM(n0.shape, F32), pltpu.VMEM(m0.shape, F32)],
        compiler_params=_cparams("arbitrary"),
        name="mlstm_sample_bwd" if backward else "mlstm_sample_fwd",
    )(*args)


def _outproj_kernel(*refs, n_prompt_tiles, a_split, x_split):
    refs = list(refs)
    is_prompt = pl.program_id(0) < n_prompt_tiles

    def take(split):
        if split:
            p_ref, s_ref = refs.pop(0), refs.pop(0)
            return jnp.where(is_prompt, p_ref[...], s_ref[...])
        return refs.pop(0)[...]

    a = take(a_split)
    x = take(x_split)
    mod_ref, w_ref, lng_ref, lnb_ref, xo_ref, uo_ref, up_ref = refs
    y = _dot(a, w_ref[...])
    z = DEEPNORM_ALPHA * x + mod_ref[0, 2:3, :] * y
    xn = _layer_norm(z, lng_ref[...], lnb_ref[...])
    xo_ref[...] = xn
    u = xn * (1.0 + mod_ref[0, 4:5, :]) + mod_ref[0, 3:4, :]
    uo_ref[...] = u.astype(BF16)
    up_ref[...] = _pack_bf16_pairs(u)


def mixer_outproj(a, x, mod, w, ln_g, ln_b, n_prompt_tok, sample_len):
    d = w.shape[0]
    tt = TOKEN_TILE
    npt = n_prompt_tok // tt
    seg = functools.partial(_seg_of_tile, tile=tt, n_prompt_tok=n_prompt_tok, sample_len=sample_len)

    def specs(v):
        if isinstance(v, tuple):
            return _split_token_specs(tt, d, npt), list(v), v[0].shape[0] + v[1].shape[0]
        return [pl.BlockSpec((tt, d), lambda i: (i, 0))], [v], v.shape[0]

    a_specs, a_args, t = specs(a)
    x_specs, x_args, _ = specs(x)
    return pl.pallas_call(
        functools.partial(_outproj_kernel, n_prompt_tiles=npt, a_split=isinstance(a, tuple),
                          x_split=isinstance(x, tuple)),
        grid=(t // tt,),
        in_specs=a_specs + x_specs + [
            pl.BlockSpec((1, 6, d), lambda i: (seg(i), 0, 0)),
            pl.BlockSpec((d, d), lambda i: (0, 0)),
            pl.BlockSpec((1, d), lambda i: (0, 0)),
            pl.BlockSpec((1, d), lambda i: (0, 0)),
        ],
        out_specs=[pl.BlockSpec((tt, d), lambda i: (i, 0)), pl.BlockSpec((tt, d), lambda i: (i, 0)),
                   pl.BlockSpec((tt, d // 2), lambda i: (i, 0))],
        out_shape=[jax.ShapeDtypeStruct((t, d), F32), jax.ShapeDtypeStruct((t, d), BF16),
                   jax.ShapeDtypeStruct((t, d // 2), jnp.int32)],
        compiler_params=_cparams("arbitrary"),
        name="mixer_outproj",
    )(*a_args, *x_args, mod, w, ln_g.reshape(1, d), ln_b.reshape(1, d))


def _swiglu_tile(x, wi, wo, dff):
    sub = FFN_SUB
    proj = lambda j: (_dot(x, wi(slice(j * sub, (j + 1) * sub))), _dot(x, wi(slice(dff + j * sub, dff + (j + 1) * sub))))
    acts, cur = [], proj(0)
    for j in range(dff // sub):
        nxt = proj(j + 1) if (j + 1) * sub < dff else None
        gp, up = cur
        acts.append((gp * jax.nn.sigmoid(gp) * up).astype(BF16))
        cur = nxt
    return _dot(jnp.concatenate(acts, axis=1), wo(slice(0, dff)))


def _mixer_ffn_kernel(ap_ref, as_ref, xp_ref, xs_ref, mod_ref, modn_ref, wmix_ref, wi_ref, wo_ref, ln_ref,
                      xo_ref, uo_ref, *, n_prompt_tiles):
    is_prompt = pl.program_id(0) < n_prompt_tiles
    a = jnp.where(is_prompt, ap_ref[...], as_ref[...])
    x = jnp.where(is_prompt, xp_ref[...], xs_ref[...])
    z = DEEPNORM_ALPHA * x + mod_ref[0, 2:3, :] * _dot(a, wmix_ref[...])
    x1 = _layer_norm(z, ln_ref[0:1, :], ln_ref[1:2, :])
    u1 = (x1 * (1.0 + mod_ref[0, 4:5, :]) + mod_ref[0, 3:4, :]).astype(BF16)
    f = _swiglu_tile(u1, lambda c: wi_ref[:, c], lambda r: wo_ref[r, :], wo_ref.shape[0])
    x2 = _layer_norm(DEEPNORM_ALPHA * x1 + mod_ref[0, 5:6, :] * f, ln_ref[2:3, :], ln_ref[3:4, :])
    xo_ref[...] = x2
    uo_ref[...] = (x2 * (1.0 + modn_ref[0, 1:2, :]) + modn_ref[0, 0:1, :]).astype(BF16)


def _resident(shape):
    return pl.BlockSpec(shape, lambda i: (0,) * len(shape), pipeline_mode=pl.Buffered(1))


def mixer_ffn(a_p, a_s, xp, xs, mod, mod_next, w_mix, w_in, w_out, ln_g, ln_b, sample_len):
    n_prompt_tok, d = xp.shape
    t = n_prompt_tok + xs.shape[0]
    tm = FFN_TOKEN_TILE
    npt = n_prompt_tok // tm
    seg = functools.partial(_seg_of_tile, tile=tm, n_prompt_tok=n_prompt_tok, sample_len=sample_len)
    ln = jnp.stack([ln_g[0], ln_b[0], ln_g[1], ln_b[1]])
    return pl.pallas_call(
        functools.partial(_mixer_ffn_kernel, n_prompt_tiles=npt),
        grid=(t // tm,),
        in_specs=_split_token_specs(tm, d, npt) + _split_token_specs(tm, d, npt) + [
            pl.BlockSpec((1, 6, d), lambda i: (seg(i), 0, 0)),
            pl.BlockSpec((1, 6, d), lambda i: (seg(i), 0, 0)),
            _resident(w_mix.shape), _resident(w_in.shape), _resident(w_out.shape), _resident(ln.shape),
        ],
        out_specs=[pl.BlockSpec((tm, d), lambda i: (i, 0)), pl.BlockSpec((tm, d), lambda i: (i, 0))],
        out_shape=[jax.ShapeDtypeStruct((t, d), F32), jax.ShapeDtypeStruct((t, d), BF16)],
        compiler_params=_cparams("arbitrary"),
        name="mixer_ffn",
    )(a_p, a_s, xp, xs, mod, mod_next, w_mix, w_in, w_out, ln)


def _modulation_tables(c, c_ctx, ada_w, ada_b):
    n_s = c.shape[0]
    cvec = jnp.concatenate([c_ctx[None, :], c, jnp.zeros((8 - 1 - n_s, c.shape[1]), F32)], axis=0)
    mod = adaln(cvec, ada_w, ada_b)
    return mod[:, :1 + n_s, :].reshape(ada_w.shape[0], 1 + n_s, 6, c.shape[1])


def _layer0(xp, xs, mod0, mod1, state_c, state_n, state_m, ln_g, ln_b, w_in, b_gates, norm_g, w_out,
            ffn_w_in, ffn_w_out, n_prompt, prompt_len, n_sample, sample_len):
    n_prompt_tok = n_prompt * prompt_len
    n_main = 2 * M_HK + 2 * D_MODEL
    qkv, ogate, g, gt = mlstm_inproj(xp, xs, mod0, w_in[:, :n_main].astype(BF16), w_in[:, n_main:].astype(BF16),
                                     b_gates.reshape(-1), sample_len)
    npc = n_prompt_tok // MLSTM_CHUNK
    cps = sample_len // MLSTM_CHUNK
    rep = lambda a: jnp.broadcast_to(a[..., None, :], a.shape[:-1] + (8, a.shape[-1]))
    n0 = rep(state_n)
    m0 = jnp.broadcast_to(state_m[..., None, None], state_m.shape + (8, 128))
    a_p, new_c, nf, mf = mlstm_prompt(qkv, g, gt, ogate, norm_g, n_prompt)
    hf = mlstm_sample(0, qkv, g, gt, state_c[:, 0], n0[:, 0], m0[:, 0], npc, cps)
    a_s = mlstm_sample(1, qkv, g, gt, state_c[:, 1], n0[:, 1], m0[:, 1], npc, cps,
                       hprev=hf, ogate=ogate, norm_g=norm_g)
    x2, u2 = mixer_ffn(a_p, a_s.reshape(-1, D_MODEL), xp, xs, mod0, mod1, w_out.astype(BF16),
                       ffn_w_in.astype(BF16), ffn_w_out.astype(BF16), ln_g, ln_b, sample_len)
    return x2, u2, new_c, nf[..., 0, :], mf[..., 0, 0]


def _qkv_kernel(u_ref, w_ref, qkv_ref, k_ref, v_ref, *, n_prompt_tiles):
    p = _dot(u_ref[...], w_ref[...])
    qkv_ref[...] = p.astype(BF16)
    tt, d = u_ref.shape

    @pl.when(pl.program_id(0) < n_prompt_tiles)
    def _():
        for h in range(NA_HEADS):
            rows = pl.ds(h, tt, stride=NA_HEADS)
            k_ref[rows, :] = p[:, d + h * NA_DH:d + (h + 1) * NA_DH]
            v_ref[rows, :] = p[:, 2 * d + h * NA_DH:2 * d + (h + 1) * NA_DH]


def na_qkv(u, w, n_prompt_tok):
    t, d = u.shape
    tt = TOKEN_TILE
    npt = n_prompt_tok // tt
    kv_spec = pl.BlockSpec((tt * NA_HEADS, NA_DH), lambda i: (jnp.minimum(i, npt - 1), 0))
    kv_shape = jax.ShapeDtypeStruct((n_prompt_tok * NA_HEADS, NA_DH), F32)
    return pl.pallas_call(
        functools.partial(_qkv_kernel, n_prompt_tiles=npt),
        grid=(t // tt,),
        in_specs=[pl.BlockSpec((tt, d), lambda i: (i, 0)), pl.BlockSpec((d, 3 * d), lambda i: (0, 0))],
        out_specs=[pl.BlockSpec((tt, 3 * d), lambda i: (i, 0)), kv_spec, kv_spec],
        out_shape=[jax.ShapeDtypeStruct((t, 3 * d), BF16), kv_shape, kv_shape],
        compiler_params=_cparams("arbitrary"),
        name="na_qkv",
    )(u, w)


def _head_pair_masks(scale):
    assert math.frexp(scale)[0] == 0.5
    lane = lax.broadcasted_iota(jnp.int32, (1, 2 * NA_DH), 1)
    first, second = lane < NA_DH, lane >= NA_DH
    return tuple((jnp.where(sel, scale, 0.0).astype(BF16), _onehot(sel)) for sel in (first, second))


def _softmax_pv(score_blocks, value_blocks):
    mx = None
    for s in score_blocks:
        bm = jnp.max(s, axis=1, keepdims=True)
        mx = bm if mx is None else jnp.maximum(mx, bm)
    acc, denom = None, None
    for s, v in zip(score_blocks, value_blocks):
        p = jnp.exp(s - mx)
        ps = jnp.sum(p, axis=1, keepdims=True)
        pv = _dot(p.astype(BF16), v)
        acc = pv if acc is None else acc + pv
        denom = ps if denom is None else denom + ps
    return acc / denom


def _ctx_attn_kernel(q_ref, k_ref, v_ref, o_ref):
    masks = _head_pair_masks(NA_DH ** -0.5)
    for hp in range(NA_HEADS // 2):
        sl = slice(hp * 2 * NA_DH, (hp + 1) * 2 * NA_DH)
        q2, k2, v2 = q_ref[:, sl], k_ref[:, sl], v_ref[:, sl]
        o2 = None
        for qmsk, vmsk in masks:
            s = _dot_nt(q2 * qmsk, k2)
            o = _softmax_pv([s], [v2 * vmsk])
            o2 = o if o2 is None else o2 + o
        o_ref[:, sl] = o2.astype(o_ref.dtype)


def context_attention(qkv, n_prompt, prompt_len):
    t = n_prompt * prompt_len
    d = D_MODEL
    return pl.pallas_call(
        _ctx_attn_kernel,
        grid=(n_prompt,),
        in_specs=[
            pl.BlockSpec((prompt_len, d), lambda b: (b, 0)),
            pl.BlockSpec((prompt_len, d), lambda b: (b, 1)),
            pl.BlockSpec((prompt_len, d), lambda b: (b, 2)),
        ],
        out_specs=pl.BlockSpec((prompt_len, d), lambda b: (b, 0)),
        out_shape=jax.ShapeDtypeStruct((t, d), BF16),
        compiler_params=_cparams("arbitrary"),
        name="context_attention",
    )(qkv, qkv, qkv)


def _na_kernel(tile_ref, q_ref, kp_ref, kc_ref, kn_ref, vp_ref, vc_ref, vn_ref, kctx_ref, vctx_ref, bias_ref,
               o_ref):
    j = pl.program_id(1)
    nblk = pl.num_programs(1)
    m = q_ref.shape[0]
    halo = NA_HALO * GRID_W
    n_pairs = (NA_QROWS + 2 * NA_HALO) // 2
    variant = jnp.where(j == 0, 0, jnp.where(j == nblk - 1, 2, 1))
    kinds = [[tile_ref[(variant * NA_QROWS + rq) * n_pairs + kp] for kp in range(n_pairs)]
             for rq in range(NA_QROWS)]

    def bias_of(head):
        return jnp.concatenate(
            [jnp.concatenate([bias_ref[kinds[rq][kp], head] for kp in range(n_pairs)], axis=1)
             for rq in range(NA_QROWS)], axis=0)

    masks = _head_pair_masks(NA_DH ** -0.5)
    for hp in range(NA_HEADS // 2):
        sl = slice(hp * 2 * NA_DH, (hp + 1) * 2 * NA_DH)
        q2 = q_ref[:, sl]
        kloc = jnp.concatenate([kp_ref[m - halo:, sl], kc_ref[:, sl], kn_ref[:halo, sl]], axis=0)
        vloc = jnp.concatenate([vp_ref[m - halo:, sl], vc_ref[:, sl], vn_ref[:halo, sl]], axis=0)
        kctx = kctx_ref[0][:, sl]
        vctx = vctx_ref[0][:, sl]
        o2 = None
        for half, (qmsk, vmsk) in enumerate(masks):
            qm = q2 * qmsk
            s_loc = _dot_nt(qm, kloc) + bias_of(2 * hp + half)
            s_ctx = _dot_nt(qm, kctx)
            o = _softmax_pv([s_loc, s_ctx], [vloc * vmsk, vctx * vmsk])
            o2 = o if o2 is None else o2 + o
        o_ref[:, sl] = o2.astype(o_ref.dtype)


def _na_bias_kernel(lc_ref, rc_ref, lv_ref, rv_ref, o_ref):
    nc, npos = lc_ref.shape[2], o_ref.shape[2]
    pair_shift = (2 * GRID_W).bit_length() - 1
    c = lax.broadcasted_iota(jnp.int32, (nc, npos), 0)
    pos = lax.broadcasted_iota(jnp.int32, (nc, npos), 1)
    qc = pos >> pair_shift
    kc = pos & (GRID_W - 1)
    c_start = jnp.clip(qc - NA_COLS // 2, 0, GRID_W - NA_COLS)
    col_in = jnp.logical_and(kc >= c_start, kc < c_start + NA_COLS)
    dc = jnp.clip(kc - qc + NA_COLS - 1, 0, 2 * NA_COLS - 2)
    sel = _onehot(jnp.logical_and(c == dc, col_in))

    def pick(row_ref):
        r1, r2, r3 = _split3(row_ref[0])
        return _dot(r1, sel) + _dot(r2, sel) + _dot(r3, sel)

    right = (pos[0:1, :] & GRID_W) != 0
    val = jnp.where(right, pick(rc_ref), pick(lc_ref))
    visible = jnp.where(right, rv_ref[0], lv_ref[0]) > 0.0
    o_ref[0] = jnp.where(jnp.logical_and(visible, col_in[0:1, :]), val, NEG)


def _na_bias_plan():
    ndr = 2 * NA_ROWS - 1
    assert NA_QROWS >= NA_ROWS // 2 and NA_HALO == NA_ROWS // 2 and (NA_QROWS + 2 * NA_HALO) % 2 == 0
    ok = lambda dr: dr if dr is not None and 0 <= dr < ndr else None
    kinds = [(ok(dr), ok(dr + 1)) for dr in range(-1, ndr)]
    first_key_row = (lambda rq: NA_HALO, lambda rq: rq, lambda rq: NA_QROWS + NA_HALO - NA_ROWS)
    table = []
    for first in first_key_row:
        for rq in range(NA_QROWS):
            for kp in range((NA_QROWS + 2 * NA_HALO) // 2):
                drs = []
                for kk in (2 * kp, 2 * kp + 1):
                    visible = first(rq) <= kk < first(rq) + NA_ROWS
                    drs.append(kk - rq + NA_ROWS - 1 - NA_HALO if visible else None)
                kind = (drs[0], drs[1])
                if kind not in kinds:
                    kinds.append(kind)
                table.append(kinds.index(kind))
    return kinds, table


def _na_bias_tiles(rpb):
    nh, ndr, ndc = rpb.shape
    nc = 32
    kinds, table = _na_bias_plan()
    nk = len(kinds)
    rpb_p = jnp.pad(rpb, ((0, 0), (0, 0), (0, nc - ndc)))
    zero = jnp.zeros((nh, nc), F32)
    lc = jnp.stack([zero if l is None else rpb_p[:, l] for l, _ in kinds])
    rc = jnp.stack([zero if r is None else rpb_p[:, r] for _, r in kinds])
    vis = lambda flags: jnp.broadcast_to(jnp.asarray(flags, F32)[:, None, None], (nk, nh, 1))
    lv = vis([0.0 if l is None else 1.0 for l, _ in kinds])
    rv = vis([0.0 if r is None else 1.0 for _, r in kinds])
    npos = GRID_W * 2 * GRID_W
    row_spec = pl.BlockSpec((1, nh, nc), lambda t: (t, 0, 0))
    flag_spec = pl.BlockSpec((1, nh, 1), lambda t: (t, 0, 0))
    tiles = pl.pallas_call(
        _na_bias_kernel,
        grid=(nk,),
        in_specs=[row_spec, row_spec, flag_spec, flag_spec],
        out_specs=pl.BlockSpec((1, nh, npos), lambda t: (t, 0, 0)),
        out_shape=jax.ShapeDtypeStruct((nk, nh, npos), F32),
        compiler_params=_cparams("arbitrary"),
        name="na_bias",
    )(lc, rc, lv, rv)
    return tiles.reshape(nk, nh, GRID_W, 2 * GRID_W), jnp.asarray(table, jnp.int32)


def neighbourhood_attention(qkv, k_ctx, v_ctx, rpb, n_prompt_tok, n_sample, sample_len):
    d = D_MODEL
    m = NA_QROWS * GRID_W
    nblk = sample_len // m
    assert nblk >= 3
    base = n_prompt_tok // m
    bias, table = _na_bias_tiles(rpb)

    def blk(col, off):
        return pl.BlockSpec((m, d), lambda b, j, tbl: (base + b * nblk + jnp.clip(j + off, 0, nblk - 1), col))

    return pl.pallas_call(
        _na_kernel,
        grid_spec=pltpu.PrefetchScalarGridSpec(
            num_scalar_prefetch=1,
            grid=(n_sample, nblk),
            in_specs=[
                blk(0, 0), blk(1, -1), blk(1, 0), blk(1, 1), blk(2, -1), blk(2, 0), blk(2, 1),
                pl.BlockSpec((1,) + k_ctx.shape[1:], lambda b, j, tbl: (b, 0, 0)),
                pl.BlockSpec((1,) + v_ctx.shape[1:], lambda b, j, tbl: (b, 0, 0)),
                pl.BlockSpec(bias.shape, lambda b, j, tbl: (0, 0, 0, 0)),
            ],
            out_specs=pl.BlockSpec((m, d), lambda b, j, tbl: (b * nblk + j, 0)),
        ),
        out_shape=jax.ShapeDtypeStruct((n_sample * sample_len, d), BF16),
        compiler_params=_cparams("arbitrary", "arbitrary"),
        name="neighbourhood_attention",
    )(table, qkv, qkv, qkv, qkv, qkv, qkv, qkv, k_ctx, v_ctx, bias)


MOE_BLOCK = 256
MOE_ROW_TILE = 512
SC_CORES = 2
SC_SUBCORES = 16
SC_GATHER_CHUNK = 128

def _router_kernel(u_ref, wt_ref, comb_ref, pos_ref, total_ref, carry_scr):
    i = pl.program_id(0)

    @pl.when(i == 0)
    def _():
        carry_scr[...] = jnp.zeros_like(carry_scr)

    u = u_ref[...]
    w1, w2, w3 = _split3(wt_ref[...])
    lg = _dot_nt(w1, u) + _dot_nt(w2, u) + _dot_nt(w3, u)
    ne, nt = lg.shape
    e = lax.broadcasted_iota(jnp.int32, lg.shape, 0)
    m1 = jnp.max(lg, axis=0, keepdims=True)
    i1 = jnp.min(jnp.where(lg == m1, e, ne), axis=0, keepdims=True)
    sel1 = e == i1
    lg2 = jnp.where(sel1, -jnp.inf, lg)
    m2 = jnp.max(lg2, axis=0, keepdims=True)
    i2 = jnp.min(jnp.where(lg2 == m2, e, ne), axis=0, keepdims=True)
    sel2 = e == i2
    ex = jnp.exp(m2 - m1)
    wa = 1.0 / (1.0 + ex)
    wb = ex / (1.0 + ex)
    comb = jnp.where(sel1, wa, jnp.where(sel2, wb, 0.0))
    assign = jnp.logical_or(sel1, sel2)
    a = jnp.where(assign, 1.0, 0.0)
    row = lax.broadcasted_iota(jnp.int32, (nt, nt), 0)
    col = lax.broadcasted_iota(jnp.int32, (nt, nt), 1)
    tri = _onehot(row < col)
    carry = carry_scr[:, 0:1]
    rank = _dot(a.astype(BF16), tri) + carry
    pos = jnp.where(assign, rank, -1.0)
    comb_ref[...] = comb
    pos_ref[...] = pos
    carry_new = carry + jnp.sum(a, axis=1, keepdims=True)
    carry_scr[...] = jnp.broadcast_to(carry_new, carry_scr.shape)
    total_ref[...] = jnp.broadcast_to(carry_new, carry_scr.shape)


def moe_router(u, w_router):
    t, d = u.shape
    ne = w_router.shape[1]
    tb = MOE_BLOCK
    nb = t // tb
    return pl.pallas_call(
        _router_kernel,
        grid=(nb,),
        in_specs=[pl.BlockSpec((tb, d), lambda i: (i, 0)), pl.BlockSpec((ne, d), lambda i: (0, 0))],
        out_specs=[
            pl.BlockSpec((ne, tb), lambda i: (0, i)),
            pl.BlockSpec((ne, tb), lambda i: (0, i)),
            pl.BlockSpec((ne, 128), lambda i: (0, 0)),
        ],
        out_shape=[
            jax.ShapeDtypeStruct((ne, t), F32),
            jax.ShapeDtypeStruct((ne, t), F32),
            jax.ShapeDtypeStruct((ne, 128), F32),
        ],
        scratch_shapes=[pltpu.VMEM((ne, 128), F32)],
        compiler_params=_cparams("arbitrary"),
        name="moe_router",
    )(u, w_router.T)


def _moe_plan(totals, pos, comb, n_tok):
    ne = totals.shape[0]
    tm = MOE_ROW_TILE
    n_tiles = (2 * n_tok) // tm + ne
    tiles_per = (totals + tm - 1) // tm
    tile_end = jnp.cumsum(tiles_per)
    n_used = tile_end[-1]
    base_rows = (tile_end - tiles_per) * tm
    ti = jnp.arange(n_tiles, dtype=jnp.int32)
    tile_expert = jnp.minimum(jnp.sum((ti[:, None] >= tile_end[None, :]).astype(jnp.int32), axis=1), ne - 1)
    last_expert = tile_expert[jnp.maximum(n_used - 1, 0)]
    tile_expert = jnp.where(ti < n_used, tile_expert, last_expert)
    routed = pos >= 0.0
    row = base_rows[:, None] + pos.astype(jnp.int32)
    row_a = jnp.min(jnp.where(routed, row, n_tiles * tm), axis=0)
    row_b = jnp.max(jnp.where(routed, row, -1), axis=0)
    w_a = jnp.sum(jnp.where(jnp.logical_and(routed, row == row_a[None, :]), comb, 0.0), axis=0)
    w_b = jnp.sum(jnp.where(jnp.logical_and(routed, row == row_b[None, :]), comb, 0.0), axis=0)
    tok = jnp.arange(n_tok, dtype=jnp.int32)
    src_token = jnp.zeros((n_tiles * tm,), jnp.int32).at[jnp.concatenate([row_a, row_b])].set(
        jnp.concatenate([tok, tok]), unique_indices=True)
    return dict(tile_expert=tile_expert.astype(jnp.int32), n_used=n_used.reshape(1).astype(jnp.int32),
                src_token=src_token, token_rows=jnp.concatenate([row_a, row_b]).astype(jnp.int32),
                token_weights=jnp.stack([w_a, w_b], axis=1))


def _pack_bf16_pairs(x):
    half = x.shape[1] // 2
    bits = pltpu.bitcast(x.astype(BF16).astype(F32), jnp.int32)
    return (bits[:, :half] & jnp.int32(-65536)) | lax.shift_right_logical(bits[:, half:], jnp.int32(16))


def _unpack_bf16_pairs(p):
    hi = pltpu.bitcast(p & jnp.int32(-65536), F32)
    lo = pltpu.bitcast(lax.shift_left(p, jnp.int32(16)), F32)
    return jnp.concatenate([hi, lo], axis=1)


def sc_gather_rows(table, idx):
    _, w = table.shape
    b = idx.shape[0]
    workers = SC_CORES * SC_SUBCORES
    ch = SC_GATHER_CHUNK
    n_chunks = b // (workers * ch)
    assert n_chunks * workers * ch == b
    mesh = plsc.VectorSubcoreMesh(core_axis_name="c", subcore_axis_name="s", num_cores=SC_CORES,
                                  num_subcores=SC_SUBCORES)

    @functools.partial(
        pl.kernel, mesh=mesh, out_type=jax.ShapeDtypeStruct((b, w), table.dtype),
        scratch_types=[pltpu.VMEM((ch,), jnp.int32), pltpu.VMEM((ch, w), table.dtype), pltpu.SemaphoreType.DMA])
    def gather(table_hbm, idx_hbm, out_hbm, idx_v, rows_v, sem):
        worker = lax.axis_index("s") * SC_CORES + lax.axis_index("c")

        @pl.loop(0, n_chunks)
        def _(c):
            base = (worker * n_chunks + c) * ch
            pltpu.sync_copy(idx_hbm.at[pl.ds(base, ch)], idx_v)
            pltpu.async_copy(table_hbm.at[idx_v], rows_v, sem).wait()
            pltpu.sync_copy(rows_v, out_hbm.at[pl.ds(base, ch)])

    return gather(table, idx)


def _gffn_kernel(te_ref, nu_ref, x_ref, wi_ref, wo_ref, y_ref):
    used = pl.program_id(0) < nu_ref[0]

    @pl.when(used)
    def _():
        x = _unpack_bf16_pairs(x_ref[...]).astype(BF16)
        y = _swiglu_tile(x, lambda c: wi_ref[0, :, c], lambda r: wo_ref[0, r, :], wo_ref.shape[1])
        y_ref[...] = _pack_bf16_pairs(y)

    @pl.when(jnp.logical_not(used))
    def _():
        y_ref[...] = jnp.zeros_like(y_ref)


def moe_grouped_ffn(xs, w_in, w_out, plan):
    nr, half = xs.shape
    ne, dff, d = w_out.shape
    tm = MOE_ROW_TILE
    return pl.pallas_call(
        _gffn_kernel,
        grid_spec=pltpu.PrefetchScalarGridSpec(
            num_scalar_prefetch=2,
            grid=(nr // tm,),
            in_specs=[
                pl.BlockSpec((tm, half), lambda i, te, nu: (jnp.minimum(i, nu[0] - 1), 0)),
                pl.BlockSpec((1, d, 2 * dff), lambda i, te, nu: (te[i], 0, 0)),
                pl.BlockSpec((1, dff, d), lambda i, te, nu: (te[i], 0, 0)),
            ],
            out_specs=pl.BlockSpec((tm, half), lambda i, te, nu: (i, 0)),
        ),
        out_shape=jax.ShapeDtypeStruct((nr, half), jnp.int32),
        compiler_params=_cparams("arbitrary"),
        name="moe_grouped_ffn",
    )(plan["tile_expert"], plan["n_used"], xs, w_in, w_out)


def _combine_kernel(ya_ref, yb_ref, w_ref, x_ref, mod_ref, lng_ref, lnb_ref, op_ref, os_ref, *, n_prompt_tiles):
    i = pl.program_id(0)
    w = w_ref[...]
    moe = w[:, 0:1] * _unpack_bf16_pairs(ya_ref[...]) + w[:, 1:2] * _unpack_bf16_pairs(yb_ref[...])
    z = DEEPNORM_ALPHA * x_ref[...] + mod_ref[0, 5:6, :] * moe
    xn = _layer_norm(z, lng_ref[...], lnb_ref[...])

    @pl.when(i < n_prompt_tiles)
    def _():
        op_ref[...] = xn

    @pl.when(i >= n_prompt_tiles)
    def _():
        os_ref[...] = xn


def moe_combine(y_tok, weights, x, mod, ln_g, ln_b, n_prompt_tok, sample_len):
    t, d = x.shape
    tt = TOKEN_TILE
    nt = t // tt
    npt = n_prompt_tok // tt
    seg = functools.partial(_seg_of_tile, tile=tt, n_prompt_tok=n_prompt_tok, sample_len=sample_len)
    return pl.pallas_call(
        functools.partial(_combine_kernel, n_prompt_tiles=npt),
        grid=(nt,),
        in_specs=[
            pl.BlockSpec((tt, d // 2), lambda i: (i, 0)),
            pl.BlockSpec((tt, d // 2), lambda i: (nt + i, 0)),
            pl.BlockSpec((tt, 2), lambda i: (i, 0)),
            pl.BlockSpec((tt, d), lambda i: (i, 0)),
            pl.BlockSpec((1, 6, d), lambda i: (seg(i), 0, 0)),
            pl.BlockSpec((1, d), lambda i: (0, 0)),
            pl.BlockSpec((1, d), lambda i: (0, 0)),
        ],
        out_specs=[
            pl.BlockSpec((tt, d), lambda i: (jnp.minimum(i, npt - 1), 0)),
            pl.BlockSpec((tt, d), lambda i: (jnp.maximum(i - npt, 0), 0)),
        ],
        out_shape=[jax.ShapeDtypeStruct((n_prompt_tok, d), F32), jax.ShapeDtypeStruct((t - n_prompt_tok, d), F32)],
        compiler_params=_cparams("arbitrary"),
        name="moe_combine",
    )(y_tok, y_tok, weights, x, mod, ln_g.reshape(1, d), ln_b.reshape(1, d))


def _layer1(x, u, mod1, cache_k, cache_v, ln_g, ln_b, w_qkv, rpb, w_out, w_router, moe_w_in, moe_w_out,
            n_prompt, prompt_len, n_sample, sample_len):
    n_prompt_tok = n_prompt * prompt_len
    d = D_MODEL
    qkv, k_p, v_p = na_qkv(u, w_qkv.astype(BF16), n_prompt_tok)
    attn_p = context_attention(qkv, n_prompt, prompt_len)
    k_ctx = cache_k.reshape(n_sample, -1, d).astype(BF16)
    v_ctx = cache_v.reshape(n_sample, -1, d).astype(BF16)
    attn_s = neighbourhood_attention(qkv, k_ctx, v_ctx, rpb, n_prompt_tok, n_sample, sample_len)
    x1, u1, u1_packed = mixer_outproj((attn_p, attn_s), x, mod1, w_out.astype(BF16), ln_g[0], ln_b[0],
                                      n_prompt_tok, sample_len)
    comb, pos, totals = moe_router(u1, w_router)
    plan = _moe_plan(totals[:, 0].astype(jnp.int32), pos, comb, x.shape[0])
    xs = sc_gather_rows(u1_packed, plan["src_token"])
    y = moe_grouped_ffn(xs, moe_w_in.astype(BF16), moe_w_out.astype(BF16), plan)
    y_tok = sc_gather_rows(y, plan["token_rows"])
    y_p, y_s = moe_combine(y_tok, plan["token_weights"], x1, mod1, ln_g[1], ln_b[1], n_prompt_tok, sample_len)
    return y_p, y_s, k_p, v_p


def kernel(x_prompt, x_sample, state_mlstm_C, state_mlstm_n, state_mlstm_m, cache_na_k, cache_na_v, c, c_ctx,
           ada_w, ada_b, ln_g, ln_b, mlstm_w_in, mlstm_b_gates, mlstm_norm_g, mlstm_w_out, na_w_qkv, na_rpb,
           na_w_out, ffn_w_in, ffn_w_out, moe_w_router, moe_w_in, moe_w_out):
    n_prompt, prompt_len, d = x_prompt.shape
    n_sample, sample_len, _ = x_sample.shape
    assert d == D_MODEL and prompt_len == MLSTM_CHUNK and sample_len % MLSTM_CHUNK == 0
    assert ada_w.shape[0] == DEPTH == 2
    mod = _modulation_tables(c, c_ctx, ada_w, ada_b)
    x2, u2, new_c, new_n, new_m = _layer0(
        x_prompt.reshape(-1, d), x_sample.reshape(-1, d), mod[0], mod[1], state_mlstm_C[:, 0], state_mlstm_n[:, 0], state_mlstm_m[:, 0], ln_g[0], ln_b[0],
        mlstm_w_in[0], mlstm_b_gates[0], mlstm_norm_g[0], mlstm_w_out[0], ffn_w_in[0], ffn_w_out[0],
        n_prompt, prompt_len, n_sample, sample_len)
    y_p, y_s, k_p, v_p = _layer1(
        x2, u2, mod[1], cache_na_k[:, 0], cache_na_v[:, 0], ln_g[1], ln_b[1], na_w_qkv[0], na_rpb[0], na_w_out[0],
        moe_w_router[0], moe_w_in[0], moe_w_out[0], n_prompt, prompt_len, n_sample, sample_len)
    kv_shape = (n_prompt, 1, prompt_len, NA_HEADS, NA_DH)
    return (y_p.reshape(x_prompt.shape), y_s.reshape(x_sample.shape), new_c, new_n, new_m,
            k_p.reshape(kv_shape), v_p.reshape(kv_shape))
```

```python
import functools
import math

import jax
import jax.numpy as jnp
from jax import lax
from jax.experimental import pallas as pl
from jax.experimental.pallas import tpu as pltpu
from jax.experimental.pallas import tpu_sc as plsc

F32 = jnp.float32
BF16 = jnp.bfloat16

D_MODEL = 1024
DEPTH = 2
GRID_W = 64
M_HEADS = 4
M_DK = 128
M_DV = 256
M_HK = M_HEADS * M_DK
NA_HEADS = 16
NA_DH = 64
NA_ROWS = 8
NA_COLS = 16
D_FF = 2816
N_EXPERTS = 8
DEEPNORM_ALPHA = (2.0 * DEPTH) ** 0.25
LN_EPS = 1e-5
NEG = -1e30

VMEM_LIMIT_BYTES = 56 * 1024 * 1024

MLSTM_CHUNK = 256
TOKEN_TILE = 512
FFN_TOKEN_TILE = 512
FFN_SUB = 256
NA_QROWS = 4
NA_HALO = 4


def _cparams(*sem):
    return pltpu.CompilerParams(dimension_semantics=sem, vmem_limit_bytes=VMEM_LIMIT_BYTES)


def _dot(a, b):
    return jnp.dot(a, b, preferred_element_type=F32)


def _dot_nt(a, b):
    return lax.dot_general(a, b, (((1,), (1,)), ((), ())), preferred_element_type=F32)


def _onehot(mask):
    return jnp.where(mask, 1.0, 0.0).astype(BF16)


def _split3(x):
    hi = x.astype(BF16)
    r = x - hi.astype(F32)
    mid = r.astype(BF16)
    lo = (r - mid.astype(F32)).astype(BF16)
    return hi, mid, lo


def _layer_norm(z, g, b):
    mu = jnp.mean(z, axis=-1, keepdims=True)
    zc = z - mu
    var = jnp.mean(zc * zc, axis=-1, keepdims=True)
    return zc * lax.rsqrt(var + LN_EPS) * g + b


def _log_sigmoid(x):
    return jnp.minimum(x, 0.0) - jnp.log(1.0 + jnp.exp(-jnp.abs(x)))


def _seg_of_tile(i, tile, n_prompt_tok, sample_len):
    tok = i * tile
    return jnp.where(tok < n_prompt_tok, 0, 1 + (tok - n_prompt_tok) // sample_len)


def _adaln_kernel(c_ref, w_ref, b_ref, o_ref):
    c = c_ref[...]
    a = (c * jax.nn.sigmoid(c)).astype(BF16)
    o_ref[0] = _dot(a, w_ref[0].astype(BF16)) + b_ref[0]


def adaln(cvec, ada_w, ada_b):
    depth, d, n = ada_w.shape
    tn = 1536
    return pl.pallas_call(
        _adaln_kernel,
        grid=(depth, n // tn),
        in_specs=[
            pl.BlockSpec((8, d), lambda l, j: (0, 0)),
            pl.BlockSpec((1, d, tn), lambda l, j: (l, 0, j)),
            pl.BlockSpec((1, 1, tn), lambda l, j: (l, 0, j)),
        ],
        out_specs=pl.BlockSpec((1, 8, tn), lambda l, j: (l, 0, j)),
        out_shape=jax.ShapeDtypeStruct((depth, 8, n), F32),
        compiler_params=_cparams("arbitrary", "arbitrary"),
        name="adaln",
    )(cvec, ada_w, ada_b.reshape(depth, 1, n))


def _inproj_kernel(xp_ref, xs_ref, mod_ref, w_ref, wg_ref, wgt_ref, bg_ref, bgt_ref,
                   qkv_ref, o_ref, g_ref, gt_ref, *, n_prompt_tiles):
    x = jnp.where(pl.program_id(0) < n_prompt_tiles, xp_ref[...], xs_ref[...])
    u = (x * (1.0 + mod_ref[0, 1:2, :]) + mod_ref[0, 0:1, :]).astype(BF16)
    p = _dot(u, w_ref[...])
    nqkv = qkv_ref.shape[1]
    qkv_ref[...] = p[:, :nqkv].astype(BF16)
    o_ref[...] = p[:, nqkv:]
    g = _dot(u, wg_ref[...]) + bg_ref[...]
    col = lax.broadcasted_iota(jnp.int32, g.shape, 1)
    g_ref[...] = jnp.where(((col >> 2) & 1) == 1, _log_sigmoid(g), g)
    gt = _dot_nt(wgt_ref[...], u) + bgt_ref[...]
    row = lax.broadcasted_iota(jnp.int32, gt.shape, 0)
    gt_ref[...] = jnp.where(((row >> 2) & 1) == 1, _log_sigmoid(gt), gt)


def _split_token_specs(tile, d, n_prompt_tiles):
    return [pl.BlockSpec((tile, d), lambda i: (jnp.minimum(i, n_prompt_tiles - 1), 0)),
            pl.BlockSpec((tile, d), lambda i: (jnp.maximum(i - n_prompt_tiles, 0), 0))]


def mlstm_inproj(xp, xs, mod, w_main, w_gate, b_gate, sample_len):
    n_prompt_tok, d = xp.shape
    t = n_prompt_tok + xs.shape[0]
    n_main = w_main.shape[1]
    n_qkv = 2 * M_HK + D_MODEL
    ng = w_gate.shape[1]
    tt = TOKEN_TILE
    npt = n_prompt_tok // tt
    seg = functools.partial(_seg_of_tile, tile=tt, n_prompt_tok=n_prompt_tok, sample_len=sample_len)
    return pl.pallas_call(
        functools.partial(_inproj_kernel, n_prompt_tiles=npt),
        grid=(t // tt,),
        in_specs=_split_token_specs(tt, d, npt) + [
            pl.BlockSpec((1, 6, d), lambda i: (seg(i), 0, 0)),
            pl.BlockSpec((d, n_main), lambda i: (0, 0)),
            pl.BlockSpec((d, ng), lambda i: (0, 0)),
            pl.BlockSpec((ng, d), lambda i: (0, 0)),
            pl.BlockSpec((1, ng), lambda i: (0, 0)),
            pl.BlockSpec((ng, 1), lambda i: (0, 0)),
        ],
        out_specs=[
            pl.BlockSpec((tt, n_qkv), lambda i: (i, 0)),
            pl.BlockSpec((tt, n_main - n_qkv), lambda i: (i, 0)),
            pl.BlockSpec((tt, ng), lambda i: (i, 0)),
            pl.BlockSpec((ng, tt), lambda i: (0, i)),
        ],
        out_shape=[
            jax.ShapeDtypeStruct((t, n_qkv), BF16),
            jax.ShapeDtypeStruct((t, n_main - n_qkv), F32),
            jax.ShapeDtypeStruct((t, ng), F32),
            jax.ShapeDtypeStruct((ng, t), F32),
        ],
        compiler_params=_cparams("arbitrary"),
        name="mlstm_inproj",
    )(xp, xs, mod, w_main, w_gate, w_gate.T, b_gate.reshape(1, ng), b_gate.reshape(ng, 1))


def _mlstm_gate_sums(g, gt, backward):
    L = g.shape[0]
    row = lax.broadcasted_iota(jnp.int32, (L, L), 0)
    col = lax.broadcasted_iota(jnp.int32, (L, L), 1)
    lower = col <= row
    upper = col >= row
    tri_col = _onehot(upper if backward else lower)
    tri_row = _onehot(lower if backward else upper)
    ghi, gmid, glo = _split3(g)
    b_cols = _dot(tri_col, ghi) + _dot(tri_col, gmid) + _dot(tri_col, glo)
    thi, tmid, tlo = _split3(gt)
    b_rows = _dot(thi, tri_row) + _dot(tmid, tri_row) + _dot(tlo, tri_row)
    return b_cols, b_rows


def _mlstm_heads(items):
    scale = M_DK ** -0.5
    log_scale = math.log(scale)
    L = items[0][0].shape[0]
    row = lax.broadcasted_iota(jnp.int32, (L, L), 0)
    col = lax.broadcasted_iota(jnp.int32, (L, L), 1)
    masks = {False: col <= row, True: col >= row}
    n = range(len(items))
    qh, kh, vh, li_row, lf_row, b_row, b_col, backward, state = zip(*items)
    has_state = [st is not None for st in state]
    m_prev = [st[2] if st is not None else 0.0 for st in state]

    qk = [_dot_nt(qh[i], kh[i]) for i in n]
    qc = [_dot(qh[i], state[i][0].astype(BF16)) if has_state[i] else None for i in n]
    qn = [_dot_nt(qh[i], state[i][1].astype(BF16))[:, 0:1] if has_state[i] else None for i in n]
    total = [jnp.sum(lf_row[i], axis=1, keepdims=True) for i in n]
    d = [jnp.where(masks[backward[i]], b_col[i] + (li_row[i] - b_row[i] + log_scale), NEG) for i in n]
    inter = [b_col[i] + m_prev[i] if has_state[i] else b_col[i] for i in n]
    m_t = [jnp.maximum(inter[i], jnp.max(d[i], axis=1, keepdims=True) - log_scale) for i in n]
    p = [jnp.exp(d[i] - m_t[i]) for i in n]
    s = [qk[i] * p[i] for i in n]
    den = [jnp.sum(s[i], axis=1, keepdims=True) for i in n]
    num = [_dot(s[i].astype(BF16), vh[i]) for i in n]
    a = [jnp.exp(inter[i] - m_t[i]) * scale if has_state[i] else None for i in n]
    num = [a[i] * qc[i] + num[i] if has_state[i] else num[i] for i in n]
    den = [a[i] * qn[i] + den[i] if has_state[i] else den[i] for i in n]
    hh = [num[i] / jnp.maximum(jnp.abs(den[i]), jnp.exp(-m_t[i])) for i in n]

    g_row = [total[i] - b_row[i] + li_row[i] for i in n]
    m_new = [jnp.maximum(total[i] + m_prev[i], jnp.max(g_row[i], axis=1, keepdims=True)) for i in n]
    ws_row = [jnp.exp(g_row[i] - m_new[i]) for i in n]
    kt = [kh[i].astype(F32).T for i in n]
    c_new = [_dot((kt[i] * ws_row[i]).astype(BF16), vh[i]) for i in n]
    n_new = [_dot(jnp.broadcast_to(ws_row[i], (8, L)).astype(BF16), kh[i]) for i in n]
    a0 = [jnp.exp(total[i] + m_prev[i] - m_new[i]) if has_state[i] else None for i in n]
    c_new = [a0[i] * state[i][0] + c_new[i] if has_state[i] else c_new[i] for i in n]
    n_new = [a0[i] * state[i][1] + n_new[i] if has_state[i] else n_new[i] for i in n]
    return hh, c_new, n_new, m_new


def _head_norm_gate(ht, norm_g, ogate):
    mu = jnp.mean(ht, axis=-1, keepdims=True)
    hc = ht - mu
    var = jnp.mean(hc * hc, axis=-1, keepdims=True)
    return (hc * lax.rsqrt(var + LN_EPS) * norm_g * jax.nn.sigmoid(ogate)).astype(BF16)


def _head_operands(q_ref, k_ref, v_ref, h):
    return (q_ref[:, h * M_DK:(h + 1) * M_DK], k_ref[:, h * M_DK:(h + 1) * M_DK], v_ref[:, h * M_DV:(h + 1) * M_DV])


def _gate_operands(gt, b_rows, b_cols, direction, h):
    ci = direction * 8 + h
    cf = direction * 8 + 4 + h
    return gt[ci:ci + 1, :], gt[cf:cf + 1, :], b_rows[cf:cf + 1, :], b_cols[:, cf:cf + 1]


def _mlstm_prompt_kernel(q_ref, k_ref, v_ref, g_ref, gt_ref, o_ref, ng_ref, a_ref, cf_ref, nf_ref, mf_ref):
    g, gt = g_ref[...], gt_ref[...]
    sums = [_mlstm_gate_sums(g, gt, direction == 1) for direction in (0, 1)]
    keys = [(direction, h) for direction in (0, 1) for h in range(M_HEADS)]
    items = [_head_operands(q_ref, k_ref, v_ref, h)
             + _gate_operands(gt, sums[direction][1], sums[direction][0], direction, h) + (direction == 1, None)
             for direction, h in keys]
    hh, c_new, n_new, m_new = _mlstm_heads(items)
    for i, (direction, h) in enumerate(keys):
        cf_ref[0, 0, direction, h] = c_new[i]
        nf_ref[0, 0, direction, h] = n_new[i]
        mf_ref[0, 0, direction, h] = jnp.broadcast_to(m_new[i], mf_ref.shape[-2:])
    for h in range(M_HEADS):
        sl = slice(h * M_DV, (h + 1) * M_DV)
        a_ref[:, sl] = _head_norm_gate(hh[h] + hh[M_HEADS + h], ng_ref[:, sl], o_ref[:, sl])


def _mlstm_sample_kernel(*refs, direction, n_units):
    backward = direction == 1
    refs = list(refs)
    unit_refs = [tuple(refs.pop(0) for _ in range(5)) for _ in range(n_units)]
    c0_ref, n0_ref, m0_ref = refs.pop(0), refs.pop(0), refs.pop(0)
    if backward:
        hprev_ref = refs.pop(0)
        o_refs = [refs.pop(0) for _ in range(n_units)]
        ng_ref = refs.pop(0)
    out_ref, c_scr, n_scr, m_scr = refs

    @pl.when(pl.program_id(0) == 0)
    def _():
        c_scr[...] = c0_ref[...]
        n_scr[...] = n0_ref[...]
        m_scr[...] = m0_ref[...]

    keys, items = [], []
    for u, (q_ref, k_ref, v_ref, g_ref, gt_ref) in enumerate(unit_refs):
        gt = gt_ref[...]
        b_cols, b_rows = _mlstm_gate_sums(g_ref[...], gt, backward)
        for h in range(M_HEADS):
            state = (c_scr[u, h], n_scr[u, h], m_scr[u, h][0:1, 0:1])
            keys.append((u, h))
            items.append(_head_operands(q_ref, k_ref, v_ref, h) + _gate_operands(gt, b_rows, b_cols, direction, h)
                         + (backward, state))
    hh, c_new, n_new, m_new = _mlstm_heads(items)
    for i, (u, h) in enumerate(keys):
        c_scr[u, h] = c_new[i]
        n_scr[u, h] = n_new[i]
        m_scr[u, h] = jnp.broadcast_to(m_new[i], m_scr.shape[-2:])
        sl = slice(h * M_DV, (h + 1) * M_DV)
        if backward:
            out_ref[u, :, sl] = _head_norm_gate(hprev_ref[u, :, sl] + hh[i], ng_ref[:, sl], o_refs[u][:, sl])
        else:
            out_ref[u, :, sl] = hh[i]


def _mlstm_chunk_specs(block_of):
    L = MLSTM_CHUNK
    return [
        pl.BlockSpec((L, M_HK), lambda s: (block_of(s), 0)),
        pl.BlockSpec((L, M_HK), lambda s: (block_of(s), 1)),
        pl.BlockSpec((L, D_MODEL), lambda s: (block_of(s), 1)),
        pl.BlockSpec((L, 16), lambda s: (block_of(s), 0)),
        pl.BlockSpec((16, L), lambda s: (0, block_of(s))),
    ]


def mlstm_prompt(qkv, g, gt, ogate, norm_g, n_prompt):
    L = MLSTM_CHUNK
    state_blk = lambda *tail: pl.BlockSpec((1, 1, 2, M_HEADS) + tail, lambda s: (s, 0, 0, 0, 0, 0))
    return pl.pallas_call(
        _mlstm_prompt_kernel,
        grid=(n_prompt,),
        in_specs=_mlstm_chunk_specs(lambda s: s) + [
            pl.BlockSpec((L, D_MODEL), lambda s: (s, 0)),
            pl.BlockSpec((1, D_MODEL), lambda s: (0, 0)),
        ],
        out_specs=[pl.BlockSpec((L, D_MODEL), lambda s: (s, 0)),
                   state_blk(M_DK, M_DV), state_blk(8, M_DK), state_blk(8, 128)],
        out_shape=[
            jax.ShapeDtypeStruct((n_prompt * L, D_MODEL), BF16),
            jax.ShapeDtypeStruct((n_prompt, 1, 2, M_HEADS, M_DK, M_DV), F32),
            jax.ShapeDtypeStruct((n_prompt, 1, 2, M_HEADS, 8, M_DK), F32),
            jax.ShapeDtypeStruct((n_prompt, 1, 2, M_HEADS, 8, 128), F32),
        ],
        compiler_params=_cparams("arbitrary"),
        name="mlstm_prompt",
    )(qkv, qkv, qkv, g, gt, ogate, norm_g.reshape(1, D_MODEL))


def mlstm_sample(direction, qkv, g, gt, c0, n0, m0, first_block, chunks_per_sample,
                 hprev=None, ogate=None, norm_g=None):
    L = MLSTM_CHUNK
    backward = direction == 1
    cps = chunks_per_sample
    n_units = c0.shape[0]
    pos = (lambda s: cps - 1 - s) if backward else (lambda s: s)
    unit_block = [functools.partial(lambda s, u: first_block + u * cps + pos(s), u=u) for u in range(n_units)]
    whole = lambda a: pl.BlockSpec(a.shape, lambda s: (0,) * a.ndim)
    in_specs, args = [], []
    for u in range(n_units):
        in_specs += _mlstm_chunk_specs(unit_block[u])
        args += [qkv, qkv, qkv, g, gt]
    in_specs += [whole(c0), whole(n0), whole(m0)]
    args += [c0, n0, m0]
    if backward:
        in_specs += [pl.BlockSpec((n_units, L, D_MODEL), lambda s: (0, pos(s), 0))]
        in_specs += [pl.BlockSpec((L, D_MODEL), functools.partial(lambda s, u: (unit_block[u](s), 0), u=u))
                     for u in range(n_units)]
        in_specs += [pl.BlockSpec((1, D_MODEL), lambda s: (0, 0))]
        args += [hprev] + [ogate] * n_units + [norm_g.reshape(1, D_MODEL)]
    return pl.pallas_call(
        functools.partial(_mlstm_sample_kernel, direction=direction, n_units=n_units),
        grid=(cps,),
        in_specs=in_specs,
        out_specs=pl.BlockSpec((n_units, L, D_MODEL), lambda s: (0, pos(s), 0)),
        out_shape=jax.ShapeDtypeStruct((n_units, cps * L, D_MODEL), BF16 if backward else F32),
        scratch_shapes=[pltpu.VMEM(c0.shape, F32), pltpu.VMEM(n0.shape, F32), pltpu.VMEM(m0.shape, F32)],
        compiler_params=_cparams("arbitrary"),
        name="mlstm_sample_bwd" if backward else "mlstm_sample_fwd",
    )(*args)


def _outproj_kernel(*refs, n_prompt_tiles, a_split, x_split):
    refs = list(refs)
    is_prompt = pl.program_id(0) < n_prompt_tiles

    def take(split):
        if split:
            p_ref, s_ref = refs.pop(0), refs.pop(0)
            return jnp.where(is_prompt, p_ref[...], s_ref[...])
        return refs.pop(0)[...]

    a = take(a_split)
    x = take(x_split)
    mod_ref, w_ref, lng_ref, lnb_ref = refs[:4]
    xo_ref, uo_ref, up_ref = refs[-3:]
    y = _dot(a, w_ref[...])
    z = DEEPNORM_ALPHA * x + mod_ref[0, 2:3, :] * y
    xn = _layer_norm(z, lng_ref[...], lnb_ref[...])
    xo_ref[...] = xn
    u = xn * (1.0 + mod_ref[0, 4:5, :]) + mod_ref[0, 3:4, :]
    uo_ref[...] = u.astype(BF16)
    up_ref[...] = _pack_bf16_pairs(u)


def mixer_outproj(a, x, mod, w, ln_g, ln_b, n_prompt_tok, sample_len, run_after=()):
    d = w.shape[0]
    tt = TOKEN_TILE
    npt = n_prompt_tok // tt
    seg = functools.partial(_seg_of_tile, tile=tt, n_prompt_tok=n_prompt_tok, sample_len=sample_len)

    def specs(v):
        if isinstance(v, tuple):
            return _split_token_specs(tt, d, npt), list(v), v[0].shape[0] + v[1].shape[0]
        return [pl.BlockSpec((tt, d), lambda i: (i, 0))], [v], v.shape[0]

    a_specs, a_args, t = specs(a)
    x_specs, x_args, _ = specs(x)
    return pl.pallas_call(
        functools.partial(_outproj_kernel, n_prompt_tiles=npt, a_split=isinstance(a, tuple),
                          x_split=isinstance(x, tuple)),
        grid=(t // tt,),
        in_specs=a_specs + x_specs + [
            pl.BlockSpec((1, 6, d), lambda i: (seg(i), 0, 0)),
            pl.BlockSpec((d, d), lambda i: (0, 0)),
            pl.BlockSpec((1, d), lambda i: (0, 0)),
            pl.BlockSpec((1, d), lambda i: (0, 0)),
        ] + [pl.BlockSpec(memory_space=pl.ANY)] * len(run_after),
        out_specs=[pl.BlockSpec((tt, d), lambda i: (i, 0)), pl.BlockSpec((tt, d), lambda i: (i, 0)),
                   pl.BlockSpec((tt, d // 2), lambda i: (i, 0))],
        out_shape=[jax.ShapeDtypeStruct((t, d), F32), jax.ShapeDtypeStruct((t, d), BF16),
                   jax.ShapeDtypeStruct((t, d // 2), jnp.int32)],
        compiler_params=_cparams("arbitrary"),
        name="mixer_outproj",
    )(*a_args, *x_args, mod, w, ln_g.reshape(1, d), ln_b.reshape(1, d), *run_after)


def _swiglu_tile(x, wi, wo, dff):
    sub = FFN_SUB
    proj = lambda j: (_dot(x, wi(slice(j * sub, (j + 1) * sub))), _dot(x, wi(slice(dff + j * sub, dff + (j + 1) * sub))))
    acts, cur = [], proj(0)
    for j in range(dff // sub):
        nxt = proj(j + 1) if (j + 1) * sub < dff else None
        gp, up = cur
        acts.append((gp * jax.nn.sigmoid(gp) * up).astype(BF16))
        cur = nxt
    return _dot(jnp.concatenate(acts, axis=1), wo(slice(0, dff)))


def _mixer_ffn_kernel(ap_ref, as_ref, xp_ref, xs_ref, mod_ref, modn_ref, wmix_ref, wi_ref, wo_ref, ln_ref,
                      xo_ref, uo_ref, *, n_prompt_tiles):
    is_prompt = pl.program_id(0) < n_prompt_tiles
    a = jnp.where(is_prompt, ap_ref[...], as_ref[...])
    x = jnp.where(is_prompt, xp_ref[...], xs_ref[...])
    z = DEEPNORM_ALPHA * x + mod_ref[0, 2:3, :] * _dot(a, wmix_ref[...])
    x1 = _layer_norm(z, ln_ref[0:1, :], ln_ref[1:2, :])
    u1 = (x1 * (1.0 + mod_ref[0, 4:5, :]) + mod_ref[0, 3:4, :]).astype(BF16)
    f = _swiglu_tile(u1, lambda c: wi_ref[:, c], lambda r: wo_ref[r, :], wo_ref.shape[0])
    x2 = _layer_norm(DEEPNORM_ALPHA * x1 + mod_ref[0, 5:6, :] * f, ln_ref[2:3, :], ln_ref[3:4, :])
    xo_ref[...] = x2
    uo_ref[...] = (x2 * (1.0 + modn_ref[0, 1:2, :]) + modn_ref[0, 0:1, :]).astype(BF16)


def _resident(shape):
    return pl.BlockSpec(shape, lambda i: (0,) * len(shape), pipeline_mode=pl.Buffered(1))


def mixer_ffn(a_p, a_s, xp, xs, mod, mod_next, w_mix, w_in, w_out, ln_g, ln_b, sample_len):
    n_prompt_tok, d = xp.shape
    t = n_prompt_tok + xs.shape[0]
    tm = FFN_TOKEN_TILE
    npt = n_prompt_tok // tm
    seg = functools.partial(_seg_of_tile, tile=tm, n_prompt_tok=n_prompt_tok, sample_len=sample_len)
    ln = jnp.stack([ln_g[0], ln_b[0], ln_g[1], ln_b[1]])
    return pl.pallas_call(
        functools.partial(_mixer_ffn_kernel, n_prompt_tiles=npt),
        grid=(t // tm,),
        in_specs=_split_token_specs(tm, d, npt) + _split_token_specs(tm, d, npt) + [
            pl.BlockSpec((1, 6, d), lambda i: (seg(i), 0, 0)),
            pl.BlockSpec((1, 6, d), lambda i: (seg(i), 0, 0)),
            _resident(w_mix.shape), _resident(w_in.shape), _resident(w_out.shape), _resident(ln.shape),
        ],
        out_specs=[pl.BlockSpec((tm, d), lambda i: (i, 0)), pl.BlockSpec((tm, d), lambda i: (i, 0))],
        out_shape=[jax.ShapeDtypeStruct((t, d), F32), jax.ShapeDtypeStruct((t, d), BF16)],
        compiler_params=_cparams("arbitrary"),
        name="mixer_ffn",
    )(a_p, a_s, xp, xs, mod, mod_next, w_mix, w_in, w_out, ln)


def _modulation_tables(c, c_ctx, ada_w, ada_b):
    n_s = c.shape[0]
    cvec = jnp.concatenate([c_ctx[None, :], c, jnp.zeros((8 - 1 - n_s, c.shape[1]), F32)], axis=0)
    mod = adaln(cvec, ada_w, ada_b)
    return mod[:, :1 + n_s, :].reshape(ada_w.shape[0], 1 + n_s, 6, c.shape[1])


def _layer0(xp, xs, mod0, mod1, state_c, state_n, state_m, ln_g, ln_b, w_in, b_gates, norm_g, w_out,
            ffn_w_in, ffn_w_out, n_prompt, prompt_len, n_sample, sample_len):
    n_prompt_tok = n_prompt * prompt_len
    n_main = 2 * M_HK + 2 * D_MODEL
    qkv, ogate, g, gt = mlstm_inproj(xp, xs, mod0, w_in[:, :n_main].astype(BF16), w_in[:, n_main:].astype(BF16),
                                     b_gates.reshape(-1), sample_len)
    npc = n_prompt_tok // MLSTM_CHUNK
    cps = sample_len // MLSTM_CHUNK
    rep = lambda a: jnp.broadcast_to(a[..., None, :], a.shape[:-1] + (8, a.shape[-1]))
    n0 = rep(state_n)
    m0 = jnp.broadcast_to(state_m[..., None, None], state_m.shape + (8, 128))
    a_p, new_c, nf, mf = mlstm_prompt(qkv, g, gt, ogate, norm_g, n_prompt)
    hf = mlstm_sample(0, qkv, g, gt, state_c[:, 0], n0[:, 0], m0[:, 0], npc, cps)
    a_s = mlstm_sample(1, qkv, g, gt, state_c[:, 1], n0[:, 1], m0[:, 1], npc, cps,
                       hprev=hf, ogate=ogate, norm_g=norm_g)
    x2, u2 = mixer_ffn(a_p, a_s.reshape(-1, D_MODEL), xp, xs, mod0, mod1, w_out.astype(BF16),
                       ffn_w_in.astype(BF16), ffn_w_out.astype(BF16), ln_g, ln_b, sample_len)
    return x2, u2, new_c, nf[..., 0, :], mf[..., 0, 0]


def _qkv_kernel(u_ref, w_ref, qkv_ref, k_ref, v_ref, *, n_prompt_tiles):
    p = _dot(u_ref[...], w_ref[...])
    qkv_ref[...] = p.astype(BF16)
    tt, d = u_ref.shape

    @pl.when(pl.program_id(0) < n_prompt_tiles)
    def _():
        for h in range(NA_HEADS):
            rows = pl.ds(h, tt, stride=NA_HEADS)
            k_ref[rows, :] = p[:, d + h * NA_DH:d + (h + 1) * NA_DH]
            v_ref[rows, :] = p[:, 2 * d + h * NA_DH:2 * d + (h + 1) * NA_DH]


def na_qkv(u, w, n_prompt_tok):
    t, d = u.shape
    tt = TOKEN_TILE
    npt = n_prompt_tok // tt
    kv_spec = pl.BlockSpec((tt * NA_HEADS, NA_DH), lambda i: (jnp.minimum(i, npt - 1), 0))
    kv_shape = jax.ShapeDtypeStruct((n_prompt_tok * NA_HEADS, NA_DH), F32)
    return pl.pallas_call(
        functools.partial(_qkv_kernel, n_prompt_tiles=npt),
        grid=(t // tt,),
        in_specs=[pl.BlockSpec((tt, d), lambda i: (i, 0)), pl.BlockSpec((d, 3 * d), lambda i: (0, 0))],
        out_specs=[pl.BlockSpec((tt, 3 * d), lambda i: (i, 0)), kv_spec, kv_spec],
        out_shape=[jax.ShapeDtypeStruct((t, 3 * d), BF16), kv_shape, kv_shape],
        compiler_params=_cparams("arbitrary"),
        name="na_qkv",
    )(u, w)


def _head_pair_masks(scale):
    assert math.frexp(scale)[0] == 0.5
    lane = lax.broadcasted_iota(jnp.int32, (1, 2 * NA_DH), 1)
    first, second = lane < NA_DH, lane >= NA_DH
    return tuple((jnp.where(sel, scale, 0.0).astype(BF16), _onehot(sel)) for sel in (first, second))


def _softmax_pv(score_blocks, value_blocks):
    mx = None
    for s in score_blocks:
        bm = jnp.max(s, axis=1, keepdims=True)
        mx = bm if mx is None else jnp.maximum(mx, bm)
    acc, denom = None, None
    for s, v in zip(score_blocks, value_blocks):
        p = jnp.exp(s - mx)
        ps = jnp.sum(p, axis=1, keepdims=True)
        pv = _dot(p.astype(BF16), v)
        acc = pv if acc is None else acc + pv
        denom = ps if denom is None else denom + ps
    return acc / denom


def _ctx_attn_kernel(q_ref, k_ref, v_ref, o_ref):
    masks = _head_pair_masks(NA_DH ** -0.5)
    for hp in range(NA_HEADS // 2):
        sl = slice(hp * 2 * NA_DH, (hp + 1) * 2 * NA_DH)
        q2, k2, v2 = q_ref[:, sl], k_ref[:, sl], v_ref[:, sl]
        o2 = None
        for qmsk, vmsk in masks:
            s = _dot_nt(q2 * qmsk, k2)
            o = _softmax_pv([s], [v2 * vmsk])
            o2 = o if o2 is None else o2 + o
        o_ref[:, sl] = o2.astype(o_ref.dtype)


def context_attention(qkv, n_prompt, prompt_len):
    t = n_prompt * prompt_len
    d = D_MODEL
    return pl.pallas_call(
        _ctx_attn_kernel,
        grid=(n_prompt,),
        in_specs=[
            pl.BlockSpec((prompt_len, d), lambda b: (b, 0)),
            pl.BlockSpec((prompt_len, d), lambda b: (b, 1)),
            pl.BlockSpec((prompt_len, d), lambda b: (b, 2)),
        ],
        out_specs=pl.BlockSpec((prompt_len, d), lambda b: (b, 0)),
        out_shape=jax.ShapeDtypeStruct((t, d), BF16),
        compiler_params=_cparams("arbitrary"),
        name="context_attention",
    )(qkv, qkv, qkv)


def _na_kernel(tile_ref, q_ref, kp_ref, kc_ref, kn_ref, vp_ref, vc_ref, vn_ref, kctx_ref, vctx_ref, bias_ref,
               o_ref):
    j = pl.program_id(1)
    nblk = pl.num_programs(1)
    m = q_ref.shape[0]
    halo = NA_HALO * GRID_W
    n_pairs = (NA_QROWS + 2 * NA_HALO) // 2
    variant = jnp.where(j == 0, 0, jnp.where(j == nblk - 1, 2, 1))
    kinds = [[tile_ref[(variant * NA_QROWS + rq) * n_pairs + kp] for kp in range(n_pairs)]
             for rq in range(NA_QROWS)]

    def bias_of(head):
        return jnp.concatenate(
            [jnp.concatenate([bias_ref[kinds[rq][kp], head] for kp in range(n_pairs)], axis=1)
             for rq in range(NA_QROWS)], axis=0)

    masks = _head_pair_masks(NA_DH ** -0.5)
    for hp in range(NA_HEADS // 2):
        sl = slice(hp * 2 * NA_DH, (hp + 1) * 2 * NA_DH)
        q2 = q_ref[:, sl]
        kloc = jnp.concatenate([kp_ref[m - halo:, sl], kc_ref[:, sl], kn_ref[:halo, sl]], axis=0)
        vloc = jnp.concatenate([vp_ref[m - halo:, sl], vc_ref[:, sl], vn_ref[:halo, sl]], axis=0)
        kctx = kctx_ref[0][:, sl]
        vctx = vctx_ref[0][:, sl]
        o2 = None
        for half, (qmsk, vmsk) in enumerate(masks):
            qm = q2 * qmsk
            s_loc = _dot_nt(qm, kloc) + bias_of(2 * hp + half)
            s_ctx = _dot_nt(qm, kctx)
            o = _softmax_pv([s_loc, s_ctx], [vloc * vmsk, vctx * vmsk])
            o2 = o if o2 is None else o2 + o
        o_ref[:, sl] = o2.astype(o_ref.dtype)


def _na_bias_kernel(lc_ref, rc_ref, lv_ref, rv_ref, o_ref):
    nc, npos = lc_ref.shape[2], o_ref.shape[2]
    pair_shift = (2 * GRID_W).bit_length() - 1
    c = lax.broadcasted_iota(jnp.int32, (nc, npos), 0)
    pos = lax.broadcasted_iota(jnp.int32, (nc, npos), 1)
    qc = pos >> pair_shift
    kc = pos & (GRID_W - 1)
    c_start = jnp.clip(qc - NA_COLS // 2, 0, GRID_W - NA_COLS)
    col_in = jnp.logical_and(kc >= c_start, kc < c_start + NA_COLS)
    dc = jnp.clip(kc - qc + NA_COLS - 1, 0, 2 * NA_COLS - 2)
    sel = _onehot(jnp.logical_and(c == dc, col_in))

    def pick(row_ref):
        r1, r2, r3 = _split3(row_ref[0])
        return _dot(r1, sel) + _dot(r2, sel) + _dot(r3, sel)

    right = (pos[0:1, :] & GRID_W) != 0
    val = jnp.where(right, pick(rc_ref), pick(lc_ref))
    visible = jnp.where(right, rv_ref[0], lv_ref[0]) > 0.0
    o_ref[0] = jnp.where(jnp.logical_and(visible, col_in[0:1, :]), val, NEG)


def _na_bias_plan():
    ndr = 2 * NA_ROWS - 1
    assert NA_QROWS >= NA_ROWS // 2 and NA_HALO == NA_ROWS // 2 and (NA_QROWS + 2 * NA_HALO) % 2 == 0
    ok = lambda dr: dr if dr is not None and 0 <= dr < ndr else None
    kinds = [(ok(dr), ok(dr + 1)) for dr in range(-1, ndr)]
    first_key_row = (lambda rq: NA_HALO, lambda rq: rq, lambda rq: NA_QROWS + NA_HALO - NA_ROWS)
    table = []
    for first in first_key_row:
        for rq in range(NA_QROWS):
            for kp in range((NA_QROWS + 2 * NA_HALO) // 2):
                drs = []
                for kk in (2 * kp, 2 * kp + 1):
                    visible = first(rq) <= kk < first(rq) + NA_ROWS
                    drs.append(kk - rq + NA_ROWS - 1 - NA_HALO if visible else None)
                kind = (drs[0], drs[1])
                if kind not in kinds:
                    kinds.append(kind)
                table.append(kinds.index(kind))
    return kinds, table


def _na_bias_tiles(rpb):
    nh, ndr, ndc = rpb.shape
    nc = 32
    kinds, table = _na_bias_plan()
    nk = len(kinds)
    rpb_p = jnp.pad(rpb, ((0, 0), (0, 0), (0, nc - ndc)))
    zero = jnp.zeros((nh, nc), F32)
    lc = jnp.stack([zero if l is None else rpb_p[:, l] for l, _ in kinds])
    rc = jnp.stack([zero if r is None else rpb_p[:, r] for _, r in kinds])
    vis = lambda flags: jnp.broadcast_to(jnp.asarray(flags, F32)[:, None, None], (nk, nh, 1))
    lv = vis([0.0 if l is None else 1.0 for l, _ in kinds])
    rv = vis([0.0 if r is None else 1.0 for _, r in kinds])
    npos = GRID_W * 2 * GRID_W
    row_spec = pl.BlockSpec((1, nh, nc), lambda t: (t, 0, 0))
    flag_spec = pl.BlockSpec((1, nh, 1), lambda t: (t, 0, 0))
    tiles = pl.pallas_call(
        _na_bias_kernel,
        grid=(nk,),
        in_specs=[row_spec, row_spec, flag_spec, flag_spec],
        out_specs=pl.BlockSpec((1, nh, npos), lambda t: (t, 0, 0)),
        out_shape=jax.ShapeDtypeStruct((nk, nh, npos), F32),
        compiler_params=_cparams("arbitrary"),
        name="na_bias",
    )(lc, rc, lv, rv)
    return tiles.reshape(nk, nh, GRID_W, 2 * GRID_W), jnp.asarray(table, jnp.int32)


def neighbourhood_attention(qkv, k_ctx, v_ctx, rpb, n_prompt_tok, n_sample, sample_len):
    d = D_MODEL
    m = NA_QROWS * GRID_W
    nblk = sample_len // m
    assert nblk >= 3
    base = n_prompt_tok // m
    bias, table = _na_bias_tiles(rpb)

    def blk(col, off):
        return pl.BlockSpec((m, d), lambda b, j, tbl: (base + b * nblk + jnp.clip(j + off, 0, nblk - 1), col))

    return pl.pallas_call(
        _na_kernel,
        grid_spec=pltpu.PrefetchScalarGridSpec(
            num_scalar_prefetch=1,
            grid=(n_sample, nblk),
            in_specs=[
                blk(0, 0), blk(1, -1), blk(1, 0), blk(1, 1), blk(2, -1), blk(2, 0), blk(2, 1),
                pl.BlockSpec((1,) + k_ctx.shape[1:], lambda b, j, tbl: (b, 0, 0)),
                pl.BlockSpec((1,) + v_ctx.shape[1:], lambda b, j, tbl: (b, 0, 0)),
                pl.BlockSpec(bias.shape, lambda b, j, tbl: (0, 0, 0, 0)),
            ],
            out_specs=pl.BlockSpec((m, d), lambda b, j, tbl: (b * nblk + j, 0)),
        ),
        out_shape=jax.ShapeDtypeStruct((n_sample * sample_len, d), BF16),
        compiler_params=_cparams("arbitrary", "arbitrary"),
        name="neighbourhood_attention",
    )(table, qkv, qkv, qkv, qkv, qkv, qkv, qkv, k_ctx, v_ctx, bias)


MOE_BLOCK = 256
MOE_ROW_TILE = 512
SC_CORES = 2
SC_SUBCORES = 16
SC_GATHER_CHUNK = 128

def _router_kernel(u_ref, wt_ref, comb_ref, pos_ref, total_ref, carry_scr):
    i = pl.program_id(0)

    @pl.when(i == 0)
    def _():
        carry_scr[...] = jnp.zeros_like(carry_scr)

    u = u_ref[...]
    w1, w2, w3 = _split3(wt_ref[...])
    lg = _dot_nt(w1, u) + _dot_nt(w2, u) + _dot_nt(w3, u)
    ne, nt = lg.shape
    e = lax.broadcasted_iota(jnp.int32, lg.shape, 0)
    m1 = jnp.max(lg, axis=0, keepdims=True)
    i1 = jnp.min(jnp.where(lg == m1, e, ne), axis=0, keepdims=True)
    sel1 = e == i1
    lg2 = jnp.where(sel1, -jnp.inf, lg)
    m2 = jnp.max(lg2, axis=0, keepdims=True)
    i2 = jnp.min(jnp.where(lg2 == m2, e, ne), axis=0, keepdims=True)
    sel2 = e == i2
    ex = jnp.exp(m2 - m1)
    wa = 1.0 / (1.0 + ex)
    wb = ex / (1.0 + ex)
    comb = jnp.where(sel1, wa, jnp.where(sel2, wb, 0.0))
    assign = jnp.logical_or(sel1, sel2)
    a = jnp.where(assign, 1.0, 0.0)
    row = lax.broadcasted_iota(jnp.int32, (nt, nt), 0)
    col = lax.broadcasted_iota(jnp.int32, (nt, nt), 1)
    tri = _onehot(row < col)
    carry = carry_scr[:, 0:1]
    rank = _dot(a.astype(BF16), tri) + carry
    pos = jnp.where(assign, rank, -1.0)
    comb_ref[...] = comb
    pos_ref[...] = pos
    carry_new = carry + jnp.sum(a, axis=1, keepdims=True)
    carry_scr[...] = jnp.broadcast_to(carry_new, carry_scr.shape)
    total_ref[...] = jnp.broadcast_to(carry_new, carry_scr.shape)


def moe_router(u, w_router):
    t, d = u.shape
    ne = w_router.shape[1]
    tb = MOE_BLOCK
    nb = t // tb
    return pl.pallas_call(
        _router_kernel,
        grid=(nb,),
        in_specs=[pl.BlockSpec((tb, d), lambda i: (i, 0)), pl.BlockSpec((ne, d), lambda i: (0, 0))],
        out_specs=[
            pl.BlockSpec((ne, tb), lambda i: (0, i)),
            pl.BlockSpec((ne, tb), lambda i: (0, i)),
            pl.BlockSpec((ne, 128), lambda i: (0, 0)),
        ],
        out_shape=[
            jax.ShapeDtypeStruct((ne, t), F32),
            jax.ShapeDtypeStruct((ne, t), F32),
            jax.ShapeDtypeStruct((ne, 128), F32),
        ],
        scratch_shapes=[pltpu.VMEM((ne, 128), F32)],
        compiler_params=_cparams("arbitrary"),
        name="moe_router",
    )(u, w_router.T)


def _moe_plan(totals, pos, comb, n_tok):
    ne = totals.shape[0]
    tm = MOE_ROW_TILE
    n_tiles = (2 * n_tok) // tm + ne
    tiles_per = (totals + tm - 1) // tm
    tile_end = jnp.cumsum(tiles_per)
    n_used = tile_end[-1]
    base_rows = (tile_end - tiles_per) * tm
    ti = jnp.arange(n_tiles, dtype=jnp.int32)
    tile_expert = jnp.minimum(jnp.sum((ti[:, None] >= tile_end[None, :]).astype(jnp.int32), axis=1), ne - 1)
    last_expert = tile_expert[jnp.maximum(n_used - 1, 0)]
    tile_expert = jnp.where(ti < n_used, tile_expert, last_expert)
    routed = pos >= 0.0
    row = base_rows[:, None] + pos.astype(jnp.int32)
    row_a = jnp.min(jnp.where(routed, row, n_tiles * tm), axis=0)
    row_b = jnp.max(jnp.where(routed, row, -1), axis=0)
    w_a = jnp.sum(jnp.where(jnp.logical_and(routed, row == row_a[None, :]), comb, 0.0), axis=0)
    w_b = jnp.sum(jnp.where(jnp.logical_and(routed, row == row_b[None, :]), comb, 0.0), axis=0)
    tile_rows = jnp.clip(totals[tile_expert] - (ti - (tile_end - tiles_per)[tile_expert]) * tm, 0, tm)
    tile_rows = jnp.where(ti < n_used, tile_rows, 0)
    return dict(n_rows=n_tiles * tm, tile_expert=tile_expert.astype(jnp.int32),
                n_used=n_used.reshape(1).astype(jnp.int32), tile_rows=tile_rows.astype(jnp.int32),
                token_rows=jnp.concatenate([row_a, row_b]).astype(jnp.int32),
                token_weights=jnp.stack([w_a, w_b], axis=1))


def _pack_bf16_pairs(x):
    half = x.shape[1] // 2
    bits = pltpu.bitcast(x.astype(BF16).astype(F32), jnp.int32)
    return (bits[:, :half] & jnp.int32(-65536)) | lax.shift_right_logical(bits[:, half:], jnp.int32(16))


def _unpack_bf16_pairs(p):
    hi = pltpu.bitcast(p & jnp.int32(-65536), F32)
    lo = pltpu.bitcast(lax.shift_left(p, jnp.int32(16)), F32)
    return jnp.concatenate([hi, lo], axis=1)


def sc_gather_rows(table, idx):
    _, w = table.shape
    b = idx.shape[0]
    workers = SC_CORES * SC_SUBCORES
    ch = SC_GATHER_CHUNK
    n_chunks = b // (workers * ch)
    assert n_chunks * workers * ch == b
    mesh = plsc.VectorSubcoreMesh(core_axis_name="c", subcore_axis_name="s", num_cores=SC_CORES,
                                  num_subcores=SC_SUBCORES)

    @functools.partial(
        pl.kernel, mesh=mesh, out_type=jax.ShapeDtypeStruct((b, w), table.dtype),
        scratch_types=[pltpu.VMEM((ch,), jnp.int32), pltpu.VMEM((ch, w), table.dtype), pltpu.SemaphoreType.DMA])
    def gather(table_hbm, idx_hbm, out_hbm, idx_v, rows_v, sem):
        worker = lax.axis_index("s") * SC_CORES + lax.axis_index("c")

        @pl.loop(0, n_chunks)
        def _(c):
            base = (worker * n_chunks + c) * ch
            pltpu.sync_copy(idx_hbm.at[pl.ds(base, ch)], idx_v)
            pltpu.async_copy(table_hbm.at[idx_v], rows_v, sem).wait()
            pltpu.sync_copy(rows_v, out_hbm.at[pl.ds(base, ch)])

    return gather(table, idx)


def sc_scatter_rows(rows, idx, n_out):
    t, w = rows.shape
    copies = idx.shape[0] // t
    workers = SC_CORES * SC_SUBCORES
    ch = SC_GATHER_CHUNK
    n_chunks = t // (workers * ch)
    assert n_chunks * workers * ch == t and copies * t == idx.shape[0]
    mesh = plsc.VectorSubcoreMesh(core_axis_name="c", subcore_axis_name="s", num_cores=SC_CORES,
                                  num_subcores=SC_SUBCORES)

    @functools.partial(
        pl.kernel, mesh=mesh, out_type=jax.ShapeDtypeStruct((n_out, w), rows.dtype),
        scratch_types=[pltpu.VMEM((ch,), jnp.int32), pltpu.VMEM((ch, w), rows.dtype), pltpu.SemaphoreType.DMA])
    def scatter(rows_hbm, idx_hbm, out_hbm, idx_v, rows_v, sem):
        worker = lax.axis_index("s") * SC_CORES + lax.axis_index("c")

        @pl.loop(0, n_chunks)
        def _(c):
            base = (worker * n_chunks + c) * ch
            pltpu.sync_copy(rows_hbm.at[pl.ds(base, ch)], rows_v)
            for k in range(copies):
                pltpu.sync_copy(idx_hbm.at[pl.ds(k * t + base, ch)], idx_v)
                pltpu.async_copy(rows_v, out_hbm.at[idx_v], sem).wait()

    return scatter(rows, idx)


def _gffn_kernel(te_ref, nu_ref, nr_ref, x_ref, wi_ref, wo_ref, y_ref):
    i = pl.program_id(0)
    used = i < nu_ref[0]

    @pl.when(used)
    def _():
        packed = x_ref[...]
        row = lax.broadcasted_iota(jnp.int32, packed.shape, 0)
        x = _unpack_bf16_pairs(jnp.where(row < nr_ref[i], packed, 0)).astype(BF16)
        y = _swiglu_tile(x, lambda c: wi_ref[0, :, c], lambda r: wo_ref[0, r, :], wo_ref.shape[1])
        y_ref[...] = _pack_bf16_pairs(y)

    @pl.when(jnp.logical_not(used))
    def _():
        y_ref[...] = jnp.zeros_like(y_ref)


def moe_grouped_ffn(xs, w_in, w_out, plan):
    nr, half = xs.shape
    ne, dff, d = w_out.shape
    tm = MOE_ROW_TILE
    return pl.pallas_call(
        _gffn_kernel,
        grid_spec=pltpu.PrefetchScalarGridSpec(
            num_scalar_prefetch=3,
            grid=(nr // tm,),
            in_specs=[
                pl.BlockSpec((tm, half), lambda i, te, nu, tr: (jnp.minimum(i, nu[0] - 1), 0)),
                pl.BlockSpec((1, d, 2 * dff), lambda i, te, nu, tr: (te[i], 0, 0)),
                pl.BlockSpec((1, dff, d), lambda i, te, nu, tr: (te[i], 0, 0)),
            ],
            out_specs=pl.BlockSpec((tm, half), lambda i, te, nu, tr: (i, 0)),
        ),
        out_shape=jax.ShapeDtypeStruct((nr, half), jnp.int32),
        compiler_params=_cparams("arbitrary"),
        name="moe_grouped_ffn",
    )(plan["tile_expert"], plan["n_used"], plan["tile_rows"], xs, w_in, w_out)


def _combine_kernel(ya_ref, yb_ref, w_ref, x_ref, mod_ref, lng_ref, lnb_ref, op_ref, os_ref, *, n_prompt_tiles):
    i = pl.program_id(0)
    w = w_ref[...]
    moe = w[:, 0:1] * _unpack_bf16_pairs(ya_ref[...]) + w[:, 1:2] * _unpack_bf16_pairs(yb_ref[...])
    z = DEEPNORM_ALPHA * x_ref[...] + mod_ref[0, 5:6, :] * moe
    xn = _layer_norm(z, lng_ref[...], lnb_ref[...])

    @pl.when(i < n_prompt_tiles)
    def _():
        op_ref[...] = xn

    @pl.when(i >= n_prompt_tiles)
    def _():
        os_ref[...] = xn


def moe_combine(y_tok, weights, x, mod, ln_g, ln_b, n_prompt_tok, sample_len):
    t, d = x.shape
    tt = TOKEN_TILE
    nt = t // tt
    npt = n_prompt_tok // tt
    seg = functools.partial(_seg_of_tile, tile=tt, n_prompt_tok=n_prompt_tok, sample_len=sample_len)
    return pl.pallas_call(
        functools.partial(_combine_kernel, n_prompt_tiles=npt),
        grid=(nt,),
        in_specs=[
            pl.BlockSpec((tt, d // 2), lambda i: (i, 0)),
            pl.BlockSpec((tt, d // 2), lambda i: (nt + i, 0)),
            pl.BlockSpec((tt, 2), lambda i: (i, 0)),
            pl.BlockSpec((tt, d), lambda i: (i, 0)),
            pl.BlockSpec((1, 6, d), lambda i: (seg(i), 0, 0)),
            pl.BlockSpec((1, d), lambda i: (0, 0)),
            pl.BlockSpec((1, d), lambda i: (0, 0)),
        ],
        out_specs=[
            pl.BlockSpec((tt, d), lambda i: (jnp.minimum(i, npt - 1), 0)),
            pl.BlockSpec((tt, d), lambda i: (jnp.maximum(i - npt, 0), 0)),
        ],
        out_shape=[jax.ShapeDtypeStruct((n_prompt_tok, d), F32), jax.ShapeDtypeStruct((t - n_prompt_tok, d), F32)],
        compiler_params=_cparams("arbitrary"),
        name="moe_combine",
    )(y_tok, y_tok, weights, x, mod, ln_g.reshape(1, d), ln_b.reshape(1, d))


def _layer1(x, u, mod1, cache_k, cache_v, ln_g, ln_b, w_qkv, rpb, w_out, w_router, moe_w_in, moe_w_out,
            n_prompt, prompt_len, n_sample, sample_len):
    n_prompt_tok = n_prompt * prompt_len
    d = D_MODEL
    qkv, k_p, v_p = na_qkv(u, w_qkv.astype(BF16), n_prompt_tok)
    attn_p = context_attention(qkv, n_prompt, prompt_len)
    k_ctx = cache_k.reshape(n_sample, -1, d).astype(BF16)
    v_ctx = cache_v.reshape(n_sample, -1, d).astype(BF16)
    attn_s = neighbourhood_attention(qkv, k_ctx, v_ctx, rpb, n_prompt_tok, n_sample, sample_len)
    moe_w_in, moe_w_out = moe_w_in.astype(BF16), moe_w_out.astype(BF16)
    x1, u1, u1_packed = mixer_outproj((attn_p, attn_s), x, mod1, w_out.astype(BF16), ln_g[0], ln_b[0],
                                      n_prompt_tok, sample_len, run_after=(moe_w_in, moe_w_out))
    comb, pos, totals = moe_router(u1, w_router)
    plan = _moe_plan(totals[:, 0].astype(jnp.int32), pos, comb, x.shape[0])
    xs = sc_scatter_rows(u1_packed, plan["token_rows"], plan["n_rows"])
    y = moe_grouped_ffn(xs, moe_w_in, moe_w_out, plan)
    y_tok = sc_gather_rows(y, plan["token_rows"])
    y_p, y_s = moe_combine(y_tok, plan["token_weights"], x1, mod1, ln_g[1], ln_b[1], n_prompt_tok, sample_len)
    return y_p, y_s, k_p, v_p


def kernel(x_prompt, x_sample, state_mlstm_C, state_mlstm_n, state_mlstm_m, cache_na_k, cache_na_v, c, c_ctx,
           ada_w, ada_b, ln_g, ln_b, mlstm_w_in, mlstm_b_gates, mlstm_norm_g, mlstm_w_out, na_w_qkv, na_rpb,
           na_w_out, ffn_w_in, ffn_w_out, moe_w_router, moe_w_in, moe_w_out):
    n_prompt, prompt_len, d = x_prompt.shape
    n_sample, sample_len, _ = x_sample.shape
    assert d == D_MODEL and prompt_len == MLSTM_CHUNK and sample_len % MLSTM_CHUNK == 0
    assert ada_w.shape[0] == DEPTH == 2
    mod = _modulation_tables(c, c_ctx, ada_w, ada_b)
    x2, u2, new_c, new_n, new_m = _layer0(
        x_prompt.reshape(-1, d), x_sample.reshape(-1, d), mod[0], mod[1], state_mlstm_C[:, 0], state_mlstm_n[:, 0], state_mlstm_m[:, 0], ln_g[0], ln_b[0],
        mlstm_w_in[0], mlstm_b_gates[0], mlstm_norm_g[0], mlstm_w_out[0], ffn_w_in[0], ffn_w_out[0],
        n_prompt, prompt_len, n_sample, sample_len)
    y_p, y_s, k_p, v_p = _layer1(
        x2, u2, mod[1], cache_na_k[:, 0], cache_na_v[:, 0], ln_g[1], ln_b[1], na_w_qkv[0], na_rpb[0], na_w_out[0],
        moe_w_router[0], moe_w_in[0], moe_w_out[0], n_prompt, prompt_len, n_sample, sample_len)
    kv_shape = (n_prompt, 1, prompt_len, NA_HEADS, NA_DH)
    return (y_p.reshape(x_prompt.shape), y_s.reshape(x_sample.shape), new_c, new_n, new_m,
            k_p.reshape(kv_shape), v_p.reshape(kv_shape))
```

```python
import functools
import math

import jax
import jax.numpy as jnp
from jax import lax
from jax.experimental import pallas as pl
from jax.experimental.pallas import tpu as pltpu
from jax.experimental.pallas import tpu_sc as plsc

F32 = jnp.float32
BF16 = jnp.bfloat16

D_MODEL = 1024
DEPTH = 2
GRID_W = 64
M_HEADS = 4
M_DK = 128
M_DV = 256
M_HK = M_HEADS * M_DK
NA_HEADS = 16
NA_DH = 64
NA_ROWS = 8
NA_COLS = 16
D_FF = 2816
N_EXPERTS = 8
DEEPNORM_ALPHA = (2.0 * DEPTH) ** 0.25
LN_EPS = 1e-5
NEG = -1e30

VMEM_LIMIT_BYTES = 56 * 1024 * 1024

MLSTM_CHUNK = 256
TOKEN_TILE = 512
FFN_TOKEN_TILE = 512
FFN_SUB = 256
NA_QROWS = 4
NA_HALO = 4


def _cparams(*sem):
    return pltpu.CompilerParams(dimension_semantics=sem, vmem_limit_bytes=VMEM_LIMIT_BYTES)


def _dot(a, b):
    return jnp.dot(a, b, preferred_element_type=F32)


def _dot_nt(a, b):
    return lax.dot_general(a, b, (((1,), (1,)), ((), ())), preferred_element_type=F32)


def _onehot(mask):
    return jnp.where(mask, 1.0, 0.0).astype(BF16)


def _split3(x):
    hi = x.astype(BF16)
    r = x - hi.astype(F32)
    mid = r.astype(BF16)
    lo = (r - mid.astype(F32)).astype(BF16)
    return hi, mid, lo


def _layer_norm(z, g, b):
    mu = jnp.mean(z, axis=-1, keepdims=True)
    zc = z - mu
    var = jnp.mean(zc * zc, axis=-1, keepdims=True)
    return zc * lax.rsqrt(var + LN_EPS) * g + b


def _log_sigmoid(x):
    return jnp.minimum(x, 0.0) - jnp.log(1.0 + jnp.exp(-jnp.abs(x)))


def _seg_of_tile(i, tile, n_prompt_tok, sample_len):
    tok = i * tile
    return jnp.where(tok < n_prompt_tok, 0, 1 + (tok - n_prompt_tok) // sample_len)


def _adaln_kernel(c_ref, w_ref, b_ref, o_ref):
    c = c_ref[...]
    a = (c * jax.nn.sigmoid(c)).astype(BF16)
    o_ref[0] = _dot(a, w_ref[0].astype(BF16)) + b_ref[0]


def adaln(cvec, ada_w, ada_b):
    depth, d, n = ada_w.shape
    tn = 1536
    return pl.pallas_call(
        _adaln_kernel,
        grid=(depth, n // tn),
        in_specs=[
            pl.BlockSpec((8, d), lambda l, j: (0, 0)),
            pl.BlockSpec((1, d, tn), lambda l, j: (l, 0, j)),
            pl.BlockSpec((1, 1, tn), lambda l, j: (l, 0, j)),
        ],
        out_specs=pl.BlockSpec((1, 8, tn), lambda l, j: (l, 0, j)),
        out_shape=jax.ShapeDtypeStruct((depth, 8, n), F32),
        compiler_params=_cparams("arbitrary", "arbitrary"),
        name="adaln",
    )(cvec, ada_w, ada_b.reshape(depth, 1, n))


def _inproj_kernel(xp_ref, xs_ref, mod_ref, w_ref, wg_ref, wgt_ref, bg_ref, bgt_ref,
                   qkv_ref, o_ref, g_ref, gt_ref, *, n_prompt_tiles):
    x = jnp.where(pl.program_id(0) < n_prompt_tiles, xp_ref[...], xs_ref[...])
    u = (x * (1.0 + mod_ref[0, 1:2, :]) + mod_ref[0, 0:1, :]).astype(BF16)
    p = _dot(u, w_ref[...])
    nqkv = qkv_ref.shape[1]
    qkv_ref[...] = p[:, :nqkv].astype(BF16)
    o_ref[...] = p[:, nqkv:]
    g = _dot(u, wg_ref[...]) + bg_ref[...]
    col = lax.broadcasted_iota(jnp.int32, g.shape, 1)
    g_ref[...] = jnp.where(((col >> 2) & 1) == 1, _log_sigmoid(g), g)
    gt = _dot_nt(wgt_ref[...], u) + bgt_ref[...]
    row = lax.broadcasted_iota(jnp.int32, gt.shape, 0)
    gt_ref[...] = jnp.where(((row >> 2) & 1) == 1, _log_sigmoid(gt), gt)


def _split_token_specs(tile, d, n_prompt_tiles):
    return [pl.BlockSpec((tile, d), lambda i: (jnp.minimum(i, n_prompt_tiles - 1), 0)),
            pl.BlockSpec((tile, d), lambda i: (jnp.maximum(i - n_prompt_tiles, 0), 0))]


def mlstm_inproj(xp, xs, mod, w_main, w_gate, b_gate, sample_len):
    n_prompt_tok, d = xp.shape
    t = n_prompt_tok + xs.shape[0]
    n_main = w_main.shape[1]
    n_qkv = 2 * M_HK + D_MODEL
    ng = w_gate.shape[1]
    tt = TOKEN_TILE
    npt = n_prompt_tok // tt
    seg = functools.partial(_seg_of_tile, tile=tt, n_prompt_tok=n_prompt_tok, sample_len=sample_len)
    return pl.pallas_call(
        functools.partial(_inproj_kernel, n_prompt_tiles=npt),
        grid=(t // tt,),
        in_specs=_split_token_specs(tt, d, npt) + [
            pl.BlockSpec((1, 6, d), lambda i: (seg(i), 0, 0)),
            pl.BlockSpec((d, n_main), lambda i: (0, 0)),
            pl.BlockSpec((d, ng), lambda i: (0, 0)),
            pl.BlockSpec((ng, d), lambda i: (0, 0)),
            pl.BlockSpec((1, ng), lambda i: (0, 0)),
            pl.BlockSpec((ng, 1), lambda i: (0, 0)),
        ],
        out_specs=[
            pl.BlockSpec((tt, n_qkv), lambda i: (i, 0)),
            pl.BlockSpec((tt, n_main - n_qkv), lambda i: (i, 0)),
            pl.BlockSpec((tt, ng), lambda i: (i, 0)),
            pl.BlockSpec((ng, tt), lambda i: (0, i)),
        ],
        out_shape=[
            jax.ShapeDtypeStruct((t, n_qkv), BF16),
            jax.ShapeDtypeStruct((t, n_main - n_qkv), F32),
            jax.ShapeDtypeStruct((t, ng), F32),
            jax.ShapeDtypeStruct((ng, t), F32),
        ],
        compiler_params=_cparams("arbitrary"),
        name="mlstm_inproj",
    )(xp, xs, mod, w_main, w_gate, w_gate.T, b_gate.reshape(1, ng), b_gate.reshape(ng, 1))


def _mlstm_gate_sums(g, gt, backward):
    L = g.shape[0]
    row = lax.broadcasted_iota(jnp.int32, (L, L), 0)
    col = lax.broadcasted_iota(jnp.int32, (L, L), 1)
    lower = col <= row
    upper = col >= row
    tri_col = _onehot(upper if backward else lower)
    tri_row = _onehot(lower if backward else upper)
    ghi, gmid, glo = _split3(g)
    b_cols = _dot(tri_col, ghi) + _dot(tri_col, gmid) + _dot(tri_col, glo)
    thi, tmid, tlo = _split3(gt)
    b_rows = _dot(thi, tri_row) + _dot(tmid, tri_row) + _dot(tlo, tri_row)
    return b_cols, b_rows


def _mlstm_heads(items):
    scale = M_DK ** -0.5
    log_scale = math.log(scale)
    L = items[0][0].shape[0]
    row = lax.broadcasted_iota(jnp.int32, (L, L), 0)
    col = lax.broadcasted_iota(jnp.int32, (L, L), 1)
    masks = {False: col <= row, True: col >= row}
    n = range(len(items))
    qh, kh, vh, li_row, lf_row, b_row, b_col, backward, state = zip(*items)
    has_state = [st is not None for st in state]
    m_prev = [st[2] if st is not None else 0.0 for st in state]

    qk = [_dot_nt(qh[i], kh[i]) for i in n]
    qc = [_dot(qh[i], state[i][0].astype(BF16)) if has_state[i] else None for i in n]
    qn = [_dot_nt(qh[i], state[i][1].astype(BF16))[:, 0:1] if has_state[i] else None for i in n]
    total = [jnp.sum(lf_row[i], axis=1, keepdims=True) for i in n]
    d = [jnp.where(masks[backward[i]], b_col[i] + (li_row[i] - b_row[i] + log_scale), NEG) for i in n]
    inter = [b_col[i] + m_prev[i] if has_state[i] else b_col[i] for i in n]
    m_t = [jnp.maximum(inter[i], jnp.max(d[i], axis=1, keepdims=True) - log_scale) for i in n]
    p = [jnp.exp(d[i] - m_t[i]) for i in n]
    s = [qk[i] * p[i] for i in n]
    den = [jnp.sum(s[i], axis=1, keepdims=True) for i in n]
    num = [_dot(s[i].astype(BF16), vh[i]) for i in n]
    a = [jnp.exp(inter[i] - m_t[i]) * scale if has_state[i] else None for i in n]
    num = [a[i] * qc[i] + num[i] if has_state[i] else num[i] for i in n]
    den = [a[i] * qn[i] + den[i] if has_state[i] else den[i] for i in n]
    hh = [num[i] / jnp.maximum(jnp.abs(den[i]), jnp.exp(-m_t[i])) for i in n]

    g_row = [total[i] - b_row[i] + li_row[i] for i in n]
    m_new = [jnp.maximum(total[i] + m_prev[i], jnp.max(g_row[i], axis=1, keepdims=True)) for i in n]
    ws_row = [jnp.exp(g_row[i] - m_new[i]) for i in n]
    kt = [kh[i].astype(F32).T for i in n]
    c_new = [_dot((kt[i] * ws_row[i]).astype(BF16), vh[i]) for i in n]
    n_new = [_dot(jnp.broadcast_to(ws_row[i], (8, L)).astype(BF16), kh[i]) for i in n]
    a0 = [jnp.exp(total[i] + m_prev[i] - m_new[i]) if has_state[i] else None for i in n]
    c_new = [a0[i] * state[i][0] + c_new[i] if has_state[i] else c_new[i] for i in n]
    n_new = [a0[i] * state[i][1] + n_new[i] if has_state[i] else n_new[i] for i in n]
    return hh, c_new, n_new, m_new


def _head_norm_gate(ht, norm_g, ogate):
    mu = jnp.mean(ht, axis=-1, keepdims=True)
    hc = ht - mu
    var = jnp.mean(hc * hc, axis=-1, keepdims=True)
    return (hc * lax.rsqrt(var + LN_EPS) * norm_g * jax.nn.sigmoid(ogate)).astype(BF16)


def _head_operands(q_ref, k_ref, v_ref, h):
    return (q_ref[:, h * M_DK:(h + 1) * M_DK], k_ref[:, h * M_DK:(h + 1) * M_DK], v_ref[:, h * M_DV:(h + 1) * M_DV])


def _gate_operands(gt, b_rows, b_cols, direction, h):
    ci = direction * 8 + h
    cf = direction * 8 + 4 + h
    return gt[ci:ci + 1, :], gt[cf:cf + 1, :], b_rows[cf:cf + 1, :], b_cols[:, cf:cf + 1]


def _mlstm_prompt_kernel(q_ref, k_ref, v_ref, g_ref, gt_ref, o_ref, ng_ref, a_ref, cf_ref, nf_ref, mf_ref):
    g, gt = g_ref[...], gt_ref[...]
    sums = [_mlstm_gate_sums(g, gt, direction == 1) for direction in (0, 1)]
    keys = [(direction, h) for direction in (0, 1) for h in range(M_HEADS)]
    items = [_head_operands(q_ref, k_ref, v_ref, h)
             + _gate_operands(gt, sums[direction][1], sums[direction][0], direction, h) + (direction == 1, None)
             for direction, h in keys]
    hh, c_new, n_new, m_new = _mlstm_heads(items)
    for i, (direction, h) in enumerate(keys):
        cf_ref[0, 0, direction, h] = c_new[i]
        nf_ref[0, 0, direction, h] = n_new[i]
        mf_ref[0, 0, direction, h] = jnp.broadcast_to(m_new[i], mf_ref.shape[-2:])
    for h in range(M_HEADS):
        sl = slice(h * M_DV, (h + 1) * M_DV)
        a_ref[:, sl] = _head_norm_gate(hh[h] + hh[M_HEADS + h], ng_ref[:, sl], o_ref[:, sl])


def _mlstm_sample_kernel(*refs, direction, n_units):
    backward = direction == 1
    refs = list(refs)
    unit_refs = [tuple(refs.pop(0) for _ in range(5)) for _ in range(n_units)]
    c0_ref, n0_ref, m0_ref = refs.pop(0), refs.pop(0), refs.pop(0)
    if backward:
        hprev_ref = refs.pop(0)
        o_refs = [refs.pop(0) for _ in range(n_units)]
        ng_ref = refs.pop(0)
    out_ref, c_scr, n_scr, m_scr = refs

    @pl.when(pl.program_id(0) == 0)
    def _():
        c_scr[...] = c0_ref[...]
        n_scr[...] = n0_ref[...]
        m_scr[...] = m0_ref[...]

    keys, items = [], []
    for u, (q_ref, k_ref, v_ref, g_ref, gt_ref) in enumerate(unit_refs):
        gt = gt_ref[...]
        b_cols, b_rows = _mlstm_gate_sums(g_ref[...], gt, backward)
        for h in range(M_HEADS):
            state = (c_scr[u, h], n_scr[u, h], m_scr[u, h][0:1, 0:1])
            keys.append((u, h))
            items.append(_head_operands(q_ref, k_ref, v_ref, h) + _gate_operands(gt, b_rows, b_cols, direction, h)
                         + (backward, state))
    hh, c_new, n_new, m_new = _mlstm_heads(items)
    for i, (u, h) in enumerate(keys):
        c_scr[u, h] = c_new[i]
        n_scr[u, h] = n_new[i]
        m_scr[u, h] = jnp.broadcast_to(m_new[i], m_scr.shape[-2:])
        sl = slice(h * M_DV, (h + 1) * M_DV)
        if backward:
            out_ref[u, :, sl] = _head_norm_gate(hprev_ref[u, :, sl] + hh[i], ng_ref[:, sl], o_refs[u][:, sl])
        else:
            out_ref[u, :, sl] = hh[i]


def _mlstm_chunk_specs(block_of):
    L = MLSTM_CHUNK
    return [
        pl.BlockSpec((L, M_HK), lambda s: (block_of(s), 0)),
        pl.BlockSpec((L, M_HK), lambda s: (block_of(s), 1)),
        pl.BlockSpec((L, D_MODEL), lambda s: (block_of(s), 1)),
        pl.BlockSpec((L, 16), lambda s: (block_of(s), 0)),
        pl.BlockSpec((16, L), lambda s: (0, block_of(s))),
    ]


def mlstm_prompt(qkv, g, gt, ogate, norm_g, n_prompt):
    L = MLSTM_CHUNK
    state_blk = lambda *tail: pl.BlockSpec((1, 1, 2, M_HEADS) + tail, lambda s: (s, 0, 0, 0, 0, 0))
    return pl.pallas_call(
        _mlstm_prompt_kernel,
        grid=(n_prompt,),
        in_specs=_mlstm_chunk_specs(lambda s: s) + [
            pl.BlockSpec((L, D_MODEL), lambda s: (s, 0)),
            pl.BlockSpec((1, D_MODEL), lambda s: (0, 0)),
        ],
        out_specs=[pl.BlockSpec((L, D_MODEL), lambda s: (s, 0)),
                   state_blk(M_DK, M_DV), state_blk(8, M_DK), state_blk(8, 128)],
        out_shape=[
            jax.ShapeDtypeStruct((n_prompt * L, D_MODEL), BF16),
            jax.ShapeDtypeStruct((n_prompt, 1, 2, M_HEADS, M_DK, M_DV), F32),
            jax.ShapeDtypeStruct((n_prompt, 1, 2, M_HEADS, 8, M_DK), F32),
            jax.ShapeDtypeStruct((n_prompt, 1, 2, M_HEADS, 8, 128), F32),
        ],
        compiler_params=_cparams("arbitrary"),
        name="mlstm_prompt",
    )(qkv, qkv, qkv, g, gt, ogate, norm_g.reshape(1, D_MODEL))


def mlstm_sample(direction, qkv, g, gt, c0, n0, m0, first_block, chunks_per_sample,
                 hprev=None, ogate=None, norm_g=None):
    L = MLSTM_CHUNK
    backward = direction == 1
    cps = chunks_per_sample
    n_units = c0.shape[0]
    pos = (lambda s: cps - 1 - s) if backward else (lambda s: s)
    unit_block = [functools.partial(lambda s, u: first_block + u * cps + pos(s), u=u) for u in range(n_units)]
    whole = lambda a: pl.BlockSpec(a.shape, lambda s: (0,) * a.ndim)
    in_specs, args = [], []
    for u in range(n_units):
        in_specs += _mlstm_chunk_specs(unit_block[u])
        args += [qkv, qkv, qkv, g, gt]
    in_specs += [whole(c0), whole(n0), whole(m0)]
    args += [c0, n0, m0]
    if backward:
        in_specs += [pl.BlockSpec((n_units, L, D_MODEL), lambda s: (0, pos(s), 0))]
        in_specs += [pl.BlockSpec((L, D_MODEL), functools.partial(lambda s, u: (unit_block[u](s), 0), u=u))
                     for u in range(n_units)]
        in_specs += [pl.BlockSpec((1, D_MODEL), lambda s: (0, 0))]
        args += [hprev] + [ogate] * n_units + [norm_g.reshape(1, D_MODEL)]
    return pl.pallas_call(
        functools.partial(_mlstm_sample_kernel, direction=direction, n_units=n_units),
        grid=(cps,),
        in_specs=in_specs,
        out_specs=pl.BlockSpec((n_units, L, D_MODEL), lambda s: (0, pos(s), 0)),
        out_shape=jax.ShapeDtypeStruct((n_units, cps * L, D_MODEL), BF16 if backward else F32),
        scratch_shapes=[pltpu.VMEM(c0.shape, F32), pltpu.VMEM(n0.shape, F32), pltpu.VMEM(m0.shape, F32)],
        compiler_params=_cparams("arbitrary"),
        name="mlstm_sample_bwd" if backward else "mlstm_sample_fwd",
    )(*args)


def _outproj_router_kernel(ap_ref, as_ref, x_ref, mod_ref, w_ref, lng_ref, lnb_ref, wr_ref, *refs,
                           n_prompt_tiles):
    xo_ref, up_ref, comb_ref, pos_ref, total_ref, carry_scr = refs[-6:]
    i = pl.program_id(0)

    @pl.when(i == 0)
    def _():
        carry_scr[...] = jnp.zeros_like(carry_scr)

    a = jnp.where(i < n_prompt_tiles, ap_ref[...], as_ref[...])
    z = DEEPNORM_ALPHA * x_ref[...] + mod_ref[0, 2:3, :] * _dot(a, w_ref[...])
    xn = _layer_norm(z, lng_ref[...], lnb_ref[...])
    xo_ref[...] = xn
    u = xn * (1.0 + mod_ref[0, 4:5, :]) + mod_ref[0, 3:4, :]
    up_ref[...] = _pack_bf16_pairs(u)
    comb, pos, carry_new = _route_tokens(u.astype(BF16), wr_ref[...], carry_scr[:, 0:1])
    comb_ref[...] = comb
    pos_ref[...] = pos
    carry_scr[...] = jnp.broadcast_to(carry_new, carry_scr.shape)
    total_ref[...] = jnp.broadcast_to(carry_new, carry_scr.shape)


def mixer_outproj_router(a_p, a_s, x, mod, w, ln_g, ln_b, w_router, sample_len, run_after=()):
    t, d = x.shape
    ne = w_router.shape[1]
    tt = TOKEN_TILE
    npt = a_p.shape[0] // tt
    seg = functools.partial(_seg_of_tile, tile=tt, n_prompt_tok=a_p.shape[0], sample_len=sample_len)
    return pl.pallas_call(
        functools.partial(_outproj_router_kernel, n_prompt_tiles=npt),
        grid=(t // tt,),
        in_specs=_split_token_specs(tt, d, npt) + [
            pl.BlockSpec((tt, d), lambda i: (i, 0)),
            pl.BlockSpec((1, 6, d), lambda i: (seg(i), 0, 0)),
            pl.BlockSpec((d, d), lambda i: (0, 0)),
            pl.BlockSpec((1, d), lambda i: (0, 0)),
            pl.BlockSpec((1, d), lambda i: (0, 0)),
            pl.BlockSpec((ne, d), lambda i: (0, 0)),
        ] + [pl.BlockSpec(memory_space=pl.ANY)] * len(run_after),
        out_specs=[
            pl.BlockSpec((tt, d), lambda i: (i, 0)),
            pl.BlockSpec((tt, d // 2), lambda i: (i, 0)),
            pl.BlockSpec((ne, tt), lambda i: (0, i)),
            pl.BlockSpec((ne, tt), lambda i: (0, i)),
            pl.BlockSpec((ne, 128), lambda i: (0, 0)),
        ],
        out_shape=[
            jax.ShapeDtypeStruct((t, d), F32),
            jax.ShapeDtypeStruct((t, d // 2), jnp.int32),
            jax.ShapeDtypeStruct((ne, t), F32),
            jax.ShapeDtypeStruct((ne, t), F32),
            jax.ShapeDtypeStruct((ne, 128), F32),
        ],
        scratch_shapes=[pltpu.VMEM((ne, 128), F32)],
        compiler_params=_cparams("arbitrary"),
        name="mixer_outproj_router",
    )(a_p, a_s, x, mod, w, ln_g.reshape(1, d), ln_b.reshape(1, d), w_router.T, *run_after)


def _swiglu_tile(x, wi, wo, dff):
    sub = FFN_SUB
    proj = lambda j: (_dot(x, wi(slice(j * sub, (j + 1) * sub))), _dot(x, wi(slice(dff + j * sub, dff + (j + 1) * sub))))
    acts, cur = [], proj(0)
    for j in range(dff // sub):
        nxt = proj(j + 1) if (j + 1) * sub < dff else None
        gp, up = cur
        acts.append((gp * jax.nn.sigmoid(gp) * up).astype(BF16))
        cur = nxt
    return _dot(jnp.concatenate(acts, axis=1), wo(slice(0, dff)))


def _mixer_ffn_kernel(ap_ref, as_ref, xp_ref, xs_ref, mod_ref, modn_ref, wmix_ref, wi_ref, wo_ref, ln_ref,
                      xo_ref, uo_ref, *, n_prompt_tiles):
    is_prompt = pl.program_id(0) < n_prompt_tiles
    a = jnp.where(is_prompt, ap_ref[...], as_ref[...])
    x = jnp.where(is_prompt, xp_ref[...], xs_ref[...])
    z = DEEPNORM_ALPHA * x + mod_ref[0, 2:3, :] * _dot(a, wmix_ref[...])
    x1 = _layer_norm(z, ln_ref[0:1, :], ln_ref[1:2, :])
    u1 = (x1 * (1.0 + mod_ref[0, 4:5, :]) + mod_ref[0, 3:4, :]).astype(BF16)
    f = _swiglu_tile(u1, lambda c: wi_ref[:, c], lambda r: wo_ref[r, :], wo_ref.shape[0])
    x2 = _layer_norm(DEEPNORM_ALPHA * x1 + mod_ref[0, 5:6, :] * f, ln_ref[2:3, :], ln_ref[3:4, :])
    xo_ref[...] = x2
    uo_ref[...] = (x2 * (1.0 + modn_ref[0, 1:2, :]) + modn_ref[0, 0:1, :]).astype(BF16)


def _resident(shape):
    return pl.BlockSpec(shape, lambda i: (0,) * len(shape), pipeline_mode=pl.Buffered(1))


def mixer_ffn(a_p, a_s, xp, xs, mod, mod_next, w_mix, w_in, w_out, ln_g, ln_b, sample_len):
    n_prompt_tok, d = xp.shape
    t = n_prompt_tok + xs.shape[0]
    tm = FFN_TOKEN_TILE
    npt = n_prompt_tok // tm
    seg = functools.partial(_seg_of_tile, tile=tm, n_prompt_tok=n_prompt_tok, sample_len=sample_len)
    ln = jnp.stack([ln_g[0], ln_b[0], ln_g[1], ln_b[1]])
    return pl.pallas_call(
        functools.partial(_mixer_ffn_kernel, n_prompt_tiles=npt),
        grid=(t // tm,),
        in_specs=_split_token_specs(tm, d, npt) + _split_token_specs(tm, d, npt) + [
            pl.BlockSpec((1, 6, d), lambda i: (seg(i), 0, 0)),
            pl.BlockSpec((1, 6, d), lambda i: (seg(i), 0, 0)),
            _resident(w_mix.shape), _resident(w_in.shape), _resident(w_out.shape), _resident(ln.shape),
        ],
        out_specs=[pl.BlockSpec((tm, d), lambda i: (i, 0)), pl.BlockSpec((tm, d), lambda i: (i, 0))],
        out_shape=[jax.ShapeDtypeStruct((t, d), F32), jax.ShapeDtypeStruct((t, d), BF16)],
        compiler_params=_cparams("arbitrary"),
        name="mixer_ffn",
    )(a_p, a_s, xp, xs, mod, mod_next, w_mix, w_in, w_out, ln)


def _modulation_tables(c, c_ctx, ada_w, ada_b):
    n_s = c.shape[0]
    cvec = jnp.concatenate([c_ctx[None, :], c, jnp.zeros((8 - 1 - n_s, c.shape[1]), F32)], axis=0)
    mod = adaln(cvec, ada_w, ada_b)
    return mod[:, :1 + n_s, :].reshape(ada_w.shape[0], 1 + n_s, 6, c.shape[1])


def _layer0(xp, xs, mod0, mod1, state_c, state_n, state_m, ln_g, ln_b, w_in, b_gates, norm_g, w_out,
            ffn_w_in, ffn_w_out, n_prompt, prompt_len, n_sample, sample_len):
    n_prompt_tok = n_prompt * prompt_len
    n_main = 2 * M_HK + 2 * D_MODEL
    qkv, ogate, g, gt = mlstm_inproj(xp, xs, mod0, w_in[:, :n_main].astype(BF16), w_in[:, n_main:].astype(BF16),
                                     b_gates.reshape(-1), sample_len)
    npc = n_prompt_tok // MLSTM_CHUNK
    cps = sample_len // MLSTM_CHUNK
    rep = lambda a: jnp.broadcast_to(a[..., None, :], a.shape[:-1] + (8, a.shape[-1]))
    n0 = rep(state_n)
    m0 = jnp.broadcast_to(state_m[..., None, None], state_m.shape + (8, 128))
    a_p, new_c, nf, mf = mlstm_prompt(qkv, g, gt, ogate, norm_g, n_prompt)
    hf = mlstm_sample(0, qkv, g, gt, state_c[:, 0], n0[:, 0], m0[:, 0], npc, cps)
    a_s = mlstm_sample(1, qkv, g, gt, state_c[:, 1], n0[:, 1], m0[:, 1], npc, cps,
                       hprev=hf, ogate=ogate, norm_g=norm_g)
    x2, u2 = mixer_ffn(a_p, a_s.reshape(-1, D_MODEL), xp, xs, mod0, mod1, w_out.astype(BF16),
                       ffn_w_in.astype(BF16), ffn_w_out.astype(BF16), ln_g, ln_b, sample_len)
    return x2, u2, new_c, nf[..., 0, :], mf[..., 0, 0]


def _qkv_kernel(u_ref, w_ref, qkv_ref, k_ref, v_ref, *, n_prompt_tiles):
    p = _dot(u_ref[...], w_ref[...])
    qkv_ref[...] = p.astype(BF16)
    tt, d = u_ref.shape

    @pl.when(pl.program_id(0) < n_prompt_tiles)
    def _():
        for h in range(NA_HEADS):
            rows = pl.ds(h, tt, stride=NA_HEADS)
            k_ref[rows, :] = p[:, d + h * NA_DH:d + (h + 1) * NA_DH]
            v_ref[rows, :] = p[:, 2 * d + h * NA_DH:2 * d + (h + 1) * NA_DH]


def na_qkv(u, w, n_prompt_tok):
    t, d = u.shape
    tt = TOKEN_TILE
    npt = n_prompt_tok // tt
    kv_spec = pl.BlockSpec((tt * NA_HEADS, NA_DH), lambda i: (jnp.minimum(i, npt - 1), 0))
    kv_shape = jax.ShapeDtypeStruct((n_prompt_tok * NA_HEADS, NA_DH), F32)
    return pl.pallas_call(
        functools.partial(_qkv_kernel, n_prompt_tiles=npt),
        grid=(t // tt,),
        in_specs=[pl.BlockSpec((tt, d), lambda i: (i, 0)), pl.BlockSpec((d, 3 * d), lambda i: (0, 0))],
        out_specs=[pl.BlockSpec((tt, 3 * d), lambda i: (i, 0)), kv_spec, kv_spec],
        out_shape=[jax.ShapeDtypeStruct((t, 3 * d), BF16), kv_shape, kv_shape],
        compiler_params=_cparams("arbitrary"),
        name="na_qkv",
    )(u, w)


ATTN_LOG2E = math.log2(math.e)
ATTN_Q_SCALE = NA_DH ** -0.5 * ATTN_LOG2E


def _head_pair_masks():
    lane = lax.broadcasted_iota(jnp.int32, (1, 2 * NA_DH), 1)
    return (_onehot(lane < NA_DH), _onehot(lane >= NA_DH))


def _softmax_pv(score_blocks, value_blocks):
    mx = None
    for s in score_blocks:
        bm = jnp.max(s, axis=1, keepdims=True)
        mx = bm if mx is None else jnp.maximum(mx, bm)
    acc, denom = None, None
    for s, v in zip(score_blocks, value_blocks):
        p = jnp.exp2(s - mx)
        ps = jnp.sum(p, axis=1, keepdims=True)
        pv = _dot(p.astype(BF16), v)
        acc = pv if acc is None else acc + pv
        denom = ps if denom is None else denom + ps
    return acc / denom


def _ctx_attn_kernel(q_ref, k_ref, v_ref, o_ref):
    masks = _head_pair_masks()
    for hp in range(NA_HEADS // 2):
        sl = slice(hp * 2 * NA_DH, (hp + 1) * 2 * NA_DH)
        q2, k2, v2 = q_ref[:, sl], k_ref[:, sl], v_ref[:, sl]
        o2 = None
        for msk in masks:
            s = _dot_nt(q2 * msk, k2)
            o = _softmax_pv([s], [v2 * msk])
            o2 = o if o2 is None else o2 + o
        o_ref[:, sl] = o2.astype(o_ref.dtype)


def context_attention(qkv, n_prompt, prompt_len):
    t = n_prompt * prompt_len
    d = D_MODEL
    return pl.pallas_call(
        _ctx_attn_kernel,
        grid=(n_prompt,),
        in_specs=[
            pl.BlockSpec((prompt_len, d), lambda b: (b, 0)),
            pl.BlockSpec((prompt_len, d), lambda b: (b, 1)),
            pl.BlockSpec((prompt_len, d), lambda b: (b, 2)),
        ],
        out_specs=pl.BlockSpec((prompt_len, d), lambda b: (b, 0)),
        out_shape=jax.ShapeDtypeStruct((t, d), BF16),
        compiler_params=_cparams("arbitrary"),
        name="context_attention",
    )(qkv, qkv, qkv)


def _na_kernel(tile_ref, q_ref, kp_ref, kc_ref, kn_ref, vp_ref, vc_ref, vn_ref, kctx_ref, vctx_ref, bias_ref,
               o_ref):
    j = pl.program_id(1)
    nblk = pl.num_programs(1)
    m = q_ref.shape[0]
    halo = NA_HALO * GRID_W
    n_pairs = (NA_QROWS + 2 * NA_HALO) // 2
    variant = jnp.where(j == 0, 0, jnp.where(j == nblk - 1, 2, 1))
    kinds = [[tile_ref[(variant * NA_QROWS + rq) * n_pairs + kp] for kp in range(n_pairs)]
             for rq in range(NA_QROWS)]

    def bias_of(head):
        return jnp.concatenate(
            [jnp.concatenate([bias_ref[kinds[rq][kp], head] for kp in range(n_pairs)], axis=1)
             for rq in range(NA_QROWS)], axis=0)

    masks = _head_pair_masks()
    for hp in range(NA_HEADS // 2):
        sl = slice(hp * 2 * NA_DH, (hp + 1) * 2 * NA_DH)
        q2 = q_ref[:, sl]
        kloc = jnp.concatenate([kp_ref[m - halo:, sl], kc_ref[:, sl], kn_ref[:halo, sl]], axis=0)
        vloc = jnp.concatenate([vp_ref[m - halo:, sl], vc_ref[:, sl], vn_ref[:halo, sl]], axis=0)
        kctx = kctx_ref[0][:, sl]
        vctx = vctx_ref[0][:, sl]
        o2 = None
        for half, msk in enumerate(masks):
            qm = q2 * msk
            s_loc = _dot_nt(qm, kloc) + bias_of(2 * hp + half)
            s_ctx = _dot_nt(qm, kctx)
            o = _softmax_pv([s_loc, s_ctx], [vloc * msk, vctx * msk])
            o2 = o if o2 is None else o2 + o
        o_ref[:, sl] = o2.astype(o_ref.dtype)


def _na_bias_kernel(lc_ref, rc_ref, lv_ref, rv_ref, o_ref):
    nc, npos = lc_ref.shape[2], o_ref.shape[2]
    pair_shift = (2 * GRID_W).bit_length() - 1
    c = lax.broadcasted_iota(jnp.int32, (nc, npos), 0)
    pos = lax.broadcasted_iota(jnp.int32, (nc, npos), 1)
    qc = pos >> pair_shift
    kc = pos & (GRID_W - 1)
    c_start = jnp.clip(qc - NA_COLS // 2, 0, GRID_W - NA_COLS)
    col_in = jnp.logical_and(kc >= c_start, kc < c_start + NA_COLS)
    dc = jnp.clip(kc - qc + NA_COLS - 1, 0, 2 * NA_COLS - 2)
    sel = _onehot(jnp.logical_and(c == dc, col_in))

    def pick(row_ref):
        r1, r2, r3 = _split3(row_ref[0])
        return _dot(r1, sel) + _dot(r2, sel) + _dot(r3, sel)

    right = (pos[0:1, :] & GRID_W) != 0
    val = jnp.where(right, pick(rc_ref), pick(lc_ref))
    visible = jnp.where(right, rv_ref[0], lv_ref[0]) > 0.0
    o_ref[0] = jnp.where(jnp.logical_and(visible, col_in[0:1, :]), val * ATTN_LOG2E, NEG)


def _na_bias_plan():
    ndr = 2 * NA_ROWS - 1
    assert NA_QROWS >= NA_ROWS // 2 and NA_HALO == NA_ROWS // 2 and (NA_QROWS + 2 * NA_HALO) % 2 == 0
    ok = lambda dr: dr if dr is not None and 0 <= dr < ndr else None
    kinds = [(ok(dr), ok(dr + 1)) for dr in range(-1, ndr)]
    first_key_row = (lambda rq: NA_HALO, lambda rq: rq, lambda rq: NA_QROWS + NA_HALO - NA_ROWS)
    table = []
    for first in first_key_row:
        for rq in range(NA_QROWS):
            for kp in range((NA_QROWS + 2 * NA_HALO) // 2):
                drs = []
                for kk in (2 * kp, 2 * kp + 1):
                    visible = first(rq) <= kk < first(rq) + NA_ROWS
                    drs.append(kk - rq + NA_ROWS - 1 - NA_HALO if visible else None)
                kind = (drs[0], drs[1])
                if kind not in kinds:
                    kinds.append(kind)
                table.append(kinds.index(kind))
    return kinds, table


def _na_bias_tiles(rpb):
    nh, ndr, ndc = rpb.shape
    nc = 32
    kinds, table = _na_bias_plan()
    nk = len(kinds)
    rpb_p = jnp.pad(rpb, ((0, 0), (0, 0), (0, nc - ndc)))
    zero = jnp.zeros((nh, nc), F32)
    lc = jnp.stack([zero if l is None else rpb_p[:, l] for l, _ in kinds])
    rc = jnp.stack([zero if r is None else rpb_p[:, r] for _, r in kinds])
    vis = lambda flags: jnp.broadcast_to(jnp.asarray(flags, F32)[:, None, None], (nk, nh, 1))
    lv = vis([0.0 if l is None else 1.0 for l, _ in kinds])
    rv = vis([0.0 if r is None else 1.0 for _, r in kinds])
    npos = GRID_W * 2 * GRID_W
    row_spec = pl.BlockSpec((1, nh, nc), lambda t: (t, 0, 0))
    flag_spec = pl.BlockSpec((1, nh, 1), lambda t: (t, 0, 0))
    tiles = pl.pallas_call(
        _na_bias_kernel,
        grid=(nk,),
        in_specs=[row_spec, row_spec, flag_spec, flag_spec],
        out_specs=pl.BlockSpec((1, nh, npos), lambda t: (t, 0, 0)),
        out_shape=jax.ShapeDtypeStruct((nk, nh, npos), F32),
        compiler_params=_cparams("arbitrary"),
        name="na_bias",
    )(lc, rc, lv, rv)
    return tiles.reshape(nk, nh, GRID_W, 2 * GRID_W), jnp.asarray(table, jnp.int32)


def neighbourhood_attention(qkv, k_ctx, v_ctx, rpb, n_prompt_tok, n_sample, sample_len):
    d = D_MODEL
    m = NA_QROWS * GRID_W
    nblk = sample_len // m
    assert nblk >= 3
    base = n_prompt_tok // m
    bias, table = _na_bias_tiles(rpb)

    def blk(col, off):
        return pl.BlockSpec((m, d), lambda b, j, tbl: (base + b * nblk + jnp.clip(j + off, 0, nblk - 1), col))

    return pl.pallas_call(
        _na_kernel,
        grid_spec=pltpu.PrefetchScalarGridSpec(
            num_scalar_prefetch=1,
            grid=(n_sample, nblk),
            in_specs=[
                blk(0, 0), blk(1, -1), blk(1, 0), blk(1, 1), blk(2, -1), blk(2, 0), blk(2, 1),
                pl.BlockSpec((1,) + k_ctx.shape[1:], lambda b, j, tbl: (b, 0, 0)),
                pl.BlockSpec((1,) + v_ctx.shape[1:], lambda b, j, tbl: (b, 0, 0)),
                pl.BlockSpec(bias.shape, lambda b, j, tbl: (0, 0, 0, 0)),
            ],
            out_specs=pl.BlockSpec((m, d), lambda b, j, tbl: (b * nblk + j, 0)),
        ),
        out_shape=jax.ShapeDtypeStruct((n_sample * sample_len, d), BF16),
        compiler_params=_cparams("arbitrary", "arbitrary"),
        name="neighbourhood_attention",
    )(table, qkv, qkv, qkv, qkv, qkv, qkv, qkv, k_ctx, v_ctx, bias)


MOE_ROW_TILE = 512
SC_CORES = 2
SC_SUBCORES = 16
SC_GATHER_CHUNK = 128

def _route_tokens(u, wt, carry):
    w1, w2, w3 = _split3(wt)
    lg = _dot_nt(w1, u) + _dot_nt(w2, u) + _dot_nt(w3, u)
    ne, nt = lg.shape
    e = lax.broadcasted_iota(jnp.int32, lg.shape, 0)
    m1 = jnp.max(lg, axis=0, keepdims=True)
    i1 = jnp.min(jnp.where(lg == m1, e, ne), axis=0, keepdims=True)
    sel1 = e == i1
    lg2 = jnp.where(sel1, -jnp.inf, lg)
    m2 = jnp.max(lg2, axis=0, keepdims=True)
    i2 = jnp.min(jnp.where(lg2 == m2, e, ne), axis=0, keepdims=True)
    sel2 = e == i2
    ex = jnp.exp(m2 - m1)
    wa = 1.0 / (1.0 + ex)
    wb = ex / (1.0 + ex)
    comb = jnp.where(sel1, wa, jnp.where(sel2, wb, 0.0))
    assign = jnp.logical_or(sel1, sel2)
    a = jnp.where(assign, 1.0, 0.0)
    row = lax.broadcasted_iota(jnp.int32, (nt, nt), 0)
    col = lax.broadcasted_iota(jnp.int32, (nt, nt), 1)
    tri = _onehot(row < col)
    rank = _dot(a.astype(BF16), tri) + carry
    pos = jnp.where(assign, rank, -1.0)
    return comb, pos, carry + jnp.sum(a, axis=1, keepdims=True)


def _moe_plan(totals, pos, comb, n_tok):
    ne = totals.shape[0]
    tm = MOE_ROW_TILE
    n_tiles = (2 * n_tok) // tm + ne
    tiles_per = (totals + tm - 1) // tm
    tile_end = jnp.cumsum(tiles_per)
    n_used = tile_end[-1]
    base_rows = (tile_end - tiles_per) * tm
    ti = jnp.arange(n_tiles, dtype=jnp.int32)
    tile_expert = jnp.minimum(jnp.sum((ti[:, None] >= tile_end[None, :]).astype(jnp.int32), axis=1), ne - 1)
    last_expert = tile_expert[jnp.maximum(n_used - 1, 0)]
    tile_expert = jnp.where(ti < n_used, tile_expert, last_expert)
    routed = pos >= 0.0
    row = base_rows[:, None] + pos.astype(jnp.int32)
    row_a = jnp.min(jnp.where(routed, row, n_tiles * tm), axis=0)
    row_b = jnp.max(jnp.where(routed, row, -1), axis=0)
    w_a = jnp.sum(jnp.where(jnp.logical_and(routed, row == row_a[None, :]), comb, 0.0), axis=0)
    w_b = jnp.sum(jnp.where(jnp.logical_and(routed, row == row_b[None, :]), comb, 0.0), axis=0)
    tile_rows = jnp.clip(totals[tile_expert] - (ti - (tile_end - tiles_per)[tile_expert]) * tm, 0, tm)
    tile_rows = jnp.where(ti < n_used, tile_rows, 0)
    return dict(n_rows=n_tiles * tm, tile_expert=tile_expert.astype(jnp.int32),
                n_used=n_used.reshape(1).astype(jnp.int32), tile_rows=tile_rows.astype(jnp.int32),
                token_rows=jnp.concatenate([row_a, row_b]).astype(jnp.int32),
                token_weights=jnp.stack([w_a, w_b], axis=1))


def _pack_bf16_pairs(x):
    half = x.shape[1] // 2
    bits = pltpu.bitcast(x.astype(BF16).astype(F32), jnp.int32)
    return (bits[:, :half] & jnp.int32(-65536)) | lax.shift_right_logical(bits[:, half:], jnp.int32(16))


def _unpack_bf16_pairs(p):
    hi = pltpu.bitcast(p & jnp.int32(-65536), F32)
    lo = pltpu.bitcast(lax.shift_left(p, jnp.int32(16)), F32)
    return jnp.concatenate([hi, lo], axis=1)


def sc_gather_rows(table, idx):
    _, w = table.shape
    b = idx.shape[0]
    workers = SC_CORES * SC_SUBCORES
    ch = SC_GATHER_CHUNK
    n_chunks = b // (workers * ch)
    assert n_chunks * workers * ch == b
    mesh = plsc.VectorSubcoreMesh(core_axis_name="c", subcore_axis_name="s", num_cores=SC_CORES,
                                  num_subcores=SC_SUBCORES)

    @functools.partial(
        pl.kernel, mesh=mesh, out_type=jax.ShapeDtypeStruct((b, w), table.dtype),
        scratch_types=[pltpu.VMEM((ch,), jnp.int32), pltpu.VMEM((ch, w), table.dtype), pltpu.SemaphoreType.DMA])
    def gather(table_hbm, idx_hbm, out_hbm, idx_v, rows_v, sem):
        worker = lax.axis_index("s") * SC_CORES + lax.axis_index("c")

        @pl.loop(0, n_chunks)
        def _(c):
            base = (worker * n_chunks + c) * ch
            pltpu.sync_copy(idx_hbm.at[pl.ds(base, ch)], idx_v)
            pltpu.async_copy(table_hbm.at[idx_v], rows_v, sem).wait()
            pltpu.sync_copy(rows_v, out_hbm.at[pl.ds(base, ch)])

    return gather(table, idx)


def sc_scatter_rows(rows, idx, n_out):
    t, w = rows.shape
    copies = idx.shape[0] // t
    workers = SC_CORES * SC_SUBCORES
    ch = SC_GATHER_CHUNK
    n_chunks = t // (workers * ch)
    assert n_chunks * workers * ch == t and copies * t == idx.shape[0]
    mesh = plsc.VectorSubcoreMesh(core_axis_name="c", subcore_axis_name="s", num_cores=SC_CORES,
                                  num_subcores=SC_SUBCORES)

    @functools.partial(
        pl.kernel, mesh=mesh, out_type=jax.ShapeDtypeStruct((n_out, w), rows.dtype),
        scratch_types=[pltpu.VMEM((ch,), jnp.int32), pltpu.VMEM((ch, w), rows.dtype), pltpu.SemaphoreType.DMA])
    def scatter(rows_hbm, idx_hbm, out_hbm, idx_v, rows_v, sem):
        worker = lax.axis_index("s") * SC_CORES + lax.axis_index("c")

        @pl.loop(0, n_chunks)
        def _(c):
            base = (worker * n_chunks + c) * ch
            pltpu.sync_copy(rows_hbm.at[pl.ds(base, ch)], rows_v)
            for k in range(copies):
                pltpu.sync_copy(idx_hbm.at[pl.ds(k * t + base, ch)], idx_v)
                pltpu.async_copy(rows_v, out_hbm.at[idx_v], sem).wait()

    return scatter(rows, idx)


def _gffn_kernel(te_ref, nu_ref, nr_ref, x_ref, wi_ref, wo_ref, y_ref):
    i = pl.program_id(0)
    used = i < nu_ref[0]

    @pl.when(used)
    def _():
        packed = x_ref[...]
        row = lax.broadcasted_iota(jnp.int32, packed.shape, 0)
        x = _unpack_bf16_pairs(jnp.where(row < nr_ref[i], packed, 0)).astype(BF16)
        y = _swiglu_tile(x, lambda c: wi_ref[0, :, c], lambda r: wo_ref[0, r, :], wo_ref.shape[1])
        y_ref[...] = _pack_bf16_pairs(y)

    @pl.when(jnp.logical_not(used))
    def _():
        y_ref[...] = jnp.zeros_like(y_ref)


def moe_grouped_ffn(xs, w_in, w_out, plan):
    nr, half = xs.shape
    ne, dff, d = w_out.shape
    tm = MOE_ROW_TILE
    return pl.pallas_call(
        _gffn_kernel,
        grid_spec=pltpu.PrefetchScalarGridSpec(
            num_scalar_prefetch=3,
            grid=(nr // tm,),
            in_specs=[
                pl.BlockSpec((tm, half), lambda i, te, nu, tr: (jnp.minimum(i, nu[0] - 1), 0)),
                pl.BlockSpec((1, d, 2 * dff), lambda i, te, nu, tr: (te[i], 0, 0)),
                pl.BlockSpec((1, dff, d), lambda i, te, nu, tr: (te[i], 0, 0)),
            ],
            out_specs=pl.BlockSpec((tm, half), lambda i, te, nu, tr: (i, 0)),
        ),
        out_shape=jax.ShapeDtypeStruct((nr, half), jnp.int32),
        compiler_params=_cparams("arbitrary"),
        name="moe_grouped_ffn",
    )(plan["tile_expert"], plan["n_used"], plan["tile_rows"], xs, w_in, w_out)


def _combine_kernel(ya_ref, yb_ref, w_ref, x_ref, mod_ref, lng_ref, lnb_ref, op_ref, os_ref, *, n_prompt_tiles):
    i = pl.program_id(0)
    w = w_ref[...]
    moe = w[:, 0:1] * _unpack_bf16_pairs(ya_ref[...]) + w[:, 1:2] * _unpack_bf16_pairs(yb_ref[...])
    z = DEEPNORM_ALPHA * x_ref[...] + mod_ref[0, 5:6, :] * moe
    xn = _layer_norm(z, lng_ref[...], lnb_ref[...])

    @pl.when(i < n_prompt_tiles)
    def _():
        op_ref[...] = xn

    @pl.when(i >= n_prompt_tiles)
    def _():
        os_ref[...] = xn


def moe_combine(y_tok, weights, x, mod, ln_g, ln_b, n_prompt_tok, sample_len):
    t, d = x.shape
    tt = TOKEN_TILE
    nt = t // tt
    npt = n_prompt_tok // tt
    seg = functools.partial(_seg_of_tile, tile=tt, n_prompt_tok=n_prompt_tok, sample_len=sample_len)
    return pl.pallas_call(
        functools.partial(_combine_kernel, n_prompt_tiles=npt),
        grid=(nt,),
        in_specs=[
            pl.BlockSpec((tt, d // 2), lambda i: (i, 0)),
            pl.BlockSpec((tt, d // 2), lambda i: (nt + i, 0)),
            pl.BlockSpec((tt, 2), lambda i: (i, 0)),
            pl.BlockSpec((tt, d), lambda i: (i, 0)),
            pl.BlockSpec((1, 6, d), lambda i: (seg(i), 0, 0)),
            pl.BlockSpec((1, d), lambda i: (0, 0)),
            pl.BlockSpec((1, d), lambda i: (0, 0)),
        ],
        out_specs=[
            pl.BlockSpec((tt, d), lambda i: (jnp.minimum(i, npt - 1), 0)),
            pl.BlockSpec((tt, d), lambda i: (jnp.maximum(i - npt, 0), 0)),
        ],
        out_shape=[jax.ShapeDtypeStruct((n_prompt_tok, d), F32), jax.ShapeDtypeStruct((t - n_prompt_tok, d), F32)],
        compiler_params=_cparams("arbitrary"),
        name="moe_combine",
    )(y_tok, y_tok, weights, x, mod, ln_g.reshape(1, d), ln_b.reshape(1, d))


def _layer1(x, u, mod1, cache_k, cache_v, ln_g, ln_b, w_qkv, rpb, w_out, w_router, moe_w_in, moe_w_out,
            n_prompt, prompt_len, n_sample, sample_len):
    n_prompt_tok = n_prompt * prompt_len
    d = D_MODEL
    w_qkv = jnp.concatenate([w_qkv[:, :d] * ATTN_Q_SCALE, w_qkv[:, d:]], axis=1).astype(BF16)
    qkv, k_p, v_p = na_qkv(u, w_qkv, n_prompt_tok)
    attn_p = context_attention(qkv, n_prompt, prompt_len)
    k_ctx = cache_k.reshape(n_sample, -1, d).astype(BF16)
    v_ctx = cache_v.reshape(n_sample, -1, d).astype(BF16)
    attn_s = neighbourhood_attention(qkv, k_ctx, v_ctx, rpb, n_prompt_tok, n_sample, sample_len)
    moe_w_in, moe_w_out = moe_w_in.astype(BF16), moe_w_out.astype(BF16)
    x1, u1_packed, comb, pos, totals = mixer_outproj_router(
        attn_p, attn_s, x, mod1, w_out.astype(BF16), ln_g[0], ln_b[0], w_router, sample_len,
        run_after=(moe_w_in, moe_w_out))
    plan = _moe_plan(totals[:, 0].astype(jnp.int32), pos, comb, x.shape[0])
    xs = sc_scatter_rows(u1_packed, plan["token_rows"], plan["n_rows"])
    y = moe_grouped_ffn(xs, moe_w_in, moe_w_out, plan)
    y_tok = sc_gather_rows(y, plan["token_rows"])
    y_p, y_s = moe_combine(y_tok, plan["token_weights"], x1, mod1, ln_g[1], ln_b[1], n_prompt_tok, sample_len)
    return y_p, y_s, k_p, v_p


def kernel(x_prompt, x_sample, state_mlstm_C, state_mlstm_n, state_mlstm_m, cache_na_k, cache_na_v, c, c_ctx,
           ada_w, ada_b, ln_g, ln_b, mlstm_w_in, mlstm_b_gates, mlstm_norm_g, mlstm_w_out, na_w_qkv, na_rpb,
           na_w_out, ffn_w_in, ffn_w_out, moe_w_router, moe_w_in, moe_w_out):
    n_prompt, prompt_len, d = x_prompt.shape
    n_sample, sample_len, _ = x_sample.shape
    assert d == D_MODEL and prompt_len == MLSTM_CHUNK and sample_len % MLSTM_CHUNK == 0
    assert ada_w.shape[0] == DEPTH == 2
    mod = _modulation_tables(c, c_ctx, ada_w, ada_b)
    x2, u2, new_c, new_n, new_m = _layer0(
        x_prompt.reshape(-1, d), x_sample.reshape(-1, d), mod[0], mod[1], state_mlstm_C[:, 0], state_mlstm_n[:, 0], state_mlstm_m[:, 0], ln_g[0], ln_b[0],
        mlstm_w_in[0], mlstm_b_gates[0], mlstm_norm_g[0], mlstm_w_out[0], ffn_w_in[0], ffn_w_out[0],
        n_prompt, prompt_len, n_sample, sample_len)
    y_p, y_s, k_p, v_p = _layer1(
        x2, u2, mod[1], cache_na_k[:, 0], cache_na_v[:, 0], ln_g[1], ln_b[1], na_w_qkv[0], na_rpb[0], na_w_out[0],
        moe_w_router[0], moe_w_in[0], moe_w_out[0], n_prompt, prompt_len, n_sample, sample_len)
    kv_shape = (n_prompt, 1, prompt_len, NA_HEADS, NA_DH)
    return (y_p.reshape(x_prompt.shape), y_s.reshape(x_sample.shape), new_c, new_n, new_m,
            k_p.reshape(kv_shape), v_p.reshape(kv_shape))
```

```python
import functools
import math

import jax
import jax.numpy as jnp
from jax import lax
from jax.experimental import pallas as pl
from jax.experimental.pallas import tpu as pltpu
from jax.experimental.pallas import tpu_sc as plsc

F32 = jnp.float32
BF16 = jnp.bfloat16

D_MODEL = 1024
DEPTH = 2
GRID_W = 64
M_HEADS = 4
M_DK = 128
M_DV = 256
M_HK = M_HEADS * M_DK
NA_HEADS = 16
NA_DH = 64
NA_ROWS = 8
NA_COLS = 16
DEEPNORM_ALPHA = (2.0 * DEPTH) ** 0.25
LN_EPS = 1e-5
NEG = -1e30

VMEM_LIMIT_BYTES = 56 * 1024 * 1024

MLSTM_CHUNK = 256
TOKEN_TILE = 512
FFN_TOKEN_TILE = 512
FFN_SUB = 256
NA_QROWS = 4
NA_HALO = 4


def _cparams(*sem):
    return pltpu.CompilerParams(dimension_semantics=sem, vmem_limit_bytes=VMEM_LIMIT_BYTES)


def _dot(a, b):
    return jnp.dot(a, b, preferred_element_type=F32)


def _dot_nt(a, b):
    return lax.dot_general(a, b, (((1,), (1,)), ((), ())), preferred_element_type=F32)


def _onehot(mask):
    return jnp.where(mask, 1.0, 0.0).astype(BF16)


def _split3(x):
    hi = x.astype(BF16)
    r = x - hi.astype(F32)
    mid = r.astype(BF16)
    lo = (r - mid.astype(F32)).astype(BF16)
    return hi, mid, lo


def _layer_norm(z, g, b):
    mu = jnp.mean(z, axis=-1, keepdims=True)
    zc = z - mu
    var = jnp.mean(zc * zc, axis=-1, keepdims=True)
    return zc * lax.rsqrt(var + LN_EPS) * g + b


def _log_sigmoid(x):
    return jnp.minimum(x, 0.0) - jnp.log(1.0 + jnp.exp(-jnp.abs(x)))


def _seg_of_tile(i, tile, n_prompt_tok, sample_len):
    tok = i * tile
    return jnp.where(tok < n_prompt_tok, 0, 1 + (tok - n_prompt_tok) // sample_len)


def _adaln_kernel(c_ref, w_ref, b_ref, o_ref):
    c = c_ref[...]
    a = (c * jax.nn.sigmoid(c)).astype(BF16)
    o_ref[0] = _dot(a, w_ref[0].astype(BF16)) + b_ref[0]


def adaln(cvec, ada_w, ada_b):
    depth, d, n = ada_w.shape
    tn = 1536
    return pl.pallas_call(
        _adaln_kernel,
        grid=(depth, n // tn),
        in_specs=[
            pl.BlockSpec((8, d), lambda l, j: (0, 0)),
            pl.BlockSpec((1, d, tn), lambda l, j: (l, 0, j)),
            pl.BlockSpec((1, 1, tn), lambda l, j: (l, 0, j)),
        ],
        out_specs=pl.BlockSpec((1, 8, tn), lambda l, j: (l, 0, j)),
        out_shape=jax.ShapeDtypeStruct((depth, 8, n), F32),
        compiler_params=_cparams("arbitrary", "arbitrary"),
        name="adaln",
    )(cvec, ada_w, ada_b.reshape(depth, 1, n))


def _inproj_kernel(xp_ref, xs_ref, mod_ref, w_ref, wg_ref, wgt_ref, bg_ref, bgt_ref,
                   qkv_ref, o_ref, g_ref, gt_ref, *, n_prompt_tiles):
    x = jnp.where(pl.program_id(0) < n_prompt_tiles, xp_ref[...], xs_ref[...])
    u = (x * (1.0 + mod_ref[0, 1:2, :]) + mod_ref[0, 0:1, :]).astype(BF16)
    p = _dot(u, w_ref[...])
    nqkv = qkv_ref.shape[1]
    qkv_ref[...] = p[:, :nqkv].astype(BF16)
    o_ref[...] = p[:, nqkv:]
    g = _dot(u, wg_ref[...]) + bg_ref[...]
    col = lax.broadcasted_iota(jnp.int32, g.shape, 1)
    g_ref[...] = jnp.where(((col >> 2) & 1) == 1, _log_sigmoid(g), g)
    gt = _dot_nt(wgt_ref[...], u) + bgt_ref[...]
    row = lax.broadcasted_iota(jnp.int32, gt.shape, 0)
    gt_ref[...] = jnp.where(((row >> 2) & 1) == 1, _log_sigmoid(gt), gt)


def _split_token_specs(tile, d, n_prompt_tiles):
    return [pl.BlockSpec((tile, d), lambda i: (jnp.minimum(i, n_prompt_tiles - 1), 0)),
            pl.BlockSpec((tile, d), lambda i: (jnp.maximum(i - n_prompt_tiles, 0), 0))]


def mlstm_inproj(xp, xs, mod, w_main, w_gate, b_gate, sample_len):
    n_prompt_tok, d = xp.shape
    t = n_prompt_tok + xs.shape[0]
    n_main = w_main.shape[1]
    n_qkv = 2 * M_HK + D_MODEL
    ng = w_gate.shape[1]
    tt = TOKEN_TILE
    npt = n_prompt_tok // tt
    seg = functools.partial(_seg_of_tile, tile=tt, n_prompt_tok=n_prompt_tok, sample_len=sample_len)
    return pl.pallas_call(
        functools.partial(_inproj_kernel, n_prompt_tiles=npt),
        grid=(t // tt,),
        in_specs=_split_token_specs(tt, d, npt) + [
            pl.BlockSpec((1, 6, d), lambda i: (seg(i), 0, 0)),
            pl.BlockSpec((d, n_main), lambda i: (0, 0)),
            pl.BlockSpec((d, ng), lambda i: (0, 0)),
            pl.BlockSpec((ng, d), lambda i: (0, 0)),
            pl.BlockSpec((1, ng), lambda i: (0, 0)),
            pl.BlockSpec((ng, 1), lambda i: (0, 0)),
        ],
        out_specs=[
            pl.BlockSpec((tt, n_qkv), lambda i: (i, 0)),
            pl.BlockSpec((tt, n_main - n_qkv), lambda i: (i, 0)),
            pl.BlockSpec((tt, ng), lambda i: (i, 0)),
            pl.BlockSpec((ng, tt), lambda i: (0, i)),
        ],
        out_shape=[
            jax.ShapeDtypeStruct((t, n_qkv), BF16),
            jax.ShapeDtypeStruct((t, n_main - n_qkv), F32),
            jax.ShapeDtypeStruct((t, ng), F32),
            jax.ShapeDtypeStruct((ng, t), F32),
        ],
        compiler_params=_cparams("arbitrary"),
        name="mlstm_inproj",
    )(xp, xs, mod, w_main, w_gate, w_gate.T, b_gate.reshape(1, ng), b_gate.reshape(ng, 1))


def _mlstm_gate_sums(g, gt, backward):
    L = g.shape[0]
    row = lax.broadcasted_iota(jnp.int32, (L, L), 0)
    col = lax.broadcasted_iota(jnp.int32, (L, L), 1)
    lower = col <= row
    upper = col >= row
    tri_col = _onehot(upper if backward else lower)
    tri_row = _onehot(lower if backward else upper)
    ghi, gmid, glo = _split3(g)
    b_cols = _dot(tri_col, ghi) + _dot(tri_col, gmid) + _dot(tri_col, glo)
    thi, tmid, tlo = _split3(gt)
    b_rows = _dot(thi, tri_row) + _dot(tmid, tri_row) + _dot(tlo, tri_row)
    return b_cols, b_rows


def _mlstm_heads(items):
    scale = M_DK ** -0.5
    log_scale = math.log(scale)
    L = items[0][0].shape[0]
    row = lax.broadcasted_iota(jnp.int32, (L, L), 0)
    col = lax.broadcasted_iota(jnp.int32, (L, L), 1)
    masks = {False: col <= row, True: col >= row}
    n = range(len(items))
    qh, kh, vh, li_row, lf_row, b_row, b_col, backward, state = zip(*items)
    has_state = [st is not None for st in state]
    m_prev = [st[2] if st is not None else 0.0 for st in state]

    qk = [_dot_nt(qh[i], kh[i]) for i in n]
    qc = [_dot(qh[i], state[i][0].astype(BF16)) if has_state[i] else None for i in n]
    qn = [_dot_nt(qh[i], state[i][1].astype(BF16))[:, 0:1] if has_state[i] else None for i in n]
    total = [jnp.sum(lf_row[i], axis=1, keepdims=True) for i in n]
    d = [jnp.where(masks[backward[i]], b_col[i] + (li_row[i] - b_row[i] + log_scale), NEG) for i in n]
    inter = [b_col[i] + m_prev[i] if has_state[i] else b_col[i] for i in n]
    m_t = [jnp.maximum(inter[i], jnp.max(d[i], axis=1, keepdims=True) - log_scale) for i in n]
    p = [jnp.exp(d[i] - m_t[i]) for i in n]
    s = [qk[i] * p[i] for i in n]
    den = [jnp.sum(s[i], axis=1, keepdims=True) for i in n]
    num = [_dot(s[i].astype(BF16), vh[i]) for i in n]
    a = [jnp.exp(inter[i] - m_t[i]) * scale if has_state[i] else None for i in n]
    num = [a[i] * qc[i] + num[i] if has_state[i] else num[i] for i in n]
    den = [a[i] * qn[i] + den[i] if has_state[i] else den[i] for i in n]
    hh = [num[i] / jnp.maximum(jnp.abs(den[i]), jnp.exp(-m_t[i])) for i in n]

    g_row = [total[i] - b_row[i] + li_row[i] for i in n]
    m_new = [jnp.maximum(total[i] + m_prev[i], jnp.max(g_row[i], axis=1, keepdims=True)) for i in n]
    ws_row = [jnp.exp(g_row[i] - m_new[i]) for i in n]
    kt = [kh[i].astype(F32).T for i in n]
    c_new = [_dot((kt[i] * ws_row[i]).astype(BF16), vh[i]) for i in n]
    n_new = [_dot(jnp.broadcast_to(ws_row[i], (8, L)).astype(BF16), kh[i]) for i in n]
    a0 = [jnp.exp(total[i] + m_prev[i] - m_new[i]) if has_state[i] else None for i in n]
    c_new = [a0[i] * state[i][0] + c_new[i] if has_state[i] else c_new[i] for i in n]
    n_new = [a0[i] * state[i][1] + n_new[i] if has_state[i] else n_new[i] for i in n]
    return hh, c_new, n_new, m_new


def _head_norm_gate(ht, norm_g, ogate):
    mu = jnp.mean(ht, axis=-1, keepdims=True)
    hc = ht - mu
    var = jnp.mean(hc * hc, axis=-1, keepdims=True)
    return (hc * lax.rsqrt(var + LN_EPS) * norm_g * jax.nn.sigmoid(ogate)).astype(BF16)


def _head_operands(q_ref, k_ref, v_ref, h):
    return (q_ref[:, h * M_DK:(h + 1) * M_DK], k_ref[:, h * M_DK:(h + 1) * M_DK], v_ref[:, h * M_DV:(h + 1) * M_DV])


def _gate_operands(gt, b_rows, b_cols, direction, h):
    ci = direction * 8 + h
    cf = direction * 8 + 4 + h
    return gt[ci:ci + 1, :], gt[cf:cf + 1, :], b_rows[cf:cf + 1, :], b_cols[:, cf:cf + 1]


def _mlstm_prompt_kernel(q_ref, k_ref, v_ref, g_ref, gt_ref, o_ref, ng_ref, a_ref, cf_ref, nf_ref, mf_ref):
    g, gt = g_ref[...], gt_ref[...]
    sums = [_mlstm_gate_sums(g, gt, direction == 1) for direction in (0, 1)]
    keys = [(direction, h) for direction in (0, 1) for h in range(M_HEADS)]
    items = [_head_operands(q_ref, k_ref, v_ref, h)
             + _gate_operands(gt, sums[direction][1], sums[direction][0], direction, h) + (direction == 1, None)
             for direction, h in keys]
    hh, c_new, n_new, m_new = _mlstm_heads(items)
    for i, (direction, h) in enumerate(keys):
        cf_ref[0, 0, direction, h] = c_new[i]
        nf_ref[0, 0, direction, h] = n_new[i]
        mf_ref[0, 0, direction, h] = jnp.broadcast_to(m_new[i], mf_ref.shape[-2:])
    for h in range(M_HEADS):
        sl = slice(h * M_DV, (h + 1) * M_DV)
        a_ref[:, sl] = _head_norm_gate(hh[h] + hh[M_HEADS + h], ng_ref[:, sl], o_ref[:, sl])


def _mlstm_sample_kernel(*refs, direction, n_units):
    backward = direction == 1
    refs = list(refs)
    unit_refs = [tuple(refs.pop(0) for _ in range(5)) for _ in range(n_units)]
    c0_ref, n0_ref, m0_ref = refs.pop(0), refs.pop(0), refs.pop(0)
    if backward:
        hprev_ref = refs.pop(0)
        o_refs = [refs.pop(0) for _ in range(n_units)]
        ng_ref = refs.pop(0)
    out_ref, c_scr, n_scr, m_scr = refs

    @pl.when(pl.program_id(0) == 0)
    def _():
        c_scr[...] = c0_ref[...]
        n_scr[...] = n0_ref[...]
        m_scr[...] = m0_ref[...]

    keys, items = [], []
    for u, (q_ref, k_ref, v_ref, g_ref, gt_ref) in enumerate(unit_refs):
        gt = gt_ref[...]
        b_cols, b_rows = _mlstm_gate_sums(g_ref[...], gt, backward)
        for h in range(M_HEADS):
            state = (c_scr[u, h], n_scr[u, h], m_scr[u, h][0:1, 0:1])
            keys.append((u, h))
            items.append(_head_operands(q_ref, k_ref, v_ref, h) + _gate_operands(gt, b_rows, b_cols, direction, h)
                         + (backward, state))
    hh, c_new, n_new, m_new = _mlstm_heads(items)
    for i, (u, h) in enumerate(keys):
        c_scr[u, h] = c_new[i]
        n_scr[u, h] = n_new[i]
        m_scr[u, h] = jnp.broadcast_to(m_new[i], m_scr.shape[-2:])
        sl = slice(h * M_DV, (h + 1) * M_DV)
        if backward:
            out_ref[u, :, sl] = _head_norm_gate(hprev_ref[u, :, sl] + hh[i], ng_ref[:, sl], o_refs[u][:, sl])
        else:
            out_ref[u, :, sl] = hh[i]


def _mlstm_chunk_specs(block_of):
    L = MLSTM_CHUNK
    return [
        pl.BlockSpec((L, M_HK), lambda s: (block_of(s), 0)),
        pl.BlockSpec((L, M_HK), lambda s: (block_of(s), 1)),
        pl.BlockSpec((L, D_MODEL), lambda s: (block_of(s), 1)),
        pl.BlockSpec((L, 16), lambda s: (block_of(s), 0)),
        pl.BlockSpec((16, L), lambda s: (0, block_of(s))),
    ]


def mlstm_prompt(qkv, g, gt, ogate, norm_g, n_prompt):
    L = MLSTM_CHUNK
    state_blk = lambda *tail: pl.BlockSpec((1, 1, 2, M_HEADS) + tail, lambda s: (s, 0, 0, 0, 0, 0))
    return pl.pallas_call(
        _mlstm_prompt_kernel,
        grid=(n_prompt,),
        in_specs=_mlstm_chunk_specs(lambda s: s) + [
            pl.BlockSpec((L, D_MODEL), lambda s: (s, 0)),
            pl.BlockSpec((1, D_MODEL), lambda s: (0, 0)),
        ],
        out_specs=[pl.BlockSpec((L, D_MODEL), lambda s: (s, 0)),
                   state_blk(M_DK, M_DV), state_blk(8, M_DK), state_blk(8, 128)],
        out_shape=[
            jax.ShapeDtypeStruct((n_prompt * L, D_MODEL), BF16),
            jax.ShapeDtypeStruct((n_prompt, 1, 2, M_HEADS, M_DK, M_DV), F32),
            jax.ShapeDtypeStruct((n_prompt, 1, 2, M_HEADS, 8, M_DK), F32),
            jax.ShapeDtypeStruct((n_prompt, 1, 2, M_HEADS, 8, 128), F32),
        ],
        compiler_params=_cparams("arbitrary"),
        name="mlstm_prompt",
    )(qkv, qkv, qkv, g, gt, ogate, norm_g.reshape(1, D_MODEL))


def mlstm_sample(direction, qkv, g, gt, c0, n0, m0, first_block, chunks_per_sample,
                 hprev=None, ogate=None, norm_g=None):
    L = MLSTM_CHUNK
    backward = direction == 1
    cps = chunks_per_sample
    n_units = c0.shape[0]
    pos = (lambda s: cps - 1 - s) if backward else (lambda s: s)
    unit_block = [functools.partial(lambda s, u: first_block + u * cps + pos(s), u=u) for u in range(n_units)]
    whole = lambda a: pl.BlockSpec(a.shape, lambda s: (0,) * a.ndim)
    in_specs, args = [], []
    for u in range(n_units):
        in_specs += _mlstm_chunk_specs(unit_block[u])
        args += [qkv, qkv, qkv, g, gt]
    in_specs += [whole(c0), whole(n0), whole(m0)]
    args += [c0, n0, m0]
    if backward:
        in_specs += [pl.BlockSpec((n_units, L, D_MODEL), lambda s: (0, pos(s), 0))]
        in_specs += [pl.BlockSpec((L, D_MODEL), functools.partial(lambda s, u: (unit_block[u](s), 0), u=u))
                     for u in range(n_units)]
        in_specs += [pl.BlockSpec((1, D_MODEL), lambda s: (0, 0))]
        args += [hprev] + [ogate] * n_units + [norm_g.reshape(1, D_MODEL)]
    return pl.pallas_call(
        functools.partial(_mlstm_sample_kernel, direction=direction, n_units=n_units),
        grid=(cps,),
        in_specs=in_specs,
        out_specs=pl.BlockSpec((n_units, L, D_MODEL), lambda s: (0, pos(s), 0)),
        out_shape=jax.ShapeDtypeStruct((n_units, cps * L, D_MODEL), BF16 if backward else F32),
        scratch_shapes=[pltpu.VMEM(c0.shape, F32), pltpu.VMEM(n0.shape, F32), pltpu.VMEM(m0.shape, F32)],
        compiler_params=_cparams("arbitrary"),
        name="mlstm_sample_bwd" if backward else "mlstm_sample_fwd",
    )(*args)


def _outproj_router_kernel(ap_ref, as_ref, x_ref, mod_ref, w_ref, lng_ref, lnb_ref, wr_ref, *refs,
                           n_prompt_tiles):
    xo_ref, up_ref, comb_ref, pos_ref, total_ref, carry_scr = refs[-6:]
    i = pl.program_id(0)

    @pl.when(i == 0)
    def _():
        carry_scr[...] = jnp.zeros_like(carry_scr)

    a = jnp.where(i < n_prompt_tiles, ap_ref[...], as_ref[...])
    z = DEEPNORM_ALPHA * x_ref[...] + mod_ref[0, 2:3, :] * _dot(a, w_ref[...])
    xn = _layer_norm(z, lng_ref[...], lnb_ref[...])
    xo_ref[...] = xn
    u = xn * (1.0 + mod_ref[0, 4:5, :]) + mod_ref[0, 3:4, :]
    up_ref[...] = _pack_bf16_pairs(u)
    comb, pos, carry_new = _route_tokens(u.astype(BF16), wr_ref[...], carry_scr[:, 0:1])
    comb_ref[...] = comb
    pos_ref[...] = pos
    carry_scr[...] = jnp.broadcast_to(carry_new, carry_scr.shape)
    total_ref[...] = jnp.broadcast_to(carry_new, carry_scr.shape)


def mixer_outproj_router(a_p, a_s, x, mod, w, ln_g, ln_b, w_router, sample_len, run_after=()):
    t, d = x.shape
    ne = w_router.shape[1]
    tt = TOKEN_TILE
    npt = a_p.shape[0] // tt
    seg = functools.partial(_seg_of_tile, tile=tt, n_prompt_tok=a_p.shape[0], sample_len=sample_len)
    return pl.pallas_call(
        functools.partial(_outproj_router_kernel, n_prompt_tiles=npt),
        grid=(t // tt,),
        in_specs=_split_token_specs(tt, d, npt) + [
            pl.BlockSpec((tt, d), lambda i: (i, 0)),
            pl.BlockSpec((1, 6, d), lambda i: (seg(i), 0, 0)),
            pl.BlockSpec((d, d), lambda i: (0, 0)),
            pl.BlockSpec((1, d), lambda i: (0, 0)),
            pl.BlockSpec((1, d), lambda i: (0, 0)),
            pl.BlockSpec((ne, d), lambda i: (0, 0)),
        ] + [pl.BlockSpec(memory_space=pl.ANY)] * len(run_after),
        out_specs=[
            pl.BlockSpec((tt, d), lambda i: (i, 0)),
            pl.BlockSpec((tt, d // 2), lambda i: (i, 0)),
            pl.BlockSpec((ne, tt), lambda i: (0, i)),
            pl.BlockSpec((ne, tt), lambda i: (0, i)),
            pl.BlockSpec((ne, 128), lambda i: (0, 0)),
        ],
        out_shape=[
            jax.ShapeDtypeStruct((t, d), F32),
            jax.ShapeDtypeStruct((t, d // 2), jnp.int32),
            jax.ShapeDtypeStruct((ne, t), F32),
            jax.ShapeDtypeStruct((ne, t), F32),
            jax.ShapeDtypeStruct((ne, 128), F32),
        ],
        scratch_shapes=[pltpu.VMEM((ne, 128), F32)],
        compiler_params=_cparams("arbitrary"),
        name="mixer_outproj_router",
    )(a_p, a_s, x, mod, w, ln_g.reshape(1, d), ln_b.reshape(1, d), w_router.T, *run_after)


def _swiglu_tile(x, wi, wo, dff):
    sub = FFN_SUB
    proj = lambda j: (_dot(x, wi(slice(j * sub, (j + 1) * sub))), _dot(x, wi(slice(dff + j * sub, dff + (j + 1) * sub))))
    acts, cur = [], proj(0)
    for j in range(dff // sub):
        nxt = proj(j + 1) if (j + 1) * sub < dff else None
        gp, up = cur
        acts.append((gp * jax.nn.sigmoid(gp) * up).astype(BF16))
        cur = nxt
    return _dot(jnp.concatenate(acts, axis=1), wo(slice(0, dff)))


def _mixer_ffn_kernel(ap_ref, as_ref, xp_ref, xs_ref, mod_ref, modn_ref, wmix_ref, wi_ref, wo_ref, ln_ref,
                      xo_ref, uo_ref, *, n_prompt_tiles):
    is_prompt = pl.program_id(0) < n_prompt_tiles
    a = jnp.where(is_prompt, ap_ref[...], as_ref[...])
    x = jnp.where(is_prompt, xp_ref[...], xs_ref[...])
    z = DEEPNORM_ALPHA * x + mod_ref[0, 2:3, :] * _dot(a, wmix_ref[...])
    x1 = _layer_norm(z, ln_ref[0:1, :], ln_ref[1:2, :])
    u1 = (x1 * (1.0 + mod_ref[0, 4:5, :]) + mod_ref[0, 3:4, :]).astype(BF16)
    f = _swiglu_tile(u1, lambda c: wi_ref[:, c], lambda r: wo_ref[r, :], wo_ref.shape[0])
    x2 = _layer_norm(DEEPNORM_ALPHA * x1 + mod_ref[0, 5:6, :] * f, ln_ref[2:3, :], ln_ref[3:4, :])
    xo_ref[...] = x2
    uo_ref[...] = (x2 * (1.0 + modn_ref[0, 1:2, :]) + modn_ref[0, 0:1, :]).astype(BF16)


def _resident(shape):
    return pl.BlockSpec(shape, lambda i: (0,) * len(shape), pipeline_mode=pl.Buffered(1))


def mixer_ffn(a_p, a_s, xp, xs, mod, mod_next, w_mix, w_in, w_out, ln_g, ln_b, sample_len):
    n_prompt_tok, d = xp.shape
    t = n_prompt_tok + xs.shape[0]
    tm = FFN_TOKEN_TILE
    npt = n_prompt_tok // tm
    seg = functools.partial(_seg_of_tile, tile=tm, n_prompt_tok=n_prompt_tok, sample_len=sample_len)
    ln = jnp.stack([ln_g[0], ln_b[0], ln_g[1], ln_b[1]])
    return pl.pallas_call(
        functools.partial(_mixer_ffn_kernel, n_prompt_tiles=npt),
        grid=(t // tm,),
        in_specs=_split_token_specs(tm, d, npt) + _split_token_specs(tm, d, npt) + [
            pl.BlockSpec((1, 6, d), lambda i: (seg(i), 0, 0)),
            pl.BlockSpec((1, 6, d), lambda i: (seg(i), 0, 0)),
            _resident(w_mix.shape), _resident(w_in.shape), _resident(w_out.shape), _resident(ln.shape),
        ],
        out_specs=[pl.BlockSpec((tm, d), lambda i: (i, 0)), pl.BlockSpec((tm, d), lambda i: (i, 0))],
        out_shape=[jax.ShapeDtypeStruct((t, d), F32), jax.ShapeDtypeStruct((t, d), BF16)],
        compiler_params=_cparams("arbitrary"),
        name="mixer_ffn",
    )(a_p, a_s, xp, xs, mod, mod_next, w_mix, w_in, w_out, ln)


def _modulation_tables(c, c_ctx, ada_w, ada_b):
    n_s = c.shape[0]
    cvec = jnp.concatenate([c_ctx[None, :], c, jnp.zeros((8 - 1 - n_s, c.shape[1]), F32)], axis=0)
    mod = adaln(cvec, ada_w, ada_b)
    return mod[:, :1 + n_s, :].reshape(ada_w.shape[0], 1 + n_s, 6, c.shape[1])


def _layer0(xp, xs, mod0, mod1, state_c, state_n, state_m, ln_g, ln_b, w_in, b_gates, norm_g, w_out,
            ffn_w_in, ffn_w_out, n_prompt, prompt_len, n_sample, sample_len):
    n_prompt_tok = n_prompt * prompt_len
    n_main = 2 * M_HK + 2 * D_MODEL
    qkv, ogate, g, gt = mlstm_inproj(xp, xs, mod0, w_in[:, :n_main].astype(BF16), w_in[:, n_main:].astype(BF16),
                                     b_gates.reshape(-1), sample_len)
    npc = n_prompt_tok // MLSTM_CHUNK
    cps = sample_len // MLSTM_CHUNK
    rep = lambda a: jnp.broadcast_to(a[..., None, :], a.shape[:-1] + (8, a.shape[-1]))
    n0 = rep(state_n)
    m0 = jnp.broadcast_to(state_m[..., None, None], state_m.shape + (8, 128))
    a_p, new_c, nf, mf = mlstm_prompt(qkv, g, gt, ogate, norm_g, n_prompt)
    hf = mlstm_sample(0, qkv, g, gt, state_c[:, 0], n0[:, 0], m0[:, 0], npc, cps)
    a_s = mlstm_sample(1, qkv, g, gt, state_c[:, 1], n0[:, 1], m0[:, 1], npc, cps,
                       hprev=hf, ogate=ogate, norm_g=norm_g)
    x2, u2 = mixer_ffn(a_p, a_s.reshape(-1, D_MODEL), xp, xs, mod0, mod1, w_out.astype(BF16),
                       ffn_w_in.astype(BF16), ffn_w_out.astype(BF16), ln_g, ln_b, sample_len)
    return x2, u2, new_c, nf[..., 0, :], mf[..., 0, 0]


def _qkv_kernel(u_ref, w_ref, qkv_ref, k_ref, v_ref, *, n_prompt_tiles):
    p = _dot(u_ref[...], w_ref[...])
    qkv_ref[...] = p.astype(BF16)
    tt, d = u_ref.shape

    @pl.when(pl.program_id(0) < n_prompt_tiles)
    def _():
        for h in range(NA_HEADS):
            rows = pl.ds(h, tt, stride=NA_HEADS)
            k_ref[rows, :] = p[:, d + h * NA_DH:d + (h + 1) * NA_DH]
            v_ref[rows, :] = p[:, 2 * d + h * NA_DH:2 * d + (h + 1) * NA_DH]


def na_qkv(u, w, n_prompt_tok):
    t, d = u.shape
    tt = TOKEN_TILE
    npt = n_prompt_tok // tt
    kv_spec = pl.BlockSpec((tt * NA_HEADS, NA_DH), lambda i: (jnp.minimum(i, npt - 1), 0))
    kv_shape = jax.ShapeDtypeStruct((n_prompt_tok * NA_HEADS, NA_DH), F32)
    return pl.pallas_call(
        functools.partial(_qkv_kernel, n_prompt_tiles=npt),
        grid=(t // tt,),
        in_specs=[pl.BlockSpec((tt, d), lambda i: (i, 0)), pl.BlockSpec((d, 3 * d), lambda i: (0, 0))],
        out_specs=[pl.BlockSpec((tt, 3 * d), lambda i: (i, 0)), kv_spec, kv_spec],
        out_shape=[jax.ShapeDtypeStruct((t, 3 * d), BF16), kv_shape, kv_shape],
        compiler_params=_cparams("arbitrary"),
        name="na_qkv",
    )(u, w)


ATTN_LOG2E = math.log2(math.e)
ATTN_Q_SCALE = NA_DH ** -0.5 * ATTN_LOG2E


def _head_pair_masks():
    lane = lax.broadcasted_iota(jnp.int32, (1, 2 * NA_DH), 1)
    return (_onehot(lane < NA_DH), _onehot(lane >= NA_DH))


def _softmax_pv(score_blocks, value_blocks):
    mx = None
    for s in score_blocks:
        bm = jnp.max(s, axis=1, keepdims=True)
        mx = bm if mx is None else jnp.maximum(mx, bm)
    acc, denom = None, None
    for s, v in zip(score_blocks, value_blocks):
        p = jnp.exp2(s - mx)
        ps = jnp.sum(p, axis=1, keepdims=True)
        pv = _dot(p.astype(BF16), v)
        acc = pv if acc is None else acc + pv
        denom = ps if denom is None else denom + ps
    return acc / denom


def _ctx_attn_kernel(q_ref, k_ref, v_ref, o_ref):
    masks = _head_pair_masks()
    for hp in range(NA_HEADS // 2):
        sl = slice(hp * 2 * NA_DH, (hp + 1) * 2 * NA_DH)
        q2, k2, v2 = q_ref[:, sl], k_ref[:, sl], v_ref[:, sl]
        o2 = None
        for msk in masks:
            s = _dot_nt(q2 * msk, k2)
            o = _softmax_pv([s], [v2 * msk])
            o2 = o if o2 is None else o2 + o
        o_ref[:, sl] = o2.astype(o_ref.dtype)


def context_attention(qkv, n_prompt, prompt_len):
    t = n_prompt * prompt_len
    d = D_MODEL
    return pl.pallas_call(
        _ctx_attn_kernel,
        grid=(n_prompt,),
        in_specs=[
            pl.BlockSpec((prompt_len, d), lambda b: (b, 0)),
            pl.BlockSpec((prompt_len, d), lambda b: (b, 1)),
            pl.BlockSpec((prompt_len, d), lambda b: (b, 2)),
        ],
        out_specs=pl.BlockSpec((prompt_len, d), lambda b: (b, 0)),
        out_shape=jax.ShapeDtypeStruct((t, d), BF16),
        compiler_params=_cparams("arbitrary"),
        name="context_attention",
    )(qkv, qkv, qkv)


def _na_kernel(tile_ref, q_ref, kp_ref, kc_ref, kn_ref, vp_ref, vc_ref, vn_ref, kctx_ref, vctx_ref, bias_ref,
               o_ref):
    j = pl.program_id(1)
    nblk = pl.num_programs(1)
    m = q_ref.shape[0]
    halo = NA_HALO * GRID_W
    n_pairs = (NA_QROWS + 2 * NA_HALO) // 2
    variant = jnp.where(j == 0, 0, jnp.where(j == nblk - 1, 2, 1))
    kinds = [[tile_ref[(variant * NA_QROWS + rq) * n_pairs + kp] for kp in range(n_pairs)]
             for rq in range(NA_QROWS)]

    def bias_of(head):
        return jnp.concatenate(
            [jnp.concatenate([bias_ref[kinds[rq][kp], head] for kp in range(n_pairs)], axis=1)
             for rq in range(NA_QROWS)], axis=0)

    masks = _head_pair_masks()
    for hp in range(NA_HEADS // 2):
        sl = slice(hp * 2 * NA_DH, (hp + 1) * 2 * NA_DH)
        q2 = q_ref[:, sl]
        kloc = jnp.concatenate([kp_ref[m - halo:, sl], kc_ref[:, sl], kn_ref[:halo, sl]], axis=0)
        vloc = jnp.concatenate([vp_ref[m - halo:, sl], vc_ref[:, sl], vn_ref[:halo, sl]], axis=0)
        kctx = kctx_ref[0][:, sl]
        vctx = vctx_ref[0][:, sl]
        o2 = None
        for half, msk in enumerate(masks):
            qm = q2 * msk
            s_loc = _dot_nt(qm, kloc) + bias_of(2 * hp + half)
            s_ctx = _dot_nt(qm, kctx)
            o = _softmax_pv([s_loc, s_ctx], [vloc * msk, vctx * msk])
            o2 = o if o2 is None else o2 + o
        o_ref[:, sl] = o2.astype(o_ref.dtype)


def _na_bias_kernel(rpb_ref, o_ref, *, kinds, n_heads):
    nc, npos = rpb_ref.shape[1], o_ref.shape[2]
    pair_shift = (2 * GRID_W).bit_length() - 1
    c = lax.broadcasted_iota(jnp.int32, (nc, npos), 0)
    pos = lax.broadcasted_iota(jnp.int32, (nc, npos), 1)
    qc = pos >> pair_shift
    kc = pos & (GRID_W - 1)
    c_start = jnp.clip(qc - NA_COLS // 2, 0, GRID_W - NA_COLS)
    col_in = jnp.logical_and(kc >= c_start, kc < c_start + NA_COLS)
    dc = jnp.clip(kc - qc + NA_COLS - 1, 0, 2 * NA_COLS - 2)
    hit = jnp.logical_and(c == dc, col_in)
    right = (pos & GRID_W) != 0
    r1, r2, r3 = _split3(rpb_ref[...] * ATTN_LOG2E)

    def placed(side):
        sel = _onehot(jnp.logical_and(hit, right == side))
        return _dot(r1, sel) + _dot(r2, sel) + _dot(r3, sel)

    left, rght = placed(False), placed(True)
    rows = lambda v, dr: v[dr * n_heads:(dr + 1) * n_heads]
    for k, (dl, dr) in enumerate(kinds):
        shown = col_in[0:1, :]
        if dl is None or dr is None:
            side_ok = jnp.zeros_like(shown) if dl is None and dr is None else (right[0:1, :] == (dl is None))
            shown = jnp.logical_and(shown, side_ok)
        parts = ([rows(left, dl)] if dl is not None else []) + ([rows(rght, dr)] if dr is not None else [])
        val = sum(parts) if parts else jnp.zeros((n_heads, npos), F32)
        o_ref[k] = jnp.where(shown, val, NEG)


def _na_bias_plan():
    ndr = 2 * NA_ROWS - 1
    assert NA_QROWS >= NA_ROWS // 2 and NA_HALO == NA_ROWS // 2 and (NA_QROWS + 2 * NA_HALO) % 2 == 0
    ok = lambda dr: dr if dr is not None and 0 <= dr < ndr else None
    kinds = [(ok(dr), ok(dr + 1)) for dr in range(-1, ndr)]
    first_key_row = (lambda rq: NA_HALO, lambda rq: rq, lambda rq: NA_QROWS + NA_HALO - NA_ROWS)
    table = []
    for first in first_key_row:
        for rq in range(NA_QROWS):
            for kp in range((NA_QROWS + 2 * NA_HALO) // 2):
                drs = []
                for kk in (2 * kp, 2 * kp + 1):
                    visible = first(rq) <= kk < first(rq) + NA_ROWS
                    drs.append(kk - rq + NA_ROWS - 1 - NA_HALO if visible else None)
                kind = (drs[0], drs[1])
                if kind not in kinds:
                    kinds.append(kind)
                table.append(kinds.index(kind))
    return kinds, table


def _na_bias_tiles(rpb):
    nh, ndr, ndc = rpb.shape
    nc = 32
    kinds, table = _na_bias_plan()
    nk = len(kinds)
    rows = jnp.pad(jnp.transpose(rpb, (1, 0, 2)), ((0, 0), (0, 0), (0, nc - ndc))).reshape(ndr * nh, nc)
    npos = GRID_W * 2 * GRID_W
    tiles = pl.pallas_call(
        functools.partial(_na_bias_kernel, kinds=tuple(kinds), n_heads=nh),
        out_shape=jax.ShapeDtypeStruct((nk, nh, npos), F32),
        compiler_params=pltpu.CompilerParams(vmem_limit_bytes=VMEM_LIMIT_BYTES),
        name="na_bias",
    )(rows)
    return tiles.reshape(nk, nh, GRID_W, 2 * GRID_W), jnp.asarray(table, jnp.int32)


def neighbourhood_attention(qkv, k_ctx, v_ctx, rpb, n_prompt_tok, n_sample, sample_len):
    d = D_MODEL
    m = NA_QROWS * GRID_W
    nblk = sample_len // m
    assert nblk >= 3
    base = n_prompt_tok // m
    bias, table = _na_bias_tiles(rpb)

    def blk(col, off):
        return pl.BlockSpec((m, d), lambda b, j, tbl: (base + b * nblk + jnp.clip(j + off, 0, nblk - 1), col))

    return pl.pallas_call(
        _na_kernel,
        grid_spec=pltpu.PrefetchScalarGridSpec(
            num_scalar_prefetch=1,
            grid=(n_sample, nblk),
            in_specs=[
                blk(0, 0), blk(1, -1), blk(1, 0), blk(1, 1), blk(2, -1), blk(2, 0), blk(2, 1),
                pl.BlockSpec((1,) + k_ctx.shape[1:], lambda b, j, tbl: (b, 0, 0)),
                pl.BlockSpec((1,) + v_ctx.shape[1:], lambda b, j, tbl: (b, 0, 0)),
                pl.BlockSpec(bias.shape, lambda b, j, tbl: (0, 0, 0, 0)),
            ],
            out_specs=pl.BlockSpec((m, d), lambda b, j, tbl: (b * nblk + j, 0)),
        ),
        out_shape=jax.ShapeDtypeStruct((n_sample * sample_len, d), BF16),
        compiler_params=_cparams("arbitrary", "arbitrary"),
        name="neighbourhood_attention",
    )(table, qkv, qkv, qkv, qkv, qkv, qkv, qkv, k_ctx, v_ctx, bias)


MOE_ROW_TILE = 512
SC_CORES = 2
SC_SUBCORES = 16
SC_GATHER_CHUNK = 128

def _route_tokens(u, wt, carry):
    w1, w2, w3 = _split3(wt)
    lg = _dot_nt(w1, u) + _dot_nt(w2, u) + _dot_nt(w3, u)
    ne, nt = lg.shape
    e = lax.broadcasted_iota(jnp.int32, lg.shape, 0)
    m1 = jnp.max(lg, axis=0, keepdims=True)
    i1 = jnp.min(jnp.where(lg == m1, e, ne), axis=0, keepdims=True)
    sel1 = e == i1
    lg2 = jnp.where(sel1, -jnp.inf, lg)
    m2 = jnp.max(lg2, axis=0, keepdims=True)
    i2 = jnp.min(jnp.where(lg2 == m2, e, ne), axis=0, keepdims=True)
    sel2 = e == i2
    ex = jnp.exp(m2 - m1)
    wa = 1.0 / (1.0 + ex)
    wb = ex / (1.0 + ex)
    comb = jnp.where(sel1, wa, jnp.where(sel2, wb, 0.0))
    assign = jnp.logical_or(sel1, sel2)
    a = jnp.where(assign, 1.0, 0.0)
    row = lax.broadcasted_iota(jnp.int32, (nt, nt), 0)
    col = lax.broadcasted_iota(jnp.int32, (nt, nt), 1)
    tri = _onehot(row < col)
    rank = _dot(a.astype(BF16), tri) + carry
    pos = jnp.where(assign, rank, -1.0)
    return comb, pos, carry + jnp.sum(a, axis=1, keepdims=True)


def _moe_plan(totals, pos, comb, n_tok):
    ne = totals.shape[0]
    tm = MOE_ROW_TILE
    n_tiles = (2 * n_tok) // tm + ne
    tiles_per = (totals + tm - 1) // tm
    tile_end = jnp.cumsum(tiles_per)
    n_used = tile_end[-1]
    base_rows = (tile_end - tiles_per) * tm
    ti = jnp.arange(n_tiles, dtype=jnp.int32)
    tile_expert = jnp.minimum(jnp.sum((ti[:, None] >= tile_end[None, :]).astype(jnp.int32), axis=1), ne - 1)
    last_expert = tile_expert[jnp.maximum(n_used - 1, 0)]
    tile_expert = jnp.where(ti < n_used, tile_expert, last_expert)
    routed = pos >= 0.0
    row = base_rows[:, None] + pos.astype(jnp.int32)
    row_a = jnp.min(jnp.where(routed, row, n_tiles * tm), axis=0)
    row_b = jnp.max(jnp.where(routed, row, -1), axis=0)
    w_a = jnp.sum(jnp.where(jnp.logical_and(routed, row == row_a[None, :]), comb, 0.0), axis=0)
    w_b = jnp.sum(jnp.where(jnp.logical_and(routed, row == row_b[None, :]), comb, 0.0), axis=0)
    tile_rows = jnp.clip(totals[tile_expert] - (ti - (tile_end - tiles_per)[tile_expert]) * tm, 0, tm)
    tile_rows = jnp.where(ti < n_used, tile_rows, 0)
    return dict(n_rows=n_tiles * tm, tile_expert=tile_expert.astype(jnp.int32),
                n_used=n_used.reshape(1).astype(jnp.int32), tile_rows=tile_rows.astype(jnp.int32),
                token_rows=jnp.concatenate([row_a, row_b]).astype(jnp.int32),
                token_weights=jnp.stack([w_a, w_b], axis=1))


def _pack_bf16_pairs(x):
    half = x.shape[1] // 2
    bits = pltpu.bitcast(x.astype(BF16).astype(F32), jnp.int32)
    return (bits[:, :half] & jnp.int32(-65536)) | lax.shift_right_logical(bits[:, half:], jnp.int32(16))


def _unpack_bf16_pairs(p):
    hi = pltpu.bitcast(p & jnp.int32(-65536), F32)
    lo = pltpu.bitcast(lax.shift_left(p, jnp.int32(16)), F32)
    return jnp.concatenate([hi, lo], axis=1)


def sc_gather_rows(table, idx):
    _, w = table.shape
    b = idx.shape[0]
    workers = SC_CORES * SC_SUBCORES
    ch = SC_GATHER_CHUNK
    n_chunks = b // (workers * ch)
    assert n_chunks * workers * ch == b
    mesh = plsc.VectorSubcoreMesh(core_axis_name="c", subcore_axis_name="s", num_cores=SC_CORES,
                                  num_subcores=SC_SUBCORES)

    @functools.partial(
        pl.kernel, mesh=mesh, out_type=jax.ShapeDtypeStruct((b, w), table.dtype),
        scratch_types=[pltpu.VMEM((ch,), jnp.int32), pltpu.VMEM((ch, w), table.dtype), pltpu.SemaphoreType.DMA])
    def gather(table_hbm, idx_hbm, out_hbm, idx_v, rows_v, sem):
        worker = lax.axis_index("s") * SC_CORES + lax.axis_index("c")

        @pl.loop(0, n_chunks)
        def _(c):
            base = (worker * n_chunks + c) * ch
            pltpu.sync_copy(idx_hbm.at[pl.ds(base, ch)], idx_v)
            pltpu.async_copy(table_hbm.at[idx_v], rows_v, sem).wait()
            pltpu.sync_copy(rows_v, out_hbm.at[pl.ds(base, ch)])

    return gather(table, idx)


def sc_scatter_rows(rows, idx, n_out):
    t, w = rows.shape
    copies = idx.shape[0] // t
    workers = SC_CORES * SC_SUBCORES
    ch = SC_GATHER_CHUNK
    n_chunks = t // (workers * ch)
    assert n_chunks * workers * ch == t and copies * t == idx.shape[0]
    mesh = plsc.VectorSubcoreMesh(core_axis_name="c", subcore_axis_name="s", num_cores=SC_CORES,
                                  num_subcores=SC_SUBCORES)

    @functools.partial(
        pl.kernel, mesh=mesh, out_type=jax.ShapeDtypeStruct((n_out, w), rows.dtype),
        scratch_types=[pltpu.VMEM((ch,), jnp.int32), pltpu.VMEM((ch, w), rows.dtype), pltpu.SemaphoreType.DMA])
    def scatter(rows_hbm, idx_hbm, out_hbm, idx_v, rows_v, sem):
        worker = lax.axis_index("s") * SC_CORES + lax.axis_index("c")

        @pl.loop(0, n_chunks)
        def _(c):
            base = (worker * n_chunks + c) * ch
            pltpu.sync_copy(rows_hbm.at[pl.ds(base, ch)], rows_v)
            for k in range(copies):
                pltpu.sync_copy(idx_hbm.at[pl.ds(k * t + base, ch)], idx_v)
                pltpu.async_copy(rows_v, out_hbm.at[idx_v], sem).wait()

    return scatter(rows, idx)


def _gffn_kernel(te_ref, nu_ref, nr_ref, x_ref, wi_ref, wo_ref, y_ref):
    i = pl.program_id(0)
    used = i < nu_ref[0]

    @pl.when(used)
    def _():
        packed = x_ref[...]
        row = lax.broadcasted_iota(jnp.int32, packed.shape, 0)
        x = _unpack_bf16_pairs(jnp.where(row < nr_ref[i], packed, 0)).astype(BF16)
        y = _swiglu_tile(x, lambda c: wi_ref[0, :, c], lambda r: wo_ref[0, r, :], wo_ref.shape[1])
        y_ref[...] = _pack_bf16_pairs(y)

    @pl.when(jnp.logical_not(used))
    def _():
        y_ref[...] = jnp.zeros_like(y_ref)


def moe_grouped_ffn(xs, w_in, w_out, plan):
    nr, half = xs.shape
    ne, dff, d = w_out.shape
    tm = MOE_ROW_TILE
    return pl.pallas_call(
        _gffn_kernel,
        grid_spec=pltpu.PrefetchScalarGridSpec(
            num_scalar_prefetch=3,
            grid=(nr // tm,),
            in_specs=[
                pl.BlockSpec((tm, half), lambda i, te, nu, tr: (jnp.minimum(i, nu[0] - 1), 0)),
                pl.BlockSpec((1, d, 2 * dff), lambda i, te, nu, tr: (te[i], 0, 0)),
                pl.BlockSpec((1, dff, d), lambda i, te, nu, tr: (te[i], 0, 0)),
            ],
            out_specs=pl.BlockSpec((tm, half), lambda i, te, nu, tr: (i, 0)),
        ),
        out_shape=jax.ShapeDtypeStruct((nr, half), jnp.int32),
        compiler_params=_cparams("arbitrary"),
        name="moe_grouped_ffn",
    )(plan["tile_expert"], plan["n_used"], plan["tile_rows"], xs, w_in, w_out)


def _combine_kernel(ya_ref, yb_ref, w_ref, x_ref, mod_ref, lng_ref, lnb_ref, op_ref, os_ref, *, n_prompt_tiles):
    i = pl.program_id(0)
    w = w_ref[...]
    moe = w[:, 0:1] * _unpack_bf16_pairs(ya_ref[...]) + w[:, 1:2] * _unpack_bf16_pairs(yb_ref[...])
    z = DEEPNORM_ALPHA * x_ref[...] + mod_ref[0, 5:6, :] * moe
    xn = _layer_norm(z, lng_ref[...], lnb_ref[...])

    @pl.when(i < n_prompt_tiles)
    def _():
        op_ref[...] = xn

    @pl.when(i >= n_prompt_tiles)
    def _():
        os_ref[...] = xn


def moe_combine(y_tok, weights, x, mod, ln_g, ln_b, n_prompt_tok, sample_len):
    t, d = x.shape
    tt = TOKEN_TILE
    nt = t // tt
    npt = n_prompt_tok // tt
    seg = functools.partial(_seg_of_tile, tile=tt, n_prompt_tok=n_prompt_tok, sample_len=sample_len)
    return pl.pallas_call(
        functools.partial(_combine_kernel, n_prompt_tiles=npt),
        grid=(nt,),
        in_specs=[
            pl.BlockSpec((tt, d // 2), lambda i: (i, 0)),
            pl.BlockSpec((tt, d // 2), lambda i: (nt + i, 0)),
            pl.BlockSpec((tt, 2), lambda i: (i, 0)),
            pl.BlockSpec((tt, d), lambda i: (i, 0)),
            pl.BlockSpec((1, 6, d), lambda i: (seg(i), 0, 0)),
            pl.BlockSpec((1, d), lambda i: (0, 0)),
            pl.BlockSpec((1, d), lambda i: (0, 0)),
        ],
        out_specs=[
            pl.BlockSpec((tt, d), lambda i: (jnp.minimum(i, npt - 1), 0)),
            pl.BlockSpec((tt, d), lambda i: (jnp.maximum(i - npt, 0), 0)),
        ],
        out_shape=[jax.ShapeDtypeStruct((n_prompt_tok, d), F32), jax.ShapeDtypeStruct((t - n_prompt_tok, d), F32)],
        compiler_params=_cparams("arbitrary"),
        name="moe_combine",
    )(y_tok, y_tok, weights, x, mod, ln_g.reshape(1, d), ln_b.reshape(1, d))


def _layer1(x, u, mod1, cache_k, cache_v, ln_g, ln_b, w_qkv, rpb, w_out, w_router, moe_w_in, moe_w_out,
            n_prompt, prompt_len, n_sample, sample_len):
    n_prompt_tok = n_prompt * prompt_len
    d = D_MODEL
    w_qkv = jnp.concatenate([w_qkv[:, :d] * ATTN_Q_SCALE, w_qkv[:, d:]], axis=1).astype(BF16)
    qkv, k_p, v_p = na_qkv(u, w_qkv, n_prompt_tok)
    attn_p = context_attention(qkv, n_prompt, prompt_len)
    k_ctx = cache_k.reshape(n_sample, -1, d).astype(BF16)
    v_ctx = cache_v.reshape(n_sample, -1, d).astype(BF16)
    attn_s = neighbourhood_attention(qkv, k_ctx, v_ctx, rpb, n_prompt_tok, n_sample, sample_len)
    moe_w_in, moe_w_out = moe_w_in.astype(BF16), moe_w_out.astype(BF16)
    x1, u1_packed, comb, pos, totals = mixer_outproj_router(
        attn_p, attn_s, x, mod1, w_out.astype(BF16), ln_g[0], ln_b[0], w_router, sample_len,
        run_after=(moe_w_in, moe_w_out))
    plan = _moe_plan(totals[:, 0].astype(jnp.int32), pos, comb, x.shape[0])
    xs = sc_scatter_rows(u1_packed, plan["token_rows"], plan["n_rows"])
    y = moe_grouped_ffn(xs, moe_w_in, moe_w_out, plan)
    y_tok = sc_gather_rows(y, plan["token_rows"])
    y_p, y_s = moe_combine(y_tok, plan["token_weights"], x1, mod1, ln_g[1], ln_b[1], n_prompt_tok, sample_len)
    return y_p, y_s, k_p, v_p


def kernel(x_prompt, x_sample, state_mlstm_C, state_mlstm_n, state_mlstm_m, cache_na_k, cache_na_v, c, c_ctx,
           ada_w, ada_b, ln_g, ln_b, mlstm_w_in, mlstm_b_gates, mlstm_norm_g, mlstm_w_out, na_w_qkv, na_rpb,
           na_w_out, ffn_w_in, ffn_w_out, moe_w_router, moe_w_in, moe_w_out):
    n_prompt, prompt_len, d = x_prompt.shape
    n_sample, sample_len, _ = x_sample.shape
    assert d == D_MODEL and prompt_len == MLSTM_CHUNK and sample_len % MLSTM_CHUNK == 0
    assert ada_w.shape[0] == DEPTH == 2
    mod = _modulation_tables(c, c_ctx, ada_w, ada_b)
    x2, u2, new_c, new_n, new_m = _layer0(
        x_prompt.reshape(-1, d), x_sample.reshape(-1, d), mod[0], mod[1], state_mlstm_C[:, 0], state_mlstm_n[:, 0], state_mlstm_m[:, 0], ln_g[0], ln_b[0],
        mlstm_w_in[0], mlstm_b_gates[0], mlstm_norm_g[0], mlstm_w_out[0], ffn_w_in[0], ffn_w_out[0],
        n_prompt, prompt_len, n_sample, sample_len)
    y_p, y_s, k_p, v_p = _layer1(
        x2, u2, mod[1], cache_na_k[:, 0], cache_na_v[:, 0], ln_g[1], ln_b[1], na_w_qkv[0], na_rpb[0], na_w_out[0],
        moe_w_router[0], moe_w_in[0], moe_w_out[0], n_prompt, prompt_len, n_sample, sample_len)
    kv_shape = (n_prompt, 1, prompt_len, NA_HEADS, NA_DH)
    return (y_p.reshape(x_prompt.shape), y_s.reshape(x_sample.shape), new_c, new_n, new_m,
            k_p.reshape(kv_shape), v_p.reshape(kv_shape))
```

```python
import functools
import math

import jax
import jax.numpy as jnp
from jax import lax
from jax.experimental import pallas as pl
from jax.experimental.pallas import tpu as pltpu
from jax.experimental.pallas import tpu_sc as plsc

F32 = jnp.float32
BF16 = jnp.bfloat16

D_MODEL = 1024
DEPTH = 2
GRID_W = 64
M_HEADS = 4
M_DK = 128
M_DV = 256
M_HK = M_HEADS * M_DK
NA_HEADS = 16
NA_DH = 64
NA_ROWS = 8
NA_COLS = 16
DEEPNORM_ALPHA = (2.0 * DEPTH) ** 0.25
LN_EPS = 1e-5
NEG = -1e30

VMEM_LIMIT_BYTES = 56 * 1024 * 1024

MLSTM_CHUNK = 256
TOKEN_TILE = 512
FFN_TOKEN_TILE = 512
FFN_SUB = 256
NA_QROWS = 4
NA_HALO = 4


def _cparams(*sem):
    return pltpu.CompilerParams(dimension_semantics=sem, vmem_limit_bytes=VMEM_LIMIT_BYTES)


def _dot(a, b):
    return jnp.dot(a, b, preferred_element_type=F32)


def _dot_nt(a, b):
    return lax.dot_general(a, b, (((1,), (1,)), ((), ())), preferred_element_type=F32)


def _onehot(mask):
    return jnp.where(mask, 1.0, 0.0).astype(BF16)


def _split3(x):
    hi = x.astype(BF16)
    r = x - hi.astype(F32)
    mid = r.astype(BF16)
    lo = (r - mid.astype(F32)).astype(BF16)
    return hi, mid, lo


def _layer_norm(z, g, b):
    mu = jnp.mean(z, axis=-1, keepdims=True)
    zc = z - mu
    var = jnp.mean(zc * zc, axis=-1, keepdims=True)
    return zc * lax.rsqrt(var + LN_EPS) * g + b


def _log_sigmoid(x):
    return jnp.minimum(x, 0.0) - jnp.log(1.0 + jnp.exp(-jnp.abs(x)))


def _seg_of_tile(i, tile, n_prompt_tok, sample_len):
    tok = i * tile
    return jnp.where(tok < n_prompt_tok, 0, 1 + (tok - n_prompt_tok) // sample_len)


def _adaln_kernel(c_ref, w_ref, b_ref, o_ref):
    c = c_ref[...]
    a = (c * jax.nn.sigmoid(c)).astype(BF16)
    o_ref[0] = _dot(a, w_ref[0].astype(BF16)) + b_ref[0]


def adaln(cvec, ada_w, ada_b):
    depth, d, n = ada_w.shape
    tn = 1536
    return pl.pallas_call(
        _adaln_kernel,
        grid=(depth, n // tn),
        in_specs=[
            pl.BlockSpec((8, d), lambda l, j: (0, 0)),
            pl.BlockSpec((1, d, tn), lambda l, j: (l, 0, j)),
            pl.BlockSpec((1, 1, tn), lambda l, j: (l, 0, j)),
        ],
        out_specs=pl.BlockSpec((1, 8, tn), lambda l, j: (l, 0, j)),
        out_shape=jax.ShapeDtypeStruct((depth, 8, n), F32),
        compiler_params=_cparams("arbitrary", "arbitrary"),
        name="adaln",
    )(cvec, ada_w, ada_b.reshape(depth, 1, n))


def _inproj_kernel(xp_ref, xs_ref, mod_ref, w_ref, wg_ref, wgt_ref, bg_ref, bgt_ref,
                   qkv_ref, o_ref, g_ref, gt_ref, *, n_prompt_tiles):
    x = jnp.where(pl.program_id(0) < n_prompt_tiles, xp_ref[...], xs_ref[...])
    u = (x * (1.0 + mod_ref[0, 1:2, :]) + mod_ref[0, 0:1, :]).astype(BF16)
    nqkv = qkv_ref.shape[1]
    half = nqkv // 2
    g = _dot(u, wg_ref[...]) + bg_ref[...]
    gt = _dot_nt(wgt_ref[...], u) + bgt_ref[...]
    p0 = _dot(u, w_ref[:, :half])
    col = lax.broadcasted_iota(jnp.int32, g.shape, 1)
    g_ref[...] = jnp.where(((col >> 2) & 1) == 1, _log_sigmoid(g), g)
    p1 = _dot(u, w_ref[:, half:nqkv])
    qkv_ref[:, :half] = p0.astype(BF16)
    row = lax.broadcasted_iota(jnp.int32, gt.shape, 0)
    gt_ref[...] = jnp.where(((row >> 2) & 1) == 1, _log_sigmoid(gt), gt)
    p2 = _dot(u, w_ref[:, nqkv:])
    qkv_ref[:, half:] = p1.astype(BF16)
    o_ref[...] = p2


def _split_token_specs(tile, d, n_prompt_tiles):
    return [pl.BlockSpec((tile, d), lambda i: (jnp.minimum(i, n_prompt_tiles - 1), 0)),
            pl.BlockSpec((tile, d), lambda i: (jnp.maximum(i - n_prompt_tiles, 0), 0))]


def mlstm_inproj(xp, xs, mod, w_main, w_gate, b_gate, sample_len):
    n_prompt_tok, d = xp.shape
    t = n_prompt_tok + xs.shape[0]
    n_main = w_main.shape[1]
    n_qkv = 2 * M_HK + D_MODEL
    ng = w_gate.shape[1]
    tt = TOKEN_TILE
    npt = n_prompt_tok // tt
    seg = functools.partial(_seg_of_tile, tile=tt, n_prompt_tok=n_prompt_tok, sample_len=sample_len)
    return pl.pallas_call(
        functools.partial(_inproj_kernel, n_prompt_tiles=npt),
        grid=(t // tt,),
        in_specs=_split_token_specs(tt, d, npt) + [
            pl.BlockSpec((1, 6, d), lambda i: (seg(i), 0, 0)),
            pl.BlockSpec((d, n_main), lambda i: (0, 0)),
            pl.BlockSpec((d, ng), lambda i: (0, 0)),
            pl.BlockSpec((ng, d), lambda i: (0, 0)),
            pl.BlockSpec((1, ng), lambda i: (0, 0)),
            pl.BlockSpec((ng, 1), lambda i: (0, 0)),
        ],
        out_specs=[
            pl.BlockSpec((tt, n_qkv), lambda i: (i, 0)),
            pl.BlockSpec((tt, n_main - n_qkv), lambda i: (i, 0)),
            pl.BlockSpec((tt, ng), lambda i: (i, 0)),
            pl.BlockSpec((ng, tt), lambda i: (0, i)),
        ],
        out_shape=[
            jax.ShapeDtypeStruct((t, n_qkv), BF16),
            jax.ShapeDtypeStruct((t, n_main - n_qkv), F32),
            jax.ShapeDtypeStruct((t, ng), F32),
            jax.ShapeDtypeStruct((ng, t), F32),
        ],
        compiler_params=_cparams("arbitrary"),
        name="mlstm_inproj",
    )(xp, xs, mod, w_main, w_gate, w_gate.T, b_gate.reshape(1, ng), b_gate.reshape(ng, 1))


def _mlstm_gate_sums(g, gt, backward):
    L = g.shape[0]
    row = lax.broadcasted_iota(jnp.int32, (L, L), 0)
    col = lax.broadcasted_iota(jnp.int32, (L, L), 1)
    lower = col <= row
    upper = col >= row
    tri_col = _onehot(upper if backward else lower)
    tri_row = _onehot(lower if backward else upper)
    ghi, gmid, glo = _split3(g)
    b_cols = _dot(tri_col, ghi) + _dot(tri_col, gmid) + _dot(tri_col, glo)
    thi, tmid, tlo = _split3(gt)
    b_rows = _dot(thi, tri_row) + _dot(tmid, tri_row) + _dot(tlo, tri_row)
    return b_cols, b_rows


def _mlstm_heads(items):
    scale = M_DK ** -0.5
    log_scale = math.log(scale)
    L = items[0][0].shape[0]
    row = lax.broadcasted_iota(jnp.int32, (L, L), 0)
    col = lax.broadcasted_iota(jnp.int32, (L, L), 1)
    masks = {False: col <= row, True: col >= row}
    n = range(len(items))
    qh, kh, vh, li_row, lf_row, b_row, b_col, backward, state = zip(*items)
    has_state = [st is not None for st in state]
    m_prev = [st[2] if st is not None else 0.0 for st in state]

    qk = [_dot_nt(qh[i], kh[i]) for i in n]
    qc = [_dot(qh[i], state[i][0].astype(BF16)) if has_state[i] else None for i in n]
    qn = [_dot_nt(qh[i], state[i][1].astype(BF16))[:, 0:1] if has_state[i] else None for i in n]
    total = [jnp.sum(lf_row[i], axis=1, keepdims=True) for i in n]
    d = [jnp.where(masks[backward[i]], b_col[i] + (li_row[i] - b_row[i] + log_scale), NEG) for i in n]
    inter = [b_col[i] + m_prev[i] if has_state[i] else b_col[i] for i in n]
    m_t = [jnp.maximum(inter[i], jnp.max(d[i], axis=1, keepdims=True) - log_scale) for i in n]
    p = [jnp.exp(d[i] - m_t[i]) for i in n]
    s = [qk[i] * p[i] for i in n]
    den = [jnp.sum(s[i], axis=1, keepdims=True) for i in n]
    num = [_dot(s[i].astype(BF16), vh[i]) for i in n]
    a = [jnp.exp(inter[i] - m_t[i]) * scale if has_state[i] else None for i in n]
    num = [a[i] * qc[i] + num[i] if has_state[i] else num[i] for i in n]
    den = [a[i] * qn[i] + den[i] if has_state[i] else den[i] for i in n]
    hh = [num[i] / jnp.maximum(jnp.abs(den[i]), jnp.exp(-m_t[i])) for i in n]

    g_row = [total[i] - b_row[i] + li_row[i] for i in n]
    m_new = [jnp.maximum(total[i] + m_prev[i], jnp.max(g_row[i], axis=1, keepdims=True)) for i in n]
    ws_row = [jnp.exp(g_row[i] - m_new[i]) for i in n]
    kt = [kh[i].astype(F32).T for i in n]
    c_new = [_dot((kt[i] * ws_row[i]).astype(BF16), vh[i]) for i in n]
    n_new = [_dot(jnp.broadcast_to(ws_row[i], (8, L)).astype(BF16), kh[i]) for i in n]
    a0 = [jnp.exp(total[i] + m_prev[i] - m_new[i]) if has_state[i] else None for i in n]
    c_new = [a0[i] * state[i][0] + c_new[i] if has_state[i] else c_new[i] for i in n]
    n_new = [a0[i] * state[i][1] + n_new[i] if has_state[i] else n_new[i] for i in n]
    return hh, c_new, n_new, m_new


def _head_norm_gate(ht, norm_g, ogate):
    mu = jnp.mean(ht, axis=-1, keepdims=True)
    hc = ht - mu
    var = jnp.mean(hc * hc, axis=-1, keepdims=True)
    return (hc * lax.rsqrt(var + LN_EPS) * norm_g * jax.nn.sigmoid(ogate)).astype(BF16)


def _head_operands(q_ref, k_ref, v_ref, h):
    return (q_ref[:, h * M_DK:(h + 1) * M_DK], k_ref[:, h * M_DK:(h + 1) * M_DK], v_ref[:, h * M_DV:(h + 1) * M_DV])


def _gate_operands(gt, b_rows, b_cols, direction, h):
    ci = direction * 8 + h
    cf = direction * 8 + 4 + h
    return gt[ci:ci + 1, :], gt[cf:cf + 1, :], b_rows[cf:cf + 1, :], b_cols[:, cf:cf + 1]


def _mlstm_prompt_kernel(q_ref, k_ref, v_ref, g_ref, gt_ref, o_ref, ng_ref, a_ref, cf_ref, nf_ref, mf_ref):
    g, gt = g_ref[...], gt_ref[...]
    sums = [_mlstm_gate_sums(g, gt, direction == 1) for direction in (0, 1)]
    keys = [(direction, h) for direction in (0, 1) for h in range(M_HEADS)]
    items = [_head_operands(q_ref, k_ref, v_ref, h)
             + _gate_operands(gt, sums[direction][1], sums[direction][0], direction, h) + (direction == 1, None)
             for direction, h in keys]
    hh, c_new, n_new, m_new = _mlstm_heads(items)
    for i, (direction, h) in enumerate(keys):
        cf_ref[0, 0, direction, h] = c_new[i]
        nf_ref[0, 0, direction, h] = n_new[i]
        mf_ref[0, 0, direction, h] = jnp.broadcast_to(m_new[i], mf_ref.shape[-2:])
    for h in range(M_HEADS):
        sl = slice(h * M_DV, (h + 1) * M_DV)
        a_ref[:, sl] = _head_norm_gate(hh[h] + hh[M_HEADS + h], ng_ref[:, sl], o_ref[:, sl])


def _mlstm_sample_kernel(*refs, direction, n_units):
    backward = direction == 1
    refs = list(refs)
    unit_refs = [tuple(refs.pop(0) for _ in range(5)) for _ in range(n_units)]
    c0_ref, n0_ref, m0_ref = refs.pop(0), refs.pop(0), refs.pop(0)
    if backward:
        hprev_ref = refs.pop(0)
        o_refs = [refs.pop(0) for _ in range(n_units)]
        ng_ref = refs.pop(0)
    out_ref, c_scr, n_scr, m_scr = refs

    @pl.when(pl.program_id(0) == 0)
    def _():
        c_scr[...] = c0_ref[...]
        n_scr[...] = n0_ref[...]
        m_scr[...] = m0_ref[...]

    keys, items = [], []
    for u, (q_ref, k_ref, v_ref, g_ref, gt_ref) in enumerate(unit_refs):
        gt = gt_ref[...]
        b_cols, b_rows = _mlstm_gate_sums(g_ref[...], gt, backward)
        for h in range(M_HEADS):
            state = (c_scr[u, h], n_scr[u, h], m_scr[u, h][0:1, 0:1])
            keys.append((u, h))
            items.append(_head_operands(q_ref, k_ref, v_ref, h) + _gate_operands(gt, b_rows, b_cols, direction, h)
                         + (backward, state))
    hh, c_new, n_new, m_new = _mlstm_heads(items)
    for i, (u, h) in enumerate(keys):
        c_scr[u, h] = c_new[i]
        n_scr[u, h] = n_new[i]
        m_scr[u, h] = jnp.broadcast_to(m_new[i], m_scr.shape[-2:])
        sl = slice(h * M_DV, (h + 1) * M_DV)
        if backward:
            out_ref[u, :, sl] = _head_norm_gate(hprev_ref[u, :, sl] + hh[i], ng_ref[:, sl], o_refs[u][:, sl])
        else:
            out_ref[u, :, sl] = hh[i]


def _mlstm_chunk_specs(block_of):
    L = MLSTM_CHUNK
    return [
        pl.BlockSpec((L, M_HK), lambda s: (block_of(s), 0)),
        pl.BlockSpec((L, M_HK), lambda s: (block_of(s), 1)),
        pl.BlockSpec((L, D_MODEL), lambda s: (block_of(s), 1)),
        pl.BlockSpec((L, 16), lambda s: (block_of(s), 0)),
        pl.BlockSpec((16, L), lambda s: (0, block_of(s))),
    ]


def mlstm_prompt(qkv, g, gt, ogate, norm_g, n_prompt):
    L = MLSTM_CHUNK
    state_blk = lambda *tail: pl.BlockSpec((1, 1, 2, M_HEADS) + tail, lambda s: (s, 0, 0, 0, 0, 0))
    return pl.pallas_call(
        _mlstm_prompt_kernel,
        grid=(n_prompt,),
        in_specs=_mlstm_chunk_specs(lambda s: s) + [
            pl.BlockSpec((L, D_MODEL), lambda s: (s, 0)),
            pl.BlockSpec((1, D_MODEL), lambda s: (0, 0)),
        ],
        out_specs=[pl.BlockSpec((L, D_MODEL), lambda s: (s, 0)),
                   state_blk(M_DK, M_DV), state_blk(8, M_DK), state_blk(8, 128)],
        out_shape=[
            jax.ShapeDtypeStruct((n_prompt * L, D_MODEL), BF16),
            jax.ShapeDtypeStruct((n_prompt, 1, 2, M_HEADS, M_DK, M_DV), F32),
            jax.ShapeDtypeStruct((n_prompt, 1, 2, M_HEADS, 8, M_DK), F32),
            jax.ShapeDtypeStruct((n_prompt, 1, 2, M_HEADS, 8, 128), F32),
        ],
        compiler_params=_cparams("arbitrary"),
        name="mlstm_prompt",
    )(qkv, qkv, qkv, g, gt, ogate, norm_g.reshape(1, D_MODEL))


def mlstm_sample(direction, qkv, g, gt, c0, n0, m0, first_block, chunks_per_sample,
                 hprev=None, ogate=None, norm_g=None):
    L = MLSTM_CHUNK
    backward = direction == 1
    cps = chunks_per_sample
    n_units = c0.shape[0]
    pos = (lambda s: cps - 1 - s) if backward else (lambda s: s)
    unit_block = [functools.partial(lambda s, u: first_block + u * cps + pos(s), u=u) for u in range(n_units)]
    whole = lambda a: pl.BlockSpec(a.shape, lambda s: (0,) * a.ndim)
    in_specs, args = [], []
    for u in range(n_units):
        in_specs += _mlstm_chunk_specs(unit_block[u])
        args += [qkv, qkv, qkv, g, gt]
    in_specs += [whole(c0), whole(n0), whole(m0)]
    args += [c0, n0, m0]
    if backward:
        in_specs += [pl.BlockSpec((n_units, L, D_MODEL), lambda s: (0, pos(s), 0))]
        in_specs += [pl.BlockSpec((L, D_MODEL), functools.partial(lambda s, u: (unit_block[u](s), 0), u=u))
                     for u in range(n_units)]
        in_specs += [pl.BlockSpec((1, D_MODEL), lambda s: (0, 0))]
        args += [hprev] + [ogate] * n_units + [norm_g.reshape(1, D_MODEL)]
    return pl.pallas_call(
        functools.partial(_mlstm_sample_kernel, direction=direction, n_units=n_units),
        grid=(cps,),
        in_specs=in_specs,
        out_specs=pl.BlockSpec((n_units, L, D_MODEL), lambda s: (0, pos(s), 0)),
        out_shape=jax.ShapeDtypeStruct((n_units, cps * L, D_MODEL), BF16 if backward else F32),
        scratch_shapes=[pltpu.VMEM(c0.shape, F32), pltpu.VMEM(n0.shape, F32), pltpu.VMEM(m0.shape, F32)],
        compiler_params=_cparams("arbitrary"),
        name="mlstm_sample_bwd" if backward else "mlstm_sample_fwd",
    )(*args)


def _outproj_router_kernel(ap_ref, as_ref, x_ref, mod_ref, w_ref, lng_ref, lnb_ref, wr_ref, *refs,
                           n_prompt_tiles):
    xo_ref, up_ref, comb_ref, pos_ref, total_ref, carry_scr = refs[-6:]
    i = pl.program_id(0)

    @pl.when(i == 0)
    def _():
        carry_scr[...] = jnp.zeros_like(carry_scr)

    a = jnp.where(i < n_prompt_tiles, ap_ref[...], as_ref[...])
    z = DEEPNORM_ALPHA * x_ref[...] + mod_ref[0, 2:3, :] * _dot(a, w_ref[...])
    xn = _layer_norm(z, lng_ref[...], lnb_ref[...])
    xo_ref[...] = xn
    u = xn * (1.0 + mod_ref[0, 4:5, :]) + mod_ref[0, 3:4, :]
    up_ref[...] = _pack_bf16_pairs(u)
    comb, pos, carry_new = _route_tokens(u.astype(BF16), wr_ref[...], carry_scr[:, 0:1])
    comb_ref[...] = comb
    pos_ref[...] = pos
    carry_scr[...] = jnp.broadcast_to(carry_new, carry_scr.shape)
    total_ref[...] = jnp.broadcast_to(carry_new, carry_scr.shape)


def mixer_outproj_router(a_p, a_s, x, mod, w, ln_g, ln_b, w_router, sample_len, run_after=()):
    t, d = x.shape
    ne = w_router.shape[1]
    tt = TOKEN_TILE
    npt = a_p.shape[0] // tt
    seg = functools.partial(_seg_of_tile, tile=tt, n_prompt_tok=a_p.shape[0], sample_len=sample_len)
    return pl.pallas_call(
        functools.partial(_outproj_router_kernel, n_prompt_tiles=npt),
        grid=(t // tt,),
        in_specs=_split_token_specs(tt, d, npt) + [
            pl.BlockSpec((tt, d), lambda i: (i, 0)),
            pl.BlockSpec((1, 6, d), lambda i: (seg(i), 0, 0)),
            pl.BlockSpec((d, d), lambda i: (0, 0)),
            pl.BlockSpec((1, d), lambda i: (0, 0)),
            pl.BlockSpec((1, d), lambda i: (0, 0)),
            pl.BlockSpec((ne, d), lambda i: (0, 0)),
        ] + [pl.BlockSpec(memory_space=pl.ANY)] * len(run_after),
        out_specs=[
            pl.BlockSpec((tt, d), lambda i: (i, 0)),
            pl.BlockSpec((tt, d // 2), lambda i: (i, 0)),
            pl.BlockSpec((ne, tt), lambda i: (0, i)),
            pl.BlockSpec((ne, tt), lambda i: (0, i)),
            pl.BlockSpec((ne, 128), lambda i: (0, 0)),
        ],
        out_shape=[
            jax.ShapeDtypeStruct((t, d), F32),
            jax.ShapeDtypeStruct((t, d // 2), jnp.int32),
            jax.ShapeDtypeStruct((ne, t), F32),
            jax.ShapeDtypeStruct((ne, t), F32),
            jax.ShapeDtypeStruct((ne, 128), F32),
        ],
        scratch_shapes=[pltpu.VMEM((ne, 128), F32)],
        compiler_params=_cparams("arbitrary"),
        name="mixer_outproj_router",
    )(a_p, a_s, x, mod, w, ln_g.reshape(1, d), ln_b.reshape(1, d), w_router.T, *run_after)


def _swiglu_tile(x, wi, wo, dff):
    sub = FFN_SUB
    proj = lambda j: (_dot(x, wi(slice(j * sub, (j + 1) * sub))), _dot(x, wi(slice(dff + j * sub, dff + (j + 1) * sub))))
    acts, cur = [], proj(0)
    for j in range(dff // sub):
        nxt = proj(j + 1) if (j + 1) * sub < dff else None
        gp, up = cur
        acts.append((gp * jax.nn.sigmoid(gp) * up).astype(BF16))
        cur = nxt
    return _dot(jnp.concatenate(acts, axis=1), wo(slice(0, dff)))


def _mixer_ffn_kernel(ap_ref, as_ref, xp_ref, xs_ref, mod_ref, modn_ref, wmix_ref, wi_ref, wo_ref, ln_ref,
                      xo_ref, uo_ref, *, n_prompt_tiles):
    is_prompt = pl.program_id(0) < n_prompt_tiles
    a = jnp.where(is_prompt, ap_ref[...], as_ref[...])
    x = jnp.where(is_prompt, xp_ref[...], xs_ref[...])
    z = DEEPNORM_ALPHA * x + mod_ref[0, 2:3, :] * _dot(a, wmix_ref[...])
    x1 = _layer_norm(z, ln_ref[0:1, :], ln_ref[1:2, :])
    u1 = (x1 * (1.0 + mod_ref[0, 4:5, :]) + mod_ref[0, 3:4, :]).astype(BF16)
    f = _swiglu_tile(u1, lambda c: wi_ref[:, c], lambda r: wo_ref[r, :], wo_ref.shape[0])
    x2 = _layer_norm(DEEPNORM_ALPHA * x1 + mod_ref[0, 5:6, :] * f, ln_ref[2:3, :], ln_ref[3:4, :])
    xo_ref[...] = x2
    uo_ref[...] = (x2 * (1.0 + modn_ref[0, 1:2, :]) + modn_ref[0, 0:1, :]).astype(BF16)


def _resident(shape):
    return pl.BlockSpec(shape, lambda i: (0,) * len(shape), pipeline_mode=pl.Buffered(1))


def mixer_ffn(a_p, a_s, xp, xs, mod, mod_next, w_mix, w_in, w_out, ln_g, ln_b, sample_len):
    n_prompt_tok, d = xp.shape
    t = n_prompt_tok + xs.shape[0]
    tm = FFN_TOKEN_TILE
    npt = n_prompt_tok // tm
    seg = functools.partial(_seg_of_tile, tile=tm, n_prompt_tok=n_prompt_tok, sample_len=sample_len)
    ln = jnp.stack([ln_g[0], ln_b[0], ln_g[1], ln_b[1]])
    return pl.pallas_call(
        functools.partial(_mixer_ffn_kernel, n_prompt_tiles=npt),
        grid=(t // tm,),
        in_specs=_split_token_specs(tm, d, npt) + _split_token_specs(tm, d, npt) + [
            pl.BlockSpec((1, 6, d), lambda i: (seg(i), 0, 0)),
            pl.BlockSpec((1, 6, d), lambda i: (seg(i), 0, 0)),
            _resident(w_mix.shape), _resident(w_in.shape), _resident(w_out.shape), _resident(ln.shape),
        ],
        out_specs=[pl.BlockSpec((tm, d), lambda i: (i, 0)), pl.BlockSpec((tm, d), lambda i: (i, 0))],
        out_shape=[jax.ShapeDtypeStruct((t, d), F32), jax.ShapeDtypeStruct((t, d), BF16)],
        compiler_params=_cparams("arbitrary"),
        name="mixer_ffn",
    )(a_p, a_s, xp, xs, mod, mod_next, w_mix, w_in, w_out, ln)


def _modulation_tables(c, c_ctx, ada_w, ada_b):
    n_s = c.shape[0]
    cvec = jnp.concatenate([c_ctx[None, :], c, jnp.zeros((8 - 1 - n_s, c.shape[1]), F32)], axis=0)
    mod = adaln(cvec, ada_w, ada_b)
    return mod[:, :1 + n_s, :].reshape(ada_w.shape[0], 1 + n_s, 6, c.shape[1])


def _layer0(xp, xs, mod0, mod1, state_c, state_n, state_m, ln_g, ln_b, w_in, b_gates, norm_g, w_out,
            ffn_w_in, ffn_w_out, n_prompt, prompt_len, n_sample, sample_len):
    n_prompt_tok = n_prompt * prompt_len
    n_main = 2 * M_HK + 2 * D_MODEL
    qkv, ogate, g, gt = mlstm_inproj(xp, xs, mod0, w_in[:, :n_main].astype(BF16), w_in[:, n_main:].astype(BF16),
                                     b_gates.reshape(-1), sample_len)
    npc = n_prompt_tok // MLSTM_CHUNK
    cps = sample_len // MLSTM_CHUNK
    rep = lambda a: jnp.broadcast_to(a[..., None, :], a.shape[:-1] + (8, a.shape[-1]))
    n0 = rep(state_n)
    m0 = jnp.broadcast_to(state_m[..., None, None], state_m.shape + (8, 128))
    a_p, new_c, nf, mf = mlstm_prompt(qkv, g, gt, ogate, norm_g, n_prompt)
    hf = mlstm_sample(0, qkv, g, gt, state_c[:, 0], n0[:, 0], m0[:, 0], npc, cps)
    a_s = mlstm_sample(1, qkv, g, gt, state_c[:, 1], n0[:, 1], m0[:, 1], npc, cps,
                       hprev=hf, ogate=ogate, norm_g=norm_g)
    x2, u2 = mixer_ffn(a_p, a_s.reshape(-1, D_MODEL), xp, xs, mod0, mod1, w_out.astype(BF16),
                       ffn_w_in.astype(BF16), ffn_w_out.astype(BF16), ln_g, ln_b, sample_len)
    return x2, u2, new_c, nf[..., 0, :], mf[..., 0, 0]


def _qkv_kernel(u_ref, w_ref, qkv_ref, *kv_refs):
    tt, d = u_ref.shape
    u = u_ref[...]

    def head_major(out_ref, p):
        for h in range(NA_HEADS):
            out_ref[pl.ds(h, tt, stride=NA_HEADS), :] = p[:, h * NA_DH:(h + 1) * NA_DH]

    pq = _dot(u, w_ref[:, :d])
    pk = _dot(u, w_ref[:, d:2 * d])
    qkv_ref[:, :d] = pq.astype(BF16)
    pv = _dot(u, w_ref[:, 2 * d:])
    qkv_ref[:, d:2 * d] = pk.astype(BF16)
    if kv_refs:
        head_major(kv_refs[0], pk)
    qkv_ref[:, 2 * d:] = pv.astype(BF16)
    if kv_refs:
        head_major(kv_refs[1], pv)


def na_qkv(u, w, first_tok, n_tok, with_kv):
    d = u.shape[1]
    tt = TOKEN_TILE
    first = first_tok // tt
    out_specs = [pl.BlockSpec((tt, 3 * d), lambda i: (i, 0))]
    out_shape = [jax.ShapeDtypeStruct((n_tok, 3 * d), BF16)]
    if with_kv:
        out_specs += [pl.BlockSpec((tt * NA_HEADS, NA_DH), lambda i: (i, 0))] * 2
        out_shape += [jax.ShapeDtypeStruct((n_tok * NA_HEADS, NA_DH), F32)] * 2
    return pl.pallas_call(
        _qkv_kernel,
        grid=(n_tok // tt,),
        in_specs=[pl.BlockSpec((tt, d), lambda i: (first + i, 0)), pl.BlockSpec((d, 3 * d), lambda i: (0, 0))],
        out_specs=out_specs,
        out_shape=out_shape,
        compiler_params=_cparams("arbitrary"),
        name="na_qkv_prompt" if with_kv else "na_qkv_latent",
    )(u, w)


ATTN_LOG2E = math.log2(math.e)
ATTN_Q_SCALE = NA_DH ** -0.5 * ATTN_LOG2E


def _head_pair_masks():
    lane = lax.broadcasted_iota(jnp.int32, (1, 2 * NA_DH), 1)
    return (_onehot(lane < NA_DH), _onehot(lane >= NA_DH))


def _softmax_pv(score_blocks, value_blocks):
    mx = None
    for s in score_blocks:
        bm = jnp.max(s, axis=1, keepdims=True)
        mx = bm if mx is None else jnp.maximum(mx, bm)
    acc, denom = None, None
    for s, v in zip(score_blocks, value_blocks):
        p = jnp.exp2(s - mx)
        ps = jnp.sum(p, axis=1, keepdims=True)
        pv = _dot(p.astype(BF16), v)
        acc = pv if acc is None else acc + pv
        denom = ps if denom is None else denom + ps
    return acc / denom


def _ctx_attn_kernel(q_ref, k_ref, v_ref, o_ref):
    masks = _head_pair_masks()
    for hp in range(NA_HEADS // 2):
        sl = slice(hp * 2 * NA_DH, (hp + 1) * 2 * NA_DH)
        q2, k2, v2 = q_ref[:, sl], k_ref[:, sl], v_ref[:, sl]
        o2 = None
        for msk in masks:
            s = _dot_nt(q2 * msk, k2)
            o = _softmax_pv([s], [v2 * msk])
            o2 = o if o2 is None else o2 + o
        o_ref[:, sl] = o2.astype(o_ref.dtype)


def context_attention(qkv, n_prompt, prompt_len):
    t = n_prompt * prompt_len
    d = D_MODEL
    return pl.pallas_call(
        _ctx_attn_kernel,
        grid=(n_prompt,),
        in_specs=[
            pl.BlockSpec((prompt_len, d), lambda b: (b, 0)),
            pl.BlockSpec((prompt_len, d), lambda b: (b, 1)),
            pl.BlockSpec((prompt_len, d), lambda b: (b, 2)),
        ],
        out_specs=pl.BlockSpec((prompt_len, d), lambda b: (b, 0)),
        out_shape=jax.ShapeDtypeStruct((t, d), BF16),
        compiler_params=_cparams("arbitrary"),
        name="context_attention",
    )(qkv, qkv, qkv)


def _na_kernel(tile_ref, q_ref, kp_ref, kc_ref, kn_ref, vp_ref, vc_ref, vn_ref, kctx_ref, vctx_ref, bias_ref,
               o_ref):
    j = pl.program_id(1)
    nblk = pl.num_programs(1)
    m = q_ref.shape[0]
    halo = NA_HALO * GRID_W
    n_pairs = (NA_QROWS + 2 * NA_HALO) // 2
    variant = jnp.where(j == 0, 0, jnp.where(j == nblk - 1, 2, 1))
    kinds = [[tile_ref[(variant * NA_QROWS + rq) * n_pairs + kp] for kp in range(n_pairs)]
             for rq in range(NA_QROWS)]

    def bias_of(head):
        return jnp.concatenate(
            [jnp.concatenate([bias_ref[kinds[rq][kp], head] for kp in range(n_pairs)], axis=1)
             for rq in range(NA_QROWS)], axis=0)

    masks = _head_pair_masks()
    for hp in range(NA_HEADS // 2):
        sl = slice(hp * 2 * NA_DH, (hp + 1) * 2 * NA_DH)
        q2 = q_ref[:, sl]
        kloc = jnp.concatenate([kp_ref[m - halo:, sl], kc_ref[:, sl], kn_ref[:halo, sl]], axis=0)
        vloc = jnp.concatenate([vp_ref[m - halo:, sl], vc_ref[:, sl], vn_ref[:halo, sl]], axis=0)
        kctx = kctx_ref[0][:, sl]
        vctx = vctx_ref[0][:, sl]
        o2 = None
        for half, msk in enumerate(masks):
            qm = q2 * msk
            s_loc = _dot_nt(qm, kloc) + bias_of(2 * hp + half)
            s_ctx = _dot_nt(qm, kctx)
            o = _softmax_pv([s_loc, s_ctx], [vloc * msk, vctx * msk])
            o2 = o if o2 is None else o2 + o
        o_ref[:, sl] = o2.astype(o_ref.dtype)


def _na_bias_kernel(rpb_ref, o_ref, *, kinds, n_heads):
    nc, npos = rpb_ref.shape[1], o_ref.shape[2]
    pair_shift = (2 * GRID_W).bit_length() - 1
    c = lax.broadcasted_iota(jnp.int32, (nc, npos), 0)
    pos = lax.broadcasted_iota(jnp.int32, (nc, npos), 1)
    qc = pos >> pair_shift
    kc = pos & (GRID_W - 1)
    c_start = jnp.clip(qc - NA_COLS // 2, 0, GRID_W - NA_COLS)
    col_in = jnp.logical_and(kc >= c_start, kc < c_start + NA_COLS)
    dc = jnp.clip(kc - qc + NA_COLS - 1, 0, 2 * NA_COLS - 2)
    hit = jnp.logical_and(c == dc, col_in)
    right = (pos & GRID_W) != 0
    r1, r2, r3 = _split3(rpb_ref[...] * ATTN_LOG2E)

    def placed(side):
        sel = _onehot(jnp.logical_and(hit, right == side))
        return _dot(r1, sel) + _dot(r2, sel) + _dot(r3, sel)

    left, rght = placed(False), placed(True)
    rows = lambda v, dr: v[dr * n_heads:(dr + 1) * n_heads]
    for k, (dl, dr) in enumerate(kinds):
        shown = col_in[0:1, :]
        if dl is None or dr is None:
            side_ok = jnp.zeros_like(shown) if dl is None and dr is None else (right[0:1, :] == (dl is None))
            shown = jnp.logical_and(shown, side_ok)
        parts = ([rows(left, dl)] if dl is not None else []) + ([rows(rght, dr)] if dr is not None else [])
        val = sum(parts) if parts else jnp.zeros((n_heads, npos), F32)
        o_ref[k] = jnp.where(shown, val, NEG)


def _na_bias_plan():
    ndr = 2 * NA_ROWS - 1
    assert NA_QROWS >= NA_ROWS // 2 and NA_HALO == NA_ROWS // 2 and (NA_QROWS + 2 * NA_HALO) % 2 == 0
    ok = lambda dr: dr if dr is not None and 0 <= dr < ndr else None
    kinds = [(ok(dr), ok(dr + 1)) for dr in range(-1, ndr)]
    first_key_row = (lambda rq: NA_HALO, lambda rq: rq, lambda rq: NA_QROWS + NA_HALO - NA_ROWS)
    table = []
    for first in first_key_row:
        for rq in range(NA_QROWS):
            for kp in range((NA_QROWS + 2 * NA_HALO) // 2):
                drs = []
                for kk in (2 * kp, 2 * kp + 1):
                    visible = first(rq) <= kk < first(rq) + NA_ROWS
                    drs.append(kk - rq + NA_ROWS - 1 - NA_HALO if visible else None)
                kind = (drs[0], drs[1])
                if kind not in kinds:
                    kinds.append(kind)
                table.append(kinds.index(kind))
    return kinds, table


def _na_bias_tiles(rpb):
    nh, ndr, ndc = rpb.shape
    nc = 32
    kinds, table = _na_bias_plan()
    nk = len(kinds)
    rows = jnp.pad(jnp.transpose(rpb, (1, 0, 2)), ((0, 0), (0, 0), (0, nc - ndc))).reshape(ndr * nh, nc)
    npos = GRID_W * 2 * GRID_W
    tiles = pl.pallas_call(
        functools.partial(_na_bias_kernel, kinds=tuple(kinds), n_heads=nh),
        out_shape=jax.ShapeDtypeStruct((nk, nh, npos), F32),
        compiler_params=pltpu.CompilerParams(vmem_limit_bytes=VMEM_LIMIT_BYTES),
        name="na_bias",
    )(rows)
    return tiles.reshape(nk, nh, GRID_W, 2 * GRID_W), jnp.asarray(table, jnp.int32)


def neighbourhood_attention(qkv, k_ctx, v_ctx, rpb, n_sample, sample_len):
    d = D_MODEL
    m = NA_QROWS * GRID_W
    nblk = sample_len // m
    assert nblk >= 3
    bias, table = _na_bias_tiles(rpb)

    def blk(col, off):
        return pl.BlockSpec((m, d), lambda b, j, tbl: (b * nblk + jnp.clip(j + off, 0, nblk - 1), col))

    return pl.pallas_call(
        _na_kernel,
        grid_spec=pltpu.PrefetchScalarGridSpec(
            num_scalar_prefetch=1,
            grid=(n_sample, nblk),
            in_specs=[
                blk(0, 0), blk(1, -1), blk(1, 0), blk(1, 1), blk(2, -1), blk(2, 0), blk(2, 1),
                pl.BlockSpec((1,) + k_ctx.shape[1:], lambda b, j, tbl: (b, 0, 0)),
                pl.BlockSpec((1,) + v_ctx.shape[1:], lambda b, j, tbl: (b, 0, 0)),
                pl.BlockSpec(bias.shape, lambda b, j, tbl: (0, 0, 0, 0)),
            ],
            out_specs=pl.BlockSpec((m, d), lambda b, j, tbl: (b * nblk + j, 0)),
        ),
        out_shape=jax.ShapeDtypeStruct((n_sample * sample_len, d), BF16),
        compiler_params=_cparams("arbitrary", "arbitrary"),
        name="neighbourhood_attention",
    )(table, qkv, qkv, qkv, qkv, qkv, qkv, qkv, k_ctx, v_ctx, bias)


MOE_ROW_TILE = 512
SC_CORES = 2
SC_SUBCORES = 16
SC_GATHER_CHUNK = 128

def _route_tokens(u, wt, carry):
    w1, w2, w3 = _split3(wt)
    lg = _dot_nt(w1, u) + _dot_nt(w2, u) + _dot_nt(w3, u)
    ne, nt = lg.shape
    e = lax.broadcasted_iota(jnp.int32, lg.shape, 0)
    m1 = jnp.max(lg, axis=0, keepdims=True)
    i1 = jnp.min(jnp.where(lg == m1, e, ne), axis=0, keepdims=True)
    sel1 = e == i1
    lg2 = jnp.where(sel1, -jnp.inf, lg)
    m2 = jnp.max(lg2, axis=0, keepdims=True)
    i2 = jnp.min(jnp.where(lg2 == m2, e, ne), axis=0, keepdims=True)
    sel2 = e == i2
    ex = jnp.exp(m2 - m1)
    wa = 1.0 / (1.0 + ex)
    wb = ex / (1.0 + ex)
    comb = jnp.where(sel1, wa, jnp.where(sel2, wb, 0.0))
    assign = jnp.logical_or(sel1, sel2)
    a = jnp.where(assign, 1.0, 0.0)
    row = lax.broadcasted_iota(jnp.int32, (nt, nt), 0)
    col = lax.broadcasted_iota(jnp.int32, (nt, nt), 1)
    tri = _onehot(row < col)
    rank = _dot(a.astype(BF16), tri) + carry
    pos = jnp.where(assign, rank, -1.0)
    return comb, pos, carry + jnp.sum(a, axis=1, keepdims=True)


def _moe_plan(totals, pos, comb, n_tok):
    ne = totals.shape[0]
    tm = MOE_ROW_TILE
    n_tiles = (2 * n_tok) // tm + ne
    tiles_per = (totals + tm - 1) // tm
    tile_end = jnp.cumsum(tiles_per)
    n_used = tile_end[-1]
    base_rows = (tile_end - tiles_per) * tm
    ti = jnp.arange(n_tiles, dtype=jnp.int32)
    tile_expert = jnp.minimum(jnp.sum((ti[:, None] >= tile_end[None, :]).astype(jnp.int32), axis=1), ne - 1)
    last_expert = tile_expert[jnp.maximum(n_used - 1, 0)]
    tile_expert = jnp.where(ti < n_used, tile_expert, last_expert)
    routed = pos >= 0.0
    row = base_rows[:, None] + pos.astype(jnp.int32)
    row_a = jnp.min(jnp.where(routed, row, n_tiles * tm), axis=0)
    row_b = jnp.max(jnp.where(routed, row, -1), axis=0)
    w_a = jnp.sum(jnp.where(jnp.logical_and(routed, row == row_a[None, :]), comb, 0.0), axis=0)
    w_b = jnp.sum(jnp.where(jnp.logical_and(routed, row == row_b[None, :]), comb, 0.0), axis=0)
    tile_rows = jnp.clip(totals[tile_expert] - (ti - (tile_end - tiles_per)[tile_expert]) * tm, 0, tm)
    tile_rows = jnp.where(ti < n_used, tile_rows, 0)
    return dict(n_rows=n_tiles * tm, tile_expert=tile_expert.astype(jnp.int32),
                n_used=n_used.reshape(1).astype(jnp.int32), tile_rows=tile_rows.astype(jnp.int32),
                token_rows=jnp.concatenate([row_a, row_b]).astype(jnp.int32),
                token_weights=jnp.stack([w_a, w_b], axis=1))


def _pack_bf16_pairs(x):
    half = x.shape[1] // 2
    bits = pltpu.bitcast(x.astype(BF16).astype(F32), jnp.int32)
    return (bits[:, :half] & jnp.int32(-65536)) | lax.shift_right_logical(bits[:, half:], jnp.int32(16))


def _unpack_bf16_pairs(p):
    hi = pltpu.bitcast(p & jnp.int32(-65536), F32)
    lo = pltpu.bitcast(lax.shift_left(p, jnp.int32(16)), F32)
    return jnp.concatenate([hi, lo], axis=1)


def sc_gather_rows(table, idx):
    _, w = table.shape
    b = idx.shape[0]
    workers = SC_CORES * SC_SUBCORES
    ch = SC_GATHER_CHUNK
    n_chunks = b // (workers * ch)
    assert n_chunks * workers * ch == b
    mesh = plsc.VectorSubcoreMesh(core_axis_name="c", subcore_axis_name="s", num_cores=SC_CORES,
                                  num_subcores=SC_SUBCORES)

    @functools.partial(
        pl.kernel, mesh=mesh, out_type=jax.ShapeDtypeStruct((b, w), table.dtype),
        scratch_types=[pltpu.VMEM((ch,), jnp.int32), pltpu.VMEM((ch, w), table.dtype), pltpu.SemaphoreType.DMA])
    def gather(table_hbm, idx_hbm, out_hbm, idx_v, rows_v, sem):
        worker = lax.axis_index("s") * SC_CORES + lax.axis_index("c")

        @pl.loop(0, n_chunks)
        def _(c):
            base = (worker * n_chunks + c) * ch
            pltpu.sync_copy(idx_hbm.at[pl.ds(base, ch)], idx_v)
            pltpu.async_copy(table_hbm.at[idx_v], rows_v, sem).wait()
            pltpu.sync_copy(rows_v, out_hbm.at[pl.ds(base, ch)])

    return gather(table, idx)


def sc_scatter_rows(rows, idx, n_out):
    t, w = rows.shape
    copies = idx.shape[0] // t
    workers = SC_CORES * SC_SUBCORES
    ch = SC_GATHER_CHUNK
    n_chunks = t // (workers * ch)
    assert n_chunks * workers * ch == t and copies * t == idx.shape[0]
    mesh = plsc.VectorSubcoreMesh(core_axis_name="c", subcore_axis_name="s", num_cores=SC_CORES,
                                  num_subcores=SC_SUBCORES)

    @functools.partial(
        pl.kernel, mesh=mesh, out_type=jax.ShapeDtypeStruct((n_out, w), rows.dtype),
        scratch_types=[pltpu.VMEM((ch,), jnp.int32), pltpu.VMEM((ch, w), rows.dtype), pltpu.SemaphoreType.DMA])
    def scatter(rows_hbm, idx_hbm, out_hbm, idx_v, rows_v, sem):
        worker = lax.axis_index("s") * SC_CORES + lax.axis_index("c")

        @pl.loop(0, n_chunks)
        def _(c):
            base = (worker * n_chunks + c) * ch
            pltpu.sync_copy(rows_hbm.at[pl.ds(base, ch)], rows_v)
            for k in range(copies):
                pltpu.sync_copy(idx_hbm.at[pl.ds(k * t + base, ch)], idx_v)
                pltpu.async_copy(rows_v, out_hbm.at[idx_v], sem).wait()

    return scatter(rows, idx)


def _gffn_kernel(te_ref, nu_ref, nr_ref, x_ref, wi_ref, wo_ref, y_ref):
    i = pl.program_id(0)
    used = i < nu_ref[0]

    @pl.when(used)
    def _():
        packed = x_ref[...]
        row = lax.broadcasted_iota(jnp.int32, packed.shape, 0)
        x = _unpack_bf16_pairs(jnp.where(row < nr_ref[i], packed, 0)).astype(BF16)
        y = _swiglu_tile(x, lambda c: wi_ref[0, :, c], lambda r: wo_ref[0, r, :], wo_ref.shape[1])
        y_ref[...] = _pack_bf16_pairs(y)

    @pl.when(jnp.logical_not(used))
    def _():
        y_ref[...] = jnp.zeros_like(y_ref)


def moe_grouped_ffn(xs, w_in, w_out, plan):
    nr, half = xs.shape
    ne, dff, d = w_out.shape
    tm = MOE_ROW_TILE
    return pl.pallas_call(
        _gffn_kernel,
        grid_spec=pltpu.PrefetchScalarGridSpec(
            num_scalar_prefetch=3,
            grid=(nr // tm,),
            in_specs=[
                pl.BlockSpec((tm, half), lambda i, te, nu, tr: (jnp.minimum(i, nu[0] - 1), 0)),
                pl.BlockSpec((1, d, 2 * dff), lambda i, te, nu, tr: (te[i], 0, 0)),
                pl.BlockSpec((1, dff, d), lambda i, te, nu, tr: (te[i], 0, 0)),
            ],
            out_specs=pl.BlockSpec((tm, half), lambda i, te, nu, tr: (i, 0)),
        ),
        out_shape=jax.ShapeDtypeStruct((nr, half), jnp.int32),
        compiler_params=_cparams("arbitrary"),
        name="moe_grouped_ffn",
    )(plan["tile_expert"], plan["n_used"], plan["tile_rows"], xs, w_in, w_out)


def _combine_kernel(ya_ref, yb_ref, w_ref, x_ref, mod_ref, lng_ref, lnb_ref, op_ref, os_ref, *, n_prompt_tiles):
    i = pl.program_id(0)
    w = w_ref[...]
    moe = w[:, 0:1] * _unpack_bf16_pairs(ya_ref[...]) + w[:, 1:2] * _unpack_bf16_pairs(yb_ref[...])
    z = DEEPNORM_ALPHA * x_ref[...] + mod_ref[0, 5:6, :] * moe
    xn = _layer_norm(z, lng_ref[...], lnb_ref[...])

    @pl.when(i < n_prompt_tiles)
    def _():
        op_ref[...] = xn

    @pl.when(i >= n_prompt_tiles)
    def _():
        os_ref[...] = xn


def moe_combine(y_tok, weights, x, mod, ln_g, ln_b, n_prompt_tok, sample_len):
    t, d = x.shape
    tt = TOKEN_TILE
    nt = t // tt
    npt = n_prompt_tok // tt
    seg = functools.partial(_seg_of_tile, tile=tt, n_prompt_tok=n_prompt_tok, sample_len=sample_len)
    return pl.pallas_call(
        functools.partial(_combine_kernel, n_prompt_tiles=npt),
        grid=(nt,),
        in_specs=[
            pl.BlockSpec((tt, d // 2), lambda i: (i, 0)),
            pl.BlockSpec((tt, d // 2), lambda i: (nt + i, 0)),
            pl.BlockSpec((tt, 2), lambda i: (i, 0)),
            pl.BlockSpec((tt, d), lambda i: (i, 0)),
            pl.BlockSpec((1, 6, d), lambda i: (seg(i), 0, 0)),
            pl.BlockSpec((1, d), lambda i: (0, 0)),
            pl.BlockSpec((1, d), lambda i: (0, 0)),
        ],
        out_specs=[
            pl.BlockSpec((tt, d), lambda i: (jnp.minimum(i, npt - 1), 0)),
            pl.BlockSpec((tt, d), lambda i: (jnp.maximum(i - npt, 0), 0)),
        ],
        out_shape=[jax.ShapeDtypeStruct((n_prompt_tok, d), F32), jax.ShapeDtypeStruct((t - n_prompt_tok, d), F32)],
        compiler_params=_cparams("arbitrary"),
        name="moe_combine",
    )(y_tok, y_tok, weights, x, mod, ln_g.reshape(1, d), ln_b.reshape(1, d))


def _layer1(x, u, mod1, cache_k, cache_v, ln_g, ln_b, w_qkv, rpb, w_out, w_router, moe_w_in, moe_w_out,
            n_prompt, prompt_len, n_sample, sample_len):
    n_prompt_tok = n_prompt * prompt_len
    d = D_MODEL
    w_qkv = jnp.concatenate([w_qkv[:, :d] * ATTN_Q_SCALE, w_qkv[:, d:]], axis=1).astype(BF16)
    qkv_p, k_p, v_p = na_qkv(u, w_qkv, 0, n_prompt_tok, with_kv=True)
    qkv_s, = na_qkv(u, w_qkv, n_prompt_tok, u.shape[0] - n_prompt_tok, with_kv=False)
    attn_p = context_attention(qkv_p, n_prompt, prompt_len)
    k_ctx = cache_k.reshape(n_sample, -1, d).astype(BF16)
    v_ctx = cache_v.reshape(n_sample, -1, d).astype(BF16)
    attn_s = neighbourhood_attention(qkv_s, k_ctx, v_ctx, rpb, n_sample, sample_len)
    moe_w_in, moe_w_out = moe_w_in.astype(BF16), moe_w_out.astype(BF16)
    x1, u1_packed, comb, pos, totals = mixer_outproj_router(
        attn_p, attn_s, x, mod1, w_out.astype(BF16), ln_g[0], ln_b[0], w_router, sample_len,
        run_after=(moe_w_in, moe_w_out))
    plan = _moe_plan(totals[:, 0].astype(jnp.int32), pos, comb, x.shape[0])
    xs = sc_scatter_rows(u1_packed, plan["token_rows"], plan["n_rows"])
    y = moe_grouped_ffn(xs, moe_w_in, moe_w_out, plan)
    y_tok = sc_gather_rows(y, plan["token_rows"])
    y_p, y_s = moe_combine(y_tok, plan["token_weights"], x1, mod1, ln_g[1], ln_b[1], n_prompt_tok, sample_len)
    return y_p, y_s, k_p, v_p


def kernel(x_prompt, x_sample, state_mlstm_C, state_mlstm_n, state_mlstm_m, cache_na_k, cache_na_v, c, c_ctx,
           ada_w, ada_b, ln_g, ln_b, mlstm_w_in, mlstm_b_gates, mlstm_norm_g, mlstm_w_out, na_w_qkv, na_rpb,
           na_w_out, ffn_w_in, ffn_w_out, moe_w_router, moe_w_in, moe_w_out):
    n_prompt, prompt_len, d = x_prompt.shape
    n_sample, sample_len, _ = x_sample.shape
    assert d == D_MODEL and prompt_len == MLSTM_CHUNK and sample_len % MLSTM_CHUNK == 0
    assert ada_w.shape[0] == DEPTH == 2
    mod = _modulation_tables(c, c_ctx, ada_w, ada_b)
    x2, u2, new_c, new_n, new_m = _layer0(
        x_prompt.reshape(-1, d), x_sample.reshape(-1, d), mod[0], mod[1], state_mlstm_C[:, 0], state_mlstm_n[:, 0], state_mlstm_m[:, 0], ln_g[0], ln_b[0],
        mlstm_w_in[0], mlstm_b_gates[0], mlstm_norm_g[0], mlstm_w_out[0], ffn_w_in[0], ffn_w_out[0],
        n_prompt, prompt_len, n_sample, sample_len)
    y_p, y_s, k_p, v_p = _layer1(
        x2, u2, mod[1], cache_na_k[:, 0], cache_na_v[:, 0], ln_g[1], ln_b[1], na_w_qkv[0], na_rpb[0], na_w_out[0],
        moe_w_router[0], moe_w_in[0], moe_w_out[0], n_prompt, prompt_len, n_sample, sample_len)
    kv_shape = (n_prompt, 1, prompt_len, NA_HEADS, NA_DH)
    return (y_p.reshape(x_prompt.shape), y_s.reshape(x_sample.shape), new_c, new_n, new_m,
            k_p.reshape(kv_shape), v_p.reshape(kv_shape))
```

```python
import functools
import math

import jax
import jax.numpy as jnp
from jax import lax
from jax.experimental import pallas as pl
from jax.experimental.pallas import tpu as pltpu
from jax.experimental.pallas import tpu_sc as plsc

F32 = jnp.float32
BF16 = jnp.bfloat16

D_MODEL = 1024
DEPTH = 2
GRID_W = 64
M_HEADS = 4
M_DK = 128
M_DV = 256
M_HK = M_HEADS * M_DK
NA_HEADS = 16
NA_DH = 64
NA_ROWS = 8
NA_COLS = 16
DEEPNORM_ALPHA = (2.0 * DEPTH) ** 0.25
LN_EPS = 1e-5
NEG = -1e30

VMEM_LIMIT_BYTES = 56 * 1024 * 1024

MLSTM_CHUNK = 256
TOKEN_TILE = 512
FFN_TOKEN_TILE = 512
FFN_SUB = 256
NA_QROWS = 4
NA_HALO = 4


def _cparams(*sem):
    return pltpu.CompilerParams(dimension_semantics=sem, vmem_limit_bytes=VMEM_LIMIT_BYTES)


def _dot(a, b):
    return jnp.dot(a, b, preferred_element_type=F32)


def _dot_nt(a, b):
    return lax.dot_general(a, b, (((1,), (1,)), ((), ())), preferred_element_type=F32)


def _onehot(mask):
    return jnp.where(mask, 1.0, 0.0).astype(BF16)


def _split3(x):
    hi = x.astype(BF16)
    r = x - hi.astype(F32)
    mid = r.astype(BF16)
    lo = (r - mid.astype(F32)).astype(BF16)
    return hi, mid, lo


def _layer_norm(z, g, b):
    mu = jnp.mean(z, axis=-1, keepdims=True)
    zc = z - mu
    var = jnp.mean(zc * zc, axis=-1, keepdims=True)
    return zc * lax.rsqrt(var + LN_EPS) * g + b


def _log_sigmoid(x):
    return jnp.minimum(x, 0.0) - jnp.log(1.0 + jnp.exp(-jnp.abs(x)))


def _seg_of_tile(i, tile, n_prompt_tok, sample_len):
    tok = i * tile
    return jnp.where(tok < n_prompt_tok, 0, 1 + (tok - n_prompt_tok) // sample_len)


def _adaln_kernel(c_ref, w_ref, b_ref, o_ref):
    c = c_ref[...]
    a = (c * jax.nn.sigmoid(c)).astype(BF16)
    o_ref[0] = _dot(a, w_ref[0].astype(BF16)) + b_ref[0]


def adaln(cvec, ada_w, ada_b):
    depth, d, n = ada_w.shape
    tn = 1536
    return pl.pallas_call(
        _adaln_kernel,
        grid=(depth, n // tn),
        in_specs=[
            pl.BlockSpec((8, d), lambda l, j: (0, 0)),
            pl.BlockSpec((1, d, tn), lambda l, j: (l, 0, j)),
            pl.BlockSpec((1, 1, tn), lambda l, j: (l, 0, j)),
        ],
        out_specs=pl.BlockSpec((1, 8, tn), lambda l, j: (l, 0, j)),
        out_shape=jax.ShapeDtypeStruct((depth, 8, n), F32),
        compiler_params=_cparams("arbitrary", "arbitrary"),
        name="adaln",
    )(cvec, ada_w, ada_b.reshape(depth, 1, n))


def _inproj_kernel(xp_ref, xs_ref, mod_ref, w_ref, wg_ref, wgt_ref, bg_ref, bgt_ref,
                   qkv_ref, o_ref, g_ref, gt_ref, *, n_prompt_tiles):
    x = jnp.where(pl.program_id(0) < n_prompt_tiles, xp_ref[...], xs_ref[...])
    u = (x * (1.0 + mod_ref[0, 1:2, :]) + mod_ref[0, 0:1, :]).astype(BF16)
    nqkv = qkv_ref.shape[1]
    half = nqkv // 2
    g = _dot(u, wg_ref[...]) + bg_ref[...]
    gt = _dot_nt(wgt_ref[...], u) + bgt_ref[...]
    p0 = _dot(u, w_ref[:, :half])
    col = lax.broadcasted_iota(jnp.int32, g.shape, 1)
    g_ref[...] = jnp.where(((col >> 2) & 1) == 1, _log_sigmoid(g), g)
    p1 = _dot(u, w_ref[:, half:nqkv])
    qkv_ref[:, :half] = p0.astype(BF16)
    row = lax.broadcasted_iota(jnp.int32, gt.shape, 0)
    gt_ref[...] = jnp.where(((row >> 2) & 1) == 1, _log_sigmoid(gt), gt)
    p2 = _dot(u, w_ref[:, nqkv:])
    qkv_ref[:, half:] = p1.astype(BF16)
    o_ref[...] = p2


def _split_token_specs(tile, d, n_prompt_tiles):
    return [pl.BlockSpec((tile, d), lambda i: (jnp.minimum(i, n_prompt_tiles - 1), 0)),
            pl.BlockSpec((tile, d), lambda i: (jnp.maximum(i - n_prompt_tiles, 0), 0))]


def mlstm_inproj(xp, xs, mod, w_main, w_gate, w_gate_t, b_gate, sample_len):
    n_prompt_tok, d = xp.shape
    t = n_prompt_tok + xs.shape[0]
    n_main = w_main.shape[1]
    n_qkv = 2 * M_HK + D_MODEL
    ng = w_gate.shape[1]
    tt = TOKEN_TILE
    npt = n_prompt_tok // tt
    seg = functools.partial(_seg_of_tile, tile=tt, n_prompt_tok=n_prompt_tok, sample_len=sample_len)
    return pl.pallas_call(
        functools.partial(_inproj_kernel, n_prompt_tiles=npt),
        grid=(t // tt,),
        in_specs=_split_token_specs(tt, d, npt) + [
            pl.BlockSpec((1, 6, d), lambda i: (seg(i), 0, 0)),
            pl.BlockSpec((d, n_main), lambda i: (0, 0)),
            pl.BlockSpec((d, ng), lambda i: (0, 0)),
            pl.BlockSpec((ng, d), lambda i: (0, 0)),
            pl.BlockSpec((1, ng), lambda i: (0, 0)),
            pl.BlockSpec((ng, 1), lambda i: (0, 0)),
        ],
        out_specs=[
            pl.BlockSpec((tt, n_qkv), lambda i: (i, 0)),
            pl.BlockSpec((tt, n_main - n_qkv), lambda i: (i, 0)),
            pl.BlockSpec((tt, ng), lambda i: (i, 0)),
            pl.BlockSpec((ng, tt), lambda i: (0, i)),
        ],
        out_shape=[
            jax.ShapeDtypeStruct((t, n_qkv), BF16),
            jax.ShapeDtypeStruct((t, n_main - n_qkv), F32),
            jax.ShapeDtypeStruct((t, ng), F32),
            jax.ShapeDtypeStruct((ng, t), F32),
        ],
        compiler_params=_cparams("arbitrary"),
        name="mlstm_inproj",
    )(xp, xs, mod, w_main, w_gate, w_gate_t, b_gate.reshape(1, ng), b_gate.reshape(ng, 1))


def _mlstm_gate_sums(g, gt, backward):
    L = g.shape[0]
    row = lax.broadcasted_iota(jnp.int32, (L, L), 0)
    col = lax.broadcasted_iota(jnp.int32, (L, L), 1)
    lower = col <= row
    upper = col >= row
    tri_col = _onehot(upper if backward else lower)
    tri_row = _onehot(lower if backward else upper)
    ghi, gmid, glo = _split3(g)
    b_cols = _dot(tri_col, ghi) + _dot(tri_col, gmid) + _dot(tri_col, glo)
    thi, tmid, tlo = _split3(gt)
    b_rows = _dot(thi, tri_row) + _dot(tmid, tri_row) + _dot(tlo, tri_row)
    return b_cols, b_rows


def _mlstm_heads(items):
    scale = M_DK ** -0.5
    log_scale = math.log(scale)
    L = items[0][0].shape[0]
    row = lax.broadcasted_iota(jnp.int32, (L, L), 0)
    col = lax.broadcasted_iota(jnp.int32, (L, L), 1)
    masks = {False: col <= row, True: col >= row}
    n = range(len(items))
    qh, kh, vh, li_row, lf_row, b_row, b_col, backward, state = zip(*items)
    has_state = [st is not None for st in state]
    m_prev = [st[2] if st is not None else 0.0 for st in state]

    qk = [_dot_nt(qh[i], kh[i]) for i in n]
    qc = [_dot(qh[i], state[i][0].astype(BF16)) if has_state[i] else None for i in n]
    qn = [_dot_nt(qh[i], state[i][1].astype(BF16))[:, 0:1] if has_state[i] else None for i in n]
    total = [jnp.sum(lf_row[i], axis=1, keepdims=True) for i in n]
    d = [jnp.where(masks[backward[i]], b_col[i] + (li_row[i] - b_row[i] + log_scale), NEG) for i in n]
    inter = [b_col[i] + m_prev[i] if has_state[i] else b_col[i] for i in n]
    m_t = [jnp.maximum(inter[i], jnp.max(d[i], axis=1, keepdims=True) - log_scale) for i in n]
    p = [jnp.exp(d[i] - m_t[i]) for i in n]
    s = [qk[i] * p[i] for i in n]
    den = [jnp.sum(s[i], axis=1, keepdims=True) for i in n]
    num = [_dot(s[i].astype(BF16), vh[i]) for i in n]
    a = [jnp.exp(inter[i] - m_t[i]) * scale if has_state[i] else None for i in n]
    num = [a[i] * qc[i] + num[i] if has_state[i] else num[i] for i in n]
    den = [a[i] * qn[i] + den[i] if has_state[i] else den[i] for i in n]
    hh = [num[i] / jnp.maximum(jnp.abs(den[i]), jnp.exp(-m_t[i])) for i in n]

    g_row = [total[i] - b_row[i] + li_row[i] for i in n]
    m_new = [jnp.maximum(total[i] + m_prev[i], jnp.max(g_row[i], axis=1, keepdims=True)) for i in n]
    ws_row = [jnp.exp(g_row[i] - m_new[i]) for i in n]
    kt = [kh[i].astype(F32).T for i in n]
    c_new = [_dot((kt[i] * ws_row[i]).astype(BF16), vh[i]) for i in n]
    n_new = [_dot(jnp.broadcast_to(ws_row[i], (8, L)).astype(BF16), kh[i]) for i in n]
    a0 = [jnp.exp(total[i] + m_prev[i] - m_new[i]) if has_state[i] else None for i in n]
    c_new = [a0[i] * state[i][0] + c_new[i] if has_state[i] else c_new[i] for i in n]
    n_new = [a0[i] * state[i][1] + n_new[i] if has_state[i] else n_new[i] for i in n]
    return hh, c_new, n_new, m_new


def _head_norm_gate(ht, norm_g, ogate):
    mu = jnp.mean(ht, axis=-1, keepdims=True)
    hc = ht - mu
    var = jnp.mean(hc * hc, axis=-1, keepdims=True)
    return (hc * lax.rsqrt(var + LN_EPS) * norm_g * jax.nn.sigmoid(ogate)).astype(BF16)


def _head_operands(q_ref, k_ref, v_ref, h):
    return (q_ref[:, h * M_DK:(h + 1) * M_DK], k_ref[:, h * M_DK:(h + 1) * M_DK], v_ref[:, h * M_DV:(h + 1) * M_DV])


def _gate_operands(gt, b_rows, b_cols, direction, h):
    ci = direction * 8 + h
    cf = direction * 8 + 4 + h
    return gt[ci:ci + 1, :], gt[cf:cf + 1, :], b_rows[cf:cf + 1, :], b_cols[:, cf:cf + 1]


def _mlstm_prompt_kernel(q_ref, k_ref, v_ref, g_ref, gt_ref, o_ref, ng_ref, a_ref, cf_ref, nf_ref, mf_ref):
    g, gt = g_ref[...], gt_ref[...]
    sums = [_mlstm_gate_sums(g, gt, direction == 1) for direction in (0, 1)]
    keys = [(direction, h) for direction in (0, 1) for h in range(M_HEADS)]
    items = [_head_operands(q_ref, k_ref, v_ref, h)
             + _gate_operands(gt, sums[direction][1], sums[direction][0], direction, h) + (direction == 1, None)
             for direction, h in keys]
    hh, c_new, n_new, m_new = _mlstm_heads(items)
    for i, (direction, h) in enumerate(keys):
        cf_ref[0, 0, direction, h] = c_new[i]
        nf_ref[0, 0, direction, h] = n_new[i]
        mf_ref[0, 0, direction, h] = jnp.broadcast_to(m_new[i], mf_ref.shape[-2:])
    for h in range(M_HEADS):
        sl = slice(h * M_DV, (h + 1) * M_DV)
        a_ref[:, sl] = _head_norm_gate(hh[h] + hh[M_HEADS + h], ng_ref[:, sl], o_ref[:, sl])


def _mlstm_sample_kernel(*refs, direction, n_units):
    backward = direction == 1
    refs = list(refs)
    unit_refs = [tuple(refs.pop(0) for _ in range(5)) for _ in range(n_units)]
    c0_ref, n0_ref, m0_ref = refs.pop(0), refs.pop(0), refs.pop(0)
    if backward:
        hprev_ref = refs.pop(0)
        o_refs = [refs.pop(0) for _ in range(n_units)]
        ng_ref = refs.pop(0)
    out_ref, c_scr, n_scr, m_scr = refs

    @pl.when(pl.program_id(0) == 0)
    def _():
        c_scr[...] = c0_ref[...]
        n_scr[...] = n0_ref[...]
        m_scr[...] = m0_ref[...]

    keys, items = [], []
    for u, (q_ref, k_ref, v_ref, g_ref, gt_ref) in enumerate(unit_refs):
        gt = gt_ref[...]
        b_cols, b_rows = _mlstm_gate_sums(g_ref[...], gt, backward)
        for h in range(M_HEADS):
            state = (c_scr[u, h], n_scr[u, h], m_scr[u, h][0:1, 0:1])
            keys.append((u, h))
            items.append(_head_operands(q_ref, k_ref, v_ref, h) + _gate_operands(gt, b_rows, b_cols, direction, h)
                         + (backward, state))
    hh, c_new, n_new, m_new = _mlstm_heads(items)
    for i, (u, h) in enumerate(keys):
        c_scr[u, h] = c_new[i]
        n_scr[u, h] = n_new[i]
        m_scr[u, h] = jnp.broadcast_to(m_new[i], m_scr.shape[-2:])
        sl = slice(h * M_DV, (h + 1) * M_DV)
        if backward:
            out_ref[u, :, sl] = _head_norm_gate(hprev_ref[u, :, sl] + hh[i], ng_ref[:, sl], o_refs[u][:, sl])
        else:
            out_ref[u, :, sl] = hh[i]


def _mlstm_chunk_specs(block_of):
    L = MLSTM_CHUNK
    return [
        pl.BlockSpec((L, M_HK), lambda s: (block_of(s), 0)),
        pl.BlockSpec((L, M_HK), lambda s: (block_of(s), 1)),
        pl.BlockSpec((L, D_MODEL), lambda s: (block_of(s), 1)),
        pl.BlockSpec((L, 16), lambda s: (block_of(s), 0)),
        pl.BlockSpec((16, L), lambda s: (0, block_of(s))),
    ]


def mlstm_prompt(qkv, g, gt, ogate, norm_g, n_prompt):
    L = MLSTM_CHUNK
    state_blk = lambda *tail: pl.BlockSpec((1, 1, 2, M_HEADS) + tail, lambda s: (s, 0, 0, 0, 0, 0))
    return pl.pallas_call(
        _mlstm_prompt_kernel,
        grid=(n_prompt,),
        in_specs=_mlstm_chunk_specs(lambda s: s) + [
            pl.BlockSpec((L, D_MODEL), lambda s: (s, 0)),
            pl.BlockSpec((1, D_MODEL), lambda s: (0, 0)),
        ],
        out_specs=[pl.BlockSpec((L, D_MODEL), lambda s: (s, 0)),
                   state_blk(M_DK, M_DV), state_blk(8, M_DK), state_blk(8, 128)],
        out_shape=[
            jax.ShapeDtypeStruct((n_prompt * L, D_MODEL), BF16),
            jax.ShapeDtypeStruct((n_prompt, 1, 2, M_HEADS, M_DK, M_DV), F32),
            jax.ShapeDtypeStruct((n_prompt, 1, 2, M_HEADS, 8, M_DK), F32),
            jax.ShapeDtypeStruct((n_prompt, 1, 2, M_HEADS, 8, 128), F32),
        ],
        compiler_params=_cparams("arbitrary"),
        name="mlstm_prompt",
    )(qkv, qkv, qkv, g, gt, ogate, norm_g.reshape(1, D_MODEL))


def mlstm_sample(direction, qkv, g, gt, c0, n0, m0, first_block, chunks_per_sample,
                 hprev=None, ogate=None, norm_g=None):
    L = MLSTM_CHUNK
    backward = direction == 1
    cps = chunks_per_sample
    n_units = c0.shape[0]
    pos = (lambda s: cps - 1 - s) if backward else (lambda s: s)
    unit_block = [functools.partial(lambda s, u: first_block + u * cps + pos(s), u=u) for u in range(n_units)]
    whole = lambda a: pl.BlockSpec(a.shape, lambda s: (0,) * a.ndim)
    in_specs, args = [], []
    for u in range(n_units):
        in_specs += _mlstm_chunk_specs(unit_block[u])
        args += [qkv, qkv, qkv, g, gt]
    in_specs += [whole(c0), whole(n0), whole(m0)]
    args += [c0, n0, m0]
    if backward:
        in_specs += [pl.BlockSpec((n_units, L, D_MODEL), lambda s: (0, pos(s), 0))]
        in_specs += [pl.BlockSpec((L, D_MODEL), functools.partial(lambda s, u: (unit_block[u](s), 0), u=u))
                     for u in range(n_units)]
        in_specs += [pl.BlockSpec((1, D_MODEL), lambda s: (0, 0))]
        args += [hprev] + [ogate] * n_units + [norm_g.reshape(1, D_MODEL)]
    return pl.pallas_call(
        functools.partial(_mlstm_sample_kernel, direction=direction, n_units=n_units),
        grid=(cps,),
        in_specs=in_specs,
        out_specs=pl.BlockSpec((n_units, L, D_MODEL), lambda s: (0, pos(s), 0)),
        out_shape=jax.ShapeDtypeStruct((n_units, cps * L, D_MODEL), BF16 if backward else F32),
        scratch_shapes=[pltpu.VMEM(c0.shape, F32), pltpu.VMEM(n0.shape, F32), pltpu.VMEM(m0.shape, F32)],
        compiler_params=_cparams("arbitrary"),
        name="mlstm_sample_bwd" if backward else "mlstm_sample_fwd",
    )(*args)


def _outproj_router_kernel(ap_ref, as_ref, x_ref, mod_ref, w_ref, lng_ref, lnb_ref, wr_ref, *refs,
                           n_prompt_tiles):
    xo_ref, up_ref, comb_ref, pos_ref, total_ref, carry_scr = refs[-6:]
    i = pl.program_id(0)

    @pl.when(i == 0)
    def _():
        carry_scr[...] = jnp.zeros_like(carry_scr)

    a = jnp.where(i < n_prompt_tiles, ap_ref[...], as_ref[...])
    z = DEEPNORM_ALPHA * x_ref[...] + mod_ref[0, 2:3, :] * _dot(a, w_ref[...])
    xn = _layer_norm(z, lng_ref[...], lnb_ref[...])
    xo_ref[...] = xn
    u = xn * (1.0 + mod_ref[0, 4:5, :]) + mod_ref[0, 3:4, :]
    up_ref[...] = _pack_bf16_pairs(u)
    comb, pos, carry_new = _route_tokens(u.astype(BF16), wr_ref[...], carry_scr[:, 0:1])
    comb_ref[...] = comb
    pos_ref[...] = pos
    carry_scr[...] = jnp.broadcast_to(carry_new, carry_scr.shape)
    total_ref[...] = jnp.broadcast_to(carry_new, carry_scr.shape)


def mixer_outproj_router(a_p, a_s, x, mod, w, ln_g, ln_b, w_router, sample_len, run_after=()):
    t, d = x.shape
    ne = w_router.shape[1]
    tt = TOKEN_TILE
    npt = a_p.shape[0] // tt
    seg = functools.partial(_seg_of_tile, tile=tt, n_prompt_tok=a_p.shape[0], sample_len=sample_len)
    return pl.pallas_call(
        functools.partial(_outproj_router_kernel, n_prompt_tiles=npt),
        grid=(t // tt,),
        in_specs=_split_token_specs(tt, d, npt) + [
            pl.BlockSpec((tt, d), lambda i: (i, 0)),
            pl.BlockSpec((1, 6, d), lambda i: (seg(i), 0, 0)),
            pl.BlockSpec((d, d), lambda i: (0, 0)),
            pl.BlockSpec((1, d), lambda i: (0, 0)),
            pl.BlockSpec((1, d), lambda i: (0, 0)),
            pl.BlockSpec((ne, d), lambda i: (0, 0)),
        ] + [pl.BlockSpec(memory_space=pl.ANY)] * len(run_after),
        out_specs=[
            pl.BlockSpec((tt, d), lambda i: (i, 0)),
            pl.BlockSpec((tt, d // 2), lambda i: (i, 0)),
            pl.BlockSpec((ne, tt), lambda i: (0, i)),
            pl.BlockSpec((ne, tt), lambda i: (0, i)),
            pl.BlockSpec((ne, 128), lambda i: (0, 0)),
        ],
        out_shape=[
            jax.ShapeDtypeStruct((t, d), F32),
            jax.ShapeDtypeStruct((t, d // 2), jnp.int32),
            jax.ShapeDtypeStruct((ne, t), F32),
            jax.ShapeDtypeStruct((ne, t), F32),
            jax.ShapeDtypeStruct((ne, 128), F32),
        ],
        scratch_shapes=[pltpu.VMEM((ne, 128), F32)],
        compiler_params=_cparams("arbitrary"),
        name="mixer_outproj_router",
    )(a_p, a_s, x, mod, w, ln_g.reshape(1, d), ln_b.reshape(1, d), w_router.T, *run_after)


def _swiglu_tile(x, wi, wo, dff):
    sub = FFN_SUB
    proj = lambda j: (_dot(x, wi(slice(j * sub, (j + 1) * sub))), _dot(x, wi(slice(dff + j * sub, dff + (j + 1) * sub))))
    acts, cur = [], proj(0)
    for j in range(dff // sub):
        nxt = proj(j + 1) if (j + 1) * sub < dff else None
        gp, up = cur
        acts.append((gp * jax.nn.sigmoid(gp) * up).astype(BF16))
        cur = nxt
    return _dot(jnp.concatenate(acts, axis=1), wo(slice(0, dff)))


def _mixer_ffn_kernel(ap_ref, as_ref, xp_ref, xs_ref, mod_ref, modn_ref, wmix_ref, wi_ref, wo_ref, ln_ref,
                      xo_ref, uo_ref, *, n_prompt_tiles):
    is_prompt = pl.program_id(0) < n_prompt_tiles
    a = jnp.where(is_prompt, ap_ref[...], as_ref[...])
    x = jnp.where(is_prompt, xp_ref[...], xs_ref[...])
    z = DEEPNORM_ALPHA * x + mod_ref[0, 2:3, :] * _dot(a, wmix_ref[...])
    x1 = _layer_norm(z, ln_ref[0:1, :], ln_ref[1:2, :])
    u1 = (x1 * (1.0 + mod_ref[0, 4:5, :]) + mod_ref[0, 3:4, :]).astype(BF16)
    f = _swiglu_tile(u1, lambda c: wi_ref[:, c], lambda r: wo_ref[r, :], wo_ref.shape[0])
    x2 = _layer_norm(DEEPNORM_ALPHA * x1 + mod_ref[0, 5:6, :] * f, ln_ref[2:3, :], ln_ref[3:4, :])
    xo_ref[...] = x2
    uo_ref[...] = (x2 * (1.0 + modn_ref[0, 1:2, :]) + modn_ref[0, 0:1, :]).astype(BF16)


def _resident(shape):
    return pl.BlockSpec(shape, lambda i: (0,) * len(shape), pipeline_mode=pl.Buffered(1))


def mixer_ffn(a_p, a_s, xp, xs, mod, mod_next, w_mix, w_in, w_out, ln_g, ln_b, sample_len):
    n_prompt_tok, d = xp.shape
    t = n_prompt_tok + xs.shape[0]
    tm = FFN_TOKEN_TILE
    npt = n_prompt_tok // tm
    seg = functools.partial(_seg_of_tile, tile=tm, n_prompt_tok=n_prompt_tok, sample_len=sample_len)
    ln = jnp.stack([ln_g[0], ln_b[0], ln_g[1], ln_b[1]])
    return pl.pallas_call(
        functools.partial(_mixer_ffn_kernel, n_prompt_tiles=npt),
        grid=(t // tm,),
        in_specs=_split_token_specs(tm, d, npt) + _split_token_specs(tm, d, npt) + [
            pl.BlockSpec((1, 6, d), lambda i: (seg(i), 0, 0)),
            pl.BlockSpec((1, 6, d), lambda i: (seg(i), 0, 0)),
            _resident(w_mix.shape), _resident(w_in.shape), _resident(w_out.shape), _resident(ln.shape),
        ],
        out_specs=[pl.BlockSpec((tm, d), lambda i: (i, 0)), pl.BlockSpec((tm, d), lambda i: (i, 0))],
        out_shape=[jax.ShapeDtypeStruct((t, d), F32), jax.ShapeDtypeStruct((t, d), BF16)],
        compiler_params=_cparams("arbitrary"),
        name="mixer_ffn",
    )(a_p, a_s, xp, xs, mod, mod_next, w_mix, w_in, w_out, ln)


def _modulation_tables(c, c_ctx, ada_w, ada_b):
    n_s = c.shape[0]
    cvec = jnp.concatenate([c_ctx[None, :], c, jnp.zeros((8 - 1 - n_s, c.shape[1]), F32)], axis=0)
    mod = adaln(cvec, ada_w, ada_b)
    return mod[:, :1 + n_s, :].reshape(ada_w.shape[0], 1 + n_s, 6, c.shape[1])


def _layer0(xp, xs, mod0, mod1, state_c, state_n, state_m, ln_g, ln_b, w_in, b_gates, norm_g, w_out,
            ffn_w_in, ffn_w_out, n_prompt, prompt_len, n_sample, sample_len):
    n_prompt_tok = n_prompt * prompt_len
    n_main = 2 * M_HK + 2 * D_MODEL
    w_gate = w_in[:, n_main:]
    qkv, ogate, g, gt = mlstm_inproj(xp, xs, mod0, w_in[:, :n_main].astype(BF16), w_gate.astype(BF16),
                                     w_gate.T.astype(BF16), b_gates.reshape(-1), sample_len)
    npc = n_prompt_tok // MLSTM_CHUNK
    cps = sample_len // MLSTM_CHUNK
    rep = lambda a: jnp.broadcast_to(a[..., None, :], a.shape[:-1] + (8, a.shape[-1]))
    n0 = rep(state_n)
    m0 = jnp.broadcast_to(state_m[..., None, None], state_m.shape + (8, 128))
    a_p, new_c, nf, mf = mlstm_prompt(qkv, g, gt, ogate, norm_g, n_prompt)
    hf = mlstm_sample(0, qkv, g, gt, state_c[:, 0], n0[:, 0], m0[:, 0], npc, cps)
    a_s = mlstm_sample(1, qkv, g, gt, state_c[:, 1], n0[:, 1], m0[:, 1], npc, cps,
                       hprev=hf, ogate=ogate, norm_g=norm_g)
    x2, u2 = mixer_ffn(a_p, a_s.reshape(-1, D_MODEL), xp, xs, mod0, mod1, w_out.astype(BF16),
                       ffn_w_in.astype(BF16), ffn_w_out.astype(BF16), ln_g, ln_b, sample_len)
    return x2, u2, new_c, nf[..., 0, :], mf[..., 0, 0]


def _qkv_kernel(u_ref, w_ref, qkv_ref, *kv_refs):
    tt, d = u_ref.shape
    u = u_ref[...]

    def head_major(out_ref, p):
        for h in range(NA_HEADS):
            out_ref[pl.ds(h, tt, stride=NA_HEADS), :] = p[:, h * NA_DH:(h + 1) * NA_DH]

    pq = _dot(u, w_ref[:, :d])
    pk = _dot(u, w_ref[:, d:2 * d])
    qkv_ref[:, :d] = pq.astype(BF16)
    pv = _dot(u, w_ref[:, 2 * d:])
    qkv_ref[:, d:2 * d] = pk.astype(BF16)
    if kv_refs:
        head_major(kv_refs[0], pk)
    qkv_ref[:, 2 * d:] = pv.astype(BF16)
    if kv_refs:
        head_major(kv_refs[1], pv)


def na_qkv(u, w, first_tok, n_tok, with_kv):
    d = u.shape[1]
    tt = TOKEN_TILE
    first = first_tok // tt
    out_specs = [pl.BlockSpec((tt, 3 * d), lambda i: (i, 0))]
    out_shape = [jax.ShapeDtypeStruct((n_tok, 3 * d), BF16)]
    if with_kv:
        out_specs += [pl.BlockSpec((tt * NA_HEADS, NA_DH), lambda i: (i, 0))] * 2
        out_shape += [jax.ShapeDtypeStruct((n_tok * NA_HEADS, NA_DH), F32)] * 2
    return pl.pallas_call(
        _qkv_kernel,
        grid=(n_tok // tt,),
        in_specs=[pl.BlockSpec((tt, d), lambda i: (first + i, 0)), pl.BlockSpec((d, 3 * d), lambda i: (0, 0))],
        out_specs=out_specs,
        out_shape=out_shape,
        compiler_params=_cparams("arbitrary"),
        name="na_qkv_prompt" if with_kv else "na_qkv_latent",
    )(u, w)


ATTN_LOG2E = math.log2(math.e)
ATTN_Q_SCALE = NA_DH ** -0.5 * ATTN_LOG2E


def _head_pair_masks():
    lane = lax.broadcasted_iota(jnp.int32, (1, 2 * NA_DH), 1)
    return (_onehot(lane < NA_DH), _onehot(lane >= NA_DH))


def _softmax_pv(score_blocks, value_blocks):
    mx = None
    for s in score_blocks:
        bm = jnp.max(s, axis=1, keepdims=True)
        mx = bm if mx is None else jnp.maximum(mx, bm)
    acc, denom = None, None
    for s, v in zip(score_blocks, value_blocks):
        p = jnp.exp2(s - mx)
        ps = jnp.sum(p, axis=1, keepdims=True)
        pv = _dot(p.astype(BF16), v)
        acc = pv if acc is None else acc + pv
        denom = ps if denom is None else denom + ps
    return acc / denom


def _ctx_attn_kernel(q_ref, k_ref, v_ref, o_ref):
    masks = _head_pair_masks()
    for hp in range(NA_HEADS // 2):
        sl = slice(hp * 2 * NA_DH, (hp + 1) * 2 * NA_DH)
        q2, k2, v2 = q_ref[:, sl], k_ref[:, sl], v_ref[:, sl]
        o2 = None
        for msk in masks:
            s = _dot_nt(q2 * msk, k2)
            o = _softmax_pv([s], [v2 * msk])
            o2 = o if o2 is None else o2 + o
        o_ref[:, sl] = o2.astype(o_ref.dtype)


def context_attention(qkv, n_prompt, prompt_len):
    t = n_prompt * prompt_len
    d = D_MODEL
    return pl.pallas_call(
        _ctx_attn_kernel,
        grid=(n_prompt,),
        in_specs=[
            pl.BlockSpec((prompt_len, d), lambda b: (b, 0)),
            pl.BlockSpec((prompt_len, d), lambda b: (b, 1)),
            pl.BlockSpec((prompt_len, d), lambda b: (b, 2)),
        ],
        out_specs=pl.BlockSpec((prompt_len, d), lambda b: (b, 0)),
        out_shape=jax.ShapeDtypeStruct((t, d), BF16),
        compiler_params=_cparams("arbitrary"),
        name="context_attention",
    )(qkv, qkv, qkv)


def _na_kernel(tile_ref, q_ref, kp_ref, kc_ref, kn_ref, vp_ref, vc_ref, vn_ref, kctx_ref, vctx_ref, bias_ref,
               o_ref):
    j = pl.program_id(1)
    nblk = pl.num_programs(1)
    m = q_ref.shape[0]
    halo = NA_HALO * GRID_W
    n_pairs = (NA_QROWS + 2 * NA_HALO) // 2
    variant = jnp.where(j == 0, 0, jnp.where(j == nblk - 1, 2, 1))
    kinds = [[tile_ref[(variant * NA_QROWS + rq) * n_pairs + kp] for kp in range(n_pairs)]
             for rq in range(NA_QROWS)]

    def bias_of(head):
        return jnp.concatenate(
            [jnp.concatenate([bias_ref[kinds[rq][kp], head] for kp in range(n_pairs)], axis=1)
             for rq in range(NA_QROWS)], axis=0)

    masks = _head_pair_masks()
    for hp in range(NA_HEADS // 2):
        sl = slice(hp * 2 * NA_DH, (hp + 1) * 2 * NA_DH)
        q2 = q_ref[:, sl]
        kloc = jnp.concatenate([kp_ref[m - halo:, sl], kc_ref[:, sl], kn_ref[:halo, sl]], axis=0)
        vloc = jnp.concatenate([vp_ref[m - halo:, sl], vc_ref[:, sl], vn_ref[:halo, sl]], axis=0)
        kctx = kctx_ref[0][:, sl]
        vctx = vctx_ref[0][:, sl]
        o2 = None
        for half, msk in enumerate(masks):
            qm = q2 * msk
            s_loc = _dot_nt(qm, kloc) + bias_of(2 * hp + half)
            s_ctx = _dot_nt(qm, kctx)
            o = _softmax_pv([s_loc, s_ctx], [vloc * msk, vctx * msk])
            o2 = o if o2 is None else o2 + o
        o_ref[:, sl] = o2.astype(o_ref.dtype)


def _na_bias_kernel(rpb_ref, o_ref, *, kinds, n_heads):
    nc, npos = rpb_ref.shape[1], o_ref.shape[2]
    pair_shift = (2 * GRID_W).bit_length() - 1
    c = lax.broadcasted_iota(jnp.int32, (nc, npos), 0)
    pos = lax.broadcasted_iota(jnp.int32, (nc, npos), 1)
    qc = pos >> pair_shift
    kc = pos & (GRID_W - 1)
    c_start = jnp.clip(qc - NA_COLS // 2, 0, GRID_W - NA_COLS)
    col_in = jnp.logical_and(kc >= c_start, kc < c_start + NA_COLS)
    dc = jnp.clip(kc - qc + NA_COLS - 1, 0, 2 * NA_COLS - 2)
    hit = jnp.logical_and(c == dc, col_in)
    right = (pos & GRID_W) != 0
    r1, r2, r3 = _split3(rpb_ref[...] * ATTN_LOG2E)

    def placed(side):
        sel = _onehot(jnp.logical_and(hit, right == side))
        return _dot(r1, sel) + _dot(r2, sel) + _dot(r3, sel)

    left, rght = placed(False), placed(True)
    rows = lambda v, dr: v[dr * n_heads:(dr + 1) * n_heads]
    for k, (dl, dr) in enumerate(kinds):
        shown = col_in[0:1, :]
        if dl is None or dr is None:
            side_ok = jnp.zeros_like(shown) if dl is None and dr is None else (right[0:1, :] == (dl is None))
            shown = jnp.logical_and(shown, side_ok)
        parts = ([rows(left, dl)] if dl is not None else []) + ([rows(rght, dr)] if dr is not None else [])
        val = sum(parts) if parts else jnp.zeros((n_heads, npos), F32)
        o_ref[k] = jnp.where(shown, val, NEG)


def _na_bias_plan():
    ndr = 2 * NA_ROWS - 1
    assert NA_QROWS >= NA_ROWS // 2 and NA_HALO == NA_ROWS // 2 and (NA_QROWS + 2 * NA_HALO) % 2 == 0
    ok = lambda dr: dr if dr is not None and 0 <= dr < ndr else None
    kinds = [(ok(dr), ok(dr + 1)) for dr in range(-1, ndr)]
    first_key_row = (lambda rq: NA_HALO, lambda rq: rq, lambda rq: NA_QROWS + NA_HALO - NA_ROWS)
    table = []
    for first in first_key_row:
        for rq in range(NA_QROWS):
            for kp in range((NA_QROWS + 2 * NA_HALO) // 2):
                drs = []
                for kk in (2 * kp, 2 * kp + 1):
                    visible = first(rq) <= kk < first(rq) + NA_ROWS
                    drs.append(kk - rq + NA_ROWS - 1 - NA_HALO if visible else None)
                kind = (drs[0], drs[1])
                if kind not in kinds:
                    kinds.append(kind)
                table.append(kinds.index(kind))
    return kinds, table


def _na_bias_tiles(rpb):
    nh, ndr, ndc = rpb.shape
    nc = 32
    kinds, table = _na_bias_plan()
    nk = len(kinds)
    rows = jnp.pad(jnp.transpose(rpb, (1, 0, 2)), ((0, 0), (0, 0), (0, nc - ndc))).reshape(ndr * nh, nc)
    npos = GRID_W * 2 * GRID_W
    tiles = pl.pallas_call(
        functools.partial(_na_bias_kernel, kinds=tuple(kinds), n_heads=nh),
        out_shape=jax.ShapeDtypeStruct((nk, nh, npos), F32),
        compiler_params=pltpu.CompilerParams(vmem_limit_bytes=VMEM_LIMIT_BYTES),
        name="na_bias",
    )(rows)
    return tiles.reshape(nk, nh, GRID_W, 2 * GRID_W), jnp.asarray(table, jnp.int32)


def neighbourhood_attention(qkv, k_ctx, v_ctx, rpb, n_sample, sample_len):
    d = D_MODEL
    m = NA_QROWS * GRID_W
    nblk = sample_len // m
    assert nblk >= 3
    bias, table = _na_bias_tiles(rpb)

    def blk(col, off):
        return pl.BlockSpec((m, d), lambda b, j, tbl: (b * nblk + jnp.clip(j + off, 0, nblk - 1), col))

    return pl.pallas_call(
        _na_kernel,
        grid_spec=pltpu.PrefetchScalarGridSpec(
            num_scalar_prefetch=1,
            grid=(n_sample, nblk),
            in_specs=[
                blk(0, 0), blk(1, -1), blk(1, 0), blk(1, 1), blk(2, -1), blk(2, 0), blk(2, 1),
                pl.BlockSpec((1,) + k_ctx.shape[1:], lambda b, j, tbl: (b, 0, 0)),
                pl.BlockSpec((1,) + v_ctx.shape[1:], lambda b, j, tbl: (b, 0, 0)),
                pl.BlockSpec(bias.shape, lambda b, j, tbl: (0, 0, 0, 0)),
            ],
            out_specs=pl.BlockSpec((m, d), lambda b, j, tbl: (b * nblk + j, 0)),
        ),
        out_shape=jax.ShapeDtypeStruct((n_sample * sample_len, d), BF16),
        compiler_params=_cparams("arbitrary", "arbitrary"),
        name="neighbourhood_attention",
    )(table, qkv, qkv, qkv, qkv, qkv, qkv, qkv, k_ctx, v_ctx, bias)


MOE_ROW_TILE = 512
SC_CORES = 2
SC_SUBCORES = 16
SC_GATHER_CHUNK = 128

def _route_tokens(u, wt, carry):
    w1, w2, w3 = _split3(wt)
    lg = _dot_nt(w1, u) + _dot_nt(w2, u) + _dot_nt(w3, u)
    ne, nt = lg.shape
    e = lax.broadcasted_iota(jnp.int32, lg.shape, 0)
    m1 = jnp.max(lg, axis=0, keepdims=True)
    i1 = jnp.min(jnp.where(lg == m1, e, ne), axis=0, keepdims=True)
    sel1 = e == i1
    lg2 = jnp.where(sel1, -jnp.inf, lg)
    m2 = jnp.max(lg2, axis=0, keepdims=True)
    i2 = jnp.min(jnp.where(lg2 == m2, e, ne), axis=0, keepdims=True)
    sel2 = e == i2
    ex = jnp.exp(m2 - m1)
    wa = 1.0 / (1.0 + ex)
    wb = ex / (1.0 + ex)
    comb = jnp.where(sel1, wa, jnp.where(sel2, wb, 0.0))
    assign = jnp.logical_or(sel1, sel2)
    a = jnp.where(assign, 1.0, 0.0)
    row = lax.broadcasted_iota(jnp.int32, (nt, nt), 0)
    col = lax.broadcasted_iota(jnp.int32, (nt, nt), 1)
    tri = _onehot(row < col)
    rank = _dot(a.astype(BF16), tri) + carry
    pos = jnp.where(assign, rank, -1.0)
    return comb, pos, carry + jnp.sum(a, axis=1, keepdims=True)


def _moe_plan(totals, pos, comb, n_tok):
    ne = totals.shape[0]
    tm = MOE_ROW_TILE
    n_tiles = (2 * n_tok) // tm + ne
    tiles_per = (totals + tm - 1) // tm
    tile_end = jnp.cumsum(tiles_per)
    n_used = tile_end[-1]
    base_rows = (tile_end - tiles_per) * tm
    ti = jnp.arange(n_tiles, dtype=jnp.int32)
    tile_expert = jnp.minimum(jnp.sum((ti[:, None] >= tile_end[None, :]).astype(jnp.int32), axis=1), ne - 1)
    last_expert = tile_expert[jnp.maximum(n_used - 1, 0)]
    tile_expert = jnp.where(ti < n_used, tile_expert, last_expert)
    routed = pos >= 0.0
    row = base_rows[:, None] + pos.astype(jnp.int32)
    row_a = jnp.min(jnp.where(routed, row, n_tiles * tm), axis=0)
    row_b = jnp.max(jnp.where(routed, row, -1), axis=0)
    w_a = jnp.sum(jnp.where(jnp.logical_and(routed, row == row_a[None, :]), comb, 0.0), axis=0)
    w_b = jnp.sum(jnp.where(jnp.logical_and(routed, row == row_b[None, :]), comb, 0.0), axis=0)
    tile_rows = jnp.clip(totals[tile_expert] - (ti - (tile_end - tiles_per)[tile_expert]) * tm, 0, tm)
    tile_rows = jnp.where(ti < n_used, tile_rows, 0)
    return dict(n_rows=n_tiles * tm, tile_expert=tile_expert.astype(jnp.int32),
                n_used=n_used.reshape(1).astype(jnp.int32), tile_rows=tile_rows.astype(jnp.int32),
                token_rows=jnp.concatenate([row_a, row_b]).astype(jnp.int32),
                token_weights=jnp.concatenate([w_a[None], w_b[None], jnp.zeros((6, n_tok), F32)], axis=0))


def _pack_bf16_pairs(x):
    half = x.shape[1] // 2
    bits = pltpu.bitcast(x.astype(BF16).astype(F32), jnp.int32)
    return (bits[:, :half] & jnp.int32(-65536)) | lax.shift_right_logical(bits[:, half:], jnp.int32(16))


def _unpack_bf16_pairs(p):
    hi = pltpu.bitcast(p & jnp.int32(-65536), F32)
    lo = pltpu.bitcast(lax.shift_left(p, jnp.int32(16)), F32)
    return jnp.concatenate([hi, lo], axis=1)


def sc_gather_rows(table, idx):
    _, w = table.shape
    b = idx.shape[0]
    workers = SC_CORES * SC_SUBCORES
    ch = SC_GATHER_CHUNK
    n_chunks = b // (workers * ch)
    assert n_chunks * workers * ch == b
    mesh = plsc.VectorSubcoreMesh(core_axis_name="c", subcore_axis_name="s", num_cores=SC_CORES,
                                  num_subcores=SC_SUBCORES)

    @functools.partial(
        pl.kernel, mesh=mesh, out_type=jax.ShapeDtypeStruct((b, w), table.dtype),
        scratch_types=[pltpu.VMEM((ch,), jnp.int32), pltpu.VMEM((ch, w), table.dtype), pltpu.SemaphoreType.DMA])
    def gather(table_hbm, idx_hbm, out_hbm, idx_v, rows_v, sem):
        worker = lax.axis_index("s") * SC_CORES + lax.axis_index("c")

        @pl.loop(0, n_chunks)
        def _(c):
            base = (worker * n_chunks + c) * ch
            pltpu.sync_copy(idx_hbm.at[pl.ds(base, ch)], idx_v)
            pltpu.async_copy(table_hbm.at[idx_v], rows_v, sem).wait()
            pltpu.sync_copy(rows_v, out_hbm.at[pl.ds(base, ch)])

    return gather(table, idx)


def sc_scatter_rows(rows, idx, n_out):
    t, w = rows.shape
    copies = idx.shape[0] // t
    workers = SC_CORES * SC_SUBCORES
    ch = SC_GATHER_CHUNK
    n_chunks = t // (workers * ch)
    assert n_chunks * workers * ch == t and copies * t == idx.shape[0]
    mesh = plsc.VectorSubcoreMesh(core_axis_name="c", subcore_axis_name="s", num_cores=SC_CORES,
                                  num_subcores=SC_SUBCORES)

    @functools.partial(
        pl.kernel, mesh=mesh, out_type=jax.ShapeDtypeStruct((n_out, w), rows.dtype),
        scratch_types=[pltpu.VMEM((ch,), jnp.int32), pltpu.VMEM((ch, w), rows.dtype), pltpu.SemaphoreType.DMA])
    def scatter(rows_hbm, idx_hbm, out_hbm, idx_v, rows_v, sem):
        worker = lax.axis_index("s") * SC_CORES + lax.axis_index("c")

        @pl.loop(0, n_chunks)
        def _(c):
            base = (worker * n_chunks + c) * ch
            pltpu.sync_copy(rows_hbm.at[pl.ds(base, ch)], rows_v)
            for k in range(copies):
                pltpu.sync_copy(idx_hbm.at[pl.ds(k * t + base, ch)], idx_v)
                pltpu.async_copy(rows_v, out_hbm.at[idx_v], sem).wait()

    return scatter(rows, idx)


def _gffn_kernel(te_ref, nu_ref, nr_ref, x_ref, wi_ref, wo_ref, y_ref):
    i = pl.program_id(0)
    used = i < nu_ref[0]

    @pl.when(used)
    def _():
        packed = x_ref[...]
        row = lax.broadcasted_iota(jnp.int32, packed.shape, 0)
        x = _unpack_bf16_pairs(jnp.where(row < nr_ref[i], packed, 0)).astype(BF16)
        y = _swiglu_tile(x, lambda c: wi_ref[0, :, c], lambda r: wo_ref[0, r, :], wo_ref.shape[1])
        y_ref[...] = _pack_bf16_pairs(y)

    @pl.when(jnp.logical_not(used))
    def _():
        y_ref[...] = jnp.zeros_like(y_ref)


def moe_grouped_ffn(xs, w_in, w_out, plan):
    nr, half = xs.shape
    ne, dff, d = w_out.shape
    tm = MOE_ROW_TILE
    return pl.pallas_call(
        _gffn_kernel,
        grid_spec=pltpu.PrefetchScalarGridSpec(
            num_scalar_prefetch=3,
            grid=(nr // tm,),
            in_specs=[
                pl.BlockSpec((tm, half), lambda i, te, nu, tr: (jnp.minimum(i, nu[0] - 1), 0)),
                pl.BlockSpec((1, d, 2 * dff), lambda i, te, nu, tr: (te[i], 0, 0)),
                pl.BlockSpec((1, dff, d), lambda i, te, nu, tr: (te[i], 0, 0)),
            ],
            out_specs=pl.BlockSpec((tm, half), lambda i, te, nu, tr: (i, 0)),
        ),
        out_shape=jax.ShapeDtypeStruct((nr, half), jnp.int32),
        compiler_params=_cparams("arbitrary"),
        name="moe_grouped_ffn",
    )(plan["tile_expert"], plan["n_used"], plan["tile_rows"], xs, w_in, w_out)


def _combine_kernel(ya_ref, yb_ref, w_ref, x_ref, mod_ref, lng_ref, lnb_ref, op_ref, os_ref, *, n_prompt_tiles):
    i = pl.program_id(0)
    w = w_ref[...].T
    moe = w[:, 0:1] * _unpack_bf16_pairs(ya_ref[...]) + w[:, 1:2] * _unpack_bf16_pairs(yb_ref[...])
    z = DEEPNORM_ALPHA * x_ref[...] + mod_ref[0, 5:6, :] * moe
    xn = _layer_norm(z, lng_ref[...], lnb_ref[...])

    @pl.when(i < n_prompt_tiles)
    def _():
        op_ref[...] = xn

    @pl.when(i >= n_prompt_tiles)
    def _():
        os_ref[...] = xn


def moe_combine(y_tok, weights, x, mod, ln_g, ln_b, n_prompt_tok, sample_len):
    t, d = x.shape
    tt = TOKEN_TILE
    nt = t // tt
    npt = n_prompt_tok // tt
    seg = functools.partial(_seg_of_tile, tile=tt, n_prompt_tok=n_prompt_tok, sample_len=sample_len)
    return pl.pallas_call(
        functools.partial(_combine_kernel, n_prompt_tiles=npt),
        grid=(nt,),
        in_specs=[
            pl.BlockSpec((tt, d // 2), lambda i: (i, 0)),
            pl.BlockSpec((tt, d // 2), lambda i: (nt + i, 0)),
            pl.BlockSpec((8, tt), lambda i: (0, i)),
            pl.BlockSpec((tt, d), lambda i: (i, 0)),
            pl.BlockSpec((1, 6, d), lambda i: (seg(i), 0, 0)),
            pl.BlockSpec((1, d), lambda i: (0, 0)),
            pl.BlockSpec((1, d), lambda i: (0, 0)),
        ],
        out_specs=[
            pl.BlockSpec((tt, d), lambda i: (jnp.minimum(i, npt - 1), 0)),
            pl.BlockSpec((tt, d), lambda i: (jnp.maximum(i - npt, 0), 0)),
        ],
        out_shape=[jax.ShapeDtypeStruct((n_prompt_tok, d), F32), jax.ShapeDtypeStruct((t - n_prompt_tok, d), F32)],
        compiler_params=_cparams("arbitrary"),
        name="moe_combine",
    )(y_tok, y_tok, weights, x, mod, ln_g.reshape(1, d), ln_b.reshape(1, d))


def _layer1(x, u, mod1, cache_k, cache_v, ln_g, ln_b, w_qkv, rpb, w_out, w_router, moe_w_in, moe_w_out,
            n_prompt, prompt_len, n_sample, sample_len):
    n_prompt_tok = n_prompt * prompt_len
    d = D_MODEL
    col_scale = jnp.where(jnp.arange(3 * d) < d, ATTN_Q_SCALE, 1.0).astype(F32)
    w_qkv = (w_qkv * col_scale[None, :]).astype(BF16)
    qkv_p, k_p, v_p = na_qkv(u, w_qkv, 0, n_prompt_tok, with_kv=True)
    qkv_s, = na_qkv(u, w_qkv, n_prompt_tok, u.shape[0] - n_prompt_tok, with_kv=False)
    attn_p = context_attention(qkv_p, n_prompt, prompt_len)
    k_ctx = cache_k.reshape(n_sample, -1, d).astype(BF16)
    v_ctx = cache_v.reshape(n_sample, -1, d).astype(BF16)
    attn_s = neighbourhood_attention(qkv_s, k_ctx, v_ctx, rpb, n_sample, sample_len)
    moe_w_in, moe_w_out = moe_w_in.astype(BF16), moe_w_out.astype(BF16)
    x1, u1_packed, comb, pos, totals = mixer_outproj_router(
        attn_p, attn_s, x, mod1, w_out.astype(BF16), ln_g[0], ln_b[0], w_router, sample_len,
        run_after=(moe_w_in, moe_w_out))
    plan = _moe_plan(totals[:, 0].astype(jnp.int32), pos, comb, x.shape[0])
    xs = sc_scatter_rows(u1_packed, plan["token_rows"], plan["n_rows"])
    y = moe_grouped_ffn(xs, moe_w_in, moe_w_out, plan)
    y_tok = sc_gather_rows(y, plan["token_rows"])
    y_p, y_s = moe_combine(y_tok, plan["token_weights"], x1, mod1, ln_g[1], ln_b[1], n_prompt_tok, sample_len)
    return y_p, y_s, k_p, v_p


def kernel(x_prompt, x_sample, state_mlstm_C, state_mlstm_n, state_mlstm_m, cache_na_k, cache_na_v, c, c_ctx,
           ada_w, ada_b, ln_g, ln_b, mlstm_w_in, mlstm_b_gates, mlstm_norm_g, mlstm_w_out, na_w_qkv, na_rpb,
           na_w_out, ffn_w_in, ffn_w_out, moe_w_router, moe_w_in, moe_w_out):
    n_prompt, prompt_len, d = x_prompt.shape
    n_sample, sample_len, _ = x_sample.shape
    assert d == D_MODEL and prompt_len == MLSTM_CHUNK and sample_len % MLSTM_CHUNK == 0
    assert ada_w.shape[0] == DEPTH == 2
    mod = _modulation_tables(c, c_ctx, ada_w, ada_b)
    x2, u2, new_c, new_n, new_m = _layer0(
        x_prompt.reshape(-1, d), x_sample.reshape(-1, d), mod[0], mod[1], state_mlstm_C[:, 0], state_mlstm_n[:, 0], state_mlstm_m[:, 0], ln_g[0], ln_b[0],
        mlstm_w_in[0], mlstm_b_gates[0], mlstm_norm_g[0], mlstm_w_out[0], ffn_w_in[0], ffn_w_out[0],
        n_prompt, prompt_len, n_sample, sample_len)
    y_p, y_s, k_p, v_p = _layer1(
        x2, u2, mod[1], cache_na_k[:, 0], cache_na_v[:, 0], ln_g[1], ln_b[1], na_w_qkv[0], na_rpb[0], na_w_out[0],
        moe_w_router[0], moe_w_in[0], moe_w_out[0], n_prompt, prompt_len, n_sample, sample_len)
    kv_shape = (n_prompt, 1, prompt_len, NA_HEADS, NA_DH)
    return (y_p.reshape(x_prompt.shape), y_s.reshape(x_sample.shape), new_c, new_n, new_m,
            k_p.reshape(kv_shape), v_p.reshape(kv_shape))
```

```python
import functools
import math

import jax
import jax.numpy as jnp
from jax import lax
from jax.experimental import pallas as pl
from jax.experimental.pallas import tpu as pltpu
from jax.experimental.pallas import tpu_sc as plsc

F32 = jnp.float32
BF16 = jnp.bfloat16

D_MODEL = 1024
DEPTH = 2
GRID_W = 64
M_HEADS = 4
M_DK = 128
M_DV = 256
M_HK = M_HEADS * M_DK
NA_HEADS = 16
NA_DH = 64
NA_ROWS = 8
NA_COLS = 16
DEEPNORM_ALPHA = (2.0 * DEPTH) ** 0.25
LN_EPS = 1e-5
NEG = -1e30

VMEM_LIMIT_BYTES = 56 * 1024 * 1024

MLSTM_CHUNK = 256
TOKEN_TILE = 512
FFN_TOKEN_TILE = 512
FFN_SUB = 256
NA_QROWS = 4
NA_HALO = 4


def _cparams(*sem):
    return pltpu.CompilerParams(dimension_semantics=sem, vmem_limit_bytes=VMEM_LIMIT_BYTES)


def _dot(a, b):
    return jnp.dot(a, b, preferred_element_type=F32)


def _dot_nt(a, b):
    return lax.dot_general(a, b, (((1,), (1,)), ((), ())), preferred_element_type=F32)


def _onehot(mask):
    return jnp.where(mask, 1.0, 0.0).astype(BF16)


def _split3(x):
    hi = x.astype(BF16)
    r = x - hi.astype(F32)
    mid = r.astype(BF16)
    lo = (r - mid.astype(F32)).astype(BF16)
    return hi, mid, lo


def _layer_norm(z, g, b):
    mu = jnp.mean(z, axis=-1, keepdims=True)
    zc = z - mu
    var = jnp.mean(zc * zc, axis=-1, keepdims=True)
    return zc * lax.rsqrt(var + LN_EPS) * g + b


def _log_sigmoid(x):
    return jnp.minimum(x, 0.0) - jnp.log(1.0 + jnp.exp(-jnp.abs(x)))


def _seg_of_tile(i, tile, n_prompt_tok, sample_len):
    tok = i * tile
    return jnp.where(tok < n_prompt_tok, 0, 1 + (tok - n_prompt_tok) // sample_len)


def _adaln_kernel(c_ref, w_ref, b_ref, o_ref):
    c = c_ref[...]
    a = (c * jax.nn.sigmoid(c)).astype(BF16)
    o_ref[0] = _dot(a, w_ref[0].astype(BF16)) + b_ref[0]


def adaln(cvec, ada_w, ada_b):
    depth, d, n = ada_w.shape
    tn = 1536
    return pl.pallas_call(
        _adaln_kernel,
        grid=(depth, n // tn),
        in_specs=[
            pl.BlockSpec((8, d), lambda l, j: (0, 0)),
            pl.BlockSpec((1, d, tn), lambda l, j: (l, 0, j)),
            pl.BlockSpec((1, 1, tn), lambda l, j: (l, 0, j)),
        ],
        out_specs=pl.BlockSpec((1, 8, tn), lambda l, j: (l, 0, j)),
        out_shape=jax.ShapeDtypeStruct((depth, 8, n), F32),
        compiler_params=_cparams("arbitrary", "arbitrary"),
        name="adaln",
    )(cvec, ada_w, ada_b.reshape(depth, 1, n))


def _inproj_kernel(xp_ref, xs_ref, mod_ref, w_ref, wg_ref, wgt_ref, bg_ref, bgt_ref,
                   qkv_ref, o_ref, g_ref, gt_ref, *, n_prompt_tiles):
    x = jnp.where(pl.program_id(0) < n_prompt_tiles, xp_ref[...], xs_ref[...])
    u = (x * (1.0 + mod_ref[0, 1:2, :]) + mod_ref[0, 0:1, :]).astype(BF16)
    nqkv = qkv_ref.shape[1]
    half = nqkv // 2
    g = _dot(u, wg_ref[...]) + bg_ref[...]
    gt = _dot_nt(wgt_ref[...], u) + bgt_ref[...]
    p0 = _dot_nt(u, w_ref[:half, :])
    col = lax.broadcasted_iota(jnp.int32, g.shape, 1)
    g_ref[...] = jnp.where(((col >> 2) & 1) == 1, _log_sigmoid(g), g)
    p1 = _dot_nt(u, w_ref[half:nqkv, :])
    qkv_ref[:, :half] = p0.astype(BF16)
    row = lax.broadcasted_iota(jnp.int32, gt.shape, 0)
    gt_ref[...] = jnp.where(((row >> 2) & 1) == 1, _log_sigmoid(gt), gt)
    p2 = _dot_nt(u, w_ref[nqkv:, :])
    qkv_ref[:, half:] = p1.astype(BF16)
    o_ref[...] = p2


def _split_token_specs(tile, d, n_prompt_tiles):
    return [pl.BlockSpec((tile, d), lambda i: (jnp.minimum(i, n_prompt_tiles - 1), 0)),
            pl.BlockSpec((tile, d), lambda i: (jnp.maximum(i - n_prompt_tiles, 0), 0))]


def mlstm_inproj(xp, xs, mod, w_main, w_gate, w_gate_t, b_gate, sample_len):
    n_prompt_tok, d = xp.shape
    t = n_prompt_tok + xs.shape[0]
    n_main = w_main.shape[0]
    n_qkv = 2 * M_HK + D_MODEL
    ng = w_gate.shape[1]
    tt = TOKEN_TILE
    npt = n_prompt_tok // tt
    seg = functools.partial(_seg_of_tile, tile=tt, n_prompt_tok=n_prompt_tok, sample_len=sample_len)
    return pl.pallas_call(
        functools.partial(_inproj_kernel, n_prompt_tiles=npt),
        grid=(t // tt,),
        in_specs=_split_token_specs(tt, d, npt) + [
            pl.BlockSpec((1, 6, d), lambda i: (seg(i), 0, 0)),
            pl.BlockSpec((n_main, d), lambda i: (0, 0)),
            pl.BlockSpec((d, ng), lambda i: (0, 0)),
            pl.BlockSpec((ng, d), lambda i: (0, 0)),
            pl.BlockSpec((1, ng), lambda i: (0, 0)),
            pl.BlockSpec((ng, 1), lambda i: (0, 0)),
        ],
        out_specs=[
            pl.BlockSpec((tt, n_qkv), lambda i: (i, 0)),
            pl.BlockSpec((tt, n_main - n_qkv), lambda i: (i, 0)),
            pl.BlockSpec((tt, ng), lambda i: (i, 0)),
            pl.BlockSpec((ng, tt), lambda i: (0, i)),
        ],
        out_shape=[
            jax.ShapeDtypeStruct((t, n_qkv), BF16),
            jax.ShapeDtypeStruct((t, n_main - n_qkv), F32),
            jax.ShapeDtypeStruct((t, ng), F32),
            jax.ShapeDtypeStruct((ng, t), F32),
        ],
        compiler_params=_cparams("arbitrary"),
        name="mlstm_inproj",
    )(xp, xs, mod, w_main, w_gate, w_gate_t, b_gate.reshape(1, ng), b_gate.reshape(ng, 1))


def _mlstm_gate_sums(g, gt, backward):
    L = g.shape[0]
    row = lax.broadcasted_iota(jnp.int32, (L, L), 0)
    col = lax.broadcasted_iota(jnp.int32, (L, L), 1)
    lower = col <= row
    upper = col >= row
    tri_col = _onehot(upper if backward else lower)
    tri_row = _onehot(lower if backward else upper)
    ghi, gmid, glo = _split3(g)
    b_cols = _dot(tri_col, ghi) + _dot(tri_col, gmid) + _dot(tri_col, glo)
    thi, tmid, tlo = _split3(gt)
    b_rows = _dot(thi, tri_row) + _dot(tmid, tri_row) + _dot(tlo, tri_row)
    return b_cols, b_rows


def _mlstm_heads(items):
    scale = M_DK ** -0.5
    log_scale = math.log(scale)
    L = items[0][0].shape[0]
    row = lax.broadcasted_iota(jnp.int32, (L, L), 0)
    col = lax.broadcasted_iota(jnp.int32, (L, L), 1)
    masks = {False: col <= row, True: col >= row}
    n = range(len(items))
    qh, kh, vh, li_row, lf_row, b_row, b_col, backward, state = zip(*items)
    has_state = [st is not None for st in state]
    m_prev = [st[2] if st is not None else 0.0 for st in state]

    qk = [_dot_nt(qh[i], kh[i]) for i in n]
    qc = [_dot(qh[i], state[i][0].astype(BF16)) if has_state[i] else None for i in n]
    qn = [_dot_nt(qh[i], state[i][1].astype(BF16))[:, 0:1] if has_state[i] else None for i in n]
    total = [jnp.sum(lf_row[i], axis=1, keepdims=True) for i in n]
    d = [jnp.where(masks[backward[i]], b_col[i] + (li_row[i] - b_row[i] + log_scale), NEG) for i in n]
    inter = [b_col[i] + m_prev[i] if has_state[i] else b_col[i] for i in n]
    m_t = [jnp.maximum(inter[i], jnp.max(d[i], axis=1, keepdims=True) - log_scale) for i in n]
    p = [jnp.exp(d[i] - m_t[i]) for i in n]
    s = [qk[i] * p[i] for i in n]
    den = [jnp.sum(s[i], axis=1, keepdims=True) for i in n]
    num = [_dot(s[i].astype(BF16), vh[i]) for i in n]
    a = [jnp.exp(inter[i] - m_t[i]) * scale if has_state[i] else None for i in n]
    num = [a[i] * qc[i] + num[i] if has_state[i] else num[i] for i in n]
    den = [a[i] * qn[i] + den[i] if has_state[i] else den[i] for i in n]
    hh = [num[i] / jnp.maximum(jnp.abs(den[i]), jnp.exp(-m_t[i])) for i in n]

    g_row = [total[i] - b_row[i] + li_row[i] for i in n]
    m_new = [jnp.maximum(total[i] + m_prev[i], jnp.max(g_row[i], axis=1, keepdims=True)) for i in n]
    ws_row = [jnp.exp(g_row[i] - m_new[i]) for i in n]
    kt = [kh[i].astype(F32).T for i in n]
    c_new = [_dot((kt[i] * ws_row[i]).astype(BF16), vh[i]) for i in n]
    n_new = [_dot(jnp.broadcast_to(ws_row[i], (8, L)).astype(BF16), kh[i]) for i in n]
    a0 = [jnp.exp(total[i] + m_prev[i] - m_new[i]) if has_state[i] else None for i in n]
    c_new = [a0[i] * state[i][0] + c_new[i] if has_state[i] else c_new[i] for i in n]
    n_new = [a0[i] * state[i][1] + n_new[i] if has_state[i] else n_new[i] for i in n]
    return hh, c_new, n_new, m_new


def _head_norm_gate(ht, norm_g, ogate):
    mu = jnp.mean(ht, axis=-1, keepdims=True)
    hc = ht - mu
    var = jnp.mean(hc * hc, axis=-1, keepdims=True)
    return (hc * lax.rsqrt(var + LN_EPS) * norm_g * jax.nn.sigmoid(ogate)).astype(BF16)


def _head_operands(q_ref, k_ref, v_ref, h):
    return (q_ref[:, h * M_DK:(h + 1) * M_DK], k_ref[:, h * M_DK:(h + 1) * M_DK], v_ref[:, h * M_DV:(h + 1) * M_DV])


def _gate_operands(gt, b_rows, b_cols, direction, h):
    ci = direction * 8 + h
    cf = direction * 8 + 4 + h
    return gt[ci:ci + 1, :], gt[cf:cf + 1, :], b_rows[cf:cf + 1, :], b_cols[:, cf:cf + 1]


def _mlstm_prompt_kernel(q_ref, k_ref, v_ref, g_ref, gt_ref, o_ref, ng_ref, a_ref, cf_ref, nf_ref, mf_ref):
    g, gt = g_ref[...], gt_ref[...]
    sums = [_mlstm_gate_sums(g, gt, direction == 1) for direction in (0, 1)]
    keys = [(direction, h) for direction in (0, 1) for h in range(M_HEADS)]
    items = [_head_operands(q_ref, k_ref, v_ref, h)
             + _gate_operands(gt, sums[direction][1], sums[direction][0], direction, h) + (direction == 1, None)
             for direction, h in keys]
    hh, c_new, n_new, m_new = _mlstm_heads(items)
    for i, (direction, h) in enumerate(keys):
        cf_ref[0, 0, direction, h] = c_new[i]
        nf_ref[0, 0, direction, h] = n_new[i]
        mf_ref[0, 0, direction, h] = jnp.broadcast_to(m_new[i], mf_ref.shape[-2:])
    for h in range(M_HEADS):
        sl = slice(h * M_DV, (h + 1) * M_DV)
        a_ref[:, sl] = _head_norm_gate(hh[h] + hh[M_HEADS + h], ng_ref[:, sl], o_ref[:, sl])


def _mlstm_sample_kernel(*refs, direction, n_units):
    backward = direction == 1
    refs = list(refs)
    unit_refs = [tuple(refs.pop(0) for _ in range(5)) for _ in range(n_units)]
    c0_ref, n0_ref, m0_ref = refs.pop(0), refs.pop(0), refs.pop(0)
    if backward:
        hprev_ref = refs.pop(0)
        o_refs = [refs.pop(0) for _ in range(n_units)]
        ng_ref = refs.pop(0)
    out_ref, c_scr, n_scr, m_scr = refs

    @pl.when(pl.program_id(0) == 0)
    def _():
        c_scr[...] = c0_ref[...]
        n_scr[...] = n0_ref[...]
        m_scr[...] = m0_ref[...]

    keys, items = [], []
    for u, (q_ref, k_ref, v_ref, g_ref, gt_ref) in enumerate(unit_refs):
        gt = gt_ref[...]
        b_cols, b_rows = _mlstm_gate_sums(g_ref[...], gt, backward)
        for h in range(M_HEADS):
            state = (c_scr[u, h], n_scr[u, h], m_scr[u, h][0:1, 0:1])
            keys.append((u, h))
            items.append(_head_operands(q_ref, k_ref, v_ref, h) + _gate_operands(gt, b_rows, b_cols, direction, h)
                         + (backward, state))
    hh, c_new, n_new, m_new = _mlstm_heads(items)
    for i, (u, h) in enumerate(keys):
        c_scr[u, h] = c_new[i]
        n_scr[u, h] = n_new[i]
        m_scr[u, h] = jnp.broadcast_to(m_new[i], m_scr.shape[-2:])
        sl = slice(h * M_DV, (h + 1) * M_DV)
        if backward:
            out_ref[u, :, sl] = _head_norm_gate(hprev_ref[u, :, sl] + hh[i], ng_ref[:, sl], o_refs[u][:, sl])
        else:
            out_ref[u, :, sl] = hh[i]


def _mlstm_chunk_specs(block_of):
    L = MLSTM_CHUNK
    return [
        pl.BlockSpec((L, M_HK), lambda s: (block_of(s), 0)),
        pl.BlockSpec((L, M_HK), lambda s: (block_of(s), 1)),
        pl.BlockSpec((L, D_MODEL), lambda s: (block_of(s), 1)),
        pl.BlockSpec((L, 16), lambda s: (block_of(s), 0)),
        pl.BlockSpec((16, L), lambda s: (0, block_of(s))),
    ]


def mlstm_prompt(qkv, g, gt, ogate, norm_g, n_prompt):
    L = MLSTM_CHUNK
    state_blk = lambda *tail: pl.BlockSpec((1, 1, 2, M_HEADS) + tail, lambda s: (s, 0, 0, 0, 0, 0))
    return pl.pallas_call(
        _mlstm_prompt_kernel,
        grid=(n_prompt,),
        in_specs=_mlstm_chunk_specs(lambda s: s) + [
            pl.BlockSpec((L, D_MODEL), lambda s: (s, 0)),
            pl.BlockSpec((1, D_MODEL), lambda s: (0, 0)),
        ],
        out_specs=[pl.BlockSpec((L, D_MODEL), lambda s: (s, 0)),
                   state_blk(M_DK, M_DV), state_blk(8, M_DK), state_blk(8, 128)],
        out_shape=[
            jax.ShapeDtypeStruct((n_prompt * L, D_MODEL), BF16),
            jax.ShapeDtypeStruct((n_prompt, 1, 2, M_HEADS, M_DK, M_DV), F32),
            jax.ShapeDtypeStruct((n_prompt, 1, 2, M_HEADS, 8, M_DK), F32),
            jax.ShapeDtypeStruct((n_prompt, 1, 2, M_HEADS, 8, 128), F32),
        ],
        compiler_params=_cparams("arbitrary"),
        name="mlstm_prompt",
    )(qkv, qkv, qkv, g, gt, ogate, norm_g.reshape(1, D_MODEL))


def mlstm_sample(direction, qkv, g, gt, c0, n0, m0, first_block, chunks_per_sample,
                 hprev=None, ogate=None, norm_g=None):
    L = MLSTM_CHUNK
    backward = direction == 1
    cps = chunks_per_sample
    n_units = c0.shape[0]
    pos = (lambda s: cps - 1 - s) if backward else (lambda s: s)
    unit_block = [functools.partial(lambda s, u: first_block + u * cps + pos(s), u=u) for u in range(n_units)]
    whole = lambda a: pl.BlockSpec(a.shape, lambda s: (0,) * a.ndim)
    in_specs, args = [], []
    for u in range(n_units):
        in_specs += _mlstm_chunk_specs(unit_block[u])
        args += [qkv, qkv, qkv, g, gt]
    in_specs += [whole(c0), whole(n0), whole(m0)]
    args += [c0, n0, m0]
    if backward:
        in_specs += [pl.BlockSpec((n_units, L, D_MODEL), lambda s: (0, pos(s), 0))]
        in_specs += [pl.BlockSpec((L, D_MODEL), functools.partial(lambda s, u: (unit_block[u](s), 0), u=u))
                     for u in range(n_units)]
        in_specs += [pl.BlockSpec((1, D_MODEL), lambda s: (0, 0))]
        args += [hprev] + [ogate] * n_units + [norm_g.reshape(1, D_MODEL)]
    return pl.pallas_call(
        functools.partial(_mlstm_sample_kernel, direction=direction, n_units=n_units),
        grid=(cps,),
        in_specs=in_specs,
        out_specs=pl.BlockSpec((n_units, L, D_MODEL), lambda s: (0, pos(s), 0)),
        out_shape=jax.ShapeDtypeStruct((n_units, cps * L, D_MODEL), BF16 if backward else F32),
        scratch_shapes=[pltpu.VMEM(c0.shape, F32), pltpu.VMEM(n0.shape, F32), pltpu.VMEM(m0.shape, F32)],
        compiler_params=_cparams("arbitrary"),
        name="mlstm_sample_bwd" if backward else "mlstm_sample_fwd",
    )(*args)


def _outproj_router_kernel(ap_ref, as_ref, x_ref, mod_ref, w_ref, lng_ref, lnb_ref, wr_ref, *refs,
                           n_prompt_tiles):
    xo_ref, up_ref, comb_ref, pos_ref, total_ref, carry_scr = refs[-6:]
    i = pl.program_id(0)

    @pl.when(i == 0)
    def _():
        carry_scr[...] = jnp.zeros_like(carry_scr)

    a = jnp.where(i < n_prompt_tiles, ap_ref[...], as_ref[...])
    z = DEEPNORM_ALPHA * x_ref[...] + mod_ref[0, 2:3, :] * _dot(a, w_ref[...])
    xn = _layer_norm(z, lng_ref[...], lnb_ref[...])
    xo_ref[...] = xn
    u = xn * (1.0 + mod_ref[0, 4:5, :]) + mod_ref[0, 3:4, :]
    up_ref[...] = _pack_bf16_pairs(u)
    comb, pos, carry_new = _route_tokens(u.astype(BF16), wr_ref[...], carry_scr[:, 0:1])
    comb_ref[...] = comb
    pos_ref[...] = pos
    carry_scr[...] = jnp.broadcast_to(carry_new, carry_scr.shape)
    total_ref[...] = jnp.broadcast_to(carry_new, carry_scr.shape)


def mixer_outproj_router(a_p, a_s, x, mod, w, ln_g, ln_b, w_router, sample_len, run_after=()):
    t, d = x.shape
    ne = w_router.shape[1]
    tt = TOKEN_TILE
    npt = a_p.shape[0] // tt
    seg = functools.partial(_seg_of_tile, tile=tt, n_prompt_tok=a_p.shape[0], sample_len=sample_len)
    return pl.pallas_call(
        functools.partial(_outproj_router_kernel, n_prompt_tiles=npt),
        grid=(t // tt,),
        in_specs=_split_token_specs(tt, d, npt) + [
            pl.BlockSpec((tt, d), lambda i: (i, 0)),
            pl.BlockSpec((1, 6, d), lambda i: (seg(i), 0, 0)),
            pl.BlockSpec((d, d), lambda i: (0, 0)),
            pl.BlockSpec((1, d), lambda i: (0, 0)),
            pl.BlockSpec((1, d), lambda i: (0, 0)),
            pl.BlockSpec((ne, d), lambda i: (0, 0)),
        ] + [pl.BlockSpec(memory_space=pl.ANY)] * len(run_after),
        out_specs=[
            pl.BlockSpec((tt, d), lambda i: (i, 0)),
            pl.BlockSpec((tt, d // 2), lambda i: (i, 0)),
            pl.BlockSpec((ne, tt), lambda i: (0, i)),
            pl.BlockSpec((ne, tt), lambda i: (0, i)),
            pl.BlockSpec((ne, 128), lambda i: (0, 0)),
        ],
        out_shape=[
            jax.ShapeDtypeStruct((t, d), F32),
            jax.ShapeDtypeStruct((t, d // 2), jnp.int32),
            jax.ShapeDtypeStruct((ne, t), F32),
            jax.ShapeDtypeStruct((ne, t), F32),
            jax.ShapeDtypeStruct((ne, 128), F32),
        ],
        scratch_shapes=[pltpu.VMEM((ne, 128), F32)],
        compiler_params=_cparams("arbitrary"),
        name="mixer_outproj_router",
    )(a_p, a_s, x, mod, w, ln_g.reshape(1, d), ln_b.reshape(1, d), w_router.T, *run_after)


def _swiglu_tile(x, wi, wo, dff):
    sub = FFN_SUB
    proj = lambda j: (_dot(x, wi(slice(j * sub, (j + 1) * sub))), _dot(x, wi(slice(dff + j * sub, dff + (j + 1) * sub))))
    acts, cur = [], proj(0)
    for j in range(dff // sub):
        nxt = proj(j + 1) if (j + 1) * sub < dff else None
        gp, up = cur
        acts.append((gp * jax.nn.sigmoid(gp) * up).astype(BF16))
        cur = nxt
    return _dot(jnp.concatenate(acts, axis=1), wo(slice(0, dff)))


def _mixer_ffn_kernel(ap_ref, as_ref, xp_ref, xs_ref, mod_ref, modn_ref, wmix_ref, wi_ref, wo_ref, ln_ref,
                      xo_ref, uo_ref, *, n_prompt_tiles):
    is_prompt = pl.program_id(0) < n_prompt_tiles
    a = jnp.where(is_prompt, ap_ref[...], as_ref[...])
    x = jnp.where(is_prompt, xp_ref[...], xs_ref[...])
    z = DEEPNORM_ALPHA * x + mod_ref[0, 2:3, :] * _dot(a, wmix_ref[...])
    x1 = _layer_norm(z, ln_ref[0:1, :], ln_ref[1:2, :])
    u1 = (x1 * (1.0 + mod_ref[0, 4:5, :]) + mod_ref[0, 3:4, :]).astype(BF16)
    f = _swiglu_tile(u1, lambda c: wi_ref[:, c], lambda r: wo_ref[r, :], wo_ref.shape[0])
    x2 = _layer_norm(DEEPNORM_ALPHA * x1 + mod_ref[0, 5:6, :] * f, ln_ref[2:3, :], ln_ref[3:4, :])
    xo_ref[...] = x2
    uo_ref[...] = (x2 * (1.0 + modn_ref[0, 1:2, :]) + modn_ref[0, 0:1, :]).astype(BF16)


def _resident(shape):
    return pl.BlockSpec(shape, lambda i: (0,) * len(shape), pipeline_mode=pl.Buffered(1))


def mixer_ffn(a_p, a_s, xp, xs, mod, mod_next, w_mix, w_in, w_out, ln_g, ln_b, sample_len):
    n_prompt_tok, d = xp.shape
    t = n_prompt_tok + xs.shape[0]
    tm = FFN_TOKEN_TILE
    npt = n_prompt_tok // tm
    seg = functools.partial(_seg_of_tile, tile=tm, n_prompt_tok=n_prompt_tok, sample_len=sample_len)
    ln = jnp.stack([ln_g[0], ln_b[0], ln_g[1], ln_b[1]])
    return pl.pallas_call(
        functools.partial(_mixer_ffn_kernel, n_prompt_tiles=npt),
        grid=(t // tm,),
        in_specs=_split_token_specs(tm, d, npt) + _split_token_specs(tm, d, npt) + [
            pl.BlockSpec((1, 6, d), lambda i: (seg(i), 0, 0)),
            pl.BlockSpec((1, 6, d), lambda i: (seg(i), 0, 0)),
            _resident(w_mix.shape), _resident(w_in.shape), _resident(w_out.shape), _resident(ln.shape),
        ],
        out_specs=[pl.BlockSpec((tm, d), lambda i: (i, 0)), pl.BlockSpec((tm, d), lambda i: (i, 0))],
        out_shape=[jax.ShapeDtypeStruct((t, d), F32), jax.ShapeDtypeStruct((t, d), BF16)],
        compiler_params=_cparams("arbitrary"),
        name="mixer_ffn",
    )(a_p, a_s, xp, xs, mod, mod_next, w_mix, w_in, w_out, ln)


def _modulation_tables(c, c_ctx, ada_w, ada_b):
    n_s = c.shape[0]
    cvec = jnp.concatenate([c_ctx[None, :], c, jnp.zeros((8 - 1 - n_s, c.shape[1]), F32)], axis=0)
    mod = adaln(cvec, ada_w, ada_b)
    return mod[:, :1 + n_s, :].reshape(ada_w.shape[0], 1 + n_s, 6, c.shape[1])


def _layer0(xp, xs, mod0, mod1, state_c, state_n, state_m, ln_g, ln_b, w_in, b_gates, norm_g, w_out,
            ffn_w_in, ffn_w_out, n_prompt, prompt_len, n_sample, sample_len):
    n_prompt_tok = n_prompt * prompt_len
    n_main = 2 * M_HK + 2 * D_MODEL
    w_t = w_in.T
    qkv, ogate, g, gt = mlstm_inproj(xp, xs, mod0, w_t[:n_main].astype(BF16), w_t[n_main:].T.astype(BF16),
                                     w_t[n_main:].astype(BF16), b_gates.reshape(-1), sample_len)
    npc = n_prompt_tok // MLSTM_CHUNK
    cps = sample_len // MLSTM_CHUNK
    rep = lambda a: jnp.broadcast_to(a[..., None, :], a.shape[:-1] + (8, a.shape[-1]))
    n0 = rep(state_n)
    m0 = jnp.broadcast_to(state_m[..., None, None], state_m.shape + (8, 128))
    a_p, new_c, nf, mf = mlstm_prompt(qkv, g, gt, ogate, norm_g, n_prompt)
    hf = mlstm_sample(0, qkv, g, gt, state_c[:, 0], n0[:, 0], m0[:, 0], npc, cps)
    a_s = mlstm_sample(1, qkv, g, gt, state_c[:, 1], n0[:, 1], m0[:, 1], npc, cps,
                       hprev=hf, ogate=ogate, norm_g=norm_g)
    x2, u2 = mixer_ffn(a_p, a_s.reshape(-1, D_MODEL), xp, xs, mod0, mod1, w_out.astype(BF16),
                       ffn_w_in.astype(BF16), ffn_w_out.astype(BF16), ln_g, ln_b, sample_len)
    return x2, u2, new_c, nf[..., 0, :], mf[..., 0, 0]


def _qkv_kernel(u_ref, w_ref, qkv_ref, *kv_refs):
    tt, d = u_ref.shape
    u = u_ref[...]

    def head_major(out_ref, p):
        for h in range(NA_HEADS):
            out_ref[pl.ds(h, tt, stride=NA_HEADS), :] = p[:, h * NA_DH:(h + 1) * NA_DH]

    pq = _dot(u, w_ref[:, :d])
    pk = _dot(u, w_ref[:, d:2 * d])
    qkv_ref[:, :d] = pq.astype(BF16)
    pv = _dot(u, w_ref[:, 2 * d:])
    qkv_ref[:, d:2 * d] = pk.astype(BF16)
    if kv_refs:
        head_major(kv_refs[0], pk)
    qkv_ref[:, 2 * d:] = pv.astype(BF16)
    if kv_refs:
        head_major(kv_refs[1], pv)


def na_qkv(u, w, first_tok, n_tok, with_kv):
    d = u.shape[1]
    tt = TOKEN_TILE
    first = first_tok // tt
    out_specs = [pl.BlockSpec((tt, 3 * d), lambda i: (i, 0))]
    out_shape = [jax.ShapeDtypeStruct((n_tok, 3 * d), BF16)]
    if with_kv:
        out_specs += [pl.BlockSpec((tt * NA_HEADS, NA_DH), lambda i: (i, 0))] * 2
        out_shape += [jax.ShapeDtypeStruct((n_tok * NA_HEADS, NA_DH), F32)] * 2
    return pl.pallas_call(
        _qkv_kernel,
        grid=(n_tok // tt,),
        in_specs=[pl.BlockSpec((tt, d), lambda i: (first + i, 0)), pl.BlockSpec((d, 3 * d), lambda i: (0, 0))],
        out_specs=out_specs,
        out_shape=out_shape,
        compiler_params=_cparams("arbitrary"),
        name="na_qkv_prompt" if with_kv else "na_qkv_latent",
    )(u, w)


ATTN_LOG2E = math.log2(math.e)
ATTN_Q_SCALE = NA_DH ** -0.5 * ATTN_LOG2E


def _head_pair_masks():
    lane = lax.broadcasted_iota(jnp.int32, (1, 2 * NA_DH), 1)
    return (_onehot(lane < NA_DH), _onehot(lane >= NA_DH))


def _softmax_pv(score_blocks, value_blocks):
    mx = None
    for s in score_blocks:
        bm = jnp.max(s, axis=1, keepdims=True)
        mx = bm if mx is None else jnp.maximum(mx, bm)
    acc, denom = None, None
    for s, v in zip(score_blocks, value_blocks):
        p = jnp.exp2(s - mx)
        ps = jnp.sum(p, axis=1, keepdims=True)
        pv = _dot(p.astype(BF16), v)
        acc = pv if acc is None else acc + pv
        denom = ps if denom is None else denom + ps
    return acc / denom


def _ctx_attn_kernel(q_ref, k_ref, v_ref, o_ref):
    masks = _head_pair_masks()
    for hp in range(NA_HEADS // 2):
        sl = slice(hp * 2 * NA_DH, (hp + 1) * 2 * NA_DH)
        q2, k2, v2 = q_ref[:, sl], k_ref[:, sl], v_ref[:, sl]
        o2 = None
        for msk in masks:
            s = _dot_nt(q2 * msk, k2)
            o = _softmax_pv([s], [v2 * msk])
            o2 = o if o2 is None else o2 + o
        o_ref[:, sl] = o2.astype(o_ref.dtype)


def context_attention(qkv, n_prompt, prompt_len):
    t = n_prompt * prompt_len
    d = D_MODEL
    return pl.pallas_call(
        _ctx_attn_kernel,
        grid=(n_prompt,),
        in_specs=[
            pl.BlockSpec((prompt_len, d), lambda b: (b, 0)),
            pl.BlockSpec((prompt_len, d), lambda b: (b, 1)),
            pl.BlockSpec((prompt_len, d), lambda b: (b, 2)),
        ],
        out_specs=pl.BlockSpec((prompt_len, d), lambda b: (b, 0)),
        out_shape=jax.ShapeDtypeStruct((t, d), BF16),
        compiler_params=_cparams("arbitrary"),
        name="context_attention",
    )(qkv, qkv, qkv)


def _na_kernel(tile_ref, q_ref, kp_ref, kc_ref, kn_ref, vp_ref, vc_ref, vn_ref, kctx_ref, vctx_ref, bias_ref,
               o_ref):
    j = pl.program_id(1)
    nblk = pl.num_programs(1)
    m = q_ref.shape[0]
    halo = NA_HALO * GRID_W
    n_pairs = (NA_QROWS + 2 * NA_HALO) // 2
    variant = jnp.where(j == 0, 0, jnp.where(j == nblk - 1, 2, 1))
    kinds = [[tile_ref[(variant * NA_QROWS + rq) * n_pairs + kp] for kp in range(n_pairs)]
             for rq in range(NA_QROWS)]

    def bias_of(head):
        return jnp.concatenate(
            [jnp.concatenate([bias_ref[kinds[rq][kp], head] for kp in range(n_pairs)], axis=1)
             for rq in range(NA_QROWS)], axis=0)

    masks = _head_pair_masks()
    for hp in range(NA_HEADS // 2):
        sl = slice(hp * 2 * NA_DH, (hp + 1) * 2 * NA_DH)
        q2 = q_ref[:, sl]
        kloc = jnp.concatenate([kp_ref[m - halo:, sl], kc_ref[:, sl], kn_ref[:halo, sl]], axis=0)
        vloc = jnp.concatenate([vp_ref[m - halo:, sl], vc_ref[:, sl], vn_ref[:halo, sl]], axis=0)
        kctx = kctx_ref[0][:, sl]
        vctx = vctx_ref[0][:, sl]
        o2 = None
        for half, msk in enumerate(masks):
            qm = q2 * msk
            s_loc = _dot_nt(qm, kloc) + bias_of(2 * hp + half)
            s_ctx = _dot_nt(qm, kctx)
            o = _softmax_pv([s_loc, s_ctx], [vloc * msk, vctx * msk])
            o2 = o if o2 is None else o2 + o
        o_ref[:, sl] = o2.astype(o_ref.dtype)


def _na_bias_kernel(rpb_ref, o_ref, *, kinds, n_heads):
    nc, npos = rpb_ref.shape[1], o_ref.shape[2]
    pair_shift = (2 * GRID_W).bit_length() - 1
    c = lax.broadcasted_iota(jnp.int32, (nc, npos), 0)
    pos = lax.broadcasted_iota(jnp.int32, (nc, npos), 1)
    qc = pos >> pair_shift
    kc = pos & (GRID_W - 1)
    c_start = jnp.clip(qc - NA_COLS // 2, 0, GRID_W - NA_COLS)
    col_in = jnp.logical_and(kc >= c_start, kc < c_start + NA_COLS)
    dc = jnp.clip(kc - qc + NA_COLS - 1, 0, 2 * NA_COLS - 2)
    hit = jnp.logical_and(c == dc, col_in)
    right = (pos & GRID_W) != 0
    r1, r2, r3 = _split3(rpb_ref[...] * ATTN_LOG2E)

    def placed(side):
        sel = _onehot(jnp.logical_and(hit, right == side))
        return _dot(r1, sel) + _dot(r2, sel) + _dot(r3, sel)

    left, rght = placed(False), placed(True)
    rows = lambda v, dr: v[dr * n_heads:(dr + 1) * n_heads]
    for k, (dl, dr) in enumerate(kinds):
        shown = col_in[0:1, :]
        if dl is None or dr is None:
            side_ok = jnp.zeros_like(shown) if dl is None and dr is None else (right[0:1, :] == (dl is None))
            shown = jnp.logical_and(shown, side_ok)
        parts = ([rows(left, dl)] if dl is not None else []) + ([rows(rght, dr)] if dr is not None else [])
        val = sum(parts) if parts else jnp.zeros((n_heads, npos), F32)
        o_ref[k] = jnp.where(shown, val, NEG)


def _na_bias_plan():
    ndr = 2 * NA_ROWS - 1
    assert NA_QROWS >= NA_ROWS // 2 and NA_HALO == NA_ROWS // 2 and (NA_QROWS + 2 * NA_HALO) % 2 == 0
    ok = lambda dr: dr if dr is not None and 0 <= dr < ndr else None
    kinds = [(ok(dr), ok(dr + 1)) for dr in range(-1, ndr)]
    first_key_row = (lambda rq: NA_HALO, lambda rq: rq, lambda rq: NA_QROWS + NA_HALO - NA_ROWS)
    table = []
    for first in first_key_row:
        for rq in range(NA_QROWS):
            for kp in range((NA_QROWS + 2 * NA_HALO) // 2):
                drs = []
                for kk in (2 * kp, 2 * kp + 1):
                    visible = first(rq) <= kk < first(rq) + NA_ROWS
                    drs.append(kk - rq + NA_ROWS - 1 - NA_HALO if visible else None)
                kind = (drs[0], drs[1])
                if kind not in kinds:
                    kinds.append(kind)
                table.append(kinds.index(kind))
    return kinds, table


def _na_bias_tiles(rpb):
    nh, ndr, ndc = rpb.shape
    nc = 32
    kinds, table = _na_bias_plan()
    nk = len(kinds)
    rows = jnp.pad(jnp.transpose(rpb, (1, 0, 2)), ((0, 0), (0, 0), (0, nc - ndc))).reshape(ndr * nh, nc)
    npos = GRID_W * 2 * GRID_W
    tiles = pl.pallas_call(
        functools.partial(_na_bias_kernel, kinds=tuple(kinds), n_heads=nh),
        out_shape=jax.ShapeDtypeStruct((nk, nh, npos), F32),
        compiler_params=pltpu.CompilerParams(vmem_limit_bytes=VMEM_LIMIT_BYTES),
        name="na_bias",
    )(rows)
    return tiles.reshape(nk, nh, GRID_W, 2 * GRID_W), jnp.asarray(table, jnp.int32)


def neighbourhood_attention(qkv, k_ctx, v_ctx, rpb, n_sample, sample_len):
    d = D_MODEL
    m = NA_QROWS * GRID_W
    nblk = sample_len // m
    assert nblk >= 3
    bias, table = _na_bias_tiles(rpb)

    def blk(col, off):
        return pl.BlockSpec((m, d), lambda b, j, tbl: (b * nblk + jnp.clip(j + off, 0, nblk - 1), col))

    return pl.pallas_call(
        _na_kernel,
        grid_spec=pltpu.PrefetchScalarGridSpec(
            num_scalar_prefetch=1,
            grid=(n_sample, nblk),
            in_specs=[
                blk(0, 0), blk(1, -1), blk(1, 0), blk(1, 1), blk(2, -1), blk(2, 0), blk(2, 1),
                pl.BlockSpec((1,) + k_ctx.shape[1:], lambda b, j, tbl: (b, 0, 0)),
                pl.BlockSpec((1,) + v_ctx.shape[1:], lambda b, j, tbl: (b, 0, 0)),
                pl.BlockSpec(bias.shape, lambda b, j, tbl: (0, 0, 0, 0)),
            ],
            out_specs=pl.BlockSpec((m, d), lambda b, j, tbl: (b * nblk + j, 0)),
        ),
        out_shape=jax.ShapeDtypeStruct((n_sample * sample_len, d), BF16),
        compiler_params=_cparams("arbitrary", "arbitrary"),
        name="neighbourhood_attention",
    )(table, qkv, qkv, qkv, qkv, qkv, qkv, qkv, k_ctx, v_ctx, bias)


MOE_ROW_TILE = 512
SC_CORES = 2
SC_SUBCORES = 16
SC_GATHER_CHUNK = 128

def _route_tokens(u, wt, carry):
    w1, w2, w3 = _split3(wt)
    lg = _dot_nt(w1, u) + _dot_nt(w2, u) + _dot_nt(w3, u)
    ne, nt = lg.shape
    e = lax.broadcasted_iota(jnp.int32, lg.shape, 0)
    m1 = jnp.max(lg, axis=0, keepdims=True)
    i1 = jnp.min(jnp.where(lg == m1, e, ne), axis=0, keepdims=True)
    sel1 = e == i1
    lg2 = jnp.where(sel1, -jnp.inf, lg)
    m2 = jnp.max(lg2, axis=0, keepdims=True)
    i2 = jnp.min(jnp.where(lg2 == m2, e, ne), axis=0, keepdims=True)
    sel2 = e == i2
    ex = jnp.exp(m2 - m1)
    wa = 1.0 / (1.0 + ex)
    wb = ex / (1.0 + ex)
    comb = jnp.where(sel1, wa, jnp.where(sel2, wb, 0.0))
    assign = jnp.logical_or(sel1, sel2)
    a = jnp.where(assign, 1.0, 0.0)
    row = lax.broadcasted_iota(jnp.int32, (nt, nt), 0)
    col = lax.broadcasted_iota(jnp.int32, (nt, nt), 1)
    tri = _onehot(row < col)
    rank = _dot(a.astype(BF16), tri) + carry
    pos = jnp.where(assign, rank, -1.0)
    return comb, pos, carry + jnp.sum(a, axis=1, keepdims=True)


def _moe_plan(totals, pos, comb, n_tok):
    ne = totals.shape[0]
    tm = MOE_ROW_TILE
    n_tiles = (2 * n_tok) // tm + ne
    tiles_per = (totals + tm - 1) // tm
    tile_end = jnp.cumsum(tiles_per)
    n_used = tile_end[-1]
    base_rows = (tile_end - tiles_per) * tm
    ti = jnp.arange(n_tiles, dtype=jnp.int32)
    tile_expert = jnp.minimum(jnp.sum((ti[:, None] >= tile_end[None, :]).astype(jnp.int32), axis=1), ne - 1)
    last_expert = tile_expert[jnp.maximum(n_used - 1, 0)]
    tile_expert = jnp.where(ti < n_used, tile_expert, last_expert)
    routed = pos >= 0.0
    row = base_rows[:, None] + pos.astype(jnp.int32)
    row_a = jnp.min(jnp.where(routed, row, n_tiles * tm), axis=0)
    row_b = jnp.max(jnp.where(routed, row, -1), axis=0)
    w_a = jnp.sum(jnp.where(jnp.logical_and(routed, row == row_a[None, :]), comb, 0.0), axis=0)
    w_b = jnp.sum(jnp.where(jnp.logical_and(routed, row == row_b[None, :]), comb, 0.0), axis=0)
    tile_rows = jnp.clip(totals[tile_expert] - (ti - (tile_end - tiles_per)[tile_expert]) * tm, 0, tm)
    tile_rows = jnp.where(ti < n_used, tile_rows, 0)
    return dict(n_rows=n_tiles * tm, tile_expert=tile_expert.astype(jnp.int32),
                n_used=n_used.reshape(1).astype(jnp.int32), tile_rows=tile_rows.astype(jnp.int32),
                token_rows=jnp.concatenate([row_a, row_b]).astype(jnp.int32),
                token_weights=jnp.concatenate([w_a[None], w_b[None], jnp.zeros((6, n_tok), F32)], axis=0))


def _pack_bf16_pairs(x):
    half = x.shape[1] // 2
    bits = pltpu.bitcast(x.astype(BF16).astype(F32), jnp.int32)
    return (bits[:, :half] & jnp.int32(-65536)) | lax.shift_right_logical(bits[:, half:], jnp.int32(16))


def _unpack_bf16_pairs(p):
    hi = pltpu.bitcast(p & jnp.int32(-65536), F32)
    lo = pltpu.bitcast(lax.shift_left(p, jnp.int32(16)), F32)
    return jnp.concatenate([hi, lo], axis=1)


def sc_gather_rows(table, idx):
    _, w = table.shape
    b = idx.shape[0]
    workers = SC_CORES * SC_SUBCORES
    ch = SC_GATHER_CHUNK
    n_chunks = b // (workers * ch)
    assert n_chunks * workers * ch == b
    mesh = plsc.VectorSubcoreMesh(core_axis_name="c", subcore_axis_name="s", num_cores=SC_CORES,
                                  num_subcores=SC_SUBCORES)

    @functools.partial(
        pl.kernel, mesh=mesh, out_type=jax.ShapeDtypeStruct((b, w), table.dtype),
        scratch_types=[pltpu.VMEM((ch,), jnp.int32), pltpu.VMEM((ch, w), table.dtype), pltpu.SemaphoreType.DMA])
    def gather(table_hbm, idx_hbm, out_hbm, idx_v, rows_v, sem):
        worker = lax.axis_index("s") * SC_CORES + lax.axis_index("c")

        @pl.loop(0, n_chunks)
        def _(c):
            base = (worker * n_chunks + c) * ch
            pltpu.sync_copy(idx_hbm.at[pl.ds(base, ch)], idx_v)
            pltpu.async_copy(table_hbm.at[idx_v], rows_v, sem).wait()
            pltpu.sync_copy(rows_v, out_hbm.at[pl.ds(base, ch)])

    return gather(table, idx)


def sc_scatter_rows(rows, idx, n_out):
    t, w = rows.shape
    copies = idx.shape[0] // t
    workers = SC_CORES * SC_SUBCORES
    ch = SC_GATHER_CHUNK
    n_chunks = t // (workers * ch)
    assert n_chunks * workers * ch == t and copies * t == idx.shape[0]
    mesh = plsc.VectorSubcoreMesh(core_axis_name="c", subcore_axis_name="s", num_cores=SC_CORES,
                                  num_subcores=SC_SUBCORES)

    @functools.partial(
        pl.kernel, mesh=mesh, out_type=jax.ShapeDtypeStruct((n_out, w), rows.dtype),
        scratch_types=[pltpu.VMEM((ch,), jnp.int32), pltpu.VMEM((ch, w), rows.dtype), pltpu.SemaphoreType.DMA])
    def scatter(rows_hbm, idx_hbm, out_hbm, idx_v, rows_v, sem):
        worker = lax.axis_index("s") * SC_CORES + lax.axis_index("c")

        @pl.loop(0, n_chunks)
        def _(c):
            base = (worker * n_chunks + c) * ch
            pltpu.sync_copy(rows_hbm.at[pl.ds(base, ch)], rows_v)
            for k in range(copies):
                pltpu.sync_copy(idx_hbm.at[pl.ds(k * t + base, ch)], idx_v)
                pltpu.async_copy(rows_v, out_hbm.at[idx_v], sem).wait()

    return scatter(rows, idx)


def _gffn_kernel(te_ref, nu_ref, nr_ref, x_ref, wi_ref, wo_ref, y_ref):
    i = pl.program_id(0)
    used = i < nu_ref[0]

    @pl.when(used)
    def _():
        packed = x_ref[...]
        row = lax.broadcasted_iota(jnp.int32, packed.shape, 0)
        x = _unpack_bf16_pairs(jnp.where(row < nr_ref[i], packed, 0)).astype(BF16)
        y = _swiglu_tile(x, lambda c: wi_ref[0, :, c], lambda r: wo_ref[0, r, :], wo_ref.shape[1])
        y_ref[...] = _pack_bf16_pairs(y)

    @pl.when(jnp.logical_not(used))
    def _():
        y_ref[...] = jnp.zeros_like(y_ref)


def moe_grouped_ffn(xs, w_in, w_out, plan):
    nr, half = xs.shape
    ne, dff, d = w_out.shape
    tm = MOE_ROW_TILE
    return pl.pallas_call(
        _gffn_kernel,
        grid_spec=pltpu.PrefetchScalarGridSpec(
            num_scalar_prefetch=3,
            grid=(nr // tm,),
            in_specs=[
                pl.BlockSpec((tm, half), lambda i, te, nu, tr: (jnp.minimum(i, nu[0] - 1), 0)),
                pl.BlockSpec((1, d, 2 * dff), lambda i, te, nu, tr: (te[i], 0, 0)),
                pl.BlockSpec((1, dff, d), lambda i, te, nu, tr: (te[i], 0, 0)),
            ],
            out_specs=pl.BlockSpec((tm, half), lambda i, te, nu, tr: (i, 0)),
        ),
        out_shape=jax.ShapeDtypeStruct((nr, half), jnp.int32),
        compiler_params=_cparams("arbitrary"),
        name="moe_grouped_ffn",
    )(plan["tile_expert"], plan["n_used"], plan["tile_rows"], xs, w_in, w_out)


def _combine_kernel(ya_ref, yb_ref, w_ref, x_ref, mod_ref, lng_ref, lnb_ref, op_ref, os_ref, *, n_prompt_tiles):
    i = pl.program_id(0)
    w = w_ref[...].T
    moe = w[:, 0:1] * _unpack_bf16_pairs(ya_ref[...]) + w[:, 1:2] * _unpack_bf16_pairs(yb_ref[...])
    z = DEEPNORM_ALPHA * x_ref[...] + mod_ref[0, 5:6, :] * moe
    xn = _layer_norm(z, lng_ref[...], lnb_ref[...])

    @pl.when(i < n_prompt_tiles)
    def _():
        op_ref[...] = xn

    @pl.when(i >= n_prompt_tiles)
    def _():
        os_ref[...] = xn


def moe_combine(y_tok, weights, x, mod, ln_g, ln_b, n_prompt_tok, sample_len):
    t, d = x.shape
    tt = TOKEN_TILE
    nt = t // tt
    npt = n_prompt_tok // tt
    seg = functools.partial(_seg_of_tile, tile=tt, n_prompt_tok=n_prompt_tok, sample_len=sample_len)
    return pl.pallas_call(
        functools.partial(_combine_kernel, n_prompt_tiles=npt),
        grid=(nt,),
        in_specs=[
            pl.BlockSpec((tt, d // 2), lambda i: (i, 0)),
            pl.BlockSpec((tt, d // 2), lambda i: (nt + i, 0)),
            pl.BlockSpec((8, tt), lambda i: (0, i)),
            pl.BlockSpec((tt, d), lambda i: (i, 0)),
            pl.BlockSpec((1, 6, d), lambda i: (seg(i), 0, 0)),
            pl.BlockSpec((1, d), lambda i: (0, 0)),
            pl.BlockSpec((1, d), lambda i: (0, 0)),
        ],
        out_specs=[
            pl.BlockSpec((tt, d), lambda i: (jnp.minimum(i, npt - 1), 0)),
            pl.BlockSpec((tt, d), lambda i: (jnp.maximum(i - npt, 0), 0)),
        ],
        out_shape=[jax.ShapeDtypeStruct((n_prompt_tok, d), F32), jax.ShapeDtypeStruct((t - n_prompt_tok, d), F32)],
        compiler_params=_cparams("arbitrary"),
        name="moe_combine",
    )(y_tok, y_tok, weights, x, mod, ln_g.reshape(1, d), ln_b.reshape(1, d))


def _layer1(x, u, mod1, cache_k, cache_v, ln_g, ln_b, w_qkv, rpb, w_out, w_router, moe_w_in, moe_w_out,
            n_prompt, prompt_len, n_sample, sample_len):
    n_prompt_tok = n_prompt * prompt_len
    d = D_MODEL
    col_scale = jnp.where(jnp.arange(3 * d) < d, ATTN_Q_SCALE, 1.0).astype(F32)
    w_qkv = (w_qkv * col_scale[None, :]).astype(BF16)
    qkv_p, k_p, v_p = na_qkv(u, w_qkv, 0, n_prompt_tok, with_kv=True)
    qkv_s, = na_qkv(u, w_qkv, n_prompt_tok, u.shape[0] - n_prompt_tok, with_kv=False)
    attn_p = context_attention(qkv_p, n_prompt, prompt_len)
    k_ctx = cache_k.reshape(n_sample, -1, d).astype(BF16)
    v_ctx = cache_v.reshape(n_sample, -1, d).astype(BF16)
    attn_s = neighbourhood_attention(qkv_s, k_ctx, v_ctx, rpb, n_sample, sample_len)
    moe_w_in, moe_w_out = moe_w_in.astype(BF16), moe_w_out.astype(BF16)
    x1, u1_packed, comb, pos, totals = mixer_outproj_router(
        attn_p, attn_s, x, mod1, w_out.astype(BF16), ln_g[0], ln_b[0], w_router, sample_len,
        run_after=(moe_w_in, moe_w_out))
    plan = _moe_plan(totals[:, 0].astype(jnp.int32), pos, comb, x.shape[0])
    xs = sc_scatter_rows(u1_packed, plan["token_rows"], plan["n_rows"])
    y = moe_grouped_ffn(xs, moe_w_in, moe_w_out, plan)
    y_tok = sc_gather_rows(y, plan["token_rows"])
    y_p, y_s = moe_combine(y_tok, plan["token_weights"], x1, mod1, ln_g[1], ln_b[1], n_prompt_tok, sample_len)
    return y_p, y_s, k_p, v_p


def kernel(x_prompt, x_sample, state_mlstm_C, state_mlstm_n, state_mlstm_m, cache_na_k, cache_na_v, c, c_ctx,
           ada_w, ada_b, ln_g, ln_b, mlstm_w_in, mlstm_b_gates, mlstm_norm_g, mlstm_w_out, na_w_qkv, na_rpb,
           na_w_out, ffn_w_in, ffn_w_out, moe_w_router, moe_w_in, moe_w_out):
    n_prompt, prompt_len, d = x_prompt.shape
    n_sample, sample_len, _ = x_sample.shape
    assert d == D_MODEL and prompt_len == MLSTM_CHUNK and sample_len % MLSTM_CHUNK == 0
    assert ada_w.shape[0] == DEPTH == 2
    mod = _modulation_tables(c, c_ctx, ada_w, ada_b)
    x2, u2, new_c, new_n, new_m = _layer0(
        x_prompt.reshape(-1, d), x_sample.reshape(-1, d), mod[0], mod[1], state_mlstm_C[:, 0], state_mlstm_n[:, 0], state_mlstm_m[:, 0], ln_g[0], ln_b[0],
        mlstm_w_in[0], mlstm_b_gates[0], mlstm_norm_g[0], mlstm_w_out[0], ffn_w_in[0], ffn_w_out[0],
        n_prompt, prompt_len, n_sample, sample_len)
    y_p, y_s, k_p, v_p = _layer1(
        x2, u2, mod[1], cache_na_k[:, 0], cache_na_v[:, 0], ln_g[1], ln_b[1], na_w_qkv[0], na_rpb[0], na_w_out[0],
        moe_w_router[0], moe_w_in[0], moe_w_out[0], n_prompt, prompt_len, n_sample, sample_len)
    kv_shape = (n_prompt, 1, prompt_len, NA_HEADS, NA_DH)
    return (y_p.reshape(x_prompt.shape), y_s.reshape(x_sample.shape), new_c, new_n, new_m,
            k_p.reshape(kv_shape), v_p.reshape(kv_shape))
```

```python
import functools
import math

import jax
import jax.numpy as jnp
from jax import lax
from jax.experimental import pallas as pl
from jax.experimental.pallas import tpu as pltpu
from jax.experimental.pallas import tpu_sc as plsc

F32 = jnp.float32
BF16 = jnp.bfloat16

D_MODEL = 1024
DEPTH = 2
GRID_W = 64
M_HEADS = 4
M_DK = 128
M_DV = 256
M_HK = M_HEADS * M_DK
NA_HEADS = 16
NA_DH = 64
NA_ROWS = 8
NA_COLS = 16
DEEPNORM_ALPHA = (2.0 * DEPTH) ** 0.25
LN_EPS = 1e-5
NEG = -1e30

VMEM_LIMIT_BYTES = 56 * 1024 * 1024

MLSTM_CHUNK = 256
MLSTM_SAMPLE_CHUNK = 256
TOKEN_TILE = 512
FFN_TOKEN_TILE = 512
FFN_SUB = 256
NA_QROWS = 4
NA_HALO = 4


def _cparams(*sem):
    return pltpu.CompilerParams(dimension_semantics=sem, vmem_limit_bytes=VMEM_LIMIT_BYTES)


def _dot(a, b):
    return jnp.dot(a, b, preferred_element_type=F32)


def _dot_nt(a, b):
    return lax.dot_general(a, b, (((1,), (1,)), ((), ())), preferred_element_type=F32)


def _onehot(mask):
    return jnp.where(mask, 1.0, 0.0).astype(BF16)


def _split3(x):
    hi = x.astype(BF16)
    r = x - hi.astype(F32)
    mid = r.astype(BF16)
    lo = (r - mid.astype(F32)).astype(BF16)
    return hi, mid, lo


def _layer_norm(z, g, b):
    mu = jnp.mean(z, axis=-1, keepdims=True)
    zc = z - mu
    var = jnp.mean(zc * zc, axis=-1, keepdims=True)
    return zc * lax.rsqrt(var + LN_EPS) * g + b


def _log_sigmoid(x):
    return jnp.minimum(x, 0.0) - jnp.log(1.0 + jnp.exp(-jnp.abs(x)))


def _seg_of_tile(i, tile, n_prompt_tok, sample_len):
    tok = i * tile
    return jnp.where(tok < n_prompt_tok, 0, 1 + (tok - n_prompt_tok) // sample_len)


def _adaln_kernel(c_ref, w_ref, b_ref, o_ref):
    c = c_ref[...]
    a = (c * jax.nn.sigmoid(c)).astype(BF16)
    o_ref[0] = _dot(a, w_ref[0].astype(BF16)) + b_ref[0]


def adaln(cvec, ada_w, ada_b):
    depth, d, n = ada_w.shape
    tn = 1536
    return pl.pallas_call(
        _adaln_kernel,
        grid=(depth, n // tn),
        in_specs=[
            pl.BlockSpec((8, d), lambda l, j: (0, 0)),
            pl.BlockSpec((1, d, tn), lambda l, j: (l, 0, j)),
            pl.BlockSpec((1, 1, tn), lambda l, j: (l, 0, j)),
        ],
        out_specs=pl.BlockSpec((1, 8, tn), lambda l, j: (l, 0, j)),
        out_shape=jax.ShapeDtypeStruct((depth, 8, n), F32),
        compiler_params=_cparams("arbitrary", "arbitrary"),
        name="adaln",
    )(cvec, ada_w, ada_b.reshape(depth, 1, n))


def _inproj_kernel(xp_ref, xs_ref, mod_ref, w_ref, wg_ref, wgt_ref, bg_ref, bgt_ref,
                   qkv_ref, o_ref, g_ref, gt_ref, *, n_prompt_tiles):
    x = jnp.where(pl.program_id(0) < n_prompt_tiles, xp_ref[...], xs_ref[...])
    u = (x * (1.0 + mod_ref[0, 1:2, :]) + mod_ref[0, 0:1, :]).astype(BF16)
    nqkv = qkv_ref.shape[1]
    half = nqkv // 2
    g = _dot(u, wg_ref[...]) + bg_ref[...]
    gt = _dot_nt(wgt_ref[...], u) + bgt_ref[...]
    p0 = _dot_nt(u, w_ref[:half, :])
    col = lax.broadcasted_iota(jnp.int32, g.shape, 1)
    g_ref[...] = jnp.where(((col >> 2) & 1) == 1, _log_sigmoid(g), g)
    p1 = _dot_nt(u, w_ref[half:nqkv, :])
    qkv_ref[:, :half] = p0.astype(BF16)
    row = lax.broadcasted_iota(jnp.int32, gt.shape, 0)
    gt_ref[...] = jnp.where(((row >> 2) & 1) == 1, _log_sigmoid(gt), gt)
    p2 = _dot_nt(u, w_ref[nqkv:, :])
    qkv_ref[:, half:] = p1.astype(BF16)
    o_ref[...] = p2


def _split_token_specs(tile, d, n_prompt_tiles):
    return [pl.BlockSpec((tile, d), lambda i: (jnp.minimum(i, n_prompt_tiles - 1), 0)),
            pl.BlockSpec((tile, d), lambda i: (jnp.maximum(i - n_prompt_tiles, 0), 0))]


def mlstm_inproj(xp, xs, mod, w_main, w_gate, w_gate_t, b_gate, sample_len):
    n_prompt_tok, d = xp.shape
    t = n_prompt_tok + xs.shape[0]
    n_main = w_main.shape[0]
    n_qkv = 2 * M_HK + D_MODEL
    ng = w_gate.shape[1]
    tt = TOKEN_TILE
    npt = n_prompt_tok // tt
    seg = functools.partial(_seg_of_tile, tile=tt, n_prompt_tok=n_prompt_tok, sample_len=sample_len)
    return pl.pallas_call(
        functools.partial(_inproj_kernel, n_prompt_tiles=npt),
        grid=(t // tt,),
        in_specs=_split_token_specs(tt, d, npt) + [
            pl.BlockSpec((1, 6, d), lambda i: (seg(i), 0, 0)),
            pl.BlockSpec((n_main, d), lambda i: (0, 0)),
            pl.BlockSpec((d, ng), lambda i: (0, 0)),
            pl.BlockSpec((ng, d), lambda i: (0, 0)),
            pl.BlockSpec((1, ng), lambda i: (0, 0)),
            pl.BlockSpec((ng, 1), lambda i: (0, 0)),
        ],
        out_specs=[
            pl.BlockSpec((tt, n_qkv), lambda i: (i, 0)),
            pl.BlockSpec((tt, n_main - n_qkv), lambda i: (i, 0)),
            pl.BlockSpec((tt, ng), lambda i: (i, 0)),
            pl.BlockSpec((ng, tt), lambda i: (0, i)),
        ],
        out_shape=[
            jax.ShapeDtypeStruct((t, n_qkv), BF16),
            jax.ShapeDtypeStruct((t, n_main - n_qkv), F32),
            jax.ShapeDtypeStruct((t, ng), F32),
            jax.ShapeDtypeStruct((ng, t), F32),
        ],
        compiler_params=_cparams("arbitrary"),
        name="mlstm_inproj",
    )(xp, xs, mod, w_main, w_gate, w_gate_t, b_gate.reshape(1, ng), b_gate.reshape(ng, 1))


def _mlstm_gate_sums(g, gt, backward):
    L = g.shape[0]
    row = lax.broadcasted_iota(jnp.int32, (L, L), 0)
    col = lax.broadcasted_iota(jnp.int32, (L, L), 1)
    lower = col <= row
    upper = col >= row
    tri_col = _onehot(upper if backward else lower)
    tri_row = _onehot(lower if backward else upper)
    ghi, gmid, glo = _split3(g)
    b_cols = _dot(tri_col, ghi) + _dot(tri_col, gmid) + _dot(tri_col, glo)
    thi, tmid, tlo = _split3(gt)
    b_rows = _dot(thi, tri_row) + _dot(tmid, tri_row) + _dot(tlo, tri_row)
    return b_cols, b_rows


def _mlstm_heads(items):
    scale = M_DK ** -0.5
    log_scale = math.log(scale)
    L = items[0][0].shape[0]
    row = lax.broadcasted_iota(jnp.int32, (L, L), 0)
    col = lax.broadcasted_iota(jnp.int32, (L, L), 1)
    masks = {False: col <= row, True: col >= row}
    n = range(len(items))
    qh, kh, vh, li_row, lf_row, b_row, b_col, backward, state = zip(*items)
    has_state = [st is not None for st in state]
    m_prev = [st[2] if st is not None else 0.0 for st in state]

    qk = [_dot_nt(qh[i], kh[i]) for i in n]
    qc = [_dot(qh[i], state[i][0].astype(BF16)) if has_state[i] else None for i in n]
    qn = [_dot_nt(qh[i], state[i][1].astype(BF16))[:, 0:1] if has_state[i] else None for i in n]
    total = [jnp.sum(lf_row[i], axis=1, keepdims=True) for i in n]
    d = [jnp.where(masks[backward[i]], b_col[i] + (li_row[i] - b_row[i] + log_scale), NEG) for i in n]
    inter = [b_col[i] + m_prev[i] if has_state[i] else b_col[i] for i in n]
    m_t = [jnp.maximum(inter[i], jnp.max(d[i], axis=1, keepdims=True) - log_scale) for i in n]
    p = [jnp.exp(d[i] - m_t[i]) for i in n]
    s = [qk[i] * p[i] for i in n]
    den = [jnp.sum(s[i], axis=1, keepdims=True) for i in n]
    num = [_dot(s[i].astype(BF16), vh[i]) for i in n]
    a = [jnp.exp(inter[i] - m_t[i]) * scale if has_state[i] else None for i in n]
    num = [a[i] * qc[i] + num[i] if has_state[i] else num[i] for i in n]
    den = [a[i] * qn[i] + den[i] if has_state[i] else den[i] for i in n]
    hh = [num[i] / jnp.maximum(jnp.abs(den[i]), jnp.exp(-m_t[i])) for i in n]

    g_row = [total[i] - b_row[i] + li_row[i] for i in n]
    m_new = [jnp.maximum(total[i] + m_prev[i], jnp.max(g_row[i], axis=1, keepdims=True)) for i in n]
    ws_row = [jnp.exp(g_row[i] - m_new[i]) for i in n]
    kt = [kh[i].astype(F32).T for i in n]
    c_new = [_dot((kt[i] * ws_row[i]).astype(BF16), vh[i]) for i in n]
    n_new = [_dot(jnp.broadcast_to(ws_row[i], (8, L)).astype(BF16), kh[i]) for i in n]
    a0 = [jnp.exp(total[i] + m_prev[i] - m_new[i]) if has_state[i] else None for i in n]
    c_new = [a0[i] * state[i][0] + c_new[i] if has_state[i] else c_new[i] for i in n]
    n_new = [a0[i] * state[i][1] + n_new[i] if has_state[i] else n_new[i] for i in n]
    return hh, c_new, n_new, m_new


def _head_norm_gate(ht, norm_g, ogate):
    mu = jnp.mean(ht, axis=-1, keepdims=True)
    hc = ht - mu
    var = jnp.mean(hc * hc, axis=-1, keepdims=True)
    return (hc * lax.rsqrt(var + LN_EPS) * norm_g * jax.nn.sigmoid(ogate)).astype(BF16)


def _head_operands(q_ref, k_ref, v_ref, h):
    return (q_ref[:, h * M_DK:(h + 1) * M_DK], k_ref[:, h * M_DK:(h + 1) * M_DK], v_ref[:, h * M_DV:(h + 1) * M_DV])


def _gate_operands(gt, b_rows, b_cols, direction, h):
    ci = direction * 8 + h
    cf = direction * 8 + 4 + h
    return gt[ci:ci + 1, :], gt[cf:cf + 1, :], b_rows[cf:cf + 1, :], b_cols[:, cf:cf + 1]


def _mlstm_prompt_kernel(q_ref, k_ref, v_ref, g_ref, gt_ref, o_ref, ng_ref, a_ref, cf_ref, nf_ref, mf_ref):
    g, gt = g_ref[...], gt_ref[...]
    sums = [_mlstm_gate_sums(g, gt, direction == 1) for direction in (0, 1)]
    keys = [(direction, h) for direction in (0, 1) for h in range(M_HEADS)]
    items = [_head_operands(q_ref, k_ref, v_ref, h)
             + _gate_operands(gt, sums[direction][1], sums[direction][0], direction, h) + (direction == 1, None)
             for direction, h in keys]
    hh, c_new, n_new, m_new = _mlstm_heads(items)
    for i, (direction, h) in enumerate(keys):
        cf_ref[0, 0, direction, h] = c_new[i]
        nf_ref[0, 0, direction, h] = n_new[i]
        mf_ref[0, 0, direction, h] = jnp.broadcast_to(m_new[i], mf_ref.shape[-2:])
    for h in range(M_HEADS):
        sl = slice(h * M_DV, (h + 1) * M_DV)
        a_ref[:, sl] = _head_norm_gate(hh[h] + hh[M_HEADS + h], ng_ref[:, sl], o_ref[:, sl])


def _mlstm_sample_kernel(*refs, direction, n_units):
    backward = direction == 1
    refs = list(refs)
    unit_refs = [tuple(refs.pop(0) for _ in range(5)) for _ in range(n_units)]
    c0_ref, n0_ref, m0_ref = refs.pop(0), refs.pop(0), refs.pop(0)
    if backward:
        hprev_ref = refs.pop(0)
        o_refs = [refs.pop(0) for _ in range(n_units)]
        ng_ref = refs.pop(0)
    out_ref, c_scr, n_scr, m_scr = refs

    @pl.when(pl.program_id(0) == 0)
    def _():
        c_scr[...] = c0_ref[...]
        n_scr[...] = n0_ref[...]
        m_scr[...] = m0_ref[...]

    keys, items = [], []
    for u, (q_ref, k_ref, v_ref, g_ref, gt_ref) in enumerate(unit_refs):
        gt = gt_ref[...]
        b_cols, b_rows = _mlstm_gate_sums(g_ref[...], gt, backward)
        for h in range(M_HEADS):
            state = (c_scr[u, h], n_scr[u, h], m_scr[u, h][0:1, 0:1])
            keys.append((u, h))
            items.append(_head_operands(q_ref, k_ref, v_ref, h) + _gate_operands(gt, b_rows, b_cols, direction, h)
                         + (backward, state))
    hh, c_new, n_new, m_new = _mlstm_heads(items)
    for i, (u, h) in enumerate(keys):
        c_scr[u, h] = c_new[i]
        n_scr[u, h] = n_new[i]
        m_scr[u, h] = jnp.broadcast_to(m_new[i], m_scr.shape[-2:])
        sl = slice(h * M_DV, (h + 1) * M_DV)
        if backward:
            out_ref[u, :, sl] = _head_norm_gate(hprev_ref[u, :, sl] + hh[i], ng_ref[:, sl], o_refs[u][:, sl])
        else:
            out_ref[u, :, sl] = hh[i]


def _mlstm_chunk_specs(block_of, L):
    return [
        pl.BlockSpec((L, M_HK), lambda s: (block_of(s), 0)),
        pl.BlockSpec((L, M_HK), lambda s: (block_of(s), 1)),
        pl.BlockSpec((L, D_MODEL), lambda s: (block_of(s), 1)),
        pl.BlockSpec((L, 16), lambda s: (block_of(s), 0)),
        pl.BlockSpec((16, L), lambda s: (0, block_of(s))),
    ]


def mlstm_prompt(qkv, g, gt, ogate, norm_g, n_prompt):
    L = MLSTM_CHUNK
    state_blk = lambda *tail: pl.BlockSpec((1, 1, 2, M_HEADS) + tail, lambda s: (s, 0, 0, 0, 0, 0))
    return pl.pallas_call(
        _mlstm_prompt_kernel,
        grid=(n_prompt,),
        in_specs=_mlstm_chunk_specs(lambda s: s, L) + [
            pl.BlockSpec((L, D_MODEL), lambda s: (s, 0)),
            pl.BlockSpec((1, D_MODEL), lambda s: (0, 0)),
        ],
        out_specs=[pl.BlockSpec((L, D_MODEL), lambda s: (s, 0)),
                   state_blk(M_DK, M_DV), state_blk(8, M_DK), state_blk(8, 128)],
        out_shape=[
            jax.ShapeDtypeStruct((n_prompt * L, D_MODEL), BF16),
            jax.ShapeDtypeStruct((n_prompt, 1, 2, M_HEADS, M_DK, M_DV), F32),
            jax.ShapeDtypeStruct((n_prompt, 1, 2, M_HEADS, 8, M_DK), F32),
            jax.ShapeDtypeStruct((n_prompt, 1, 2, M_HEADS, 8, 128), F32),
        ],
        compiler_params=_cparams("arbitrary"),
        name="mlstm_prompt",
    )(qkv, qkv, qkv, g, gt, ogate, norm_g.reshape(1, D_MODEL))


def mlstm_sample(direction, qkv, g, gt, c0, n0, m0, first_block, chunks_per_sample,
                 hprev=None, ogate=None, norm_g=None):
    L = MLSTM_SAMPLE_CHUNK
    backward = direction == 1
    cps = chunks_per_sample
    n_units = c0.shape[0]
    pos = (lambda s: cps - 1 - s) if backward else (lambda s: s)
    unit_block = [functools.partial(lambda s, u: first_block + u * cps + pos(s), u=u) for u in range(n_units)]
    whole = lambda a: pl.BlockSpec(a.shape, lambda s: (0,) * a.ndim)
    in_specs, args = [], []
    for u in range(n_units):
        in_specs += _mlstm_chunk_specs(unit_block[u], L)
        args += [qkv, qkv, qkv, g, gt]
    in_specs += [whole(c0), whole(n0), whole(m0)]
    args += [c0, n0, m0]
    if backward:
        in_specs += [pl.BlockSpec((n_units, L, D_MODEL), lambda s: (0, pos(s), 0))]
        in_specs += [pl.BlockSpec((L, D_MODEL), functools.partial(lambda s, u: (unit_block[u](s), 0), u=u))
                     for u in range(n_units)]
        in_specs += [pl.BlockSpec((1, D_MODEL), lambda s: (0, 0))]
        args += [hprev] + [ogate] * n_units + [norm_g.reshape(1, D_MODEL)]
    return pl.pallas_call(
        functools.partial(_mlstm_sample_kernel, direction=direction, n_units=n_units),
        grid=(cps,),
        in_specs=in_specs,
        out_specs=pl.BlockSpec((n_units, L, D_MODEL), lambda s: (0, pos(s), 0)),
        out_shape=jax.ShapeDtypeStruct((n_units, cps * L, D_MODEL), BF16 if backward else F32),
        scratch_shapes=[pltpu.VMEM(c0.shape, F32), pltpu.VMEM(n0.shape, F32), pltpu.VMEM(m0.shape, F32)],
        compiler_params=_cparams("arbitrary"),
        name="mlstm_sample_bwd" if backward else "mlstm_sample_fwd",
    )(*args)


def _outproj_router_kernel(ap_ref, as_ref, x_ref, mod_ref, w_ref, lng_ref, lnb_ref, wr_ref, *refs,
                           n_prompt_tiles):
    xo_ref, up_ref, comb_ref, pos_ref, total_ref, carry_scr = refs[-6:]
    i = pl.program_id(0)

    @pl.when(i == 0)
    def _():
        carry_scr[...] = jnp.zeros_like(carry_scr)

    a = jnp.where(i < n_prompt_tiles, ap_ref[...], as_ref[...])
    z = DEEPNORM_ALPHA * x_ref[...] + mod_ref[0, 2:3, :] * _dot(a, w_ref[...])
    xn = _layer_norm(z, lng_ref[...], lnb_ref[...])
    xo_ref[...] = xn
    u = xn * (1.0 + mod_ref[0, 4:5, :]) + mod_ref[0, 3:4, :]
    up_ref[...] = _pack_bf16_pairs(u)
    comb, pos, carry_new = _route_tokens(u.astype(BF16), wr_ref[...], carry_scr[:, 0:1])
    comb_ref[...] = comb
    pos_ref[...] = pos
    carry_scr[...] = jnp.broadcast_to(carry_new, carry_scr.shape)
    total_ref[...] = jnp.broadcast_to(carry_new, carry_scr.shape)


def mixer_outproj_router(a_p, a_s, x, mod, w, ln_g, ln_b, w_router, sample_len, run_after=()):
    t, d = x.shape
    ne = w_router.shape[1]
    tt = TOKEN_TILE
    npt = a_p.shape[0] // tt
    seg = functools.partial(_seg_of_tile, tile=tt, n_prompt_tok=a_p.shape[0], sample_len=sample_len)
    return pl.pallas_call(
        functools.partial(_outproj_router_kernel, n_prompt_tiles=npt),
        grid=(t // tt,),
        in_specs=_split_token_specs(tt, d, npt) + [
            pl.BlockSpec((tt, d), lambda i: (i, 0)),
            pl.BlockSpec((1, 6, d), lambda i: (seg(i), 0, 0)),
            pl.BlockSpec((d, d), lambda i: (0, 0)),
            pl.BlockSpec((1, d), lambda i: (0, 0)),
            pl.BlockSpec((1, d), lambda i: (0, 0)),
            pl.BlockSpec((ne, d), lambda i: (0, 0)),
        ] + [pl.BlockSpec(memory_space=pl.ANY)] * len(run_after),
        out_specs=[
            pl.BlockSpec((tt, d), lambda i: (i, 0)),
            pl.BlockSpec((tt, d // 2), lambda i: (i, 0)),
            pl.BlockSpec((ne, tt), lambda i: (0, i)),
            pl.BlockSpec((ne, tt), lambda i: (0, i)),
            pl.BlockSpec((ne, 128), lambda i: (0, 0)),
        ],
        out_shape=[
            jax.ShapeDtypeStruct((t, d), F32),
            jax.ShapeDtypeStruct((t, d // 2), jnp.int32),
            jax.ShapeDtypeStruct((ne, t), F32),
            jax.ShapeDtypeStruct((ne, t), F32),
            jax.ShapeDtypeStruct((ne, 128), F32),
        ],
        scratch_shapes=[pltpu.VMEM((ne, 128), F32)],
        compiler_params=_cparams("arbitrary"),
        name="mixer_outproj_router",
    )(a_p, a_s, x, mod, w, ln_g.reshape(1, d), ln_b.reshape(1, d), w_router.T, *run_after)


def _swiglu_tile(x, wi, wo, dff):
    sub = FFN_SUB
    proj = lambda j: (_dot(x, wi(slice(j * sub, (j + 1) * sub))), _dot(x, wi(slice(dff + j * sub, dff + (j + 1) * sub))))
    acts, cur = [], proj(0)
    for j in range(dff // sub):
        nxt = proj(j + 1) if (j + 1) * sub < dff else None
        gp, up = cur
        acts.append((gp * jax.nn.sigmoid(gp) * up).astype(BF16))
        cur = nxt
    return _dot(jnp.concatenate(acts, axis=1), wo(slice(0, dff)))


def _mixer_ffn_kernel(ap_ref, as_ref, xp_ref, xs_ref, mod_ref, modn_ref, wmix_ref, wi_ref, wo_ref, ln_ref,
                      xo_ref, uo_ref, *, n_prompt_tiles):
    is_prompt = pl.program_id(0) < n_prompt_tiles
    a = jnp.where(is_prompt, ap_ref[...], as_ref[...])
    x = jnp.where(is_prompt, xp_ref[...], xs_ref[...])
    z = DEEPNORM_ALPHA * x + mod_ref[0, 2:3, :] * _dot(a, wmix_ref[...])
    x1 = _layer_norm(z, ln_ref[0:1, :], ln_ref[1:2, :])
    u1 = (x1 * (1.0 + mod_ref[0, 4:5, :]) + mod_ref[0, 3:4, :]).astype(BF16)
    f = _swiglu_tile(u1, lambda c: wi_ref[:, c], lambda r: wo_ref[r, :], wo_ref.shape[0])
    x2 = _layer_norm(DEEPNORM_ALPHA * x1 + mod_ref[0, 5:6, :] * f, ln_ref[2:3, :], ln_ref[3:4, :])
    xo_ref[...] = x2
    uo_ref[...] = (x2 * (1.0 + modn_ref[0, 1:2, :]) + modn_ref[0, 0:1, :]).astype(BF16)


def _resident(shape):
    return pl.BlockSpec(shape, lambda i: (0,) * len(shape), pipeline_mode=pl.Buffered(1))


def mixer_ffn(a_p, a_s, xp, xs, mod, mod_next, w_mix, w_in, w_out, ln_g, ln_b, sample_len):
    n_prompt_tok, d = xp.shape
    t = n_prompt_tok + xs.shape[0]
    tm = FFN_TOKEN_TILE
    npt = n_prompt_tok // tm
    seg = functools.partial(_seg_of_tile, tile=tm, n_prompt_tok=n_prompt_tok, sample_len=sample_len)
    ln = jnp.stack([ln_g[0], ln_b[0], ln_g[1], ln_b[1]])
    return pl.pallas_call(
        functools.partial(_mixer_ffn_kernel, n_prompt_tiles=npt),
        grid=(t // tm,),
        in_specs=_split_token_specs(tm, d, npt) + _split_token_specs(tm, d, npt) + [
            pl.BlockSpec((1, 6, d), lambda i: (seg(i), 0, 0)),
            pl.BlockSpec((1, 6, d), lambda i: (seg(i), 0, 0)),
            _resident(w_mix.shape), _resident(w_in.shape), _resident(w_out.shape), _resident(ln.shape),
        ],
        out_specs=[pl.BlockSpec((tm, d), lambda i: (i, 0)), pl.BlockSpec((tm, d), lambda i: (i, 0))],
        out_shape=[jax.ShapeDtypeStruct((t, d), F32), jax.ShapeDtypeStruct((t, d), BF16)],
        compiler_params=_cparams("arbitrary"),
        name="mixer_ffn",
    )(a_p, a_s, xp, xs, mod, mod_next, w_mix, w_in, w_out, ln)


def _modulation_tables(c, c_ctx, ada_w, ada_b):
    n_s = c.shape[0]
    cvec = jnp.concatenate([c_ctx[None, :], c, jnp.zeros((8 - 1 - n_s, c.shape[1]), F32)], axis=0)
    mod = adaln(cvec, ada_w, ada_b)
    return mod[:, :1 + n_s, :].reshape(ada_w.shape[0], 1 + n_s, 6, c.shape[1])


def _layer0(xp, xs, mod0, mod1, state_c, state_n, state_m, ln_g, ln_b, w_in, b_gates, norm_g, w_out,
            ffn_w_in, ffn_w_out, n_prompt, prompt_len, n_sample, sample_len):
    n_prompt_tok = n_prompt * prompt_len
    n_main = 2 * M_HK + 2 * D_MODEL
    w_t = w_in.T
    qkv, ogate, g, gt = mlstm_inproj(xp, xs, mod0, w_t[:n_main].astype(BF16), w_t[n_main:].T.astype(BF16),
                                     w_t[n_main:].astype(BF16), b_gates.reshape(-1), sample_len)
    npc = n_prompt_tok // MLSTM_SAMPLE_CHUNK
    cps = sample_len // MLSTM_SAMPLE_CHUNK
    rep = lambda a: jnp.broadcast_to(a[..., None, :], a.shape[:-1] + (8, a.shape[-1]))
    n0 = rep(state_n)
    m0 = jnp.broadcast_to(state_m[..., None, None], state_m.shape + (8, 128))
    a_p, new_c, nf, mf = mlstm_prompt(qkv, g, gt, ogate, norm_g, n_prompt)
    hf = mlstm_sample(0, qkv, g, gt, state_c[:, 0], n0[:, 0], m0[:, 0], npc, cps)
    a_s = mlstm_sample(1, qkv, g, gt, state_c[:, 1], n0[:, 1], m0[:, 1], npc, cps,
                       hprev=hf, ogate=ogate, norm_g=norm_g)
    x2, u2 = mixer_ffn(a_p, a_s.reshape(-1, D_MODEL), xp, xs, mod0, mod1, w_out.astype(BF16),
                       ffn_w_in.astype(BF16), ffn_w_out.astype(BF16), ln_g, ln_b, sample_len)
    return x2, u2, new_c, nf[..., 0, :], mf[..., 0, 0]


def _qkv_kernel(u_ref, w_ref, qkv_ref, *kv_refs):
    tt, d = u_ref.shape
    u = u_ref[...]

    def head_major(out_ref, p):
        for h in range(NA_HEADS):
            out_ref[pl.ds(h, tt, stride=NA_HEADS), :] = p[:, h * NA_DH:(h + 1) * NA_DH]

    pq = _dot(u, w_ref[:, :d])
    pk = _dot(u, w_ref[:, d:2 * d])
    qkv_ref[:, :d] = pq.astype(BF16)
    pv = _dot(u, w_ref[:, 2 * d:])
    qkv_ref[:, d:2 * d] = pk.astype(BF16)
    if kv_refs:
        head_major(kv_refs[0], pk)
    qkv_ref[:, 2 * d:] = pv.astype(BF16)
    if kv_refs:
        head_major(kv_refs[1], pv)


def na_qkv(u, w, first_tok, n_tok, with_kv):
    d = u.shape[1]
    tt = TOKEN_TILE
    first = first_tok // tt
    out_specs = [pl.BlockSpec((tt, 3 * d), lambda i: (i, 0))]
    out_shape = [jax.ShapeDtypeStruct((n_tok, 3 * d), BF16)]
    if with_kv:
        out_specs += [pl.BlockSpec((tt * NA_HEADS, NA_DH), lambda i: (i, 0))] * 2
        out_shape += [jax.ShapeDtypeStruct((n_tok * NA_HEADS, NA_DH), F32)] * 2
    return pl.pallas_call(
        _qkv_kernel,
        grid=(n_tok // tt,),
        in_specs=[pl.BlockSpec((tt, d), lambda i: (first + i, 0)), pl.BlockSpec((d, 3 * d), lambda i: (0, 0))],
        out_specs=out_specs,
        out_shape=out_shape,
        compiler_params=_cparams("arbitrary"),
        name="na_qkv_prompt" if with_kv else "na_qkv_latent",
    )(u, w)


ATTN_LOG2E = math.log2(math.e)
ATTN_Q_SCALE = NA_DH ** -0.5 * ATTN_LOG2E


def _head_pair_masks():
    lane = lax.broadcasted_iota(jnp.int32, (1, 2 * NA_DH), 1)
    return (_onehot(lane < NA_DH), _onehot(lane >= NA_DH))


def _softmax_pv(score_blocks, value_blocks):
    mx = None
    for s in score_blocks:
        bm = jnp.max(s, axis=1, keepdims=True)
        mx = bm if mx is None else jnp.maximum(mx, bm)
    acc, denom = None, None
    for s, v in zip(score_blocks, value_blocks):
        p = jnp.exp2(s - mx)
        ps = jnp.sum(p, axis=1, keepdims=True)
        pv = _dot(p.astype(BF16), v)
        acc = pv if acc is None else acc + pv
        denom = ps if denom is None else denom + ps
    return acc / denom


def _ctx_attn_kernel(q_ref, k_ref, v_ref, o_ref):
    masks = _head_pair_masks()
    for hp in range(NA_HEADS // 2):
        sl = slice(hp * 2 * NA_DH, (hp + 1) * 2 * NA_DH)
        q2, k2, v2 = q_ref[:, sl], k_ref[:, sl], v_ref[:, sl]
        o2 = None
        for msk in masks:
            s = _dot_nt(q2 * msk, k2)
            o = _softmax_pv([s], [v2 * msk])
            o2 = o if o2 is None else o2 + o
        o_ref[:, sl] = o2.astype(o_ref.dtype)


def context_attention(qkv, n_prompt, prompt_len):
    t = n_prompt * prompt_len
    d = D_MODEL
    return pl.pallas_call(
        _ctx_attn_kernel,
        grid=(n_prompt,),
        in_specs=[
            pl.BlockSpec((prompt_len, d), lambda b: (b, 0)),
            pl.BlockSpec((prompt_len, d), lambda b: (b, 1)),
            pl.BlockSpec((prompt_len, d), lambda b: (b, 2)),
        ],
        out_specs=pl.BlockSpec((prompt_len, d), lambda b: (b, 0)),
        out_shape=jax.ShapeDtypeStruct((t, d), BF16),
        compiler_params=_cparams("arbitrary"),
        name="context_attention",
    )(qkv, qkv, qkv)


def _na_kernel(tile_ref, q_ref, kp_ref, kc_ref, kn_ref, vp_ref, vc_ref, vn_ref, kctx_ref, vctx_ref, bias_ref,
               o_ref):
    j = pl.program_id(1)
    nblk = pl.num_programs(1)
    m = q_ref.shape[0]
    halo = NA_HALO * GRID_W
    n_pairs = (NA_QROWS + 2 * NA_HALO) // 2
    variant = jnp.where(j == 0, 0, jnp.where(j == nblk - 1, 2, 1))
    kinds = [[tile_ref[(variant * NA_QROWS + rq) * n_pairs + kp] for kp in range(n_pairs)]
             for rq in range(NA_QROWS)]

    def bias_of(head):
        return jnp.concatenate(
            [jnp.concatenate([bias_ref[kinds[rq][kp], head] for kp in range(n_pairs)], axis=1)
             for rq in range(NA_QROWS)], axis=0)

    masks = _head_pair_masks()
    for hp in range(NA_HEADS // 2):
        sl = slice(hp * 2 * NA_DH, (hp + 1) * 2 * NA_DH)
        q2 = q_ref[:, sl]
        kloc = jnp.concatenate([kp_ref[m - halo:, sl], kc_ref[:, sl], kn_ref[:halo, sl]], axis=0)
        vloc = jnp.concatenate([vp_ref[m - halo:, sl], vc_ref[:, sl], vn_ref[:halo, sl]], axis=0)
        kctx = kctx_ref[0][:, sl]
        vctx = vctx_ref[0][:, sl]
        o2 = None
        for half, msk in enumerate(masks):
            qm = q2 * msk
            s_loc = _dot_nt(qm, kloc) + bias_of(2 * hp + half)
            s_ctx = _dot_nt(qm, kctx)
            o = _softmax_pv([s_loc, s_ctx], [vloc * msk, vctx * msk])
            o2 = o if o2 is None else o2 + o
        o_ref[:, sl] = o2.astype(o_ref.dtype)


def _na_bias_kernel(rpb_ref, o_ref, *, kinds, n_heads):
    nc, npos = rpb_ref.shape[1], o_ref.shape[2]
    pair_shift = (2 * GRID_W).bit_length() - 1
    c = lax.broadcasted_iota(jnp.int32, (nc, npos), 0)
    pos = lax.broadcasted_iota(jnp.int32, (nc, npos), 1)
    qc = pos >> pair_shift
    kc = pos & (GRID_W - 1)
    c_start = jnp.clip(qc - NA_COLS // 2, 0, GRID_W - NA_COLS)
    col_in = jnp.logical_and(kc >= c_start, kc < c_start + NA_COLS)
    dc = jnp.clip(kc - qc + NA_COLS - 1, 0, 2 * NA_COLS - 2)
    hit = jnp.logical_and(c == dc, col_in)
    right = (pos & GRID_W) != 0
    r1, r2, r3 = _split3(rpb_ref[...] * ATTN_LOG2E)

    def placed(side):
        sel = _onehot(jnp.logical_and(hit, right == side))
        return _dot(r1, sel) + _dot(r2, sel) + _dot(r3, sel)

    left, rght = placed(False), placed(True)
    rows = lambda v, dr: v[dr * n_heads:(dr + 1) * n_heads]
    for k, (dl, dr) in enumerate(kinds):
        shown = col_in[0:1, :]
        if dl is None or dr is None:
            side_ok = jnp.zeros_like(shown) if dl is None and dr is None else (right[0:1, :] == (dl is None))
            shown = jnp.logical_and(shown, side_ok)
        parts = ([rows(left, dl)] if dl is not None else []) + ([rows(rght, dr)] if dr is not None else [])
        val = sum(parts) if parts else jnp.zeros((n_heads, npos), F32)
        o_ref[k] = jnp.where(shown, val, NEG)


def _na_bias_plan():
    ndr = 2 * NA_ROWS - 1
    assert NA_QROWS >= NA_ROWS // 2 and NA_HALO == NA_ROWS // 2 and (NA_QROWS + 2 * NA_HALO) % 2 == 0
    ok = lambda dr: dr if dr is not None and 0 <= dr < ndr else None
    kinds = [(ok(dr), ok(dr + 1)) for dr in range(-1, ndr)]
    first_key_row = (lambda rq: NA_HALO, lambda rq: rq, lambda rq: NA_QROWS + NA_HALO - NA_ROWS)
    table = []
    for first in first_key_row:
        for rq in range(NA_QROWS):
            for kp in range((NA_QROWS + 2 * NA_HALO) // 2):
                drs = []
                for kk in (2 * kp, 2 * kp + 1):
                    visible = first(rq) <= kk < first(rq) + NA_ROWS
                    drs.append(kk - rq + NA_ROWS - 1 - NA_HALO if visible else None)
                kind = (drs[0], drs[1])
                if kind not in kinds:
                    kinds.append(kind)
                table.append(kinds.index(kind))
    return kinds, table


def _na_bias_tiles(rpb):
    nh, ndr, ndc = rpb.shape
    nc = 32
    kinds, table = _na_bias_plan()
    nk = len(kinds)
    rows = jnp.pad(jnp.transpose(rpb, (1, 0, 2)), ((0, 0), (0, 0), (0, nc - ndc))).reshape(ndr * nh, nc)
    npos = GRID_W * 2 * GRID_W
    tiles = pl.pallas_call(
        functools.partial(_na_bias_kernel, kinds=tuple(kinds), n_heads=nh),
        out_shape=jax.ShapeDtypeStruct((nk, nh, npos), F32),
        compiler_params=pltpu.CompilerParams(vmem_limit_bytes=VMEM_LIMIT_BYTES),
        name="na_bias",
    )(rows)
    return tiles.reshape(nk, nh, GRID_W, 2 * GRID_W), jnp.asarray(table, jnp.int32)


def neighbourhood_attention(qkv, k_ctx, v_ctx, rpb, n_sample, sample_len):
    d = D_MODEL
    m = NA_QROWS * GRID_W
    nblk = sample_len // m
    assert nblk >= 3
    bias, table = _na_bias_tiles(rpb)

    def blk(col, off):
        return pl.BlockSpec((m, d), lambda b, j, tbl: (b * nblk + jnp.clip(j + off, 0, nblk - 1), col))

    return pl.pallas_call(
        _na_kernel,
        grid_spec=pltpu.PrefetchScalarGridSpec(
            num_scalar_prefetch=1,
            grid=(n_sample, nblk),
            in_specs=[
                blk(0, 0), blk(1, -1), blk(1, 0), blk(1, 1), blk(2, -1), blk(2, 0), blk(2, 1),
                pl.BlockSpec((1,) + k_ctx.shape[1:], lambda b, j, tbl: (b, 0, 0)),
                pl.BlockSpec((1,) + v_ctx.shape[1:], lambda b, j, tbl: (b, 0, 0)),
                pl.BlockSpec(bias.shape, lambda b, j, tbl: (0, 0, 0, 0)),
            ],
            out_specs=pl.BlockSpec((m, d), lambda b, j, tbl: (b * nblk + j, 0)),
        ),
        out_shape=jax.ShapeDtypeStruct((n_sample * sample_len, d), BF16),
        compiler_params=_cparams("arbitrary", "arbitrary"),
        name="neighbourhood_attention",
    )(table, qkv, qkv, qkv, qkv, qkv, qkv, qkv, k_ctx, v_ctx, bias)


MOE_ROW_TILE = 512
SC_CORES = 2
SC_SUBCORES = 16
SC_GATHER_CHUNK = 64

def _route_tokens(u, wt, carry):
    w1, w2, w3 = _split3(wt)
    lg = _dot_nt(w1, u) + _dot_nt(w2, u) + _dot_nt(w3, u)
    ne, nt = lg.shape
    e = lax.broadcasted_iota(jnp.int32, lg.shape, 0)
    m1 = jnp.max(lg, axis=0, keepdims=True)
    i1 = jnp.min(jnp.where(lg == m1, e, ne), axis=0, keepdims=True)
    sel1 = e == i1
    lg2 = jnp.where(sel1, -jnp.inf, lg)
    m2 = jnp.max(lg2, axis=0, keepdims=True)
    i2 = jnp.min(jnp.where(lg2 == m2, e, ne), axis=0, keepdims=True)
    sel2 = e == i2
    ex = jnp.exp(m2 - m1)
    wa = 1.0 / (1.0 + ex)
    wb = ex / (1.0 + ex)
    comb = jnp.where(sel1, wa, jnp.where(sel2, wb, 0.0))
    assign = jnp.logical_or(sel1, sel2)
    a = jnp.where(assign, 1.0, 0.0)
    row = lax.broadcasted_iota(jnp.int32, (nt, nt), 0)
    col = lax.broadcasted_iota(jnp.int32, (nt, nt), 1)
    tri = _onehot(row < col)
    rank = _dot(a.astype(BF16), tri) + carry
    pos = jnp.where(assign, rank, -1.0)
    return comb, pos, carry + jnp.sum(a, axis=1, keepdims=True)


def _moe_plan(totals, pos, comb, n_tok):
    ne = totals.shape[0]
    tm = MOE_ROW_TILE
    n_tiles = (2 * n_tok) // tm + ne
    tiles_per = (totals + tm - 1) // tm
    tile_end = jnp.cumsum(tiles_per)
    n_used = tile_end[-1]
    base_rows = (tile_end - tiles_per) * tm
    ti = jnp.arange(n_tiles, dtype=jnp.int32)
    tile_expert = jnp.minimum(jnp.sum((ti[:, None] >= tile_end[None, :]).astype(jnp.int32), axis=1), ne - 1)
    last_expert = tile_expert[jnp.maximum(n_used - 1, 0)]
    tile_expert = jnp.where(ti < n_used, tile_expert, last_expert)
    routed = pos >= 0.0
    row = base_rows[:, None] + pos.astype(jnp.int32)
    row_a = jnp.min(jnp.where(routed, row, n_tiles * tm), axis=0)
    row_b = jnp.max(jnp.where(routed, row, -1), axis=0)
    w_a = jnp.sum(jnp.where(jnp.logical_and(routed, row == row_a[None, :]), comb, 0.0), axis=0)
    w_b = jnp.sum(jnp.where(jnp.logical_and(routed, row == row_b[None, :]), comb, 0.0), axis=0)
    tile_rows = jnp.clip(totals[tile_expert] - (ti - (tile_end - tiles_per)[tile_expert]) * tm, 0, tm)
    tile_rows = jnp.where(ti < n_used, tile_rows, 0)
    return dict(n_rows=n_tiles * tm, tile_expert=tile_expert.astype(jnp.int32),
                n_used=n_used.reshape(1).astype(jnp.int32), tile_rows=tile_rows.astype(jnp.int32),
                token_rows=jnp.concatenate([row_a, row_b]).astype(jnp.int32),
                token_weights=jnp.concatenate([w_a[None], w_b[None], jnp.zeros((6, n_tok), F32)], axis=0))


def _pack_bf16_pairs(x):
    half = x.shape[1] // 2
    bits = pltpu.bitcast(x.astype(BF16).astype(F32), jnp.int32)
    return (bits[:, :half] & jnp.int32(-65536)) | lax.shift_right_logical(bits[:, half:], jnp.int32(16))


def _unpack_bf16_pairs(p):
    hi = pltpu.bitcast(p & jnp.int32(-65536), F32)
    lo = pltpu.bitcast(lax.shift_left(p, jnp.int32(16)), F32)
    return jnp.concatenate([hi, lo], axis=1)


def sc_gather_rows(table, idx):
    _, w = table.shape
    b = idx.shape[0]
    workers = SC_CORES * SC_SUBCORES
    ch = SC_GATHER_CHUNK
    n_chunks = b // (workers * ch)
    assert n_chunks * workers * ch == b
    mesh = plsc.VectorSubcoreMesh(core_axis_name="c", subcore_axis_name="s", num_cores=SC_CORES,
                                  num_subcores=SC_SUBCORES)

    @functools.partial(
        pl.kernel, mesh=mesh, out_type=jax.ShapeDtypeStruct((b, w), table.dtype),
        scratch_types=[pltpu.VMEM((ch,), jnp.int32)] * 2 + [pltpu.VMEM((ch, w), table.dtype)] * 2
        + [pltpu.SemaphoreType.DMA] * 2)
    def gather(table_hbm, idx_hbm, out_hbm, idx0, idx1, rows0, rows1, sem0, sem1):
        idx_v, rows_v, sems = (idx0, idx1), (rows0, rows1), (sem0, sem1)
        worker = lax.axis_index("s") * SC_CORES + lax.axis_index("c")
        base = lambda c: (worker * n_chunks + c) * ch

        def start(c):
            slot = c % 2
            pltpu.sync_copy(idx_hbm.at[pl.ds(base(c), ch)], idx_v[slot])
            return pltpu.async_copy(table_hbm.at[idx_v[slot]], rows_v[slot], sems[slot])

        in_flight = start(0)
        for c in range(n_chunks):
            following = start(c + 1) if c + 1 < n_chunks else None
            in_flight.wait()
            pltpu.sync_copy(rows_v[c % 2], out_hbm.at[pl.ds(base(c), ch)])
            in_flight = following

    return gather(table, idx)


def sc_scatter_rows(rows, idx, n_out):
    t, w = rows.shape
    copies = idx.shape[0] // t
    workers = SC_CORES * SC_SUBCORES
    ch = SC_GATHER_CHUNK
    n_chunks = t // (workers * ch)
    assert n_chunks * workers * ch == t and copies * t == idx.shape[0]
    mesh = plsc.VectorSubcoreMesh(core_axis_name="c", subcore_axis_name="s", num_cores=SC_CORES,
                                  num_subcores=SC_SUBCORES)

    @functools.partial(
        pl.kernel, mesh=mesh, out_type=jax.ShapeDtypeStruct((n_out, w), rows.dtype),
        scratch_types=[pltpu.VMEM((ch,), jnp.int32)] * (2 * copies) + [pltpu.VMEM((ch, w), rows.dtype)] * 2
        + [pltpu.SemaphoreType.DMA] * 2)
    def scatter(rows_hbm, idx_hbm, out_hbm, *scratch):
        idx_v = (scratch[:copies], scratch[copies:2 * copies])
        rows_v = scratch[2 * copies:2 * copies + 2]
        sems = scratch[2 * copies + 2:]
        worker = lax.axis_index("s") * SC_CORES + lax.axis_index("c")
        base = lambda c: (worker * n_chunks + c) * ch

        def load(c):
            slot = c % 2
            pltpu.sync_copy(rows_hbm.at[pl.ds(base(c), ch)], rows_v[slot])
            for k in range(copies):
                pltpu.sync_copy(idx_hbm.at[pl.ds(k * t + base(c), ch)], idx_v[slot][k])

        load(0)
        for c in range(n_chunks):
            slot = c % 2
            writes = [pltpu.async_copy(rows_v[slot], out_hbm.at[idx_v[slot][k]], sems[slot]) for k in range(copies)]
            if c + 1 < n_chunks:
                load(c + 1)
            for write in writes:
                write.wait()

    return scatter(rows, idx)


def _gffn_kernel(te_ref, nu_ref, nr_ref, x_ref, wi_ref, wo_ref, y_ref):
    i = pl.program_id(0)
    used = i < nu_ref[0]

    @pl.when(used)
    def _():
        packed = x_ref[...]
        row = lax.broadcasted_iota(jnp.int32, packed.shape, 0)
        x = _unpack_bf16_pairs(jnp.where(row < nr_ref[i], packed, 0)).astype(BF16)
        y = _swiglu_tile(x, lambda c: wi_ref[0, :, c], lambda r: wo_ref[0, r, :], wo_ref.shape[1])
        y_ref[...] = _pack_bf16_pairs(y)

    @pl.when(jnp.logical_not(used))
    def _():
        y_ref[...] = jnp.zeros_like(y_ref)


def moe_grouped_ffn(xs, w_in, w_out, plan):
    nr, half = xs.shape
    ne, dff, d = w_out.shape
    tm = MOE_ROW_TILE
    return pl.pallas_call(
        _gffn_kernel,
        grid_spec=pltpu.PrefetchScalarGridSpec(
            num_scalar_prefetch=3,
            grid=(nr // tm,),
            in_specs=[
                pl.BlockSpec((tm, half), lambda i, te, nu, tr: (jnp.minimum(i, nu[0] - 1), 0)),
                pl.BlockSpec((1, d, 2 * dff), lambda i, te, nu, tr: (te[i], 0, 0)),
                pl.BlockSpec((1, dff, d), lambda i, te, nu, tr: (te[i], 0, 0)),
            ],
            out_specs=pl.BlockSpec((tm, half), lambda i, te, nu, tr: (i, 0)),
        ),
        out_shape=jax.ShapeDtypeStruct((nr, half), jnp.int32),
        compiler_params=_cparams("arbitrary"),
        name="moe_grouped_ffn",
    )(plan["tile_expert"], plan["n_used"], plan["tile_rows"], xs, w_in, w_out)


def _combine_kernel(ya_ref, yb_ref, w_ref, x_ref, mod_ref, lng_ref, lnb_ref, op_ref, os_ref, *, n_prompt_tiles):
    i = pl.program_id(0)
    w = w_ref[...].T
    moe = w[:, 0:1] * _unpack_bf16_pairs(ya_ref[...]) + w[:, 1:2] * _unpack_bf16_pairs(yb_ref[...])
    z = DEEPNORM_ALPHA * x_ref[...] + mod_ref[0, 5:6, :] * moe
    xn = _layer_norm(z, lng_ref[...], lnb_ref[...])

    @pl.when(i < n_prompt_tiles)
    def _():
        op_ref[...] = xn

    @pl.when(i >= n_prompt_tiles)
    def _():
        os_ref[...] = xn


def moe_combine(y_tok, weights, x, mod, ln_g, ln_b, n_prompt_tok, sample_len):
    t, d = x.shape
    tt = TOKEN_TILE
    nt = t // tt
    npt = n_prompt_tok // tt
    seg = functools.partial(_seg_of_tile, tile=tt, n_prompt_tok=n_prompt_tok, sample_len=sample_len)
    return pl.pallas_call(
        functools.partial(_combine_kernel, n_prompt_tiles=npt),
        grid=(nt,),
        in_specs=[
            pl.BlockSpec((tt, d // 2), lambda i: (i, 0)),
            pl.BlockSpec((tt, d // 2), lambda i: (nt + i, 0)),
            pl.BlockSpec((8, tt), lambda i: (0, i)),
            pl.BlockSpec((tt, d), lambda i: (i, 0)),
            pl.BlockSpec((1, 6, d), lambda i: (seg(i), 0, 0)),
            pl.BlockSpec((1, d), lambda i: (0, 0)),
            pl.BlockSpec((1, d), lambda i: (0, 0)),
        ],
        out_specs=[
            pl.BlockSpec((tt, d), lambda i: (jnp.minimum(i, npt - 1), 0)),
            pl.BlockSpec((tt, d), lambda i: (jnp.maximum(i - npt, 0), 0)),
        ],
        out_shape=[jax.ShapeDtypeStruct((n_prompt_tok, d), F32), jax.ShapeDtypeStruct((t - n_prompt_tok, d), F32)],
        compiler_params=_cparams("arbitrary"),
        name="moe_combine",
    )(y_tok, y_tok, weights, x, mod, ln_g.reshape(1, d), ln_b.reshape(1, d))


def _layer1(x, u, mod1, cache_k, cache_v, ln_g, ln_b, w_qkv, rpb, w_out, w_router, moe_w_in, moe_w_out,
            n_prompt, prompt_len, n_sample, sample_len):
    n_prompt_tok = n_prompt * prompt_len
    d = D_MODEL
    col_scale = jnp.where(jnp.arange(3 * d) < d, ATTN_Q_SCALE, 1.0).astype(F32)
    w_qkv = (w_qkv * col_scale[None, :]).astype(BF16)
    qkv_p, k_p, v_p = na_qkv(u, w_qkv, 0, n_prompt_tok, with_kv=True)
    qkv_s, = na_qkv(u, w_qkv, n_prompt_tok, u.shape[0] - n_prompt_tok, with_kv=False)
    attn_p = context_attention(qkv_p, n_prompt, prompt_len)
    k_ctx = cache_k.reshape(n_sample, -1, d).astype(BF16)
    v_ctx = cache_v.reshape(n_sample, -1, d).astype(BF16)
    attn_s = neighbourhood_attention(qkv_s, k_ctx, v_ctx, rpb, n_sample, sample_len)
    moe_w_in, moe_w_out = moe_w_in.astype(BF16), moe_w_out.astype(BF16)
    x1, u1_packed, comb, pos, totals = mixer_outproj_router(
        attn_p, attn_s, x, mod1, w_out.astype(BF16), ln_g[0], ln_b[0], w_router, sample_len,
        run_after=(moe_w_in, moe_w_out))
    plan = _moe_plan(totals[:, 0].astype(jnp.int32), pos, comb, x.shape[0])
    xs = sc_scatter_rows(u1_packed, plan["token_rows"], plan["n_rows"])
    y = moe_grouped_ffn(xs, moe_w_in, moe_w_out, plan)
    y_tok = sc_gather_rows(y, plan["token_rows"])
    y_p, y_s = moe_combine(y_tok, plan["token_weights"], x1, mod1, ln_g[1], ln_b[1], n_prompt_tok, sample_len)
    return y_p, y_s, k_p, v_p


def kernel(x_prompt, x_sample, state_mlstm_C, state_mlstm_n, state_mlstm_m, cache_na_k, cache_na_v, c, c_ctx,
           ada_w, ada_b, ln_g, ln_b, mlstm_w_in, mlstm_b_gates, mlstm_norm_g, mlstm_w_out, na_w_qkv, na_rpb,
           na_w_out, ffn_w_in, ffn_w_out, moe_w_router, moe_w_in, moe_w_out):
    n_prompt, prompt_len, d = x_prompt.shape
    n_sample, sample_len, _ = x_sample.shape
    assert d == D_MODEL and prompt_len == MLSTM_CHUNK and sample_len % MLSTM_SAMPLE_CHUNK == 0
    assert ada_w.shape[0] == DEPTH == 2
    mod = _modulation_tables(c, c_ctx, ada_w, ada_b)
    x2, u2, new_c, new_n, new_m = _layer0(
        x_prompt.reshape(-1, d), x_sample.reshape(-1, d), mod[0], mod[1],
        state_mlstm_C[:, 0], state_mlstm_n[:, 0], state_mlstm_m[:, 0], ln_g[0], ln_b[0], mlstm_w_in[0],
        mlstm_b_gates[0], mlstm_norm_g[0], mlstm_w_out[0], ffn_w_in[0], ffn_w_out[0],
        n_prompt, prompt_len, n_sample, sample_len)
    y_p, y_s, k_p, v_p = _layer1(
        x2, u2, mod[1], cache_na_k[:, 0], cache_na_v[:, 0], ln_g[1], ln_b[1], na_w_qkv[0], na_rpb[0], na_w_out[0],
        moe_w_router[0], moe_w_in[0], moe_w_out[0], n_prompt, prompt_len, n_sample, sample_len)
    kv_shape = (n_prompt, 1, prompt_len, NA_HEADS, NA_DH)
    return (y_p.reshape(x_prompt.shape), y_s.reshape(x_sample.shape), new_c, new_n, new_m,
            k_p.reshape(kv_shape), v_p.reshape(kv_shape))
```

```python
import functools
import math

import jax
import jax.numpy as jnp
from jax import lax
from jax.experimental import pallas as pl
from jax.experimental.pallas import tpu as pltpu
from jax.experimental.pallas import tpu_sc as plsc

F32 = jnp.float32
BF16 = jnp.bfloat16

D_MODEL = 1024
DEPTH = 2
GRID_W = 64
M_HEADS = 4
M_DK = 128
M_DV = 256
M_HK = M_HEADS * M_DK
NA_HEADS = 16
NA_DH = 64
NA_ROWS = 8
NA_COLS = 16
DEEPNORM_ALPHA = (2.0 * DEPTH) ** 0.25
LN_EPS = 1e-5
NEG = -1e30

VMEM_LIMIT_BYTES = 56 * 1024 * 1024

MLSTM_CHUNK = 256
MLSTM_SAMPLE_CHUNK = 256
TOKEN_TILE = 512
FFN_TOKEN_TILE = 512
FFN_SUB = 256
NA_QROWS = 4
NA_HALO = 4


def _cparams(*sem):
    return pltpu.CompilerParams(dimension_semantics=sem, vmem_limit_bytes=VMEM_LIMIT_BYTES)


def _dot(a, b):
    return jnp.dot(a, b, preferred_element_type=F32)


def _dot_nt(a, b):
    return lax.dot_general(a, b, (((1,), (1,)), ((), ())), preferred_element_type=F32)


def _onehot(mask):
    return jnp.where(mask, 1.0, 0.0).astype(BF16)


def _split3(x):
    hi = x.astype(BF16)
    r = x - hi.astype(F32)
    mid = r.astype(BF16)
    lo = (r - mid.astype(F32)).astype(BF16)
    return hi, mid, lo


def _layer_norm(z, g, b):
    mu = jnp.mean(z, axis=-1, keepdims=True)
    zc = z - mu
    var = jnp.mean(zc * zc, axis=-1, keepdims=True)
    return zc * lax.rsqrt(var + LN_EPS) * g + b


def _log_sigmoid(x):
    return jnp.minimum(x, 0.0) - jnp.log(1.0 + jnp.exp(-jnp.abs(x)))


def _seg_of_tile(i, tile, n_prompt_tok, sample_len):
    tok = i * tile
    return jnp.where(tok < n_prompt_tok, 0, 1 + (tok - n_prompt_tok) // sample_len)


def _adaln_kernel(c_ref, w_ref, b_ref, o_ref):
    c = c_ref[...]
    a = (c * jax.nn.sigmoid(c)).astype(BF16)
    o_ref[0] = _dot(a, w_ref[0].astype(BF16)) + b_ref[0]


def adaln(cvec, ada_w, ada_b):
    depth, d, n = ada_w.shape
    tn = 1536
    return pl.pallas_call(
        _adaln_kernel,
        grid=(depth, n // tn),
        in_specs=[
            pl.BlockSpec((8, d), lambda l, j: (0, 0)),
            pl.BlockSpec((1, d, tn), lambda l, j: (l, 0, j)),
            pl.BlockSpec((1, 1, tn), lambda l, j: (l, 0, j)),
        ],
        out_specs=pl.BlockSpec((1, 8, tn), lambda l, j: (l, 0, j)),
        out_shape=jax.ShapeDtypeStruct((depth, 8, n), F32),
        compiler_params=_cparams("arbitrary", "arbitrary"),
        name="adaln",
    )(cvec, ada_w, ada_b.reshape(depth, 1, n))


def _inproj_kernel(xp_ref, xs_ref, mod_ref, w_ref, wg_ref, wgt_ref, bg_ref, bgt_ref,
                   qkv_ref, o_ref, g_ref, gt_ref, *, n_prompt_tiles):
    x = jnp.where(pl.program_id(0) < n_prompt_tiles, xp_ref[...], xs_ref[...])
    u = (x * (1.0 + mod_ref[0, 1:2, :]) + mod_ref[0, 0:1, :]).astype(BF16)
    nqkv = qkv_ref.shape[1]
    half = nqkv // 2
    g = _dot(u, wg_ref[...]) + bg_ref[...]
    gt = _dot_nt(wgt_ref[...], u) + bgt_ref[...]
    p0 = _dot_nt(u, w_ref[:half, :])
    col = lax.broadcasted_iota(jnp.int32, g.shape, 1)
    g_ref[...] = jnp.where(((col >> 2) & 1) == 1, _log_sigmoid(g), g)
    p1 = _dot_nt(u, w_ref[half:nqkv, :])
    qkv_ref[:, :half] = p0.astype(BF16)
    row = lax.broadcasted_iota(jnp.int32, gt.shape, 0)
    gt_ref[...] = jnp.where(((row >> 2) & 1) == 1, _log_sigmoid(gt), gt)
    p2 = _dot_nt(u, w_ref[nqkv:, :])
    qkv_ref[:, half:] = p1.astype(BF16)
    o_ref[...] = p2


def _split_token_specs(tile, d, n_prompt_tiles):
    return [pl.BlockSpec((tile, d), lambda i: (jnp.minimum(i, n_prompt_tiles - 1), 0)),
            pl.BlockSpec((tile, d), lambda i: (jnp.maximum(i - n_prompt_tiles, 0), 0))]


def mlstm_inproj(xp, xs, mod, w_main, w_gate, w_gate_t, b_gate, sample_len):
    n_prompt_tok, d = xp.shape
    t = n_prompt_tok + xs.shape[0]
    n_main = w_main.shape[0]
    n_qkv = 2 * M_HK + D_MODEL
    ng = w_gate.shape[1]
    tt = TOKEN_TILE
    npt = n_prompt_tok // tt
    seg = functools.partial(_seg_of_tile, tile=tt, n_prompt_tok=n_prompt_tok, sample_len=sample_len)
    return pl.pallas_call(
        functools.partial(_inproj_kernel, n_prompt_tiles=npt),
        grid=(t // tt,),
        in_specs=_split_token_specs(tt, d, npt) + [
            pl.BlockSpec((1, 6, d), lambda i: (seg(i), 0, 0)),
            pl.BlockSpec((n_main, d), lambda i: (0, 0)),
            pl.BlockSpec((d, ng), lambda i: (0, 0)),
            pl.BlockSpec((ng, d), lambda i: (0, 0)),
            pl.BlockSpec((1, ng), lambda i: (0, 0)),
            pl.BlockSpec((ng, 1), lambda i: (0, 0)),
        ],
        out_specs=[
            pl.BlockSpec((tt, n_qkv), lambda i: (i, 0)),
            pl.BlockSpec((tt, n_main - n_qkv), lambda i: (i, 0)),
            pl.BlockSpec((tt, ng), lambda i: (i, 0)),
            pl.BlockSpec((ng, tt), lambda i: (0, i)),
        ],
        out_shape=[
            jax.ShapeDtypeStruct((t, n_qkv), BF16),
            jax.ShapeDtypeStruct((t, n_main - n_qkv), F32),
            jax.ShapeDtypeStruct((t, ng), F32),
            jax.ShapeDtypeStruct((ng, t), F32),
        ],
        compiler_params=_cparams("arbitrary"),
        name="mlstm_inproj",
    )(xp, xs, mod, w_main, w_gate, w_gate_t, b_gate.reshape(1, ng), b_gate.reshape(ng, 1))


def _mlstm_gate_sums(g, gt, backward):
    L = g.shape[0]
    row = lax.broadcasted_iota(jnp.int32, (L, L), 0)
    col = lax.broadcasted_iota(jnp.int32, (L, L), 1)
    lower = col <= row
    upper = col >= row
    tri_col = _onehot(upper if backward else lower)
    tri_row = _onehot(lower if backward else upper)
    ghi, gmid, glo = _split3(g)
    b_cols = _dot(tri_col, ghi) + _dot(tri_col, gmid) + _dot(tri_col, glo)
    thi, tmid, tlo = _split3(gt)
    b_rows = _dot(thi, tri_row) + _dot(tmid, tri_row) + _dot(tlo, tri_row)
    return b_cols, b_rows


def _mlstm_heads(items):
    scale = M_DK ** -0.5
    log_scale = math.log(scale)
    L = items[0][0].shape[0]
    row = lax.broadcasted_iota(jnp.int32, (L, L), 0)
    col = lax.broadcasted_iota(jnp.int32, (L, L), 1)
    masks = {False: col <= row, True: col >= row}
    n = range(len(items))
    qh, kh, vh, li_row, lf_row, b_row, b_col, backward, state = zip(*items)
    has_state = [st is not None for st in state]
    m_prev = [st[2] if st is not None else 0.0 for st in state]

    qk = [_dot_nt(qh[i], kh[i]) for i in n]
    qc = [_dot(qh[i], state[i][0].astype(BF16)) if has_state[i] else None for i in n]
    qn = [_dot_nt(qh[i], state[i][1].astype(BF16))[:, 0:1] if has_state[i] else None for i in n]
    total = [jnp.sum(lf_row[i], axis=1, keepdims=True) for i in n]
    d = [jnp.where(masks[backward[i]], b_col[i] + (li_row[i] - b_row[i] + log_scale), NEG) for i in n]
    inter = [b_col[i] + m_prev[i] if has_state[i] else b_col[i] for i in n]
    m_t = [jnp.maximum(inter[i], jnp.max(d[i], axis=1, keepdims=True) - log_scale) for i in n]
    p = [jnp.exp(d[i] - m_t[i]) for i in n]
    s = [qk[i] * p[i] for i in n]
    den = [jnp.sum(s[i], axis=1, keepdims=True) for i in n]
    num = [_dot(s[i].astype(BF16), vh[i]) for i in n]
    a = [jnp.exp(inter[i] - m_t[i]) * scale if has_state[i] else None for i in n]
    num = [a[i] * qc[i] + num[i] if has_state[i] else num[i] for i in n]
    den = [a[i] * qn[i] + den[i] if has_state[i] else den[i] for i in n]
    hh = [num[i] / jnp.maximum(jnp.abs(den[i]), jnp.exp(-m_t[i])) for i in n]

    g_row = [total[i] - b_row[i] + li_row[i] for i in n]
    m_new = [jnp.maximum(total[i] + m_prev[i], jnp.max(g_row[i], axis=1, keepdims=True)) for i in n]
    ws_row = [jnp.exp(g_row[i] - m_new[i]) for i in n]
    kt = [kh[i].astype(F32).T for i in n]
    c_new = [_dot((kt[i] * ws_row[i]).astype(BF16), vh[i]) for i in n]
    n_new = [_dot(jnp.broadcast_to(ws_row[i], (8, L)).astype(BF16), kh[i]) for i in n]
    a0 = [jnp.exp(total[i] + m_prev[i] - m_new[i]) if has_state[i] else None for i in n]
    c_new = [a0[i] * state[i][0] + c_new[i] if has_state[i] else c_new[i] for i in n]
    n_new = [a0[i] * state[i][1] + n_new[i] if has_state[i] else n_new[i] for i in n]
    return hh, c_new, n_new, m_new


def _head_norm_gate(ht, norm_g, ogate):
    mu = jnp.mean(ht, axis=-1, keepdims=True)
    hc = ht - mu
    var = jnp.mean(hc * hc, axis=-1, keepdims=True)
    return (hc * lax.rsqrt(var + LN_EPS) * norm_g * jax.nn.sigmoid(ogate)).astype(BF16)


def _head_operands(q_ref, k_ref, v_ref, h):
    return (q_ref[:, h * M_DK:(h + 1) * M_DK], k_ref[:, h * M_DK:(h + 1) * M_DK], v_ref[:, h * M_DV:(h + 1) * M_DV])


def _gate_operands(gt, b_rows, b_cols, direction, h):
    ci = direction * 8 + h
    cf = direction * 8 + 4 + h
    return gt[ci:ci + 1, :], gt[cf:cf + 1, :], b_rows[cf:cf + 1, :], b_cols[:, cf:cf + 1]


def _mlstm_prompt_kernel(q_ref, k_ref, v_ref, g_ref, gt_ref, o_ref, ng_ref, a_ref, cf_ref, nf_ref, mf_ref):
    g, gt = g_ref[...], gt_ref[...]
    sums = [_mlstm_gate_sums(g, gt, direction == 1) for direction in (0, 1)]
    keys = [(direction, h) for direction in (0, 1) for h in range(M_HEADS)]
    items = [_head_operands(q_ref, k_ref, v_ref, h)
             + _gate_operands(gt, sums[direction][1], sums[direction][0], direction, h) + (direction == 1, None)
             for direction, h in keys]
    hh, c_new, n_new, m_new = _mlstm_heads(items)
    for i, (direction, h) in enumerate(keys):
        cf_ref[0, 0, direction, h] = c_new[i]
        nf_ref[0, 0, direction, h] = n_new[i]
        mf_ref[0, 0, direction, h] = jnp.broadcast_to(m_new[i], mf_ref.shape[-2:])
    for h in range(M_HEADS):
        sl = slice(h * M_DV, (h + 1) * M_DV)
        a_ref[:, sl] = _head_norm_gate(hh[h] + hh[M_HEADS + h], ng_ref[:, sl], o_ref[:, sl])


def _mlstm_sample_kernel(*refs, direction, n_units):
    backward = direction == 1
    refs = list(refs)
    unit_refs = [tuple(refs.pop(0) for _ in range(5)) for _ in range(n_units)]
    c0_ref, n0_ref, m0_ref = refs.pop(0), refs.pop(0), refs.pop(0)
    if backward:
        hprev_ref = refs.pop(0)
        o_refs = [refs.pop(0) for _ in range(n_units)]
        ng_ref = refs.pop(0)
    out_ref, c_scr, n_scr, m_scr = refs

    @pl.when(pl.program_id(0) == 0)
    def _():
        c_scr[...] = c0_ref[...]
        n_scr[...] = n0_ref[...]
        m_scr[...] = m0_ref[...]

    keys, items = [], []
    for u, (q_ref, k_ref, v_ref, g_ref, gt_ref) in enumerate(unit_refs):
        gt = gt_ref[...]
        b_cols, b_rows = _mlstm_gate_sums(g_ref[...], gt, backward)
        for h in range(M_HEADS):
            state = (c_scr[u, h], n_scr[u, h], m_scr[u, h][0:1, 0:1])
            keys.append((u, h))
            items.append(_head_operands(q_ref, k_ref, v_ref, h) + _gate_operands(gt, b_rows, b_cols, direction, h)
                         + (backward, state))
    hh, c_new, n_new, m_new = _mlstm_heads(items)
    for i, (u, h) in enumerate(keys):
        c_scr[u, h] = c_new[i]
        n_scr[u, h] = n_new[i]
        m_scr[u, h] = jnp.broadcast_to(m_new[i], m_scr.shape[-2:])
        sl = slice(h * M_DV, (h + 1) * M_DV)
        if backward:
            out_ref[u, :, sl] = _head_norm_gate(hprev_ref[u, :, sl] + hh[i], ng_ref[:, sl], o_refs[u][:, sl])
        else:
            out_ref[u, :, sl] = hh[i]


def _mlstm_chunk_specs(block_of, L):
    return [
        pl.BlockSpec((L, M_HK), lambda s: (block_of(s), 0)),
        pl.BlockSpec((L, M_HK), lambda s: (block_of(s), 1)),
        pl.BlockSpec((L, D_MODEL), lambda s: (block_of(s), 1)),
        pl.BlockSpec((L, 16), lambda s: (block_of(s), 0)),
        pl.BlockSpec((16, L), lambda s: (0, block_of(s))),
    ]


def mlstm_prompt(qkv, g, gt, ogate, norm_g, n_prompt):
    L = MLSTM_CHUNK
    state_blk = lambda *tail: pl.BlockSpec((1, 1, 2, M_HEADS) + tail, lambda s: (s, 0, 0, 0, 0, 0))
    return pl.pallas_call(
        _mlstm_prompt_kernel,
        grid=(n_prompt,),
        in_specs=_mlstm_chunk_specs(lambda s: s, L) + [
            pl.BlockSpec((L, D_MODEL), lambda s: (s, 0)),
            pl.BlockSpec((1, D_MODEL), lambda s: (0, 0)),
        ],
        out_specs=[pl.BlockSpec((L, D_MODEL), lambda s: (s, 0)),
                   state_blk(M_DK, M_DV), state_blk(8, M_DK), state_blk(8, 128)],
        out_shape=[
            jax.ShapeDtypeStruct((n_prompt * L, D_MODEL), BF16),
            jax.ShapeDtypeStruct((n_prompt, 1, 2, M_HEADS, M_DK, M_DV), F32),
            jax.ShapeDtypeStruct((n_prompt, 1, 2, M_HEADS, 8, M_DK), F32),
            jax.ShapeDtypeStruct((n_prompt, 1, 2, M_HEADS, 8, 128), F32),
        ],
        compiler_params=_cparams("arbitrary"),
        name="mlstm_prompt",
    )(qkv, qkv, qkv, g, gt, ogate, norm_g.reshape(1, D_MODEL))


def mlstm_sample(direction, qkv, g, gt, c0, n0, m0, first_block, chunks_per_sample,
                 hprev=None, ogate=None, norm_g=None):
    L = MLSTM_SAMPLE_CHUNK
    backward = direction == 1
    cps = chunks_per_sample
    n_units = c0.shape[0]
    pos = (lambda s: cps - 1 - s) if backward else (lambda s: s)
    unit_block = [functools.partial(lambda s, u: first_block + u * cps + pos(s), u=u) for u in range(n_units)]
    whole = lambda a: pl.BlockSpec(a.shape, lambda s: (0,) * a.ndim)
    in_specs, args = [], []
    for u in range(n_units):
        in_specs += _mlstm_chunk_specs(unit_block[u], L)
        args += [qkv, qkv, qkv, g, gt]
    in_specs += [whole(c0), whole(n0), whole(m0)]
    args += [c0, n0, m0]
    if backward:
        in_specs += [pl.BlockSpec((n_units, L, D_MODEL), lambda s: (0, pos(s), 0))]
        in_specs += [pl.BlockSpec((L, D_MODEL), functools.partial(lambda s, u: (unit_block[u](s), 0), u=u))
                     for u in range(n_units)]
        in_specs += [pl.BlockSpec((1, D_MODEL), lambda s: (0, 0))]
        args += [hprev] + [ogate] * n_units + [norm_g.reshape(1, D_MODEL)]
    return pl.pallas_call(
        functools.partial(_mlstm_sample_kernel, direction=direction, n_units=n_units),
        grid=(cps,),
        in_specs=in_specs,
        out_specs=pl.BlockSpec((n_units, L, D_MODEL), lambda s: (0, pos(s), 0)),
        out_shape=jax.ShapeDtypeStruct((n_units, cps * L, D_MODEL), BF16 if backward else F32),
        scratch_shapes=[pltpu.VMEM(c0.shape, F32), pltpu.VMEM(n0.shape, F32), pltpu.VMEM(m0.shape, F32)],
        compiler_params=_cparams("arbitrary"),
        name="mlstm_sample_bwd" if backward else "mlstm_sample_fwd",
    )(*args)


def _outproj_router_kernel(ap_ref, as_ref, x_ref, mod_ref, w_ref, lng_ref, lnb_ref, wr_ref, *refs,
                           n_prompt_tiles):
    xo_ref, up_ref, comb_ref, pos_ref, total_ref, carry_scr = refs[-6:]
    i = pl.program_id(0)

    @pl.when(i == 0)
    def _():
        carry_scr[...] = jnp.zeros_like(carry_scr)

    a = jnp.where(i < n_prompt_tiles, ap_ref[...], as_ref[...])
    z = DEEPNORM_ALPHA * x_ref[...] + mod_ref[0, 2:3, :] * _dot(a, w_ref[...])
    xn = _layer_norm(z, lng_ref[...], lnb_ref[...])
    xo_ref[...] = xn
    u = xn * (1.0 + mod_ref[0, 4:5, :]) + mod_ref[0, 3:4, :]
    up_ref[...] = _pack_bf16_pairs(u)
    comb, pos, carry_new = _route_tokens(u.astype(BF16), wr_ref[...], carry_scr[:, 0:1])
    comb_ref[...] = comb
    pos_ref[...] = pos
    carry_scr[...] = jnp.broadcast_to(carry_new, carry_scr.shape)
    total_ref[...] = jnp.broadcast_to(carry_new, carry_scr.shape)


def mixer_outproj_router(a_p, a_s, x, mod, w, ln_g, ln_b, w_router, sample_len, run_after=()):
    t, d = x.shape
    ne = w_router.shape[1]
    tt = TOKEN_TILE
    npt = a_p.shape[0] // tt
    seg = functools.partial(_seg_of_tile, tile=tt, n_prompt_tok=a_p.shape[0], sample_len=sample_len)
    return pl.pallas_call(
        functools.partial(_outproj_router_kernel, n_prompt_tiles=npt),
        grid=(t // tt,),
        in_specs=_split_token_specs(tt, d, npt) + [
            pl.BlockSpec((tt, d), lambda i: (i, 0)),
            pl.BlockSpec((1, 6, d), lambda i: (seg(i), 0, 0)),
            pl.BlockSpec((d, d), lambda i: (0, 0)),
            pl.BlockSpec((1, d), lambda i: (0, 0)),
            pl.BlockSpec((1, d), lambda i: (0, 0)),
            pl.BlockSpec((ne, d), lambda i: (0, 0)),
        ] + [pl.BlockSpec(memory_space=pl.ANY)] * len(run_after),
        out_specs=[
            pl.BlockSpec((tt, d), lambda i: (i, 0)),
            pl.BlockSpec((tt, d // 2), lambda i: (i, 0)),
            pl.BlockSpec((ne, tt), lambda i: (0, i)),
            pl.BlockSpec((ne, tt), lambda i: (0, i)),
            pl.BlockSpec((ne, 128), lambda i: (0, 0)),
        ],
        out_shape=[
            jax.ShapeDtypeStruct((t, d), F32),
            jax.ShapeDtypeStruct((t, d // 2), jnp.int32),
            jax.ShapeDtypeStruct((ne, t), F32),
            jax.ShapeDtypeStruct((ne, t), F32),
            jax.ShapeDtypeStruct((ne, 128), F32),
        ],
        scratch_shapes=[pltpu.VMEM((ne, 128), F32)],
        compiler_params=_cparams("arbitrary"),
        name="mixer_outproj_router",
    )(a_p, a_s, x, mod, w, ln_g.reshape(1, d), ln_b.reshape(1, d), w_router.T, *run_after)


def _swiglu_tile(x, wi, wo, dff):
    sub = FFN_SUB
    proj = lambda j: (_dot(x, wi(slice(j * sub, (j + 1) * sub))), _dot(x, wi(slice(dff + j * sub, dff + (j + 1) * sub))))
    acts, cur = [], proj(0)
    for j in range(dff // sub):
        nxt = proj(j + 1) if (j + 1) * sub < dff else None
        gp, up = cur
        acts.append((gp * jax.nn.sigmoid(gp) * up).astype(BF16))
        cur = nxt
    return _dot(jnp.concatenate(acts, axis=1), wo(slice(0, dff)))


def _mixer_ffn_kernel(ap_ref, as_ref, xp_ref, xs_ref, mod_ref, modn_ref, wmix_ref, wi_ref, wo_ref, ln_ref,
                      xo_ref, uo_ref, *, n_prompt_tiles):
    is_prompt = pl.program_id(0) < n_prompt_tiles
    a = jnp.where(is_prompt, ap_ref[...], as_ref[...])
    x = jnp.where(is_prompt, xp_ref[...], xs_ref[...])
    z = DEEPNORM_ALPHA * x + mod_ref[0, 2:3, :] * _dot(a, wmix_ref[...])
    x1 = _layer_norm(z, ln_ref[0:1, :], ln_ref[1:2, :])
    u1 = (x1 * (1.0 + mod_ref[0, 4:5, :]) + mod_ref[0, 3:4, :]).astype(BF16)
    f = _swiglu_tile(u1, lambda c: wi_ref[:, c], lambda r: wo_ref[r, :], wo_ref.shape[0])
    x2 = _layer_norm(DEEPNORM_ALPHA * x1 + mod_ref[0, 5:6, :] * f, ln_ref[2:3, :], ln_ref[3:4, :])
    xo_ref[...] = x2
    uo_ref[...] = (x2 * (1.0 + modn_ref[0, 1:2, :]) + modn_ref[0, 0:1, :]).astype(BF16)


def _resident(shape):
    return pl.BlockSpec(shape, lambda i: (0,) * len(shape), pipeline_mode=pl.Buffered(1))


def mixer_ffn(a_p, a_s, xp, xs, mod, mod_next, w_mix, w_in, w_out, ln_g, ln_b, sample_len):
    n_prompt_tok, d = xp.shape
    t = n_prompt_tok + xs.shape[0]
    tm = FFN_TOKEN_TILE
    npt = n_prompt_tok // tm
    seg = functools.partial(_seg_of_tile, tile=tm, n_prompt_tok=n_prompt_tok, sample_len=sample_len)
    ln = jnp.stack([ln_g[0], ln_b[0], ln_g[1], ln_b[1]])
    return pl.pallas_call(
        functools.partial(_mixer_ffn_kernel, n_prompt_tiles=npt),
        grid=(t // tm,),
        in_specs=_split_token_specs(tm, d, npt) + _split_token_specs(tm, d, npt) + [
            pl.BlockSpec((1, 6, d), lambda i: (seg(i), 0, 0)),
            pl.BlockSpec((1, 6, d), lambda i: (seg(i), 0, 0)),
            _resident(w_mix.shape), _resident(w_in.shape), _resident(w_out.shape), _resident(ln.shape),
        ],
        out_specs=[pl.BlockSpec((tm, d), lambda i: (i, 0)), pl.BlockSpec((tm, d), lambda i: (i, 0))],
        out_shape=[jax.ShapeDtypeStruct((t, d), F32), jax.ShapeDtypeStruct((t, d), BF16)],
        compiler_params=_cparams("arbitrary"),
        name="mixer_ffn",
    )(a_p, a_s, xp, xs, mod, mod_next, w_mix, w_in, w_out, ln)


def _modulation_tables(c, c_ctx, ada_w, ada_b):
    n_s = c.shape[0]
    cvec = jnp.concatenate([c_ctx[None, :], c, jnp.zeros((8 - 1 - n_s, c.shape[1]), F32)], axis=0)
    mod = adaln(cvec, ada_w, ada_b)
    return mod[:, :1 + n_s, :].reshape(ada_w.shape[0], 1 + n_s, 6, c.shape[1])


def _layer0(xp, xs, mod0, mod1, state_c, state_n, state_m, ln_g, ln_b, w_in, b_gates, norm_g, w_out,
            ffn_w_in, ffn_w_out, n_prompt, prompt_len, n_sample, sample_len):
    n_prompt_tok = n_prompt * prompt_len
    n_main = 2 * M_HK + 2 * D_MODEL
    w_t = w_in.T
    qkv, ogate, g, gt = mlstm_inproj(xp, xs, mod0, w_t[:n_main].astype(BF16), w_t[n_main:].T.astype(BF16),
                                     w_t[n_main:].astype(BF16), b_gates.reshape(-1), sample_len)
    npc = n_prompt_tok // MLSTM_SAMPLE_CHUNK
    cps = sample_len // MLSTM_SAMPLE_CHUNK
    rep = lambda a: jnp.broadcast_to(a[..., None, :], a.shape[:-1] + (8, a.shape[-1]))
    n0 = rep(state_n)
    m0 = jnp.broadcast_to(state_m[..., None, None], state_m.shape + (8, 128))
    a_p, new_c, nf, mf = mlstm_prompt(qkv, g, gt, ogate, norm_g, n_prompt)
    hf = mlstm_sample(0, qkv, g, gt, state_c[:, 0], n0[:, 0], m0[:, 0], npc, cps)
    a_s = mlstm_sample(1, qkv, g, gt, state_c[:, 1], n0[:, 1], m0[:, 1], npc, cps,
                       hprev=hf, ogate=ogate, norm_g=norm_g)
    x2, u2 = mixer_ffn(a_p, a_s.reshape(-1, D_MODEL), xp, xs, mod0, mod1, w_out.astype(BF16),
                       ffn_w_in.astype(BF16), ffn_w_out.astype(BF16), ln_g, ln_b, sample_len)
    return x2, u2, new_c, nf[..., 0, :], mf[..., 0, 0]


def _qkv_kernel(u_ref, w_ref, qkv_ref, *kv_refs):
    tt, d = u_ref.shape
    u = u_ref[...]

    def head_major(out_ref, p):
        for h in range(NA_HEADS):
            out_ref[pl.ds(h, tt, stride=NA_HEADS), :] = p[:, h * NA_DH:(h + 1) * NA_DH]

    pq = _dot(u, w_ref[:, :d])
    pk = _dot(u, w_ref[:, d:2 * d])
    qkv_ref[:, :d] = pq.astype(BF16)
    pv = _dot(u, w_ref[:, 2 * d:])
    qkv_ref[:, d:2 * d] = pk.astype(BF16)
    if kv_refs:
        head_major(kv_refs[0], pk)
    qkv_ref[:, 2 * d:] = pv.astype(BF16)
    if kv_refs:
        head_major(kv_refs[1], pv)


def na_qkv(u, w, first_tok, n_tok, with_kv):
    d = u.shape[1]
    tt = TOKEN_TILE
    first = first_tok // tt
    out_specs = [pl.BlockSpec((tt, 3 * d), lambda i: (i, 0))]
    out_shape = [jax.ShapeDtypeStruct((n_tok, 3 * d), BF16)]
    if with_kv:
        out_specs += [pl.BlockSpec((tt * NA_HEADS, NA_DH), lambda i: (i, 0))] * 2
        out_shape += [jax.ShapeDtypeStruct((n_tok * NA_HEADS, NA_DH), F32)] * 2
    return pl.pallas_call(
        _qkv_kernel,
        grid=(n_tok // tt,),
        in_specs=[pl.BlockSpec((tt, d), lambda i: (first + i, 0)), pl.BlockSpec((d, 3 * d), lambda i: (0, 0))],
        out_specs=out_specs,
        out_shape=out_shape,
        compiler_params=_cparams("arbitrary"),
        name="na_qkv_prompt" if with_kv else "na_qkv_latent",
    )(u, w)


ATTN_LOG2E = math.log2(math.e)
ATTN_Q_SCALE = NA_DH ** -0.5 * ATTN_LOG2E


def _head_pair_masks():
    lane = lax.broadcasted_iota(jnp.int32, (1, 2 * NA_DH), 1)
    return (_onehot(lane < NA_DH), _onehot(lane >= NA_DH))


def _softmax_pv(score_blocks, value_blocks):
    mx = None
    for s in score_blocks:
        bm = jnp.max(s, axis=1, keepdims=True)
        mx = bm if mx is None else jnp.maximum(mx, bm)
    acc, denom = None, None
    for s, v in zip(score_blocks, value_blocks):
        p = jnp.exp2(s - mx)
        ps = jnp.sum(p, axis=1, keepdims=True)
        pv = _dot(p.astype(BF16), v)
        acc = pv if acc is None else acc + pv
        denom = ps if denom is None else denom + ps
    return acc / denom


def _ctx_attn_kernel(q_ref, k_ref, v_ref, o_ref):
    masks = _head_pair_masks()
    for hp in range(NA_HEADS // 2):
        sl = slice(hp * 2 * NA_DH, (hp + 1) * 2 * NA_DH)
        q2, k2, v2 = q_ref[:, sl], k_ref[:, sl], v_ref[:, sl]
        o2 = None
        for msk in masks:
            s = _dot_nt(q2 * msk, k2)
            o = _softmax_pv([s], [v2 * msk])
            o2 = o if o2 is None else o2 + o
        o_ref[:, sl] = o2.astype(o_ref.dtype)


def context_attention(qkv, n_prompt, prompt_len):
    t = n_prompt * prompt_len
    d = D_MODEL
    return pl.pallas_call(
        _ctx_attn_kernel,
        grid=(n_prompt,),
        in_specs=[
            pl.BlockSpec((prompt_len, d), lambda b: (b, 0)),
            pl.BlockSpec((prompt_len, d), lambda b: (b, 1)),
            pl.BlockSpec((prompt_len, d), lambda b: (b, 2)),
        ],
        out_specs=pl.BlockSpec((prompt_len, d), lambda b: (b, 0)),
        out_shape=jax.ShapeDtypeStruct((t, d), BF16),
        compiler_params=_cparams("arbitrary"),
        name="context_attention",
    )(qkv, qkv, qkv)


def _na_kernel(tile_ref, q_ref, kp_ref, kc_ref, kn_ref, vp_ref, vc_ref, vn_ref, kctx_ref, vctx_ref, bias_ref,
               *refs):
    n_cast = (len(refs) - 1) // 2
    cast_in, o_ref, cast_out = refs[:n_cast], refs[n_cast], refs[n_cast + 1:]
    for src, dst in zip(cast_in, cast_out):
        dst[...] = src[...].astype(dst.dtype)
    j = pl.program_id(1)
    nblk = pl.num_programs(1)
    m = q_ref.shape[0]
    halo = NA_HALO * GRID_W
    n_pairs = (NA_QROWS + 2 * NA_HALO) // 2
    variant = jnp.where(j == 0, 0, jnp.where(j == nblk - 1, 2, 1))
    kinds = [[tile_ref[(variant * NA_QROWS + rq) * n_pairs + kp] for kp in range(n_pairs)]
             for rq in range(NA_QROWS)]

    def bias_of(head):
        return jnp.concatenate(
            [jnp.concatenate([bias_ref[kinds[rq][kp], head] for kp in range(n_pairs)], axis=1)
             for rq in range(NA_QROWS)], axis=0)

    masks = _head_pair_masks()
    for hp in range(NA_HEADS // 2):
        sl = slice(hp * 2 * NA_DH, (hp + 1) * 2 * NA_DH)
        q2 = q_ref[:, sl]
        kloc = jnp.concatenate([kp_ref[m - halo:, sl], kc_ref[:, sl], kn_ref[:halo, sl]], axis=0)
        vloc = jnp.concatenate([vp_ref[m - halo:, sl], vc_ref[:, sl], vn_ref[:halo, sl]], axis=0)
        kctx = kctx_ref[0][:, sl]
        vctx = vctx_ref[0][:, sl]
        o2 = None
        for half, msk in enumerate(masks):
            qm = q2 * msk
            s_loc = _dot_nt(qm, kloc) + bias_of(2 * hp + half)
            s_ctx = _dot_nt(qm, kctx)
            o = _softmax_pv([s_loc, s_ctx], [vloc * msk, vctx * msk])
            o2 = o if o2 is None else o2 + o
        o_ref[:, sl] = o2.astype(o_ref.dtype)


def _na_bias_kernel(rpb_ref, o_ref, *, kinds, n_heads):
    nc, npos = rpb_ref.shape[1], o_ref.shape[2]
    pair_shift = (2 * GRID_W).bit_length() - 1
    c = lax.broadcasted_iota(jnp.int32, (nc, npos), 0)
    pos = lax.broadcasted_iota(jnp.int32, (nc, npos), 1)
    qc = pos >> pair_shift
    kc = pos & (GRID_W - 1)
    c_start = jnp.clip(qc - NA_COLS // 2, 0, GRID_W - NA_COLS)
    col_in = jnp.logical_and(kc >= c_start, kc < c_start + NA_COLS)
    dc = jnp.clip(kc - qc + NA_COLS - 1, 0, 2 * NA_COLS - 2)
    hit = jnp.logical_and(c == dc, col_in)
    right = (pos & GRID_W) != 0
    r1, r2, r3 = _split3(rpb_ref[...] * ATTN_LOG2E)

    def placed(side):
        sel = _onehot(jnp.logical_and(hit, right == side))
        return _dot(r1, sel) + _dot(r2, sel) + _dot(r3, sel)

    left, rght = placed(False), placed(True)
    rows = lambda v, dr: v[dr * n_heads:(dr + 1) * n_heads]
    for k, (dl, dr) in enumerate(kinds):
        shown = col_in[0:1, :]
        if dl is None or dr is None:
            side_ok = jnp.zeros_like(shown) if dl is None and dr is None else (right[0:1, :] == (dl is None))
            shown = jnp.logical_and(shown, side_ok)
        parts = ([rows(left, dl)] if dl is not None else []) + ([rows(rght, dr)] if dr is not None else [])
        val = sum(parts) if parts else jnp.zeros((n_heads, npos), F32)
        o_ref[k] = jnp.where(shown, val, NEG)


def _na_bias_plan():
    ndr = 2 * NA_ROWS - 1
    assert NA_QROWS >= NA_ROWS // 2 and NA_HALO == NA_ROWS // 2 and (NA_QROWS + 2 * NA_HALO) % 2 == 0
    ok = lambda dr: dr if dr is not None and 0 <= dr < ndr else None
    kinds = [(ok(dr), ok(dr + 1)) for dr in range(-1, ndr)]
    first_key_row = (lambda rq: NA_HALO, lambda rq: rq, lambda rq: NA_QROWS + NA_HALO - NA_ROWS)
    table = []
    for first in first_key_row:
        for rq in range(NA_QROWS):
            for kp in range((NA_QROWS + 2 * NA_HALO) // 2):
                drs = []
                for kk in (2 * kp, 2 * kp + 1):
                    visible = first(rq) <= kk < first(rq) + NA_ROWS
                    drs.append(kk - rq + NA_ROWS - 1 - NA_HALO if visible else None)
                kind = (drs[0], drs[1])
                if kind not in kinds:
                    kinds.append(kind)
                table.append(kinds.index(kind))
    return kinds, table


def _na_bias_tiles(rpb):
    nh, ndr, ndc = rpb.shape
    nc = 32
    kinds, table = _na_bias_plan()
    nk = len(kinds)
    rows = jnp.pad(jnp.transpose(rpb, (1, 0, 2)), ((0, 0), (0, 0), (0, nc - ndc))).reshape(ndr * nh, nc)
    npos = GRID_W * 2 * GRID_W
    tiles = pl.pallas_call(
        functools.partial(_na_bias_kernel, kinds=tuple(kinds), n_heads=nh),
        out_shape=jax.ShapeDtypeStruct((nk, nh, npos), F32),
        compiler_params=pltpu.CompilerParams(vmem_limit_bytes=VMEM_LIMIT_BYTES),
        name="na_bias",
    )(rows)
    return tiles.reshape(nk, nh, GRID_W, 2 * GRID_W), jnp.asarray(table, jnp.int32)


def neighbourhood_attention(qkv, k_ctx, v_ctx, rpb, n_sample, sample_len, side_casts=()):
    d = D_MODEL
    m = NA_QROWS * GRID_W
    nblk = sample_len // m
    assert nblk >= 3
    bias, table = _na_bias_tiles(rpb)
    steps = n_sample * nblk

    def blk(col, off):
        return pl.BlockSpec((m, d), lambda b, j, tbl: (b * nblk + jnp.clip(j + off, 0, nblk - 1), col))

    flat = [a.reshape(-1, a.shape[-1]) for a in side_casts]
    slab_specs = [pl.BlockSpec((a.shape[0] // steps, a.shape[1]), lambda b, j, tbl: (b * nblk + j, 0)) for a in flat]
    assert all(a.shape[0] % (steps * 16) == 0 for a in flat)
    outs = pl.pallas_call(
        _na_kernel,
        grid_spec=pltpu.PrefetchScalarGridSpec(
            num_scalar_prefetch=1,
            grid=(n_sample, nblk),
            in_specs=[
                blk(0, 0), blk(1, -1), blk(1, 0), blk(1, 1), blk(2, -1), blk(2, 0), blk(2, 1),
                pl.BlockSpec((1,) + k_ctx.shape[1:], lambda b, j, tbl: (b, 0, 0), pipeline_mode=pl.Buffered(1)),
                pl.BlockSpec((1,) + v_ctx.shape[1:], lambda b, j, tbl: (b, 0, 0), pipeline_mode=pl.Buffered(1)),
                pl.BlockSpec(bias.shape, lambda b, j, tbl: (0, 0, 0, 0), pipeline_mode=pl.Buffered(1)),
            ] + slab_specs,
            out_specs=[pl.BlockSpec((m, d), lambda b, j, tbl: (b * nblk + j, 0))] + slab_specs,
        ),
        out_shape=[jax.ShapeDtypeStruct((n_sample * sample_len, d), BF16)]
        + [jax.ShapeDtypeStruct(a.shape, BF16) for a in flat],
        compiler_params=_cparams("arbitrary", "arbitrary"),
        name="neighbourhood_attention",
    )(table, qkv, qkv, qkv, qkv, qkv, qkv, qkv, k_ctx, v_ctx, bias, *flat)
    return [outs[0]] + [o.reshape(a.shape) for o, a in zip(outs[1:], side_casts)]


MOE_ROW_TILE = 512
SC_CORES = 2
SC_SUBCORES = 16
SC_GATHER_CHUNK = 128

def _route_tokens(u, wt, carry):
    w1, w2, w3 = _split3(wt)
    lg = _dot_nt(w1, u) + _dot_nt(w2, u) + _dot_nt(w3, u)
    ne, nt = lg.shape
    e = lax.broadcasted_iota(jnp.int32, lg.shape, 0)
    m1 = jnp.max(lg, axis=0, keepdims=True)
    i1 = jnp.min(jnp.where(lg == m1, e, ne), axis=0, keepdims=True)
    sel1 = e == i1
    lg2 = jnp.where(sel1, -jnp.inf, lg)
    m2 = jnp.max(lg2, axis=0, keepdims=True)
    i2 = jnp.min(jnp.where(lg2 == m2, e, ne), axis=0, keepdims=True)
    sel2 = e == i2
    ex = jnp.exp(m2 - m1)
    wa = 1.0 / (1.0 + ex)
    wb = ex / (1.0 + ex)
    comb = jnp.where(sel1, wa, jnp.where(sel2, wb, 0.0))
    assign = jnp.logical_or(sel1, sel2)
    a = jnp.where(assign, 1.0, 0.0)
    row = lax.broadcasted_iota(jnp.int32, (nt, nt), 0)
    col = lax.broadcasted_iota(jnp.int32, (nt, nt), 1)
    tri = _onehot(row < col)
    rank = _dot(a.astype(BF16), tri) + carry
    pos = jnp.where(assign, rank, -1.0)
    return comb, pos, carry + jnp.sum(a, axis=1, keepdims=True)


def _moe_plan(totals, pos, comb, n_tok):
    ne = totals.shape[0]
    tm = MOE_ROW_TILE
    n_tiles = (2 * n_tok) // tm + ne
    tiles_per = (totals + tm - 1) // tm
    tile_end = jnp.cumsum(tiles_per)
    n_used = tile_end[-1]
    base_rows = (tile_end - tiles_per) * tm
    ti = jnp.arange(n_tiles, dtype=jnp.int32)
    tile_expert = jnp.minimum(jnp.sum((ti[:, None] >= tile_end[None, :]).astype(jnp.int32), axis=1), ne - 1)
    last_expert = tile_expert[jnp.maximum(n_used - 1, 0)]
    tile_expert = jnp.where(ti < n_used, tile_expert, last_expert)
    routed = pos >= 0.0
    row = base_rows[:, None] + pos.astype(jnp.int32)
    row_a = jnp.min(jnp.where(routed, row, n_tiles * tm), axis=0)
    row_b = jnp.max(jnp.where(routed, row, -1), axis=0)
    w_a = jnp.sum(jnp.where(jnp.logical_and(routed, row == row_a[None, :]), comb, 0.0), axis=0)
    w_b = jnp.sum(jnp.where(jnp.logical_and(routed, row == row_b[None, :]), comb, 0.0), axis=0)
    tile_rows = jnp.clip(totals[tile_expert] - (ti - (tile_end - tiles_per)[tile_expert]) * tm, 0, tm)
    tile_rows = jnp.where(ti < n_used, tile_rows, 0)
    return dict(n_rows=n_tiles * tm, tile_expert=tile_expert.astype(jnp.int32),
                n_used=n_used.reshape(1).astype(jnp.int32), tile_rows=tile_rows.astype(jnp.int32),
                token_rows=jnp.concatenate([row_a, row_b]).astype(jnp.int32),
                token_weights=jnp.concatenate([w_a[None], w_b[None], jnp.zeros((6, n_tok), F32)], axis=0))


def _pack_bf16_pairs(x):
    half = x.shape[1] // 2
    bits = pltpu.bitcast(x.astype(BF16).astype(F32), jnp.int32)
    return (bits[:, :half] & jnp.int32(-65536)) | lax.shift_right_logical(bits[:, half:], jnp.int32(16))


def _unpack_bf16_pairs(p):
    hi = pltpu.bitcast(p & jnp.int32(-65536), F32)
    lo = pltpu.bitcast(lax.shift_left(p, jnp.int32(16)), F32)
    return jnp.concatenate([hi, lo], axis=1)


def sc_gather_rows(table, idx):
    _, w = table.shape
    b = idx.shape[0]
    workers = SC_CORES * SC_SUBCORES
    ch = SC_GATHER_CHUNK
    n_chunks = b // (workers * ch)
    assert n_chunks * workers * ch == b
    mesh = plsc.VectorSubcoreMesh(core_axis_name="c", subcore_axis_name="s", num_cores=SC_CORES,
                                  num_subcores=SC_SUBCORES)

    @functools.partial(
        pl.kernel, mesh=mesh, out_type=jax.ShapeDtypeStruct((b, w), table.dtype),
        scratch_types=[pltpu.VMEM((ch,), jnp.int32), pltpu.VMEM((ch, w), table.dtype), pltpu.SemaphoreType.DMA])
    def gather(table_hbm, idx_hbm, out_hbm, idx_v, rows_v, sem):
        worker = lax.axis_index("s") * SC_CORES + lax.axis_index("c")

        @pl.loop(0, n_chunks)
        def _(c):
            base = (worker * n_chunks + c) * ch
            pltpu.sync_copy(idx_hbm.at[pl.ds(base, ch)], idx_v)
            pltpu.async_copy(table_hbm.at[idx_v], rows_v, sem).wait()
            pltpu.sync_copy(rows_v, out_hbm.at[pl.ds(base, ch)])

    return gather(table, idx)


def sc_scatter_rows(rows, idx, n_out):
    t, w = rows.shape
    copies = idx.shape[0] // t
    workers = SC_CORES * SC_SUBCORES
    ch = SC_GATHER_CHUNK
    n_chunks = t // (workers * ch)
    assert n_chunks * workers * ch == t and copies * t == idx.shape[0]
    mesh = plsc.VectorSubcoreMesh(core_axis_name="c", subcore_axis_name="s", num_cores=SC_CORES,
                                  num_subcores=SC_SUBCORES)

    @functools.partial(
        pl.kernel, mesh=mesh, out_type=jax.ShapeDtypeStruct((n_out, w), rows.dtype),
        scratch_types=[pltpu.VMEM((ch,), jnp.int32), pltpu.VMEM((ch, w), rows.dtype), pltpu.SemaphoreType.DMA])
    def scatter(rows_hbm, idx_hbm, out_hbm, idx_v, rows_v, sem):
        worker = lax.axis_index("s") * SC_CORES + lax.axis_index("c")

        @pl.loop(0, n_chunks)
        def _(c):
            base = (worker * n_chunks + c) * ch
            pltpu.sync_copy(rows_hbm.at[pl.ds(base, ch)], rows_v)
            for k in range(copies):
                pltpu.sync_copy(idx_hbm.at[pl.ds(k * t + base, ch)], idx_v)
                pltpu.async_copy(rows_v, out_hbm.at[idx_v], sem).wait()

    return scatter(rows, idx)


def _gffn_kernel(te_ref, nu_ref, nr_ref, x_ref, wi_ref, wo_ref, y_ref):
    i = pl.program_id(0)
    used = i < nu_ref[0]

    @pl.when(used)
    def _():
        packed = x_ref[...]
        row = lax.broadcasted_iota(jnp.int32, packed.shape, 0)
        x = _unpack_bf16_pairs(jnp.where(row < nr_ref[i], packed, 0)).astype(BF16)
        y = _swiglu_tile(x, lambda c: wi_ref[0, :, c], lambda r: wo_ref[0, r, :], wo_ref.shape[1])
        y_ref[...] = _pack_bf16_pairs(y)

    @pl.when(jnp.logical_not(used))
    def _():
        y_ref[...] = jnp.zeros_like(y_ref)


def moe_grouped_ffn(xs, w_in, w_out, plan):
    nr, half = xs.shape
    ne, dff, d = w_out.shape
    tm = MOE_ROW_TILE
    return pl.pallas_call(
        _gffn_kernel,
        grid_spec=pltpu.PrefetchScalarGridSpec(
            num_scalar_prefetch=3,
            grid=(nr // tm,),
            in_specs=[
                pl.BlockSpec((tm, half), lambda i, te, nu, tr: (jnp.minimum(i, nu[0] - 1), 0)),
                pl.BlockSpec((1, d, 2 * dff), lambda i, te, nu, tr: (te[i], 0, 0)),
                pl.BlockSpec((1, dff, d), lambda i, te, nu, tr: (te[i], 0, 0)),
            ],
            out_specs=pl.BlockSpec((tm, half), lambda i, te, nu, tr: (i, 0)),
        ),
        out_shape=jax.ShapeDtypeStruct((nr, half), jnp.int32),
        compiler_params=_cparams("arbitrary"),
        name="moe_grouped_ffn",
    )(plan["tile_expert"], plan["n_used"], plan["tile_rows"], xs, w_in, w_out)


def _combine_kernel(ya_ref, yb_ref, w_ref, x_ref, mod_ref, lng_ref, lnb_ref, op_ref, os_ref, *, n_prompt_tiles):
    i = pl.program_id(0)
    w = w_ref[...].T
    moe = w[:, 0:1] * _unpack_bf16_pairs(ya_ref[...]) + w[:, 1:2] * _unpack_bf16_pairs(yb_ref[...])
    z = DEEPNORM_ALPHA * x_ref[...] + mod_ref[0, 5:6, :] * moe
    xn = _layer_norm(z, lng_ref[...], lnb_ref[...])

    @pl.when(i < n_prompt_tiles)
    def _():
        op_ref[...] = xn

    @pl.when(i >= n_prompt_tiles)
    def _():
        os_ref[...] = xn


def moe_combine(y_tok, weights, x, mod, ln_g, ln_b, n_prompt_tok, sample_len):
    t, d = x.shape
    tt = TOKEN_TILE
    nt = t // tt
    npt = n_prompt_tok // tt
    seg = functools.partial(_seg_of_tile, tile=tt, n_prompt_tok=n_prompt_tok, sample_len=sample_len)
    return pl.pallas_call(
        functools.partial(_combine_kernel, n_prompt_tiles=npt),
        grid=(nt,),
        in_specs=[
            pl.BlockSpec((tt, d // 2), lambda i: (i, 0)),
            pl.BlockSpec((tt, d // 2), lambda i: (nt + i, 0)),
            pl.BlockSpec((8, tt), lambda i: (0, i)),
            pl.BlockSpec((tt, d), lambda i: (i, 0)),
            pl.BlockSpec((1, 6, d), lambda i: (seg(i), 0, 0)),
            pl.BlockSpec((1, d), lambda i: (0, 0)),
            pl.BlockSpec((1, d), lambda i: (0, 0)),
        ],
        out_specs=[
            pl.BlockSpec((tt, d), lambda i: (jnp.minimum(i, npt - 1), 0)),
            pl.BlockSpec((tt, d), lambda i: (jnp.maximum(i - npt, 0), 0)),
        ],
        out_shape=[jax.ShapeDtypeStruct((n_prompt_tok, d), F32), jax.ShapeDtypeStruct((t - n_prompt_tok, d), F32)],
        compiler_params=_cparams("arbitrary"),
        name="moe_combine",
    )(y_tok, y_tok, weights, x, mod, ln_g.reshape(1, d), ln_b.reshape(1, d))


def _layer1(x, u, mod1, cache_k, cache_v, ln_g, ln_b, w_qkv, rpb, w_out, w_router, moe_w_in, moe_w_out,
            n_prompt, prompt_len, n_sample, sample_len):
    n_prompt_tok = n_prompt * prompt_len
    d = D_MODEL
    col_scale = jnp.where(jnp.arange(3 * d) < d, ATTN_Q_SCALE, 1.0).astype(F32)
    w_qkv = (w_qkv * col_scale[None, :]).astype(BF16)
    qkv_p, k_p, v_p = na_qkv(u, w_qkv, 0, n_prompt_tok, with_kv=True)
    qkv_s, = na_qkv(u, w_qkv, n_prompt_tok, u.shape[0] - n_prompt_tok, with_kv=False)
    attn_p = context_attention(qkv_p, n_prompt, prompt_len)
    k_ctx = cache_k.reshape(n_sample, -1, d).astype(BF16)
    v_ctx = cache_v.reshape(n_sample, -1, d).astype(BF16)
    attn_s, moe_w_in, moe_w_out = neighbourhood_attention(qkv_s, k_ctx, v_ctx, rpb, n_sample, sample_len,
                                                          side_casts=(moe_w_in, moe_w_out))
    x1, u1_packed, comb, pos, totals = mixer_outproj_router(
        attn_p, attn_s, x, mod1, w_out.astype(BF16), ln_g[0], ln_b[0], w_router, sample_len)
    plan = _moe_plan(totals[:, 0].astype(jnp.int32), pos, comb, x.shape[0])
    xs = sc_scatter_rows(u1_packed, plan["token_rows"], plan["n_rows"])
    y = moe_grouped_ffn(xs, moe_w_in, moe_w_out, plan)
    y_tok = sc_gather_rows(y, plan["token_rows"])
    y_p, y_s = moe_combine(y_tok, plan["token_weights"], x1, mod1, ln_g[1], ln_b[1], n_prompt_tok, sample_len)
    return y_p, y_s, k_p, v_p


def kernel(x_prompt, x_sample, state_mlstm_C, state_mlstm_n, state_mlstm_m, cache_na_k, cache_na_v, c, c_ctx,
           ada_w, ada_b, ln_g, ln_b, mlstm_w_in, mlstm_b_gates, mlstm_norm_g, mlstm_w_out, na_w_qkv, na_rpb,
           na_w_out, ffn_w_in, ffn_w_out, moe_w_router, moe_w_in, moe_w_out):
    n_prompt, prompt_len, d = x_prompt.shape
    n_sample, sample_len, _ = x_sample.shape
    assert d == D_MODEL and prompt_len == MLSTM_CHUNK and sample_len % MLSTM_SAMPLE_CHUNK == 0
    assert ada_w.shape[0] == DEPTH == 2
    mod = _modulation_tables(c, c_ctx, ada_w, ada_b)
    x2, u2, new_c, new_n, new_m = _layer0(
        x_prompt.reshape(-1, d), x_sample.reshape(-1, d), mod[0], mod[1],
        state_mlstm_C[:, 0], state_mlstm_n[:, 0], state_mlstm_m[:, 0], ln_g[0], ln_b[0], mlstm_w_in[0],
        mlstm_b_gates[0], mlstm_norm_g[0], mlstm_w_out[0], ffn_w_in[0], ffn_w_out[0],
        n_prompt, prompt_len, n_sample, sample_len)
    y_p, y_s, k_p, v_p = _layer1(
        x2, u2, mod[1], cache_na_k[:, 0], cache_na_v[:, 0], ln_g[1], ln_b[1], na_w_qkv[0], na_rpb[0], na_w_out[0],
        moe_w_router[0], moe_w_in[0], moe_w_out[0], n_prompt, prompt_len, n_sample, sample_len)
    kv_shape = (n_prompt, 1, prompt_len, NA_HEADS, NA_DH)
    return (y_p.reshape(x_prompt.shape), y_s.reshape(x_sample.shape), new_c, new_n, new_m,
            k_p.reshape(kv_shape), v_p.reshape(kv_shape))
```

```python
import functools
import math

import jax
import jax.numpy as jnp
from jax import lax
from jax.experimental import pallas as pl
from jax.experimental.pallas import tpu as pltpu
from jax.experimental.pallas import tpu_sc as plsc

F32 = jnp.float32
BF16 = jnp.bfloat16

D_MODEL = 1024
DEPTH = 2
GRID_W = 64
M_HEADS = 4
M_DK = 128
M_DV = 256
M_HK = M_HEADS * M_DK
NA_HEADS = 16
NA_DH = 64
NA_ROWS = 8
NA_COLS = 16
DEEPNORM_ALPHA = (2.0 * DEPTH) ** 0.25
LN_EPS = 1e-5
NEG = -1e30

VMEM_LIMIT_BYTES = 56 * 1024 * 1024

MLSTM_CHUNK = 256
MLSTM_SAMPLE_CHUNK = 256
TOKEN_TILE = 512
FFN_TOKEN_TILE = 512
FFN_SUB = 256
NA_QROWS = 4
NA_HALO = 4


def _cparams(*sem):
    return pltpu.CompilerParams(dimension_semantics=sem, vmem_limit_bytes=VMEM_LIMIT_BYTES)


def _dot(a, b):
    return jnp.dot(a, b, preferred_element_type=F32)


def _dot_nt(a, b):
    return lax.dot_general(a, b, (((1,), (1,)), ((), ())), preferred_element_type=F32)


def _onehot(mask):
    return jnp.where(mask, 1.0, 0.0).astype(BF16)


def _split3(x):
    hi = x.astype(BF16)
    r = x - hi.astype(F32)
    mid = r.astype(BF16)
    lo = (r - mid.astype(F32)).astype(BF16)
    return hi, mid, lo


def _layer_norm(z, g, b):
    mu = jnp.mean(z, axis=-1, keepdims=True)
    zc = z - mu
    var = jnp.mean(zc * zc, axis=-1, keepdims=True)
    return zc * lax.rsqrt(var + LN_EPS) * g + b


def _log_sigmoid(x):
    return jnp.minimum(x, 0.0) - jnp.log(1.0 + jnp.exp(-jnp.abs(x)))


def _seg_of_tile(i, tile, n_prompt_tok, sample_len):
    tok = i * tile
    return jnp.where(tok < n_prompt_tok, 0, 1 + (tok - n_prompt_tok) // sample_len)


def _adaln_kernel(c_ref, w_ref, b_ref, o_ref):
    c = c_ref[...]
    a = (c * jax.nn.sigmoid(c)).astype(BF16)
    o_ref[0] = _dot(a, w_ref[0].astype(BF16)) + b_ref[0]


def adaln(cvec, ada_w, ada_b):
    depth, d, n = ada_w.shape
    tn = 1536
    return pl.pallas_call(
        _adaln_kernel,
        grid=(depth, n // tn),
        in_specs=[
            pl.BlockSpec((8, d), lambda l, j: (0, 0)),
            pl.BlockSpec((1, d, tn), lambda l, j: (l, 0, j)),
            pl.BlockSpec((1, 1, tn), lambda l, j: (l, 0, j)),
        ],
        out_specs=pl.BlockSpec((1, 8, tn), lambda l, j: (l, 0, j)),
        out_shape=jax.ShapeDtypeStruct((depth, 8, n), F32),
        compiler_params=_cparams("arbitrary", "arbitrary"),
        name="adaln",
    )(cvec, ada_w, ada_b.reshape(depth, 1, n))


def _inproj_kernel(xp_ref, xs_ref, mod_ref, w_ref, wg_ref, wgt_ref, bg_ref, bgt_ref,
                   qkv_ref, o_ref, g_ref, gt_ref, *, n_prompt_tiles):
    x = jnp.where(pl.program_id(0) < n_prompt_tiles, xp_ref[...], xs_ref[...])
    u = (x * (1.0 + mod_ref[0, 1:2, :]) + mod_ref[0, 0:1, :]).astype(BF16)
    nqkv = qkv_ref.shape[1]
    half = nqkv // 2
    g = _dot(u, wg_ref[...]) + bg_ref[...]
    gt = _dot_nt(wgt_ref[...], u) + bgt_ref[...]
    p0 = _dot_nt(u, w_ref[:half, :])
    col = lax.broadcasted_iota(jnp.int32, g.shape, 1)
    g_ref[...] = jnp.where(((col >> 2) & 1) == 1, _log_sigmoid(g), g)
    p1 = _dot_nt(u, w_ref[half:nqkv, :])
    qkv_ref[:, :half] = p0.astype(BF16)
    row = lax.broadcasted_iota(jnp.int32, gt.shape, 0)
    gt_ref[...] = jnp.where(((row >> 2) & 1) == 1, _log_sigmoid(gt), gt)
    p2 = _dot_nt(u, w_ref[nqkv:, :])
    qkv_ref[:, half:] = p1.astype(BF16)
    o_ref[...] = p2


def _split_token_specs(tile, d, n_prompt_tiles):
    return [pl.BlockSpec((tile, d), lambda i: (jnp.minimum(i, n_prompt_tiles - 1), 0)),
            pl.BlockSpec((tile, d), lambda i: (jnp.maximum(i - n_prompt_tiles, 0), 0))]


def mlstm_inproj(xp, xs, mod, w_main, w_gate, w_gate_t, b_gate, sample_len):
    n_prompt_tok, d = xp.shape
    t = n_prompt_tok + xs.shape[0]
    n_main = w_main.shape[0]
    n_qkv = 2 * M_HK + D_MODEL
    ng = w_gate.shape[1]
    tt = TOKEN_TILE
    npt = n_prompt_tok // tt
    seg = functools.partial(_seg_of_tile, tile=tt, n_prompt_tok=n_prompt_tok, sample_len=sample_len)
    return pl.pallas_call(
        functools.partial(_inproj_kernel, n_prompt_tiles=npt),
        grid=(t // tt,),
        in_specs=_split_token_specs(tt, d, npt) + [
            pl.BlockSpec((1, 6, d), lambda i: (seg(i), 0, 0)),
            pl.BlockSpec((n_main, d), lambda i: (0, 0)),
            pl.BlockSpec((d, ng), lambda i: (0, 0)),
            pl.BlockSpec((ng, d), lambda i: (0, 0)),
            pl.BlockSpec((1, ng), lambda i: (0, 0)),
            pl.BlockSpec((ng, 1), lambda i: (0, 0)),
        ],
        out_specs=[
            pl.BlockSpec((tt, n_qkv), lambda i: (i, 0)),
            pl.BlockSpec((tt, n_main - n_qkv), lambda i: (i, 0)),
            pl.BlockSpec((tt, ng), lambda i: (i, 0)),
            pl.BlockSpec((ng, tt), lambda i: (0, i)),
        ],
        out_shape=[
            jax.ShapeDtypeStruct((t, n_qkv), BF16),
            jax.ShapeDtypeStruct((t, n_main - n_qkv), F32),
            jax.ShapeDtypeStruct((t, ng), F32),
            jax.ShapeDtypeStruct((ng, t), F32),
        ],
        compiler_params=_cparams("arbitrary"),
        name="mlstm_inproj",
    )(xp, xs, mod, w_main, w_gate, w_gate_t, b_gate.reshape(1, ng), b_gate.reshape(ng, 1))


def _mlstm_gate_sums(g, gt, backward):
    L = g.shape[0]
    row = lax.broadcasted_iota(jnp.int32, (L, L), 0)
    col = lax.broadcasted_iota(jnp.int32, (L, L), 1)
    lower = col <= row
    upper = col >= row
    tri_col = _onehot(upper if backward else lower)
    tri_row = _onehot(lower if backward else upper)
    ghi, gmid, glo = _split3(g)
    b_cols = _dot(tri_col, ghi) + _dot(tri_col, gmid) + _dot(tri_col, glo)
    thi, tmid, tlo = _split3(gt)
    b_rows = _dot(thi, tri_row) + _dot(tmid, tri_row) + _dot(tlo, tri_row)
    return b_cols, b_rows


def _mlstm_heads(items):
    scale = M_DK ** -0.5
    log_scale = math.log(scale)
    L = items[0][0].shape[0]
    row = lax.broadcasted_iota(jnp.int32, (L, L), 0)
    col = lax.broadcasted_iota(jnp.int32, (L, L), 1)
    masks = {False: col <= row, True: col >= row}
    n = range(len(items))
    qh, kh, vh, li_row, lf_row, b_row, b_col, backward, state = zip(*items)
    has_state = [st is not None for st in state]
    m_prev = [st[2] if st is not None else 0.0 for st in state]

    qk = [_dot_nt(qh[i], kh[i]) for i in n]
    qc = [_dot(qh[i], state[i][0].astype(BF16)) if has_state[i] else None for i in n]
    qn = [_dot_nt(qh[i], state[i][1].astype(BF16))[:, 0:1] if has_state[i] else None for i in n]
    total = [jnp.sum(lf_row[i], axis=1, keepdims=True) for i in n]
    d = [jnp.where(masks[backward[i]], b_col[i] + (li_row[i] - b_row[i] + log_scale), NEG) for i in n]
    inter = [b_col[i] + m_prev[i] if has_state[i] else b_col[i] for i in n]
    m_t = [jnp.maximum(inter[i], jnp.max(d[i], axis=1, keepdims=True) - log_scale) for i in n]
    p = [jnp.exp(d[i] - m_t[i]) for i in n]
    s = [qk[i] * p[i] for i in n]
    den = [jnp.sum(s[i], axis=1, keepdims=True) for i in n]
    num = [_dot(s[i].astype(BF16), vh[i]) for i in n]
    a = [jnp.exp(inter[i] - m_t[i]) * scale if has_state[i] else None for i in n]
    num = [a[i] * qc[i] + num[i] if has_state[i] else num[i] for i in n]
    den = [a[i] * qn[i] + den[i] if has_state[i] else den[i] for i in n]
    hh = [num[i] / jnp.maximum(jnp.abs(den[i]), jnp.exp(-m_t[i])) for i in n]

    g_row = [total[i] - b_row[i] + li_row[i] for i in n]
    m_new = [jnp.maximum(total[i] + m_prev[i], jnp.max(g_row[i], axis=1, keepdims=True)) for i in n]
    ws_row = [jnp.exp(g_row[i] - m_new[i]) for i in n]
    kt = [kh[i].astype(F32).T for i in n]
    c_new = [_dot((kt[i] * ws_row[i]).astype(BF16), vh[i]) for i in n]
    n_new = [_dot(jnp.broadcast_to(ws_row[i], (8, L)).astype(BF16), kh[i]) for i in n]
    a0 = [jnp.exp(total[i] + m_prev[i] - m_new[i]) if has_state[i] else None for i in n]
    c_new = [a0[i] * state[i][0] + c_new[i] if has_state[i] else c_new[i] for i in n]
    n_new = [a0[i] * state[i][1] + n_new[i] if has_state[i] else n_new[i] for i in n]
    return hh, c_new, n_new, m_new


def _head_norm_gate(ht, norm_g, ogate):
    mu = jnp.mean(ht, axis=-1, keepdims=True)
    hc = ht - mu
    var = jnp.mean(hc * hc, axis=-1, keepdims=True)
    return (hc * lax.rsqrt(var + LN_EPS) * norm_g * jax.nn.sigmoid(ogate)).astype(BF16)


def _head_operands(q_ref, k_ref, v_ref, h):
    return (q_ref[:, h * M_DK:(h + 1) * M_DK], k_ref[:, h * M_DK:(h + 1) * M_DK], v_ref[:, h * M_DV:(h + 1) * M_DV])


def _gate_operands(gt, b_rows, b_cols, direction, h):
    ci = direction * 8 + h
    cf = direction * 8 + 4 + h
    return gt[ci:ci + 1, :], gt[cf:cf + 1, :], b_rows[cf:cf + 1, :], b_cols[:, cf:cf + 1]


def _mlstm_prompt_kernel(q_ref, k_ref, v_ref, g_ref, gt_ref, o_ref, ng_ref, a_ref, cf_ref, nf_ref, mf_ref):
    g, gt = g_ref[...], gt_ref[...]
    sums = [_mlstm_gate_sums(g, gt, direction == 1) for direction in (0, 1)]
    keys = [(direction, h) for direction in (0, 1) for h in range(M_HEADS)]
    items = [_head_operands(q_ref, k_ref, v_ref, h)
             + _gate_operands(gt, sums[direction][1], sums[direction][0], direction, h) + (direction == 1, None)
             for direction, h in keys]
    hh, c_new, n_new, m_new = _mlstm_heads(items)
    for i, (direction, h) in enumerate(keys):
        cf_ref[0, 0, direction, h] = c_new[i]
        nf_ref[0, 0, direction, h] = n_new[i]
        mf_ref[0, 0, direction, h] = jnp.broadcast_to(m_new[i], mf_ref.shape[-2:])
    for h in range(M_HEADS):
        sl = slice(h * M_DV, (h + 1) * M_DV)
        a_ref[:, sl] = _head_norm_gate(hh[h] + hh[M_HEADS + h], ng_ref[:, sl], o_ref[:, sl])


def _side_cast_specs(arrays, steps, step_of):
    flat = [a.reshape(-1, a.shape[-1]) for a in arrays]
    assert all(a.shape[0] % (steps * 16) == 0 for a in flat)
    specs = [pl.BlockSpec((a.shape[0] // steps, a.shape[1]), lambda *idx: (step_of(*idx), 0)) for a in flat]
    return flat, specs, [jax.ShapeDtypeStruct(a.shape, BF16) for a in flat]


def _run_side_casts(srcs, dsts):
    for src, dst in zip(srcs, dsts):
        dst[...] = src[...].astype(dst.dtype)


def _mlstm_sample_kernel(*refs, direction, n_units, n_cast):
    backward = direction == 1
    refs = list(refs)
    unit_refs = [tuple(refs.pop(0) for _ in range(5)) for _ in range(n_units)]
    c0_ref, n0_ref, m0_ref = refs.pop(0), refs.pop(0), refs.pop(0)
    if backward:
        hprev_ref = refs.pop(0)
        o_refs = [refs.pop(0) for _ in range(n_units)]
        ng_ref = refs.pop(0)
    cast_in = [refs.pop(0) for _ in range(n_cast)]
    out_ref = refs.pop(0)
    cast_out = [refs.pop(0) for _ in range(n_cast)]
    c_scr, n_scr, m_scr = refs
    _run_side_casts(cast_in, cast_out)

    @pl.when(pl.program_id(0) == 0)
    def _():
        c_scr[...] = c0_ref[...]
        n_scr[...] = n0_ref[...]
        m_scr[...] = m0_ref[...]

    keys, items = [], []
    for u, (q_ref, k_ref, v_ref, g_ref, gt_ref) in enumerate(unit_refs):
        gt = gt_ref[...]
        b_cols, b_rows = _mlstm_gate_sums(g_ref[...], gt, backward)
        for h in range(M_HEADS):
            state = (c_scr[u, h], n_scr[u, h], m_scr[u, h][0:1, 0:1])
            keys.append((u, h))
            items.append(_head_operands(q_ref, k_ref, v_ref, h) + _gate_operands(gt, b_rows, b_cols, direction, h)
                         + (backward, state))
    hh, c_new, n_new, m_new = _mlstm_heads(items)
    for i, (u, h) in enumerate(keys):
        c_scr[u, h] = c_new[i]
        n_scr[u, h] = n_new[i]
        m_scr[u, h] = jnp.broadcast_to(m_new[i], m_scr.shape[-2:])
        sl = slice(h * M_DV, (h + 1) * M_DV)
        if backward:
            out_ref[u, :, sl] = _head_norm_gate(hprev_ref[u, :, sl] + hh[i], ng_ref[:, sl], o_refs[u][:, sl])
        else:
            out_ref[u, :, sl] = hh[i]


def _mlstm_chunk_specs(block_of, L):
    return [
        pl.BlockSpec((L, M_HK), lambda s: (block_of(s), 0)),
        pl.BlockSpec((L, M_HK), lambda s: (block_of(s), 1)),
        pl.BlockSpec((L, D_MODEL), lambda s: (block_of(s), 1)),
        pl.BlockSpec((L, 16), lambda s: (block_of(s), 0)),
        pl.BlockSpec((16, L), lambda s: (0, block_of(s))),
    ]


def mlstm_prompt(qkv, g, gt, ogate, norm_g, n_prompt):
    L = MLSTM_CHUNK
    state_blk = lambda *tail: pl.BlockSpec((1, 1, 2, M_HEADS) + tail, lambda s: (s, 0, 0, 0, 0, 0))
    return pl.pallas_call(
        _mlstm_prompt_kernel,
        grid=(n_prompt,),
        in_specs=_mlstm_chunk_specs(lambda s: s, L) + [
            pl.BlockSpec((L, D_MODEL), lambda s: (s, 0)),
            pl.BlockSpec((1, D_MODEL), lambda s: (0, 0)),
        ],
        out_specs=[pl.BlockSpec((L, D_MODEL), lambda s: (s, 0)),
                   state_blk(M_DK, M_DV), state_blk(8, M_DK), state_blk(8, 128)],
        out_shape=[
            jax.ShapeDtypeStruct((n_prompt * L, D_MODEL), BF16),
            jax.ShapeDtypeStruct((n_prompt, 1, 2, M_HEADS, M_DK, M_DV), F32),
            jax.ShapeDtypeStruct((n_prompt, 1, 2, M_HEADS, 8, M_DK), F32),
            jax.ShapeDtypeStruct((n_prompt, 1, 2, M_HEADS, 8, 128), F32),
        ],
        compiler_params=_cparams("arbitrary"),
        name="mlstm_prompt",
    )(qkv, qkv, qkv, g, gt, ogate, norm_g.reshape(1, D_MODEL))


def mlstm_sample(direction, qkv, g, gt, c0, n0, m0, first_block, chunks_per_sample,
                 hprev=None, ogate=None, norm_g=None, side_casts=()):
    L = MLSTM_SAMPLE_CHUNK
    backward = direction == 1
    cps = chunks_per_sample
    n_units = c0.shape[0]
    pos = (lambda s: cps - 1 - s) if backward else (lambda s: s)
    unit_block = [functools.partial(lambda s, u: first_block + u * cps + pos(s), u=u) for u in range(n_units)]
    whole = lambda a: pl.BlockSpec(a.shape, lambda s: (0,) * a.ndim)
    in_specs, args = [], []
    for u in range(n_units):
        in_specs += _mlstm_chunk_specs(unit_block[u], L)
        args += [qkv, qkv, qkv, g, gt]
    in_specs += [whole(c0), whole(n0), whole(m0)]
    args += [c0, n0, m0]
    if backward:
        in_specs += [pl.BlockSpec((n_units, L, D_MODEL), lambda s: (0, pos(s), 0))]
        in_specs += [pl.BlockSpec((L, D_MODEL), functools.partial(lambda s, u: (unit_block[u](s), 0), u=u))
                     for u in range(n_units)]
        in_specs += [pl.BlockSpec((1, D_MODEL), lambda s: (0, 0))]
        args += [hprev] + [ogate] * n_units + [norm_g.reshape(1, D_MODEL)]
    flat, cast_specs, cast_shapes = _side_cast_specs(side_casts, cps, lambda s: s)
    outs = pl.pallas_call(
        functools.partial(_mlstm_sample_kernel, direction=direction, n_units=n_units, n_cast=len(flat)),
        grid=(cps,),
        in_specs=in_specs + cast_specs,
        out_specs=[pl.BlockSpec((n_units, L, D_MODEL), lambda s: (0, pos(s), 0))] + cast_specs,
        out_shape=[jax.ShapeDtypeStruct((n_units, cps * L, D_MODEL), BF16 if backward else F32)] + cast_shapes,
        scratch_shapes=[pltpu.VMEM(c0.shape, F32), pltpu.VMEM(n0.shape, F32), pltpu.VMEM(m0.shape, F32)],
        compiler_params=_cparams("arbitrary"),
        name="mlstm_sample_bwd" if backward else "mlstm_sample_fwd",
    )(*args, *flat)
    return [outs[0]] + [o.reshape(a.shape) for o, a in zip(outs[1:], side_casts)]


def _outproj_router_kernel(ap_ref, as_ref, x_ref, mod_ref, w_ref, lng_ref, lnb_ref, wr_ref,
                           xo_ref, up_ref, comb_ref, pos_ref, total_ref, carry_scr, *, n_prompt_tiles):
    i = pl.program_id(0)

    @pl.when(i == 0)
    def _():
        carry_scr[...] = jnp.zeros_like(carry_scr)

    a = jnp.where(i < n_prompt_tiles, ap_ref[...], as_ref[...])
    z = DEEPNORM_ALPHA * x_ref[...] + mod_ref[0, 2:3, :] * _dot(a, w_ref[...])
    xn = _layer_norm(z, lng_ref[...], lnb_ref[...])
    xo_ref[...] = xn
    u = xn * (1.0 + mod_ref[0, 4:5, :]) + mod_ref[0, 3:4, :]
    up_ref[...] = _pack_bf16_pairs(u)
    comb, pos, carry_new = _route_tokens(u.astype(BF16), wr_ref[...], carry_scr[:, 0:1])
    comb_ref[...] = comb
    pos_ref[...] = pos
    carry_scr[...] = jnp.broadcast_to(carry_new, carry_scr.shape)
    total_ref[...] = jnp.broadcast_to(carry_new, carry_scr.shape)


def mixer_outproj_router(a_p, a_s, x, mod, w, ln_g, ln_b, w_router, sample_len):
    t, d = x.shape
    ne = w_router.shape[1]
    tt = TOKEN_TILE
    npt = a_p.shape[0] // tt
    seg = functools.partial(_seg_of_tile, tile=tt, n_prompt_tok=a_p.shape[0], sample_len=sample_len)
    return pl.pallas_call(
        functools.partial(_outproj_router_kernel, n_prompt_tiles=npt),
        grid=(t // tt,),
        in_specs=_split_token_specs(tt, d, npt) + [
            pl.BlockSpec((tt, d), lambda i: (i, 0)),
            pl.BlockSpec((1, 6, d), lambda i: (seg(i), 0, 0)),
            pl.BlockSpec((d, d), lambda i: (0, 0)),
            pl.BlockSpec((1, d), lambda i: (0, 0)),
            pl.BlockSpec((1, d), lambda i: (0, 0)),
            pl.BlockSpec((ne, d), lambda i: (0, 0)),
        ],
        out_specs=[
            pl.BlockSpec((tt, d), lambda i: (i, 0)),
            pl.BlockSpec((tt, d // 2), lambda i: (i, 0)),
            pl.BlockSpec((ne, tt), lambda i: (0, i)),
            pl.BlockSpec((ne, tt), lambda i: (0, i)),
            pl.BlockSpec((ne, 128), lambda i: (0, 0)),
        ],
        out_shape=[
            jax.ShapeDtypeStruct((t, d), F32),
            jax.ShapeDtypeStruct((t, d // 2), jnp.int32),
            jax.ShapeDtypeStruct((ne, t), F32),
            jax.ShapeDtypeStruct((ne, t), F32),
            jax.ShapeDtypeStruct((ne, 128), F32),
        ],
        scratch_shapes=[pltpu.VMEM((ne, 128), F32)],
        compiler_params=_cparams("arbitrary"),
        name="mixer_outproj_router",
    )(a_p, a_s, x, mod, w, ln_g.reshape(1, d), ln_b.reshape(1, d), w_router.T)


def _swiglu_tile(x, wi, wo, dff):
    sub = FFN_SUB
    proj = lambda j: (_dot(x, wi(slice(j * sub, (j + 1) * sub))), _dot(x, wi(slice(dff + j * sub, dff + (j + 1) * sub))))
    acts, cur = [], proj(0)
    for j in range(dff // sub):
        nxt = proj(j + 1) if (j + 1) * sub < dff else None
        gp, up = cur
        acts.append((gp * jax.nn.sigmoid(gp) * up).astype(BF16))
        cur = nxt
    return _dot(jnp.concatenate(acts, axis=1), wo(slice(0, dff)))


def _mixer_ffn_kernel(ap_ref, as_ref, xp_ref, xs_ref, mod_ref, modn_ref, wmix_ref, wi_ref, wo_ref, ln_ref,
                      xo_ref, uo_ref, *, n_prompt_tiles):
    is_prompt = pl.program_id(0) < n_prompt_tiles
    a = jnp.where(is_prompt, ap_ref[...], as_ref[...])
    x = jnp.where(is_prompt, xp_ref[...], xs_ref[...])
    z = DEEPNORM_ALPHA * x + mod_ref[0, 2:3, :] * _dot(a, wmix_ref[...])
    x1 = _layer_norm(z, ln_ref[0:1, :], ln_ref[1:2, :])
    u1 = (x1 * (1.0 + mod_ref[0, 4:5, :]) + mod_ref[0, 3:4, :]).astype(BF16)
    f = _swiglu_tile(u1, lambda c: wi_ref[:, c], lambda r: wo_ref[r, :], wo_ref.shape[0])
    x2 = _layer_norm(DEEPNORM_ALPHA * x1 + mod_ref[0, 5:6, :] * f, ln_ref[2:3, :], ln_ref[3:4, :])
    xo_ref[...] = x2
    uo_ref[...] = (x2 * (1.0 + modn_ref[0, 1:2, :]) + modn_ref[0, 0:1, :]).astype(BF16)


def _resident(shape):
    return pl.BlockSpec(shape, lambda i: (0,) * len(shape), pipeline_mode=pl.Buffered(1))


def mixer_ffn(a_p, a_s, xp, xs, mod, mod_next, w_mix, w_in, w_out, ln_g, ln_b, sample_len):
    n_prompt_tok, d = xp.shape
    t = n_prompt_tok + xs.shape[0]
    tm = FFN_TOKEN_TILE
    npt = n_prompt_tok // tm
    seg = functools.partial(_seg_of_tile, tile=tm, n_prompt_tok=n_prompt_tok, sample_len=sample_len)
    ln = jnp.stack([ln_g[0], ln_b[0], ln_g[1], ln_b[1]])
    return pl.pallas_call(
        functools.partial(_mixer_ffn_kernel, n_prompt_tiles=npt),
        grid=(t // tm,),
        in_specs=_split_token_specs(tm, d, npt) + _split_token_specs(tm, d, npt) + [
            pl.BlockSpec((1, 6, d), lambda i: (seg(i), 0, 0)),
            pl.BlockSpec((1, 6, d), lambda i: (seg(i), 0, 0)),
            _resident(w_mix.shape), _resident(w_in.shape), _resident(w_out.shape), _resident(ln.shape),
        ],
        out_specs=[pl.BlockSpec((tm, d), lambda i: (i, 0)), pl.BlockSpec((tm, d), lambda i: (i, 0))],
        out_shape=[jax.ShapeDtypeStruct((t, d), F32), jax.ShapeDtypeStruct((t, d), BF16)],
        compiler_params=_cparams("arbitrary"),
        name="mixer_ffn",
    )(a_p, a_s, xp, xs, mod, mod_next, w_mix, w_in, w_out, ln)


def _modulation_tables(c, c_ctx, ada_w, ada_b):
    n_s = c.shape[0]
    cvec = jnp.concatenate([c_ctx[None, :], c, jnp.zeros((8 - 1 - n_s, c.shape[1]), F32)], axis=0)
    mod = adaln(cvec, ada_w, ada_b)
    return mod[:, :1 + n_s, :].reshape(ada_w.shape[0], 1 + n_s, 6, c.shape[1])


def _layer0(xp, xs, mod0, mod1, state_c, state_n, state_m, ln_g, ln_b, w_in, b_gates, norm_g, w_out,
            ffn_w_in, ffn_w_out, n_prompt, prompt_len, n_sample, sample_len, later_weights):
    n_prompt_tok = n_prompt * prompt_len
    n_main = 2 * M_HK + 2 * D_MODEL
    w_t = w_in.T
    qkv, ogate, g, gt = mlstm_inproj(xp, xs, mod0, w_t[:n_main].astype(BF16), w_t[n_main:].T.astype(BF16),
                                     w_t[n_main:].astype(BF16), b_gates.reshape(-1), sample_len)
    npc = n_prompt_tok // MLSTM_SAMPLE_CHUNK
    cps = sample_len // MLSTM_SAMPLE_CHUNK
    rep = lambda a: jnp.broadcast_to(a[..., None, :], a.shape[:-1] + (8, a.shape[-1]))
    n0 = rep(state_n)
    m0 = jnp.broadcast_to(state_m[..., None, None], state_m.shape + (8, 128))
    a_p, new_c, nf, mf = mlstm_prompt(qkv, g, gt, ogate, norm_g, n_prompt)
    hf, w_out, ffn_w_in, ffn_w_out = mlstm_sample(0, qkv, g, gt, state_c[:, 0], n0[:, 0], m0[:, 0], npc, cps,
                                                  side_casts=(w_out, ffn_w_in, ffn_w_out))
    a_s, *later_weights = mlstm_sample(1, qkv, g, gt, state_c[:, 1], n0[:, 1], m0[:, 1], npc, cps,
                                       hprev=hf, ogate=ogate, norm_g=norm_g, side_casts=later_weights)
    x2, u2 = mixer_ffn(a_p, a_s.reshape(-1, D_MODEL), xp, xs, mod0, mod1, w_out, ffn_w_in, ffn_w_out, ln_g, ln_b,
                       sample_len)
    return x2, u2, new_c, nf[..., 0, :], mf[..., 0, 0], later_weights


def _qkv_kernel(u_ref, w_ref, qkv_ref, *kv_refs):
    tt, d = u_ref.shape
    u = u_ref[...]

    def head_major(out_ref, p):
        for h in range(NA_HEADS):
            out_ref[pl.ds(h, tt, stride=NA_HEADS), :] = p[:, h * NA_DH:(h + 1) * NA_DH]

    pq = _dot(u, w_ref[:, :d])
    pk = _dot(u, w_ref[:, d:2 * d])
    qkv_ref[:, :d] = (pq * ATTN_Q_SCALE).astype(BF16)
    pv = _dot(u, w_ref[:, 2 * d:])
    qkv_ref[:, d:2 * d] = pk.astype(BF16)
    if kv_refs:
        head_major(kv_refs[0], pk)
    qkv_ref[:, 2 * d:] = pv.astype(BF16)
    if kv_refs:
        head_major(kv_refs[1], pv)


def na_qkv(u, w, first_tok, n_tok, with_kv):
    d = u.shape[1]
    tt = TOKEN_TILE
    first = first_tok // tt
    out_specs = [pl.BlockSpec((tt, 3 * d), lambda i: (i, 0))]
    out_shape = [jax.ShapeDtypeStruct((n_tok, 3 * d), BF16)]
    if with_kv:
        out_specs += [pl.BlockSpec((tt * NA_HEADS, NA_DH), lambda i: (i, 0))] * 2
        out_shape += [jax.ShapeDtypeStruct((n_tok * NA_HEADS, NA_DH), F32)] * 2
    return pl.pallas_call(
        _qkv_kernel,
        grid=(n_tok // tt,),
        in_specs=[pl.BlockSpec((tt, d), lambda i: (first + i, 0)), pl.BlockSpec((d, 3 * d), lambda i: (0, 0))],
        out_specs=out_specs,
        out_shape=out_shape,
        compiler_params=_cparams("arbitrary"),
        name="na_qkv_prompt" if with_kv else "na_qkv_latent",
    )(u, w)


ATTN_LOG2E = math.log2(math.e)
ATTN_Q_SCALE = NA_DH ** -0.5 * ATTN_LOG2E


def _head_pair_masks():
    lane = lax.broadcasted_iota(jnp.int32, (1, 2 * NA_DH), 1)
    return (_onehot(lane < NA_DH), _onehot(lane >= NA_DH))


def _softmax_pv(score_blocks, value_blocks):
    mx = None
    for s in score_blocks:
        bm = jnp.max(s, axis=1, keepdims=True)
        mx = bm if mx is None else jnp.maximum(mx, bm)
    acc, denom = None, None
    for s, v in zip(score_blocks, value_blocks):
        p = jnp.exp2(s - mx)
        ps = jnp.sum(p, axis=1, keepdims=True)
        pv = _dot(p.astype(BF16), v)
        acc = pv if acc is None else acc + pv
        denom = ps if denom is None else denom + ps
    return acc / denom


def _ctx_attn_kernel(q_ref, k_ref, v_ref, o_ref):
    masks = _head_pair_masks()
    for hp in range(NA_HEADS // 2):
        sl = slice(hp * 2 * NA_DH, (hp + 1) * 2 * NA_DH)
        q2, k2, v2 = q_ref[:, sl], k_ref[:, sl], v_ref[:, sl]
        o2 = None
        for msk in masks:
            s = _dot_nt(q2 * msk, k2)
            o = _softmax_pv([s], [v2 * msk])
            o2 = o if o2 is None else o2 + o
        o_ref[:, sl] = o2.astype(o_ref.dtype)


def context_attention(qkv, n_prompt, prompt_len):
    t = n_prompt * prompt_len
    d = D_MODEL
    return pl.pallas_call(
        _ctx_attn_kernel,
        grid=(n_prompt,),
        in_specs=[
            pl.BlockSpec((prompt_len, d), lambda b: (b, 0)),
            pl.BlockSpec((prompt_len, d), lambda b: (b, 1)),
            pl.BlockSpec((prompt_len, d), lambda b: (b, 2)),
        ],
        out_specs=pl.BlockSpec((prompt_len, d), lambda b: (b, 0)),
        out_shape=jax.ShapeDtypeStruct((t, d), BF16),
        compiler_params=_cparams("arbitrary"),
        name="context_attention",
    )(qkv, qkv, qkv)


def _na_kernel(tile_ref, q_ref, kp_ref, kc_ref, kn_ref, vp_ref, vc_ref, vn_ref, kctx_ref, vctx_ref, bias_ref,
               *refs):
    n_cast = (len(refs) - 1) // 2
    cast_in, o_ref, cast_out = refs[:n_cast], refs[n_cast], refs[n_cast + 1:]
    _run_side_casts(cast_in, cast_out)
    j = pl.program_id(1)
    nblk = pl.num_programs(1)
    m = q_ref.shape[0]
    halo = NA_HALO * GRID_W
    n_pairs = (NA_QROWS + 2 * NA_HALO) // 2
    variant = jnp.where(j == 0, 0, jnp.where(j == nblk - 1, 2, 1))
    kinds = [[tile_ref[(variant * NA_QROWS + rq) * n_pairs + kp] for kp in range(n_pairs)]
             for rq in range(NA_QROWS)]

    def bias_of(head):
        return jnp.concatenate(
            [jnp.concatenate([bias_ref[kinds[rq][kp], head] for kp in range(n_pairs)], axis=1)
             for rq in range(NA_QROWS)], axis=0)

    masks = _head_pair_masks()
    for hp in range(NA_HEADS // 2):
        sl = slice(hp * 2 * NA_DH, (hp + 1) * 2 * NA_DH)
        q2 = q_ref[:, sl]
        kloc = jnp.concatenate([kp_ref[m - halo:, sl], kc_ref[:, sl], kn_ref[:halo, sl]], axis=0)
        vloc = jnp.concatenate([vp_ref[m - halo:, sl], vc_ref[:, sl], vn_ref[:halo, sl]], axis=0)
        kctx = kctx_ref[0][:, sl]
        vctx = vctx_ref[0][:, sl]
        o2 = None
        for half, msk in enumerate(masks):
            qm = q2 * msk
            s_loc = _dot_nt(qm, kloc) + bias_of(2 * hp + half)
            s_ctx = _dot_nt(qm, kctx)
            o = _softmax_pv([s_loc, s_ctx], [vloc * msk, vctx * msk])
            o2 = o if o2 is None else o2 + o
        o_ref[:, sl] = o2.astype(o_ref.dtype)


def _na_bias_kernel(rpb_ref, o_ref, *, kinds, n_heads):
    nc, npos = rpb_ref.shape[1], o_ref.shape[2]
    pair_shift = (2 * GRID_W).bit_length() - 1
    c = lax.broadcasted_iota(jnp.int32, (nc, npos), 0)
    pos = lax.broadcasted_iota(jnp.int32, (nc, npos), 1)
    qc = pos >> pair_shift
    kc = pos & (GRID_W - 1)
    c_start = jnp.clip(qc - NA_COLS // 2, 0, GRID_W - NA_COLS)
    col_in = jnp.logical_and(kc >= c_start, kc < c_start + NA_COLS)
    dc = jnp.clip(kc - qc + NA_COLS - 1, 0, 2 * NA_COLS - 2)
    hit = jnp.logical_and(c == dc, col_in)
    right = (pos & GRID_W) != 0
    r1, r2, r3 = _split3(rpb_ref[...] * ATTN_LOG2E)

    def placed(side):
        sel = _onehot(jnp.logical_and(hit, right == side))
        return _dot(r1, sel) + _dot(r2, sel) + _dot(r3, sel)

    left, rght = placed(False), placed(True)
    rows = lambda v, dr: v[dr * n_heads:(dr + 1) * n_heads]
    for k, (dl, dr) in enumerate(kinds):
        shown = col_in[0:1, :]
        if dl is None or dr is None:
            side_ok = jnp.zeros_like(shown) if dl is None and dr is None else (right[0:1, :] == (dl is None))
            shown = jnp.logical_and(shown, side_ok)
        parts = ([rows(left, dl)] if dl is not None else []) + ([rows(rght, dr)] if dr is not None else [])
        val = sum(parts) if parts else jnp.zeros((n_heads, npos), F32)
        o_ref[k] = jnp.where(shown, val, NEG)


def _na_bias_plan():
    ndr = 2 * NA_ROWS - 1
    assert NA_QROWS >= NA_ROWS // 2 and NA_HALO == NA_ROWS // 2 and (NA_QROWS + 2 * NA_HALO) % 2 == 0
    ok = lambda dr: dr if dr is not None and 0 <= dr < ndr else None
    kinds = [(ok(dr), ok(dr + 1)) for dr in range(-1, ndr)]
    first_key_row = (lambda rq: NA_HALO, lambda rq: rq, lambda rq: NA_QROWS + NA_HALO - NA_ROWS)
    table = []
    for first in first_key_row:
        for rq in range(NA_QROWS):
            for kp in range((NA_QROWS + 2 * NA_HALO) // 2):
                drs = []
                for kk in (2 * kp, 2 * kp + 1):
                    visible = first(rq) <= kk < first(rq) + NA_ROWS
                    drs.append(kk - rq + NA_ROWS - 1 - NA_HALO if visible else None)
                kind = (drs[0], drs[1])
                if kind not in kinds:
                    kinds.append(kind)
                table.append(kinds.index(kind))
    return kinds, table


def _na_bias_tiles(rpb):
    nh, ndr, ndc = rpb.shape
    nc = 32
    kinds, table = _na_bias_plan()
    nk = len(kinds)
    rows = jnp.pad(jnp.transpose(rpb, (1, 0, 2)), ((0, 0), (0, 0), (0, nc - ndc))).reshape(ndr * nh, nc)
    npos = GRID_W * 2 * GRID_W
    tiles = pl.pallas_call(
        functools.partial(_na_bias_kernel, kinds=tuple(kinds), n_heads=nh),
        out_shape=jax.ShapeDtypeStruct((nk, nh, npos), F32),
        compiler_params=pltpu.CompilerParams(vmem_limit_bytes=VMEM_LIMIT_BYTES),
        name="na_bias",
    )(rows)
    return tiles.reshape(nk, nh, GRID_W, 2 * GRID_W), jnp.asarray(table, jnp.int32)


def neighbourhood_attention(qkv, k_ctx, v_ctx, rpb, n_sample, sample_len, side_casts=()):
    d = D_MODEL
    m = NA_QROWS * GRID_W
    nblk = sample_len // m
    assert nblk >= 3
    bias, table = _na_bias_tiles(rpb)
    steps = n_sample * nblk

    def blk(col, off):
        return pl.BlockSpec((m, d), lambda b, j, tbl: (b * nblk + jnp.clip(j + off, 0, nblk - 1), col))

    flat, slab_specs, slab_shapes = _side_cast_specs(side_casts, steps, lambda b, j, tbl: b * nblk + j)
    outs = pl.pallas_call(
        _na_kernel,
        grid_spec=pltpu.PrefetchScalarGridSpec(
            num_scalar_prefetch=1,
            grid=(n_sample, nblk),
            in_specs=[
                blk(0, 0), blk(1, -1), blk(1, 0), blk(1, 1), blk(2, -1), blk(2, 0), blk(2, 1),
                pl.BlockSpec((1,) + k_ctx.shape[1:], lambda b, j, tbl: (b, 0, 0), pipeline_mode=pl.Buffered(1)),
                pl.BlockSpec((1,) + v_ctx.shape[1:], lambda b, j, tbl: (b, 0, 0), pipeline_mode=pl.Buffered(1)),
                pl.BlockSpec(bias.shape, lambda b, j, tbl: (0, 0, 0, 0), pipeline_mode=pl.Buffered(1)),
            ] + slab_specs,
            out_specs=[pl.BlockSpec((m, d), lambda b, j, tbl: (b * nblk + j, 0))] + slab_specs,
        ),
        out_shape=[jax.ShapeDtypeStruct((n_sample * sample_len, d), BF16)] + slab_shapes,
        compiler_params=_cparams("arbitrary", "arbitrary"),
        name="neighbourhood_attention",
    )(table, qkv, qkv, qkv, qkv, qkv, qkv, qkv, k_ctx, v_ctx, bias, *flat)
    return [outs[0]] + [o.reshape(a.shape) for o, a in zip(outs[1:], side_casts)]


MOE_ROW_TILE = 512
SC_CORES = 2
SC_SUBCORES = 16
SC_GATHER_CHUNK = 128

def _route_tokens(u, wt, carry):
    w1, w2, w3 = _split3(wt)
    lg = _dot_nt(w1, u) + _dot_nt(w2, u) + _dot_nt(w3, u)
    ne, nt = lg.shape
    e = lax.broadcasted_iota(jnp.int32, lg.shape, 0)
    m1 = jnp.max(lg, axis=0, keepdims=True)
    i1 = jnp.min(jnp.where(lg == m1, e, ne), axis=0, keepdims=True)
    sel1 = e == i1
    lg2 = jnp.where(sel1, -jnp.inf, lg)
    m2 = jnp.max(lg2, axis=0, keepdims=True)
    i2 = jnp.min(jnp.where(lg2 == m2, e, ne), axis=0, keepdims=True)
    sel2 = e == i2
    ex = jnp.exp(m2 - m1)
    wa = 1.0 / (1.0 + ex)
    wb = ex / (1.0 + ex)
    comb = jnp.where(sel1, wa, jnp.where(sel2, wb, 0.0))
    assign = jnp.logical_or(sel1, sel2)
    a = jnp.where(assign, 1.0, 0.0)
    row = lax.broadcasted_iota(jnp.int32, (nt, nt), 0)
    col = lax.broadcasted_iota(jnp.int32, (nt, nt), 1)
    tri = _onehot(row < col)
    rank = _dot(a.astype(BF16), tri) + carry
    pos = jnp.where(assign, rank, -1.0)
    return comb, pos, carry + jnp.sum(a, axis=1, keepdims=True)


def _moe_plan(totals, pos, comb, n_tok):
    ne = totals.shape[0]
    tm = MOE_ROW_TILE
    n_tiles = (2 * n_tok) // tm + ne
    tiles_per = (totals + tm - 1) // tm
    tile_end = jnp.cumsum(tiles_per)
    n_used = tile_end[-1]
    base_rows = (tile_end - tiles_per) * tm
    ti = jnp.arange(n_tiles, dtype=jnp.int32)
    tile_expert = jnp.minimum(jnp.sum((ti[:, None] >= tile_end[None, :]).astype(jnp.int32), axis=1), ne - 1)
    last_expert = tile_expert[jnp.maximum(n_used - 1, 0)]
    tile_expert = jnp.where(ti < n_used, tile_expert, last_expert)
    routed = pos >= 0.0
    row = base_rows[:, None] + pos.astype(jnp.int32)
    row_a = jnp.min(jnp.where(routed, row, n_tiles * tm), axis=0)
    row_b = jnp.max(jnp.where(routed, row, -1), axis=0)
    w_a = jnp.sum(jnp.where(jnp.logical_and(routed, row == row_a[None, :]), comb, 0.0), axis=0)
    w_b = jnp.sum(jnp.where(jnp.logical_and(routed, row == row_b[None, :]), comb, 0.0), axis=0)
    tile_rows = jnp.clip(totals[tile_expert] - (ti - (tile_end - tiles_per)[tile_expert]) * tm, 0, tm)
    tile_rows = jnp.where(ti < n_used, tile_rows, 0)
    return dict(n_rows=n_tiles * tm, tile_expert=tile_expert.astype(jnp.int32),
                n_used=n_used.reshape(1).astype(jnp.int32), tile_rows=tile_rows.astype(jnp.int32),
                token_rows=jnp.concatenate([row_a, row_b]).astype(jnp.int32),
                token_weights=jnp.concatenate([w_a[None], w_b[None], jnp.zeros((6, n_tok), F32)], axis=0))


def _pack_bf16_pairs(x):
    half = x.shape[1] // 2
    bits = pltpu.bitcast(x.astype(BF16).astype(F32), jnp.int32)
    return (bits[:, :half] & jnp.int32(-65536)) | lax.shift_right_logical(bits[:, half:], jnp.int32(16))


def _unpack_bf16_pairs(p):
    hi = pltpu.bitcast(p & jnp.int32(-65536), F32)
    lo = pltpu.bitcast(lax.shift_left(p, jnp.int32(16)), F32)
    return jnp.concatenate([hi, lo], axis=1)


def sc_gather_rows(table, idx):
    _, w = table.shape
    b = idx.shape[0]
    workers = SC_CORES * SC_SUBCORES
    ch = SC_GATHER_CHUNK
    n_chunks = b // (workers * ch)
    assert n_chunks * workers * ch == b
    mesh = plsc.VectorSubcoreMesh(core_axis_name="c", subcore_axis_name="s", num_cores=SC_CORES,
                                  num_subcores=SC_SUBCORES)

    @functools.partial(
        pl.kernel, mesh=mesh, out_type=jax.ShapeDtypeStruct((b, w), table.dtype),
        scratch_types=[pltpu.VMEM((ch,), jnp.int32), pltpu.VMEM((ch, w), table.dtype), pltpu.SemaphoreType.DMA])
    def gather(table_hbm, idx_hbm, out_hbm, idx_v, rows_v, sem):
        worker = lax.axis_index("s") * SC_CORES + lax.axis_index("c")

        @pl.loop(0, n_chunks)
        def _(c):
            base = (worker * n_chunks + c) * ch
            pltpu.sync_copy(idx_hbm.at[pl.ds(base, ch)], idx_v)
            pltpu.async_copy(table_hbm.at[idx_v], rows_v, sem).wait()
            pltpu.sync_copy(rows_v, out_hbm.at[pl.ds(base, ch)])

    return gather(table, idx)


def sc_scatter_rows(rows, idx, n_out):
    t, w = rows.shape
    copies = idx.shape[0] // t
    workers = SC_CORES * SC_SUBCORES
    ch = SC_GATHER_CHUNK
    n_chunks = t // (workers * ch)
    assert n_chunks * workers * ch == t and copies * t == idx.shape[0]
    mesh = plsc.VectorSubcoreMesh(core_axis_name="c", subcore_axis_name="s", num_cores=SC_CORES,
                                  num_subcores=SC_SUBCORES)

    @functools.partial(
        pl.kernel, mesh=mesh, out_type=jax.ShapeDtypeStruct((n_out, w), rows.dtype),
        scratch_types=[pltpu.VMEM((ch,), jnp.int32), pltpu.VMEM((ch, w), rows.dtype), pltpu.SemaphoreType.DMA])
    def scatter(rows_hbm, idx_hbm, out_hbm, idx_v, rows_v, sem):
        worker = lax.axis_index("s") * SC_CORES + lax.axis_index("c")

        @pl.loop(0, n_chunks)
        def _(c):
            base = (worker * n_chunks + c) * ch
            pltpu.sync_copy(rows_hbm.at[pl.ds(base, ch)], rows_v)
            for k in range(copies):
                pltpu.sync_copy(idx_hbm.at[pl.ds(k * t + base, ch)], idx_v)
                pltpu.async_copy(rows_v, out_hbm.at[idx_v], sem).wait()

    return scatter(rows, idx)


def _gffn_kernel(te_ref, nu_ref, nr_ref, x_ref, wi_ref, wo_ref, y_ref):
    i = pl.program_id(0)
    used = i < nu_ref[0]

    @pl.when(used)
    def _():
        packed = x_ref[...]
        row = lax.broadcasted_iota(jnp.int32, packed.shape, 0)
        x = _unpack_bf16_pairs(jnp.where(row < nr_ref[i], packed, 0)).astype(BF16)
        y = _swiglu_tile(x, lambda c: wi_ref[0, :, c], lambda r: wo_ref[0, r, :], wo_ref.shape[1])
        y_ref[...] = _pack_bf16_pairs(y)

    @pl.when(jnp.logical_not(used))
    def _():
        y_ref[...] = jnp.zeros_like(y_ref)


def moe_grouped_ffn(xs, w_in, w_out, plan):
    nr, half = xs.shape
    ne, dff, d = w_out.shape
    tm = MOE_ROW_TILE
    return pl.pallas_call(
        _gffn_kernel,
        grid_spec=pltpu.PrefetchScalarGridSpec(
            num_scalar_prefetch=3,
            grid=(nr // tm,),
            in_specs=[
                pl.BlockSpec((tm, half), lambda i, te, nu, tr: (jnp.minimum(i, nu[0] - 1), 0)),
                pl.BlockSpec((1, d, 2 * dff), lambda i, te, nu, tr: (te[i], 0, 0)),
                pl.BlockSpec((1, dff, d), lambda i, te, nu, tr: (te[i], 0, 0)),
            ],
            out_specs=pl.BlockSpec((tm, half), lambda i, te, nu, tr: (i, 0)),
        ),
        out_shape=jax.ShapeDtypeStruct((nr, half), jnp.int32),
        compiler_params=_cparams("arbitrary"),
        name="moe_grouped_ffn",
    )(plan["tile_expert"], plan["n_used"], plan["tile_rows"], xs, w_in, w_out)


def _combine_kernel(ya_ref, yb_ref, w_ref, x_ref, mod_ref, lng_ref, lnb_ref, op_ref, os_ref, *, n_prompt_tiles):
    i = pl.program_id(0)
    w = w_ref[...].T
    moe = w[:, 0:1] * _unpack_bf16_pairs(ya_ref[...]) + w[:, 1:2] * _unpack_bf16_pairs(yb_ref[...])
    z = DEEPNORM_ALPHA * x_ref[...] + mod_ref[0, 5:6, :] * moe
    xn = _layer_norm(z, lng_ref[...], lnb_ref[...])

    @pl.when(i < n_prompt_tiles)
    def _():
        op_ref[...] = xn

    @pl.when(i >= n_prompt_tiles)
    def _():
        os_ref[...] = xn


def moe_combine(y_tok, weights, x, mod, ln_g, ln_b, n_prompt_tok, sample_len):
    t, d = x.shape
    tt = TOKEN_TILE
    nt = t // tt
    npt = n_prompt_tok // tt
    seg = functools.partial(_seg_of_tile, tile=tt, n_prompt_tok=n_prompt_tok, sample_len=sample_len)
    return pl.pallas_call(
        functools.partial(_combine_kernel, n_prompt_tiles=npt),
        grid=(nt,),
        in_specs=[
            pl.BlockSpec((tt, d // 2), lambda i: (i, 0)),
            pl.BlockSpec((tt, d // 2), lambda i: (nt + i, 0)),
            pl.BlockSpec((8, tt), lambda i: (0, i)),
            pl.BlockSpec((tt, d), lambda i: (i, 0)),
            pl.BlockSpec((1, 6, d), lambda i: (seg(i), 0, 0)),
            pl.BlockSpec((1, d), lambda i: (0, 0)),
            pl.BlockSpec((1, d), lambda i: (0, 0)),
        ],
        out_specs=[
            pl.BlockSpec((tt, d), lambda i: (jnp.minimum(i, npt - 1), 0)),
            pl.BlockSpec((tt, d), lambda i: (jnp.maximum(i - npt, 0), 0)),
        ],
        out_shape=[jax.ShapeDtypeStruct((n_prompt_tok, d), F32), jax.ShapeDtypeStruct((t - n_prompt_tok, d), F32)],
        compiler_params=_cparams("arbitrary"),
        name="moe_combine",
    )(y_tok, y_tok, weights, x, mod, ln_g.reshape(1, d), ln_b.reshape(1, d))


def _layer1(x, u, mod1, cache_k, cache_v, ln_g, ln_b, w_qkv, rpb, w_out, w_router, moe_w_in, moe_w_out,
            n_prompt, prompt_len, n_sample, sample_len):
    n_prompt_tok = n_prompt * prompt_len
    d = D_MODEL
    qkv_p, k_p, v_p = na_qkv(u, w_qkv, 0, n_prompt_tok, with_kv=True)
    qkv_s, = na_qkv(u, w_qkv, n_prompt_tok, u.shape[0] - n_prompt_tok, with_kv=False)
    attn_p = context_attention(qkv_p, n_prompt, prompt_len)
    k_ctx = cache_k.reshape(n_sample, -1, d).astype(BF16)
    v_ctx = cache_v.reshape(n_sample, -1, d).astype(BF16)
    attn_s, moe_w_in, moe_w_out = neighbourhood_attention(qkv_s, k_ctx, v_ctx, rpb, n_sample, sample_len,
                                                          side_casts=(moe_w_in, moe_w_out))
    x1, u1_packed, comb, pos, totals = mixer_outproj_router(
        attn_p, attn_s, x, mod1, w_out, ln_g[0], ln_b[0], w_router, sample_len)
    plan = _moe_plan(totals[:, 0].astype(jnp.int32), pos, comb, x.shape[0])
    xs = sc_scatter_rows(u1_packed, plan["token_rows"], plan["n_rows"])
    y = moe_grouped_ffn(xs, moe_w_in, moe_w_out, plan)
    y_tok = sc_gather_rows(y, plan["token_rows"])
    y_p, y_s = moe_combine(y_tok, plan["token_weights"], x1, mod1, ln_g[1], ln_b[1], n_prompt_tok, sample_len)
    return y_p, y_s, k_p, v_p


def kernel(x_prompt, x_sample, state_mlstm_C, state_mlstm_n, state_mlstm_m, cache_na_k, cache_na_v, c, c_ctx,
           ada_w, ada_b, ln_g, ln_b, mlstm_w_in, mlstm_b_gates, mlstm_norm_g, mlstm_w_out, na_w_qkv, na_rpb,
           na_w_out, ffn_w_in, ffn_w_out, moe_w_router, moe_w_in, moe_w_out):
    n_prompt, prompt_len, d = x_prompt.shape
    n_sample, sample_len, _ = x_sample.shape
    assert d == D_MODEL and prompt_len == MLSTM_CHUNK and sample_len % MLSTM_SAMPLE_CHUNK == 0
    assert ada_w.shape[0] == DEPTH == 2
    mod = _modulation_tables(c, c_ctx, ada_w, ada_b)
    x2, u2, new_c, new_n, new_m, (w_qkv, na_w_out) = _layer0(
        x_prompt.reshape(-1, d), x_sample.reshape(-1, d), mod[0], mod[1],
        state_mlstm_C[:, 0], state_mlstm_n[:, 0], state_mlstm_m[:, 0], ln_g[0], ln_b[0], mlstm_w_in[0],
        mlstm_b_gates[0], mlstm_norm_g[0], mlstm_w_out[0], ffn_w_in[0], ffn_w_out[0],
        n_prompt, prompt_len, n_sample, sample_len, later_weights=(na_w_qkv[0], na_w_out[0]))
    y_p, y_s, k_p, v_p = _layer1(
        x2, u2, mod[1], cache_na_k[:, 0], cache_na_v[:, 0], ln_g[1], ln_b[1], w_qkv, na_rpb[0], na_w_out,
        moe_w_router[0], moe_w_in[0], moe_w_out[0], n_prompt, prompt_len, n_sample, sample_len)
    kv_shape = (n_prompt, 1, prompt_len, NA_HEADS, NA_DH)
    return (y_p.reshape(x_prompt.shape), y_s.reshape(x_sample.shape), new_c, new_n, new_m,
            k_p.reshape(kv_shape), v_p.reshape(kv_shape))
```

```python
import functools
import math

import jax
import jax.numpy as jnp
from jax import lax
from jax.experimental import pallas as pl
from jax.experimental.pallas import tpu as pltpu
from jax.experimental.pallas import tpu_sc as plsc

F32 = jnp.float32
BF16 = jnp.bfloat16

D_MODEL = 1024
DEPTH = 2
GRID_W = 64
M_HEADS = 4
M_DK = 128
M_DV = 256
M_HK = M_HEADS * M_DK
NA_HEADS = 16
NA_DH = 64
NA_ROWS = 8
NA_COLS = 16
DEEPNORM_ALPHA = (2.0 * DEPTH) ** 0.25
LN_EPS = 1e-5
NEG = -1e30

VMEM_LIMIT_BYTES = 56 * 1024 * 1024

MLSTM_CHUNK = 256
MLSTM_SAMPLE_CHUNK = 256
TOKEN_TILE = 512
STREAM_TOKEN_TILE = 1024
FFN_TOKEN_TILE = 512
FFN_SUB = 256
NA_QROWS = 4
NA_HALO = 4


def _cparams(*sem):
    return pltpu.CompilerParams(dimension_semantics=sem, vmem_limit_bytes=VMEM_LIMIT_BYTES)


def _dot(a, b):
    return jnp.dot(a, b, preferred_element_type=F32)


def _dot_nt(a, b):
    return lax.dot_general(a, b, (((1,), (1,)), ((), ())), preferred_element_type=F32)


def _onehot(mask):
    return jnp.where(mask, 1.0, 0.0).astype(BF16)


def _split3(x):
    hi = x.astype(BF16)
    r = x - hi.astype(F32)
    mid = r.astype(BF16)
    lo = (r - mid.astype(F32)).astype(BF16)
    return hi, mid, lo


def _layer_norm(z, g, b):
    mu = jnp.mean(z, axis=-1, keepdims=True)
    zc = z - mu
    var = jnp.mean(zc * zc, axis=-1, keepdims=True)
    return zc * lax.rsqrt(var + LN_EPS) * g + b


def _log_sigmoid(x):
    return jnp.minimum(x, 0.0) - jnp.log(1.0 + jnp.exp(-jnp.abs(x)))


def _seg_of_tile(i, tile, n_prompt_tok, sample_len):
    tok = i * tile
    return jnp.where(tok < n_prompt_tok, 0, 1 + (tok - n_prompt_tok) // sample_len)


def _adaln_kernel(c_ref, w_ref, b_ref, o_ref):
    c = c_ref[...]
    a = (c * jax.nn.sigmoid(c)).astype(BF16)
    o_ref[0] = _dot(a, w_ref[0].astype(BF16)) + b_ref[0]


def adaln(cvec, ada_w, ada_b):
    depth, d, n = ada_w.shape
    tn = 1536
    return pl.pallas_call(
        _adaln_kernel,
        grid=(depth, n // tn),
        in_specs=[
            pl.BlockSpec((8, d), lambda l, j: (0, 0)),
            pl.BlockSpec((1, d, tn), lambda l, j: (l, 0, j)),
            pl.BlockSpec((1, 1, tn), lambda l, j: (l, 0, j)),
        ],
        out_specs=pl.BlockSpec((1, 8, tn), lambda l, j: (l, 0, j)),
        out_shape=jax.ShapeDtypeStruct((depth, 8, n), F32),
        compiler_params=_cparams("arbitrary", "arbitrary"),
        name="adaln",
    )(cvec, ada_w, ada_b.reshape(depth, 1, n))


def _inproj_kernel(xp_ref, xs_ref, mod_ref, w_ref, wg_ref, wgt_ref, bg_ref, bgt_ref,
                   qkv_ref, o_ref, g_ref, gt_ref, *, n_prompt_tiles):
    x = jnp.where(pl.program_id(0) < n_prompt_tiles, xp_ref[...], xs_ref[...])
    u = (x * (1.0 + mod_ref[0, 1:2, :]) + mod_ref[0, 0:1, :]).astype(BF16)
    nqkv = qkv_ref.shape[1]
    half = nqkv // 2
    g = _dot(u, wg_ref[...]) + bg_ref[...]
    gt = _dot_nt(wgt_ref[...], u) + bgt_ref[...]
    p0 = _dot_nt(u, w_ref[:half, :])
    col = lax.broadcasted_iota(jnp.int32, g.shape, 1)
    g_ref[...] = jnp.where(((col >> 2) & 1) == 1, _log_sigmoid(g), g)
    p1 = _dot_nt(u, w_ref[half:nqkv, :])
    qkv_ref[:, :half] = p0.astype(BF16)
    row = lax.broadcasted_iota(jnp.int32, gt.shape, 0)
    gt_ref[...] = jnp.where(((row >> 2) & 1) == 1, _log_sigmoid(gt), gt)
    p2 = _dot_nt(u, w_ref[nqkv:, :])
    qkv_ref[:, half:] = p1.astype(BF16)
    o_ref[...] = p2


def _split_token_specs(tile, d, n_prompt_tiles):
    return [pl.BlockSpec((tile, d), lambda i: (jnp.minimum(i, n_prompt_tiles - 1), 0)),
            pl.BlockSpec((tile, d), lambda i: (jnp.maximum(i - n_prompt_tiles, 0), 0))]


def mlstm_inproj(xp, xs, mod, w_main, w_gate, w_gate_t, b_gate, sample_len):
    n_prompt_tok, d = xp.shape
    t = n_prompt_tok + xs.shape[0]
    n_main = w_main.shape[0]
    n_qkv = 2 * M_HK + D_MODEL
    ng = w_gate.shape[1]
    tt = TOKEN_TILE
    npt = n_prompt_tok // tt
    seg = functools.partial(_seg_of_tile, tile=tt, n_prompt_tok=n_prompt_tok, sample_len=sample_len)
    return pl.pallas_call(
        functools.partial(_inproj_kernel, n_prompt_tiles=npt),
        grid=(t // tt,),
        in_specs=_split_token_specs(tt, d, npt) + [
            pl.BlockSpec((1, 6, d), lambda i: (seg(i), 0, 0)),
            pl.BlockSpec((n_main, d), lambda i: (0, 0)),
            pl.BlockSpec((d, ng), lambda i: (0, 0)),
            pl.BlockSpec((ng, d), lambda i: (0, 0)),
            pl.BlockSpec((1, ng), lambda i: (0, 0)),
            pl.BlockSpec((ng, 1), lambda i: (0, 0)),
        ],
        out_specs=[
            pl.BlockSpec((tt, n_qkv), lambda i: (i, 0)),
            pl.BlockSpec((tt, n_main - n_qkv), lambda i: (i, 0)),
            pl.BlockSpec((tt, ng), lambda i: (i, 0)),
            pl.BlockSpec((ng, tt), lambda i: (0, i)),
        ],
        out_shape=[
            jax.ShapeDtypeStruct((t, n_qkv), BF16),
            jax.ShapeDtypeStruct((t, n_main - n_qkv), F32),
            jax.ShapeDtypeStruct((t, ng), F32),
            jax.ShapeDtypeStruct((ng, t), F32),
        ],
        compiler_params=_cparams("arbitrary"),
        name="mlstm_inproj",
    )(xp, xs, mod, w_main, w_gate, w_gate_t, b_gate.reshape(1, ng), b_gate.reshape(ng, 1))


def _mlstm_gate_sums(g, gt, backward):
    L = g.shape[0]
    row = lax.broadcasted_iota(jnp.int32, (L, L), 0)
    col = lax.broadcasted_iota(jnp.int32, (L, L), 1)
    lower = col <= row
    upper = col >= row
    tri_col = _onehot(upper if backward else lower)
    tri_row = _onehot(lower if backward else upper)
    ghi, gmid, glo = _split3(g)
    b_cols = _dot(tri_col, ghi) + _dot(tri_col, gmid) + _dot(tri_col, glo)
    thi, tmid, tlo = _split3(gt)
    b_rows = _dot(thi, tri_row) + _dot(tmid, tri_row) + _dot(tlo, tri_row)
    return b_cols, b_rows


def _mlstm_heads(items):
    scale = M_DK ** -0.5
    log_scale = math.log(scale)
    L = items[0][0].shape[0]
    row = lax.broadcasted_iota(jnp.int32, (L, L), 0)
    col = lax.broadcasted_iota(jnp.int32, (L, L), 1)
    masks = {False: col <= row, True: col >= row}
    n = range(len(items))
    qh, kh, vh, li_row, lf_row, b_row, b_col, backward, state = zip(*items)
    has_state = [st is not None for st in state]
    m_prev = [st[2] if st is not None else 0.0 for st in state]

    qk = [_dot_nt(qh[i], kh[i]) for i in n]
    qc = [_dot(qh[i], state[i][0].astype(BF16)) if has_state[i] else None for i in n]
    qn = [_dot_nt(qh[i], state[i][1].astype(BF16))[:, 0:1] if has_state[i] else None for i in n]
    total = [jnp.sum(lf_row[i], axis=1, keepdims=True) for i in n]
    d = [jnp.where(masks[backward[i]], b_col[i] + (li_row[i] - b_row[i] + log_scale), NEG) for i in n]
    inter = [b_col[i] + m_prev[i] if has_state[i] else b_col[i] for i in n]
    m_t = [jnp.maximum(inter[i], jnp.max(d[i], axis=1, keepdims=True) - log_scale) for i in n]
    p = [jnp.exp(d[i] - m_t[i]) for i in n]
    s = [qk[i] * p[i] for i in n]
    den = [jnp.sum(s[i], axis=1, keepdims=True) for i in n]
    num = [_dot(s[i].astype(BF16), vh[i]) for i in n]
    a = [jnp.exp(inter[i] - m_t[i]) * scale if has_state[i] else None for i in n]
    num = [a[i] * qc[i] + num[i] if has_state[i] else num[i] for i in n]
    den = [a[i] * qn[i] + den[i] if has_state[i] else den[i] for i in n]
    hh = [num[i] / jnp.maximum(jnp.abs(den[i]), jnp.exp(-m_t[i])) for i in n]

    g_row = [total[i] - b_row[i] + li_row[i] for i in n]
    m_new = [jnp.maximum(total[i] + m_prev[i], jnp.max(g_row[i], axis=1, keepdims=True)) for i in n]
    ws_row = [jnp.exp(g_row[i] - m_new[i]) for i in n]
    kt = [kh[i].astype(F32).T for i in n]
    c_new = [_dot((kt[i] * ws_row[i]).astype(BF16), vh[i]) for i in n]
    n_new = [_dot(jnp.broadcast_to(ws_row[i], (8, L)).astype(BF16), kh[i]) for i in n]
    a0 = [jnp.exp(total[i] + m_prev[i] - m_new[i]) if has_state[i] else None for i in n]
    c_new = [a0[i] * state[i][0] + c_new[i] if has_state[i] else c_new[i] for i in n]
    n_new = [a0[i] * state[i][1] + n_new[i] if has_state[i] else n_new[i] for i in n]
    return hh, c_new, n_new, m_new


def _head_norm_gate(ht, norm_g, ogate):
    mu = jnp.mean(ht, axis=-1, keepdims=True)
    hc = ht - mu
    var = jnp.mean(hc * hc, axis=-1, keepdims=True)
    return (hc * lax.rsqrt(var + LN_EPS) * norm_g * jax.nn.sigmoid(ogate)).astype(BF16)


def _head_operands(q_ref, k_ref, v_ref, h):
    return (q_ref[:, h * M_DK:(h + 1) * M_DK], k_ref[:, h * M_DK:(h + 1) * M_DK], v_ref[:, h * M_DV:(h + 1) * M_DV])


def _gate_operands(gt, b_rows, b_cols, direction, h):
    ci = direction * 8 + h
    cf = direction * 8 + 4 + h
    return gt[ci:ci + 1, :], gt[cf:cf + 1, :], b_rows[cf:cf + 1, :], b_cols[:, cf:cf + 1]


def _mlstm_prompt_kernel(q_ref, k_ref, v_ref, g_ref, gt_ref, o_ref, ng_ref, a_ref, cf_ref, nf_ref, mf_ref):
    g, gt = g_ref[...], gt_ref[...]
    sums = [_mlstm_gate_sums(g, gt, direction == 1) for direction in (0, 1)]
    keys = [(direction, h) for direction in (0, 1) for h in range(M_HEADS)]
    items = [_head_operands(q_ref, k_ref, v_ref, h)
             + _gate_operands(gt, sums[direction][1], sums[direction][0], direction, h) + (direction == 1, None)
             for direction, h in keys]
    hh, c_new, n_new, m_new = _mlstm_heads(items)
    for i, (direction, h) in enumerate(keys):
        cf_ref[0, 0, direction, h] = c_new[i]
        nf_ref[0, 0, direction, h] = n_new[i]
        mf_ref[0, 0, direction, h] = jnp.broadcast_to(m_new[i], mf_ref.shape[-2:])
    for h in range(M_HEADS):
        sl = slice(h * M_DV, (h + 1) * M_DV)
        a_ref[:, sl] = _head_norm_gate(hh[h] + hh[M_HEADS + h], ng_ref[:, sl], o_ref[:, sl])


def _side_cast_specs(arrays, steps, step_of):
    flat = [a.reshape(-1, a.shape[-1]) for a in arrays]
    assert all(a.shape[0] % (steps * 16) == 0 for a in flat)
    specs = [pl.BlockSpec((a.shape[0] // steps, a.shape[1]), lambda *idx: (step_of(*idx), 0)) for a in flat]
    return flat, specs, [jax.ShapeDtypeStruct(a.shape, BF16) for a in flat]


def _run_side_casts(srcs, dsts):
    for src, dst in zip(srcs, dsts):
        dst[...] = src[...].astype(dst.dtype)


def _mlstm_sample_kernel(*refs, direction, n_units, n_cast):
    backward = direction == 1
    refs = list(refs)
    unit_refs = [tuple(refs.pop(0) for _ in range(5)) for _ in range(n_units)]
    c0_ref, n0_ref, m0_ref = refs.pop(0), refs.pop(0), refs.pop(0)
    if backward:
        hprev_ref = refs.pop(0)
        o_refs = [refs.pop(0) for _ in range(n_units)]
        ng_ref = refs.pop(0)
    cast_in = [refs.pop(0) for _ in range(n_cast)]
    out_ref = refs.pop(0)
    cast_out = [refs.pop(0) for _ in range(n_cast)]
    c_scr, n_scr, m_scr = refs
    _run_side_casts(cast_in, cast_out)

    @pl.when(pl.program_id(0) == 0)
    def _():
        c_scr[...] = c0_ref[...]
        n_scr[...] = n0_ref[...]
        m_scr[...] = m0_ref[...]

    keys, items = [], []
    for u, (q_ref, k_ref, v_ref, g_ref, gt_ref) in enumerate(unit_refs):
        gt = gt_ref[...]
        b_cols, b_rows = _mlstm_gate_sums(g_ref[...], gt, backward)
        for h in range(M_HEADS):
            state = (c_scr[u, h], n_scr[u, h], m_scr[u, h][0:1, 0:1])
            keys.append((u, h))
            items.append(_head_operands(q_ref, k_ref, v_ref, h) + _gate_operands(gt, b_rows, b_cols, direction, h)
                         + (backward, state))
    hh, c_new, n_new, m_new = _mlstm_heads(items)
    for i, (u, h) in enumerate(keys):
        c_scr[u, h] = c_new[i]
        n_scr[u, h] = n_new[i]
        m_scr[u, h] = jnp.broadcast_to(m_new[i], m_scr.shape[-2:])
        sl = slice(h * M_DV, (h + 1) * M_DV)
        if backward:
            out_ref[u, :, sl] = _head_norm_gate(hprev_ref[u, :, sl] + hh[i], ng_ref[:, sl], o_refs[u][:, sl])
        else:
            out_ref[u, :, sl] = hh[i]


def _mlstm_chunk_specs(block_of, L):
    return [
        pl.BlockSpec((L, M_HK), lambda s: (block_of(s), 0)),
        pl.BlockSpec((L, M_HK), lambda s: (block_of(s), 1)),
        pl.BlockSpec((L, D_MODEL), lambda s: (block_of(s), 1)),
        pl.BlockSpec((L, 16), lambda s: (block_of(s), 0)),
        pl.BlockSpec((16, L), lambda s: (0, block_of(s))),
    ]


def mlstm_prompt(qkv, g, gt, ogate, norm_g, n_prompt):
    L = MLSTM_CHUNK
    state_blk = lambda *tail: pl.BlockSpec((1, 1, 2, M_HEADS) + tail, lambda s: (s, 0, 0, 0, 0, 0))
    return pl.pallas_call(
        _mlstm_prompt_kernel,
        grid=(n_prompt,),
        in_specs=_mlstm_chunk_specs(lambda s: s, L) + [
            pl.BlockSpec((L, D_MODEL), lambda s: (s, 0)),
            pl.BlockSpec((1, D_MODEL), lambda s: (0, 0)),
        ],
        out_specs=[pl.BlockSpec((L, D_MODEL), lambda s: (s, 0)),
                   state_blk(M_DK, M_DV), state_blk(8, M_DK), state_blk(8, 128)],
        out_shape=[
            jax.ShapeDtypeStruct((n_prompt * L, D_MODEL), BF16),
            jax.ShapeDtypeStruct((n_prompt, 1, 2, M_HEADS, M_DK, M_DV), F32),
            jax.ShapeDtypeStruct((n_prompt, 1, 2, M_HEADS, 8, M_DK), F32),
            jax.ShapeDtypeStruct((n_prompt, 1, 2, M_HEADS, 8, 128), F32),
        ],
        compiler_params=_cparams("arbitrary"),
        name="mlstm_prompt",
    )(qkv, qkv, qkv, g, gt, ogate, norm_g.reshape(1, D_MODEL))


def mlstm_sample(direction, qkv, g, gt, c0, n0, m0, first_block, chunks_per_sample,
                 hprev=None, ogate=None, norm_g=None, side_casts=()):
    L = MLSTM_SAMPLE_CHUNK
    backward = direction == 1
    cps = chunks_per_sample
    n_units = c0.shape[0]
    pos = (lambda s: cps - 1 - s) if backward else (lambda s: s)
    unit_block = [functools.partial(lambda s, u: first_block + u * cps + pos(s), u=u) for u in range(n_units)]
    whole = lambda a: pl.BlockSpec(a.shape, lambda s: (0,) * a.ndim)
    in_specs, args = [], []
    for u in range(n_units):
        in_specs += _mlstm_chunk_specs(unit_block[u], L)
        args += [qkv, qkv, qkv, g, gt]
    in_specs += [whole(c0), whole(n0), whole(m0)]
    args += [c0, n0, m0]
    if backward:
        in_specs += [pl.BlockSpec((n_units, L, D_MODEL), lambda s: (0, pos(s), 0))]
        in_specs += [pl.BlockSpec((L, D_MODEL), functools.partial(lambda s, u: (unit_block[u](s), 0), u=u))
                     for u in range(n_units)]
        in_specs += [pl.BlockSpec((1, D_MODEL), lambda s: (0, 0))]
        args += [hprev] + [ogate] * n_units + [norm_g.reshape(1, D_MODEL)]
    flat, cast_specs, cast_shapes = _side_cast_specs(side_casts, cps, lambda s: s)
    outs = pl.pallas_call(
        functools.partial(_mlstm_sample_kernel, direction=direction, n_units=n_units, n_cast=len(flat)),
        grid=(cps,),
        in_specs=in_specs + cast_specs,
        out_specs=[pl.BlockSpec((n_units, L, D_MODEL), lambda s: (0, pos(s), 0))] + cast_specs,
        out_shape=[jax.ShapeDtypeStruct((n_units, cps * L, D_MODEL), BF16 if backward else F32)] + cast_shapes,
        scratch_shapes=[pltpu.VMEM(c0.shape, F32), pltpu.VMEM(n0.shape, F32), pltpu.VMEM(m0.shape, F32)],
        compiler_params=_cparams("arbitrary"),
        name="mlstm_sample_bwd" if backward else "mlstm_sample_fwd",
    )(*args, *flat)
    return [outs[0]] + [o.reshape(a.shape) for o, a in zip(outs[1:], side_casts)]


def _outproj_router_kernel(ap_ref, as_ref, x_ref, mod_ref, w_ref, lng_ref, lnb_ref, wr_ref,
                           xo_ref, up_ref, comb_ref, pos_ref, total_ref, carry_scr, *, n_prompt_tiles):
    i = pl.program_id(0)

    @pl.when(i == 0)
    def _():
        carry_scr[...] = jnp.zeros_like(carry_scr)

    a = jnp.where(i < n_prompt_tiles, ap_ref[...], as_ref[...])
    z = DEEPNORM_ALPHA * x_ref[...] + mod_ref[0, 2:3, :] * _dot(a, w_ref[...])
    xn = _layer_norm(z, lng_ref[...], lnb_ref[...])
    xo_ref[...] = xn
    u = xn * (1.0 + mod_ref[0, 4:5, :]) + mod_ref[0, 3:4, :]
    up_ref[...] = _pack_bf16_pairs(u)
    comb, pos, carry_new = _route_tokens(u.astype(BF16), wr_ref[...], carry_scr[:, 0:1])
    comb_ref[...] = comb
    pos_ref[...] = pos
    carry_scr[...] = jnp.broadcast_to(carry_new, carry_scr.shape)
    total_ref[...] = jnp.broadcast_to(carry_new, carry_scr.shape)


def mixer_outproj_router(a_p, a_s, x, mod, w, ln_g, ln_b, w_router, sample_len):
    t, d = x.shape
    ne = w_router.shape[1]
    tt = STREAM_TOKEN_TILE
    npt = a_p.shape[0] // tt
    seg = functools.partial(_seg_of_tile, tile=tt, n_prompt_tok=a_p.shape[0], sample_len=sample_len)
    return pl.pallas_call(
        functools.partial(_outproj_router_kernel, n_prompt_tiles=npt),
        grid=(t // tt,),
        in_specs=_split_token_specs(tt, d, npt) + [
            pl.BlockSpec((tt, d), lambda i: (i, 0)),
            pl.BlockSpec((1, 6, d), lambda i: (seg(i), 0, 0)),
            pl.BlockSpec((d, d), lambda i: (0, 0)),
            pl.BlockSpec((1, d), lambda i: (0, 0)),
            pl.BlockSpec((1, d), lambda i: (0, 0)),
            pl.BlockSpec((ne, d), lambda i: (0, 0)),
        ],
        out_specs=[
            pl.BlockSpec((tt, d), lambda i: (i, 0)),
            pl.BlockSpec((tt, d // 2), lambda i: (i, 0)),
            pl.BlockSpec((ne, tt), lambda i: (0, i)),
            pl.BlockSpec((ne, tt), lambda i: (0, i)),
            pl.BlockSpec((ne, 128), lambda i: (0, 0)),
        ],
        out_shape=[
            jax.ShapeDtypeStruct((t, d), F32),
            jax.ShapeDtypeStruct((t, d // 2), jnp.int32),
            jax.ShapeDtypeStruct((ne, t), F32),
            jax.ShapeDtypeStruct((ne, t), F32),
            jax.ShapeDtypeStruct((ne, 128), F32),
        ],
        scratch_shapes=[pltpu.VMEM((ne, 128), F32)],
        compiler_params=_cparams("arbitrary"),
        name="mixer_outproj_router",
    )(a_p, a_s, x, mod, w, ln_g.reshape(1, d), ln_b.reshape(1, d), w_router.T)


def _swiglu_tile(x, wi, wo, dff):
    sub = FFN_SUB
    proj = lambda j: (_dot(x, wi(slice(j * sub, (j + 1) * sub))), _dot(x, wi(slice(dff + j * sub, dff + (j + 1) * sub))))
    acts, cur = [], proj(0)
    for j in range(dff // sub):
        nxt = proj(j + 1) if (j + 1) * sub < dff else None
        gp, up = cur
        acts.append((gp * jax.nn.sigmoid(gp) * up).astype(BF16))
        cur = nxt
    return _dot(jnp.concatenate(acts, axis=1), wo(slice(0, dff)))


def _mixer_ffn_kernel(ap_ref, as_ref, xp_ref, xs_ref, mod_ref, modn_ref, wmix_ref, wi_ref, wo_ref, ln_ref,
                      xo_ref, uo_ref, *, n_prompt_tiles):
    is_prompt = pl.program_id(0) < n_prompt_tiles
    a = jnp.where(is_prompt, ap_ref[...], as_ref[...])
    x = jnp.where(is_prompt, xp_ref[...], xs_ref[...])
    z = DEEPNORM_ALPHA * x + mod_ref[0, 2:3, :] * _dot(a, wmix_ref[...])
    x1 = _layer_norm(z, ln_ref[0:1, :], ln_ref[1:2, :])
    u1 = (x1 * (1.0 + mod_ref[0, 4:5, :]) + mod_ref[0, 3:4, :]).astype(BF16)
    f = _swiglu_tile(u1, lambda c: wi_ref[:, c], lambda r: wo_ref[r, :], wo_ref.shape[0])
    x2 = _layer_norm(DEEPNORM_ALPHA * x1 + mod_ref[0, 5:6, :] * f, ln_ref[2:3, :], ln_ref[3:4, :])
    xo_ref[...] = x2
    uo_ref[...] = (x2 * (1.0 + modn_ref[0, 1:2, :]) + modn_ref[0, 0:1, :]).astype(BF16)


def _resident(shape):
    return pl.BlockSpec(shape, lambda i: (0,) * len(shape), pipeline_mode=pl.Buffered(1))


def mixer_ffn(a_p, a_s, xp, xs, mod, mod_next, w_mix, w_in, w_out, ln_g, ln_b, sample_len):
    n_prompt_tok, d = xp.shape
    t = n_prompt_tok + xs.shape[0]
    tm = FFN_TOKEN_TILE
    npt = n_prompt_tok // tm
    seg = functools.partial(_seg_of_tile, tile=tm, n_prompt_tok=n_prompt_tok, sample_len=sample_len)
    ln = jnp.stack([ln_g[0], ln_b[0], ln_g[1], ln_b[1]])
    return pl.pallas_call(
        functools.partial(_mixer_ffn_kernel, n_prompt_tiles=npt),
        grid=(t // tm,),
        in_specs=_split_token_specs(tm, d, npt) + _split_token_specs(tm, d, npt) + [
            pl.BlockSpec((1, 6, d), lambda i: (seg(i), 0, 0)),
            pl.BlockSpec((1, 6, d), lambda i: (seg(i), 0, 0)),
            _resident(w_mix.shape), _resident(w_in.shape), _resident(w_out.shape), _resident(ln.shape),
        ],
        out_specs=[pl.BlockSpec((tm, d), lambda i: (i, 0)), pl.BlockSpec((tm, d), lambda i: (i, 0))],
        out_shape=[jax.ShapeDtypeStruct((t, d), F32), jax.ShapeDtypeStruct((t, d), BF16)],
        compiler_params=_cparams("arbitrary"),
        name="mixer_ffn",
    )(a_p, a_s, xp, xs, mod, mod_next, w_mix, w_in, w_out, ln)


def _modulation_tables(c, c_ctx, ada_w, ada_b):
    n_s = c.shape[0]
    cvec = jnp.concatenate([c_ctx[None, :], c, jnp.zeros((8 - 1 - n_s, c.shape[1]), F32)], axis=0)
    mod = adaln(cvec, ada_w, ada_b)
    return mod[:, :1 + n_s, :].reshape(ada_w.shape[0], 1 + n_s, 6, c.shape[1])


def _layer0(xp, xs, mod0, mod1, state_c, state_n, state_m, ln_g, ln_b, w_in, b_gates, norm_g, w_out,
            ffn_w_in, ffn_w_out, n_prompt, prompt_len, n_sample, sample_len, later_weights):
    n_prompt_tok = n_prompt * prompt_len
    n_main = 2 * M_HK + 2 * D_MODEL
    w_t = w_in.T
    qkv, ogate, g, gt = mlstm_inproj(xp, xs, mod0, w_t[:n_main].astype(BF16), w_t[n_main:].T.astype(BF16),
                                     w_t[n_main:].astype(BF16), b_gates.reshape(-1), sample_len)
    npc = n_prompt_tok // MLSTM_SAMPLE_CHUNK
    cps = sample_len // MLSTM_SAMPLE_CHUNK
    rep = lambda a: jnp.broadcast_to(a[..., None, :], a.shape[:-1] + (8, a.shape[-1]))
    n0 = rep(state_n)
    m0 = jnp.broadcast_to(state_m[..., None, None], state_m.shape + (8, 128))
    a_p, new_c, nf, mf = mlstm_prompt(qkv, g, gt, ogate, norm_g, n_prompt)
    hf, w_out, ffn_w_in, ffn_w_out = mlstm_sample(0, qkv, g, gt, state_c[:, 0], n0[:, 0], m0[:, 0], npc, cps,
                                                  side_casts=(w_out, ffn_w_in, ffn_w_out))
    a_s, *later_weights = mlstm_sample(1, qkv, g, gt, state_c[:, 1], n0[:, 1], m0[:, 1], npc, cps,
                                       hprev=hf, ogate=ogate, norm_g=norm_g, side_casts=later_weights)
    x2, u2 = mixer_ffn(a_p, a_s.reshape(-1, D_MODEL), xp, xs, mod0, mod1, w_out, ffn_w_in, ffn_w_out, ln_g, ln_b,
                       sample_len)
    return x2, u2, new_c, nf[..., 0, :], mf[..., 0, 0], later_weights


def _qkv_kernel(u_ref, w_ref, qkv_ref, *kv_refs):
    tt, d = u_ref.shape
    u = u_ref[...]

    def head_major(out_ref, p):
        for h in range(NA_HEADS):
            out_ref[pl.ds(h, tt, stride=NA_HEADS), :] = p[:, h * NA_DH:(h + 1) * NA_DH]

    pq = _dot(u, w_ref[:, :d])
    pk = _dot(u, w_ref[:, d:2 * d])
    qkv_ref[:, :d] = (pq * ATTN_Q_SCALE).astype(BF16)
    pv = _dot(u, w_ref[:, 2 * d:])
    qkv_ref[:, d:2 * d] = pk.astype(BF16)
    if kv_refs:
        head_major(kv_refs[0], pk)
    qkv_ref[:, 2 * d:] = pv.astype(BF16)
    if kv_refs:
        head_major(kv_refs[1], pv)


def na_qkv(u, w, first_tok, n_tok, with_kv):
    d = u.shape[1]
    tt = TOKEN_TILE
    first = first_tok // tt
    out_specs = [pl.BlockSpec((tt, 3 * d), lambda i: (i, 0))]
    out_shape = [jax.ShapeDtypeStruct((n_tok, 3 * d), BF16)]
    if with_kv:
        out_specs += [pl.BlockSpec((tt * NA_HEADS, NA_DH), lambda i: (i, 0))] * 2
        out_shape += [jax.ShapeDtypeStruct((n_tok * NA_HEADS, NA_DH), F32)] * 2
    return pl.pallas_call(
        _qkv_kernel,
        grid=(n_tok // tt,),
        in_specs=[pl.BlockSpec((tt, d), lambda i: (first + i, 0)), pl.BlockSpec((d, 3 * d), lambda i: (0, 0))],
        out_specs=out_specs,
        out_shape=out_shape,
        compiler_params=_cparams("arbitrary"),
        name="na_qkv_prompt" if with_kv else "na_qkv_latent",
    )(u, w)


ATTN_LOG2E = math.log2(math.e)
ATTN_Q_SCALE = NA_DH ** -0.5 * ATTN_LOG2E


def _head_pair_masks():
    lane = lax.broadcasted_iota(jnp.int32, (1, 2 * NA_DH), 1)
    return (_onehot(lane < NA_DH), _onehot(lane >= NA_DH))


def _softmax_pv(score_blocks, value_blocks):
    mx = None
    for s in score_blocks:
        bm = jnp.max(s, axis=1, keepdims=True)
        mx = bm if mx is None else jnp.maximum(mx, bm)
    acc, denom = None, None
    for s, v in zip(score_blocks, value_blocks):
        p = jnp.exp2(s - mx)
        ps = jnp.sum(p, axis=1, keepdims=True)
        pv = _dot(p.astype(BF16), v)
        acc = pv if acc is None else acc + pv
        denom = ps if denom is None else denom + ps
    return acc / denom


def _ctx_attn_kernel(q_ref, k_ref, v_ref, o_ref):
    masks = _head_pair_masks()
    for hp in range(NA_HEADS // 2):
        sl = slice(hp * 2 * NA_DH, (hp + 1) * 2 * NA_DH)
        q2, k2, v2 = q_ref[:, sl], k_ref[:, sl], v_ref[:, sl]
        o2 = None
        for msk in masks:
            s = _dot_nt(q2 * msk, k2)
            o = _softmax_pv([s], [v2 * msk])
            o2 = o if o2 is None else o2 + o
        o_ref[:, sl] = o2.astype(o_ref.dtype)


def context_attention(qkv, n_prompt, prompt_len):
    t = n_prompt * prompt_len
    d = D_MODEL
    return pl.pallas_call(
        _ctx_attn_kernel,
        grid=(n_prompt,),
        in_specs=[
            pl.BlockSpec((prompt_len, d), lambda b: (b, 0)),
            pl.BlockSpec((prompt_len, d), lambda b: (b, 1)),
            pl.BlockSpec((prompt_len, d), lambda b: (b, 2)),
        ],
        out_specs=pl.BlockSpec((prompt_len, d), lambda b: (b, 0)),
        out_shape=jax.ShapeDtypeStruct((t, d), BF16),
        compiler_params=_cparams("arbitrary"),
        name="context_attention",
    )(qkv, qkv, qkv)


def _na_kernel(tile_ref, q_ref, kp_ref, kc_ref, kn_ref, vp_ref, vc_ref, vn_ref, kctx_ref, vctx_ref, bias_ref,
               *refs):
    n_cast = (len(refs) - 1) // 2
    cast_in, o_ref, cast_out = refs[:n_cast], refs[n_cast], refs[n_cast + 1:]
    _run_side_casts(cast_in, cast_out)
    j = pl.program_id(1)
    nblk = pl.num_programs(1)
    m = q_ref.shape[0]
    halo = NA_HALO * GRID_W
    n_pairs = (NA_QROWS + 2 * NA_HALO) // 2
    variant = jnp.where(j == 0, 0, jnp.where(j == nblk - 1, 2, 1))
    kinds = [[tile_ref[(variant * NA_QROWS + rq) * n_pairs + kp] for kp in range(n_pairs)]
             for rq in range(NA_QROWS)]

    def bias_of(head):
        return jnp.concatenate(
            [jnp.concatenate([bias_ref[kinds[rq][kp], head] for kp in range(n_pairs)], axis=1)
             for rq in range(NA_QROWS)], axis=0)

    masks = _head_pair_masks()
    for hp in range(NA_HEADS // 2):
        sl = slice(hp * 2 * NA_DH, (hp + 1) * 2 * NA_DH)
        q2 = q_ref[:, sl]
        kloc = jnp.concatenate([kp_ref[m - halo:, sl], kc_ref[:, sl], kn_ref[:halo, sl]], axis=0)
        vloc = jnp.concatenate([vp_ref[m - halo:, sl], vc_ref[:, sl], vn_ref[:halo, sl]], axis=0)
        kctx = kctx_ref[0][:, sl]
        vctx = vctx_ref[0][:, sl]
        o2 = None
        for half, msk in enumerate(masks):
            qm = q2 * msk
            s_loc = _dot_nt(qm, kloc) + bias_of(2 * hp + half)
            s_ctx = _dot_nt(qm, kctx)
            o = _softmax_pv([s_loc, s_ctx], [vloc * msk, vctx * msk])
            o2 = o if o2 is None else o2 + o
        o_ref[:, sl] = o2.astype(o_ref.dtype)


def _na_bias_kernel(rpb_ref, o_ref, *, kinds, n_heads):
    nc, npos = rpb_ref.shape[1], o_ref.shape[2]
    pair_shift = (2 * GRID_W).bit_length() - 1
    c = lax.broadcasted_iota(jnp.int32, (nc, npos), 0)
    pos = lax.broadcasted_iota(jnp.int32, (nc, npos), 1)
    qc = pos >> pair_shift
    kc = pos & (GRID_W - 1)
    c_start = jnp.clip(qc - NA_COLS // 2, 0, GRID_W - NA_COLS)
    col_in = jnp.logical_and(kc >= c_start, kc < c_start + NA_COLS)
    dc = jnp.clip(kc - qc + NA_COLS - 1, 0, 2 * NA_COLS - 2)
    hit = jnp.logical_and(c == dc, col_in)
    right = (pos & GRID_W) != 0
    r1, r2, r3 = _split3(rpb_ref[...] * ATTN_LOG2E)

    def placed(side):
        sel = _onehot(jnp.logical_and(hit, right == side))
        return _dot(r1, sel) + _dot(r2, sel) + _dot(r3, sel)

    left, rght = placed(False), placed(True)
    rows = lambda v, dr: v[dr * n_heads:(dr + 1) * n_heads]
    for k, (dl, dr) in enumerate(kinds):
        shown = col_in[0:1, :]
        if dl is None or dr is None:
            side_ok = jnp.zeros_like(shown) if dl is None and dr is None else (right[0:1, :] == (dl is None))
            shown = jnp.logical_and(shown, side_ok)
        parts = ([rows(left, dl)] if dl is not None else []) + ([rows(rght, dr)] if dr is not None else [])
        val = sum(parts) if parts else jnp.zeros((n_heads, npos), F32)
        o_ref[k] = jnp.where(shown, val, NEG)


def _na_bias_plan():
    ndr = 2 * NA_ROWS - 1
    assert NA_QROWS >= NA_ROWS // 2 and NA_HALO == NA_ROWS // 2 and (NA_QROWS + 2 * NA_HALO) % 2 == 0
    ok = lambda dr: dr if dr is not None and 0 <= dr < ndr else None
    kinds = [(ok(dr), ok(dr + 1)) for dr in range(-1, ndr)]
    first_key_row = (lambda rq: NA_HALO, lambda rq: rq, lambda rq: NA_QROWS + NA_HALO - NA_ROWS)
    table = []
    for first in first_key_row:
        for rq in range(NA_QROWS):
            for kp in range((NA_QROWS + 2 * NA_HALO) // 2):
                drs = []
                for kk in (2 * kp, 2 * kp + 1):
                    visible = first(rq) <= kk < first(rq) + NA_ROWS
                    drs.append(kk - rq + NA_ROWS - 1 - NA_HALO if visible else None)
                kind = (drs[0], drs[1])
                if kind not in kinds:
                    kinds.append(kind)
                table.append(kinds.index(kind))
    return kinds, table


def _na_bias_tiles(rpb):
    nh, ndr, ndc = rpb.shape
    nc = 32
    kinds, table = _na_bias_plan()
    nk = len(kinds)
    rows = jnp.pad(jnp.transpose(rpb, (1, 0, 2)), ((0, 0), (0, 0), (0, nc - ndc))).reshape(ndr * nh, nc)
    npos = GRID_W * 2 * GRID_W
    tiles = pl.pallas_call(
        functools.partial(_na_bias_kernel, kinds=tuple(kinds), n_heads=nh),
        out_shape=jax.ShapeDtypeStruct((nk, nh, npos), F32),
        compiler_params=pltpu.CompilerParams(vmem_limit_bytes=VMEM_LIMIT_BYTES),
        name="na_bias",
    )(rows)
    return tiles.reshape(nk, nh, GRID_W, 2 * GRID_W), jnp.asarray(table, jnp.int32)


def neighbourhood_attention(qkv, k_ctx, v_ctx, rpb, n_sample, sample_len, side_casts=()):
    d = D_MODEL
    m = NA_QROWS * GRID_W
    nblk = sample_len // m
    assert nblk >= 3
    bias, table = _na_bias_tiles(rpb)
    steps = n_sample * nblk

    def blk(col, off):
        return pl.BlockSpec((m, d), lambda b, j, tbl: (b * nblk + jnp.clip(j + off, 0, nblk - 1), col))

    flat, slab_specs, slab_shapes = _side_cast_specs(side_casts, steps, lambda b, j, tbl: b * nblk + j)
    outs = pl.pallas_call(
        _na_kernel,
        grid_spec=pltpu.PrefetchScalarGridSpec(
            num_scalar_prefetch=1,
            grid=(n_sample, nblk),
            in_specs=[
                blk(0, 0), blk(1, -1), blk(1, 0), blk(1, 1), blk(2, -1), blk(2, 0), blk(2, 1),
                pl.BlockSpec((1,) + k_ctx.shape[1:], lambda b, j, tbl: (b, 0, 0), pipeline_mode=pl.Buffered(1)),
                pl.BlockSpec((1,) + v_ctx.shape[1:], lambda b, j, tbl: (b, 0, 0), pipeline_mode=pl.Buffered(1)),
                pl.BlockSpec(bias.shape, lambda b, j, tbl: (0, 0, 0, 0), pipeline_mode=pl.Buffered(1)),
            ] + slab_specs,
            out_specs=[pl.BlockSpec((m, d), lambda b, j, tbl: (b * nblk + j, 0))] + slab_specs,
        ),
        out_shape=[jax.ShapeDtypeStruct((n_sample * sample_len, d), BF16)] + slab_shapes,
        compiler_params=_cparams("arbitrary", "arbitrary"),
        name="neighbourhood_attention",
    )(table, qkv, qkv, qkv, qkv, qkv, qkv, qkv, k_ctx, v_ctx, bias, *flat)
    return [outs[0]] + [o.reshape(a.shape) for o, a in zip(outs[1:], side_casts)]


MOE_ROW_TILE = 512
SC_CORES = 2
SC_SUBCORES = 16
SC_GATHER_CHUNK = 128

def _route_tokens(u, wt, carry):
    w1, w2, w3 = _split3(wt)
    lg = _dot_nt(w1, u) + _dot_nt(w2, u) + _dot_nt(w3, u)
    ne, nt = lg.shape
    e = lax.broadcasted_iota(jnp.int32, lg.shape, 0)
    m1 = jnp.max(lg, axis=0, keepdims=True)
    i1 = jnp.min(jnp.where(lg == m1, e, ne), axis=0, keepdims=True)
    sel1 = e == i1
    lg2 = jnp.where(sel1, -jnp.inf, lg)
    m2 = jnp.max(lg2, axis=0, keepdims=True)
    i2 = jnp.min(jnp.where(lg2 == m2, e, ne), axis=0, keepdims=True)
    sel2 = e == i2
    ex = jnp.exp(m2 - m1)
    wa = 1.0 / (1.0 + ex)
    wb = ex / (1.0 + ex)
    comb = jnp.where(sel1, wa, jnp.where(sel2, wb, 0.0))
    assign = jnp.logical_or(sel1, sel2)
    a = jnp.where(assign, 1.0, 0.0)
    row = lax.broadcasted_iota(jnp.int32, (nt, nt), 0)
    col = lax.broadcasted_iota(jnp.int32, (nt, nt), 1)
    tri = _onehot(row < col)
    rank = _dot(a.astype(BF16), tri) + carry
    pos = jnp.where(assign, rank, -1.0)
    return comb, pos, carry + jnp.sum(a, axis=1, keepdims=True)


def _moe_plan(totals, pos, comb, n_tok):
    ne = totals.shape[0]
    tm = MOE_ROW_TILE
    n_tiles = (2 * n_tok) // tm + ne
    tiles_per = (totals + tm - 1) // tm
    tile_end = jnp.cumsum(tiles_per)
    n_used = tile_end[-1]
    base_rows = (tile_end - tiles_per) * tm
    ti = jnp.arange(n_tiles, dtype=jnp.int32)
    tile_expert = jnp.minimum(jnp.sum((ti[:, None] >= tile_end[None, :]).astype(jnp.int32), axis=1), ne - 1)
    last_expert = tile_expert[jnp.maximum(n_used - 1, 0)]
    tile_expert = jnp.where(ti < n_used, tile_expert, last_expert)
    routed = pos >= 0.0
    row = base_rows[:, None] + pos.astype(jnp.int32)
    row_a = jnp.min(jnp.where(routed, row, n_tiles * tm), axis=0)
    row_b = jnp.max(jnp.where(routed, row, -1), axis=0)
    w_a = jnp.sum(jnp.where(jnp.logical_and(routed, row == row_a[None, :]), comb, 0.0), axis=0)
    w_b = jnp.sum(jnp.where(jnp.logical_and(routed, row == row_b[None, :]), comb, 0.0), axis=0)
    tile_rows = jnp.clip(totals[tile_expert] - (ti - (tile_end - tiles_per)[tile_expert]) * tm, 0, tm)
    tile_rows = jnp.where(ti < n_used, tile_rows, 0)
    return dict(n_rows=n_tiles * tm, tile_expert=tile_expert.astype(jnp.int32),
                n_used=n_used.reshape(1).astype(jnp.int32), tile_rows=tile_rows.astype(jnp.int32),
                token_rows=jnp.concatenate([row_a, row_b]).astype(jnp.int32),
                token_weights=jnp.concatenate([w_a[None], w_b[None], jnp.zeros((6, n_tok), F32)], axis=0))


def _pack_bf16_pairs(x):
    half = x.shape[1] // 2
    bits = pltpu.bitcast(x.astype(BF16).astype(F32), jnp.int32)
    return (bits[:, :half] & jnp.int32(-65536)) | lax.shift_right_logical(bits[:, half:], jnp.int32(16))


def _unpack_bf16_pairs(p):
    hi = pltpu.bitcast(p & jnp.int32(-65536), F32)
    lo = pltpu.bitcast(lax.shift_left(p, jnp.int32(16)), F32)
    return jnp.concatenate([hi, lo], axis=1)


def sc_gather_rows(table, idx):
    _, w = table.shape
    b = idx.shape[0]
    workers = SC_CORES * SC_SUBCORES
    ch = SC_GATHER_CHUNK
    n_chunks = b // (workers * ch)
    assert n_chunks * workers * ch == b
    mesh = plsc.VectorSubcoreMesh(core_axis_name="c", subcore_axis_name="s", num_cores=SC_CORES,
                                  num_subcores=SC_SUBCORES)

    @functools.partial(
        pl.kernel, mesh=mesh, out_type=jax.ShapeDtypeStruct((b, w), table.dtype),
        scratch_types=[pltpu.VMEM((ch,), jnp.int32), pltpu.VMEM((ch, w), table.dtype), pltpu.SemaphoreType.DMA])
    def gather(table_hbm, idx_hbm, out_hbm, idx_v, rows_v, sem):
        worker = lax.axis_index("s") * SC_CORES + lax.axis_index("c")

        @pl.loop(0, n_chunks)
        def _(c):
            base = (worker * n_chunks + c) * ch
            pltpu.sync_copy(idx_hbm.at[pl.ds(base, ch)], idx_v)
            pltpu.async_copy(table_hbm.at[idx_v], rows_v, sem).wait()
            pltpu.sync_copy(rows_v, out_hbm.at[pl.ds(base, ch)])

    return gather(table, idx)


def sc_scatter_rows(rows, idx, n_out):
    t, w = rows.shape
    copies = idx.shape[0] // t
    workers = SC_CORES * SC_SUBCORES
    ch = SC_GATHER_CHUNK
    n_chunks = t // (workers * ch)
    assert n_chunks * workers * ch == t and copies * t == idx.shape[0]
    mesh = plsc.VectorSubcoreMesh(core_axis_name="c", subcore_axis_name="s", num_cores=SC_CORES,
                                  num_subcores=SC_SUBCORES)

    @functools.partial(
        pl.kernel, mesh=mesh, out_type=jax.ShapeDtypeStruct((n_out, w), rows.dtype),
        scratch_types=[pltpu.VMEM((ch,), jnp.int32), pltpu.VMEM((ch, w), rows.dtype), pltpu.SemaphoreType.DMA])
    def scatter(rows_hbm, idx_hbm, out_hbm, idx_v, rows_v, sem):
        worker = lax.axis_index("s") * SC_CORES + lax.axis_index("c")

        @pl.loop(0, n_chunks)
        def _(c):
            base = (worker * n_chunks + c) * ch
            pltpu.sync_copy(rows_hbm.at[pl.ds(base, ch)], rows_v)
            for k in range(copies):
                pltpu.sync_copy(idx_hbm.at[pl.ds(k * t + base, ch)], idx_v)
                pltpu.async_copy(rows_v, out_hbm.at[idx_v], sem).wait()

    return scatter(rows, idx)


def _gffn_kernel(te_ref, nu_ref, nr_ref, x_ref, wi_ref, wo_ref, y_ref):
    i = pl.program_id(0)
    used = i < nu_ref[0]

    @pl.when(used)
    def _():
        packed = x_ref[...]
        row = lax.broadcasted_iota(jnp.int32, packed.shape, 0)
        x = _unpack_bf16_pairs(jnp.where(row < nr_ref[i], packed, 0)).astype(BF16)
        y = _swiglu_tile(x, lambda c: wi_ref[0, :, c], lambda r: wo_ref[0, r, :], wo_ref.shape[1])
        y_ref[...] = _pack_bf16_pairs(y)

    @pl.when(jnp.logical_not(used))
    def _():
        y_ref[...] = jnp.zeros_like(y_ref)


def moe_grouped_ffn(xs, w_in, w_out, plan):
    nr, half = xs.shape
    ne, dff, d = w_out.shape
    tm = MOE_ROW_TILE
    return pl.pallas_call(
        _gffn_kernel,
        grid_spec=pltpu.PrefetchScalarGridSpec(
            num_scalar_prefetch=3,
            grid=(nr // tm,),
            in_specs=[
                pl.BlockSpec((tm, half), lambda i, te, nu, tr: (jnp.minimum(i, nu[0] - 1), 0)),
                pl.BlockSpec((1, d, 2 * dff), lambda i, te, nu, tr: (te[i], 0, 0)),
                pl.BlockSpec((1, dff, d), lambda i, te, nu, tr: (te[i], 0, 0)),
            ],
            out_specs=pl.BlockSpec((tm, half), lambda i, te, nu, tr: (i, 0)),
        ),
        out_shape=jax.ShapeDtypeStruct((nr, half), jnp.int32),
        compiler_params=_cparams("arbitrary"),
        name="moe_grouped_ffn",
    )(plan["tile_expert"], plan["n_used"], plan["tile_rows"], xs, w_in, w_out)


def _combine_kernel(ya_ref, yb_ref, w_ref, x_ref, mod_ref, lng_ref, lnb_ref, op_ref, os_ref, *, n_prompt_tiles):
    i = pl.program_id(0)
    w = w_ref[...].T
    moe = w[:, 0:1] * _unpack_bf16_pairs(ya_ref[...]) + w[:, 1:2] * _unpack_bf16_pairs(yb_ref[...])
    z = DEEPNORM_ALPHA * x_ref[...] + mod_ref[0, 5:6, :] * moe
    xn = _layer_norm(z, lng_ref[...], lnb_ref[...])

    @pl.when(i < n_prompt_tiles)
    def _():
        op_ref[...] = xn

    @pl.when(i >= n_prompt_tiles)
    def _():
        os_ref[...] = xn


def moe_combine(y_tok, weights, x, mod, ln_g, ln_b, n_prompt_tok, sample_len):
    t, d = x.shape
    tt = STREAM_TOKEN_TILE
    nt = t // tt
    npt = n_prompt_tok // tt
    seg = functools.partial(_seg_of_tile, tile=tt, n_prompt_tok=n_prompt_tok, sample_len=sample_len)
    return pl.pallas_call(
        functools.partial(_combine_kernel, n_prompt_tiles=npt),
        grid=(nt,),
        in_specs=[
            pl.BlockSpec((tt, d // 2), lambda i: (i, 0)),
            pl.BlockSpec((tt, d // 2), lambda i: (nt + i, 0)),
            pl.BlockSpec((8, tt), lambda i: (0, i)),
            pl.BlockSpec((tt, d), lambda i: (i, 0)),
            pl.BlockSpec((1, 6, d), lambda i: (seg(i), 0, 0)),
            pl.BlockSpec((1, d), lambda i: (0, 0)),
            pl.BlockSpec((1, d), lambda i: (0, 0)),
        ],
        out_specs=[
            pl.BlockSpec((tt, d), lambda i: (jnp.minimum(i, npt - 1), 0)),
            pl.BlockSpec((tt, d), lambda i: (jnp.maximum(i - npt, 0), 0)),
        ],
        out_shape=[jax.ShapeDtypeStruct((n_prompt_tok, d), F32), jax.ShapeDtypeStruct((t - n_prompt_tok, d), F32)],
        compiler_params=_cparams("arbitrary"),
        name="moe_combine",
    )(y_tok, y_tok, weights, x, mod, ln_g.reshape(1, d), ln_b.reshape(1, d))


def _layer1(x, u, mod1, cache_k, cache_v, ln_g, ln_b, w_qkv, rpb, w_out, w_router, moe_w_in, moe_w_out,
            n_prompt, prompt_len, n_sample, sample_len):
    n_prompt_tok = n_prompt * prompt_len
    d = D_MODEL
    qkv_p, k_p, v_p = na_qkv(u, w_qkv, 0, n_prompt_tok, with_kv=True)
    qkv_s, = na_qkv(u, w_qkv, n_prompt_tok, u.shape[0] - n_prompt_tok, with_kv=False)
    attn_p = context_attention(qkv_p, n_prompt, prompt_len)
    k_ctx = cache_k.reshape(n_sample, -1, d).astype(BF16)
    v_ctx = cache_v.reshape(n_sample, -1, d).astype(BF16)
    attn_s, moe_w_in, moe_w_out = neighbourhood_attention(qkv_s, k_ctx, v_ctx, rpb, n_sample, sample_len,
                                                          side_casts=(moe_w_in, moe_w_out))
    x1, u1_packed, comb, pos, totals = mixer_outproj_router(
        attn_p, attn_s, x, mod1, w_out, ln_g[0], ln_b[0], w_router, sample_len)
    plan = _moe_plan(totals[:, 0].astype(jnp.int32), pos, comb, x.shape[0])
    xs = sc_scatter_rows(u1_packed, plan["token_rows"], plan["n_rows"])
    y = moe_grouped_ffn(xs, moe_w_in, moe_w_out, plan)
    y_tok = sc_gather_rows(y, plan["token_rows"])
    y_p, y_s = moe_combine(y_tok, plan["token_weights"], x1, mod1, ln_g[1], ln_b[1], n_prompt_tok, sample_len)
    return y_p, y_s, k_p, v_p


def kernel(x_prompt, x_sample, state_mlstm_C, state_mlstm_n, state_mlstm_m, cache_na_k, cache_na_v, c, c_ctx,
           ada_w, ada_b, ln_g, ln_b, mlstm_w_in, mlstm_b_gates, mlstm_norm_g, mlstm_w_out, na_w_qkv, na_rpb,
           na_w_out, ffn_w_in, ffn_w_out, moe_w_router, moe_w_in, moe_w_out):
    n_prompt, prompt_len, d = x_prompt.shape
    n_sample, sample_len, _ = x_sample.shape
    assert d == D_MODEL and prompt_len == MLSTM_CHUNK and sample_len % MLSTM_SAMPLE_CHUNK == 0
    assert ada_w.shape[0] == DEPTH == 2
    mod = _modulation_tables(c, c_ctx, ada_w, ada_b)
    x2, u2, new_c, new_n, new_m, (w_qkv, na_w_out) = _layer0(
        x_prompt.reshape(-1, d), x_sample.reshape(-1, d), mod[0], mod[1],
        state_mlstm_C[:, 0], state_mlstm_n[:, 0], state_mlstm_m[:, 0], ln_g[0], ln_b[0], mlstm_w_in[0],
        mlstm_b_gates[0], mlstm_norm_g[0], mlstm_w_out[0], ffn_w_in[0], ffn_w_out[0],
        n_prompt, prompt_len, n_sample, sample_len, later_weights=(na_w_qkv[0], na_w_out[0]))
    y_p, y_s, k_p, v_p = _layer1(
        x2, u2, mod[1], cache_na_k[:, 0], cache_na_v[:, 0], ln_g[1], ln_b[1], w_qkv, na_rpb[0], na_w_out,
        moe_w_router[0], moe_w_in[0], moe_w_out[0], n_prompt, prompt_len, n_sample, sample_len)
    kv_shape = (n_prompt, 1, prompt_len, NA_HEADS, NA_DH)
    return (y_p.reshape(x_prompt.shape), y_s.reshape(x_sample.shape), new_c, new_n, new_m,
            k_p.reshape(kv_shape), v_p.reshape(kv_shape))
```

```python
import functools
import math

import jax
import jax.numpy as jnp
from jax import lax
from jax.experimental import pallas as pl
from jax.experimental.pallas import tpu as pltpu
from jax.experimental.pallas import tpu_sc as plsc

F32 = jnp.float32
BF16 = jnp.bfloat16

D_MODEL = 1024
DEPTH = 2
GRID_W = 64
M_HEADS = 4
M_DK = 128
M_DV = 256
M_HK = M_HEADS * M_DK
NA_HEADS = 16
NA_DH = 64
NA_ROWS = 8
NA_COLS = 16
DEEPNORM_ALPHA = (2.0 * DEPTH) ** 0.25
LN_EPS = 1e-5
NEG = -1e30

VMEM_LIMIT_BYTES = 56 * 1024 * 1024

MLSTM_CHUNK = 256
MLSTM_SAMPLE_CHUNK = 256
TOKEN_TILE = 512
STREAM_TOKEN_TILE = 1024
ROUTE_GROUP = 128
FFN_TOKEN_TILE = 512
FFN_SUB = 256
NA_QROWS = 4
NA_HALO = 4


def _cparams(*sem):
    return pltpu.CompilerParams(dimension_semantics=sem, vmem_limit_bytes=VMEM_LIMIT_BYTES)


def _dot(a, b):
    return jnp.dot(a, b, preferred_element_type=F32)


def _dot_nt(a, b):
    return lax.dot_general(a, b, (((1,), (1,)), ((), ())), preferred_element_type=F32)


def _onehot(mask):
    return jnp.where(mask, 1.0, 0.0).astype(BF16)


def _split3(x):
    hi = x.astype(BF16)
    r = x - hi.astype(F32)
    mid = r.astype(BF16)
    lo = (r - mid.astype(F32)).astype(BF16)
    return hi, mid, lo


def _layer_norm(z, g, b):
    mu = jnp.mean(z, axis=-1, keepdims=True)
    zc = z - mu
    var = jnp.mean(zc * zc, axis=-1, keepdims=True)
    return zc * lax.rsqrt(var + LN_EPS) * g + b


def _log_sigmoid(x):
    return jnp.minimum(x, 0.0) - jnp.log(1.0 + jnp.exp(-jnp.abs(x)))


def _seg_of_tile(i, tile, n_prompt_tok, sample_len):
    tok = i * tile
    return jnp.where(tok < n_prompt_tok, 0, 1 + (tok - n_prompt_tok) // sample_len)


def _adaln_kernel(c_ref, w_ref, b_ref, o_ref):
    c = c_ref[...]
    a = (c * jax.nn.sigmoid(c)).astype(BF16)
    o_ref[0] = _dot(a, w_ref[0].astype(BF16)) + b_ref[0]


def adaln(cvec, ada_w, ada_b):
    depth, d, n = ada_w.shape
    tn = 3072
    return pl.pallas_call(
        _adaln_kernel,
        grid=(depth, n // tn),
        in_specs=[
            pl.BlockSpec((8, d), lambda l, j: (0, 0)),
            pl.BlockSpec((1, d, tn), lambda l, j: (l, 0, j)),
            pl.BlockSpec((1, 1, tn), lambda l, j: (l, 0, j)),
        ],
        out_specs=pl.BlockSpec((1, 8, tn), lambda l, j: (l, 0, j)),
        out_shape=jax.ShapeDtypeStruct((depth, 8, n), F32),
        compiler_params=_cparams("arbitrary", "arbitrary"),
        name="adaln",
    )(cvec, ada_w, ada_b.reshape(depth, 1, n))


def _inproj_kernel(xp_ref, xs_ref, mod_ref, w_ref, wg_ref, bg_ref, qkv_ref, o_ref, g_ref, gt_ref, *, n_prompt_tiles):
    x = jnp.where(pl.program_id(0) < n_prompt_tiles, xp_ref[...], xs_ref[...])
    u = (x * (1.0 + mod_ref[0, 1:2, :]) + mod_ref[0, 0:1, :]).astype(BF16)
    nqkv = qkv_ref.shape[1]
    half = nqkv // 2
    g = _dot(u, wg_ref[...]) + bg_ref[...]
    p0 = _dot_nt(u, w_ref[:half, :])
    col = lax.broadcasted_iota(jnp.int32, g.shape, 1)
    g = jnp.where(((col >> 2) & 1) == 1, _log_sigmoid(g), g)
    g_ref[...] = g
    p1 = _dot_nt(u, w_ref[half:nqkv, :])
    qkv_ref[:, :half] = p0.astype(BF16)
    gt_ref[...] = g.T
    p2 = _dot_nt(u, w_ref[nqkv:, :])
    qkv_ref[:, half:] = p1.astype(BF16)
    o_ref[...] = p2


def _split_token_specs(tile, d, n_prompt_tiles):
    return [pl.BlockSpec((tile, d), lambda i: (jnp.minimum(i, n_prompt_tiles - 1), 0)),
            pl.BlockSpec((tile, d), lambda i: (jnp.maximum(i - n_prompt_tiles, 0), 0))]


def mlstm_inproj(xp, xs, mod, w_main, w_gate, b_gate, sample_len):
    n_prompt_tok, d = xp.shape
    t = n_prompt_tok + xs.shape[0]
    n_main = w_main.shape[0]
    n_qkv = 2 * M_HK + D_MODEL
    ng = w_gate.shape[1]
    tt = TOKEN_TILE
    npt = n_prompt_tok // tt
    seg = functools.partial(_seg_of_tile, tile=tt, n_prompt_tok=n_prompt_tok, sample_len=sample_len)
    return pl.pallas_call(
        functools.partial(_inproj_kernel, n_prompt_tiles=npt),
        grid=(t // tt,),
        in_specs=_split_token_specs(tt, d, npt) + [
            pl.BlockSpec((1, 6, d), lambda i: (seg(i), 0, 0)),
            pl.BlockSpec((n_main, d), lambda i: (0, 0)),
            pl.BlockSpec((d, ng), lambda i: (0, 0)),
            pl.BlockSpec((1, ng), lambda i: (0, 0)),
        ],
        out_specs=[
            pl.BlockSpec((tt, n_qkv), lambda i: (i, 0)),
            pl.BlockSpec((tt, n_main - n_qkv), lambda i: (i, 0)),
            pl.BlockSpec((tt, ng), lambda i: (i, 0)),
            pl.BlockSpec((ng, tt), lambda i: (0, i)),
        ],
        out_shape=[
            jax.ShapeDtypeStruct((t, n_qkv), BF16),
            jax.ShapeDtypeStruct((t, n_main - n_qkv), F32),
            jax.ShapeDtypeStruct((t, ng), F32),
            jax.ShapeDtypeStruct((ng, t), F32),
        ],
        compiler_params=_cparams("arbitrary"),
        name="mlstm_inproj",
    )(xp, xs, mod, w_main, w_gate, b_gate.reshape(1, ng))


def _mlstm_gate_sums(g, gt, backward):
    L = g.shape[0]
    row = lax.broadcasted_iota(jnp.int32, (L, L), 0)
    col = lax.broadcasted_iota(jnp.int32, (L, L), 1)
    lower = col <= row
    upper = col >= row
    tri_col = _onehot(upper if backward else lower)
    tri_row = _onehot(lower if backward else upper)
    ghi, gmid, glo = _split3(g)
    b_cols = _dot(tri_col, ghi) + _dot(tri_col, gmid) + _dot(tri_col, glo)
    thi, tmid, tlo = _split3(gt)
    b_rows = _dot(thi, tri_row) + _dot(tmid, tri_row) + _dot(tlo, tri_row)
    return b_cols, b_rows


def _mlstm_heads(items):
    scale = M_DK ** -0.5
    log_scale = math.log(scale)
    L = items[0][0].shape[0]
    row = lax.broadcasted_iota(jnp.int32, (L, L), 0)
    col = lax.broadcasted_iota(jnp.int32, (L, L), 1)
    masks = {False: col <= row, True: col >= row}
    n = range(len(items))
    qh, kh, vh, li_row, lf_row, b_row, b_col, backward, state = zip(*items)
    has_state = [st is not None for st in state]
    m_prev = [st[2] if st is not None else 0.0 for st in state]

    qk = [_dot_nt(qh[i], kh[i]) for i in n]
    qc = [_dot(qh[i], state[i][0].astype(BF16)) if has_state[i] else None for i in n]
    qn = [_dot_nt(qh[i], state[i][1].astype(BF16))[:, 0:1] if has_state[i] else None for i in n]
    total = [jnp.sum(lf_row[i], axis=1, keepdims=True) for i in n]
    d = [jnp.where(masks[backward[i]], b_col[i] + (li_row[i] - b_row[i] + log_scale), NEG) for i in n]
    inter = [b_col[i] + m_prev[i] if has_state[i] else b_col[i] for i in n]
    m_t = [jnp.maximum(inter[i], jnp.max(d[i], axis=1, keepdims=True) - log_scale) for i in n]
    p = [jnp.exp(d[i] - m_t[i]) for i in n]
    s = [qk[i] * p[i] for i in n]
    den = [jnp.sum(s[i], axis=1, keepdims=True) for i in n]
    num = [_dot(s[i].astype(BF16), vh[i]) for i in n]
    a = [jnp.exp(inter[i] - m_t[i]) * scale if has_state[i] else None for i in n]
    num = [a[i] * qc[i] + num[i] if has_state[i] else num[i] for i in n]
    den = [a[i] * qn[i] + den[i] if has_state[i] else den[i] for i in n]
    hh = [num[i] / jnp.maximum(jnp.abs(den[i]), jnp.exp(-m_t[i])) for i in n]

    g_row = [total[i] - b_row[i] + li_row[i] for i in n]
    m_new = [jnp.maximum(total[i] + m_prev[i], jnp.max(g_row[i], axis=1, keepdims=True)) for i in n]
    ws_row = [jnp.exp(g_row[i] - m_new[i]) for i in n]
    kt = [kh[i].astype(F32).T for i in n]
    c_new = [_dot((kt[i] * ws_row[i]).astype(BF16), vh[i]) for i in n]
    n_new = [_dot(jnp.broadcast_to(ws_row[i], (8, L)).astype(BF16), kh[i]) for i in n]
    a0 = [jnp.exp(total[i] + m_prev[i] - m_new[i]) if has_state[i] else None for i in n]
    c_new = [a0[i] * state[i][0] + c_new[i] if has_state[i] else c_new[i] for i in n]
    n_new = [a0[i] * state[i][1] + n_new[i] if has_state[i] else n_new[i] for i in n]
    return hh, c_new, n_new, m_new


def _head_norm_gate(ht, norm_g, ogate):
    mu = jnp.mean(ht, axis=-1, keepdims=True)
    hc = ht - mu
    var = jnp.mean(hc * hc, axis=-1, keepdims=True)
    return (hc * lax.rsqrt(var + LN_EPS) * norm_g * jax.nn.sigmoid(ogate)).astype(BF16)


def _head_operands(q_ref, k_ref, v_ref, h):
    return (q_ref[:, h * M_DK:(h + 1) * M_DK], k_ref[:, h * M_DK:(h + 1) * M_DK], v_ref[:, h * M_DV:(h + 1) * M_DV])


def _gate_operands(gt, b_rows, b_cols, direction, h):
    ci = direction * 8 + h
    cf = direction * 8 + 4 + h
    return gt[ci:ci + 1, :], gt[cf:cf + 1, :], b_rows[cf:cf + 1, :], b_cols[:, cf:cf + 1]


def _mlstm_prompt_kernel(q_ref, k_ref, v_ref, g_ref, gt_ref, o_ref, ng_ref, a_ref, cf_ref, nf_ref, mf_ref):
    g, gt = g_ref[...], gt_ref[...]
    sums = [_mlstm_gate_sums(g, gt, direction == 1) for direction in (0, 1)]
    keys = [(direction, h) for direction in (0, 1) for h in range(M_HEADS)]
    items = [_head_operands(q_ref, k_ref, v_ref, h)
             + _gate_operands(gt, sums[direction][1], sums[direction][0], direction, h) + (direction == 1, None)
             for direction, h in keys]
    hh, c_new, n_new, m_new = _mlstm_heads(items)
    for i, (direction, h) in enumerate(keys):
        cf_ref[0, 0, direction, h] = c_new[i]
        nf_ref[0, 0, direction, h] = n_new[i]
        mf_ref[0, 0, direction, h] = jnp.broadcast_to(m_new[i], mf_ref.shape[-2:])
    for h in range(M_HEADS):
        sl = slice(h * M_DV, (h + 1) * M_DV)
        a_ref[:, sl] = _head_norm_gate(hh[h] + hh[M_HEADS + h], ng_ref[:, sl], o_ref[:, sl])


def _side_cast_specs(arrays, steps, step_of):
    flat = [a.reshape(-1, a.shape[-1]) for a in arrays]
    assert all(a.shape[0] % (steps * 16) == 0 for a in flat)
    specs = [pl.BlockSpec((a.shape[0] // steps, a.shape[1]), lambda *idx: (step_of(*idx), 0)) for a in flat]
    return flat, specs, [jax.ShapeDtypeStruct(a.shape, BF16) for a in flat]


def _run_side_casts(srcs, dsts):
    for src, dst in zip(srcs, dsts):
        dst[...] = src[...].astype(dst.dtype)


def _mlstm_sample_kernel(*refs, direction, n_units, n_cast):
    backward = direction == 1
    refs = list(refs)
    unit_refs = [tuple(refs.pop(0) for _ in range(5)) for _ in range(n_units)]
    c0_ref, n0_ref, m0_ref = refs.pop(0), refs.pop(0), refs.pop(0)
    if backward:
        hprev_ref = refs.pop(0)
        o_refs = [refs.pop(0) for _ in range(n_units)]
        ng_ref = refs.pop(0)
    cast_in = [refs.pop(0) for _ in range(n_cast)]
    out_ref = refs.pop(0)
    cast_out = [refs.pop(0) for _ in range(n_cast)]
    c_scr, n_scr, m_scr = refs
    _run_side_casts(cast_in, cast_out)

    @pl.when(pl.program_id(0) == 0)
    def _():
        c_scr[...] = c0_ref[...]
        n_scr[...] = n0_ref[...]
        m_scr[...] = m0_ref[...]

    keys, items = [], []
    for u, (q_ref, k_ref, v_ref, g_ref, gt_ref) in enumerate(unit_refs):
        gt = gt_ref[...]
        b_cols, b_rows = _mlstm_gate_sums(g_ref[...], gt, backward)
        for h in range(M_HEADS):
            state = (c_scr[u, h], n_scr[u, h], m_scr[u, h][0:1, 0:1])
            keys.append((u, h))
            items.append(_head_operands(q_ref, k_ref, v_ref, h) + _gate_operands(gt, b_rows, b_cols, direction, h)
                         + (backward, state))
    hh, c_new, n_new, m_new = _mlstm_heads(items)
    for i, (u, h) in enumerate(keys):
        c_scr[u, h] = c_new[i]
        n_scr[u, h] = n_new[i]
        m_scr[u, h] = jnp.broadcast_to(m_new[i], m_scr.shape[-2:])
        sl = slice(h * M_DV, (h + 1) * M_DV)
        if backward:
            out_ref[u, :, sl] = _head_norm_gate(hprev_ref[u, :, sl] + hh[i], ng_ref[:, sl], o_refs[u][:, sl])
        else:
            out_ref[u, :, sl] = hh[i]


def _mlstm_chunk_specs(block_of, L):
    return [
        pl.BlockSpec((L, M_HK), lambda s: (block_of(s), 0)),
        pl.BlockSpec((L, M_HK), lambda s: (block_of(s), 1)),
        pl.BlockSpec((L, D_MODEL), lambda s: (block_of(s), 1)),
        pl.BlockSpec((L, 16), lambda s: (block_of(s), 0)),
        pl.BlockSpec((16, L), lambda s: (0, block_of(s))),
    ]


def mlstm_prompt(qkv, g, gt, ogate, norm_g, n_prompt):
    L = MLSTM_CHUNK
    state_blk = lambda *tail: pl.BlockSpec((1, 1, 2, M_HEADS) + tail, lambda s: (s, 0, 0, 0, 0, 0))
    return pl.pallas_call(
        _mlstm_prompt_kernel,
        grid=(n_prompt,),
        in_specs=_mlstm_chunk_specs(lambda s: s, L) + [
            pl.BlockSpec((L, D_MODEL), lambda s: (s, 0)),
            pl.BlockSpec((1, D_MODEL), lambda s: (0, 0)),
        ],
        out_specs=[pl.BlockSpec((L, D_MODEL), lambda s: (s, 0)),
                   state_blk(M_DK, M_DV), state_blk(8, M_DK), state_blk(8, 128)],
        out_shape=[
            jax.ShapeDtypeStruct((n_prompt * L, D_MODEL), BF16),
            jax.ShapeDtypeStruct((n_prompt, 1, 2, M_HEADS, M_DK, M_DV), F32),
            jax.ShapeDtypeStruct((n_prompt, 1, 2, M_HEADS, 8, M_DK), F32),
            jax.ShapeDtypeStruct((n_prompt, 1, 2, M_HEADS, 8, 128), F32),
        ],
        compiler_params=_cparams("arbitrary"),
        name="mlstm_prompt",
    )(qkv, qkv, qkv, g, gt, ogate, norm_g.reshape(1, D_MODEL))


def mlstm_sample(direction, qkv, g, gt, c0, n0, m0, first_block, chunks_per_sample,
                 hprev=None, ogate=None, norm_g=None, side_casts=()):
    L = MLSTM_SAMPLE_CHUNK
    backward = direction == 1
    cps = chunks_per_sample
    n_units = c0.shape[0]
    pos = (lambda s: cps - 1 - s) if backward else (lambda s: s)
    unit_block = [functools.partial(lambda s, u: first_block + u * cps + pos(s), u=u) for u in range(n_units)]
    whole = lambda a: pl.BlockSpec(a.shape, lambda s: (0,) * a.ndim)
    in_specs, args = [], []
    for u in range(n_units):
        in_specs += _mlstm_chunk_specs(unit_block[u], L)
        args += [qkv, qkv, qkv, g, gt]
    in_specs += [whole(c0), whole(n0), whole(m0)]
    args += [c0, n0, m0]
    if backward:
        in_specs += [pl.BlockSpec((n_units, L, D_MODEL), lambda s: (0, pos(s), 0))]
        in_specs += [pl.BlockSpec((L, D_MODEL), functools.partial(lambda s, u: (unit_block[u](s), 0), u=u))
                     for u in range(n_units)]
        in_specs += [pl.BlockSpec((1, D_MODEL), lambda s: (0, 0))]
        args += [hprev] + [ogate] * n_units + [norm_g.reshape(1, D_MODEL)]
    flat, cast_specs, cast_shapes = _side_cast_specs(side_casts, cps, lambda s: s)
    outs = pl.pallas_call(
        functools.partial(_mlstm_sample_kernel, direction=direction, n_units=n_units, n_cast=len(flat)),
        grid=(cps,),
        in_specs=in_specs + cast_specs,
        out_specs=[pl.BlockSpec((n_units, L, D_MODEL), lambda s: (0, pos(s), 0))] + cast_specs,
        out_shape=[jax.ShapeDtypeStruct((n_units, cps * L, D_MODEL), BF16 if backward else F32)] + cast_shapes,
        scratch_shapes=[pltpu.VMEM(c0.shape, F32), pltpu.VMEM(n0.shape, F32), pltpu.VMEM(m0.shape, F32)],
        compiler_params=_cparams("arbitrary"),
        name="mlstm_sample_bwd" if backward else "mlstm_sample_fwd",
    )(*args, *flat)
    return [outs[0]] + [o.reshape(a.shape) for o, a in zip(outs[1:], side_casts)]


def _outproj_router_kernel(ap_ref, as_ref, x_ref, mod_ref, w_ref, lng_ref, lnb_ref, wr_ref,
                           xo_ref, up_ref, comb_ref, pos_ref, total_ref, carry_scr, *, n_prompt_tiles):
    i = pl.program_id(0)

    @pl.when(i == 0)
    def _():
        carry_scr[...] = jnp.zeros_like(carry_scr)

    a = jnp.where(i < n_prompt_tiles, ap_ref[...], as_ref[...])
    z = DEEPNORM_ALPHA * x_ref[...] + mod_ref[0, 2:3, :] * _dot(a, w_ref[...])
    xn = _layer_norm(z, lng_ref[...], lnb_ref[...])
    xo_ref[...] = xn
    u = xn * (1.0 + mod_ref[0, 4:5, :]) + mod_ref[0, 3:4, :]
    up_ref[...] = _pack_bf16_pairs(u)
    comb, pos, carry_new = _route_tokens(u.astype(BF16), wr_ref[...], carry_scr[:, 0:1])
    comb_ref[...] = comb
    pos_ref[...] = pos
    carry_scr[...] = jnp.broadcast_to(carry_new, carry_scr.shape)
    total_ref[...] = jnp.broadcast_to(carry_new, carry_scr.shape)


def mixer_outproj_router(a_p, a_s, x, mod, w, ln_g, ln_b, w_router, sample_len):
    t, d = x.shape
    ne = w_router.shape[1]
    tt = STREAM_TOKEN_TILE
    npt = a_p.shape[0] // tt
    seg = functools.partial(_seg_of_tile, tile=tt, n_prompt_tok=a_p.shape[0], sample_len=sample_len)
    return pl.pallas_call(
        functools.partial(_outproj_router_kernel, n_prompt_tiles=npt),
        grid=(t // tt,),
        in_specs=_split_token_specs(tt, d, npt) + [
            pl.BlockSpec((tt, d), lambda i: (i, 0)),
            pl.BlockSpec((1, 6, d), lambda i: (seg(i), 0, 0)),
            pl.BlockSpec((d, d), lambda i: (0, 0)),
            pl.BlockSpec((1, d), lambda i: (0, 0)),
            pl.BlockSpec((1, d), lambda i: (0, 0)),
            pl.BlockSpec((ne, d), lambda i: (0, 0)),
        ],
        out_specs=[
            pl.BlockSpec((tt, d), lambda i: (i, 0)),
            pl.BlockSpec((tt, d // 2), lambda i: (i, 0)),
            pl.BlockSpec((ne, tt), lambda i: (0, i)),
            pl.BlockSpec((ne, tt), lambda i: (0, i)),
            pl.BlockSpec((ne, 128), lambda i: (0, 0)),
        ],
        out_shape=[
            jax.ShapeDtypeStruct((t, d), F32),
            jax.ShapeDtypeStruct((t, d // 2), jnp.int32),
            jax.ShapeDtypeStruct((ne, t), F32),
            jax.ShapeDtypeStruct((ne, t), F32),
            jax.ShapeDtypeStruct((ne, 128), F32),
        ],
        scratch_shapes=[pltpu.VMEM((ne, 128), F32)],
        compiler_params=_cparams("arbitrary"),
        name="mixer_outproj_router",
    )(a_p, a_s, x, mod, w, ln_g.reshape(1, d), ln_b.reshape(1, d), w_router.T)


def _swiglu_tile(x, wi, wo, dff):
    sub = FFN_SUB
    proj = lambda j: (_dot(x, wi(slice(j * sub, (j + 1) * sub))), _dot(x, wi(slice(dff + j * sub, dff + (j + 1) * sub))))
    acts, cur = [], proj(0)
    for j in range(dff // sub):
        nxt = proj(j + 1) if (j + 1) * sub < dff else None
        gp, up = cur
        acts.append((gp * jax.nn.sigmoid(gp) * up).astype(BF16))
        cur = nxt
    return _dot(jnp.concatenate(acts, axis=1), wo(slice(0, dff)))


def _mixer_ffn_kernel(ap_ref, as_ref, xp_ref, xs_ref, mod_ref, modn_ref, wmix_ref, wi_ref, wo_ref, ln_ref,
                      xo_ref, uo_ref, *, n_prompt_tiles):
    is_prompt = pl.program_id(0) < n_prompt_tiles
    a = jnp.where(is_prompt, ap_ref[...], as_ref[...])
    x = jnp.where(is_prompt, xp_ref[...], xs_ref[...])
    z = DEEPNORM_ALPHA * x + mod_ref[0, 2:3, :] * _dot(a, wmix_ref[...])
    x1 = _layer_norm(z, ln_ref[0:1, :], ln_ref[1:2, :])
    u1 = (x1 * (1.0 + mod_ref[0, 4:5, :]) + mod_ref[0, 3:4, :]).astype(BF16)
    f = _swiglu_tile(u1, lambda c: wi_ref[:, c], lambda r: wo_ref[r, :], wo_ref.shape[0])
    x2 = _layer_norm(DEEPNORM_ALPHA * x1 + mod_ref[0, 5:6, :] * f, ln_ref[2:3, :], ln_ref[3:4, :])
    xo_ref[...] = x2
    uo_ref[...] = (x2 * (1.0 + modn_ref[0, 1:2, :]) + modn_ref[0, 0:1, :]).astype(BF16)


def _resident(shape):
    return pl.BlockSpec(shape, lambda i: (0,) * len(shape), pipeline_mode=pl.Buffered(1))


def mixer_ffn(a_p, a_s, xp, xs, mod, mod_next, w_mix, w_in, w_out, ln_g, ln_b, sample_len):
    n_prompt_tok, d = xp.shape
    t = n_prompt_tok + xs.shape[0]
    tm = FFN_TOKEN_TILE
    npt = n_prompt_tok // tm
    seg = functools.partial(_seg_of_tile, tile=tm, n_prompt_tok=n_prompt_tok, sample_len=sample_len)
    ln = jnp.stack([ln_g[0], ln_b[0], ln_g[1], ln_b[1]])
    return pl.pallas_call(
        functools.partial(_mixer_ffn_kernel, n_prompt_tiles=npt),
        grid=(t // tm,),
        in_specs=_split_token_specs(tm, d, npt) + _split_token_specs(tm, d, npt) + [
            pl.BlockSpec((1, 6, d), lambda i: (seg(i), 0, 0)),
            pl.BlockSpec((1, 6, d), lambda i: (seg(i), 0, 0)),
            _resident(w_mix.shape), _resident(w_in.shape), _resident(w_out.shape), _resident(ln.shape),
        ],
        out_specs=[pl.BlockSpec((tm, d), lambda i: (i, 0)), pl.BlockSpec((tm, d), lambda i: (i, 0))],
        out_shape=[jax.ShapeDtypeStruct((t, d), F32), jax.ShapeDtypeStruct((t, d), BF16)],
        compiler_params=_cparams("arbitrary"),
        name="mixer_ffn",
    )(a_p, a_s, xp, xs, mod, mod_next, w_mix, w_in, w_out, ln)


def _modulation_tables(c, c_ctx, ada_w, ada_b):
    n_s = c.shape[0]
    cvec = jnp.concatenate([c_ctx[None, :], c, jnp.zeros((8 - 1 - n_s, c.shape[1]), F32)], axis=0)
    mod = adaln(cvec, ada_w, ada_b)
    return mod[:, :1 + n_s, :].reshape(ada_w.shape[0], 1 + n_s, 6, c.shape[1])


def _layer0(xp, xs, mod0, mod1, state_c, state_n, state_m, ln_g, ln_b, w_in, b_gates, norm_g, w_out,
            ffn_w_in, ffn_w_out, n_prompt, prompt_len, n_sample, sample_len, later_weights):
    n_prompt_tok = n_prompt * prompt_len
    n_main = 2 * M_HK + 2 * D_MODEL
    w_t = w_in.T
    qkv, ogate, g, gt = mlstm_inproj(xp, xs, mod0, w_t[:n_main].astype(BF16), w_t[n_main:].T.astype(BF16),
                                     b_gates.reshape(-1), sample_len)
    npc = n_prompt_tok // MLSTM_SAMPLE_CHUNK
    cps = sample_len // MLSTM_SAMPLE_CHUNK
    rep = lambda a: jnp.broadcast_to(a[..., None, :], a.shape[:-1] + (8, a.shape[-1]))
    n0 = rep(state_n)
    m0 = jnp.broadcast_to(state_m[..., None, None], state_m.shape + (8, 128))
    a_p, new_c, nf, mf = mlstm_prompt(qkv, g, gt, ogate, norm_g, n_prompt)
    hf, w_out, ffn_w_in, ffn_w_out = mlstm_sample(0, qkv, g, gt, state_c[:, 0], n0[:, 0], m0[:, 0], npc, cps,
                                                  side_casts=(w_out, ffn_w_in, ffn_w_out))
    a_s, *later_weights = mlstm_sample(1, qkv, g, gt, state_c[:, 1], n0[:, 1], m0[:, 1], npc, cps,
                                       hprev=hf, ogate=ogate, norm_g=norm_g, side_casts=later_weights)
    x2, u2 = mixer_ffn(a_p, a_s.reshape(-1, D_MODEL), xp, xs, mod0, mod1, w_out, ffn_w_in, ffn_w_out, ln_g, ln_b,
                       sample_len)
    return x2, u2, new_c, nf[..., 0, :], mf[..., 0, 0], later_weights


def _qkv_kernel(u_ref, w_ref, qkv_ref, *kv_refs):
    tt, d = u_ref.shape
    u = u_ref[...]

    def head_major(out_ref, p):
        for h in range(NA_HEADS):
            out_ref[pl.ds(h, tt, stride=NA_HEADS), :] = p[:, h * NA_DH:(h + 1) * NA_DH]

    pq = _dot(u, w_ref[:, :d])
    pk = _dot(u, w_ref[:, d:2 * d])
    qkv_ref[:, :d] = (pq * ATTN_Q_SCALE).astype(BF16)
    pv = _dot(u, w_ref[:, 2 * d:])
    qkv_ref[:, d:2 * d] = pk.astype(BF16)
    if kv_refs:
        head_major(kv_refs[0], pk)
    qkv_ref[:, 2 * d:] = pv.astype(BF16)
    if kv_refs:
        head_major(kv_refs[1], pv)


def na_qkv(u, w, first_tok, n_tok, with_kv):
    d = u.shape[1]
    tt = TOKEN_TILE
    first = first_tok // tt
    out_specs = [pl.BlockSpec((tt, 3 * d), lambda i: (i, 0))]
    out_shape = [jax.ShapeDtypeStruct((n_tok, 3 * d), BF16)]
    if with_kv:
        out_specs += [pl.BlockSpec((tt * NA_HEADS, NA_DH), lambda i: (i, 0))] * 2
        out_shape += [jax.ShapeDtypeStruct((n_tok * NA_HEADS, NA_DH), F32)] * 2
    return pl.pallas_call(
        _qkv_kernel,
        grid=(n_tok // tt,),
        in_specs=[pl.BlockSpec((tt, d), lambda i: (first + i, 0)), pl.BlockSpec((d, 3 * d), lambda i: (0, 0))],
        out_specs=out_specs,
        out_shape=out_shape,
        compiler_params=_cparams("arbitrary"),
        name="na_qkv_prompt" if with_kv else "na_qkv_latent",
    )(u, w)


ATTN_LOG2E = math.log2(math.e)
ATTN_Q_SCALE = NA_DH ** -0.5 * ATTN_LOG2E


def _head_pair_masks():
    lane = lax.broadcasted_iota(jnp.int32, (1, 2 * NA_DH), 1)
    return (_onehot(lane < NA_DH), _onehot(lane >= NA_DH))


def _softmax_pv(score_blocks, value_blocks):
    mx = None
    for s in score_blocks:
        bm = jnp.max(s, axis=1, keepdims=True)
        mx = bm if mx is None else jnp.maximum(mx, bm)
    acc, denom = None, None
    for s, v in zip(score_blocks, value_blocks):
        p = jnp.exp2(s - mx)
        ps = jnp.sum(p, axis=1, keepdims=True)
        pv = _dot(p.astype(BF16), v)
        acc = pv if acc is None else acc + pv
        denom = ps if denom is None else denom + ps
    return acc / denom


def _ctx_attn_kernel(q_ref, k_ref, v_ref, o_ref):
    masks = _head_pair_masks()
    for hp in range(NA_HEADS // 2):
        sl = slice(hp * 2 * NA_DH, (hp + 1) * 2 * NA_DH)
        q2, k2, v2 = q_ref[:, sl], k_ref[:, sl], v_ref[:, sl]
        o2 = None
        for msk in masks:
            s = _dot_nt(q2 * msk, k2)
            o = _softmax_pv([s], [v2 * msk])
            o2 = o if o2 is None else o2 + o
        o_ref[:, sl] = o2.astype(o_ref.dtype)


def context_attention(qkv, n_prompt, prompt_len):
    t = n_prompt * prompt_len
    d = D_MODEL
    return pl.pallas_call(
        _ctx_attn_kernel,
        grid=(n_prompt,),
        in_specs=[
            pl.BlockSpec((prompt_len, d), lambda b: (b, 0)),
            pl.BlockSpec((prompt_len, d), lambda b: (b, 1)),
            pl.BlockSpec((prompt_len, d), lambda b: (b, 2)),
        ],
        out_specs=pl.BlockSpec((prompt_len, d), lambda b: (b, 0)),
        out_shape=jax.ShapeDtypeStruct((t, d), BF16),
        compiler_params=_cparams("arbitrary"),
        name="context_attention",
    )(qkv, qkv, qkv)


def _na_kernel(tile_ref, q_ref, kp_ref, kc_ref, kn_ref, vp_ref, vc_ref, vn_ref, kctx_ref, vctx_ref, bias_ref,
               *refs):
    n_cast = (len(refs) - 1) // 2
    cast_in, o_ref, cast_out = refs[:n_cast], refs[n_cast], refs[n_cast + 1:]
    _run_side_casts(cast_in, cast_out)
    j = pl.program_id(1)
    nblk = pl.num_programs(1)
    m = q_ref.shape[0]
    halo = NA_HALO * GRID_W
    n_pairs = (NA_QROWS + 2 * NA_HALO) // 2
    variant = jnp.where(j == 0, 0, jnp.where(j == nblk - 1, 2, 1))
    kinds = [[tile_ref[(variant * NA_QROWS + rq) * n_pairs + kp] for kp in range(n_pairs)]
             for rq in range(NA_QROWS)]

    def bias_of(head):
        return jnp.concatenate(
            [jnp.concatenate([bias_ref[kinds[rq][kp], head] for kp in range(n_pairs)], axis=1)
             for rq in range(NA_QROWS)], axis=0)

    masks = _head_pair_masks()
    for hp in range(NA_HEADS // 2):
        sl = slice(hp * 2 * NA_DH, (hp + 1) * 2 * NA_DH)
        q2 = q_ref[:, sl]
        kloc = jnp.concatenate([kp_ref[m - halo:, sl], kc_ref[:, sl], kn_ref[:halo, sl]], axis=0)
        vloc = jnp.concatenate([vp_ref[m - halo:, sl], vc_ref[:, sl], vn_ref[:halo, sl]], axis=0)
        kctx = kctx_ref[0][:, sl]
        vctx = vctx_ref[0][:, sl]
        o2 = None
        for half, msk in enumerate(masks):
            qm = q2 * msk
            s_loc = _dot_nt(qm, kloc) + bias_of(2 * hp + half)
            s_ctx = _dot_nt(qm, kctx)
            o = _softmax_pv([s_loc, s_ctx], [vloc * msk, vctx * msk])
            o2 = o if o2 is None else o2 + o
        o_ref[:, sl] = o2.astype(o_ref.dtype)


def _na_bias_kernel(rpb_ref, o_ref, *, kinds, n_heads):
    nc, npos = rpb_ref.shape[1], o_ref.shape[2]
    pair_shift = (2 * GRID_W).bit_length() - 1
    c = lax.broadcasted_iota(jnp.int32, (nc, npos), 0)
    pos = lax.broadcasted_iota(jnp.int32, (nc, npos), 1)
    qc = pos >> pair_shift
    kc = pos & (GRID_W - 1)
    c_start = jnp.clip(qc - NA_COLS // 2, 0, GRID_W - NA_COLS)
    col_in = jnp.logical_and(kc >= c_start, kc < c_start + NA_COLS)
    dc = jnp.clip(kc - qc + NA_COLS - 1, 0, 2 * NA_COLS - 2)
    hit = jnp.logical_and(c == dc, col_in)
    right = (pos & GRID_W) != 0
    r1, r2, r3 = _split3(rpb_ref[...] * ATTN_LOG2E)

    def placed(side):
        sel = _onehot(jnp.logical_and(hit, right == side))
        return _dot(r1, sel) + _dot(r2, sel) + _dot(r3, sel)

    left, rght = placed(False), placed(True)
    rows = lambda v, dr: v[dr * n_heads:(dr + 1) * n_heads]
    for k, (dl, dr) in enumerate(kinds):
        shown = col_in[0:1, :]
        if dl is None or dr is None:
            side_ok = jnp.zeros_like(shown) if dl is None and dr is None else (right[0:1, :] == (dl is None))
            shown = jnp.logical_and(shown, side_ok)
        parts = ([rows(left, dl)] if dl is not None else []) + ([rows(rght, dr)] if dr is not None else [])
        val = sum(parts) if parts else jnp.zeros((n_heads, npos), F32)
        o_ref[k] = jnp.where(shown, val, NEG)


def _na_bias_plan():
    ndr = 2 * NA_ROWS - 1
    assert NA_QROWS >= NA_ROWS // 2 and NA_HALO == NA_ROWS // 2 and (NA_QROWS + 2 * NA_HALO) % 2 == 0
    ok = lambda dr: dr if dr is not None and 0 <= dr < ndr else None
    kinds = [(ok(dr), ok(dr + 1)) for dr in range(-1, ndr)]
    first_key_row = (lambda rq: NA_HALO, lambda rq: rq, lambda rq: NA_QROWS + NA_HALO - NA_ROWS)
    table = []
    for first in first_key_row:
        for rq in range(NA_QROWS):
            for kp in range((NA_QROWS + 2 * NA_HALO) // 2):
                drs = []
                for kk in (2 * kp, 2 * kp + 1):
                    visible = first(rq) <= kk < first(rq) + NA_ROWS
                    drs.append(kk - rq + NA_ROWS - 1 - NA_HALO if visible else None)
                kind = (drs[0], drs[1])
                if kind not in kinds:
                    kinds.append(kind)
                table.append(kinds.index(kind))
    return kinds, table


def _na_bias_tiles(rpb):
    nh, ndr, ndc = rpb.shape
    nc = 32
    kinds, table = _na_bias_plan()
    nk = len(kinds)
    rows = jnp.pad(jnp.transpose(rpb, (1, 0, 2)), ((0, 0), (0, 0), (0, nc - ndc))).reshape(ndr * nh, nc)
    npos = GRID_W * 2 * GRID_W
    tiles = pl.pallas_call(
        functools.partial(_na_bias_kernel, kinds=tuple(kinds), n_heads=nh),
        out_shape=jax.ShapeDtypeStruct((nk, nh, npos), F32),
        compiler_params=pltpu.CompilerParams(vmem_limit_bytes=VMEM_LIMIT_BYTES),
        name="na_bias",
    )(rows)
    return tiles.reshape(nk, nh, GRID_W, 2 * GRID_W), jnp.asarray(table, jnp.int32)


def neighbourhood_attention(qkv, k_ctx, v_ctx, rpb, n_sample, sample_len, side_casts=()):
    d = D_MODEL
    m = NA_QROWS * GRID_W
    nblk = sample_len // m
    assert nblk >= 3
    bias, table = _na_bias_tiles(rpb)
    steps = n_sample * nblk

    def blk(col, off):
        return pl.BlockSpec((m, d), lambda b, j, tbl: (b * nblk + jnp.clip(j + off, 0, nblk - 1), col))

    flat, slab_specs, slab_shapes = _side_cast_specs(side_casts, steps, lambda b, j, tbl: b * nblk + j)
    outs = pl.pallas_call(
        _na_kernel,
        grid_spec=pltpu.PrefetchScalarGridSpec(
            num_scalar_prefetch=1,
            grid=(n_sample, nblk),
            in_specs=[
                blk(0, 0), blk(1, -1), blk(1, 0), blk(1, 1), blk(2, -1), blk(2, 0), blk(2, 1),
                pl.BlockSpec((1,) + k_ctx.shape[1:], lambda b, j, tbl: (b, 0, 0), pipeline_mode=pl.Buffered(1)),
                pl.BlockSpec((1,) + v_ctx.shape[1:], lambda b, j, tbl: (b, 0, 0), pipeline_mode=pl.Buffered(1)),
                pl.BlockSpec(bias.shape, lambda b, j, tbl: (0, 0, 0, 0), pipeline_mode=pl.Buffered(1)),
            ] + slab_specs,
            out_specs=[pl.BlockSpec((m, d), lambda b, j, tbl: (b * nblk + j, 0))] + slab_specs,
        ),
        out_shape=[jax.ShapeDtypeStruct((n_sample * sample_len, d), BF16)] + slab_shapes,
        compiler_params=_cparams("arbitrary", "arbitrary"),
        name="neighbourhood_attention",
    )(table, qkv, qkv, qkv, qkv, qkv, qkv, qkv, k_ctx, v_ctx, bias, *flat)
    return [outs[0]] + [o.reshape(a.shape) for o, a in zip(outs[1:], side_casts)]


MOE_ROW_TILE = 512
SC_CORES = 2
SC_SUBCORES = 16
SC_GATHER_CHUNK = 128

def _route_tokens(u, wt, carry):
    w1, w2, w3 = _split3(wt)
    lg = _dot_nt(w1, u) + _dot_nt(w2, u) + _dot_nt(w3, u)
    ne, nt = lg.shape
    e = lax.broadcasted_iota(jnp.int32, lg.shape, 0)
    m1 = jnp.max(lg, axis=0, keepdims=True)
    i1 = jnp.min(jnp.where(lg == m1, e, ne), axis=0, keepdims=True)
    sel1 = e == i1
    lg2 = jnp.where(sel1, -jnp.inf, lg)
    m2 = jnp.max(lg2, axis=0, keepdims=True)
    i2 = jnp.min(jnp.where(lg2 == m2, e, ne), axis=0, keepdims=True)
    sel2 = e == i2
    ex = jnp.exp(m2 - m1)
    wa = 1.0 / (1.0 + ex)
    wb = ex / (1.0 + ex)
    comb = jnp.where(sel1, wa, jnp.where(sel2, wb, 0.0))
    assign = jnp.logical_or(sel1, sel2)
    a = jnp.where(assign, 1.0, 0.0)
    grp = ROUTE_GROUP
    row = lax.broadcasted_iota(jnp.int32, (grp, grp), 0)
    col = lax.broadcasted_iota(jnp.int32, (grp, grp), 1)
    tri = _onehot(row < col)
    ranks = []
    for g0 in range(0, nt, grp):
        ag = a[:, g0:g0 + grp]
        ranks.append(_dot(ag.astype(BF16), tri) + carry)
        carry = carry + jnp.sum(ag, axis=1, keepdims=True)
    pos = jnp.where(assign, jnp.concatenate(ranks, axis=1), -1.0)
    return comb, pos, carry


def _moe_plan(totals, pos, comb, n_tok):
    ne = totals.shape[0]
    tm = MOE_ROW_TILE
    n_tiles = (2 * n_tok) // tm + ne
    tiles_per = (totals + tm - 1) // tm
    tile_end = jnp.cumsum(tiles_per)
    n_used = tile_end[-1]
    base_rows = (tile_end - tiles_per) * tm
    ti = jnp.arange(n_tiles, dtype=jnp.int32)
    tile_expert = jnp.minimum(jnp.sum((ti[:, None] >= tile_end[None, :]).astype(jnp.int32), axis=1), ne - 1)
    last_expert = tile_expert[jnp.maximum(n_used - 1, 0)]
    tile_expert = jnp.where(ti < n_used, tile_expert, last_expert)
    routed = pos >= 0.0
    row = base_rows[:, None] + pos.astype(jnp.int32)
    row_a = jnp.min(jnp.where(routed, row, n_tiles * tm), axis=0)
    row_b = jnp.max(jnp.where(routed, row, -1), axis=0)
    w_a = jnp.sum(jnp.where(jnp.logical_and(routed, row == row_a[None, :]), comb, 0.0), axis=0)
    w_b = jnp.sum(jnp.where(jnp.logical_and(routed, row == row_b[None, :]), comb, 0.0), axis=0)
    tile_rows = jnp.clip(totals[tile_expert] - (ti - (tile_end - tiles_per)[tile_expert]) * tm, 0, tm)
    tile_rows = jnp.where(ti < n_used, tile_rows, 0)
    return dict(n_rows=n_tiles * tm, tile_expert=tile_expert.astype(jnp.int32),
                n_used=n_used.reshape(1).astype(jnp.int32), tile_rows=tile_rows.astype(jnp.int32),
                token_rows=jnp.concatenate([row_a, row_b]).astype(jnp.int32),
                token_weights=jnp.concatenate([w_a[None], w_b[None], jnp.zeros((6, n_tok), F32)], axis=0))


def _pack_bf16_pairs(x):
    half = x.shape[1] // 2
    bits = pltpu.bitcast(x.astype(BF16).astype(F32), jnp.int32)
    return (bits[:, :half] & jnp.int32(-65536)) | lax.shift_right_logical(bits[:, half:], jnp.int32(16))


def _unpack_bf16_pairs(p):
    hi = pltpu.bitcast(p & jnp.int32(-65536), F32)
    lo = pltpu.bitcast(lax.shift_left(p, jnp.int32(16)), F32)
    return jnp.concatenate([hi, lo], axis=1)


def sc_gather_rows(table, idx):
    _, w = table.shape
    b = idx.shape[0]
    workers = SC_CORES * SC_SUBCORES
    ch = SC_GATHER_CHUNK
    n_chunks = b // (workers * ch)
    assert n_chunks * workers * ch == b
    mesh = plsc.VectorSubcoreMesh(core_axis_name="c", subcore_axis_name="s", num_cores=SC_CORES,
                                  num_subcores=SC_SUBCORES)

    @functools.partial(
        pl.kernel, mesh=mesh, out_type=jax.ShapeDtypeStruct((b, w), table.dtype),
        scratch_types=[pltpu.VMEM((ch,), jnp.int32), pltpu.VMEM((ch, w), table.dtype), pltpu.SemaphoreType.DMA])
    def gather(table_hbm, idx_hbm, out_hbm, idx_v, rows_v, sem):
        worker = lax.axis_index("s") * SC_CORES + lax.axis_index("c")

        @pl.loop(0, n_chunks)
        def _(c):
            base = (worker * n_chunks + c) * ch
            pltpu.sync_copy(idx_hbm.at[pl.ds(base, ch)], idx_v)
            pltpu.async_copy(table_hbm.at[idx_v], rows_v, sem).wait()
            pltpu.sync_copy(rows_v, out_hbm.at[pl.ds(base, ch)])

    return gather(table, idx)


def sc_scatter_rows(rows, idx, n_out):
    t, w = rows.shape
    copies = idx.shape[0] // t
    workers = SC_CORES * SC_SUBCORES
    ch = SC_GATHER_CHUNK
    n_chunks = t // (workers * ch)
    assert n_chunks * workers * ch == t and copies * t == idx.shape[0]
    mesh = plsc.VectorSubcoreMesh(core_axis_name="c", subcore_axis_name="s", num_cores=SC_CORES,
                                  num_subcores=SC_SUBCORES)

    @functools.partial(
        pl.kernel, mesh=mesh, out_type=jax.ShapeDtypeStruct((n_out, w), rows.dtype),
        scratch_types=[pltpu.VMEM((ch,), jnp.int32), pltpu.VMEM((ch, w), rows.dtype), pltpu.SemaphoreType.DMA])
    def scatter(rows_hbm, idx_hbm, out_hbm, idx_v, rows_v, sem):
        worker = lax.axis_index("s") * SC_CORES + lax.axis_index("c")

        @pl.loop(0, n_chunks)
        def _(c):
            base = (worker * n_chunks + c) * ch
            pltpu.sync_copy(rows_hbm.at[pl.ds(base, ch)], rows_v)
            for k in range(copies):
                pltpu.sync_copy(idx_hbm.at[pl.ds(k * t + base, ch)], idx_v)
                pltpu.async_copy(rows_v, out_hbm.at[idx_v], sem).wait()

    return scatter(rows, idx)


def _gffn_kernel(te_ref, nu_ref, nr_ref, x_ref, wi_ref, wo_ref, y_ref):
    i = pl.program_id(0)
    used = i < nu_ref[0]

    @pl.when(used)
    def _():
        packed = x_ref[...]
        row = lax.broadcasted_iota(jnp.int32, packed.shape, 0)
        x = _unpack_bf16_pairs(jnp.where(row < nr_ref[i], packed, 0)).astype(BF16)
        y = _swiglu_tile(x, lambda c: wi_ref[0, :, c], lambda r: wo_ref[0, r, :], wo_ref.shape[1])
        y_ref[...] = _pack_bf16_pairs(y)

    @pl.when(jnp.logical_not(used))
    def _():
        y_ref[...] = jnp.zeros_like(y_ref)


def moe_grouped_ffn(xs, w_in, w_out, plan):
    nr, half = xs.shape
    ne, dff, d = w_out.shape
    tm = MOE_ROW_TILE
    return pl.pallas_call(
        _gffn_kernel,
        grid_spec=pltpu.PrefetchScalarGridSpec(
            num_scalar_prefetch=3,
            grid=(nr // tm,),
            in_specs=[
                pl.BlockSpec((tm, half), lambda i, te, nu, tr: (jnp.minimum(i, nu[0] - 1), 0)),
                pl.BlockSpec((1, d, 2 * dff), lambda i, te, nu, tr: (te[i], 0, 0)),
                pl.BlockSpec((1, dff, d), lambda i, te, nu, tr: (te[i], 0, 0)),
            ],
            out_specs=pl.BlockSpec((tm, half), lambda i, te, nu, tr: (i, 0)),
        ),
        out_shape=jax.ShapeDtypeStruct((nr, half), jnp.int32),
        compiler_params=_cparams("arbitrary"),
        name="moe_grouped_ffn",
    )(plan["tile_expert"], plan["n_used"], plan["tile_rows"], xs, w_in, w_out)


def _combine_kernel(ya_ref, yb_ref, w_ref, x_ref, mod_ref, lng_ref, lnb_ref, op_ref, os_ref, *, n_prompt_tiles):
    i = pl.program_id(0)
    w = w_ref[...].T
    moe = w[:, 0:1] * _unpack_bf16_pairs(ya_ref[...]) + w[:, 1:2] * _unpack_bf16_pairs(yb_ref[...])
    z = DEEPNORM_ALPHA * x_ref[...] + mod_ref[0, 5:6, :] * moe
    xn = _layer_norm(z, lng_ref[...], lnb_ref[...])

    @pl.when(i < n_prompt_tiles)
    def _():
        op_ref[...] = xn

    @pl.when(i >= n_prompt_tiles)
    def _():
        os_ref[...] = xn


def moe_combine(y_tok, weights, x, mod, ln_g, ln_b, n_prompt_tok, sample_len):
    t, d = x.shape
    tt = STREAM_TOKEN_TILE
    nt = t // tt
    npt = n_prompt_tok // tt
    seg = functools.partial(_seg_of_tile, tile=tt, n_prompt_tok=n_prompt_tok, sample_len=sample_len)
    return pl.pallas_call(
        functools.partial(_combine_kernel, n_prompt_tiles=npt),
        grid=(nt,),
        in_specs=[
            pl.BlockSpec((tt, d // 2), lambda i: (i, 0)),
            pl.BlockSpec((tt, d // 2), lambda i: (nt + i, 0)),
            pl.BlockSpec((8, tt), lambda i: (0, i)),
            pl.BlockSpec((tt, d), lambda i: (i, 0)),
            pl.BlockSpec((1, 6, d), lambda i: (seg(i), 0, 0)),
            pl.BlockSpec((1, d), lambda i: (0, 0)),
            pl.BlockSpec((1, d), lambda i: (0, 0)),
        ],
        out_specs=[
            pl.BlockSpec((tt, d), lambda i: (jnp.minimum(i, npt - 1), 0)),
            pl.BlockSpec((tt, d), lambda i: (jnp.maximum(i - npt, 0), 0)),
        ],
        out_shape=[jax.ShapeDtypeStruct((n_prompt_tok, d), F32), jax.ShapeDtypeStruct((t - n_prompt_tok, d), F32)],
        compiler_params=_cparams("arbitrary"),
        name="moe_combine",
    )(y_tok, y_tok, weights, x, mod, ln_g.reshape(1, d), ln_b.reshape(1, d))


def _layer1(x, u, mod1, cache_k, cache_v, ln_g, ln_b, w_qkv, rpb, w_out, w_router, moe_w_in, moe_w_out,
            n_prompt, prompt_len, n_sample, sample_len):
    n_prompt_tok = n_prompt * prompt_len
    d = D_MODEL
    qkv_p, k_p, v_p = na_qkv(u, w_qkv, 0, n_prompt_tok, with_kv=True)
    qkv_s, = na_qkv(u, w_qkv, n_prompt_tok, u.shape[0] - n_prompt_tok, with_kv=False)
    attn_p = context_attention(qkv_p, n_prompt, prompt_len)
    k_ctx = cache_k.reshape(n_sample, -1, d).astype(BF16)
    v_ctx = cache_v.reshape(n_sample, -1, d).astype(BF16)
    attn_s, moe_w_in, moe_w_out = neighbourhood_attention(qkv_s, k_ctx, v_ctx, rpb, n_sample, sample_len,
                                                          side_casts=(moe_w_in, moe_w_out))
    x1, u1_packed, comb, pos, totals = mixer_outproj_router(
        attn_p, attn_s, x, mod1, w_out, ln_g[0], ln_b[0], w_router, sample_len)
    plan = _moe_plan(totals[:, 0].astype(jnp.int32), pos, comb, x.shape[0])
    xs = sc_scatter_rows(u1_packed, plan["token_rows"], plan["n_rows"])
    y = moe_grouped_ffn(xs, moe_w_in, moe_w_out, plan)
    y_tok = sc_gather_rows(y, plan["token_rows"])
    y_p, y_s = moe_combine(y_tok, plan["token_weights"], x1, mod1, ln_g[1], ln_b[1], n_prompt_tok, sample_len)
    return y_p, y_s, k_p, v_p


def kernel(x_prompt, x_sample, state_mlstm_C, state_mlstm_n, state_mlstm_m, cache_na_k, cache_na_v, c, c_ctx,
           ada_w, ada_b, ln_g, ln_b, mlstm_w_in, mlstm_b_gates, mlstm_norm_g, mlstm_w_out, na_w_qkv, na_rpb,
           na_w_out, ffn_w_in, ffn_w_out, moe_w_router, moe_w_in, moe_w_out):
    n_prompt, prompt_len, d = x_prompt.shape
    n_sample, sample_len, _ = x_sample.shape
    assert d == D_MODEL and prompt_len == MLSTM_CHUNK and sample_len % MLSTM_SAMPLE_CHUNK == 0
    assert ada_w.shape[0] == DEPTH == 2
    mod = _modulation_tables(c, c_ctx, ada_w, ada_b)
    x2, u2, new_c, new_n, new_m, (w_qkv, na_w_out) = _layer0(
        x_prompt.reshape(-1, d), x_sample.reshape(-1, d), mod[0], mod[1],
        state_mlstm_C[:, 0], state_mlstm_n[:, 0], state_mlstm_m[:, 0], ln_g[0], ln_b[0], mlstm_w_in[0],
        mlstm_b_gates[0], mlstm_norm_g[0], mlstm_w_out[0], ffn_w_in[0], ffn_w_out[0],
        n_prompt, prompt_len, n_sample, sample_len, later_weights=(na_w_qkv[0], na_w_out[0]))
    y_p, y_s, k_p, v_p = _layer1(
        x2, u2, mod[1], cache_na_k[:, 0], cache_na_v[:, 0], ln_g[1], ln_b[1], w_qkv, na_rpb[0], na_w_out,
        moe_w_router[0], moe_w_in[0], moe_w_out[0], n_prompt, prompt_len, n_sample, sample_len)
    kv_shape = (n_prompt, 1, prompt_len, NA_HEADS, NA_DH)
    return (y_p.reshape(x_prompt.shape), y_s.reshape(x_sample.shape), new_c, new_n, new_m,
            k_p.reshape(kv_shape), v_p.reshape(kv_shape))
```

```python
import functools
import math

import jax
import jax.numpy as jnp
from jax import lax
from jax.experimental import pallas as pl
from jax.experimental.pallas import tpu as pltpu
from jax.experimental.pallas import tpu_sc as plsc

F32 = jnp.float32
BF16 = jnp.bfloat16

D_MODEL = 1024
DEPTH = 2
GRID_W = 64
M_HEADS = 4
M_DK = 128
M_DV = 256
M_HK = M_HEADS * M_DK
NA_HEADS = 16
NA_DH = 64
NA_ROWS = 8
NA_COLS = 16
DEEPNORM_ALPHA = (2.0 * DEPTH) ** 0.25
LN_EPS = 1e-5
NEG = -1e30

VMEM_LIMIT_BYTES = 56 * 1024 * 1024

MLSTM_CHUNK = 256
MLSTM_SAMPLE_CHUNK = 256
TOKEN_TILE = 512
STREAM_TOKEN_TILE = 1024
ROUTE_GROUP = 128
FFN_TOKEN_TILE = 512
FFN_SUB = 256
NA_QROWS = 4
NA_HALO = 4


def _cparams(*sem):
    return pltpu.CompilerParams(dimension_semantics=sem, vmem_limit_bytes=VMEM_LIMIT_BYTES)


def _dot(a, b):
    return jnp.dot(a, b, preferred_element_type=F32)


def _dot_nt(a, b):
    return lax.dot_general(a, b, (((1,), (1,)), ((), ())), preferred_element_type=F32)


def _onehot(mask):
    return jnp.where(mask, 1.0, 0.0).astype(BF16)


def _split3(x):
    hi = x.astype(BF16)
    r = x - hi.astype(F32)
    mid = r.astype(BF16)
    lo = (r - mid.astype(F32)).astype(BF16)
    return hi, mid, lo


def _layer_norm(z, g, b):
    mu = jnp.mean(z, axis=-1, keepdims=True)
    zc = z - mu
    var = jnp.mean(zc * zc, axis=-1, keepdims=True)
    return zc * lax.rsqrt(var + LN_EPS) * g + b


def _log_sigmoid(x):
    return jnp.minimum(x, 0.0) - jnp.log(1.0 + jnp.exp(-jnp.abs(x)))


def _seg_of_tile(i, tile, n_prompt_tok, sample_len):
    tok = i * tile
    return jnp.where(tok < n_prompt_tok, 0, 1 + (tok - n_prompt_tok) // sample_len)


def _adaln_kernel(c_ref, w_ref, b_ref, o_ref):
    c = c_ref[...]
    a = (c * jax.nn.sigmoid(c)).astype(BF16)
    o_ref[0] = _dot(a, w_ref[0].astype(BF16)) + b_ref[0]


def adaln(cvec, ada_w, ada_b):
    depth, d, n = ada_w.shape
    tn = 1536
    return pl.pallas_call(
        _adaln_kernel,
        grid=(depth, n // tn),
        in_specs=[
            pl.BlockSpec((8, d), lambda l, j: (0, 0)),
            pl.BlockSpec((1, d, tn), lambda l, j: (l, 0, j)),
            pl.BlockSpec((1, 1, tn), lambda l, j: (l, 0, j)),
        ],
        out_specs=pl.BlockSpec((1, 8, tn), lambda l, j: (l, 0, j)),
        out_shape=jax.ShapeDtypeStruct((depth, 8, n), F32),
        compiler_params=_cparams("arbitrary", "arbitrary"),
        name="adaln",
    )(cvec, ada_w, ada_b.reshape(depth, 1, n))


def _inproj_kernel(xp_ref, xs_ref, mod_ref, w_ref, wg_ref, bg_ref, qkv_ref, o_ref, g_ref, gt_ref, *, n_prompt_tiles):
    x = jnp.where(pl.program_id(0) < n_prompt_tiles, xp_ref[...], xs_ref[...])
    u = (x * (1.0 + mod_ref[0, 1:2, :]) + mod_ref[0, 0:1, :]).astype(BF16)
    nqkv = qkv_ref.shape[1]
    half = nqkv // 2
    g = _dot_nt(u, wg_ref[...]) + bg_ref[...]
    p0 = _dot_nt(u, w_ref[:half, :])
    col = lax.broadcasted_iota(jnp.int32, g.shape, 1)
    g = jnp.where(((col >> 2) & 1) == 1, _log_sigmoid(g), g)
    g_ref[...] = g
    p1 = _dot_nt(u, w_ref[half:nqkv, :])
    qkv_ref[:, :half] = p0.astype(BF16)
    gt_ref[...] = g.T
    p2 = _dot_nt(u, w_ref[nqkv:, :])
    qkv_ref[:, half:] = p1.astype(BF16)
    o_ref[...] = p2


def _split_token_specs(tile, d, n_prompt_tiles):
    return [pl.BlockSpec((tile, d), lambda i: (jnp.minimum(i, n_prompt_tiles - 1), 0)),
            pl.BlockSpec((tile, d), lambda i: (jnp.maximum(i - n_prompt_tiles, 0), 0))]


def mlstm_inproj(xp, xs, mod, w_main, w_gate, b_gate, sample_len):
    n_prompt_tok, d = xp.shape
    t = n_prompt_tok + xs.shape[0]
    n_main = w_main.shape[0]
    n_qkv = 2 * M_HK + D_MODEL
    ng = w_gate.shape[0]
    tt = TOKEN_TILE
    npt = n_prompt_tok // tt
    seg = functools.partial(_seg_of_tile, tile=tt, n_prompt_tok=n_prompt_tok, sample_len=sample_len)
    return pl.pallas_call(
        functools.partial(_inproj_kernel, n_prompt_tiles=npt),
        grid=(t // tt,),
        in_specs=_split_token_specs(tt, d, npt) + [
            pl.BlockSpec((1, 6, d), lambda i: (seg(i), 0, 0)),
            pl.BlockSpec((n_main, d), lambda i: (0, 0)),
            pl.BlockSpec((ng, d), lambda i: (0, 0)),
            pl.BlockSpec((1, ng), lambda i: (0, 0)),
        ],
        out_specs=[
            pl.BlockSpec((tt, n_qkv), lambda i: (i, 0)),
            pl.BlockSpec((tt, n_main - n_qkv), lambda i: (i, 0)),
            pl.BlockSpec((tt, ng), lambda i: (i, 0)),
            pl.BlockSpec((ng, tt), lambda i: (0, i)),
        ],
        out_shape=[
            jax.ShapeDtypeStruct((t, n_qkv), BF16),
            jax.ShapeDtypeStruct((t, n_main - n_qkv), F32),
            jax.ShapeDtypeStruct((t, ng), F32),
            jax.ShapeDtypeStruct((ng, t), F32),
        ],
        compiler_params=_cparams("arbitrary"),
        name="mlstm_inproj",
    )(xp, xs, mod, w_main, w_gate, b_gate.reshape(1, ng))


def _mlstm_gate_sums(g, gt, backward):
    L = g.shape[0]
    row = lax.broadcasted_iota(jnp.int32, (L, L), 0)
    col = lax.broadcasted_iota(jnp.int32, (L, L), 1)
    lower = col <= row
    upper = col >= row
    tri_col = _onehot(upper if backward else lower)
    tri_row = _onehot(lower if backward else upper)
    ghi, gmid, glo = _split3(g)
    b_cols = _dot(tri_col, ghi) + _dot(tri_col, gmid) + _dot(tri_col, glo)
    thi, tmid, tlo = _split3(gt)
    b_rows = _dot(thi, tri_row) + _dot(tmid, tri_row) + _dot(tlo, tri_row)
    return b_cols, b_rows


def _mlstm_heads(items):
    scale = M_DK ** -0.5
    log_scale = math.log(scale)
    L = items[0][0].shape[0]
    row = lax.broadcasted_iota(jnp.int32, (L, L), 0)
    col = lax.broadcasted_iota(jnp.int32, (L, L), 1)
    masks = {False: col <= row, True: col >= row}
    n = range(len(items))
    qh, kh, vh, li_row, lf_row, b_row, b_col, backward, state = zip(*items)
    has_state = [st is not None for st in state]
    m_prev = [st[2] if st is not None else 0.0 for st in state]

    qk = [_dot_nt(qh[i], kh[i]) for i in n]
    qc = [_dot(qh[i], state[i][0].astype(BF16)) if has_state[i] else None for i in n]
    qn = [_dot_nt(qh[i], state[i][1].astype(BF16))[:, 0:1] if has_state[i] else None for i in n]
    total = [jnp.sum(lf_row[i], axis=1, keepdims=True) for i in n]
    d = [jnp.where(masks[backward[i]], b_col[i] + (li_row[i] - b_row[i] + log_scale), NEG) for i in n]
    inter = [b_col[i] + m_prev[i] if has_state[i] else b_col[i] for i in n]
    m_t = [jnp.maximum(inter[i], jnp.max(d[i], axis=1, keepdims=True) - log_scale) for i in n]
    p = [jnp.exp(d[i] - m_t[i]) for i in n]
    s = [qk[i] * p[i] for i in n]
    den = [jnp.sum(s[i], axis=1, keepdims=True) for i in n]
    num = [_dot(s[i].astype(BF16), vh[i]) for i in n]
    a = [jnp.exp(inter[i] - m_t[i]) * scale if has_state[i] else None for i in n]
    num = [a[i] * qc[i] + num[i] if has_state[i] else num[i] for i in n]
    den = [a[i] * qn[i] + den[i] if has_state[i] else den[i] for i in n]
    hh = [num[i] / jnp.maximum(jnp.abs(den[i]), jnp.exp(-m_t[i])) for i in n]

    g_row = [total[i] - b_row[i] + li_row[i] for i in n]
    m_new = [jnp.maximum(total[i] + m_prev[i], jnp.max(g_row[i], axis=1, keepdims=True)) for i in n]
    ws_row = [jnp.exp(g_row[i] - m_new[i]) for i in n]
    kt = [kh[i].astype(F32).T for i in n]
    c_new = [_dot((kt[i] * ws_row[i]).astype(BF16), vh[i]) for i in n]
    n_new = [_dot(jnp.broadcast_to(ws_row[i], (8, L)).astype(BF16), kh[i]) for i in n]
    a0 = [jnp.exp(total[i] + m_prev[i] - m_new[i]) if has_state[i] else None for i in n]
    c_new = [a0[i] * state[i][0] + c_new[i] if has_state[i] else c_new[i] for i in n]
    n_new = [a0[i] * state[i][1] + n_new[i] if has_state[i] else n_new[i] for i in n]
    return hh, c_new, n_new, m_new


def _head_norm_gate(ht, norm_g, ogate):
    mu = jnp.mean(ht, axis=-1, keepdims=True)
    hc = ht - mu
    var = jnp.mean(hc * hc, axis=-1, keepdims=True)
    return (hc * lax.rsqrt(var + LN_EPS) * norm_g * jax.nn.sigmoid(ogate)).astype(BF16)


def _head_operands(q_ref, k_ref, v_ref, h):
    return (q_ref[:, h * M_DK:(h + 1) * M_DK], k_ref[:, h * M_DK:(h + 1) * M_DK], v_ref[:, h * M_DV:(h + 1) * M_DV])


def _gate_operands(gt, b_rows, b_cols, direction, h):
    ci = direction * 8 + h
    cf = direction * 8 + 4 + h
    return gt[ci:ci + 1, :], gt[cf:cf + 1, :], b_rows[cf:cf + 1, :], b_cols[:, cf:cf + 1]


def _mlstm_prompt_kernel(q_ref, k_ref, v_ref, g_ref, gt_ref, o_ref, ng_ref, a_ref, cf_ref, nf_ref, mf_ref):
    g, gt = g_ref[...], gt_ref[...]
    sums = [_mlstm_gate_sums(g, gt, direction == 1) for direction in (0, 1)]
    keys = [(direction, h) for direction in (0, 1) for h in range(M_HEADS)]
    items = [_head_operands(q_ref, k_ref, v_ref, h)
             + _gate_operands(gt, sums[direction][1], sums[direction][0], direction, h) + (direction == 1, None)
             for direction, h in keys]
    hh, c_new, n_new, m_new = _mlstm_heads(items)
    for i, (direction, h) in enumerate(keys):
        cf_ref[0, 0, direction, h] = c_new[i]
        nf_ref[0, 0, direction, h] = n_new[i]
        mf_ref[0, 0, direction, h] = jnp.broadcast_to(m_new[i], mf_ref.shape[-2:])
    for h in range(M_HEADS):
        sl = slice(h * M_DV, (h + 1) * M_DV)
        a_ref[:, sl] = _head_norm_gate(hh[h] + hh[M_HEADS + h], ng_ref[:, sl], o_ref[:, sl])


def _side_cast_specs(arrays, steps, step_of):
    flat = [a.reshape(-1, a.shape[-1]) for a in arrays]
    assert all(a.shape[0] % (steps * 16) == 0 for a in flat)
    specs = [pl.BlockSpec((a.shape[0] // steps, a.shape[1]), lambda *idx: (step_of(*idx), 0)) for a in flat]
    return flat, specs, [jax.ShapeDtypeStruct(a.shape, BF16) for a in flat]


def _run_side_casts(srcs, dsts):
    for src, dst in zip(srcs, dsts):
        dst[...] = src[...].astype(dst.dtype)


def _mlstm_sample_kernel(*refs, direction, n_units, n_cast):
    backward = direction == 1
    refs = list(refs)
    unit_refs = [tuple(refs.pop(0) for _ in range(5)) for _ in range(n_units)]
    c0_ref, n0_ref, m0_ref = refs.pop(0), refs.pop(0), refs.pop(0)
    if backward:
        hprev_ref = refs.pop(0)
        o_refs = [refs.pop(0) for _ in range(n_units)]
        ng_ref = refs.pop(0)
    cast_in = [refs.pop(0) for _ in range(n_cast)]
    out_ref = refs.pop(0)
    cast_out = [refs.pop(0) for _ in range(n_cast)]
    c_scr, n_scr, m_scr = refs
    _run_side_casts(cast_in, cast_out)

    @pl.when(pl.program_id(0) == 0)
    def _():
        c_scr[...] = c0_ref[...]
        n_scr[...] = n0_ref[...]
        m_scr[...] = m0_ref[...]

    keys, items = [], []
    for u, (q_ref, k_ref, v_ref, g_ref, gt_ref) in enumerate(unit_refs):
        gt = gt_ref[...]
        b_cols, b_rows = _mlstm_gate_sums(g_ref[...], gt, backward)
        for h in range(M_HEADS):
            state = (c_scr[u, h], n_scr[u, h], m_scr[u, h][0:1, 0:1])
            keys.append((u, h))
            items.append(_head_operands(q_ref, k_ref, v_ref, h) + _gate_operands(gt, b_rows, b_cols, direction, h)
                         + (backward, state))
    hh, c_new, n_new, m_new = _mlstm_heads(items)
    for i, (u, h) in enumerate(keys):
        c_scr[u, h] = c_new[i]
        n_scr[u, h] = n_new[i]
        m_scr[u, h] = jnp.broadcast_to(m_new[i], m_scr.shape[-2:])
        sl = slice(h * M_DV, (h + 1) * M_DV)
        if backward:
            out_ref[u, :, sl] = _head_norm_gate(hprev_ref[u, :, sl] + hh[i], ng_ref[:, sl], o_refs[u][:, sl])
        else:
            out_ref[u, :, sl] = hh[i]


def _mlstm_chunk_specs(block_of, L):
    return [
        pl.BlockSpec((L, M_HK), lambda s: (block_of(s), 0)),
        pl.BlockSpec((L, M_HK), lambda s: (block_of(s), 1)),
        pl.BlockSpec((L, D_MODEL), lambda s: (block_of(s), 1)),
        pl.BlockSpec((L, 16), lambda s: (block_of(s), 0)),
        pl.BlockSpec((16, L), lambda s: (0, block_of(s))),
    ]


def mlstm_prompt(qkv, g, gt, ogate, norm_g, n_prompt):
    L = MLSTM_CHUNK
    state_blk = lambda *tail: pl.BlockSpec((1, 1, 2, M_HEADS) + tail, lambda s: (s, 0, 0, 0, 0, 0))
    return pl.pallas_call(
        _mlstm_prompt_kernel,
        grid=(n_prompt,),
        in_specs=_mlstm_chunk_specs(lambda s: s, L) + [
            pl.BlockSpec((L, D_MODEL), lambda s: (s, 0)),
            pl.BlockSpec((1, D_MODEL), lambda s: (0, 0)),
        ],
        out_specs=[pl.BlockSpec((L, D_MODEL), lambda s: (s, 0)),
                   state_blk(M_DK, M_DV), state_blk(8, M_DK), state_blk(8, 128)],
        out_shape=[
            jax.ShapeDtypeStruct((n_prompt * L, D_MODEL), BF16),
            jax.ShapeDtypeStruct((n_prompt, 1, 2, M_HEADS, M_DK, M_DV), F32),
            jax.ShapeDtypeStruct((n_prompt, 1, 2, M_HEADS, 8, M_DK), F32),
            jax.ShapeDtypeStruct((n_prompt, 1, 2, M_HEADS, 8, 128), F32),
        ],
        compiler_params=_cparams("arbitrary"),
        name="mlstm_prompt",
    )(qkv, qkv, qkv, g, gt, ogate, norm_g.reshape(1, D_MODEL))


def mlstm_sample(direction, qkv, g, gt, c0, n0, m0, first_block, chunks_per_sample,
                 hprev=None, ogate=None, norm_g=None, side_casts=()):
    L = MLSTM_SAMPLE_CHUNK
    backward = direction == 1
    cps = chunks_per_sample
    n_units = c0.shape[0]
    pos = (lambda s: cps - 1 - s) if backward else (lambda s: s)
    unit_block = [functools.partial(lambda s, u: first_block + u * cps + pos(s), u=u) for u in range(n_units)]
    whole = lambda a: pl.BlockSpec(a.shape, lambda s: (0,) * a.ndim)
    in_specs, args = [], []
    for u in range(n_units):
        in_specs += _mlstm_chunk_specs(unit_block[u], L)
        args += [qkv, qkv, qkv, g, gt]
    in_specs += [whole(c0), whole(n0), whole(m0)]
    args += [c0, n0, m0]
    if backward:
        in_specs += [pl.BlockSpec((n_units, L, D_MODEL), lambda s: (0, pos(s), 0))]
        in_specs += [pl.BlockSpec((L, D_MODEL), functools.partial(lambda s, u: (unit_block[u](s), 0), u=u))
                     for u in range(n_units)]
        in_specs += [pl.BlockSpec((1, D_MODEL), lambda s: (0, 0))]
        args += [hprev] + [ogate] * n_units + [norm_g.reshape(1, D_MODEL)]
    flat, cast_specs, cast_shapes = _side_cast_specs(side_casts, cps, lambda s: s)
    outs = pl.pallas_call(
        functools.partial(_mlstm_sample_kernel, direction=direction, n_units=n_units, n_cast=len(flat)),
        grid=(cps,),
        in_specs=in_specs + cast_specs,
        out_specs=[pl.BlockSpec((n_units, L, D_MODEL), lambda s: (0, pos(s), 0))] + cast_specs,
        out_shape=[jax.ShapeDtypeStruct((n_units, cps * L, D_MODEL), BF16 if backward else F32)] + cast_shapes,
        scratch_shapes=[pltpu.VMEM(c0.shape, F32), pltpu.VMEM(n0.shape, F32), pltpu.VMEM(m0.shape, F32)],
        compiler_params=_cparams("arbitrary"),
        name="mlstm_sample_bwd" if backward else "mlstm_sample_fwd",
    )(*args, *flat)
    return [outs[0]] + [o.reshape(a.shape) for o, a in zip(outs[1:], side_casts)]


def _outproj_router_kernel(ap_ref, as_ref, x_ref, mod_ref, w_ref, lng_ref, lnb_ref, wr_ref,
                           xo_ref, up_ref, comb_ref, pos_ref, total_ref, carry_scr, *, n_prompt_tiles):
    i = pl.program_id(0)

    @pl.when(i == 0)
    def _():
        carry_scr[...] = jnp.zeros_like(carry_scr)

    a = jnp.where(i < n_prompt_tiles, ap_ref[...], as_ref[...])
    z = DEEPNORM_ALPHA * x_ref[...] + mod_ref[0, 2:3, :] * _dot(a, w_ref[...])
    xn = _layer_norm(z, lng_ref[...], lnb_ref[...])
    xo_ref[...] = xn
    u = xn * (1.0 + mod_ref[0, 4:5, :]) + mod_ref[0, 3:4, :]
    up_ref[...] = _pack_bf16_pairs(u)
    comb, pos, carry_new = _route_tokens(u.astype(BF16), wr_ref[...], carry_scr[:, 0:1])
    comb_ref[...] = comb
    pos_ref[...] = pos
    carry_scr[...] = jnp.broadcast_to(carry_new, carry_scr.shape)
    total_ref[...] = jnp.broadcast_to(carry_new, carry_scr.shape)


def mixer_outproj_router(a_p, a_s, x, mod, w, ln_g, ln_b, w_router, sample_len):
    t, d = x.shape
    ne = w_router.shape[1]
    tt = STREAM_TOKEN_TILE
    npt = a_p.shape[0] // tt
    seg = functools.partial(_seg_of_tile, tile=tt, n_prompt_tok=a_p.shape[0], sample_len=sample_len)
    return pl.pallas_call(
        functools.partial(_outproj_router_kernel, n_prompt_tiles=npt),
        grid=(t // tt,),
        in_specs=_split_token_specs(tt, d, npt) + [
            pl.BlockSpec((tt, d), lambda i: (i, 0)),
            pl.BlockSpec((1, 6, d), lambda i: (seg(i), 0, 0)),
            pl.BlockSpec((d, d), lambda i: (0, 0)),
            pl.BlockSpec((1, d), lambda i: (0, 0)),
            pl.BlockSpec((1, d), lambda i: (0, 0)),
            pl.BlockSpec((ne, d), lambda i: (0, 0)),
        ],
        out_specs=[
            pl.BlockSpec((tt, d), lambda i: (i, 0)),
            pl.BlockSpec((tt, d // 2), lambda i: (i, 0)),
            pl.BlockSpec((ne, tt), lambda i: (0, i)),
            pl.BlockSpec((ne, tt), lambda i: (0, i)),
            pl.BlockSpec((ne, 128), lambda i: (0, 0)),
        ],
        out_shape=[
            jax.ShapeDtypeStruct((t, d), F32),
            jax.ShapeDtypeStruct((t, d // 2), jnp.int32),
            jax.ShapeDtypeStruct((ne, t), F32),
            jax.ShapeDtypeStruct((ne, t), F32),
            jax.ShapeDtypeStruct((ne, 128), F32),
        ],
        scratch_shapes=[pltpu.VMEM((ne, 128), F32)],
        compiler_params=_cparams("arbitrary"),
        name="mixer_outproj_router",
    )(a_p, a_s, x, mod, w, ln_g.reshape(1, d), ln_b.reshape(1, d), w_router.T)


def _swiglu_tile(x, wi, wo, dff):
    sub = FFN_SUB
    proj = lambda j: (_dot(x, wi(slice(j * sub, (j + 1) * sub))), _dot(x, wi(slice(dff + j * sub, dff + (j + 1) * sub))))
    acts, cur = [], proj(0)
    for j in range(dff // sub):
        nxt = proj(j + 1) if (j + 1) * sub < dff else None
        gp, up = cur
        acts.append((gp * jax.nn.sigmoid(gp) * up).astype(BF16))
        cur = nxt
    return _dot(jnp.concatenate(acts, axis=1), wo(slice(0, dff)))


def _mixer_ffn_kernel(ap_ref, as_ref, xp_ref, xs_ref, mod_ref, modn_ref, wmix_ref, wi_ref, wo_ref, ln_ref,
                      xo_ref, uo_ref, *, n_prompt_tiles):
    is_prompt = pl.program_id(0) < n_prompt_tiles
    a = jnp.where(is_prompt, ap_ref[...], as_ref[...])
    x = jnp.where(is_prompt, xp_ref[...], xs_ref[...])
    z = DEEPNORM_ALPHA * x + mod_ref[0, 2:3, :] * _dot(a, wmix_ref[...])
    x1 = _layer_norm(z, ln_ref[0:1, :], ln_ref[1:2, :])
    u1 = (x1 * (1.0 + mod_ref[0, 4:5, :]) + mod_ref[0, 3:4, :]).astype(BF16)
    f = _swiglu_tile(u1, lambda c: wi_ref[:, c], lambda r: wo_ref[r, :], wo_ref.shape[0])
    x2 = _layer_norm(DEEPNORM_ALPHA * x1 + mod_ref[0, 5:6, :] * f, ln_ref[2:3, :], ln_ref[3:4, :])
    xo_ref[...] = x2
    uo_ref[...] = (x2 * (1.0 + modn_ref[0, 1:2, :]) + modn_ref[0, 0:1, :]).astype(BF16)


def _resident(shape):
    return pl.BlockSpec(shape, lambda i: (0,) * len(shape), pipeline_mode=pl.Buffered(1))


def mixer_ffn(a_p, a_s, xp, xs, mod, mod_next, w_mix, w_in, w_out, ln_g, ln_b, sample_len):
    n_prompt_tok, d = xp.shape
    t = n_prompt_tok + xs.shape[0]
    tm = FFN_TOKEN_TILE
    npt = n_prompt_tok // tm
    seg = functools.partial(_seg_of_tile, tile=tm, n_prompt_tok=n_prompt_tok, sample_len=sample_len)
    ln = jnp.stack([ln_g[0], ln_b[0], ln_g[1], ln_b[1]])
    return pl.pallas_call(
        functools.partial(_mixer_ffn_kernel, n_prompt_tiles=npt),
        grid=(t // tm,),
        in_specs=_split_token_specs(tm, d, npt) + _split_token_specs(tm, d, npt) + [
            pl.BlockSpec((1, 6, d), lambda i: (seg(i), 0, 0)),
            pl.BlockSpec((1, 6, d), lambda i: (seg(i), 0, 0)),
            _resident(w_mix.shape), _resident(w_in.shape), _resident(w_out.shape), _resident(ln.shape),
        ],
        out_specs=[pl.BlockSpec((tm, d), lambda i: (i, 0)), pl.BlockSpec((tm, d), lambda i: (i, 0))],
        out_shape=[jax.ShapeDtypeStruct((t, d), F32), jax.ShapeDtypeStruct((t, d), BF16)],
        compiler_params=_cparams("arbitrary"),
        name="mixer_ffn",
    )(a_p, a_s, xp, xs, mod, mod_next, w_mix, w_in, w_out, ln)


def _modulation_tables(c, c_ctx, ada_w, ada_b):
    n_s = c.shape[0]
    cvec = jnp.concatenate([c_ctx[None, :], c, jnp.zeros((8 - 1 - n_s, c.shape[1]), F32)], axis=0)
    mod = adaln(cvec, ada_w, ada_b)
    return mod[:, :1 + n_s, :].reshape(ada_w.shape[0], 1 + n_s, 6, c.shape[1])


def _layer0(xp, xs, mod0, mod1, state_c, state_n, state_m, ln_g, ln_b, w_in, b_gates, norm_g, w_out,
            ffn_w_in, ffn_w_out, n_prompt, prompt_len, n_sample, sample_len, later_weights):
    n_prompt_tok = n_prompt * prompt_len
    n_main = 2 * M_HK + 2 * D_MODEL
    w_t = w_in.T
    qkv, ogate, g, gt = mlstm_inproj(xp, xs, mod0, w_t[:n_main].astype(BF16), w_t[n_main:].astype(BF16),
                                     b_gates.reshape(-1), sample_len)
    npc = n_prompt_tok // MLSTM_SAMPLE_CHUNK
    cps = sample_len // MLSTM_SAMPLE_CHUNK
    rep = lambda a: jnp.broadcast_to(a[..., None, :], a.shape[:-1] + (8, a.shape[-1]))
    n0 = rep(state_n)
    m0 = jnp.broadcast_to(state_m[..., None, None], state_m.shape + (8, 128))
    a_p, new_c, nf, mf = mlstm_prompt(qkv, g, gt, ogate, norm_g, n_prompt)
    hf, w_out, ffn_w_in, ffn_w_out = mlstm_sample(0, qkv, g, gt, state_c[:, 0], n0[:, 0], m0[:, 0], npc, cps,
                                                  side_casts=(w_out, ffn_w_in, ffn_w_out))
    a_s, *later_weights = mlstm_sample(1, qkv, g, gt, state_c[:, 1], n0[:, 1], m0[:, 1], npc, cps,
                                       hprev=hf, ogate=ogate, norm_g=norm_g, side_casts=later_weights)
    x2, u2 = mixer_ffn(a_p, a_s.reshape(-1, D_MODEL), xp, xs, mod0, mod1, w_out, ffn_w_in, ffn_w_out, ln_g, ln_b,
                       sample_len)
    return x2, u2, new_c, nf[..., 0, :], mf[..., 0, 0], later_weights


def _qkv_kernel(u_ref, w_ref, qkv_ref, *kv_refs):
    tt, d = u_ref.shape
    u = u_ref[...]

    def head_major(out_ref, p):
        for h in range(NA_HEADS):
            out_ref[pl.ds(h, tt, stride=NA_HEADS), :] = p[:, h * NA_DH:(h + 1) * NA_DH]

    pq = _dot(u, w_ref[:, :d])
    pk = _dot(u, w_ref[:, d:2 * d])
    qkv_ref[:, :d] = (pq * ATTN_Q_SCALE).astype(BF16)
    pv = _dot(u, w_ref[:, 2 * d:])
    qkv_ref[:, d:2 * d] = pk.astype(BF16)
    if kv_refs:
        head_major(kv_refs[0], pk)
    qkv_ref[:, 2 * d:] = pv.astype(BF16)
    if kv_refs:
        head_major(kv_refs[1], pv)


def na_qkv(u, w, first_tok, n_tok, with_kv):
    d = u.shape[1]
    tt = TOKEN_TILE
    first = first_tok // tt
    out_specs = [pl.BlockSpec((tt, 3 * d), lambda i: (i, 0))]
    out_shape = [jax.ShapeDtypeStruct((n_tok, 3 * d), BF16)]
    if with_kv:
        out_specs += [pl.BlockSpec((tt * NA_HEADS, NA_DH), lambda i: (i, 0))] * 2
        out_shape += [jax.ShapeDtypeStruct((n_tok * NA_HEADS, NA_DH), F32)] * 2
    return pl.pallas_call(
        _qkv_kernel,
        grid=(n_tok // tt,),
        in_specs=[pl.BlockSpec((tt, d), lambda i: (first + i, 0)), pl.BlockSpec((d, 3 * d), lambda i: (0, 0))],
        out_specs=out_specs,
        out_shape=out_shape,
        compiler_params=_cparams("arbitrary"),
        name="na_qkv_prompt" if with_kv else "na_qkv_latent",
    )(u, w)


ATTN_LOG2E = math.log2(math.e)
ATTN_Q_SCALE = NA_DH ** -0.5 * ATTN_LOG2E


def _head_pair_masks():
    lane = lax.broadcasted_iota(jnp.int32, (1, 2 * NA_DH), 1)
    return (_onehot(lane < NA_DH), _onehot(lane >= NA_DH))


def _softmax_pv(score_blocks, value_blocks):
    mx = None
    for s in score_blocks:
        bm = jnp.max(s, axis=1, keepdims=True)
        mx = bm if mx is None else jnp.maximum(mx, bm)
    acc, denom = None, None
    for s, v in zip(score_blocks, value_blocks):
        p = jnp.exp2(s - mx)
        ps = jnp.sum(p, axis=1, keepdims=True)
        pv = _dot(p.astype(BF16), v)
        acc = pv if acc is None else acc + pv
        denom = ps if denom is None else denom + ps
    return acc / denom


def _ctx_attn_kernel(q_ref, k_ref, v_ref, o_ref):
    masks = _head_pair_masks()
    for hp in range(NA_HEADS // 2):
        sl = slice(hp * 2 * NA_DH, (hp + 1) * 2 * NA_DH)
        q2, k2, v2 = q_ref[:, sl], k_ref[:, sl], v_ref[:, sl]
        o2 = None
        for msk in masks:
            s = _dot_nt(q2 * msk, k2)
            o = _softmax_pv([s], [v2 * msk])
            o2 = o if o2 is None else o2 + o
        o_ref[:, sl] = o2.astype(o_ref.dtype)


def context_attention(qkv, n_prompt, prompt_len):
    t = n_prompt * prompt_len
    d = D_MODEL
    return pl.pallas_call(
        _ctx_attn_kernel,
        grid=(n_prompt,),
        in_specs=[
            pl.BlockSpec((prompt_len, d), lambda b: (b, 0)),
            pl.BlockSpec((prompt_len, d), lambda b: (b, 1)),
            pl.BlockSpec((prompt_len, d), lambda b: (b, 2)),
        ],
        out_specs=pl.BlockSpec((prompt_len, d), lambda b: (b, 0)),
        out_shape=jax.ShapeDtypeStruct((t, d), BF16),
        compiler_params=_cparams("arbitrary"),
        name="context_attention",
    )(qkv, qkv, qkv)


def _na_kernel(tile_ref, q_ref, kp_ref, kc_ref, kn_ref, vp_ref, vc_ref, vn_ref, kctx_ref, vctx_ref, bias_ref,
               *refs):
    n_cast = (len(refs) - 1) // 2
    cast_in, o_ref, cast_out = refs[:n_cast], refs[n_cast], refs[n_cast + 1:]
    _run_side_casts(cast_in, cast_out)
    j = pl.program_id(1)
    nblk = pl.num_programs(1)
    m = q_ref.shape[0]
    halo = NA_HALO * GRID_W
    n_pairs = (NA_QROWS + 2 * NA_HALO) // 2
    variant = jnp.where(j == 0, 0, jnp.where(j == nblk - 1, 2, 1))
    kinds = [[tile_ref[(variant * NA_QROWS + rq) * n_pairs + kp] for kp in range(n_pairs)]
             for rq in range(NA_QROWS)]

    def bias_of(head):
        return jnp.concatenate(
            [jnp.concatenate([bias_ref[kinds[rq][kp], head] for kp in range(n_pairs)], axis=1)
             for rq in range(NA_QROWS)], axis=0)

    masks = _head_pair_masks()
    for hp in range(NA_HEADS // 2):
        sl = slice(hp * 2 * NA_DH, (hp + 1) * 2 * NA_DH)
        q2 = q_ref[:, sl]
        kloc = jnp.concatenate([kp_ref[m - halo:, sl], kc_ref[:, sl], kn_ref[:halo, sl]], axis=0)
        vloc = jnp.concatenate([vp_ref[m - halo:, sl], vc_ref[:, sl], vn_ref[:halo, sl]], axis=0)
        kctx = kctx_ref[0][:, sl]
        vctx = vctx_ref[0][:, sl]
        o2 = None
        for half, msk in enumerate(masks):
            qm = q2 * msk
            s_loc = _dot_nt(qm, kloc) + bias_of(2 * hp + half)
            s_ctx = _dot_nt(qm, kctx)
            o = _softmax_pv([s_loc, s_ctx], [vloc * msk, vctx * msk])
            o2 = o if o2 is None else o2 + o
        o_ref[:, sl] = o2.astype(o_ref.dtype)


def _na_bias_kernel(rpb_ref, o_ref, *, kinds, n_heads):
    nc, npos = rpb_ref.shape[1], o_ref.shape[2]
    pair_shift = (2 * GRID_W).bit_length() - 1
    c = lax.broadcasted_iota(jnp.int32, (nc, npos), 0)
    pos = lax.broadcasted_iota(jnp.int32, (nc, npos), 1)
    qc = pos >> pair_shift
    kc = pos & (GRID_W - 1)
    c_start = jnp.clip(qc - NA_COLS // 2, 0, GRID_W - NA_COLS)
    col_in = jnp.logical_and(kc >= c_start, kc < c_start + NA_COLS)
    dc = jnp.clip(kc - qc + NA_COLS - 1, 0, 2 * NA_COLS - 2)
    hit = jnp.logical_and(c == dc, col_in)
    right = (pos & GRID_W) != 0
    r1, r2, r3 = _split3(rpb_ref[...] * ATTN_LOG2E)

    def placed(side):
        sel = _onehot(jnp.logical_and(hit, right == side))
        return _dot(r1, sel) + _dot(r2, sel) + _dot(r3, sel)

    left, rght = placed(False), placed(True)
    rows = lambda v, dr: v[dr * n_heads:(dr + 1) * n_heads]
    for k, (dl, dr) in enumerate(kinds):
        shown = col_in[0:1, :]
        if dl is None or dr is None:
            side_ok = jnp.zeros_like(shown) if dl is None and dr is None else (right[0:1, :] == (dl is None))
            shown = jnp.logical_and(shown, side_ok)
        parts = ([rows(left, dl)] if dl is not None else []) + ([rows(rght, dr)] if dr is not None else [])
        val = sum(parts) if parts else jnp.zeros((n_heads, npos), F32)
        o_ref[k] = jnp.where(shown, val, NEG)


def _na_bias_plan():
    ndr = 2 * NA_ROWS - 1
    assert NA_QROWS >= NA_ROWS // 2 and NA_HALO == NA_ROWS // 2 and (NA_QROWS + 2 * NA_HALO) % 2 == 0
    ok = lambda dr: dr if dr is not None and 0 <= dr < ndr else None
    kinds = [(ok(dr), ok(dr + 1)) for dr in range(-1, ndr)]
    first_key_row = (lambda rq: NA_HALO, lambda rq: rq, lambda rq: NA_QROWS + NA_HALO - NA_ROWS)
    table = []
    for first in first_key_row:
        for rq in range(NA_QROWS):
            for kp in range((NA_QROWS + 2 * NA_HALO) // 2):
                drs = []
                for kk in (2 * kp, 2 * kp + 1):
                    visible = first(rq) <= kk < first(rq) + NA_ROWS
                    drs.append(kk - rq + NA_ROWS - 1 - NA_HALO if visible else None)
                kind = (drs[0], drs[1])
                if kind not in kinds:
                    kinds.append(kind)
                table.append(kinds.index(kind))
    return kinds, table


def _na_bias_tiles(rpb):
    nh, ndr, ndc = rpb.shape
    nc = 32
    kinds, table = _na_bias_plan()
    nk = len(kinds)
    rows = jnp.pad(jnp.transpose(rpb, (1, 0, 2)), ((0, 0), (0, 0), (0, nc - ndc))).reshape(ndr * nh, nc)
    npos = GRID_W * 2 * GRID_W
    tiles = pl.pallas_call(
        functools.partial(_na_bias_kernel, kinds=tuple(kinds), n_heads=nh),
        out_shape=jax.ShapeDtypeStruct((nk, nh, npos), F32),
        compiler_params=pltpu.CompilerParams(vmem_limit_bytes=VMEM_LIMIT_BYTES),
        name="na_bias",
    )(rows)
    return tiles.reshape(nk, nh, GRID_W, 2 * GRID_W), jnp.asarray(table, jnp.int32)


def neighbourhood_attention(qkv, k_ctx, v_ctx, rpb, n_sample, sample_len, side_casts=()):
    d = D_MODEL
    m = NA_QROWS * GRID_W
    nblk = sample_len // m
    assert nblk >= 3
    bias, table = _na_bias_tiles(rpb)
    steps = n_sample * nblk

    def blk(col, off):
        return pl.BlockSpec((m, d), lambda b, j, tbl: (b * nblk + jnp.clip(j + off, 0, nblk - 1), col))

    flat, slab_specs, slab_shapes = _side_cast_specs(side_casts, steps, lambda b, j, tbl: b * nblk + j)
    outs = pl.pallas_call(
        _na_kernel,
        grid_spec=pltpu.PrefetchScalarGridSpec(
            num_scalar_prefetch=1,
            grid=(n_sample, nblk),
            in_specs=[
                blk(0, 0), blk(1, -1), blk(1, 0), blk(1, 1), blk(2, -1), blk(2, 0), blk(2, 1),
                pl.BlockSpec((1,) + k_ctx.shape[1:], lambda b, j, tbl: (b, 0, 0), pipeline_mode=pl.Buffered(1)),
                pl.BlockSpec((1,) + v_ctx.shape[1:], lambda b, j, tbl: (b, 0, 0), pipeline_mode=pl.Buffered(1)),
                pl.BlockSpec(bias.shape, lambda b, j, tbl: (0, 0, 0, 0), pipeline_mode=pl.Buffered(1)),
            ] + slab_specs,
            out_specs=[pl.BlockSpec((m, d), lambda b, j, tbl: (b * nblk + j, 0))] + slab_specs,
        ),
        out_shape=[jax.ShapeDtypeStruct((n_sample * sample_len, d), BF16)] + slab_shapes,
        compiler_params=_cparams("arbitrary", "arbitrary"),
        name="neighbourhood_attention",
    )(table, qkv, qkv, qkv, qkv, qkv, qkv, qkv, k_ctx, v_ctx, bias, *flat)
    return [outs[0]] + [o.reshape(a.shape) for o, a in zip(outs[1:], side_casts)]


MOE_ROW_TILE = 512
SC_CORES = 2
SC_SUBCORES = 16
SC_GATHER_CHUNK = 128

def _route_tokens(u, wt, carry):
    w1, w2, w3 = _split3(wt)
    lg = _dot_nt(w1, u) + _dot_nt(w2, u) + _dot_nt(w3, u)
    ne, nt = lg.shape
    e = lax.broadcasted_iota(jnp.int32, lg.shape, 0)
    m1 = jnp.max(lg, axis=0, keepdims=True)
    i1 = jnp.min(jnp.where(lg == m1, e, ne), axis=0, keepdims=True)
    sel1 = e == i1
    lg2 = jnp.where(sel1, -jnp.inf, lg)
    m2 = jnp.max(lg2, axis=0, keepdims=True)
    i2 = jnp.min(jnp.where(lg2 == m2, e, ne), axis=0, keepdims=True)
    sel2 = e == i2
    ex = jnp.exp(m2 - m1)
    wa = 1.0 / (1.0 + ex)
    wb = ex / (1.0 + ex)
    comb = jnp.where(sel1, wa, jnp.where(sel2, wb, 0.0))
    assign = jnp.logical_or(sel1, sel2)
    a = jnp.where(assign, 1.0, 0.0)
    grp = ROUTE_GROUP
    row = lax.broadcasted_iota(jnp.int32, (grp, grp), 0)
    col = lax.broadcasted_iota(jnp.int32, (grp, grp), 1)
    tri = _onehot(row < col)
    ranks = []
    for g0 in range(0, nt, grp):
        ag = a[:, g0:g0 + grp]
        ranks.append(_dot(ag.astype(BF16), tri) + carry)
        carry = carry + jnp.sum(ag, axis=1, keepdims=True)
    pos = jnp.where(assign, jnp.concatenate(ranks, axis=1), -1.0)
    return comb, pos, carry


def _moe_plan(totals, pos, comb, n_tok):
    ne = totals.shape[0]
    tm = MOE_ROW_TILE
    n_tiles = (2 * n_tok) // tm + ne
    tiles_per = (totals + tm - 1) // tm
    tile_end = jnp.cumsum(tiles_per)
    n_used = tile_end[-1]
    base_rows = (tile_end - tiles_per) * tm
    ti = jnp.arange(n_tiles, dtype=jnp.int32)
    tile_expert = jnp.minimum(jnp.sum((ti[:, None] >= tile_end[None, :]).astype(jnp.int32), axis=1), ne - 1)
    last_expert = tile_expert[jnp.maximum(n_used - 1, 0)]
    tile_expert = jnp.where(ti < n_used, tile_expert, last_expert)
    routed = pos >= 0.0
    row = base_rows[:, None] + pos.astype(jnp.int32)
    row_a = jnp.min(jnp.where(routed, row, n_tiles * tm), axis=0)
    row_b = jnp.max(jnp.where(routed, row, -1), axis=0)
    w_a = jnp.sum(jnp.where(jnp.logical_and(routed, row == row_a[None, :]), comb, 0.0), axis=0)
    w_b = jnp.sum(jnp.where(jnp.logical_and(routed, row == row_b[None, :]), comb, 0.0), axis=0)
    tile_rows = jnp.clip(totals[tile_expert] - (ti - (tile_end - tiles_per)[tile_expert]) * tm, 0, tm)
    tile_rows = jnp.where(ti < n_used, tile_rows, 0)
    return dict(n_rows=n_tiles * tm, tile_expert=tile_expert.astype(jnp.int32),
                n_used=n_used.reshape(1).astype(jnp.int32), tile_rows=tile_rows.astype(jnp.int32),
                token_rows=jnp.concatenate([row_a, row_b]).astype(jnp.int32),
                token_weights=jnp.concatenate([w_a[None], w_b[None], jnp.zeros((6, n_tok), F32)], axis=0))


def _pack_bf16_pairs(x):
    half = x.shape[1] // 2
    bits = pltpu.bitcast(x.astype(BF16).astype(F32), jnp.int32)
    return (bits[:, :half] & jnp.int32(-65536)) | lax.shift_right_logical(bits[:, half:], jnp.int32(16))


def _unpack_bf16_pairs(p):
    hi = pltpu.bitcast(p & jnp.int32(-65536), F32)
    lo = pltpu.bitcast(lax.shift_left(p, jnp.int32(16)), F32)
    return jnp.concatenate([hi, lo], axis=1)


def sc_gather_rows(table, idx):
    _, w = table.shape
    b = idx.shape[0]
    workers = SC_CORES * SC_SUBCORES
    ch = SC_GATHER_CHUNK
    n_chunks = b // (workers * ch)
    assert n_chunks * workers * ch == b
    mesh = plsc.VectorSubcoreMesh(core_axis_name="c", subcore_axis_name="s", num_cores=SC_CORES,
                                  num_subcores=SC_SUBCORES)

    @functools.partial(
        pl.kernel, mesh=mesh, out_type=jax.ShapeDtypeStruct((b, w), table.dtype),
        scratch_types=[pltpu.VMEM((ch,), jnp.int32), pltpu.VMEM((ch, w), table.dtype), pltpu.SemaphoreType.DMA])
    def gather(table_hbm, idx_hbm, out_hbm, idx_v, rows_v, sem):
        worker = lax.axis_index("s") * SC_CORES + lax.axis_index("c")

        @pl.loop(0, n_chunks)
        def _(c):
            base = (worker * n_chunks + c) * ch
            pltpu.sync_copy(idx_hbm.at[pl.ds(base, ch)], idx_v)
            pltpu.async_copy(table_hbm.at[idx_v], rows_v, sem).wait()
            pltpu.sync_copy(rows_v, out_hbm.at[pl.ds(base, ch)])

    return gather(table, idx)


def sc_scatter_rows(rows, idx, n_out):
    t, w = rows.shape
    copies = idx.shape[0] // t
    workers = SC_CORES * SC_SUBCORES
    ch = SC_GATHER_CHUNK
    n_chunks = t // (workers * ch)
    assert n_chunks * workers * ch == t and copies * t == idx.shape[0]
    mesh = plsc.VectorSubcoreMesh(core_axis_name="c", subcore_axis_name="s", num_cores=SC_CORES,
                                  num_subcores=SC_SUBCORES)

    @functools.partial(
        pl.kernel, mesh=mesh, out_type=jax.ShapeDtypeStruct((n_out, w), rows.dtype),
        scratch_types=[pltpu.VMEM((ch,), jnp.int32), pltpu.VMEM((ch, w), rows.dtype), pltpu.SemaphoreType.DMA])
    def scatter(rows_hbm, idx_hbm, out_hbm, idx_v, rows_v, sem):
        worker = lax.axis_index("s") * SC_CORES + lax.axis_index("c")

        @pl.loop(0, n_chunks)
        def _(c):
            base = (worker * n_chunks + c) * ch
            pltpu.sync_copy(rows_hbm.at[pl.ds(base, ch)], rows_v)
            for k in range(copies):
                pltpu.sync_copy(idx_hbm.at[pl.ds(k * t + base, ch)], idx_v)
                pltpu.async_copy(rows_v, out_hbm.at[idx_v], sem).wait()

    return scatter(rows, idx)


def _gffn_kernel(te_ref, nu_ref, nr_ref, x_ref, wi_ref, wo_ref, y_ref):
    i = pl.program_id(0)
    used = i < nu_ref[0]

    @pl.when(used)
    def _():
        packed = x_ref[...]
        row = lax.broadcasted_iota(jnp.int32, packed.shape, 0)
        x = _unpack_bf16_pairs(jnp.where(row < nr_ref[i], packed, 0)).astype(BF16)
        y = _swiglu_tile(x, lambda c: wi_ref[0, :, c], lambda r: wo_ref[0, r, :], wo_ref.shape[1])
        y_ref[...] = _pack_bf16_pairs(y)

    @pl.when(jnp.logical_not(used))
    def _():
        y_ref[...] = jnp.zeros_like(y_ref)


def moe_grouped_ffn(xs, w_in, w_out, plan):
    nr, half = xs.shape
    ne, dff, d = w_out.shape
    tm = MOE_ROW_TILE
    return pl.pallas_call(
        _gffn_kernel,
        grid_spec=pltpu.PrefetchScalarGridSpec(
            num_scalar_prefetch=3,
            grid=(nr // tm,),
            in_specs=[
                pl.BlockSpec((tm, half), lambda i, te, nu, tr: (jnp.minimum(i, nu[0] - 1), 0)),
                pl.BlockSpec((1, d, 2 * dff), lambda i, te, nu, tr: (te[i], 0, 0)),
                pl.BlockSpec((1, dff, d), lambda i, te, nu, tr: (te[i], 0, 0)),
            ],
            out_specs=pl.BlockSpec((tm, half), lambda i, te, nu, tr: (i, 0)),
        ),
        out_shape=jax.ShapeDtypeStruct((nr, half), jnp.int32),
        compiler_params=_cparams("arbitrary"),
        name="moe_grouped_ffn",
    )(plan["tile_expert"], plan["n_used"], plan["tile_rows"], xs, w_in, w_out)


def _combine_kernel(ya_ref, yb_ref, w_ref, x_ref, mod_ref, lng_ref, lnb_ref, op_ref, os_ref, *, n_prompt_tiles):
    i = pl.program_id(0)
    w = w_ref[...].T
    moe = w[:, 0:1] * _unpack_bf16_pairs(ya_ref[...]) + w[:, 1:2] * _unpack_bf16_pairs(yb_ref[...])
    z = DEEPNORM_ALPHA * x_ref[...] + mod_ref[0, 5:6, :] * moe
    xn = _layer_norm(z, lng_ref[...], lnb_ref[...])

    @pl.when(i < n_prompt_tiles)
    def _():
        op_ref[...] = xn

    @pl.when(i >= n_prompt_tiles)
    def _():
        os_ref[...] = xn


def moe_combine(y_tok, weights, x, mod, ln_g, ln_b, n_prompt_tok, sample_len):
    t, d = x.shape
    tt = STREAM_TOKEN_TILE
    nt = t // tt
    npt = n_prompt_tok // tt
    seg = functools.partial(_seg_of_tile, tile=tt, n_prompt_tok=n_prompt_tok, sample_len=sample_len)
    return pl.pallas_call(
        functools.partial(_combine_kernel, n_prompt_tiles=npt),
        grid=(nt,),
        in_specs=[
            pl.BlockSpec((tt, d // 2), lambda i: (i, 0)),
            pl.BlockSpec((tt, d // 2), lambda i: (nt + i, 0)),
            pl.BlockSpec((8, tt), lambda i: (0, i)),
            pl.BlockSpec((tt, d), lambda i: (i, 0)),
            pl.BlockSpec((1, 6, d), lambda i: (seg(i), 0, 0)),
            pl.BlockSpec((1, d), lambda i: (0, 0)),
            pl.BlockSpec((1, d), lambda i: (0, 0)),
        ],
        out_specs=[
            pl.BlockSpec((tt, d), lambda i: (jnp.minimum(i, npt - 1), 0)),
            pl.BlockSpec((tt, d), lambda i: (jnp.maximum(i - npt, 0), 0)),
        ],
        out_shape=[jax.ShapeDtypeStruct((n_prompt_tok, d), F32), jax.ShapeDtypeStruct((t - n_prompt_tok, d), F32)],
        compiler_params=_cparams("arbitrary"),
        name="moe_combine",
    )(y_tok, y_tok, weights, x, mod, ln_g.reshape(1, d), ln_b.reshape(1, d))


def _layer1(x, u, mod1, cache_k, cache_v, ln_g, ln_b, w_qkv, rpb, w_out, w_router, moe_w_in, moe_w_out,
            n_prompt, prompt_len, n_sample, sample_len):
    n_prompt_tok = n_prompt * prompt_len
    d = D_MODEL
    qkv_p, k_p, v_p = na_qkv(u, w_qkv, 0, n_prompt_tok, with_kv=True)
    qkv_s, = na_qkv(u, w_qkv, n_prompt_tok, u.shape[0] - n_prompt_tok, with_kv=False)
    attn_p = context_attention(qkv_p, n_prompt, prompt_len)
    k_ctx = cache_k.reshape(n_sample, -1, d).astype(BF16)
    v_ctx = cache_v.reshape(n_sample, -1, d).astype(BF16)
    attn_s, moe_w_in, moe_w_out = neighbourhood_attention(qkv_s, k_ctx, v_ctx, rpb, n_sample, sample_len,
                                                          side_casts=(moe_w_in, moe_w_out))
    x1, u1_packed, comb, pos, totals = mixer_outproj_router(
        attn_p, attn_s, x, mod1, w_out, ln_g[0], ln_b[0], w_router, sample_len)
    plan = _moe_plan(totals[:, 0].astype(jnp.int32), pos, comb, x.shape[0])
    xs = sc_scatter_rows(u1_packed, plan["token_rows"], plan["n_rows"])
    y = moe_grouped_ffn(xs, moe_w_in, moe_w_out, plan)
    y_tok = sc_gather_rows(y, plan["token_rows"])
    y_p, y_s = moe_combine(y_tok, plan["token_weights"], x1, mod1, ln_g[1], ln_b[1], n_prompt_tok, sample_len)
    return y_p, y_s, k_p, v_p


def kernel(x_prompt, x_sample, state_mlstm_C, state_mlstm_n, state_mlstm_m, cache_na_k, cache_na_v, c, c_ctx,
           ada_w, ada_b, ln_g, ln_b, mlstm_w_in, mlstm_b_gates, mlstm_norm_g, mlstm_w_out, na_w_qkv, na_rpb,
           na_w_out, ffn_w_in, ffn_w_out, moe_w_router, moe_w_in, moe_w_out):
    n_prompt, prompt_len, d = x_prompt.shape
    n_sample, sample_len, _ = x_sample.shape
    assert d == D_MODEL and prompt_len == MLSTM_CHUNK and sample_len % MLSTM_SAMPLE_CHUNK == 0
    assert ada_w.shape[0] == DEPTH == 2
    mod = _modulation_tables(c, c_ctx, ada_w, ada_b)
    x2, u2, new_c, new_n, new_m, (w_qkv, na_w_out) = _layer0(
        x_prompt.reshape(-1, d), x_sample.reshape(-1, d), mod[0], mod[1],
        state_mlstm_C[:, 0], state_mlstm_n[:, 0], state_mlstm_m[:, 0], ln_g[0], ln_b[0], mlstm_w_in[0],
        mlstm_b_gates[0], mlstm_norm_g[0], mlstm_w_out[0], ffn_w_in[0], ffn_w_out[0],
        n_prompt, prompt_len, n_sample, sample_len, later_weights=(na_w_qkv[0], na_w_out[0]))
    y_p, y_s, k_p, v_p = _layer1(
        x2, u2, mod[1], cache_na_k[:, 0], cache_na_v[:, 0], ln_g[1], ln_b[1], w_qkv, na_rpb[0], na_w_out,
        moe_w_router[0], moe_w_in[0], moe_w_out[0], n_prompt, prompt_len, n_sample, sample_len)
    kv_shape = (n_prompt, 1, prompt_len, NA_HEADS, NA_DH)
    return (y_p.reshape(x_prompt.shape), y_s.reshape(x_sample.shape), new_c, new_n, new_m,
            k_p.reshape(kv_shape), v_p.reshape(kv_shape))
```

```python
import functools
import math

import jax
import jax.numpy as jnp
from jax import lax
from jax.experimental import pallas as pl
from jax.experimental.pallas import tpu as pltpu
from jax.experimental.pallas import tpu_sc as plsc

F32 = jnp.float32
BF16 = jnp.bfloat16

D_MODEL = 1024
DEPTH = 2
GRID_W = 64
M_HEADS = 4
M_DK = 128
M_DV = 256
M_HK = M_HEADS * M_DK
NA_HEADS = 16
NA_DH = 64
NA_ROWS = 8
NA_COLS = 16
DEEPNORM_ALPHA = (2.0 * DEPTH) ** 0.25
LN_EPS = 1e-5
NEG = -1e30

VMEM_LIMIT_BYTES = 56 * 1024 * 1024

MLSTM_CHUNK = 256
MLSTM_SAMPLE_CHUNK = 256
TOKEN_TILE = 512
STREAM_TOKEN_TILE = 1024
ROUTE_GROUP = 128
FFN_TOKEN_TILE = 512
FFN_SUB = 256
NA_QROWS = 4
NA_HALO = 4


def _cparams(*sem):
    return pltpu.CompilerParams(dimension_semantics=sem, vmem_limit_bytes=VMEM_LIMIT_BYTES)


def _dot(a, b):
    return jnp.dot(a, b, preferred_element_type=F32)


def _dot_nt(a, b):
    return lax.dot_general(a, b, (((1,), (1,)), ((), ())), preferred_element_type=F32)


def _onehot(mask):
    return jnp.where(mask, 1.0, 0.0).astype(BF16)


def _split3(x):
    hi = x.astype(BF16)
    r = x - hi.astype(F32)
    mid = r.astype(BF16)
    lo = (r - mid.astype(F32)).astype(BF16)
    return hi, mid, lo


def _layer_norm(z, g, b):
    mu = jnp.mean(z, axis=-1, keepdims=True)
    zc = z - mu
    var = jnp.mean(zc * zc, axis=-1, keepdims=True)
    return zc * lax.rsqrt(var + LN_EPS) * g + b


def _log_sigmoid(x):
    return jnp.minimum(x, 0.0) - jnp.log(1.0 + jnp.exp(-jnp.abs(x)))


def _seg_of_tile(i, tile, n_prompt_tok, sample_len):
    tok = i * tile
    return jnp.where(tok < n_prompt_tok, 0, 1 + (tok - n_prompt_tok) // sample_len)


def _adaln_kernel(c_ref, w_ref, b_ref, o_ref):
    c = c_ref[...]
    a = (c * jax.nn.sigmoid(c)).astype(BF16)
    o_ref[0] = _dot(a, w_ref[0].astype(BF16)) + b_ref[0]


def adaln(cvec, ada_w, ada_b):
    depth, d, n = ada_w.shape
    tn = 1536
    return pl.pallas_call(
        _adaln_kernel,
        grid=(depth, n // tn),
        in_specs=[
            pl.BlockSpec((8, d), lambda l, j: (0, 0)),
            pl.BlockSpec((1, d, tn), lambda l, j: (l, 0, j)),
            pl.BlockSpec((1, 1, tn), lambda l, j: (l, 0, j)),
        ],
        out_specs=pl.BlockSpec((1, 8, tn), lambda l, j: (l, 0, j)),
        out_shape=jax.ShapeDtypeStruct((depth, 8, n), F32),
        compiler_params=_cparams("arbitrary", "arbitrary"),
        name="adaln",
    )(cvec, ada_w, ada_b.reshape(depth, 1, n))


def _inproj_kernel(xp_ref, xs_ref, mod_ref, w_ref, wg_ref, bg_ref, qkv_ref, o_ref, g_ref, gt_ref, *, n_prompt_tiles):
    x = jnp.where(pl.program_id(0) < n_prompt_tiles, xp_ref[...], xs_ref[...])
    u = (x * (1.0 + mod_ref[0, 1:2, :]) + mod_ref[0, 0:1, :]).astype(BF16)
    nqkv = qkv_ref.shape[1]
    half = nqkv // 2
    g = _dot_nt(u, wg_ref[...]) + bg_ref[...]
    p0 = _dot_nt(u, w_ref[:half, :])
    col = lax.broadcasted_iota(jnp.int32, g.shape, 1)
    g = jnp.where(((col >> 2) & 1) == 1, _log_sigmoid(g), g)
    g_ref[...] = g
    p1 = _dot_nt(u, w_ref[half:nqkv, :])
    qkv_ref[:, :half] = p0.astype(BF16)
    gt_ref[...] = g.T
    p2 = _dot_nt(u, w_ref[nqkv:, :])
    qkv_ref[:, half:] = p1.astype(BF16)
    o_ref[...] = p2


def _split_token_specs(tile, d, n_prompt_tiles):
    return [pl.BlockSpec((tile, d), lambda i: (jnp.minimum(i, n_prompt_tiles - 1), 0)),
            pl.BlockSpec((tile, d), lambda i: (jnp.maximum(i - n_prompt_tiles, 0), 0))]


def mlstm_inproj(xp, xs, mod, w_main, w_gate, b_gate, sample_len):
    n_prompt_tok, d = xp.shape
    t = n_prompt_tok + xs.shape[0]
    n_main = w_main.shape[0]
    n_qkv = 2 * M_HK + D_MODEL
    ng = w_gate.shape[0]
    tt = TOKEN_TILE
    npt = n_prompt_tok // tt
    seg = functools.partial(_seg_of_tile, tile=tt, n_prompt_tok=n_prompt_tok, sample_len=sample_len)
    return pl.pallas_call(
        functools.partial(_inproj_kernel, n_prompt_tiles=npt),
        grid=(t // tt,),
        in_specs=_split_token_specs(tt, d, npt) + [
            pl.BlockSpec((1, 6, d), lambda i: (seg(i), 0, 0)),
            pl.BlockSpec((n_main, d), lambda i: (0, 0)),
            pl.BlockSpec((ng, d), lambda i: (0, 0)),
            pl.BlockSpec((1, ng), lambda i: (0, 0)),
        ],
        out_specs=[
            pl.BlockSpec((tt, n_qkv), lambda i: (i, 0)),
            pl.BlockSpec((tt, n_main - n_qkv), lambda i: (i, 0)),
            pl.BlockSpec((tt, ng), lambda i: (i, 0)),
            pl.BlockSpec((ng, tt), lambda i: (0, i)),
        ],
        out_shape=[
            jax.ShapeDtypeStruct((t, n_qkv), BF16),
            jax.ShapeDtypeStruct((t, n_main - n_qkv), F32),
            jax.ShapeDtypeStruct((t, ng), F32),
            jax.ShapeDtypeStruct((ng, t), F32),
        ],
        compiler_params=_cparams("arbitrary"),
        name="mlstm_inproj",
    )(xp, xs, mod, w_main, w_gate, b_gate.reshape(1, ng))


def _mlstm_gate_sums(g, gt, backward):
    L = g.shape[0]
    row = lax.broadcasted_iota(jnp.int32, (L, L), 0)
    col = lax.broadcasted_iota(jnp.int32, (L, L), 1)
    lower = col <= row
    upper = col >= row
    tri_col = _onehot(upper if backward else lower)
    tri_row = _onehot(lower if backward else upper)
    ghi, gmid, glo = _split3(g)
    b_cols = _dot(tri_col, ghi) + _dot(tri_col, gmid) + _dot(tri_col, glo)
    thi, tmid, tlo = _split3(gt)
    b_rows = _dot(thi, tri_row) + _dot(tmid, tri_row) + _dot(tlo, tri_row)
    return b_cols, b_rows


def _mlstm_heads(items):
    scale = M_DK ** -0.5
    log_scale = math.log(scale)
    L = items[0][0].shape[0]
    row = lax.broadcasted_iota(jnp.int32, (L, L), 0)
    col = lax.broadcasted_iota(jnp.int32, (L, L), 1)
    masks = {False: col <= row, True: col >= row}
    n = range(len(items))
    qh, kh, vh, li_row, lf_row, b_row, b_col, backward, state = zip(*items)
    has_state = [st is not None for st in state]
    m_prev = [st[2] if st is not None else 0.0 for st in state]

    qk = [_dot_nt(qh[i], kh[i]) for i in n]
    qc = [_dot(qh[i], state[i][0].astype(BF16)) if has_state[i] else None for i in n]
    qn = [_dot_nt(qh[i], state[i][1].astype(BF16))[:, 0:1] if has_state[i] else None for i in n]
    total = [jnp.sum(lf_row[i], axis=1, keepdims=True) for i in n]
    d = [jnp.where(masks[backward[i]], b_col[i] + (li_row[i] - b_row[i] + log_scale), NEG) for i in n]
    inter = [b_col[i] + m_prev[i] if has_state[i] else b_col[i] for i in n]
    m_t = [jnp.maximum(inter[i], jnp.max(d[i], axis=1, keepdims=True) - log_scale) for i in n]
    p = [jnp.exp(d[i] - m_t[i]) for i in n]
    s = [qk[i] * p[i] for i in n]
    den = [jnp.sum(s[i], axis=1, keepdims=True) for i in n]
    num = [_dot(s[i].astype(BF16), vh[i]) for i in n]
    a = [jnp.exp(inter[i] - m_t[i]) * scale if has_state[i] else None for i in n]
    num = [a[i] * qc[i] + num[i] if has_state[i] else num[i] for i in n]
    den = [a[i] * qn[i] + den[i] if has_state[i] else den[i] for i in n]
    hh = [num[i] / jnp.maximum(jnp.abs(den[i]), jnp.exp(-m_t[i])) for i in n]

    g_row = [total[i] - b_row[i] + li_row[i] for i in n]
    m_new = [jnp.maximum(total[i] + m_prev[i], jnp.max(g_row[i], axis=1, keepdims=True)) for i in n]
    ws_row = [jnp.exp(g_row[i] - m_new[i]) for i in n]
    kt = [kh[i].astype(F32).T for i in n]
    c_new = [_dot((kt[i] * ws_row[i]).astype(BF16), vh[i]) for i in n]
    n_new = [_dot(jnp.broadcast_to(ws_row[i], (8, L)).astype(BF16), kh[i]) for i in n]
    a0 = [jnp.exp(total[i] + m_prev[i] - m_new[i]) if has_state[i] else None for i in n]
    c_new = [a0[i] * state[i][0] + c_new[i] if has_state[i] else c_new[i] for i in n]
    n_new = [a0[i] * state[i][1] + n_new[i] if has_state[i] else n_new[i] for i in n]
    return hh, c_new, n_new, m_new


def _head_norm_gate(ht, norm_g, ogate):
    mu = jnp.mean(ht, axis=-1, keepdims=True)
    hc = ht - mu
    var = jnp.mean(hc * hc, axis=-1, keepdims=True)
    return (hc * lax.rsqrt(var + LN_EPS) * norm_g * jax.nn.sigmoid(ogate)).astype(BF16)


def _head_operands(q_ref, k_ref, v_ref, h):
    return (q_ref[:, h * M_DK:(h + 1) * M_DK], k_ref[:, h * M_DK:(h + 1) * M_DK], v_ref[:, h * M_DV:(h + 1) * M_DV])


def _gate_operands(gt, b_rows, b_cols, direction, h):
    ci = direction * 8 + h
    cf = direction * 8 + 4 + h
    return gt[ci:ci + 1, :], gt[cf:cf + 1, :], b_rows[cf:cf + 1, :], b_cols[:, cf:cf + 1]


def _mlstm_prompt_kernel(q_ref, k_ref, v_ref, g_ref, gt_ref, o_ref, ng_ref, a_ref, cf_ref, nf_ref, mf_ref):
    g, gt = g_ref[...], gt_ref[...]
    sums = [_mlstm_gate_sums(g, gt, direction == 1) for direction in (0, 1)]
    keys = [(direction, h) for direction in (0, 1) for h in range(M_HEADS)]
    items = [_head_operands(q_ref, k_ref, v_ref, h)
             + _gate_operands(gt, sums[direction][1], sums[direction][0], direction, h) + (direction == 1, None)
             for direction, h in keys]
    hh, c_new, n_new, m_new = _mlstm_heads(items)
    for i, (direction, h) in enumerate(keys):
        cf_ref[0, 0, direction, h] = c_new[i]
        nf_ref[0, 0, direction, h] = n_new[i]
        mf_ref[0, 0, direction, h] = jnp.broadcast_to(m_new[i], mf_ref.shape[-2:])
    for h in range(M_HEADS):
        sl = slice(h * M_DV, (h + 1) * M_DV)
        a_ref[:, sl] = _head_norm_gate(hh[h] + hh[M_HEADS + h], ng_ref[:, sl], o_ref[:, sl])


def _side_cast_specs(arrays, steps, step_of):
    flat = [a.reshape(-1, a.shape[-1]) for a in arrays]
    assert all(a.shape[0] % (steps * 16) == 0 for a in flat)
    specs = [pl.BlockSpec((a.shape[0] // steps, a.shape[1]), lambda *idx: (step_of(*idx), 0)) for a in flat]
    return flat, specs, [jax.ShapeDtypeStruct(a.shape, BF16) for a in flat]


def _run_side_casts(srcs, dsts):
    for src, dst in zip(srcs, dsts):
        dst[...] = src[...].astype(dst.dtype)


def _mlstm_sample_kernel(*refs, direction, n_units, n_cast):
    backward = direction == 1
    refs = list(refs)
    unit_refs = [tuple(refs.pop(0) for _ in range(5)) for _ in range(n_units)]
    c0_ref, n0_ref, m0_ref = refs.pop(0), refs.pop(0), refs.pop(0)
    if backward:
        hprev_ref = refs.pop(0)
        o_refs = [refs.pop(0) for _ in range(n_units)]
        ng_ref = refs.pop(0)
    cast_in = [refs.pop(0) for _ in range(n_cast)]
    out_ref = refs.pop(0)
    cast_out = [refs.pop(0) for _ in range(n_cast)]
    c_scr, n_scr, m_scr = refs
    _run_side_casts(cast_in, cast_out)

    @pl.when(pl.program_id(0) == 0)
    def _():
        c_scr[...] = c0_ref[...]
        n_scr[...] = n0_ref[...]
        m_scr[...] = m0_ref[...]

    keys, items = [], []
    for u, (q_ref, k_ref, v_ref, g_ref, gt_ref) in enumerate(unit_refs):
        gt = gt_ref[...]
        b_cols, b_rows = _mlstm_gate_sums(g_ref[...], gt, backward)
        for h in range(M_HEADS):
            state = (c_scr[u, h], n_scr[u, h], m_scr[u, h][0:1, 0:1])
            keys.append((u, h))
            items.append(_head_operands(q_ref, k_ref, v_ref, h) + _gate_operands(gt, b_rows, b_cols, direction, h)
                         + (backward, state))
    hh, c_new, n_new, m_new = _mlstm_heads(items)
    for i, (u, h) in enumerate(keys):
        c_scr[u, h] = c_new[i]
        n_scr[u, h] = n_new[i]
        m_scr[u, h] = jnp.broadcast_to(m_new[i], m_scr.shape[-2:])
        sl = slice(h * M_DV, (h + 1) * M_DV)
        if backward:
            out_ref[u, :, sl] = _head_norm_gate(hprev_ref[u, :, sl] + hh[i], ng_ref[:, sl], o_refs[u][:, sl])
        else:
            out_ref[u, :, sl] = hh[i]


def _mlstm_chunk_specs(block_of, L):
    return [
        pl.BlockSpec((L, M_HK), lambda s: (block_of(s), 0)),
        pl.BlockSpec((L, M_HK), lambda s: (block_of(s), 1)),
        pl.BlockSpec((L, D_MODEL), lambda s: (block_of(s), 1)),
        pl.BlockSpec((L, 16), lambda s: (block_of(s), 0)),
        pl.BlockSpec((16, L), lambda s: (0, block_of(s))),
    ]


def mlstm_prompt(qkv, g, gt, ogate, norm_g, n_prompt):
    L = MLSTM_CHUNK
    state_blk = lambda *tail: pl.BlockSpec((1, 1, 2, M_HEADS) + tail, lambda s: (s, 0, 0, 0, 0, 0))
    return pl.pallas_call(
        _mlstm_prompt_kernel,
        grid=(n_prompt,),
        in_specs=_mlstm_chunk_specs(lambda s: s, L) + [
            pl.BlockSpec((L, D_MODEL), lambda s: (s, 0)),
            pl.BlockSpec((1, D_MODEL), lambda s: (0, 0)),
        ],
        out_specs=[pl.BlockSpec((L, D_MODEL), lambda s: (s, 0)),
                   state_blk(M_DK, M_DV), state_blk(8, M_DK), state_blk(8, 128)],
        out_shape=[
            jax.ShapeDtypeStruct((n_prompt * L, D_MODEL), BF16),
            jax.ShapeDtypeStruct((n_prompt, 1, 2, M_HEADS, M_DK, M_DV), F32),
            jax.ShapeDtypeStruct((n_prompt, 1, 2, M_HEADS, 8, M_DK), F32),
            jax.ShapeDtypeStruct((n_prompt, 1, 2, M_HEADS, 8, 128), F32),
        ],
        compiler_params=_cparams("arbitrary"),
        name="mlstm_prompt",
    )(qkv, qkv, qkv, g, gt, ogate, norm_g.reshape(1, D_MODEL))


def mlstm_sample(direction, qkv, g, gt, c0, n0, m0, first_block, chunks_per_sample,
                 hprev=None, ogate=None, norm_g=None, side_casts=()):
    L = MLSTM_SAMPLE_CHUNK
    backward = direction == 1
    cps = chunks_per_sample
    n_units = c0.shape[0]
    pos = (lambda s: cps - 1 - s) if backward else (lambda s: s)
    unit_block = [functools.partial(lambda s, u: first_block + u * cps + pos(s), u=u) for u in range(n_units)]
    whole = lambda a: pl.BlockSpec(a.shape, lambda s: (0,) * a.ndim)
    in_specs, args = [], []
    for u in range(n_units):
        in_specs += _mlstm_chunk_specs(unit_block[u], L)
        args += [qkv, qkv, qkv, g, gt]
    in_specs += [whole(c0), whole(n0), whole(m0)]
    args += [c0, n0, m0]
    if backward:
        in_specs += [pl.BlockSpec((n_units, L, D_MODEL), lambda s: (0, pos(s), 0))]
        in_specs += [pl.BlockSpec((L, D_MODEL), functools.partial(lambda s, u: (unit_block[u](s), 0), u=u))
                     for u in range(n_units)]
        in_specs += [pl.BlockSpec((1, D_MODEL), lambda s: (0, 0))]
        args += [hprev] + [ogate] * n_units + [norm_g.reshape(1, D_MODEL)]
    flat, cast_specs, cast_shapes = _side_cast_specs(side_casts, cps, lambda s: s)
    outs = pl.pallas_call(
        functools.partial(_mlstm_sample_kernel, direction=direction, n_units=n_units, n_cast=len(flat)),
        grid=(cps,),
        in_specs=in_specs + cast_specs,
        out_specs=[pl.BlockSpec((n_units, L, D_MODEL), lambda s: (0, pos(s), 0))] + cast_specs,
        out_shape=[jax.ShapeDtypeStruct((n_units, cps * L, D_MODEL), BF16 if backward else F32)] + cast_shapes,
        scratch_shapes=[pltpu.VMEM(c0.shape, F32), pltpu.VMEM(n0.shape, F32), pltpu.VMEM(m0.shape, F32)],
        compiler_params=_cparams("arbitrary"),
        name="mlstm_sample_bwd" if backward else "mlstm_sample_fwd",
    )(*args, *flat)
    return [outs[0]] + [o.reshape(a.shape) for o, a in zip(outs[1:], side_casts)]


def _outproj_router_kernel(ap_ref, as_ref, x_ref, mod_ref, w_ref, lng_ref, lnb_ref, wr_ref,
                           xo_ref, up_ref, comb_ref, pos_ref, total_ref, carry_scr, *, n_prompt_tiles):
    i = pl.program_id(0)

    @pl.when(i == 0)
    def _():
        carry_scr[...] = jnp.zeros_like(carry_scr)

    a = jnp.where(i < n_prompt_tiles, ap_ref[...], as_ref[...])
    z = DEEPNORM_ALPHA * x_ref[...] + mod_ref[0, 2:3, :] * _dot(a, w_ref[...])
    xn = _layer_norm(z, lng_ref[...], lnb_ref[...])
    xo_ref[...] = xn
    u = xn * (1.0 + mod_ref[0, 4:5, :]) + mod_ref[0, 3:4, :]
    up_ref[...] = _pack_bf16_pairs(u)
    comb, pos, carry_new = _route_tokens(u.astype(BF16), wr_ref[...], carry_scr[:, 0:1])
    comb_ref[...] = comb
    pos_ref[...] = pos
    carry_scr[...] = jnp.broadcast_to(carry_new, carry_scr.shape)
    total_ref[...] = jnp.broadcast_to(carry_new, carry_scr.shape)


def mixer_outproj_router(a_p, a_s, x, mod, w, ln_g, ln_b, w_router, sample_len):
    t, d = x.shape
    ne = w_router.shape[1]
    tt = STREAM_TOKEN_TILE
    npt = a_p.shape[0] // tt
    seg = functools.partial(_seg_of_tile, tile=tt, n_prompt_tok=a_p.shape[0], sample_len=sample_len)
    return pl.pallas_call(
        functools.partial(_outproj_router_kernel, n_prompt_tiles=npt),
        grid=(t // tt,),
        in_specs=_split_token_specs(tt, d, npt) + [
            pl.BlockSpec((tt, d), lambda i: (i, 0)),
            pl.BlockSpec((1, 6, d), lambda i: (seg(i), 0, 0)),
            pl.BlockSpec((d, d), lambda i: (0, 0)),
            pl.BlockSpec((1, d), lambda i: (0, 0)),
            pl.BlockSpec((1, d), lambda i: (0, 0)),
            pl.BlockSpec((ne, d), lambda i: (0, 0)),
        ],
        out_specs=[
            pl.BlockSpec((tt, d), lambda i: (i, 0)),
            pl.BlockSpec((tt, d // 2), lambda i: (i, 0)),
            pl.BlockSpec((ne, tt), lambda i: (0, i)),
            pl.BlockSpec((ne, tt), lambda i: (0, i)),
            pl.BlockSpec((ne, 128), lambda i: (0, 0)),
        ],
        out_shape=[
            jax.ShapeDtypeStruct((t, d), F32),
            jax.ShapeDtypeStruct((t, d // 2), jnp.int32),
            jax.ShapeDtypeStruct((ne, t), F32),
            jax.ShapeDtypeStruct((ne, t), F32),
            jax.ShapeDtypeStruct((ne, 128), F32),
        ],
        scratch_shapes=[pltpu.VMEM((ne, 128), F32)],
        compiler_params=_cparams("arbitrary"),
        name="mixer_outproj_router",
    )(a_p, a_s, x, mod, w, ln_g.reshape(1, d), ln_b.reshape(1, d), w_router.T)


def _swiglu_tile(x, wi, wo, dff, side_work=None):
    sub = FFN_SUB
    proj = lambda j: (_dot(x, wi(slice(j * sub, (j + 1) * sub))), _dot(x, wi(slice(dff + j * sub, dff + (j + 1) * sub))))
    acts, cur = [], proj(0)
    for j in range(dff // sub):
        nxt = proj(j + 1) if (j + 1) * sub < dff else None
        if j == 0 and side_work is not None:
            side_work()
        gp, up = cur
        acts.append((gp * jax.nn.sigmoid(gp) * up).astype(BF16))
        cur = nxt
    return _dot(jnp.concatenate(acts, axis=1), wo(slice(0, dff)))


def _mixer_ffn_kernel(ap_ref, as_ref, xp_ref, xs_ref, mod_ref, modn_ref, wmix_ref, wi_ref, wo_ref, ln_ref,
                      xo_ref, uo_ref, *, n_prompt_tiles):
    is_prompt = pl.program_id(0) < n_prompt_tiles
    half = xo_ref.shape[0] // 2
    halves = (slice(0, half), slice(half, 2 * half))

    def mixer_norm(rows):
        a = jnp.where(is_prompt, ap_ref[rows, :], as_ref[rows, :])
        x = jnp.where(is_prompt, xp_ref[rows, :], xs_ref[rows, :])
        z = DEEPNORM_ALPHA * x + mod_ref[0, 2:3, :] * _dot(a, wmix_ref[...])
        x1 = _layer_norm(z, ln_ref[0:1, :], ln_ref[1:2, :])
        return x1, (x1 * (1.0 + mod_ref[0, 4:5, :]) + mod_ref[0, 3:4, :]).astype(BF16)

    def ffn_norm(rows, x1, f):
        x2 = _layer_norm(DEEPNORM_ALPHA * x1 + mod_ref[0, 5:6, :] * f, ln_ref[2:3, :], ln_ref[3:4, :])
        xo_ref[rows, :] = x2
        uo_ref[rows, :] = (x2 * (1.0 + modn_ref[0, 1:2, :]) + modn_ref[0, 0:1, :]).astype(BF16)

    swiglu = functools.partial(_swiglu_tile, wi=lambda c: wi_ref[:, c], wo=lambda r: wo_ref[r, :], dff=wo_ref.shape[0])
    later = {}
    x1_a, u1_a = mixer_norm(halves[0])
    f_a = swiglu(u1_a, side_work=lambda: later.update(b=mixer_norm(halves[1])))
    x1_b, u1_b = later["b"]
    f_b = swiglu(u1_b, side_work=lambda: ffn_norm(halves[0], x1_a, f_a))
    ffn_norm(halves[1], x1_b, f_b)


def _resident(shape):
    return pl.BlockSpec(shape, lambda i: (0,) * len(shape), pipeline_mode=pl.Buffered(1))


def mixer_ffn(a_p, a_s, xp, xs, mod, mod_next, w_mix, w_in, w_out, ln_g, ln_b, sample_len):
    n_prompt_tok, d = xp.shape
    t = n_prompt_tok + xs.shape[0]
    tm = FFN_TOKEN_TILE
    npt = n_prompt_tok // tm
    seg = functools.partial(_seg_of_tile, tile=tm, n_prompt_tok=n_prompt_tok, sample_len=sample_len)
    ln = jnp.stack([ln_g[0], ln_b[0], ln_g[1], ln_b[1]])
    return pl.pallas_call(
        functools.partial(_mixer_ffn_kernel, n_prompt_tiles=npt),
        grid=(t // tm,),
        in_specs=_split_token_specs(tm, d, npt) + _split_token_specs(tm, d, npt) + [
            pl.BlockSpec((1, 6, d), lambda i: (seg(i), 0, 0)),
            pl.BlockSpec((1, 6, d), lambda i: (seg(i), 0, 0)),
            _resident(w_mix.shape), _resident(w_in.shape), _resident(w_out.shape), _resident(ln.shape),
        ],
        out_specs=[pl.BlockSpec((tm, d), lambda i: (i, 0)), pl.BlockSpec((tm, d), lambda i: (i, 0))],
        out_shape=[jax.ShapeDtypeStruct((t, d), F32), jax.ShapeDtypeStruct((t, d), BF16)],
        compiler_params=_cparams("arbitrary"),
        name="mixer_ffn",
    )(a_p, a_s, xp, xs, mod, mod_next, w_mix, w_in, w_out, ln)


def _modulation_tables(c, c_ctx, ada_w, ada_b):
    n_s = c.shape[0]
    cvec = jnp.concatenate([c_ctx[None, :], c, jnp.zeros((8 - 1 - n_s, c.shape[1]), F32)], axis=0)
    mod = adaln(cvec, ada_w, ada_b)
    return mod[:, :1 + n_s, :].reshape(ada_w.shape[0], 1 + n_s, 6, c.shape[1])


def _layer0(xp, xs, mod0, mod1, state_c, state_n, state_m, ln_g, ln_b, w_in, b_gates, norm_g, w_out,
            ffn_w_in, ffn_w_out, n_prompt, prompt_len, n_sample, sample_len, later_weights):
    n_prompt_tok = n_prompt * prompt_len
    n_main = 2 * M_HK + 2 * D_MODEL
    w_t = w_in.T
    qkv, ogate, g, gt = mlstm_inproj(xp, xs, mod0, w_t[:n_main].astype(BF16), w_t[n_main:].astype(BF16),
                                     b_gates.reshape(-1), sample_len)
    npc = n_prompt_tok // MLSTM_SAMPLE_CHUNK
    cps = sample_len // MLSTM_SAMPLE_CHUNK
    rep = lambda a: jnp.broadcast_to(a[..., None, :], a.shape[:-1] + (8, a.shape[-1]))
    n0 = rep(state_n)
    m0 = jnp.broadcast_to(state_m[..., None, None], state_m.shape + (8, 128))
    a_p, new_c, nf, mf = mlstm_prompt(qkv, g, gt, ogate, norm_g, n_prompt)
    hf, w_out, ffn_w_in, ffn_w_out = mlstm_sample(0, qkv, g, gt, state_c[:, 0], n0[:, 0], m0[:, 0], npc, cps,
                                                  side_casts=(w_out, ffn_w_in, ffn_w_out))
    a_s, *later_weights = mlstm_sample(1, qkv, g, gt, state_c[:, 1], n0[:, 1], m0[:, 1], npc, cps,
                                       hprev=hf, ogate=ogate, norm_g=norm_g, side_casts=later_weights)
    x2, u2 = mixer_ffn(a_p, a_s.reshape(-1, D_MODEL), xp, xs, mod0, mod1, w_out, ffn_w_in, ffn_w_out, ln_g, ln_b,
                       sample_len)
    return x2, u2, new_c, nf[..., 0, :], mf[..., 0, 0], later_weights


def _qkv_kernel(u_ref, w_ref, qkv_ref, *kv_refs):
    tt, d = u_ref.shape
    u = u_ref[...]

    def head_major(out_ref, p):
        for h in range(NA_HEADS):
            out_ref[pl.ds(h, tt, stride=NA_HEADS), :] = p[:, h * NA_DH:(h + 1) * NA_DH]

    pq = _dot(u, w_ref[:, :d])
    pk = _dot(u, w_ref[:, d:2 * d])
    qkv_ref[:, :d] = (pq * ATTN_Q_SCALE).astype(BF16)
    pv = _dot(u, w_ref[:, 2 * d:])
    qkv_ref[:, d:2 * d] = pk.astype(BF16)
    if kv_refs:
        head_major(kv_refs[0], pk)
    qkv_ref[:, 2 * d:] = pv.astype(BF16)
    if kv_refs:
        head_major(kv_refs[1], pv)


def na_qkv(u, w, first_tok, n_tok, with_kv):
    d = u.shape[1]
    tt = TOKEN_TILE
    first = first_tok // tt
    out_specs = [pl.BlockSpec((tt, 3 * d), lambda i: (i, 0))]
    out_shape = [jax.ShapeDtypeStruct((n_tok, 3 * d), BF16)]
    if with_kv:
        out_specs += [pl.BlockSpec((tt * NA_HEADS, NA_DH), lambda i: (i, 0))] * 2
        out_shape += [jax.ShapeDtypeStruct((n_tok * NA_HEADS, NA_DH), F32)] * 2
    return pl.pallas_call(
        _qkv_kernel,
        grid=(n_tok // tt,),
        in_specs=[pl.BlockSpec((tt, d), lambda i: (first + i, 0)), pl.BlockSpec((d, 3 * d), lambda i: (0, 0))],
        out_specs=out_specs,
        out_shape=out_shape,
        compiler_params=_cparams("arbitrary"),
        name="na_qkv_prompt" if with_kv else "na_qkv_latent",
    )(u, w)


ATTN_LOG2E = math.log2(math.e)
ATTN_Q_SCALE = NA_DH ** -0.5 * ATTN_LOG2E


def _head_pair_masks():
    lane = lax.broadcasted_iota(jnp.int32, (1, 2 * NA_DH), 1)
    return (_onehot(lane < NA_DH), _onehot(lane >= NA_DH))


def _softmax_pv(score_blocks, value_blocks):
    mx = None
    for s in score_blocks:
        bm = jnp.max(s, axis=1, keepdims=True)
        mx = bm if mx is None else jnp.maximum(mx, bm)
    acc, denom = None, None
    for s, v in zip(score_blocks, value_blocks):
        p = jnp.exp2(s - mx)
        ps = jnp.sum(p, axis=1, keepdims=True)
        pv = _dot(p.astype(BF16), v)
        acc = pv if acc is None else acc + pv
        denom = ps if denom is None else denom + ps
    return acc / denom


def _ctx_attn_kernel(q_ref, k_ref, v_ref, o_ref):
    masks = _head_pair_masks()
    for hp in range(NA_HEADS // 2):
        sl = slice(hp * 2 * NA_DH, (hp + 1) * 2 * NA_DH)
        q2, k2, v2 = q_ref[:, sl], k_ref[:, sl], v_ref[:, sl]
        o2 = None
        for msk in masks:
            s = _dot_nt(q2 * msk, k2)
            o = _softmax_pv([s], [v2 * msk])
            o2 = o if o2 is None else o2 + o
        o_ref[:, sl] = o2.astype(o_ref.dtype)


def context_attention(qkv, n_prompt, prompt_len):
    t = n_prompt * prompt_len
    d = D_MODEL
    return pl.pallas_call(
        _ctx_attn_kernel,
        grid=(n_prompt,),
        in_specs=[
            pl.BlockSpec((prompt_len, d), lambda b: (b, 0)),
            pl.BlockSpec((prompt_len, d), lambda b: (b, 1)),
            pl.BlockSpec((prompt_len, d), lambda b: (b, 2)),
        ],
        out_specs=pl.BlockSpec((prompt_len, d), lambda b: (b, 0)),
        out_shape=jax.ShapeDtypeStruct((t, d), BF16),
        compiler_params=_cparams("arbitrary"),
        name="context_attention",
    )(qkv, qkv, qkv)


def _na_kernel(tile_ref, q_ref, kp_ref, kc_ref, kn_ref, vp_ref, vc_ref, vn_ref, kctx_ref, vctx_ref, bias_ref,
               *refs):
    n_cast = (len(refs) - 1) // 2
    cast_in, o_ref, cast_out = refs[:n_cast], refs[n_cast], refs[n_cast + 1:]
    _run_side_casts(cast_in, cast_out)
    j = pl.program_id(1)
    nblk = pl.num_programs(1)
    m = q_ref.shape[0]
    halo = NA_HALO * GRID_W
    n_pairs = (NA_QROWS + 2 * NA_HALO) // 2
    variant = jnp.where(j == 0, 0, jnp.where(j == nblk - 1, 2, 1))
    kinds = [[tile_ref[(variant * NA_QROWS + rq) * n_pairs + kp] for kp in range(n_pairs)]
             for rq in range(NA_QROWS)]

    def bias_of(head):
        return jnp.concatenate(
            [jnp.concatenate([bias_ref[kinds[rq][kp], head] for kp in range(n_pairs)], axis=1)
             for rq in range(NA_QROWS)], axis=0)

    masks = _head_pair_masks()
    for hp in range(NA_HEADS // 2):
        sl = slice(hp * 2 * NA_DH, (hp + 1) * 2 * NA_DH)
        q2 = q_ref[:, sl]
        kloc = jnp.concatenate([kp_ref[m - halo:, sl], kc_ref[:, sl], kn_ref[:halo, sl]], axis=0)
        vloc = jnp.concatenate([vp_ref[m - halo:, sl], vc_ref[:, sl], vn_ref[:halo, sl]], axis=0)
        kctx = kctx_ref[0][:, sl]
        vctx = vctx_ref[0][:, sl]
        o2 = None
        for half, msk in enumerate(masks):
            qm = q2 * msk
            s_loc = _dot_nt(qm, kloc) + bias_of(2 * hp + half)
            s_ctx = _dot_nt(qm, kctx)
            o = _softmax_pv([s_loc, s_ctx], [vloc * msk, vctx * msk])
            o2 = o if o2 is None else o2 + o
        o_ref[:, sl] = o2.astype(o_ref.dtype)


def _na_bias_kernel(rpb_ref, o_ref, *, kinds, n_heads):
    nc, npos = rpb_ref.shape[1], o_ref.shape[2]
    pair_shift = (2 * GRID_W).bit_length() - 1
    c = lax.broadcasted_iota(jnp.int32, (nc, npos), 0)
    pos = lax.broadcasted_iota(jnp.int32, (nc, npos), 1)
    qc = pos >> pair_shift
    kc = pos & (GRID_W - 1)
    c_start = jnp.clip(qc - NA_COLS // 2, 0, GRID_W - NA_COLS)
    col_in = jnp.logical_and(kc >= c_start, kc < c_start + NA_COLS)
    dc = jnp.clip(kc - qc + NA_COLS - 1, 0, 2 * NA_COLS - 2)
    hit = jnp.logical_and(c == dc, col_in)
    right = (pos & GRID_W) != 0
    r1, r2, r3 = _split3(rpb_ref[...] * ATTN_LOG2E)

    def placed(side):
        sel = _onehot(jnp.logical_and(hit, right == side))
        return _dot(r1, sel) + _dot(r2, sel) + _dot(r3, sel)

    left, rght = placed(False), placed(True)
    rows = lambda v, dr: v[dr * n_heads:(dr + 1) * n_heads]
    for k, (dl, dr) in enumerate(kinds):
        shown = col_in[0:1, :]
        if dl is None or dr is None:
            side_ok = jnp.zeros_like(shown) if dl is None and dr is None else (right[0:1, :] == (dl is None))
            shown = jnp.logical_and(shown, side_ok)
        parts = ([rows(left, dl)] if dl is not None else []) + ([rows(rght, dr)] if dr is not None else [])
        val = sum(parts) if parts else jnp.zeros((n_heads, npos), F32)
        o_ref[k] = jnp.where(shown, val, NEG)


def _na_bias_plan():
    ndr = 2 * NA_ROWS - 1
    assert NA_QROWS >= NA_ROWS // 2 and NA_HALO == NA_ROWS // 2 and (NA_QROWS + 2 * NA_HALO) % 2 == 0
    ok = lambda dr: dr if dr is not None and 0 <= dr < ndr else None
    kinds = [(ok(dr), ok(dr + 1)) for dr in range(-1, ndr)]
    first_key_row = (lambda rq: NA_HALO, lambda rq: rq, lambda rq: NA_QROWS + NA_HALO - NA_ROWS)
    table = []
    for first in first_key_row:
        for rq in range(NA_QROWS):
            for kp in range((NA_QROWS + 2 * NA_HALO) // 2):
                drs = []
                for kk in (2 * kp, 2 * kp + 1):
                    visible = first(rq) <= kk < first(rq) + NA_ROWS
                    drs.append(kk - rq + NA_ROWS - 1 - NA_HALO if visible else None)
                kind = (drs[0], drs[1])
                if kind not in kinds:
                    kinds.append(kind)
                table.append(kinds.index(kind))
    return kinds, table


def _na_bias_tiles(rpb):
    nh, ndr, ndc = rpb.shape
    nc = 32
    kinds, table = _na_bias_plan()
    nk = len(kinds)
    rows = jnp.pad(jnp.transpose(rpb, (1, 0, 2)), ((0, 0), (0, 0), (0, nc - ndc))).reshape(ndr * nh, nc)
    npos = GRID_W * 2 * GRID_W
    tiles = pl.pallas_call(
        functools.partial(_na_bias_kernel, kinds=tuple(kinds), n_heads=nh),
        out_shape=jax.ShapeDtypeStruct((nk, nh, npos), F32),
        compiler_params=pltpu.CompilerParams(vmem_limit_bytes=VMEM_LIMIT_BYTES),
        name="na_bias",
    )(rows)
    return tiles.reshape(nk, nh, GRID_W, 2 * GRID_W), jnp.asarray(table, jnp.int32)


def neighbourhood_attention(qkv, k_ctx, v_ctx, rpb, n_sample, sample_len, side_casts=()):
    d = D_MODEL
    m = NA_QROWS * GRID_W
    nblk = sample_len // m
    assert nblk >= 3
    bias, table = _na_bias_tiles(rpb)
    steps = n_sample * nblk

    def blk(col, off):
        return pl.BlockSpec((m, d), lambda b, j, tbl: (b * nblk + jnp.clip(j + off, 0, nblk - 1), col))

    flat, slab_specs, slab_shapes = _side_cast_specs(side_casts, steps, lambda b, j, tbl: b * nblk + j)
    outs = pl.pallas_call(
        _na_kernel,
        grid_spec=pltpu.PrefetchScalarGridSpec(
            num_scalar_prefetch=1,
            grid=(n_sample, nblk),
            in_specs=[
                blk(0, 0), blk(1, -1), blk(1, 0), blk(1, 1), blk(2, -1), blk(2, 0), blk(2, 1),
                pl.BlockSpec((1,) + k_ctx.shape[1:], lambda b, j, tbl: (b, 0, 0), pipeline_mode=pl.Buffered(1)),
                pl.BlockSpec((1,) + v_ctx.shape[1:], lambda b, j, tbl: (b, 0, 0), pipeline_mode=pl.Buffered(1)),
                pl.BlockSpec(bias.shape, lambda b, j, tbl: (0, 0, 0, 0), pipeline_mode=pl.Buffered(1)),
            ] + slab_specs,
            out_specs=[pl.BlockSpec((m, d), lambda b, j, tbl: (b * nblk + j, 0))] + slab_specs,
        ),
        out_shape=[jax.ShapeDtypeStruct((n_sample * sample_len, d), BF16)] + slab_shapes,
        compiler_params=_cparams("arbitrary", "arbitrary"),
        name="neighbourhood_attention",
    )(table, qkv, qkv, qkv, qkv, qkv, qkv, qkv, k_ctx, v_ctx, bias, *flat)
    return [outs[0]] + [o.reshape(a.shape) for o, a in zip(outs[1:], side_casts)]


MOE_ROW_TILE = 512
SC_CORES = 2
SC_SUBCORES = 16
SC_GATHER_CHUNK = 128

def _route_tokens(u, wt, carry):
    w1, w2, w3 = _split3(wt)
    lg = _dot_nt(w1, u) + _dot_nt(w2, u) + _dot_nt(w3, u)
    ne, nt = lg.shape
    e = lax.broadcasted_iota(jnp.int32, lg.shape, 0)
    m1 = jnp.max(lg, axis=0, keepdims=True)
    i1 = jnp.min(jnp.where(lg == m1, e, ne), axis=0, keepdims=True)
    sel1 = e == i1
    lg2 = jnp.where(sel1, -jnp.inf, lg)
    m2 = jnp.max(lg2, axis=0, keepdims=True)
    i2 = jnp.min(jnp.where(lg2 == m2, e, ne), axis=0, keepdims=True)
    sel2 = e == i2
    ex = jnp.exp(m2 - m1)
    wa = 1.0 / (1.0 + ex)
    wb = ex / (1.0 + ex)
    comb = jnp.where(sel1, wa, jnp.where(sel2, wb, 0.0))
    assign = jnp.logical_or(sel1, sel2)
    a = jnp.where(assign, 1.0, 0.0)
    grp = ROUTE_GROUP
    row = lax.broadcasted_iota(jnp.int32, (grp, grp), 0)
    col = lax.broadcasted_iota(jnp.int32, (grp, grp), 1)
    tri = _onehot(row < col)
    ranks = []
    for g0 in range(0, nt, grp):
        ag = a[:, g0:g0 + grp]
        ranks.append(_dot(ag.astype(BF16), tri) + carry)
        carry = carry + jnp.sum(ag, axis=1, keepdims=True)
    pos = jnp.where(assign, jnp.concatenate(ranks, axis=1), -1.0)
    return comb, pos, carry


def _moe_plan(totals, pos, comb, n_tok):
    ne = totals.shape[0]
    tm = MOE_ROW_TILE
    n_tiles = (2 * n_tok) // tm + ne
    tiles_per = (totals + tm - 1) // tm
    tile_end = jnp.cumsum(tiles_per)
    n_used = tile_end[-1]
    base_rows = (tile_end - tiles_per) * tm
    ti = jnp.arange(n_tiles, dtype=jnp.int32)
    tile_expert = jnp.minimum(jnp.sum((ti[:, None] >= tile_end[None, :]).astype(jnp.int32), axis=1), ne - 1)
    last_expert = tile_expert[jnp.maximum(n_used - 1, 0)]
    tile_expert = jnp.where(ti < n_used, tile_expert, last_expert)
    routed = pos >= 0.0
    row = base_rows[:, None] + pos.astype(jnp.int32)
    row_a = jnp.min(jnp.where(routed, row, n_tiles * tm), axis=0)
    row_b = jnp.max(jnp.where(routed, row, -1), axis=0)
    w_a = jnp.sum(jnp.where(jnp.logical_and(routed, row == row_a[None, :]), comb, 0.0), axis=0)
    w_b = jnp.sum(jnp.where(jnp.logical_and(routed, row == row_b[None, :]), comb, 0.0), axis=0)
    tile_rows = jnp.clip(totals[tile_expert] - (ti - (tile_end - tiles_per)[tile_expert]) * tm, 0, tm)
    tile_rows = jnp.where(ti < n_used, tile_rows, 0)
    return dict(n_rows=n_tiles * tm, tile_expert=tile_expert.astype(jnp.int32),
                n_used=n_used.reshape(1).astype(jnp.int32), tile_rows=tile_rows.astype(jnp.int32),
                token_rows=jnp.concatenate([row_a, row_b]).astype(jnp.int32),
                token_weights=jnp.concatenate([w_a[None], w_b[None], jnp.zeros((6, n_tok), F32)], axis=0))


def _pack_bf16_pairs(x):
    half = x.shape[1] // 2
    bits = pltpu.bitcast(x.astype(BF16).astype(F32), jnp.int32)
    return (bits[:, :half] & jnp.int32(-65536)) | lax.shift_right_logical(bits[:, half:], jnp.int32(16))


def _unpack_bf16_pairs(p):
    hi = pltpu.bitcast(p & jnp.int32(-65536), F32)
    lo = pltpu.bitcast(lax.shift_left(p, jnp.int32(16)), F32)
    return jnp.concatenate([hi, lo], axis=1)


def sc_gather_rows(table, idx):
    _, w = table.shape
    b = idx.shape[0]
    workers = SC_CORES * SC_SUBCORES
    ch = SC_GATHER_CHUNK
    n_chunks = b // (workers * ch)
    assert n_chunks * workers * ch == b
    mesh = plsc.VectorSubcoreMesh(core_axis_name="c", subcore_axis_name="s", num_cores=SC_CORES,
                                  num_subcores=SC_SUBCORES)

    @functools.partial(
        pl.kernel, mesh=mesh, out_type=jax.ShapeDtypeStruct((b, w), table.dtype),
        scratch_types=[pltpu.VMEM((ch,), jnp.int32), pltpu.VMEM((ch, w), table.dtype), pltpu.SemaphoreType.DMA])
    def gather(table_hbm, idx_hbm, out_hbm, idx_v, rows_v, sem):
        worker = lax.axis_index("s") * SC_CORES + lax.axis_index("c")

        @pl.loop(0, n_chunks)
        def _(c):
            base = (worker * n_chunks + c) * ch
            pltpu.sync_copy(idx_hbm.at[pl.ds(base, ch)], idx_v)
            pltpu.async_copy(table_hbm.at[idx_v], rows_v, sem).wait()
            pltpu.sync_copy(rows_v, out_hbm.at[pl.ds(base, ch)])

    return gather(table, idx)


def sc_scatter_rows(rows, idx, n_out):
    t, w = rows.shape
    copies = idx.shape[0] // t
    workers = SC_CORES * SC_SUBCORES
    ch = SC_GATHER_CHUNK
    n_chunks = t // (workers * ch)
    assert n_chunks * workers * ch == t and copies * t == idx.shape[0]
    mesh = plsc.VectorSubcoreMesh(core_axis_name="c", subcore_axis_name="s", num_cores=SC_CORES,
                                  num_subcores=SC_SUBCORES)

    @functools.partial(
        pl.kernel, mesh=mesh, out_type=jax.ShapeDtypeStruct((n_out, w), rows.dtype),
        scratch_types=[pltpu.VMEM((ch,), jnp.int32), pltpu.VMEM((ch, w), rows.dtype), pltpu.SemaphoreType.DMA])
    def scatter(rows_hbm, idx_hbm, out_hbm, idx_v, rows_v, sem):
        worker = lax.axis_index("s") * SC_CORES + lax.axis_index("c")

        @pl.loop(0, n_chunks)
        def _(c):
            base = (worker * n_chunks + c) * ch
            pltpu.sync_copy(rows_hbm.at[pl.ds(base, ch)], rows_v)
            for k in range(copies):
                pltpu.sync_copy(idx_hbm.at[pl.ds(k * t + base, ch)], idx_v)
                pltpu.async_copy(rows_v, out_hbm.at[idx_v], sem).wait()

    return scatter(rows, idx)


def _gffn_kernel(te_ref, nu_ref, nr_ref, x_ref, wi_ref, wo_ref, y_ref):
    i = pl.program_id(0)
    used = i < nu_ref[0]

    @pl.when(used)
    def _():
        packed = x_ref[...]
        row = lax.broadcasted_iota(jnp.int32, packed.shape, 0)
        x = _unpack_bf16_pairs(jnp.where(row < nr_ref[i], packed, 0)).astype(BF16)
        y = _swiglu_tile(x, lambda c: wi_ref[0, :, c], lambda r: wo_ref[0, r, :], wo_ref.shape[1])
        y_ref[...] = _pack_bf16_pairs(y)

    @pl.when(jnp.logical_not(used))
    def _():
        y_ref[...] = jnp.zeros_like(y_ref)


def moe_grouped_ffn(xs, w_in, w_out, plan):
    nr, half = xs.shape
    ne, dff, d = w_out.shape
    tm = MOE_ROW_TILE
    return pl.pallas_call(
        _gffn_kernel,
        grid_spec=pltpu.PrefetchScalarGridSpec(
            num_scalar_prefetch=3,
            grid=(nr // tm,),
            in_specs=[
                pl.BlockSpec((tm, half), lambda i, te, nu, tr: (jnp.minimum(i, nu[0] - 1), 0)),
                pl.BlockSpec((1, d, 2 * dff), lambda i, te, nu, tr: (te[i], 0, 0)),
                pl.BlockSpec((1, dff, d), lambda i, te, nu, tr: (te[i], 0, 0)),
            ],
            out_specs=pl.BlockSpec((tm, half), lambda i, te, nu, tr: (i, 0)),
        ),
        out_shape=jax.ShapeDtypeStruct((nr, half), jnp.int32),
        compiler_params=_cparams("arbitrary"),
        name="moe_grouped_ffn",
    )(plan["tile_expert"], plan["n_used"], plan["tile_rows"], xs, w_in, w_out)


def _combine_kernel(ya_ref, yb_ref, w_ref, x_ref, mod_ref, lng_ref, lnb_ref, op_ref, os_ref, *, n_prompt_tiles):
    i = pl.program_id(0)
    w = w_ref[...].T
    moe = w[:, 0:1] * _unpack_bf16_pairs(ya_ref[...]) + w[:, 1:2] * _unpack_bf16_pairs(yb_ref[...])
    z = DEEPNORM_ALPHA * x_ref[...] + mod_ref[0, 5:6, :] * moe
    xn = _layer_norm(z, lng_ref[...], lnb_ref[...])

    @pl.when(i < n_prompt_tiles)
    def _():
        op_ref[...] = xn

    @pl.when(i >= n_prompt_tiles)
    def _():
        os_ref[...] = xn


def moe_combine(y_tok, weights, x, mod, ln_g, ln_b, n_prompt_tok, sample_len):
    t, d = x.shape
    tt = STREAM_TOKEN_TILE
    nt = t // tt
    npt = n_prompt_tok // tt
    seg = functools.partial(_seg_of_tile, tile=tt, n_prompt_tok=n_prompt_tok, sample_len=sample_len)
    return pl.pallas_call(
        functools.partial(_combine_kernel, n_prompt_tiles=npt),
        grid=(nt,),
        in_specs=[
            pl.BlockSpec((tt, d // 2), lambda i: (i, 0)),
            pl.BlockSpec((tt, d // 2), lambda i: (nt + i, 0)),
            pl.BlockSpec((8, tt), lambda i: (0, i)),
            pl.BlockSpec((tt, d), lambda i: (i, 0)),
            pl.BlockSpec((1, 6, d), lambda i: (seg(i), 0, 0)),
            pl.BlockSpec((1, d), lambda i: (0, 0)),
            pl.BlockSpec((1, d), lambda i: (0, 0)),
        ],
        out_specs=[
            pl.BlockSpec((tt, d), lambda i: (jnp.minimum(i, npt - 1), 0)),
            pl.BlockSpec((tt, d), lambda i: (jnp.maximum(i - npt, 0), 0)),
        ],
        out_shape=[jax.ShapeDtypeStruct((n_prompt_tok, d), F32), jax.ShapeDtypeStruct((t - n_prompt_tok, d), F32)],
        compiler_params=_cparams("arbitrary"),
        name="moe_combine",
    )(y_tok, y_tok, weights, x, mod, ln_g.reshape(1, d), ln_b.reshape(1, d))


def _layer1(x, u, mod1, cache_k, cache_v, ln_g, ln_b, w_qkv, rpb, w_out, w_router, moe_w_in, moe_w_out,
            n_prompt, prompt_len, n_sample, sample_len):
    n_prompt_tok = n_prompt * prompt_len
    d = D_MODEL
    qkv_p, k_p, v_p = na_qkv(u, w_qkv, 0, n_prompt_tok, with_kv=True)
    qkv_s, = na_qkv(u, w_qkv, n_prompt_tok, u.shape[0] - n_prompt_tok, with_kv=False)
    attn_p = context_attention(qkv_p, n_prompt, prompt_len)
    k_ctx = cache_k.reshape(n_sample, -1, d).astype(BF16)
    v_ctx = cache_v.reshape(n_sample, -1, d).astype(BF16)
    attn_s, moe_w_in, moe_w_out = neighbourhood_attention(qkv_s, k_ctx, v_ctx, rpb, n_sample, sample_len,
                                                          side_casts=(moe_w_in, moe_w_out))
    x1, u1_packed, comb, pos, totals = mixer_outproj_router(
        attn_p, attn_s, x, mod1, w_out, ln_g[0], ln_b[0], w_router, sample_len)
    plan = _moe_plan(totals[:, 0].astype(jnp.int32), pos, comb, x.shape[0])
    xs = sc_scatter_rows(u1_packed, plan["token_rows"], plan["n_rows"])
    y = moe_grouped_ffn(xs, moe_w_in, moe_w_out, plan)
    y_tok = sc_gather_rows(y, plan["token_rows"])
    y_p, y_s = moe_combine(y_tok, plan["token_weights"], x1, mod1, ln_g[1], ln_b[1], n_prompt_tok, sample_len)
    return y_p, y_s, k_p, v_p


def kernel(x_prompt, x_sample, state_mlstm_C, state_mlstm_n, state_mlstm_m, cache_na_k, cache_na_v, c, c_ctx,
           ada_w, ada_b, ln_g, ln_b, mlstm_w_in, mlstm_b_gates, mlstm_norm_g, mlstm_w_out, na_w_qkv, na_rpb,
           na_w_out, ffn_w_in, ffn_w_out, moe_w_router, moe_w_in, moe_w_out):
    n_prompt, prompt_len, d = x_prompt.shape
    n_sample, sample_len, _ = x_sample.shape
    assert d == D_MODEL and prompt_len == MLSTM_CHUNK and sample_len % MLSTM_SAMPLE_CHUNK == 0
    assert ada_w.shape[0] == DEPTH == 2
    mod = _modulation_tables(c, c_ctx, ada_w, ada_b)
    x2, u2, new_c, new_n, new_m, (w_qkv, na_w_out) = _layer0(
        x_prompt.reshape(-1, d), x_sample.reshape(-1, d), mod[0], mod[1],
        state_mlstm_C[:, 0], state_mlstm_n[:, 0], state_mlstm_m[:, 0], ln_g[0], ln_b[0], mlstm_w_in[0],
        mlstm_b_gates[0], mlstm_norm_g[0], mlstm_w_out[0], ffn_w_in[0], ffn_w_out[0],
        n_prompt, prompt_len, n_sample, sample_len, later_weights=(na_w_qkv[0], na_w_out[0]))
    y_p, y_s, k_p, v_p = _layer1(
        x2, u2, mod[1], cache_na_k[:, 0], cache_na_v[:, 0], ln_g[1], ln_b[1], w_qkv, na_rpb[0], na_w_out,
        moe_w_router[0], moe_w_in[0], moe_w_out[0], n_prompt, prompt_len, n_sample, sample_len)
    kv_shape = (n_prompt, 1, prompt_len, NA_HEADS, NA_DH)
    return (y_p.reshape(x_prompt.shape), y_s.reshape(x_sample.shape), new_c, new_n, new_m,
            k_p.reshape(kv_shape), v_p.reshape(kv_shape))
```
